```python
import jax, jax.numpy as jnp
from jax import lax
import numpy as np

D_MODEL = 1024
BATCH = 2
SEQ = 8192
DEPTH = 2

HEAD_DIM = 64
ROT_DIM = HEAD_DIM // 4
ROPE_THETA = 500000.0
ATTN_SCALE = HEAD_DIM ** -0.5
EPS = 1e-6
NEG_INF = -1e30

MOBA_HEADS = 8
FOX_HEADS = 8
MOBA_BLOCK = 256
MOBA_TOPK = 3
MOBA_Q_CHUNK = 64
FOX_Q_BLOCK = 128
A_W = MOBA_HEADS * HEAD_DIM
B_W = FOX_HEADS * HEAD_DIM
EVEN_COLS = 3 * A_W + 4 * B_W + FOX_HEADS
EVEN_MIX_WIDTH = A_W + B_W

SWA_Q_HEADS = 16
SWA_KV_HEADS = 2
SWA_WINDOW = 128
ODD_COLS = (SWA_Q_HEADS + 2 * SWA_KV_HEADS) * HEAD_DIM
ODD_MIX_WIDTH = SWA_Q_HEADS * HEAD_DIM

PEER_HEADS = 8
PEER_N_KEYS = 128
PEER_N_EXPERTS = PEER_N_KEYS * PEER_N_KEYS
PEER_TOPK = 16
PEER_QUERY_DIM = 128
PEER_TOKEN_CHUNK = 128

N_EVEN = (DEPTH + 1) // 2
N_ODD = DEPTH // 2

kernel_name = "hybrid_moba_fox_swa_peer"


def rms_norm(x, g):
    x32 = x.astype(jnp.float32)
    y = x32 * lax.rsqrt(jnp.mean(x32 * x32, axis=-1, keepdims=True) + EPS)
    return y.astype(x.dtype) * g


def rope_tables(seq):
    inv_freq = jnp.power(ROPE_THETA, -jnp.arange(0, ROT_DIM, 2, dtype=jnp.float32) / ROT_DIM)
    ang = jnp.arange(seq, dtype=jnp.float32)[:, None] * inv_freq[None, :]
    return jnp.cos(ang), jnp.sin(ang)


def partial_rope(x, cos, sin):
    x32 = x.astype(jnp.float32)
    half = ROT_DIM // 2
    x1 = x32[..., :half]
    x2 = x32[..., half:ROT_DIM]
    out = jnp.concatenate([x1 * cos - x2 * sin, x2 * cos + x1 * sin, x32[..., ROT_DIM:]], axis=-1)
    return out.astype(x.dtype)


def split_heads(t, n):
    b, s, _ = t.shape
    return t.reshape(b, s, n, HEAD_DIM).transpose(0, 2, 1, 3)


def merge_heads(t):
    b, h, s, d = t.shape
    return t.transpose(0, 2, 1, 3).reshape(b, s, h * d)


def moba_attention(q, k, v):
    B, H, S, D = q.shape
    Sp = -(-S // MOBA_BLOCK) * MOBA_BLOCK
    pad = ((0, 0), (0, 0), (0, Sp - S), (0, 0))
    q, k, v = jnp.pad(q, pad), jnp.pad(k, pad), jnp.pad(v, pad)
    nb = Sp // MOBA_BLOCK
    topk = min(MOBA_TOPK, nb)
    kb = k.reshape(B, H, nb, MOBA_BLOCK, D)
    vb = v.reshape(B, H, nb, MOBA_BLOCK, D)
    kmean = jnp.mean(kb.astype(jnp.float32), axis=3).astype(k.dtype)
    bi = jnp.arange(B)[:, None, None, None]
    hi = jnp.arange(H)[None, :, None, None]
    blk_ids = jnp.arange(nb)

    def step(c):
        start = c * MOBA_Q_CHUNK
        blk = start // MOBA_BLOCK
        qc = lax.dynamic_slice_in_dim(q, start, MOBA_Q_CHUNK, axis=2)
        gate = jnp.einsum('bhqd,bhnd->bhqn', qc, kmean).astype(jnp.float32)
        gate = jnp.where(blk_ids < blk, gate, -jnp.inf)
        _, sel = lax.top_k(gate, topk)
        valid = sel < blk
        ks = kb[bi, hi, sel]
        vs = vb[bi, hi, sel]
        ls = jnp.einsum('bhqd,bhqnkd->bhqnk', qc, ks).astype(jnp.float32) * ATTN_SCALE
        ls = jnp.where(valid[..., None], ls, NEG_INF)
        kown = lax.dynamic_index_in_dim(kb, blk, axis=2, keepdims=False)
        vown = lax.dynamic_index_in_dim(vb, blk, axis=2, keepdims=False)
        lo = jnp.einsum('bhqd,bhkd->bhqk', qc, kown).astype(jnp.float32) * ATTN_SCALE
        qpos = start + jnp.arange(MOBA_Q_CHUNK)
        kpos = blk * MOBA_BLOCK + jnp.arange(MOBA_BLOCK)
        lo = jnp.where(kpos[None, :] <= qpos[:, None], lo, NEG_INF)
        logits = jnp.concatenate([ls.reshape(B, H, MOBA_Q_CHUNK, topk * MOBA_BLOCK), lo], axis=-1)
        p = jax.nn.softmax(logits, axis=-1).astype(v.dtype)
        ps = p[..., :topk * MOBA_BLOCK].reshape(B, H, MOBA_Q_CHUNK, topk, MOBA_BLOCK)
        po = p[..., topk * MOBA_BLOCK:]
        return (jnp.einsum('bhqnk,bhqnkd->bhqd', ps, vs)
                + jnp.einsum('bhqk,bhkd->bhqd', po, vown))

    out = lax.map(step, jnp.arange(Sp // MOBA_Q_CHUNK))
    out = out.transpose(1, 2, 0, 3, 4).reshape(B, H, Sp, D)
    return out[:, :, :S]


def fox_attention(q, k, v, log_F):
    B, H, S, D = q.shape
    kpos = jnp.arange(S)

    def step(i):
        start = i * FOX_Q_BLOCK
        qb = lax.dynamic_slice_in_dim(q, start, FOX_Q_BLOCK, axis=2)
        fq = lax.dynamic_slice_in_dim(log_F, start, FOX_Q_BLOCK, axis=2)
        logits = (jnp.einsum('bhqd,bhkd->bhqk', qb, k).astype(jnp.float32) * ATTN_SCALE
                  + fq[..., :, None] - log_F[:, :, None, :])
        qpos = start + jnp.arange(FOX_Q_BLOCK)
        logits = jnp.where(kpos[None, :] <= qpos[:, None], logits, NEG_INF)
        p = jax.nn.softmax(logits, axis=-1).astype(v.dtype)
        return jnp.einsum('bhqk,bhkd->bhqd', p, v)

    out = lax.map(step, jnp.arange(S // FOX_Q_BLOCK))
    return out.transpose(1, 2, 0, 3, 4).reshape(B, H, S, D)


def sliding_window_attention(q, k, v, sinks):
    B, Hq, S, D = q.shape
    Hkv = k.shape[1]
    G = Hq // Hkv
    W = SWA_WINDOW
    nb = S // W
    qb = q.reshape(B, Hkv, G, nb, W, D)

    def band(t):
        tb = t.reshape(B, Hkv, nb, W, D)
        prev = jnp.concatenate([jnp.zeros_like(tb[:, :, :1]), tb[:, :, :-1]], axis=2)
        return jnp.concatenate([prev, tb], axis=3)

    kc, vc = band(k), band(v)
    logits = jnp.einsum('bkgnqd,bknjd->bkgnqj', qb, kc).astype(jnp.float32) * ATTN_SCALE
    i = jnp.arange(W)[:, None]
    j = jnp.arange(2 * W)[None, :]
    n = jnp.arange(nb)[:, None, None]
    dist = W + i - j
    allowed = (dist >= 0) & (dist < SWA_WINDOW) & (n * W + j - W >= 0)
    logits = jnp.where(allowed, logits, NEG_INF)
    sink = jnp.broadcast_to(sinks.astype(jnp.float32).reshape(1, Hkv, G, 1, 1, 1),
                            logits.shape[:-1] + (1,))
    p = jax.nn.softmax(jnp.concatenate([logits, sink], axis=-1), axis=-1)[..., :2 * W]
    o = jnp.einsum('bkgnqj,bknjd->bkgnqd', p.astype(v.dtype), vc)
    return o.reshape(B, Hq, S, D)


def even_mixer(h, w_in, f_bias, qn_a, kn_a, qn_b, kn_b, w_out, cos, sin):
    proj = h @ w_in
    offs = [A_W, 2 * A_W, 3 * A_W, 3 * A_W + B_W, 3 * A_W + 2 * B_W, 3 * A_W + 3 * B_W, 3 * A_W + 4 * B_W]
    qa, ka, va, qb, kb, vb, gb, fb = jnp.split(proj, offs, axis=-1)
    qa = partial_rope(rms_norm(split_heads(qa, MOBA_HEADS), qn_a), cos, sin)
    ka = partial_rope(rms_norm(split_heads(ka, MOBA_HEADS), kn_a), cos, sin)
    oa = moba_attention(qa, ka, split_heads(va, MOBA_HEADS))
    qb = rms_norm(split_heads(qb, FOX_HEADS), qn_b)
    kb = rms_norm(split_heads(kb, FOX_HEADS), kn_b)
    log_f = jax.nn.log_sigmoid((fb + f_bias).astype(jnp.float32)).transpose(0, 2, 1)
    log_F = jnp.cumsum(log_f, axis=-1)
    ob = fox_attention(qb, kb, split_heads(vb, FOX_HEADS), log_F)
    ob = merge_heads(ob) * jax.nn.sigmoid(gb)
    return jnp.concatenate([merge_heads(oa), ob], axis=-1) @ w_out


def odd_mixer(h, w_in, qn, kn, sinks, w_out, cos, sin):
    proj = h @ w_in
    qw = SWA_Q_HEADS * HEAD_DIM
    kw = SWA_KV_HEADS * HEAD_DIM
    q, k, v = jnp.split(proj, [qw, qw + kw], axis=-1)
    q = partial_rope(rms_norm(split_heads(q, SWA_Q_HEADS), qn), cos, sin)
    k = partial_rope(rms_norm(split_heads(k, SWA_KV_HEADS), kn), cos, sin)
    o = sliding_window_attention(q, k, split_heads(v, SWA_KV_HEADS), sinks)
    return merge_heads(o) @ w_out


def peer_ffn(h, w_query, sub_keys, expert_down, expert_up):
    B, S, D = h.shape
    T = B * S
    ht = h.reshape(T, D)
    q = (ht @ w_query).reshape(T, PEER_HEADS, 2, PEER_QUERY_DIM // 2)
    s = jnp.einsum('thpd,hpnd->thpn', q, sub_keys).astype(jnp.float32)
    s1, i1 = lax.top_k(s[:, :, 0], PEER_TOPK)
    s2, i2 = lax.top_k(s[:, :, 1], PEER_TOPK)
    cand = (s1[..., :, None] + s2[..., None, :]).reshape(T, PEER_HEADS, PEER_TOPK * PEER_TOPK)
    cand_idx = (i1[..., :, None] * PEER_N_KEYS + i2[..., None, :]).reshape(T, PEER_HEADS, PEER_TOPK * PEER_TOPK)
    top_s, pos = lax.top_k(cand, PEER_TOPK)
    idx = jnp.take_along_axis(cand_idx, pos, axis=-1)
    g = jax.nn.softmax(top_s, axis=-1)
    nc = T // PEER_TOKEN_CHUNK

    def chunk(args):
        xc, ic, gc = args
        u = expert_down[ic]
        a = jax.nn.gelu(jnp.einsum('cd,chkd->chk', xc, u).astype(jnp.float32), approximate=False)
        w = (gc * a).astype(xc.dtype)
        return jnp.einsum('chk,chkd->cd', w, expert_up[ic])

    out = lax.map(chunk, (ht.reshape(nc, PEER_TOKEN_CHUNK, D),
                          idx.reshape(nc, PEER_TOKEN_CHUNK, PEER_HEADS, PEER_TOPK),
                          g.reshape(nc, PEER_TOKEN_CHUNK, PEER_HEADS, PEER_TOPK)))
    return out.reshape(B, S, D)


def setup_inputs(seed: int = 0) -> dict:
    key = jax.random.key(seed)
    ks = jax.random.split(key, 20)
    f32 = jnp.float32

    def nrm(k, shape, scale):
        return jax.random.normal(k, shape, f32) * scale

    def gain(k, shape):
        return 1.0 + 0.05 * jax.random.normal(k, shape, f32)

    return {
        "x": jax.random.normal(ks[0], (BATCH, SEQ, D_MODEL), f32),
        "attn_norm": gain(ks[1], (DEPTH, D_MODEL)),
        "ffn_norm": gain(ks[2], (DEPTH, D_MODEL)),
        "ev_w_in": nrm(ks[3], (N_EVEN, D_MODEL, EVEN_COLS), D_MODEL ** -0.5),
        "ev_forget_bias": 2.0 + 0.1 * jax.random.normal(ks[4], (N_EVEN, FOX_HEADS), f32),
        "ev_q_norm_a": gain(ks[5], (N_EVEN, HEAD_DIM)),
        "ev_k_norm_a": gain(ks[6], (N_EVEN, HEAD_DIM)),
        "ev_q_norm_b": gain(ks[7], (N_EVEN, HEAD_DIM)),
        "ev_k_norm_b": gain(ks[8], (N_EVEN, HEAD_DIM)),
        "ev_w_out": nrm(ks[9], (N_EVEN, EVEN_MIX_WIDTH, D_MODEL), EVEN_MIX_WIDTH ** -0.5),
        "od_w_in": nrm(ks[10], (N_ODD, D_MODEL, ODD_COLS), D_MODEL ** -0.5),
        "od_q_norm": gain(ks[11], (N_ODD, HEAD_DIM)),
        "od_k_norm": gain(ks[12], (N_ODD, HEAD_DIM)),
        "od_sinks": nrm(ks[13], (N_ODD, SWA_Q_HEADS), 0.5),
        "od_w_out": nrm(ks[14], (N_ODD, ODD_MIX_WIDTH, D_MODEL), ODD_MIX_WIDTH ** -0.5),
        "peer_w_query": nrm(ks[15], (DEPTH, D_MODEL, PEER_HEADS * PEER_QUERY_DIM), D_MODEL ** -0.5),
        "peer_sub_keys": nrm(ks[16], (DEPTH, PEER_HEADS, 2, PEER_N_KEYS, PEER_QUERY_DIM // 2),
                             (PEER_QUERY_DIM // 2) ** -0.5),
        "peer_down": nrm(ks[17], (DEPTH, PEER_N_EXPERTS, D_MODEL), D_MODEL ** -0.5),
        "peer_up": nrm(ks[18], (DEPTH, PEER_N_EXPERTS, D_MODEL), 0.25),
    }


def reference(x, attn_norm, ffn_norm, ev_w_in, ev_forget_bias, ev_q_norm_a, ev_k_norm_a,
              ev_q_norm_b, ev_k_norm_b, ev_w_out, od_w_in, od_q_norm, od_k_norm, od_sinks,
              od_w_out, peer_w_query, peer_sub_keys, peer_down, peer_up):
    S = x.shape[1]
    cos, sin = rope_tables(S)
    for l in range(DEPTH):
        h = rms_norm(x, attn_norm[l])
        i = l // 2
        if l % 2 == 0:
            x = x + even_mixer(h, ev_w_in[i], ev_forget_bias[i], ev_q_norm_a[i], ev_k_norm_a[i],
                               ev_q_norm_b[i], ev_k_norm_b[i], ev_w_out[i], cos, sin)
        else:
            x = x + odd_mixer(h, od_w_in[i], od_q_norm[i], od_k_norm[i], od_sinks[i],
                              od_w_out[i], cos, sin)
        h = rms_norm(x, ffn_norm[l])
        x = x + peer_ffn(h, peer_w_query[l], peer_sub_keys[l], peer_down[l], peer_up[l])
    return x
```

```python
import functools

import numpy as np
import jax
import jax.numpy as jnp
from jax import lax
from jax.experimental import pallas as pl
from jax.experimental.pallas import tpu as pltpu

F32 = jnp.float32
BF16 = jnp.bfloat16

HEAD_DIM = 64
ROT_DIM = HEAD_DIM // 4
ROPE_THETA = 500000.0
ATTN_SCALE = HEAD_DIM ** -0.5
EPS = 1e-6
NEG_INF = -1e30

MOBA_HEADS = 8
FOX_HEADS = 8
MOBA_BLOCK = 256
MOBA_TOPK = 3
A_W = MOBA_HEADS * HEAD_DIM
B_W = FOX_HEADS * HEAD_DIM

SWA_Q_HEADS = 16
SWA_KV_HEADS = 2
SWA_WINDOW = 128

PEER_HEADS = 8
PEER_N_KEYS = 128
PEER_TOPK = 16
PEER_QUERY_DIM = 128

LANES = 128
PAIR_W = 2 * HEAD_DIM
VMEM_LIMIT = 48 * 1024 * 1024


def _cparams(sem):
    return pltpu.CompilerParams(dimension_semantics=sem, vmem_limit_bytes=VMEM_LIMIT)


def _dot_nt(a, b):
    return lax.dot_general(a, b, (((1,), (1,)), ((), ())), preferred_element_type=F32)


def _dot(a, b):
    return jnp.dot(a, b, preferred_element_type=F32)


def _split3(x):
    h1 = x.astype(BF16)
    r1 = x - h1.astype(F32)
    h2 = r1.astype(BF16)
    h3 = (r1 - h2.astype(F32)).astype(BF16)
    return h1, h2, h3


def _rmsnorm_kernel(x_ref, g_ref, o_ref):
    x = x_ref[...]
    ms = jnp.mean(x * x, axis=-1, keepdims=True)
    o_ref[...] = (x * lax.rsqrt(ms + EPS) * g_ref[...]).astype(o_ref.dtype)


def _rmsnorm(x, gain, tm=512):
    T, D = x.shape
    return pl.pallas_call(
        _rmsnorm_kernel,
        grid=(T // tm,),
        in_specs=[pl.BlockSpec((tm, D), lambda i: (i, 0)),
                  pl.BlockSpec((1, D), lambda i: (0, 0))],
        out_specs=pl.BlockSpec((tm, D), lambda i: (i, 0)),
        out_shape=jax.ShapeDtypeStruct((T, D), BF16),
        compiler_params=_cparams(("parallel",)),
        name="rmsnorm",
    )(x, gain.reshape(1, D))


def _proj_kernel(*refs, norm, rope, kmean, tn):
    it = iter(refs)
    h_ref, w_ref = next(it), next(it)
    gain_ref = next(it) if norm else None
    bd_ref = next(it) if norm else None
    if rope:
        c_ref, sa_ref, sb_ref = next(it), next(it), next(it)
    o_ref = next(it)
    km_ref = next(it) if kmean else None

    y = _dot(h_ref[...], w_ref[...])
    if norm:
        y2 = y * y
        bd = bd_ref[...]
        cols = []
        for c in range(tn // LANES):
            h1, h2, h3 = _split3(y2[:, c * LANES:(c + 1) * LANES])
            cols.append(_dot(h1, bd) + _dot(h2, bd) + _dot(h3, bd))
        ms = cols[0] if len(cols) == 1 else jnp.concatenate(cols, axis=1)
        y = y * lax.rsqrt(ms + EPS) * gain_ref[...]
    if rope:
        rep = tn // LANES
        tile = (lambda t: t) if rep == 1 else (lambda t: jnp.concatenate([t] * rep, axis=1))
        y = (y * tile(c_ref[...])
             + pltpu.roll(y, tn - ROT_DIM // 2, 1) * tile(sa_ref[...])
             + pltpu.roll(y, ROT_DIM // 2, 1) * tile(sb_ref[...]))
    o_ref[...] = y.astype(o_ref.dtype)
    if kmean:
        km_ref[0] = jnp.mean(y, axis=0, keepdims=True)


def _proj(h, w, *, seq, tn, tm=512, gain=None, rope_tabs=None, kmean=False):
    T, D = h.shape
    N = w.shape[1]
    norm = gain is not None
    rope = rope_tabs is not None
    nseq = seq // tm
    in_specs = [pl.BlockSpec((tm, D), lambda i, j: (i, 0)),
                pl.BlockSpec((D, tn), lambda i, j: (0, j))]
    args = [h, w]
    if norm:
        bd = np.kron(np.eye(LANES // HEAD_DIM), np.ones((HEAD_DIM, HEAD_DIM))) / HEAD_DIM
        in_specs += [pl.BlockSpec((1, tn), lambda i, j: (0, j)),
                     pl.BlockSpec((LANES, LANES), lambda i, j: (0, 0))]
        args += [gain.reshape(1, N).astype(F32), jnp.asarray(bd, BF16)]
    if rope:
        in_specs += [pl.BlockSpec((tm, LANES), lambda i, j: (i % nseq, 0))] * 3
        args += list(rope_tabs)
    out_specs = [pl.BlockSpec((tm, tn), lambda i, j: (i, j))]
    out_shape = [jax.ShapeDtypeStruct((T, N), BF16)]
    if kmean:
        out_specs.append(pl.BlockSpec((1, 1, tn), lambda i, j: (i, 0, j)))
        out_shape.append(jax.ShapeDtypeStruct((T // tm, 1, N), F32))
    res = pl.pallas_call(
        functools.partial(_proj_kernel, norm=norm, rope=rope, kmean=kmean, tn=tn),
        grid=(T // tm, N // tn),
        in_specs=in_specs, out_specs=out_specs, out_shape=out_shape,
        compiler_params=_cparams(("parallel", "parallel")),
        name="proj",
    )(*args)
    return res if kmean else res[0]


def _rope_tables(seq):
    half = ROT_DIM // 2
    inv_freq = jnp.power(ROPE_THETA, -jnp.arange(0, ROT_DIM, 2, dtype=F32) / ROT_DIM)
    ang = jnp.arange(seq, dtype=F32)[:, None] * inv_freq[None, :]
    cos, sin = jnp.cos(ang), jnp.sin(ang)
    one = jnp.ones((seq, HEAD_DIM - ROT_DIM), F32)
    zero = jnp.zeros((seq, HEAD_DIM - ROT_DIM), F32)
    z8 = jnp.zeros((seq, half), F32)
    c = jnp.concatenate([cos, cos, one], axis=1)
    sa = jnp.concatenate([-sin, z8, zero], axis=1)
    sb = jnp.concatenate([z8, sin, zero], axis=1)
    rep = LANES // HEAD_DIM
    return tuple(jnp.concatenate([t] * rep, axis=1) for t in (c, sa, sb))


def _gates_kernel(h_ref, wf_ref, b_ref, tri_ref, o_ref, carry_ref):
    @pl.when(pl.program_id(1) == 0)
    def _():
        carry_ref[...] = jnp.zeros_like(carry_ref)

    z = _dot_nt(wf_ref[...], h_ref[...]) + b_ref[...][:, :1]
    lf = jnp.minimum(z, 0.0) - jnp.log1p(jnp.exp(-jnp.abs(z)))
    tri = tri_ref[...]
    h1, h2, h3 = _split3(lf)
    cs = _dot(h1, tri) + _dot(h2, tri) + _dot(h3, tri) + carry_ref[...][:, :1]
    o_ref[0] = cs
    carry_ref[...] = jnp.broadcast_to(cs[:, -1:], carry_ref.shape)


def _fox_gates(h, wf_t, bias, *, batch, seq, tm=512):
    T, D = h.shape
    nh = wf_t.shape[0]
    nseq = seq // tm
    tri = jnp.asarray(np.triu(np.ones((tm, tm))), BF16)
    return pl.pallas_call(
        _gates_kernel,
        grid=(batch, nseq),
        in_specs=[pl.BlockSpec((tm, D), lambda b, s: (b * nseq + s, 0)),
                  pl.BlockSpec((nh, D), lambda b, s: (0, 0)),
                  pl.BlockSpec((nh, LANES), lambda b, s: (0, 0)),
                  pl.BlockSpec((tm, tm), lambda b, s: (0, 0))],
        out_specs=pl.BlockSpec((1, nh, tm), lambda b, s: (b, 0, s)),
        out_shape=jax.ShapeDtypeStruct((batch, nh, seq), F32),
        scratch_shapes=[pltpu.VMEM((nh, LANES), F32)],
        compiler_params=_cparams(("parallel", "arbitrary")),
        name="fox_gates",
    )(h, wf_t, jnp.broadcast_to(bias.astype(F32)[:, None], (nh, LANES)), tri)


def _softmax_step(carry, s, v):
    m, l, acc = carry
    m_new = jnp.maximum(m, jnp.max(s, axis=1, keepdims=True))
    alpha = jnp.exp(m - m_new)
    p = jnp.exp(s - m_new)
    l = alpha * l + jnp.sum(p, axis=1, keepdims=True)
    acc = alpha * acc + _dot(p.astype(BF16), v)
    return m_new, l, acc


def _softmax_first(s, v):
    m = jnp.max(s, axis=1, keepdims=True)
    p = jnp.exp(s - m)
    return m, jnp.sum(p, axis=1, keepdims=True), _dot(p.astype(BF16), v)


def _moba_kernel(q_ref, k_ref, v_ref, km_ref, o_ref):
    tq = q_ref.shape[1]
    qi = pl.program_id(2)
    q = q_ref[0]
    lane = lax.broadcasted_iota(jnp.int32, (tq, LANES), 1)
    row = lax.broadcasted_iota(jnp.int32, (tq, tq), 0)
    col = lax.broadcasted_iota(jnp.int32, (tq, tq), 1)
    start = pl.multiple_of(qi * tq, tq)
    k_own = k_ref[0, pl.ds(start, tq), :]
    v_own = v_ref[0, pl.ds(start, tq), :]
    outs = []
    for hh in range(2):
        in_head = (lane >= hh * HEAD_DIM) & (lane < (hh + 1) * HEAD_DIM)
        qh = jnp.where(in_head, q, jnp.zeros_like(q))
        gate = _dot_nt(qh, km_ref[0])
        gate = jnp.where(lane < qi, gate, -jnp.inf)
        sel = jnp.zeros((tq, LANES), F32)
        lane_f = lane.astype(F32)
        for _ in range(MOBA_TOPK):
            m = jnp.max(gate, axis=1, keepdims=True)
            idx = jnp.min(jnp.where(gate == m, lane_f, float(LANES)), axis=1, keepdims=True)
            hit = lane_f == idx
            sel = jnp.where(hit & (m > -jnp.inf), 1.0, sel)
            gate = jnp.where(hit, -jnp.inf, gate)
        s = _dot_nt(qh, k_own) * ATTN_SCALE
        s = jnp.where(col <= row, s, NEG_INF)
        carry = _softmax_first(s, v_own)

        def body(j, carry, qh=qh, sel=sel):
            off = pl.multiple_of(j * tq, tq)
            kj = k_ref[0, pl.ds(off, tq), :]
            vj = v_ref[0, pl.ds(off, tq), :]
            chosen = jnp.max(jnp.where(lane == j, sel, 0.0), axis=1, keepdims=True)
            s = _dot_nt(qh, kj) * ATTN_SCALE
            s = jnp.where(chosen > 0.0, s, NEG_INF)
            return _softmax_step(carry, s, vj)

        m, l, acc = lax.fori_loop(0, qi, body, carry)
        outs.append(acc / l)
    o_ref[0] = jnp.where(lane < HEAD_DIM, outs[0], outs[1]).astype(o_ref.dtype)


def _moba(q, k, v, kmean, *, batch, seq):
    W = q.shape[1]
    tq = MOBA_BLOCK
    q3, k3, v3 = (t.reshape(batch, seq, W) for t in (q, k, v))
    out = pl.pallas_call(
        _moba_kernel,
        grid=(batch, W // PAIR_W, seq // tq),
        in_specs=[pl.BlockSpec((1, tq, PAIR_W), lambda b, p, i: (b, i, p)),
                  pl.BlockSpec((1, seq, PAIR_W), lambda b, p, i: (b, 0, p)),
                  pl.BlockSpec((1, seq, PAIR_W), lambda b, p, i: (b, 0, p)),
                  pl.BlockSpec((1, LANES, PAIR_W), lambda b, p, i: (b, 0, p))],
        out_specs=pl.BlockSpec((1, tq, PAIR_W), lambda b, p, i: (b, i, p)),
        out_shape=jax.ShapeDtypeStruct((batch, seq, W), BF16),
        compiler_params=_cparams(("parallel", "parallel", "parallel")),
        name="moba",
    )(q3, k3, v3, kmean)
    return out.reshape(batch * seq, W)


def _fox_kernel(q_ref, k_ref, v_ref, g_ref, fcol_ref, frow_ref, o_ref):
    tq = q_ref.shape[1]
    pr = pl.program_id(1)
    qi = pl.program_id(2)
    q = q_ref[0]
    lane = lax.broadcasted_iota(jnp.int32, (tq, LANES), 1)
    row = lax.broadcasted_iota(jnp.int32, (tq, tq), 0)
    col = lax.broadcasted_iota(jnp.int32, (tq, tq), 1)
    start = pl.multiple_of(qi * tq, tq)
    k_own = k_ref[0, pl.ds(start, tq), :]
    v_own = v_ref[0, pl.ds(start, tq), :]
    fcol = fcol_ref[0]
    outs = []
    for hh in range(2):
        head = 2 * pr + hh
        in_head = (lane >= hh * HEAD_DIM) & (lane < (hh + 1) * HEAD_DIM)
        qh = jnp.where(in_head, q, jnp.zeros_like(q))
        fq = jnp.sum(jnp.where(lane == head, fcol, 0.0), axis=1, keepdims=True)
        def key_gate(off, head=head):
            blk = frow_ref[0, :, pl.ds(off, tq)]
            sub = lax.broadcasted_iota(jnp.int32, blk.shape, 0)
            return jnp.sum(jnp.where(sub == head, blk, 0.0), axis=0, keepdims=True)

        s = _dot_nt(qh, k_own) * ATTN_SCALE + fq - key_gate(start)
        s = jnp.where(col <= row, s, NEG_INF)
        carry = _softmax_first(s, v_own)

        def body(j, carry, qh=qh, fq=fq, key_gate=key_gate):
            off = pl.multiple_of(j * tq, tq)
            kj = k_ref[0, pl.ds(off, tq), :]
            vj = v_ref[0, pl.ds(off, tq), :]
            s = _dot_nt(qh, kj) * ATTN_SCALE + fq - key_gate(off)
            return _softmax_step(carry, s, vj)

        m, l, acc = lax.fori_loop(0, qi, body, carry)
        outs.append(acc / l)
    o = jnp.where(lane < HEAD_DIM, outs[0], outs[1])
    o_ref[0] = (o * jax.nn.sigmoid(g_ref[0].astype(F32))).astype(o_ref.dtype)


def _fox(q, k, v, g, fcol, frow, *, batch, seq, tq=256):
    W = q.shape[1]
    nh = frow.shape[1]
    q3, k3, v3, g3 = (t.reshape(batch, seq, W) for t in (q, k, v, g))
    out = pl.pallas_call(
        _fox_kernel,
        grid=(batch, W // PAIR_W, seq // tq),
        in_specs=[pl.BlockSpec((1, tq, PAIR_W), lambda b, p, i: (b, i, p)),
                  pl.BlockSpec((1, seq, PAIR_W), lambda b, p, i: (b, 0, p)),
                  pl.BlockSpec((1, seq, PAIR_W), lambda b, p, i: (b, 0, p)),
                  pl.BlockSpec((1, tq, PAIR_W), lambda b, p, i: (b, i, p)),
                  pl.BlockSpec((1, tq, LANES), lambda b, p, i: (b, i, 0)),
                  pl.BlockSpec((1, nh, seq), lambda b, p, i: (b, 0, 0))],
        out_specs=pl.BlockSpec((1, tq, PAIR_W), lambda b, p, i: (b, i, p)),
        out_shape=jax.ShapeDtypeStruct((batch, seq, W), BF16),
        compiler_params=_cparams(("parallel", "parallel", "parallel")),
        name="fox",
    )(q3, k3, v3, g3, fcol, frow)
    return out.reshape(batch * seq, W)


def _swa_kernel(q_ref, k_ref, v_ref, sink_ref, o_ref, *, pairs_per_kv):
    tq = q_ref.shape[1]
    pr = pl.program_id(1)
    qi = pl.program_id(2)
    kv_head = pr // pairs_per_kv
    q = q_ref[0]
    lane = lax.broadcasted_iota(jnp.int32, (tq, LANES), 1)
    kstart = pl.multiple_of(jnp.maximum(qi * tq - SWA_WINDOW, 0), SWA_WINDOW)
    tk = tq + SWA_WINDOW
    k = k_ref[0, pl.ds(kstart, tk), :]
    v = v_ref[0, pl.ds(kstart, tk), :]
    qpos = qi * tq + lax.broadcasted_iota(jnp.int32, (tq, tk), 0)
    kpos = kstart + lax.broadcasted_iota(jnp.int32, (tq, tk), 1)
    dist = qpos - kpos
    allowed = (dist >= 0) & (dist < SWA_WINDOW)
    outs = []
    for hh in range(2):
        in_head = (lane >= hh * HEAD_DIM) & (lane < (hh + 1) * HEAD_DIM)
        qh = jnp.where(in_head, q, jnp.zeros_like(q)).astype(F32)
        aligned = kv_head == hh
        qa = jnp.where(aligned, qh, pltpu.roll(qh, HEAD_DIM, 1)).astype(BF16)
        s = _dot_nt(qa, k) * ATTN_SCALE
        s = jnp.where(allowed, s, NEG_INF)
        sink_tab = sink_ref[...]
        sub = lax.broadcasted_iota(jnp.int32, sink_tab.shape, 0)
        sink = jnp.sum(jnp.where(sub == 2 * pr + hh, sink_tab, 0.0), axis=0, keepdims=True)[:, :1]
        m = jnp.maximum(jnp.max(s, axis=1, keepdims=True), sink)
        p = jnp.exp(s - m)
        l = jnp.sum(p, axis=1, keepdims=True) + jnp.exp(sink - m)
        o = _dot(p.astype(BF16), v) / l
        outs.append(jnp.where(aligned, o, pltpu.roll(o, HEAD_DIM, 1)))
    o_ref[0] = jnp.where(lane < HEAD_DIM, outs[0], outs[1]).astype(o_ref.dtype)


def _swa(q, k, v, sinks, *, batch, seq, tq=SWA_WINDOW):
    W = q.shape[1]
    n_pairs = W // PAIR_W
    q3 = q.reshape(batch, seq, W)
    k3, v3 = (t.reshape(batch, seq, PAIR_W) for t in (k, v))
    sink_tab = jnp.broadcast_to(sinks.astype(F32)[:, None], (SWA_Q_HEADS, LANES))
    out = pl.pallas_call(
        functools.partial(_swa_kernel, pairs_per_kv=n_pairs // SWA_KV_HEADS),
        grid=(batch, n_pairs, seq // tq),
        in_specs=[pl.BlockSpec((1, tq, PAIR_W), lambda b, p, i: (b, i, p)),
                  pl.BlockSpec((1, seq, PAIR_W), lambda b, p, i: (b, 0, 0)),
                  pl.BlockSpec((1, seq, PAIR_W), lambda b, p, i: (b, 0, 0)),
                  pl.BlockSpec((SWA_Q_HEADS, LANES), lambda b, p, i: (0, 0))],
        out_specs=pl.BlockSpec((1, tq, PAIR_W), lambda b, p, i: (b, i, p)),
        out_shape=jax.ShapeDtypeStruct((batch, seq, W), BF16),
        compiler_params=_cparams(("parallel", "parallel", "parallel")),
        name="swa",
    )(q3, k3, v3, sink_tab)
    return out.reshape(batch * seq, W)


def _outproj_kernel(*refs, n_parts):
    parts = refs[:n_parts]
    w_ref, x_ref, o_ref = refs[n_parts:]
    y = x_ref[...]
    off = 0
    for p_ref in parts:
        kw = p_ref.shape[1]
        y = y + _dot(p_ref[...], w_ref[off:off + kw, :])
        off += kw
    o_ref[...] = y


def _outproj(parts, w, x, tm=512):
    T, D = x.shape
    in_specs = [pl.BlockSpec((tm, p.shape[1]), lambda i: (i, 0)) for p in parts]
    in_specs += [pl.BlockSpec(w.shape, lambda i: (0, 0)),
                 pl.BlockSpec((tm, D), lambda i: (i, 0))]
    return pl.pallas_call(
        functools.partial(_outproj_kernel, n_parts=len(parts)),
        grid=(T // tm,),
        in_specs=in_specs,
        out_specs=pl.BlockSpec((tm, D), lambda i: (i, 0)),
        out_shape=jax.ShapeDtypeStruct((T, D), F32),
        compiler_params=_cparams(("parallel",)),
        name="outproj",
    )(*parts, w, x)


_CAND_ROWS = 80


def _cand_tables():
    pos = np.zeros((_CAND_ROWS,), np.float32)
    neg = np.zeros((_CAND_ROWS,), np.float32)
    r = 0
    for a, nb in ((0, 16), (1, 8), (2, 8), (3, 8), (4, 8), (5, 8), (6, 8), (7, 8)):
        for b in range(nb):
            pos[r] = a * PEER_TOPK + b
            neg[r] = 0.0 if (a + 1) * (b + 1) <= PEER_TOPK else -np.inf
            r += 1
    for a in range(8, 16):
        pos[r] = a * PEER_TOPK
        r += 1
    assert r == _CAND_ROWS
    tab = lambda t: jnp.asarray(np.broadcast_to(t[:, None], (_CAND_ROWS, LANES)).copy())
    return tab(pos), tab(neg)


def _extract_top(v, nk):
    kio = lax.broadcasted_iota(jnp.int32, v.shape, 0).astype(F32)
    slot = lax.broadcasted_iota(jnp.int32, (PEER_TOPK, v.shape[1]), 0)

    def body(a, carry):
        v, rank, vals = carry
        m = jnp.max(v, axis=0, keepdims=True)
        idx = jnp.min(jnp.where(v == m, kio, float(nk)), axis=0, keepdims=True)
        hit = kio == idx
        return (jnp.where(hit, -jnp.inf, v), jnp.where(hit, a.astype(F32), rank),
                jnp.where(slot == a, m, vals))

    init = (v, jnp.full(v.shape, float(PEER_TOPK), F32), jnp.zeros((PEER_TOPK, v.shape[1]), F32))
    _, rank, vals = lax.fori_loop(0, PEER_TOPK, body, init)
    return vals, rank


def _route_kernel(h_ref, wq_ref, keys_ref, pos_ref, neg_ref,
                  c1_ref, e1_ref, r2_ref, e2_ref, qt_ref):
    half = PEER_QUERY_DIM // 2
    qt_ref[...] = _dot_nt(wq_ref[...], h_ref[...]).astype(BF16)
    pos = pos_ref[...]
    neg = neg_ref[...]
    slot = lax.broadcasted_iota(jnp.int32, (PEER_TOPK, h_ref.shape[0]), 0)

    def head_body(h, _):
        r0 = pl.multiple_of(h * PEER_QUERY_DIM, PEER_QUERY_DIM)
        s1 = _dot(keys_ref[2 * h], qt_ref[pl.ds(r0, half), :])
        s2 = _dot(keys_ref[2 * h + 1], qt_ref[pl.ds(r0 + half, half), :])
        v1, rank1 = _extract_top(s1, PEER_N_KEYS)
        v2, rank2 = _extract_top(s2, PEER_N_KEYS)
        blocks = [v1[0:1] + v2[0:8], v1[0:1] + v2[8:16]]
        blocks += [v1[a:a + 1] + v2[0:8] for a in range(1, 8)]
        blocks += [v1[8:16] + v2[0:1]]
        cand = jnp.concatenate(blocks, axis=0) + neg

        def pick(kk, carry):
            cand, chosen, ts = carry
            m = jnp.max(cand, axis=0, keepdims=True)
            first = jnp.min(jnp.where(cand == m, pos, 1e9), axis=0, keepdims=True)
            hit = pos == first
            return (jnp.where(hit, -jnp.inf, cand), jnp.where(hit, 1.0, chosen),
                    jnp.where(slot == kk, m, ts))

        _, chosen, ts = lax.fori_loop(0, PEER_TOPK, pick,
                                      (cand, jnp.zeros_like(cand), jnp.zeros_like(v1)))
        z =jnp.sum(jnp.exp(ts - ts[0:1]), axis=0, keepdims=True)
        counts = [jnp.sum(chosen[0:16], axis=0, keepdims=True)]
        counts += [jnp.sum(chosen[8 * a + 8:8 * a + 16], axis=0, keepdims=True) for a in range(1, 8)]
        counts += [chosen[72 + a:73 + a] for a in range(8)]
        c1 = jnp.zeros_like(rank1)
        for a in range(PEER_TOPK):
            c1 = jnp.where(rank1 == float(a), counts[a], c1)
        c1_ref[h] = c1
        e1_ref[h] = jnp.exp(s1 - v1[0:1]) / z
        r2_ref[h] = rank2
        e2_ref[h] = jnp.exp(s2 - v2[0:1])
        return 0

    lax.fori_loop(0, PEER_HEADS, head_body, 0)


def _peer_route(h2, wq_t, keys, tt=LANES):
    T, D = h2.shape
    pos, neg = _cand_tables()
    stat = jax.ShapeDtypeStruct((PEER_HEADS, PEER_N_KEYS, T), F32)
    stat_spec = pl.BlockSpec((PEER_HEADS, PEER_N_KEYS, tt), lambda i: (0, 0, i))
    return pl.pallas_call(
        _route_kernel,
        grid=(T // tt,),
        in_specs=[pl.BlockSpec((tt, D), lambda i: (i, 0)),
                  pl.BlockSpec(wq_t.shape, lambda i: (0, 0)),
                  pl.BlockSpec(keys.shape, lambda i: (0, 0, 0)),
                  pl.BlockSpec((_CAND_ROWS, LANES), lambda i: (0, 0)),
                  pl.BlockSpec((_CAND_ROWS, LANES), lambda i: (0, 0))],
        out_specs=[stat_spec] * 4,
        out_shape=[stat] * 4,
        scratch_shapes=[pltpu.VMEM((PEER_HEADS * PEER_QUERY_DIM, tt), BF16)],
        compiler_params=_cparams(("parallel",)),
        name="peer_route",
    )(h2, wq_t, keys, pos, neg)


_CHUNK_ROWS = 64


def _experts_kernel(h_ref, dn_ref, upt_ref, c1_ref, e1_ref, r2_ref, e2_ref, x_ref,
                    o_ref, a_ref, p_ref, acc_ref):
    te, tt = a_ref.shape
    eb = pl.program_id(1)
    groups = te // PEER_N_KEYS

    @pl.when(eb == 0)
    def _():
        acc_ref[...] = jnp.zeros_like(acc_ref)

    a_ref[...] = _dot_nt(dn_ref[...], h_ref[...])

    n_lane = tt // LANES
    n_row = PEER_N_KEYS // _CHUNK_ROWS
    key0 = pl.multiple_of(eb * groups, groups)

    for g in range(groups):

        def chunk(c, _, g=g):
            tl = pl.multiple_of((c % n_lane) * LANES, LANES)
            jr = pl.multiple_of((c // n_lane) * _CHUNK_ROWS, _CHUNK_ROWS)
            w = jnp.zeros((_CHUNK_ROWS, LANES), F32)
            for h in range(PEER_HEADS):
                thr = c1_ref[h, pl.ds(key0, groups), pl.ds(tl, LANES)][g:g + 1]
                e1 = e1_ref[h, pl.ds(key0, groups), pl.ds(tl, LANES)][g:g + 1]
                r2 = r2_ref[h, pl.ds(jr, _CHUNK_ROWS), pl.ds(tl, LANES)]
                e2 = e2_ref[h, pl.ds(jr, _CHUNK_ROWS), pl.ds(tl, LANES)]
                w = w + jnp.where(r2 < thr, e2, 0.0) * e1
            a = a_ref[pl.ds(g * PEER_N_KEYS + jr, _CHUNK_ROWS), pl.ds(tl, LANES)]
            gelu = 0.5 * a * (1.0 + lax.erf(a * (2.0 ** -0.5)))
            p_ref[pl.ds(g * PEER_N_KEYS + jr, _CHUNK_ROWS), pl.ds(tl, LANES)] = (w * gelu).astype(BF16)
            return 0

        lax.fori_loop(0, n_lane * n_row, chunk, 0)

    acc_ref[...] += _dot(upt_ref[...], p_ref[...])

    @pl.when(eb == pl.num_programs(1) - 1)
    def _():
        o_ref[...] = x_ref[...] + acc_ref[...].T


def _peer_experts(h2, down, up_t, stats, x, tt=512, te=8 * PEER_N_KEYS):
    T, D = h2.shape
    E = down.shape[0]
    assert te // PEER_N_KEYS == 8
    stat_spec = pl.BlockSpec((PEER_HEADS, PEER_N_KEYS, tt), lambda i, e: (0, 0, i))
    return pl.pallas_call(
        _experts_kernel,
        grid=(T // tt, E // te),
        in_specs=[pl.BlockSpec((tt, D), lambda i, e: (i, 0)),
                  pl.BlockSpec((te, D), lambda i, e: (e, 0)),
                  pl.BlockSpec((D, te), lambda i, e: (0, e)),
                  stat_spec, stat_spec, stat_spec, stat_spec,
                  pl.BlockSpec((tt, D), lambda i, e: (i, 0))],
        out_specs=pl.BlockSpec((tt, D), lambda i, e: (i, 0)),
        out_shape=jax.ShapeDtypeStruct((T, D), F32),
        scratch_shapes=[pltpu.VMEM((te, tt), F32),
                        pltpu.VMEM((te, tt), BF16),
                        pltpu.VMEM((D, tt), F32)],
        compiler_params=_cparams(("parallel", "arbitrary")),
        name="peer_experts",
    )(h2, down, up_t, *stats, x)


def _peer_layer(x, norm_gain, w_query, sub_keys, down, up):
    h2 = _rmsnorm(x, norm_gain)
    keys = sub_keys.reshape(PEER_HEADS * 2, PEER_N_KEYS, PEER_QUERY_DIM // 2).astype(BF16)
    stats = _peer_route(h2, w_query.T.astype(BF16), keys)
    return _peer_experts(h2, down.astype(BF16), up.T.astype(BF16), stats, x)


def _tile_heads(g, n):
    return jnp.tile(g.astype(F32), n)


def _even_mixer(x, norm_gain, w_in, f_bias, qn_a, kn_a, qn_b, kn_b, w_out, tabs, *, batch, seq):
    h = _rmsnorm(x, norm_gain)
    w = w_in.astype(BF16)
    o_qa, o_ka, o_va, o_qb, o_kb, o_vb, o_gb, o_fb = (
        0, A_W, 2 * A_W, 3 * A_W, 3 * A_W + B_W, 3 * A_W + 2 * B_W, 3 * A_W + 3 * B_W, 3 * A_W + 4 * B_W)
    gain_a = jnp.concatenate([_tile_heads(qn_a, MOBA_HEADS), _tile_heads(kn_a, MOBA_HEADS)])
    qk_a, km = _proj(h, w[:, o_qa:o_va], seq=seq, tn=A_W, tm=MOBA_BLOCK, gain=gain_a,
                     rope_tabs=tabs, kmean=True)
    nb = seq // MOBA_BLOCK
    kmean = km.reshape(batch, nb, 2 * A_W)[:, :, A_W:]
    kmean = jnp.pad(kmean, ((0, 0), (0, LANES - nb), (0, 0))).astype(BF16)
    gain_b = jnp.concatenate([_tile_heads(qn_b, FOX_HEADS), _tile_heads(kn_b, FOX_HEADS)])
    qk_b = _proj(h, w[:, o_qb:o_vb], seq=seq, tn=B_W, gain=gain_b)
    w_plain = jnp.concatenate([w[:, o_va:o_qb], w[:, o_vb:o_fb]], axis=1)
    vvg = _proj(h, w_plain, seq=seq, tn=A_W)
    va, vb, gb = vvg[:, :A_W], vvg[:, A_W:A_W + B_W], vvg[:, A_W + B_W:]
    frow = _fox_gates(h, w[:, o_fb:].T, f_bias, batch=batch, seq=seq)
    fcol = jnp.pad(frow.transpose(0, 2, 1), ((0, 0), (0, 0), (0, LANES - FOX_HEADS)))
    oa = _moba(qk_a[:, :A_W], qk_a[:, A_W:], va, kmean, batch=batch, seq=seq)
    ob = _fox(qk_b[:, :B_W], qk_b[:, B_W:], vb, gb, fcol, frow, batch=batch, seq=seq)
    return _outproj([oa, ob], w_out.astype(BF16), x)


def _odd_mixer(x, norm_gain, w_in, qn, kn, sinks, w_out, tabs, *, batch, seq):
    h = _rmsnorm(x, norm_gain)
    w = w_in.astype(BF16)
    qw = SWA_Q_HEADS * HEAD_DIM
    kw = SWA_KV_HEADS * HEAD_DIM
    q = _proj(h, w[:, :qw], seq=seq, tn=512, gain=_tile_heads(qn, SWA_Q_HEADS), rope_tabs=tabs)
    k = _proj(h, w[:, qw:qw + kw], seq=seq, tn=kw, gain=_tile_heads(kn, SWA_KV_HEADS), rope_tabs=tabs)
    v = _proj(h, w[:, qw + kw:], seq=seq, tn=kw)
    o = _swa(q, k, v, sinks, batch=batch, seq=seq)
    return _outproj([o], w_out.astype(BF16), x)


def kernel(x, attn_norm, ffn_norm, ev_w_in, ev_forget_bias, ev_q_norm_a, ev_k_norm_a, ev_q_norm_b,
           ev_k_norm_b, ev_w_out, od_w_in, od_q_norm, od_k_norm, od_sinks, od_w_out,
           peer_w_query, peer_sub_keys, peer_down, peer_up):
    batch, seq, d_model = x.shape
    depth = attn_norm.shape[0]
    tabs = _rope_tables(seq)
    xt = x.reshape(batch * seq, d_model)
    for l in range(depth):
        i = l // 2
        if l % 2 == 0:
            xt = _even_mixer(xt, attn_norm[l], ev_w_in[i], ev_forget_bias[i], ev_q_norm_a[i],
                             ev_k_norm_a[i], ev_q_norm_b[i], ev_k_norm_b[i], ev_w_out[i], tabs,
                             batch=batch, seq=seq)
        else:
            xt = _odd_mixer(xt, attn_norm[l], od_w_in[i], od_q_norm[i], od_k_norm[i], od_sinks[i],
                            od_w_out[i], tabs, batch=batch, seq=seq)
        xt = _peer_layer(xt, ffn_norm[l], peer_w_query[l], peer_sub_keys[l], peer_down[l], peer_up[l])
    return xt.reshape(batch, seq, d_model)
```

```python
import functools

import numpy as np
import jax
import jax.numpy as jnp
from jax import lax
from jax.experimental import pallas as pl
from jax.experimental.pallas import tpu as pltpu

F32 = jnp.float32
BF16 = jnp.bfloat16

HEAD_DIM = 64
ROT_DIM = HEAD_DIM // 4
ROPE_THETA = 500000.0
ATTN_SCALE = HEAD_DIM ** -0.5
EPS = 1e-6
NEG_INF = -1e30

MOBA_HEADS = 8
FOX_HEADS = 8
MOBA_BLOCK = 256
MOBA_TOPK = 3
A_W = MOBA_HEADS * HEAD_DIM
B_W = FOX_HEADS * HEAD_DIM

SWA_Q_HEADS = 16
SWA_KV_HEADS = 2
SWA_WINDOW = 128

PEER_HEADS = 8
PEER_N_KEYS = 128
PEER_TOPK = 16
PEER_QUERY_DIM = 128

LANES = 128
PAIR_W = 2 * HEAD_DIM
VMEM_LIMIT = 48 * 1024 * 1024


def _cparams(sem):
    return pltpu.CompilerParams(dimension_semantics=sem, vmem_limit_bytes=VMEM_LIMIT)


def _dot_nt(a, b):
    return lax.dot_general(a, b, (((1,), (1,)), ((), ())), preferred_element_type=F32)


def _dot(a, b):
    return jnp.dot(a, b, preferred_element_type=F32)


def _split3(x):
    h1 = x.astype(BF16)
    r1 = x - h1.astype(F32)
    h2 = r1.astype(BF16)
    h3 = (r1 - h2.astype(F32)).astype(BF16)
    return h1, h2, h3


def _rmsnorm_kernel(x_ref, g_ref, o_ref):
    x = x_ref[...]
    ms = jnp.mean(x * x, axis=-1, keepdims=True)
    o_ref[...] = (x * lax.rsqrt(ms + EPS) * g_ref[...]).astype(o_ref.dtype)


def _rmsnorm(x, gain, tm=512):
    T, D = x.shape
    return pl.pallas_call(
        _rmsnorm_kernel,
        grid=(T // tm,),
        in_specs=[pl.BlockSpec((tm, D), lambda i: (i, 0)),
                  pl.BlockSpec((1, D), lambda i: (0, 0))],
        out_specs=pl.BlockSpec((tm, D), lambda i: (i, 0)),
        out_shape=jax.ShapeDtypeStruct((T, D), BF16),
        compiler_params=_cparams(("parallel",)),
        name="rmsnorm",
    )(x, gain.reshape(1, D))


def _proj_kernel(*refs, norm, rope, kmean, tn):
    it = iter(refs)
    h_ref, w_ref = next(it), next(it)
    gain_ref = next(it) if norm else None
    bd_ref = next(it) if norm else None
    if rope:
        c_ref, sa_ref, sb_ref = next(it), next(it), next(it)
    o_ref = next(it)
    km_ref = next(it) if kmean else None

    y = _dot(h_ref[...], w_ref[...])
    if norm:
        y2 = y * y
        bd = bd_ref[...]
        cols = []
        for c in range(tn // LANES):
            h1, h2, h3 = _split3(y2[:, c * LANES:(c + 1) * LANES])
            cols.append(_dot(h1, bd) + _dot(h2, bd) + _dot(h3, bd))
        ms = cols[0] if len(cols) == 1 else jnp.concatenate(cols, axis=1)
        y = y * lax.rsqrt(ms + EPS) * gain_ref[...]
    if rope:
        rep = tn // LANES
        tile = (lambda t: t) if rep == 1 else (lambda t: jnp.concatenate([t] * rep, axis=1))
        y = (y * tile(c_ref[...])
             + pltpu.roll(y, tn - ROT_DIM // 2, 1) * tile(sa_ref[...])
             + pltpu.roll(y, ROT_DIM // 2, 1) * tile(sb_ref[...]))
    o_ref[...] = y.astype(o_ref.dtype)
    if kmean:
        km_ref[0] = jnp.mean(y, axis=0, keepdims=True)


def _proj(h, w, *, seq, tn, tm=512, gain=None, rope_tabs=None, kmean=False):
    T, D = h.shape
    N = w.shape[1]
    norm = gain is not None
    rope = rope_tabs is not None
    nseq = seq // tm
    in_specs = [pl.BlockSpec((tm, D), lambda i, j: (i, 0)),
                pl.BlockSpec((D, tn), lambda i, j: (0, j))]
    args = [h, w]
    if norm:
        bd = np.kron(np.eye(LANES // HEAD_DIM), np.ones((HEAD_DIM, HEAD_DIM))) / HEAD_DIM
        in_specs += [pl.BlockSpec((1, tn), lambda i, j: (0, j)),
                     pl.BlockSpec((LANES, LANES), lambda i, j: (0, 0))]
        args += [gain.reshape(1, N).astype(F32), jnp.asarray(bd, BF16)]
    if rope:
        in_specs += [pl.BlockSpec((tm, LANES), lambda i, j: (i % nseq, 0))] * 3
        args += list(rope_tabs)
    out_specs = [pl.BlockSpec((tm, tn), lambda i, j: (i, j))]
    out_shape = [jax.ShapeDtypeStruct((T, N), BF16)]
    if kmean:
        out_specs.append(pl.BlockSpec((1, 1, tn), lambda i, j: (i, 0, j)))
        out_shape.append(jax.ShapeDtypeStruct((T // tm, 1, N), F32))
    res = pl.pallas_call(
        functools.partial(_proj_kernel, norm=norm, rope=rope, kmean=kmean, tn=tn),
        grid=(T // tm, N // tn),
        in_specs=in_specs, out_specs=out_specs, out_shape=out_shape,
        compiler_params=_cparams(("parallel", "parallel")),
        name="proj",
    )(*args)
    return res if kmean else res[0]


def _rope_tables(seq):
    half = ROT_DIM // 2
    inv_freq = jnp.power(ROPE_THETA, -jnp.arange(0, ROT_DIM, 2, dtype=F32) / ROT_DIM)
    ang = jnp.arange(seq, dtype=F32)[:, None] * inv_freq[None, :]
    cos, sin = jnp.cos(ang), jnp.sin(ang)
    one = jnp.ones((seq, HEAD_DIM - ROT_DIM), F32)
    zero = jnp.zeros((seq, HEAD_DIM - ROT_DIM), F32)
    z8 = jnp.zeros((seq, half), F32)
    c = jnp.concatenate([cos, cos, one], axis=1)
    sa = jnp.concatenate([-sin, z8, zero], axis=1)
    sb = jnp.concatenate([z8, sin, zero], axis=1)
    rep = LANES // HEAD_DIM
    return tuple(jnp.concatenate([t] * rep, axis=1) for t in (c, sa, sb))


def _gates_kernel(h_ref, wf_ref, b_ref, tri_ref, o_ref, carry_ref):
    @pl.when(pl.program_id(1) == 0)
    def _():
        carry_ref[...] = jnp.zeros_like(carry_ref)

    z = _dot_nt(wf_ref[...], h_ref[...]) + b_ref[...][:, :1]
    lf = jnp.minimum(z, 0.0) - jnp.log1p(jnp.exp(-jnp.abs(z)))
    tri = tri_ref[...]
    h1, h2, h3 = _split3(lf)
    cs = _dot(h1, tri) + _dot(h2, tri) + _dot(h3, tri) + carry_ref[...][:, :1]
    o_ref[0] = cs
    carry_ref[...] = jnp.broadcast_to(cs[:, -1:], carry_ref.shape)


def _fox_gates(h, wf_t, bias, *, batch, seq, tm=512):
    T, D = h.shape
    nh = wf_t.shape[0]
    nseq = seq // tm
    tri = jnp.asarray(np.triu(np.ones((tm, tm))), BF16)
    return pl.pallas_call(
        _gates_kernel,
        grid=(batch, nseq),
        in_specs=[pl.BlockSpec((tm, D), lambda b, s: (b * nseq + s, 0)),
                  pl.BlockSpec((nh, D), lambda b, s: (0, 0)),
                  pl.BlockSpec((nh, LANES), lambda b, s: (0, 0)),
                  pl.BlockSpec((tm, tm), lambda b, s: (0, 0))],
        out_specs=pl.BlockSpec((1, nh, tm), lambda b, s: (b, 0, s)),
        out_shape=jax.ShapeDtypeStruct((batch, nh, seq), F32),
        scratch_shapes=[pltpu.VMEM((nh, LANES), F32)],
        compiler_params=_cparams(("parallel", "arbitrary")),
        name="fox_gates",
    )(h, wf_t, jnp.broadcast_to(bias.astype(F32)[:, None], (nh, LANES)), tri)


def _lane_tile(x, width):
    rep = width // LANES
    return x if rep == 1 else jnp.concatenate([x] * rep, axis=1)


def _flash_init(m_ref, l_ref, acc_ref):
    m_ref[...] = jnp.full(m_ref.shape, NEG_INF, F32)
    l_ref[...] = jnp.zeros(l_ref.shape, F32)
    acc_ref[...] = jnp.zeros(acc_ref.shape, F32)


def _flash_update(slot, s, v, m_ref, l_ref, acc_ref):
    tk = s.shape[1]
    m_prev = m_ref[slot]
    m_new = jnp.maximum(m_prev, jnp.max(s, axis=1, keepdims=True))
    alpha = jnp.exp(m_prev - m_new)
    p = jnp.exp(s - _lane_tile(m_new, tk))
    psum = p[:, :LANES]
    for c in range(1, tk // LANES):
        psum = psum + p[:, c * LANES:(c + 1) * LANES]
    l_ref[slot] = alpha * l_ref[slot] + psum
    acc_ref[slot] = alpha * acc_ref[slot] + _dot(p.astype(BF16), v)
    m_ref[slot] = m_new


def _flash_finish(lane, l_ref, acc_ref):
    outs = [acc_ref[hh] / jnp.sum(l_ref[hh], axis=1, keepdims=True) for hh in range(2)]
    return jnp.where(lane < HEAD_DIM, outs[0], outs[1])


def _head_queries(q, lane):
    qs = q * ATTN_SCALE
    return [jnp.where((lane >= hh * HEAD_DIM) & (lane < (hh + 1) * HEAD_DIM), qs, jnp.zeros_like(qs))
            for hh in range(2)]


def _moba_kernel(q_ref, k_ref, v_ref, km_ref, o_ref, m_ref, l_ref, acc_ref):
    tq = q_ref.shape[1]
    tk = tq
    qi = pl.program_id(2)
    lane = lax.broadcasted_iota(jnp.int32, (tq, LANES), 1)
    lane_f = lane.astype(F32)
    rowv = lax.broadcasted_iota(jnp.int32, (tq, LANES), 0)
    row_blk = 2 * qi + (rowv >= MOBA_BLOCK).astype(jnp.int32)
    row = lax.broadcasted_iota(jnp.int32, (tq, tk), 0)
    col = lax.broadcasted_iota(jnp.int32, (tq, tk), 1)
    qh = _head_queries(q_ref[0], lane)
    _flash_init(m_ref, l_ref, acc_ref)

    sels = []
    for hh in range(2):
        gate = _dot_nt(qh[hh], km_ref[0])
        gate = jnp.where(lane < row_blk, gate, -jnp.inf)
        sel = jnp.zeros((tq, LANES), F32)
        for _ in range(MOBA_TOPK):
            m = jnp.max(gate, axis=1, keepdims=True)
            idx = jnp.min(jnp.where(gate == m, lane_f, float(LANES)), axis=1, keepdims=True)
            hit = lane_f == idx
            sel = jnp.where(hit & (m > -jnp.inf), 1.0, sel)
            gate = jnp.where(hit, -jnp.inf, gate)
        sels.append(sel)

    def chosen(sel, blk):
        return jnp.max(jnp.where(lane == blk, sel, 0.0), axis=1, keepdims=True) > 0.0

    start = pl.multiple_of(qi * tq, tq)
    k_d = k_ref[0, pl.ds(start, tk), :]
    v_d = v_ref[0, pl.ds(start, tk), :]
    for hh in range(2):
        visible = (col >= MOBA_BLOCK) | (row < MOBA_BLOCK) | chosen(sels[hh], 2 * qi)
        s = jnp.where((col <= row) & visible, _dot_nt(qh[hh], k_d), NEG_INF)
        _flash_update(hh, s, v_d, m_ref, l_ref, acc_ref)

    def body(j, carry):
        off = pl.multiple_of(j * tk, tk)
        kj = k_ref[0, pl.ds(off, tk), :]
        vj = v_ref[0, pl.ds(off, tk), :]
        for hh in range(2):
            s = _dot_nt(qh[hh], kj)
            s = jnp.concatenate(
                [jnp.where(chosen(sels[hh], 2 * j), s[:, :MOBA_BLOCK], NEG_INF),
                 jnp.where(chosen(sels[hh], 2 * j + 1), s[:, MOBA_BLOCK:], NEG_INF)], axis=1)
            _flash_update(hh, s, vj, m_ref, l_ref, acc_ref)
        return carry

    lax.fori_loop(0, qi, body, 0)
    o_ref[0] = _flash_finish(lane, l_ref, acc_ref).astype(o_ref.dtype)


def _flash_scratch(tq):
    return [pltpu.VMEM((2, tq, LANES), F32)] * 3


def _moba(q, k, v, kmean, *, batch, seq):
    W = q.shape[1]
    tq = 2 * MOBA_BLOCK
    q3, k3, v3 = (t.reshape(batch, seq, W) for t in (q, k, v))
    out = pl.pallas_call(
        _moba_kernel,
        grid=(batch, W // PAIR_W, seq // tq),
        in_specs=[pl.BlockSpec((1, tq, PAIR_W), lambda b, p, i: (b, i, p)),
                  pl.BlockSpec((1, seq, PAIR_W), lambda b, p, i: (b, 0, p)),
                  pl.BlockSpec((1, seq, PAIR_W), lambda b, p, i: (b, 0, p)),
                  pl.BlockSpec((1, LANES, PAIR_W), lambda b, p, i: (b, 0, p))],
        out_specs=pl.BlockSpec((1, tq, PAIR_W), lambda b, p, i: (b, i, p)),
        out_shape=jax.ShapeDtypeStruct((batch, seq, W), BF16),
        scratch_shapes=_flash_scratch(tq),
        compiler_params=_cparams(("parallel", "parallel", "parallel")),
        name="moba",
    )(q3, k3, v3, kmean)
    return out.reshape(batch * seq, W)


def _fox_kernel(q_ref, k_ref, v_ref, g_ref, fcol_ref, frow_ref, o_ref, m_ref, l_ref, acc_ref):
    tq = q_ref.shape[1]
    tk = tq
    pr = pl.program_id(1)
    qi = pl.program_id(2)
    lane = lax.broadcasted_iota(jnp.int32, (tq, LANES), 1)
    row = lax.broadcasted_iota(jnp.int32, (tq, tk), 0)
    col = lax.broadcasted_iota(jnp.int32, (tq, tk), 1)
    qh = _head_queries(q_ref[0], lane)
    fcol = fcol_ref[0]
    fq = [jnp.sum(jnp.where(lane == 2 * pr + hh, fcol, 0.0), axis=1, keepdims=True) for hh in range(2)]
    _flash_init(m_ref, l_ref, acc_ref)

    def tile(off, diagonal):
        kj = k_ref[0, pl.ds(off, tk), :]
        vj = v_ref[0, pl.ds(off, tk), :]
        fk_all = frow_ref[0, :, pl.ds(off, tk)]
        sub = lax.broadcasted_iota(jnp.int32, fk_all.shape, 0)
        for hh in range(2):
            fk = jnp.sum(jnp.where(sub == 2 * pr + hh, fk_all, 0.0), axis=0, keepdims=True)
            s = _dot_nt(qh[hh], kj) + fq[hh] - fk
            if diagonal:
                s = jnp.where(col <= row, s, NEG_INF)
            _flash_update(hh, s, vj, m_ref, l_ref, acc_ref)

    tile(pl.multiple_of(qi * tq, tq), True)

    def body(j, carry):
        tile(pl.multiple_of(j * tk, tk), False)
        return carry

    lax.fori_loop(0, qi, body, 0)
    o = _flash_finish(lane, l_ref, acc_ref)
    o_ref[0] = (o * jax.nn.sigmoid(g_ref[0].astype(F32))).astype(o_ref.dtype)


def _fox(q, k, v, g, fcol, frow, *, batch, seq, tq=512):
    W = q.shape[1]
    nh = frow.shape[1]
    q3, k3, v3, g3 = (t.reshape(batch, seq, W) for t in (q, k, v, g))
    out = pl.pallas_call(
        _fox_kernel,
        grid=(batch, W // PAIR_W, seq // tq),
        in_specs=[pl.BlockSpec((1, tq, PAIR_W), lambda b, p, i: (b, i, p)),
                  pl.BlockSpec((1, seq, PAIR_W), lambda b, p, i: (b, 0, p)),
                  pl.BlockSpec((1, seq, PAIR_W), lambda b, p, i: (b, 0, p)),
                  pl.BlockSpec((1, tq, PAIR_W), lambda b, p, i: (b, i, p)),
                  pl.BlockSpec((1, tq, LANES), lambda b, p, i: (b, i, 0)),
                  pl.BlockSpec((1, nh, seq), lambda b, p, i: (b, 0, 0))],
        out_specs=pl.BlockSpec((1, tq, PAIR_W), lambda b, p, i: (b, i, p)),
        out_shape=jax.ShapeDtypeStruct((batch, seq, W), BF16),
        scratch_shapes=_flash_scratch(tq),
        compiler_params=_cparams(("parallel", "parallel", "parallel")),
        name="fox",
    )(q3, k3, v3, g3, fcol, frow)
    return out.reshape(batch * seq, W)


def _swa_kernel(q_ref, k_ref, v_ref, sink_ref, o_ref):
    tq = q_ref.shape[1]
    qi = pl.program_id(1)
    group = SWA_Q_HEADS // SWA_KV_HEADS
    rows = group * tq
    tk = tq + SWA_WINDOW
    lane = lax.broadcasted_iota(jnp.int32, (tq, LANES), 1)
    kstart = pl.multiple_of(jnp.maximum(qi * tq - SWA_WINDOW, 0), SWA_WINDOW)
    k = k_ref[0, pl.ds(kstart, tk), :]
    v = v_ref[0, pl.ds(kstart, tk), :]
    qpos = qi * tq + (lax.broadcasted_iota(jnp.int32, (rows, tk), 0) & (tq - 1))
    kpos = kstart + lax.broadcasted_iota(jnp.int32, (rows, tk), 1)
    dist = qpos - kpos
    allowed = (dist >= 0) & (dist < SWA_WINDOW)
    sink_tab = sink_ref[...]
    outs = [None] * SWA_Q_HEADS
    for c in range(SWA_KV_HEADS):
        pieces, sinks = [], []
        for g in range(group):
            head = c * group + g
            blk = q_ref[0, :, (head // 2) * PAIR_W:(head // 2 + 1) * PAIR_W] * ATTN_SCALE
            hh = head % 2
            qm = jnp.where((lane >= hh * HEAD_DIM) & (lane < (hh + 1) * HEAD_DIM), blk, jnp.zeros_like(blk))
            if hh != c:
                qm = pltpu.roll(qm.astype(F32), HEAD_DIM, 1).astype(BF16)
            pieces.append(qm)
            sinks.append(jnp.broadcast_to(sink_tab[head:head + 1, :1], (tq, 1)))
        qs = jnp.concatenate(pieces, axis=0)
        sink = jnp.concatenate(sinks, axis=0)
        s = jnp.where(allowed, _dot_nt(qs, k), NEG_INF)
        m = jnp.maximum(jnp.max(s, axis=1, keepdims=True), sink)
        p = jnp.exp(s - m)
        l = jnp.sum(p, axis=1, keepdims=True) + jnp.exp(sink - m)
        o = _dot(p.astype(BF16), v) / l
        for g in range(group):
            head = c * group + g
            oh = o[g * tq:(g + 1) * tq]
            outs[head] = oh if head % 2 == c else pltpu.roll(oh, HEAD_DIM, 1)
    for pp in range(SWA_Q_HEADS // 2):
        o_ref[0, :, pp * PAIR_W:(pp + 1) * PAIR_W] = jnp.where(
            lane < HEAD_DIM, outs[2 * pp], outs[2 * pp + 1]).astype(o_ref.dtype)


def _swa(q, k, v, sinks, *, batch, seq, tq=SWA_WINDOW):
    W = q.shape[1]
    assert tq & (tq - 1) == 0
    q3 = q.reshape(batch, seq, W)
    k3, v3 = (t.reshape(batch, seq, PAIR_W) for t in (k, v))
    sink_tab = jnp.broadcast_to(sinks.astype(F32)[:, None], (SWA_Q_HEADS, LANES))
    out = pl.pallas_call(
        _swa_kernel,
        grid=(batch, seq // tq),
        in_specs=[pl.BlockSpec((1, tq, W), lambda b, i: (b, i, 0)),
                  pl.BlockSpec((1, seq, PAIR_W), lambda b, i: (b, 0, 0)),
                  pl.BlockSpec((1, seq, PAIR_W), lambda b, i: (b, 0, 0)),
                  pl.BlockSpec((SWA_Q_HEADS, LANES), lambda b, i: (0, 0))],
        out_specs=pl.BlockSpec((1, tq, W), lambda b, i: (b, i, 0)),
        out_shape=jax.ShapeDtypeStruct((batch, seq, W), BF16),
        compiler_params=_cparams(("parallel", "parallel")),
        name="swa",
    )(q3, k3, v3, sink_tab)
    return out.reshape(batch * seq, W)


def _outproj_kernel(*refs, n_parts):
    parts = refs[:n_parts]
    w_ref, x_ref, o_ref = refs[n_parts:]
    y = x_ref[...]
    off = 0
    for p_ref in parts:
        kw = p_ref.shape[1]
        y = y + _dot(p_ref[...], w_ref[off:off + kw, :])
        off += kw
    o_ref[...] = y


def _outproj(parts, w, x, tm=512):
    T, D = x.shape
    in_specs = [pl.BlockSpec((tm, p.shape[1]), lambda i: (i, 0)) for p in parts]
    in_specs += [pl.BlockSpec(w.shape, lambda i: (0, 0)),
                 pl.BlockSpec((tm, D), lambda i: (i, 0))]
    return pl.pallas_call(
        functools.partial(_outproj_kernel, n_parts=len(parts)),
        grid=(T // tm,),
        in_specs=in_specs,
        out_specs=pl.BlockSpec((tm, D), lambda i: (i, 0)),
        out_shape=jax.ShapeDtypeStruct((T, D), F32),
        compiler_params=_cparams(("parallel",)),
        name="outproj",
    )(*parts, w, x)


_CAND_ROWS = 80


def _cand_tables():
    pos = np.zeros((_CAND_ROWS,), np.float32)
    neg = np.zeros((_CAND_ROWS,), np.float32)
    r = 0
    for a, nb in ((0, 16), (1, 8), (2, 8), (3, 8), (4, 8), (5, 8), (6, 8), (7, 8)):
        for b in range(nb):
            pos[r] = a * PEER_TOPK + b
            neg[r] = 0.0 if (a + 1) * (b + 1) <= PEER_TOPK else -np.inf
            r += 1
    for a in range(8, 16):
        pos[r] = a * PEER_TOPK
        r += 1
    assert r == _CAND_ROWS
    tab = lambda t: jnp.asarray(np.broadcast_to(t[:, None], (_CAND_ROWS, LANES)).copy())
    return tab(pos), tab(neg)


def _extract_top(v, nk):
    kio = lax.broadcasted_iota(jnp.int32, v.shape, 0).astype(F32)
    slot = lax.broadcasted_iota(jnp.int32, (PEER_TOPK, v.shape[1]), 0)

    def body(a, carry):
        v, rank, vals = carry
        m = jnp.max(v, axis=0, keepdims=True)
        idx = jnp.min(jnp.where(v == m, kio, float(nk)), axis=0, keepdims=True)
        hit = kio == idx
        return (jnp.where(hit, -jnp.inf, v), jnp.where(hit, a.astype(F32), rank),
                jnp.where(slot == a, m, vals))

    init = (v, jnp.full(v.shape, float(PEER_TOPK), F32), jnp.zeros((PEER_TOPK, v.shape[1]), F32))
    _, rank, vals = lax.fori_loop(0, PEER_TOPK, body, init)
    return vals, rank


def _route_kernel(h_ref, wq_ref, keys_ref, pos_ref, neg_ref,
                  c1_ref, e1_ref, r2_ref, e2_ref, qt_ref):
    half = PEER_QUERY_DIM // 2
    qt_ref[...] = _dot_nt(wq_ref[...], h_ref[...]).astype(BF16)
    pos = pos_ref[...]
    neg = neg_ref[...]
    slot = lax.broadcasted_iota(jnp.int32, (PEER_TOPK, h_ref.shape[0]), 0)

    def head_body(h, _):
        r0 = pl.multiple_of(h * PEER_QUERY_DIM, PEER_QUERY_DIM)
        s1 = _dot(keys_ref[2 * h], qt_ref[pl.ds(r0, half), :])
        s2 = _dot(keys_ref[2 * h + 1], qt_ref[pl.ds(r0 + half, half), :])
        v1, rank1 = _extract_top(s1, PEER_N_KEYS)
        v2, rank2 = _extract_top(s2, PEER_N_KEYS)
        blocks = [v1[0:1] + v2[0:8], v1[0:1] + v2[8:16]]
        blocks += [v1[a:a + 1] + v2[0:8] for a in range(1, 8)]
        blocks += [v1[8:16] + v2[0:1]]
        cand = jnp.concatenate(blocks, axis=0) + neg

        def pick(kk, carry):
            cand, chosen, ts = carry
            m = jnp.max(cand, axis=0, keepdims=True)
            first = jnp.min(jnp.where(cand == m, pos, 1e9), axis=0, keepdims=True)
            hit = pos == first
            return (jnp.where(hit, -jnp.inf, cand), jnp.where(hit, 1.0, chosen),
                    jnp.where(slot == kk, m, ts))

        _, chosen, ts = lax.fori_loop(0, PEER_TOPK, pick,
                                      (cand, jnp.zeros_like(cand), jnp.zeros_like(v1)))
        z =jnp.sum(jnp.exp(ts - ts[0:1]), axis=0, keepdims=True)
        counts = [jnp.sum(chosen[0:16], axis=0, keepdims=True)]
        counts += [jnp.sum(chosen[8 * a + 8:8 * a + 16], axis=0, keepdims=True) for a in range(1, 8)]
        counts += [chosen[72 + a:73 + a] for a in range(8)]
        c1 = jnp.zeros_like(rank1)
        for a in range(PEER_TOPK):
            c1 = jnp.where(rank1 == float(a), counts[a], c1)
        c1_ref[h] = c1
        e1_ref[h] = jnp.exp(s1 - v1[0:1]) / z
        r2_ref[h] = rank2
        e2_ref[h] = jnp.exp(s2 - v2[0:1])
        return 0

    lax.fori_loop(0, PEER_HEADS, head_body, 0)


def _peer_route(h2, wq_t, keys, tt=LANES):
    T, D = h2.shape
    pos, neg = _cand_tables()
    stat = jax.ShapeDtypeStruct((PEER_HEADS, PEER_N_KEYS, T), F32)
    stat_spec = pl.BlockSpec((PEER_HEADS, PEER_N_KEYS, tt), lambda i: (0, 0, i))
    return pl.pallas_call(
        _route_kernel,
        grid=(T // tt,),
        in_specs=[pl.BlockSpec((tt, D), lambda i: (i, 0)),
                  pl.BlockSpec(wq_t.shape, lambda i: (0, 0)),
                  pl.BlockSpec(keys.shape, lambda i: (0, 0, 0)),
                  pl.BlockSpec((_CAND_ROWS, LANES), lambda i: (0, 0)),
                  pl.BlockSpec((_CAND_ROWS, LANES), lambda i: (0, 0))],
        out_specs=[stat_spec] * 4,
        out_shape=[stat] * 4,
        scratch_shapes=[pltpu.VMEM((PEER_HEADS * PEER_QUERY_DIM, tt), BF16)],
        compiler_params=_cparams(("parallel",)),
        name="peer_route",
    )(h2, wq_t, keys, pos, neg)


_CHUNK_ROWS = 32


def _experts_kernel(h_ref, dn_ref, upt_ref, c1_ref, e1_ref, r2_ref, e2_ref, x_ref,
                    o_ref, a_ref, p_ref, acc_ref):
    te, tt = a_ref.shape
    eb = pl.program_id(1)
    groups = te // PEER_N_KEYS

    @pl.when(eb == 0)
    def _():
        acc_ref[...] = jnp.zeros_like(acc_ref)

    a_ref[...] = _dot_nt(dn_ref[...], h_ref[...])

    n_lane = tt // LANES
    n_row = PEER_N_KEYS // _CHUNK_ROWS
    key0 = pl.multiple_of(eb * groups, groups)

    def chunk(c, _):
        tl = pl.multiple_of((c % n_lane) * LANES, LANES)
        jr = pl.multiple_of((c // n_lane) * _CHUNK_ROWS, _CHUNK_ROWS)
        w = [jnp.zeros((_CHUNK_ROWS, LANES), F32) for _ in range(groups)]
        for h in range(PEER_HEADS):
            c1 = c1_ref[h, pl.ds(key0, groups), pl.ds(tl, LANES)]
            e1 = e1_ref[h, pl.ds(key0, groups), pl.ds(tl, LANES)]
            r2 = r2_ref[h, pl.ds(jr, _CHUNK_ROWS), pl.ds(tl, LANES)]
            e2 = e2_ref[h, pl.ds(jr, _CHUNK_ROWS), pl.ds(tl, LANES)]
            for g in range(groups):
                w[g] = w[g] + jnp.where(r2 < c1[g:g + 1], e2, 0.0) * e1[g:g + 1]
        for g in range(groups):
            a = a_ref[pl.ds(g * PEER_N_KEYS + jr, _CHUNK_ROWS), pl.ds(tl, LANES)]
            gelu = 0.5 * a * (1.0 + lax.erf(a * (2.0 ** -0.5)))
            p_ref[pl.ds(g * PEER_N_KEYS + jr, _CHUNK_ROWS), pl.ds(tl, LANES)] = (w[g] * gelu).astype(BF16)
        return 0

    lax.fori_loop(0, n_lane * n_row, chunk, 0)

    acc_ref[...] += _dot(upt_ref[...], p_ref[...])

    @pl.when(eb == pl.num_programs(1) - 1)
    def _():
        o_ref[...] = x_ref[...] + acc_ref[...].T


def _peer_experts(h2, down, up_t, stats, x, tt=512, te=8 * PEER_N_KEYS):
    T, D = h2.shape
    E = down.shape[0]
    assert te // PEER_N_KEYS == 8
    stat_spec = pl.BlockSpec((PEER_HEADS, PEER_N_KEYS, tt), lambda i, e: (0, 0, i))
    return pl.pallas_call(
        _experts_kernel,
        grid=(T // tt, E // te),
        in_specs=[pl.BlockSpec((tt, D), lambda i, e: (i, 0)),
                  pl.BlockSpec((te, D), lambda i, e: (e, 0)),
                  pl.BlockSpec((D, te), lambda i, e: (0, e)),
                  stat_spec, stat_spec, stat_spec, stat_spec,
                  pl.BlockSpec((tt, D), lambda i, e: (i, 0))],
        out_specs=pl.BlockSpec((tt, D), lambda i, e: (i, 0)),
        out_shape=jax.ShapeDtypeStruct((T, D), F32),
        scratch_shapes=[pltpu.VMEM((te, tt), F32),
                        pltpu.VMEM((te, tt), BF16),
                        pltpu.VMEM((D, tt), F32)],
        compiler_params=_cparams(("parallel", "arbitrary")),
        name="peer_experts",
    )(h2, down, up_t, *stats, x)


def _peer_layer(x, norm_gain, w_query, sub_keys, down, up):
    h2 = _rmsnorm(x, norm_gain)
    keys = sub_keys.reshape(PEER_HEADS * 2, PEER_N_KEYS, PEER_QUERY_DIM // 2).astype(BF16)
    stats = _peer_route(h2, w_query.T.astype(BF16), keys)
    return _peer_experts(h2, down.astype(BF16), up.T.astype(BF16), stats, x)


def _tile_heads(g, n):
    return jnp.tile(g.astype(F32), n)


def _even_mixer(x, norm_gain, w_in, f_bias, qn_a, kn_a, qn_b, kn_b, w_out, tabs, *, batch, seq):
    h = _rmsnorm(x, norm_gain)
    w = w_in.astype(BF16)
    o_qa, o_ka, o_va, o_qb, o_kb, o_vb, o_gb, o_fb = (
        0, A_W, 2 * A_W, 3 * A_W, 3 * A_W + B_W, 3 * A_W + 2 * B_W, 3 * A_W + 3 * B_W, 3 * A_W + 4 * B_W)
    gain_a = jnp.concatenate([_tile_heads(qn_a, MOBA_HEADS), _tile_heads(kn_a, MOBA_HEADS)])
    qk_a, km = _proj(h, w[:, o_qa:o_va], seq=seq, tn=A_W, tm=MOBA_BLOCK, gain=gain_a,
                     rope_tabs=tabs, kmean=True)
    nb = seq // MOBA_BLOCK
    kmean = km.reshape(batch, nb, 2 * A_W)[:, :, A_W:]
    kmean = jnp.pad(kmean, ((0, 0), (0, LANES - nb), (0, 0))).astype(BF16)
    gain_b = jnp.concatenate([_tile_heads(qn_b, FOX_HEADS), _tile_heads(kn_b, FOX_HEADS)])
    qk_b = _proj(h, w[:, o_qb:o_vb], seq=seq, tn=B_W, gain=gain_b)
    w_plain = jnp.concatenate([w[:, o_va:o_qb], w[:, o_vb:o_fb]], axis=1)
    vvg = _proj(h, w_plain, seq=seq, tn=A_W)
    va, vb, gb = vvg[:, :A_W], vvg[:, A_W:A_W + B_W], vvg[:, A_W + B_W:]
    frow = _fox_gates(h, w[:, o_fb:].T, f_bias, batch=batch, seq=seq)
    fcol = jnp.pad(frow.transpose(0, 2, 1), ((0, 0), (0, 0), (0, LANES - FOX_HEADS)))
    oa = _moba(qk_a[:, :A_W], qk_a[:, A_W:], va, kmean, batch=batch, seq=seq)
    ob = _fox(qk_b[:, :B_W], qk_b[:, B_W:], vb, gb, fcol, frow, batch=batch, seq=seq)
    return _outproj([oa, ob], w_out.astype(BF16), x)


def _odd_mixer(x, norm_gain, w_in, qn, kn, sinks, w_out, tabs, *, batch, seq):
    h = _rmsnorm(x, norm_gain)
    w = w_in.astype(BF16)
    qw = SWA_Q_HEADS * HEAD_DIM
    kw = SWA_KV_HEADS * HEAD_DIM
    q = _proj(h, w[:, :qw], seq=seq, tn=512, gain=_tile_heads(qn, SWA_Q_HEADS), rope_tabs=tabs)
    k = _proj(h, w[:, qw:qw + kw], seq=seq, tn=kw, gain=_tile_heads(kn, SWA_KV_HEADS), rope_tabs=tabs)
    v = _proj(h, w[:, qw + kw:], seq=seq, tn=kw)
    o = _swa(q, k, v, sinks, batch=batch, seq=seq)
    return _outproj([o], w_out.astype(BF16), x)


def kernel(x, attn_norm, ffn_norm, ev_w_in, ev_forget_bias, ev_q_norm_a, ev_k_norm_a, ev_q_norm_b,
           ev_k_norm_b, ev_w_out, od_w_in, od_q_norm, od_k_norm, od_sinks, od_w_out,
           peer_w_query, peer_sub_keys, peer_down, peer_up):
    batch, seq, d_model = x.shape
    depth = attn_norm.shape[0]
    tabs = _rope_tables(seq)
    xt = x.reshape(batch * seq, d_model)
    for l in range(depth):
        i = l // 2
        if l % 2 == 0:
            xt = _even_mixer(xt, attn_norm[l], ev_w_in[i], ev_forget_bias[i], ev_q_norm_a[i],
                             ev_k_norm_a[i], ev_q_norm_b[i], ev_k_norm_b[i], ev_w_out[i], tabs,
                             batch=batch, seq=seq)
        else:
            xt = _odd_mixer(xt, attn_norm[l], od_w_in[i], od_q_norm[i], od_k_norm[i], od_sinks[i],
                            od_w_out[i], tabs, batch=batch, seq=seq)
        xt = _peer_layer(xt, ffn_norm[l], peer_w_query[l], peer_sub_keys[l], peer_down[l], peer_up[l])
    return xt.reshape(batch, seq, d_model)
```

```python
import functools

import numpy as np
import jax
import jax.numpy as jnp
from jax import lax
from jax.experimental import pallas as pl
from jax.experimental.pallas import tpu as pltpu

F32 = jnp.float32
BF16 = jnp.bfloat16

HEAD_DIM = 64
ROT_DIM = HEAD_DIM // 4
ROPE_THETA = 500000.0
ATTN_SCALE = HEAD_DIM ** -0.5
EPS = 1e-6
NEG_INF = -1e30

MOBA_HEADS = 8
FOX_HEADS = 8
MOBA_BLOCK = 256
MOBA_TOPK = 3
A_W = MOBA_HEADS * HEAD_DIM
B_W = FOX_HEADS * HEAD_DIM

SWA_Q_HEADS = 16
SWA_KV_HEADS = 2
SWA_WINDOW = 128

PEER_HEADS = 8
PEER_N_KEYS = 128
PEER_TOPK = 16
PEER_QUERY_DIM = 128

LANES = 128
PAIR_W = 2 * HEAD_DIM
VMEM_LIMIT = 48 * 1024 * 1024


def _cparams(sem):
    return pltpu.CompilerParams(dimension_semantics=sem, vmem_limit_bytes=VMEM_LIMIT)


def _dot_nt(a, b):
    return lax.dot_general(a, b, (((1,), (1,)), ((), ())), preferred_element_type=F32)


def _dot(a, b):
    return jnp.dot(a, b, preferred_element_type=F32)


def _split3(x):
    h1 = x.astype(BF16)
    r1 = x - h1.astype(F32)
    h2 = r1.astype(BF16)
    h3 = (r1 - h2.astype(F32)).astype(BF16)
    return h1, h2, h3


def _rmsnorm_kernel(x_ref, g_ref, o_ref):
    x = x_ref[...]
    ms = jnp.mean(x * x, axis=-1, keepdims=True)
    o_ref[...] = (x * lax.rsqrt(ms + EPS) * g_ref[...]).astype(o_ref.dtype)


def _rmsnorm(x, gain, tm=512):
    T, D = x.shape
    return pl.pallas_call(
        _rmsnorm_kernel,
        grid=(T // tm,),
        in_specs=[pl.BlockSpec((tm, D), lambda i: (i, 0)),
                  pl.BlockSpec((1, D), lambda i: (0, 0))],
        out_specs=pl.BlockSpec((tm, D), lambda i: (i, 0)),
        out_shape=jax.ShapeDtypeStruct((T, D), BF16),
        compiler_params=_cparams(("parallel",)),
        name="rmsnorm",
    )(x, gain.reshape(1, D))


def _proj_kernel(*refs, norm, rope, kmean, tn):
    it = iter(refs)
    h_ref, w_ref = next(it), next(it)
    gain_ref = next(it) if norm else None
    bd_ref = next(it) if norm else None
    if rope:
        c_ref, sa_ref, sb_ref = next(it), next(it), next(it)
    o_ref = next(it)
    km_ref = next(it) if kmean else None

    y = _dot(h_ref[...], w_ref[...])
    if norm:
        y2 = y * y
        bd = bd_ref[...]
        cols = []
        for c in range(tn // LANES):
            h1, h2, h3 = _split3(y2[:, c * LANES:(c + 1) * LANES])
            cols.append(_dot(h1, bd) + _dot(h2, bd) + _dot(h3, bd))
        ms = cols[0] if len(cols) == 1 else jnp.concatenate(cols, axis=1)
        y = y * lax.rsqrt(ms + EPS) * gain_ref[...]
    if rope:
        rep = tn // LANES
        tile = (lambda t: t) if rep == 1 else (lambda t: jnp.concatenate([t] * rep, axis=1))
        y = (y * tile(c_ref[...])
             + pltpu.roll(y, tn - ROT_DIM // 2, 1) * tile(sa_ref[...])
             + pltpu.roll(y, ROT_DIM // 2, 1) * tile(sb_ref[...]))
    o_ref[...] = y.astype(o_ref.dtype)
    if kmean:
        km_ref[0] = jnp.mean(y, axis=0, keepdims=True)


def _proj(h, w, *, seq, tn, tm=512, gain=None, rope_tabs=None, kmean=False):
    T, D = h.shape
    N = w.shape[1]
    norm = gain is not None
    rope = rope_tabs is not None
    nseq = seq // tm
    in_specs = [pl.BlockSpec((tm, D), lambda i, j: (i, 0)),
                pl.BlockSpec((D, tn), lambda i, j: (0, j))]
    args = [h, w]
    if norm:
        bd = np.kron(np.eye(LANES // HEAD_DIM), np.ones((HEAD_DIM, HEAD_DIM))) / HEAD_DIM
        in_specs += [pl.BlockSpec((1, tn), lambda i, j: (0, j)),
                     pl.BlockSpec((LANES, LANES), lambda i, j: (0, 0))]
        args += [gain.reshape(1, N).astype(F32), jnp.asarray(bd, BF16)]
    if rope:
        in_specs += [pl.BlockSpec((tm, LANES), lambda i, j: (i % nseq, 0))] * 3
        args += list(rope_tabs)
    out_specs = [pl.BlockSpec((tm, tn), lambda i, j: (i, j))]
    out_shape = [jax.ShapeDtypeStruct((T, N), BF16)]
    if kmean:
        out_specs.append(pl.BlockSpec((1, 1, tn), lambda i, j: (i, 0, j)))
        out_shape.append(jax.ShapeDtypeStruct((T // tm, 1, N), F32))
    res = pl.pallas_call(
        functools.partial(_proj_kernel, norm=norm, rope=rope, kmean=kmean, tn=tn),
        grid=(T // tm, N // tn),
        in_specs=in_specs, out_specs=out_specs, out_shape=out_shape,
        compiler_params=_cparams(("parallel", "parallel")),
        name="proj",
    )(*args)
    return res if kmean else res[0]


def _rope_tables(seq):
    half = ROT_DIM // 2
    inv_freq = jnp.power(ROPE_THETA, -jnp.arange(0, ROT_DIM, 2, dtype=F32) / ROT_DIM)
    ang = jnp.arange(seq, dtype=F32)[:, None] * inv_freq[None, :]
    cos, sin = jnp.cos(ang), jnp.sin(ang)
    one = jnp.ones((seq, HEAD_DIM - ROT_DIM), F32)
    zero = jnp.zeros((seq, HEAD_DIM - ROT_DIM), F32)
    z8 = jnp.zeros((seq, half), F32)
    c = jnp.concatenate([cos, cos, one], axis=1)
    sa = jnp.concatenate([-sin, z8, zero], axis=1)
    sb = jnp.concatenate([z8, sin, zero], axis=1)
    rep = LANES // HEAD_DIM
    return tuple(jnp.concatenate([t] * rep, axis=1) for t in (c, sa, sb))


def _gates_kernel(h_ref, wf_ref, b_ref, tri_ref, o_ref, carry_ref):
    @pl.when(pl.program_id(1) == 0)
    def _():
        carry_ref[...] = jnp.zeros_like(carry_ref)

    z = _dot_nt(wf_ref[...], h_ref[...]) + b_ref[...][:, :1]
    lf = jnp.minimum(z, 0.0) - jnp.log1p(jnp.exp(-jnp.abs(z)))
    tri = tri_ref[...]
    h1, h2, h3 = _split3(lf)
    cs = _dot(h1, tri) + _dot(h2, tri) + _dot(h3, tri) + carry_ref[...][:, :1]
    o_ref[0] = cs
    carry_ref[...] = jnp.broadcast_to(cs[:, -1:], carry_ref.shape)


def _fox_gates(h, wf_t, bias, *, batch, seq, tm=512):
    T, D = h.shape
    nh = wf_t.shape[0]
    nseq = seq // tm
    tri = jnp.asarray(np.triu(np.ones((tm, tm))), BF16)
    return pl.pallas_call(
        _gates_kernel,
        grid=(batch, nseq),
        in_specs=[pl.BlockSpec((tm, D), lambda b, s: (b * nseq + s, 0)),
                  pl.BlockSpec((nh, D), lambda b, s: (0, 0)),
                  pl.BlockSpec((nh, LANES), lambda b, s: (0, 0)),
                  pl.BlockSpec((tm, tm), lambda b, s: (0, 0))],
        out_specs=pl.BlockSpec((1, nh, tm), lambda b, s: (b, 0, s)),
        out_shape=jax.ShapeDtypeStruct((batch, nh, seq), F32),
        scratch_shapes=[pltpu.VMEM((nh, LANES), F32)],
        compiler_params=_cparams(("parallel", "arbitrary")),
        name="fox_gates",
    )(h, wf_t, jnp.broadcast_to(bias.astype(F32)[:, None], (nh, LANES)), tri)


def _lane_tile(x, width):
    rep = width // LANES
    return x if rep == 1 else jnp.concatenate([x] * rep, axis=1)


def _flash_init(m_ref, l_ref, acc_ref):
    m_ref[...] = jnp.full(m_ref.shape, NEG_INF, F32)
    l_ref[...] = jnp.zeros(l_ref.shape, F32)
    acc_ref[...] = jnp.zeros(acc_ref.shape, F32)


def _flash_update(slot, s, v, m_ref, l_ref, acc_ref):
    tk = s.shape[1]
    m_prev = m_ref[slot]
    m_new = jnp.maximum(m_prev, jnp.max(s, axis=1, keepdims=True))
    alpha = jnp.exp(m_prev - m_new)
    p = jnp.exp(s - _lane_tile(m_new, tk))
    psum = p[:, :LANES]
    for c in range(1, tk // LANES):
        psum = psum + p[:, c * LANES:(c + 1) * LANES]
    l_ref[slot] = alpha * l_ref[slot] + psum
    acc_ref[slot] = alpha * acc_ref[slot] + _dot(p.astype(BF16), v)
    m_ref[slot] = m_new


def _flash_finish(lane, l_ref, acc_ref):
    outs = [acc_ref[hh] / jnp.sum(l_ref[hh], axis=1, keepdims=True) for hh in range(2)]
    return jnp.where(lane < HEAD_DIM, outs[0], outs[1])


def _head_queries(q, lane):
    qs = q * ATTN_SCALE
    return [jnp.where((lane >= hh * HEAD_DIM) & (lane < (hh + 1) * HEAD_DIM), qs, jnp.zeros_like(qs))
            for hh in range(2)]


def _moba_kernel(q_ref, k_ref, v_ref, km_ref, o_ref, m_ref, l_ref, acc_ref):
    tq = q_ref.shape[1]
    tk = tq
    qi = pl.program_id(2)
    lane = lax.broadcasted_iota(jnp.int32, (tq, LANES), 1)
    lane_f = lane.astype(F32)
    rowv = lax.broadcasted_iota(jnp.int32, (tq, LANES), 0)
    row_blk = 2 * qi + (rowv >= MOBA_BLOCK).astype(jnp.int32)
    row = lax.broadcasted_iota(jnp.int32, (tq, tk), 0)
    col = lax.broadcasted_iota(jnp.int32, (tq, tk), 1)
    qh = _head_queries(q_ref[0], lane)
    _flash_init(m_ref, l_ref, acc_ref)

    sels = []
    for hh in range(2):
        gate = _dot_nt(qh[hh], km_ref[0])
        gate = jnp.where(lane < row_blk, gate, -jnp.inf)
        sel = jnp.zeros((tq, LANES), F32)
        for _ in range(MOBA_TOPK):
            m = jnp.max(gate, axis=1, keepdims=True)
            idx = jnp.min(jnp.where(gate == m, lane_f, float(LANES)), axis=1, keepdims=True)
            hit = lane_f == idx
            sel = jnp.where(hit & (m > -jnp.inf), 1.0, sel)
            gate = jnp.where(hit, -jnp.inf, gate)
        sels.append(sel)

    def chosen(sel, blk):
        return jnp.max(jnp.where(lane == blk, sel, 0.0), axis=1, keepdims=True) > 0.0

    start = pl.multiple_of(qi * tq, tq)
    k_d = k_ref[0, pl.ds(start, tk), :]
    v_d = v_ref[0, pl.ds(start, tk), :]
    for hh in range(2):
        visible = (col >= MOBA_BLOCK) | (row < MOBA_BLOCK) | chosen(sels[hh], 2 * qi)
        s = jnp.where((col <= row) & visible, _dot_nt(qh[hh], k_d), NEG_INF)
        _flash_update(hh, s, v_d, m_ref, l_ref, acc_ref)

    def body(j, carry):
        off = pl.multiple_of(j * tk, tk)
        kj = k_ref[0, pl.ds(off, tk), :]
        vj = v_ref[0, pl.ds(off, tk), :]
        for hh in range(2):
            s = _dot_nt(qh[hh], kj)
            s = jnp.concatenate(
                [jnp.where(chosen(sels[hh], 2 * j), s[:, :MOBA_BLOCK], NEG_INF),
                 jnp.where(chosen(sels[hh], 2 * j + 1), s[:, MOBA_BLOCK:], NEG_INF)], axis=1)
            _flash_update(hh, s, vj, m_ref, l_ref, acc_ref)
        return carry

    lax.fori_loop(0, qi, body, 0)
    o_ref[0] = _flash_finish(lane, l_ref, acc_ref).astype(o_ref.dtype)


def _flash_scratch(tq):
    return [pltpu.VMEM((2, tq, LANES), F32)] * 3


def _moba(q, k, v, kmean, *, batch, seq):
    W = q.shape[1]
    tq = 2 * MOBA_BLOCK
    q3, k3, v3 = (t.reshape(batch, seq, W) for t in (q, k, v))
    out = pl.pallas_call(
        _moba_kernel,
        grid=(batch, W // PAIR_W, seq // tq),
        in_specs=[pl.BlockSpec((1, tq, PAIR_W), lambda b, p, i: (b, i, p)),
                  pl.BlockSpec((1, seq, PAIR_W), lambda b, p, i: (b, 0, p)),
                  pl.BlockSpec((1, seq, PAIR_W), lambda b, p, i: (b, 0, p)),
                  pl.BlockSpec((1, LANES, PAIR_W), lambda b, p, i: (b, 0, p))],
        out_specs=pl.BlockSpec((1, tq, PAIR_W), lambda b, p, i: (b, i, p)),
        out_shape=jax.ShapeDtypeStruct((batch, seq, W), BF16),
        scratch_shapes=_flash_scratch(tq),
        compiler_params=_cparams(("parallel", "parallel", "parallel")),
        name="moba",
    )(q3, k3, v3, kmean)
    return out.reshape(batch * seq, W)


def _fox_kernel(q_ref, k_ref, v_ref, g_ref, fcol_ref, frow_ref, o_ref, m_ref, l_ref, acc_ref):
    tq = q_ref.shape[1]
    tk = tq
    pr = pl.program_id(1)
    qi = pl.program_id(2)
    lane = lax.broadcasted_iota(jnp.int32, (tq, LANES), 1)
    row = lax.broadcasted_iota(jnp.int32, (tq, tk), 0)
    col = lax.broadcasted_iota(jnp.int32, (tq, tk), 1)
    qh = _head_queries(q_ref[0], lane)
    fcol = fcol_ref[0]
    fq = [jnp.sum(jnp.where(lane == 2 * pr + hh, fcol, 0.0), axis=1, keepdims=True) for hh in range(2)]
    _flash_init(m_ref, l_ref, acc_ref)

    def tile(off, diagonal):
        kj = k_ref[0, pl.ds(off, tk), :]
        vj = v_ref[0, pl.ds(off, tk), :]
        fk_all = frow_ref[0, :, pl.ds(off, tk)]
        sub = lax.broadcasted_iota(jnp.int32, fk_all.shape, 0)
        for hh in range(2):
            fk = jnp.sum(jnp.where(sub == 2 * pr + hh, fk_all, 0.0), axis=0, keepdims=True)
            s = _dot_nt(qh[hh], kj) + fq[hh] - fk
            if diagonal:
                s = jnp.where(col <= row, s, NEG_INF)
            _flash_update(hh, s, vj, m_ref, l_ref, acc_ref)

    tile(pl.multiple_of(qi * tq, tq), True)

    def body(j, carry):
        tile(pl.multiple_of(j * tk, tk), False)
        return carry

    lax.fori_loop(0, qi, body, 0)
    o = _flash_finish(lane, l_ref, acc_ref)
    o_ref[0] = (o * jax.nn.sigmoid(g_ref[0].astype(F32))).astype(o_ref.dtype)


def _fox(q, k, v, g, fcol, frow, *, batch, seq, tq=512):
    W = q.shape[1]
    nh = frow.shape[1]
    q3, k3, v3, g3 = (t.reshape(batch, seq, W) for t in (q, k, v, g))
    out = pl.pallas_call(
        _fox_kernel,
        grid=(batch, W // PAIR_W, seq // tq),
        in_specs=[pl.BlockSpec((1, tq, PAIR_W), lambda b, p, i: (b, i, p)),
                  pl.BlockSpec((1, seq, PAIR_W), lambda b, p, i: (b, 0, p)),
                  pl.BlockSpec((1, seq, PAIR_W), lambda b, p, i: (b, 0, p)),
                  pl.BlockSpec((1, tq, PAIR_W), lambda b, p, i: (b, i, p)),
                  pl.BlockSpec((1, tq, LANES), lambda b, p, i: (b, i, 0)),
                  pl.BlockSpec((1, nh, seq), lambda b, p, i: (b, 0, 0))],
        out_specs=pl.BlockSpec((1, tq, PAIR_W), lambda b, p, i: (b, i, p)),
        out_shape=jax.ShapeDtypeStruct((batch, seq, W), BF16),
        scratch_shapes=_flash_scratch(tq),
        compiler_params=_cparams(("parallel", "parallel", "parallel")),
        name="fox",
    )(q3, k3, v3, g3, fcol, frow)
    return out.reshape(batch * seq, W)


def _swa_kernel(q_ref, k_ref, v_ref, sink_ref, o_ref):
    tq = q_ref.shape[1]
    qi = pl.program_id(1)
    group = SWA_Q_HEADS // SWA_KV_HEADS
    rows = group * tq
    tk = tq + SWA_WINDOW
    lane = lax.broadcasted_iota(jnp.int32, (tq, LANES), 1)
    kstart = pl.multiple_of(jnp.maximum(qi * tq - SWA_WINDOW, 0), SWA_WINDOW)
    k = k_ref[0, pl.ds(kstart, tk), :]
    v = v_ref[0, pl.ds(kstart, tk), :]
    qpos = qi * tq + (lax.broadcasted_iota(jnp.int32, (rows, tk), 0) & (tq - 1))
    kpos = kstart + lax.broadcasted_iota(jnp.int32, (rows, tk), 1)
    dist = qpos - kpos
    allowed = (dist >= 0) & (dist < SWA_WINDOW)
    sink_tab = sink_ref[...]
    outs = [None] * SWA_Q_HEADS
    for c in range(SWA_KV_HEADS):
        pieces, sinks = [], []
        for g in range(group):
            head = c * group + g
            blk = q_ref[0, :, (head // 2) * PAIR_W:(head // 2 + 1) * PAIR_W] * ATTN_SCALE
            hh = head % 2
            qm = jnp.where((lane >= hh * HEAD_DIM) & (lane < (hh + 1) * HEAD_DIM), blk, jnp.zeros_like(blk))
            if hh != c:
                qm = pltpu.roll(qm.astype(F32), HEAD_DIM, 1).astype(BF16)
            pieces.append(qm)
            sinks.append(jnp.broadcast_to(sink_tab[head:head + 1, :1], (tq, 1)))
        qs = jnp.concatenate(pieces, axis=0)
        sink = jnp.concatenate(sinks, axis=0)
        s = jnp.where(allowed, _dot_nt(qs, k), NEG_INF)
        m = jnp.maximum(jnp.max(s, axis=1, keepdims=True), sink)
        p = jnp.exp(s - m)
        l = jnp.sum(p, axis=1, keepdims=True) + jnp.exp(sink - m)
        o = _dot(p.astype(BF16), v) / l
        for g in range(group):
            head = c * group + g
            oh = o[g * tq:(g + 1) * tq]
            outs[head] = oh if head % 2 == c else pltpu.roll(oh, HEAD_DIM, 1)
    for pp in range(SWA_Q_HEADS // 2):
        o_ref[0, :, pp * PAIR_W:(pp + 1) * PAIR_W] = jnp.where(
            lane < HEAD_DIM, outs[2 * pp], outs[2 * pp + 1]).astype(o_ref.dtype)


def _swa(q, k, v, sinks, *, batch, seq, tq=SWA_WINDOW):
    W = q.shape[1]
    assert tq & (tq - 1) == 0
    q3 = q.reshape(batch, seq, W)
    k3, v3 = (t.reshape(batch, seq, PAIR_W) for t in (k, v))
    sink_tab = jnp.broadcast_to(sinks.astype(F32)[:, None], (SWA_Q_HEADS, LANES))
    out = pl.pallas_call(
        _swa_kernel,
        grid=(batch, seq // tq),
        in_specs=[pl.BlockSpec((1, tq, W), lambda b, i: (b, i, 0)),
                  pl.BlockSpec((1, seq, PAIR_W), lambda b, i: (b, 0, 0)),
                  pl.BlockSpec((1, seq, PAIR_W), lambda b, i: (b, 0, 0)),
                  pl.BlockSpec((SWA_Q_HEADS, LANES), lambda b, i: (0, 0))],
        out_specs=pl.BlockSpec((1, tq, W), lambda b, i: (b, i, 0)),
        out_shape=jax.ShapeDtypeStruct((batch, seq, W), BF16),
        compiler_params=_cparams(("parallel", "parallel")),
        name="swa",
    )(q3, k3, v3, sink_tab)
    return out.reshape(batch * seq, W)


def _outproj_kernel(*refs, n_parts):
    parts = refs[:n_parts]
    w_ref, x_ref, o_ref = refs[n_parts:]
    y = x_ref[...]
    off = 0
    for p_ref in parts:
        kw = p_ref.shape[1]
        y = y + _dot(p_ref[...], w_ref[off:off + kw, :])
        off += kw
    o_ref[...] = y


def _outproj(parts, w, x, tm=512):
    T, D = x.shape
    in_specs = [pl.BlockSpec((tm, p.shape[1]), lambda i: (i, 0)) for p in parts]
    in_specs += [pl.BlockSpec(w.shape, lambda i: (0, 0)),
                 pl.BlockSpec((tm, D), lambda i: (i, 0))]
    return pl.pallas_call(
        functools.partial(_outproj_kernel, n_parts=len(parts)),
        grid=(T // tm,),
        in_specs=in_specs,
        out_specs=pl.BlockSpec((tm, D), lambda i: (i, 0)),
        out_shape=jax.ShapeDtypeStruct((T, D), F32),
        compiler_params=_cparams(("parallel",)),
        name="outproj",
    )(*parts, w, x)


_CAND_ROWS = 80


def _cand_tables():
    pos = np.zeros((_CAND_ROWS,), np.float32)
    neg = np.zeros((_CAND_ROWS,), np.float32)
    r = 0
    for a, nb in ((0, 16), (1, 8), (2, 8), (3, 8), (4, 8), (5, 8), (6, 8), (7, 8)):
        for b in range(nb):
            pos[r] = a * PEER_TOPK + b
            neg[r] = 0.0 if (a + 1) * (b + 1) <= PEER_TOPK else -np.inf
            r += 1
    for a in range(8, 16):
        pos[r] = a * PEER_TOPK
        r += 1
    assert r == _CAND_ROWS
    tab = lambda t: jnp.asarray(np.broadcast_to(t[:, None], (_CAND_ROWS, LANES)).copy())
    return tab(pos), tab(neg)


def _extract_top(v, nk):
    kio = lax.broadcasted_iota(jnp.int32, v.shape, 0).astype(F32)
    slot = lax.broadcasted_iota(jnp.int32, (PEER_TOPK, v.shape[1]), 0)

    def body(a, carry):
        v, rank, vals = carry
        m = jnp.max(v, axis=0, keepdims=True)
        idx = jnp.min(jnp.where(v == m, kio, float(nk)), axis=0, keepdims=True)
        hit = kio == idx
        return (jnp.where(hit, -jnp.inf, v), jnp.where(hit, jnp.asarray(a, F32), rank),
                jnp.where(slot == a, m, vals))

    init = (v, jnp.full(v.shape, float(PEER_TOPK), F32), jnp.zeros((PEER_TOPK, v.shape[1]), F32))
    _, rank, vals = lax.fori_loop(0, PEER_TOPK, body, init)
    return vals, rank


def _route_kernel(h_ref, wq_ref, keys_ref, pos_ref, neg_ref,
                  c1_ref, e1_ref, r2_ref, e2_ref, qt_ref):
    half = PEER_QUERY_DIM // 2
    qt_ref[...] = _dot_nt(wq_ref[...], h_ref[...]).astype(BF16)
    pos = pos_ref[...]
    neg = neg_ref[...]
    slot = lax.broadcasted_iota(jnp.int32, (PEER_TOPK, h_ref.shape[0]), 0)

    def head_body(h, _):
        r0 = pl.multiple_of(h * PEER_QUERY_DIM, PEER_QUERY_DIM)
        s1 = _dot(keys_ref[2 * h], qt_ref[pl.ds(r0, half), :])
        s2 = _dot(keys_ref[2 * h + 1], qt_ref[pl.ds(r0 + half, half), :])
        v1, rank1 = _extract_top(s1, PEER_N_KEYS)
        v2, rank2 = _extract_top(s2, PEER_N_KEYS)
        blocks = [v1[0:1] + v2[0:8], v1[0:1] + v2[8:16]]
        blocks += [v1[a:a + 1] + v2[0:8] for a in range(1, 8)]
        blocks += [v1[8:16] + v2[0:1]]
        cand = jnp.concatenate(blocks, axis=0) + neg

        def pick(kk, carry):
            cand, chosen, ts = carry
            m = jnp.max(cand, axis=0, keepdims=True)
            first = jnp.min(jnp.where(cand == m, pos, 1e9), axis=0, keepdims=True)
            hit = pos == first
            return (jnp.where(hit, -jnp.inf, cand), jnp.where(hit, 1.0, chosen),
                    jnp.where(slot == kk, m, ts))

        _, chosen, ts = lax.fori_loop(0, PEER_TOPK, pick,
                                      (cand, jnp.zeros_like(cand), jnp.zeros_like(v1)))
        z =jnp.sum(jnp.exp(ts - ts[0:1]), axis=0, keepdims=True)
        counts = [jnp.sum(chosen[0:16], axis=0, keepdims=True)]
        counts += [jnp.sum(chosen[8 * a + 8:8 * a + 16], axis=0, keepdims=True) for a in range(1, 8)]
        counts += [chosen[72 + a:73 + a] for a in range(8)]
        c1 = jnp.zeros_like(rank1)
        for a in range(PEER_TOPK):
            c1 = jnp.where(rank1 == float(a), counts[a], c1)
        c1_ref[h] = c1
        e1_ref[h] = jnp.exp(s1 - v1[0:1]) / z
        r2_ref[h] = rank2.astype(BF16)
        e2_ref[h] = jnp.exp(s2 - v2[0:1]).astype(BF16)
        return 0

    lax.fori_loop(0, PEER_HEADS, head_body, 0)


def _peer_route(h2, wq_t, keys, tt=LANES):
    T, D = h2.shape
    pos, neg = _cand_tables()
    stat_spec = pl.BlockSpec((PEER_HEADS, PEER_N_KEYS, tt), lambda i: (0, 0, i))
    stat = lambda dt: jax.ShapeDtypeStruct((PEER_HEADS, PEER_N_KEYS, T), dt)
    return pl.pallas_call(
        _route_kernel,
        grid=(T // tt,),
        in_specs=[pl.BlockSpec((tt, D), lambda i: (i, 0)),
                  pl.BlockSpec(wq_t.shape, lambda i: (0, 0)),
                  pl.BlockSpec(keys.shape, lambda i: (0, 0, 0)),
                  pl.BlockSpec((_CAND_ROWS, LANES), lambda i: (0, 0)),
                  pl.BlockSpec((_CAND_ROWS, LANES), lambda i: (0, 0))],
        out_specs=[stat_spec] * 4,
        out_shape=[stat(F32), stat(F32), stat(BF16), stat(BF16)],
        scratch_shapes=[pltpu.VMEM((PEER_HEADS * PEER_QUERY_DIM, tt), BF16)],
        compiler_params=_cparams(("parallel",)),
        name="peer_route",
    )(h2, wq_t, keys, pos, neg)


_KEY_GROUP = 8
_UNITS = 4


def _build_gated(a_ref, p_ref, c1_ref, e1_ref, r2_ref, e2_ref, key0, g0, ng, lt):
    rep = PEER_N_KEYS // 16

    def rows16(row):
        blk = jnp.broadcast_to(row, (16, LANES)).astype(BF16)
        return jnp.concatenate([blk] * rep, axis=0)

    ls = slice(lt * LANES, (lt + 1) * LANES)
    w = [jnp.zeros((PEER_N_KEYS, LANES), BF16) for _ in range(ng)]
    for h in range(PEER_HEADS):
        c1 = c1_ref[h, pl.ds(key0, _KEY_GROUP), ls]
        e1 = e1_ref[h, pl.ds(key0, _KEY_GROUP), ls]
        r2 = r2_ref[h, :, ls]
        e2 = e2_ref[h, :, ls]
        for g in range(ng):
            thr = rows16(c1[g0 + g:g0 + g + 1])
            gate = rows16(e1[g0 + g:g0 + g + 1])
            w[g] = w[g] + jnp.where(r2 < thr, e2, jnp.zeros_like(e2)) * gate
    for g in range(g0, g0 + ng):
        rs = slice(g * PEER_N_KEYS, (g + 1) * PEER_N_KEYS)
        a = a_ref[rs, ls]
        gelu = 0.5 * a * (1.0 + lax.erf(a * (2.0 ** -0.5)))
        p_ref[rs, ls] = gelu.astype(BF16) * w[g - g0]


def _experts_kernel(h_ref, dn_ref, upt_ref, c1_ref, e1_ref, r2_ref, e2_ref, x_ref, o_ref,
                    a_ref, p_ref, acc_ref):
    te, tt = a_ref.shape
    d_model = acc_ref.shape[0]
    e = pl.program_id(1)
    n_tiles = pl.num_programs(1) - 1
    cur = e % 2
    ng = _KEY_GROUP // _UNITS
    n_lane = tt // LANES
    key0 = pl.multiple_of(jnp.minimum(e, n_tiles - 1) * _KEY_GROUP, _KEY_GROUP)

    def front_mm(u):
        rows = te // _UNITS
        rs = slice(u * rows, (u + 1) * rows)
        a_ref[rs, :] = _dot_nt(dn_ref[rs, :], h_ref[...])

    def back_mm(u):
        rows = d_model // _UNITS
        rs = slice(u * rows, (u + 1) * rows)
        acc_ref[rs, :] += _dot(upt_ref[rs, :], p_ref[1 - cur])

    def run(front, back):
        if front:
            front_mm(0)
        for u in range(_UNITS):
            for lt in range(n_lane):
                if front:
                    _build_gated(a_ref, p_ref.at[cur], c1_ref, e1_ref, r2_ref, e2_ref, key0, u * ng, ng, lt)
                if front and lt == 0 and u + 1 < _UNITS:
                    front_mm(u + 1)
                if back and lt == n_lane // 2:
                    back_mm(u)

    @pl.when(e == 0)
    def _():
        acc_ref[...] = jnp.zeros_like(acc_ref)
        run(True, False)

    @pl.when((e > 0) & (e < n_tiles))
    def _():
        run(True, True)

    @pl.when(e == n_tiles)
    def _():
        run(False, True)
        o_ref[...] = x_ref[...] + acc_ref[...].T


def _peer_experts(h2, down, up_t, stats, x, tt=512):
    T, D = h2.shape
    E = down.shape[0]
    te = _KEY_GROUP * PEER_N_KEYS
    n_tiles = E // te
    stat_spec = pl.BlockSpec((PEER_HEADS, PEER_N_KEYS, tt), lambda i, e: (0, 0, i))
    return pl.pallas_call(
        _experts_kernel,
        grid=(T // tt, n_tiles + 1),
        in_specs=[pl.BlockSpec((tt, D), lambda i, e: (i, 0)),
                  pl.BlockSpec((te, D), lambda i, e: (jnp.minimum(e, n_tiles - 1), 0)),
                  pl.BlockSpec((D, te), lambda i, e: (0, jnp.maximum(e - 1, 0))),
                  stat_spec, stat_spec, stat_spec, stat_spec,
                  pl.BlockSpec((tt, D), lambda i, e: (i, 0))],
        out_specs=pl.BlockSpec((tt, D), lambda i, e: (i, 0)),
        out_shape=jax.ShapeDtypeStruct((T, D), F32),
        scratch_shapes=[pltpu.VMEM((te, tt), F32),
                        pltpu.VMEM((2, te, tt), BF16),
                        pltpu.VMEM((D, tt), F32)],
        compiler_params=_cparams(("parallel", "arbitrary")),
        name="peer_experts",
    )(h2, down, up_t, *stats, x)


def _peer_layer(x, norm_gain, w_query, sub_keys, down, up):
    h2 = _rmsnorm(x, norm_gain)
    keys = sub_keys.reshape(PEER_HEADS * 2, PEER_N_KEYS, PEER_QUERY_DIM // 2).astype(BF16)
    stats = _peer_route(h2, w_query.T.astype(BF16), keys)
    return _peer_experts(h2, down.astype(BF16), up.T.astype(BF16), stats, x)


def _tile_heads(g, n):
    return jnp.tile(g.astype(F32), n)


def _even_mixer(x, norm_gain, w_in, f_bias, qn_a, kn_a, qn_b, kn_b, w_out, tabs, *, batch, seq):
    h = _rmsnorm(x, norm_gain)
    w = w_in.astype(BF16)
    o_qa, o_ka, o_va, o_qb, o_kb, o_vb, o_gb, o_fb = (
        0, A_W, 2 * A_W, 3 * A_W, 3 * A_W + B_W, 3 * A_W + 2 * B_W, 3 * A_W + 3 * B_W, 3 * A_W + 4 * B_W)
    gain_a = jnp.concatenate([_tile_heads(qn_a, MOBA_HEADS), _tile_heads(kn_a, MOBA_HEADS)])
    qk_a, km = _proj(h, w[:, o_qa:o_va], seq=seq, tn=A_W, tm=MOBA_BLOCK, gain=gain_a,
                     rope_tabs=tabs, kmean=True)
    nb = seq // MOBA_BLOCK
    kmean = km.reshape(batch, nb, 2 * A_W)[:, :, A_W:]
    kmean = jnp.pad(kmean, ((0, 0), (0, LANES - nb), (0, 0))).astype(BF16)
    gain_b = jnp.concatenate([_tile_heads(qn_b, FOX_HEADS), _tile_heads(kn_b, FOX_HEADS)])
    qk_b = _proj(h, w[:, o_qb:o_vb], seq=seq, tn=B_W, gain=gain_b)
    w_plain = jnp.concatenate([w[:, o_va:o_qb], w[:, o_vb:o_fb]], axis=1)
    vvg = _proj(h, w_plain, seq=seq, tn=A_W)
    va, vb, gb = vvg[:, :A_W], vvg[:, A_W:A_W + B_W], vvg[:, A_W + B_W:]
    frow = _fox_gates(h, w[:, o_fb:].T, f_bias, batch=batch, seq=seq)
    fcol = jnp.pad(frow.transpose(0, 2, 1), ((0, 0), (0, 0), (0, LANES - FOX_HEADS)))
    oa = _moba(qk_a[:, :A_W], qk_a[:, A_W:], va, kmean, batch=batch, seq=seq)
    ob = _fox(qk_b[:, :B_W], qk_b[:, B_W:], vb, gb, fcol, frow, batch=batch, seq=seq)
    return _outproj([oa, ob], w_out.astype(BF16), x)


def _odd_mixer(x, norm_gain, w_in, qn, kn, sinks, w_out, tabs, *, batch, seq):
    h = _rmsnorm(x, norm_gain)
    w = w_in.astype(BF16)
    qw = SWA_Q_HEADS * HEAD_DIM
    kw = SWA_KV_HEADS * HEAD_DIM
    q = _proj(h, w[:, :qw], seq=seq, tn=512, gain=_tile_heads(qn, SWA_Q_HEADS), rope_tabs=tabs)
    k = _proj(h, w[:, qw:qw + kw], seq=seq, tn=kw, gain=_tile_heads(kn, SWA_KV_HEADS), rope_tabs=tabs)
    v = _proj(h, w[:, qw + kw:], seq=seq, tn=kw)
    o = _swa(q, k, v, sinks, batch=batch, seq=seq)
    return _outproj([o], w_out.astype(BF16), x)


def kernel(x, attn_norm, ffn_norm, ev_w_in, ev_forget_bias, ev_q_norm_a, ev_k_norm_a, ev_q_norm_b,
           ev_k_norm_b, ev_w_out, od_w_in, od_q_norm, od_k_norm, od_sinks, od_w_out,
           peer_w_query, peer_sub_keys, peer_down, peer_up):
    batch, seq, d_model = x.shape
    depth = attn_norm.shape[0]
    tabs = _rope_tables(seq)
    xt = x.reshape(batch * seq, d_model)
    for l in range(depth):
        i = l // 2
        if l % 2 == 0:
            xt = _even_mixer(xt, attn_norm[l], ev_w_in[i], ev_forget_bias[i], ev_q_norm_a[i],
                             ev_k_norm_a[i], ev_q_norm_b[i], ev_k_norm_b[i], ev_w_out[i], tabs,
                             batch=batch, seq=seq)
        else:
            xt = _odd_mixer(xt, attn_norm[l], od_w_in[i], od_q_norm[i], od_k_norm[i], od_sinks[i],
                            od_w_out[i], tabs, batch=batch, seq=seq)
        xt = _peer_layer(xt, ffn_norm[l], peer_w_query[l], peer_sub_keys[l], peer_down[l], peer_up[l])
    return xt.reshape(batch, seq, d_model)
```

```python
import functools

import numpy as np
import jax
import jax.numpy as jnp
from jax import lax
from jax.experimental import pallas as pl
from jax.experimental.pallas import tpu as pltpu

F32 = jnp.float32
BF16 = jnp.bfloat16

HEAD_DIM = 64
ROT_DIM = HEAD_DIM // 4
ROPE_THETA = 500000.0
ATTN_SCALE = HEAD_DIM ** -0.5
EPS = 1e-6
NEG_INF = -1e30

MOBA_HEADS = 8
FOX_HEADS = 8
MOBA_BLOCK = 256
MOBA_TOPK = 3
A_W = MOBA_HEADS * HEAD_DIM
B_W = FOX_HEADS * HEAD_DIM

SWA_Q_HEADS = 16
SWA_KV_HEADS = 2
SWA_WINDOW = 128

PEER_HEADS = 8
PEER_N_KEYS = 128
PEER_TOPK = 16
PEER_QUERY_DIM = 128

LANES = 128
PAIR_W = 2 * HEAD_DIM
VMEM_LIMIT = 48 * 1024 * 1024


def _cparams(sem):
    return pltpu.CompilerParams(dimension_semantics=sem, vmem_limit_bytes=VMEM_LIMIT)


def _dot_nt(a, b):
    return lax.dot_general(a, b, (((1,), (1,)), ((), ())), preferred_element_type=F32)


def _dot(a, b):
    return jnp.dot(a, b, preferred_element_type=F32)


def _split3(x):
    h1 = x.astype(BF16)
    r1 = x - h1.astype(F32)
    h2 = r1.astype(BF16)
    h3 = (r1 - h2.astype(F32)).astype(BF16)
    return h1, h2, h3


def _rmsnorm_kernel(x_ref, g_ref, o_ref):
    x = x_ref[...]
    ms = jnp.mean(x * x, axis=-1, keepdims=True)
    o_ref[...] = (x * lax.rsqrt(ms + EPS) * g_ref[...]).astype(o_ref.dtype)


def _rmsnorm(x, gain, tm=512):
    T, D = x.shape
    return pl.pallas_call(
        _rmsnorm_kernel,
        grid=(T // tm,),
        in_specs=[pl.BlockSpec((tm, D), lambda i: (i, 0)),
                  pl.BlockSpec((1, D), lambda i: (0, 0))],
        out_specs=pl.BlockSpec((tm, D), lambda i: (i, 0)),
        out_shape=jax.ShapeDtypeStruct((T, D), BF16),
        compiler_params=_cparams(("parallel",)),
        name="rmsnorm",
    )(x, gain.reshape(1, D))


def _proj_kernel(*refs, norm, rope, kmean, tn):
    it = iter(refs)
    h_ref, w_ref = next(it), next(it)
    gain_ref = next(it) if norm else None
    bd_ref = next(it) if norm else None
    if rope:
        c_ref, sa_ref, sb_ref = next(it), next(it), next(it)
    o_ref = next(it)
    km_ref = next(it) if kmean else None

    y = _dot(h_ref[...], w_ref[...])
    if norm:
        y2 = y * y
        bd = bd_ref[...]
        cols = []
        for c in range(tn // LANES):
            h1, h2, h3 = _split3(y2[:, c * LANES:(c + 1) * LANES])
            cols.append(_dot(h1, bd) + _dot(h2, bd) + _dot(h3, bd))
        ms = cols[0] if len(cols) == 1 else jnp.concatenate(cols, axis=1)
        y = y * lax.rsqrt(ms + EPS) * gain_ref[...]
    if rope:
        rep = tn // LANES
        tile = (lambda t: t) if rep == 1 else (lambda t: jnp.concatenate([t] * rep, axis=1))
        y = (y * tile(c_ref[...])
             + pltpu.roll(y, tn - ROT_DIM // 2, 1) * tile(sa_ref[...])
             + pltpu.roll(y, ROT_DIM // 2, 1) * tile(sb_ref[...]))
    o_ref[...] = y.astype(o_ref.dtype)
    if kmean:
        km_ref[0] = jnp.mean(y, axis=0, keepdims=True)


def _proj(h, w, *, seq, tn, tm=512, gain=None, rope_tabs=None, kmean=False):
    T, D = h.shape
    N = w.shape[1]
    norm = gain is not None
    rope = rope_tabs is not None
    nseq = seq // tm
    in_specs = [pl.BlockSpec((tm, D), lambda i, j: (i, 0)),
                pl.BlockSpec((D, tn), lambda i, j: (0, j))]
    args = [h, w]
    if norm:
        bd = np.kron(np.eye(LANES // HEAD_DIM), np.ones((HEAD_DIM, HEAD_DIM))) / HEAD_DIM
        in_specs += [pl.BlockSpec((1, tn), lambda i, j: (0, j)),
                     pl.BlockSpec((LANES, LANES), lambda i, j: (0, 0))]
        args += [gain.reshape(1, N).astype(F32), jnp.asarray(bd, BF16)]
    if rope:
        in_specs += [pl.BlockSpec((tm, LANES), lambda i, j: (i % nseq, 0))] * 3
        args += list(rope_tabs)
    out_specs = [pl.BlockSpec((tm, tn), lambda i, j: (i, j))]
    out_shape = [jax.ShapeDtypeStruct((T, N), BF16)]
    if kmean:
        out_specs.append(pl.BlockSpec((1, 1, tn), lambda i, j: (i, 0, j)))
        out_shape.append(jax.ShapeDtypeStruct((T // tm, 1, N), F32))
    res = pl.pallas_call(
        functools.partial(_proj_kernel, norm=norm, rope=rope, kmean=kmean, tn=tn),
        grid=(T // tm, N // tn),
        in_specs=in_specs, out_specs=out_specs, out_shape=out_shape,
        compiler_params=_cparams(("parallel", "parallel")),
        name="proj",
    )(*args)
    return res if kmean else res[0]


def _rope_tables(seq):
    half = ROT_DIM // 2
    inv_freq = jnp.power(ROPE_THETA, -jnp.arange(0, ROT_DIM, 2, dtype=F32) / ROT_DIM)
    ang = jnp.arange(seq, dtype=F32)[:, None] * inv_freq[None, :]
    cos, sin = jnp.cos(ang), jnp.sin(ang)
    one = jnp.ones((seq, HEAD_DIM - ROT_DIM), F32)
    zero = jnp.zeros((seq, HEAD_DIM - ROT_DIM), F32)
    z8 = jnp.zeros((seq, half), F32)
    c = jnp.concatenate([cos, cos, one], axis=1)
    sa = jnp.concatenate([-sin, z8, zero], axis=1)
    sb = jnp.concatenate([z8, sin, zero], axis=1)
    rep = LANES // HEAD_DIM
    return tuple(jnp.concatenate([t] * rep, axis=1) for t in (c, sa, sb))


def _gates_kernel(h_ref, wf_ref, b_ref, tri_ref, o_ref, carry_ref):
    @pl.when(pl.program_id(1) == 0)
    def _():
        carry_ref[...] = jnp.zeros_like(carry_ref)

    z = _dot_nt(wf_ref[...], h_ref[...]) + b_ref[...][:, :1]
    lf = jnp.minimum(z, 0.0) - jnp.log1p(jnp.exp(-jnp.abs(z)))
    tri = tri_ref[...]
    h1, h2, h3 = _split3(lf)
    cs = _dot(h1, tri) + _dot(h2, tri) + _dot(h3, tri) + carry_ref[...][:, :1]
    o_ref[0] = cs
    carry_ref[...] = jnp.broadcast_to(cs[:, -1:], carry_ref.shape)


def _fox_gates(h, wf_t, bias, *, batch, seq, tm=512):
    T, D = h.shape
    nh = wf_t.shape[0]
    nseq = seq // tm
    tri = jnp.asarray(np.triu(np.ones((tm, tm))), BF16)
    return pl.pallas_call(
        _gates_kernel,
        grid=(batch, nseq),
        in_specs=[pl.BlockSpec((tm, D), lambda b, s: (b * nseq + s, 0)),
                  pl.BlockSpec((nh, D), lambda b, s: (0, 0)),
                  pl.BlockSpec((nh, LANES), lambda b, s: (0, 0)),
                  pl.BlockSpec((tm, tm), lambda b, s: (0, 0))],
        out_specs=pl.BlockSpec((1, nh, tm), lambda b, s: (b, 0, s)),
        out_shape=jax.ShapeDtypeStruct((batch, nh, seq), F32),
        scratch_shapes=[pltpu.VMEM((nh, LANES), F32)],
        compiler_params=_cparams(("parallel", "arbitrary")),
        name="fox_gates",
    )(h, wf_t, jnp.broadcast_to(bias.astype(F32)[:, None], (nh, LANES)), tri)


def _lane_tile(x, width):
    rep = width // LANES
    return x if rep == 1 else jnp.concatenate([x] * rep, axis=1)


def _flash_init(m_ref, acc_ref):
    m_ref[...] = jnp.full(m_ref.shape, NEG_INF, F32)
    acc_ref[...] = jnp.zeros(acc_ref.shape, F32)


def _head_values(v):
    lane = lax.broadcasted_iota(jnp.int32, v.shape, 1)
    return [jnp.where((lane >= hh * HEAD_DIM) & (lane < (hh + 1) * HEAD_DIM), v, jnp.ones_like(v))
            for hh in range(2)]


def _flash_update(slot, s, v, m_ref, acc_ref):
    tk = s.shape[1]
    m_prev = m_ref[slot]
    m_new = jnp.maximum(m_prev, jnp.max(s, axis=1, keepdims=True))
    alpha = jnp.exp(m_prev - m_new)
    p = jnp.exp(s - _lane_tile(m_new, tk))
    acc_ref[slot] = alpha * acc_ref[slot] + _dot(p.astype(BF16), v)
    m_ref[slot] = m_new


def _flash_finish(lane, acc_ref):
    outs = []
    for hh in range(2):
        acc = acc_ref[hh]
        den = (1 - hh) * HEAD_DIM
        outs.append(acc / acc[:, den:den + 1])
    return jnp.where(lane < HEAD_DIM, outs[0], outs[1])


def _head_queries(q, lane):
    qs = q * ATTN_SCALE
    return [jnp.where((lane >= hh * HEAD_DIM) & (lane < (hh + 1) * HEAD_DIM), qs, jnp.zeros_like(qs))
            for hh in range(2)]


def _moba_kernel(q_ref, k_ref, v_ref, km_ref, o_ref, m_ref, acc_ref):
    tq = q_ref.shape[1]
    tk = tq
    qi = pl.program_id(2)
    lane = lax.broadcasted_iota(jnp.int32, (tq, LANES), 1)
    lane_f = lane.astype(F32)
    rowv = lax.broadcasted_iota(jnp.int32, (tq, LANES), 0)
    row_blk = 2 * qi + (rowv >= MOBA_BLOCK).astype(jnp.int32)
    row = lax.broadcasted_iota(jnp.int32, (tq, tk), 0)
    col = lax.broadcasted_iota(jnp.int32, (tq, tk), 1)
    qh = _head_queries(q_ref[0], lane)
    _flash_init(m_ref, acc_ref)

    sels = []
    for hh in range(2):
        gate = _dot_nt(qh[hh], km_ref[0])
        gate = jnp.where(lane < row_blk, gate, -jnp.inf)
        sel = jnp.zeros((tq, LANES), F32)
        for _ in range(MOBA_TOPK):
            m = jnp.max(gate, axis=1, keepdims=True)
            idx = jnp.min(jnp.where(gate == m, lane_f, float(LANES)), axis=1, keepdims=True)
            hit = lane_f == idx
            sel = jnp.where(hit & (m > -jnp.inf), 1.0, sel)
            gate = jnp.where(hit, -jnp.inf, gate)
        sels.append(sel)

    def chosen(sel, blk):
        return jnp.max(jnp.where(lane == blk, sel, 0.0), axis=1, keepdims=True) > 0.0

    start = pl.multiple_of(qi * tq, tq)
    k_d = k_ref[0, pl.ds(start, tk), :]
    v_d = _head_values(v_ref[0, pl.ds(start, tk), :])
    for hh in range(2):
        visible = (col >= MOBA_BLOCK) | (row < MOBA_BLOCK) | chosen(sels[hh], 2 * qi)
        s = jnp.where((col <= row) & visible, _dot_nt(qh[hh], k_d), NEG_INF)
        _flash_update(hh, s, v_d[hh], m_ref, acc_ref)

    def body(j, carry):
        off = pl.multiple_of(j * tk, tk)
        kj = k_ref[0, pl.ds(off, tk), :]
        vh = _head_values(v_ref[0, pl.ds(off, tk), :])
        for hh in range(2):
            s = _dot_nt(qh[hh], kj)
            s = jnp.concatenate(
                [jnp.where(chosen(sels[hh], 2 * j), s[:, :MOBA_BLOCK], NEG_INF),
                 jnp.where(chosen(sels[hh], 2 * j + 1), s[:, MOBA_BLOCK:], NEG_INF)], axis=1)
            _flash_update(hh, s, vh[hh], m_ref, acc_ref)
        return carry

    lax.fori_loop(0, qi, body, 0)
    o_ref[0] = _flash_finish(lane, acc_ref).astype(o_ref.dtype)


def _flash_scratch(tq):
    return [pltpu.VMEM((2, tq, LANES), F32)] * 2


def _moba(q, k, v, kmean, *, batch, seq):
    W = q.shape[1]
    tq = 2 * MOBA_BLOCK
    q3, k3, v3 = (t.reshape(batch, seq, W) for t in (q, k, v))
    out = pl.pallas_call(
        _moba_kernel,
        grid=(batch, W // PAIR_W, seq // tq),
        in_specs=[pl.BlockSpec((1, tq, PAIR_W), lambda b, p, i: (b, i, p)),
                  pl.BlockSpec((1, seq, PAIR_W), lambda b, p, i: (b, 0, p)),
                  pl.BlockSpec((1, seq, PAIR_W), lambda b, p, i: (b, 0, p)),
                  pl.BlockSpec((1, LANES, PAIR_W), lambda b, p, i: (b, 0, p))],
        out_specs=pl.BlockSpec((1, tq, PAIR_W), lambda b, p, i: (b, i, p)),
        out_shape=jax.ShapeDtypeStruct((batch, seq, W), BF16),
        scratch_shapes=_flash_scratch(tq),
        compiler_params=_cparams(("parallel", "parallel", "parallel")),
        name="moba",
    )(q3, k3, v3, kmean)
    return out.reshape(batch * seq, W)


def _fox_kernel(q_ref, k_ref, v_ref, g_ref, fcol_ref, frow_ref, o_ref, m_ref, acc_ref):
    tq = q_ref.shape[1]
    tk = tq
    pr = pl.program_id(1)
    qi = pl.program_id(2)
    lane = lax.broadcasted_iota(jnp.int32, (tq, LANES), 1)
    row = lax.broadcasted_iota(jnp.int32, (tq, tk), 0)
    col = lax.broadcasted_iota(jnp.int32, (tq, tk), 1)
    qh = _head_queries(q_ref[0], lane)
    fcol = fcol_ref[0]
    fq = [jnp.sum(jnp.where(lane == 2 * pr + hh, fcol, 0.0), axis=1, keepdims=True) for hh in range(2)]
    _flash_init(m_ref, acc_ref)

    def tile(off, diagonal):
        kj = k_ref[0, pl.ds(off, tk), :]
        vh = _head_values(v_ref[0, pl.ds(off, tk), :])
        fk_all = frow_ref[0, :, pl.ds(off, tk)]
        sub = lax.broadcasted_iota(jnp.int32, fk_all.shape, 0)
        for hh in range(2):
            fk = jnp.sum(jnp.where(sub == 2 * pr + hh, fk_all, 0.0), axis=0, keepdims=True)
            s = _dot_nt(qh[hh], kj) + fq[hh] - fk
            if diagonal:
                s = jnp.where(col <= row, s, NEG_INF)
            _flash_update(hh, s, vh[hh], m_ref, acc_ref)

    tile(pl.multiple_of(qi * tq, tq), True)

    def body(j, carry):
        tile(pl.multiple_of(j * tk, tk), False)
        return carry

    lax.fori_loop(0, qi, body, 0)
    o = _flash_finish(lane, acc_ref)
    o_ref[0] = (o * jax.nn.sigmoid(g_ref[0].astype(F32))).astype(o_ref.dtype)


def _fox(q, k, v, g, fcol, frow, *, batch, seq, tq=512):
    W = q.shape[1]
    nh = frow.shape[1]
    q3, k3, v3, g3 = (t.reshape(batch, seq, W) for t in (q, k, v, g))
    out = pl.pallas_call(
        _fox_kernel,
        grid=(batch, W // PAIR_W, seq // tq),
        in_specs=[pl.BlockSpec((1, tq, PAIR_W), lambda b, p, i: (b, i, p)),
                  pl.BlockSpec((1, seq, PAIR_W), lambda b, p, i: (b, 0, p)),
                  pl.BlockSpec((1, seq, PAIR_W), lambda b, p, i: (b, 0, p)),
                  pl.BlockSpec((1, tq, PAIR_W), lambda b, p, i: (b, i, p)),
                  pl.BlockSpec((1, tq, LANES), lambda b, p, i: (b, i, 0)),
                  pl.BlockSpec((1, nh, seq), lambda b, p, i: (b, 0, 0))],
        out_specs=pl.BlockSpec((1, tq, PAIR_W), lambda b, p, i: (b, i, p)),
        out_shape=jax.ShapeDtypeStruct((batch, seq, W), BF16),
        scratch_shapes=_flash_scratch(tq),
        compiler_params=_cparams(("parallel", "parallel", "parallel")),
        name="fox",
    )(q3, k3, v3, g3, fcol, frow)
    return out.reshape(batch * seq, W)


def _swa_kernel(q_ref, k_ref, v_ref, sink_ref, o_ref):
    tq = q_ref.shape[1]
    qi = pl.program_id(1)
    group = SWA_Q_HEADS // SWA_KV_HEADS
    rows = group * tq
    tk = tq + SWA_WINDOW
    lane = lax.broadcasted_iota(jnp.int32, (tq, LANES), 1)
    kstart = pl.multiple_of(jnp.maximum(qi * tq - SWA_WINDOW, 0), SWA_WINDOW)
    k = k_ref[0, pl.ds(kstart, tk), :]
    v = v_ref[0, pl.ds(kstart, tk), :]
    qpos = qi * tq + (lax.broadcasted_iota(jnp.int32, (rows, tk), 0) & (tq - 1))
    kpos = kstart + lax.broadcasted_iota(jnp.int32, (rows, tk), 1)
    dist = qpos - kpos
    allowed = (dist >= 0) & (dist < SWA_WINDOW)
    sink_tab = sink_ref[...]
    outs = [None] * SWA_Q_HEADS
    for c in range(SWA_KV_HEADS):
        pieces, sinks = [], []
        for g in range(group):
            head = c * group + g
            blk = q_ref[0, :, (head // 2) * PAIR_W:(head // 2 + 1) * PAIR_W] * ATTN_SCALE
            hh = head % 2
            qm = jnp.where((lane >= hh * HEAD_DIM) & (lane < (hh + 1) * HEAD_DIM), blk, jnp.zeros_like(blk))
            if hh != c:
                qm = pltpu.roll(qm.astype(F32), HEAD_DIM, 1).astype(BF16)
            pieces.append(qm)
            sinks.append(jnp.broadcast_to(sink_tab[head:head + 1, :1], (tq, 1)))
        qs = jnp.concatenate(pieces, axis=0)
        sink = jnp.concatenate(sinks, axis=0)
        s = jnp.where(allowed, _dot_nt(qs, k), NEG_INF)
        m = jnp.maximum(jnp.max(s, axis=1, keepdims=True), sink)
        p = jnp.exp(s - m)
        l = jnp.sum(p, axis=1, keepdims=True) + jnp.exp(sink - m)
        o = _dot(p.astype(BF16), v) / l
        for g in range(group):
            head = c * group + g
            oh = o[g * tq:(g + 1) * tq]
            outs[head] = oh if head % 2 == c else pltpu.roll(oh, HEAD_DIM, 1)
    for pp in range(SWA_Q_HEADS // 2):
        o_ref[0, :, pp * PAIR_W:(pp + 1) * PAIR_W] = jnp.where(
            lane < HEAD_DIM, outs[2 * pp], outs[2 * pp + 1]).astype(o_ref.dtype)


def _swa(q, k, v, sinks, *, batch, seq, tq=SWA_WINDOW):
    W = q.shape[1]
    assert tq & (tq - 1) == 0
    q3 = q.reshape(batch, seq, W)
    k3, v3 = (t.reshape(batch, seq, PAIR_W) for t in (k, v))
    sink_tab = jnp.broadcast_to(sinks.astype(F32)[:, None], (SWA_Q_HEADS, LANES))
    out = pl.pallas_call(
        _swa_kernel,
        grid=(batch, seq // tq),
        in_specs=[pl.BlockSpec((1, tq, W), lambda b, i: (b, i, 0)),
                  pl.BlockSpec((1, seq, PAIR_W), lambda b, i: (b, 0, 0)),
                  pl.BlockSpec((1, seq, PAIR_W), lambda b, i: (b, 0, 0)),
                  pl.BlockSpec((SWA_Q_HEADS, LANES), lambda b, i: (0, 0))],
        out_specs=pl.BlockSpec((1, tq, W), lambda b, i: (b, i, 0)),
        out_shape=jax.ShapeDtypeStruct((batch, seq, W), BF16),
        compiler_params=_cparams(("parallel", "parallel")),
        name="swa",
    )(q3, k3, v3, sink_tab)
    return out.reshape(batch * seq, W)


def _outproj_kernel(*refs, n_parts):
    parts = refs[:n_parts]
    w_ref, x_ref, o_ref = refs[n_parts:]
    y = x_ref[...]
    off = 0
    for p_ref in parts:
        kw = p_ref.shape[1]
        y = y + _dot(p_ref[...], w_ref[off:off + kw, :])
        off += kw
    o_ref[...] = y


def _outproj(parts, w, x, tm=512):
    T, D = x.shape
    in_specs = [pl.BlockSpec((tm, p.shape[1]), lambda i: (i, 0)) for p in parts]
    in_specs += [pl.BlockSpec(w.shape, lambda i: (0, 0)),
                 pl.BlockSpec((tm, D), lambda i: (i, 0))]
    return pl.pallas_call(
        functools.partial(_outproj_kernel, n_parts=len(parts)),
        grid=(T // tm,),
        in_specs=in_specs,
        out_specs=pl.BlockSpec((tm, D), lambda i: (i, 0)),
        out_shape=jax.ShapeDtypeStruct((T, D), F32),
        compiler_params=_cparams(("parallel",)),
        name="outproj",
    )(*parts, w, x)


_CAND_ROWS = 80


def _cand_tables():
    pos = np.zeros((_CAND_ROWS,), np.float32)
    neg = np.zeros((_CAND_ROWS,), np.float32)
    r = 0
    for a, nb in ((0, 16), (1, 8), (2, 8), (3, 8), (4, 8), (5, 8), (6, 8), (7, 8)):
        for b in range(nb):
            pos[r] = a * PEER_TOPK + b
            neg[r] = 0.0 if (a + 1) * (b + 1) <= PEER_TOPK else -np.inf
            r += 1
    for a in range(8, 16):
        pos[r] = a * PEER_TOPK
        r += 1
    assert r == _CAND_ROWS
    tab = lambda t: jnp.asarray(np.broadcast_to(t[:, None], (_CAND_ROWS, LANES)).copy())
    return tab(pos), tab(neg)


_ROUTE_HEADS_PER_STEP = 2


def _extract_sorted(scores, by_key):
    nk, lanes = scores[0].shape
    kio = lax.broadcasted_iota(jnp.int32, (nk, lanes), 0).astype(F32)
    slot = lax.broadcasted_iota(jnp.int32, (PEER_TOPK, lanes), 0)

    def body(a, carry):
        here = slot == a
        out = []
        for (v, vals, aux), ranked in zip(carry, by_key):
            m = jnp.max(v, axis=0, keepdims=True)
            idx = jnp.min(jnp.where(v == m, kio, float(nk)), axis=0, keepdims=True)
            hit = kio == idx
            aux = jnp.where(hit, jnp.asarray(a, F32), aux) if ranked else jnp.where(here, idx, aux)
            out.append((jnp.where(hit, -jnp.inf, v), jnp.where(here, m, vals), aux))
        return tuple(out)

    small = jnp.zeros((PEER_TOPK, lanes), F32)
    unranked = jnp.full((nk, lanes), float(PEER_TOPK), F32)
    init = tuple((v, small, unranked if ranked else small) for v, ranked in zip(scores, by_key))
    return [(vals, aux) for _, vals, aux in lax.fori_loop(0, PEER_TOPK, body, init)]


def _route_kernel(h_ref, wq_ref, keys_ref, pos_ref, neg_ref,
                  c1_ref, e1_ref, r2_ref, e2_ref, qt_ref):
    half = PEER_QUERY_DIM // 2
    lanes = h_ref.shape[0]
    qt_ref[...] = _dot_nt(wq_ref[...], h_ref[...]).astype(BF16)
    pos = pos_ref[...]
    neg = neg_ref[...]
    slot = lax.broadcasted_iota(jnp.int32, (PEER_TOPK, lanes), 0)
    kio = lax.broadcasted_iota(jnp.int32, (PEER_N_KEYS, lanes), 0).astype(F32)
    nh = _ROUTE_HEADS_PER_STEP

    def heads_body(hp, _):
        heads = [hp * nh + i for i in range(nh)]
        scores = []
        for h in heads:
            r0 = pl.multiple_of(h * PEER_QUERY_DIM, PEER_QUERY_DIM)
            scores.append(_dot(keys_ref[2 * h], qt_ref[pl.ds(r0, half), :]))
            scores.append(_dot(keys_ref[2 * h + 1], qt_ref[pl.ds(r0 + half, half), :]))
        tops = []
        for i in range(nh):
            tops += _extract_sorted(scores[2 * i:2 * i + 2], [False, True])

        cands = []
        for i in range(nh):
            v1, v2 = tops[2 * i][0], tops[2 * i + 1][0]
            blocks = [v1[0:1] + v2[0:8], v1[0:1] + v2[8:16]]
            blocks += [v1[a:a + 1] + v2[0:8] for a in range(1, 8)]
            blocks += [v1[8:16] + v2[0:1]]
            cands.append(jnp.concatenate(blocks, axis=0) + neg)

        def pick(kk, carry):
            here = slot == kk
            out = []
            for cand, chosen, ts in carry:
                m = jnp.max(cand, axis=0, keepdims=True)
                first = jnp.min(jnp.where(cand == m, pos, 1e9), axis=0, keepdims=True)
                hit = pos == first
                out.append((jnp.where(hit, -jnp.inf, cand), jnp.where(hit, 1.0, chosen),
                            jnp.where(here, m, ts)))
            return tuple(out)

        picked = lax.fori_loop(0, PEER_TOPK, pick,
                               tuple((c, jnp.zeros_like(c), jnp.zeros((PEER_TOPK, lanes), F32)) for c in cands))
        for i, h in enumerate(heads):
            (v1, idx1), (v2, rank2) = tops[2 * i], tops[2 * i + 1]
            _, chosen, ts = picked[i]
            z = jnp.sum(jnp.exp(ts - ts[0:1]), axis=0, keepdims=True)
            counts = [jnp.sum(chosen[0:16], axis=0, keepdims=True)]
            counts += [jnp.sum(chosen[8 * a + 8:8 * a + 16], axis=0, keepdims=True) for a in range(1, 8)]
            counts += [chosen[72 + a:73 + a] for a in range(8)]
            c1 = jnp.zeros((PEER_N_KEYS, lanes), F32)
            for a in range(PEER_TOPK):
                c1 = jnp.where(kio == idx1[a:a + 1], counts[a], c1)
            c1_ref[h] = c1
            e1_ref[h] = jnp.exp(scores[2 * i] - v1[0:1]) / z
            r2_ref[h] = rank2.astype(BF16)
            e2_ref[h] = jnp.exp(scores[2 * i + 1] - v2[0:1]).astype(BF16)
        return 0

    lax.fori_loop(0, PEER_HEADS // nh, heads_body, 0)


def _peer_route(h2, wq_t, keys, tt=LANES):
    T, D = h2.shape
    pos, neg = _cand_tables()
    stat_spec = pl.BlockSpec((PEER_HEADS, PEER_N_KEYS, tt), lambda i: (0, 0, i))
    stat = lambda dt: jax.ShapeDtypeStruct((PEER_HEADS, PEER_N_KEYS, T), dt)
    return pl.pallas_call(
        _route_kernel,
        grid=(T // tt,),
        in_specs=[pl.BlockSpec((tt, D), lambda i: (i, 0)),
                  pl.BlockSpec(wq_t.shape, lambda i: (0, 0)),
                  pl.BlockSpec(keys.shape, lambda i: (0, 0, 0)),
                  pl.BlockSpec((_CAND_ROWS, LANES), lambda i: (0, 0)),
                  pl.BlockSpec((_CAND_ROWS, LANES), lambda i: (0, 0))],
        out_specs=[stat_spec] * 4,
        out_shape=[stat(F32), stat(F32), stat(BF16), stat(BF16)],
        scratch_shapes=[pltpu.VMEM((PEER_HEADS * PEER_QUERY_DIM, tt), BF16)],
        compiler_params=_cparams(("parallel",)),
        name="peer_route",
    )(h2, wq_t, keys, pos, neg)


_KEY_GROUP = 8
_UNITS = 4


def _build_gated(a_ref, p_ref, c1_ref, e1_ref, r2_ref, e2_ref, key0, g0, ng, lt):
    rep = PEER_N_KEYS // 16

    def rows16(row):
        blk = jnp.broadcast_to(row, (16, LANES)).astype(BF16)
        return jnp.concatenate([blk] * rep, axis=0)

    ls = slice(lt * LANES, (lt + 1) * LANES)
    w = [jnp.zeros((PEER_N_KEYS, LANES), BF16) for _ in range(ng)]
    for h in range(PEER_HEADS):
        c1 = c1_ref[h, pl.ds(key0, _KEY_GROUP), ls]
        e1 = e1_ref[h, pl.ds(key0, _KEY_GROUP), ls]
        r2 = r2_ref[h, :, ls]
        e2 = e2_ref[h, :, ls]
        for g in range(ng):
            thr = rows16(c1[g0 + g:g0 + g + 1])
            gate = rows16(e1[g0 + g:g0 + g + 1])
            w[g] = w[g] + jnp.where(r2 < thr, e2, jnp.zeros_like(e2)) * gate
    for g in range(g0, g0 + ng):
        rs = slice(g * PEER_N_KEYS, (g + 1) * PEER_N_KEYS)
        a = a_ref[rs, ls]
        gelu = 0.5 * a * (1.0 + lax.erf(a * (2.0 ** -0.5)))
        p_ref[rs, ls] = gelu.astype(BF16) * w[g - g0]


def _experts_kernel(h_ref, dn_ref, upt_ref, c1_ref, e1_ref, r2_ref, e2_ref, x_ref, o_ref,
                    a_ref, p_ref, acc_ref):
    te, tt = a_ref.shape
    d_model = acc_ref.shape[0]
    e = pl.program_id(1)
    n_tiles = pl.num_programs(1) - 1
    cur = e % 2
    ng = _KEY_GROUP // _UNITS
    n_lane = tt // LANES
    key0 = pl.multiple_of(jnp.minimum(e, n_tiles - 1) * _KEY_GROUP, _KEY_GROUP)

    def front_mm(u):
        rows = te // _UNITS
        rs = slice(u * rows, (u + 1) * rows)
        a_ref[rs, :] = _dot_nt(dn_ref[rs, :], h_ref[...])

    def back_mm(u):
        rows = d_model // _UNITS
        rs = slice(u * rows, (u + 1) * rows)
        acc_ref[rs, :] += _dot(upt_ref[rs, :], p_ref[1 - cur])

    def run(front, back):
        if front:
            front_mm(0)
        for u in range(_UNITS):
            for lt in range(n_lane):
                if front:
                    _build_gated(a_ref, p_ref.at[cur], c1_ref, e1_ref, r2_ref, e2_ref, key0, u * ng, ng, lt)
                if front and lt == 0 and u + 1 < _UNITS:
                    front_mm(u + 1)
                if back and lt == n_lane // 2:
                    back_mm(u)

    @pl.when(e == 0)
    def _():
        acc_ref[...] = jnp.zeros_like(acc_ref)
        run(True, False)

    @pl.when((e > 0) & (e < n_tiles))
    def _():
        run(True, True)

    @pl.when(e == n_tiles)
    def _():
        run(False, True)
        o_ref[...] = x_ref[...] + acc_ref[...].T


def _peer_experts(h2, down, up_t, stats, x, tt=512):
    T, D = h2.shape
    E = down.shape[0]
    te = _KEY_GROUP * PEER_N_KEYS
    n_tiles = E // te
    stat_spec = pl.BlockSpec((PEER_HEADS, PEER_N_KEYS, tt), lambda i, e: (0, 0, i))
    return pl.pallas_call(
        _experts_kernel,
        grid=(T // tt, n_tiles + 1),
        in_specs=[pl.BlockSpec((tt, D), lambda i, e: (i, 0)),
                  pl.BlockSpec((te, D), lambda i, e: (jnp.minimum(e, n_tiles - 1), 0)),
                  pl.BlockSpec((D, te), lambda i, e: (0, jnp.maximum(e - 1, 0))),
                  stat_spec, stat_spec, stat_spec, stat_spec,
                  pl.BlockSpec((tt, D), lambda i, e: (i, 0))],
        out_specs=pl.BlockSpec((tt, D), lambda i, e: (i, 0)),
        out_shape=jax.ShapeDtypeStruct((T, D), F32),
        scratch_shapes=[pltpu.VMEM((te, tt), F32),
                        pltpu.VMEM((2, te, tt), BF16),
                        pltpu.VMEM((D, tt), F32)],
        compiler_params=_cparams(("parallel", "arbitrary")),
        name="peer_experts",
    )(h2, down, up_t, *stats, x)


def _peer_layer(x, norm_gain, w_query, sub_keys, down, up):
    h2 = _rmsnorm(x, norm_gain)
    keys = sub_keys.reshape(PEER_HEADS * 2, PEER_N_KEYS, PEER_QUERY_DIM // 2).astype(BF16)
    stats = _peer_route(h2, w_query.T.astype(BF16), keys)
    return _peer_experts(h2, down.astype(BF16), up.T.astype(BF16), stats, x)


def _tile_heads(g, n):
    return jnp.tile(g.astype(F32), n)


def _even_mixer(x, norm_gain, w_in, f_bias, qn_a, kn_a, qn_b, kn_b, w_out, tabs, *, batch, seq):
    h = _rmsnorm(x, norm_gain)
    w = w_in.astype(BF16)
    o_qa, o_ka, o_va, o_qb, o_kb, o_vb, o_gb, o_fb = (
        0, A_W, 2 * A_W, 3 * A_W, 3 * A_W + B_W, 3 * A_W + 2 * B_W, 3 * A_W + 3 * B_W, 3 * A_W + 4 * B_W)
    gain_a = jnp.concatenate([_tile_heads(qn_a, MOBA_HEADS), _tile_heads(kn_a, MOBA_HEADS)])
    qk_a, km = _proj(h, w[:, o_qa:o_va], seq=seq, tn=A_W, tm=MOBA_BLOCK, gain=gain_a,
                     rope_tabs=tabs, kmean=True)
    nb = seq // MOBA_BLOCK
    kmean = km.reshape(batch, nb, 2 * A_W)[:, :, A_W:]
    kmean = jnp.pad(kmean, ((0, 0), (0, LANES - nb), (0, 0))).astype(BF16)
    gain_b = jnp.concatenate([_tile_heads(qn_b, FOX_HEADS), _tile_heads(kn_b, FOX_HEADS)])
    qk_b = _proj(h, w[:, o_qb:o_vb], seq=seq, tn=B_W, gain=gain_b)
    w_plain = jnp.concatenate([w[:, o_va:o_qb], w[:, o_vb:o_fb]], axis=1)
    vvg = _proj(h, w_plain, seq=seq, tn=A_W)
    va, vb, gb = vvg[:, :A_W], vvg[:, A_W:A_W + B_W], vvg[:, A_W + B_W:]
    frow = _fox_gates(h, w[:, o_fb:].T, f_bias, batch=batch, seq=seq)
    fcol = jnp.pad(frow.transpose(0, 2, 1), ((0, 0), (0, 0), (0, LANES - FOX_HEADS)))
    oa = _moba(qk_a[:, :A_W], qk_a[:, A_W:], va, kmean, batch=batch, seq=seq)
    ob = _fox(qk_b[:, :B_W], qk_b[:, B_W:], vb, gb, fcol, frow, batch=batch, seq=seq)
    return _outproj([oa, ob], w_out.astype(BF16), x)


def _odd_mixer(x, norm_gain, w_in, qn, kn, sinks, w_out, tabs, *, batch, seq):
    h = _rmsnorm(x, norm_gain)
    w = w_in.astype(BF16)
    qw = SWA_Q_HEADS * HEAD_DIM
    kw = SWA_KV_HEADS * HEAD_DIM
    q = _proj(h, w[:, :qw], seq=seq, tn=512, gain=_tile_heads(qn, SWA_Q_HEADS), rope_tabs=tabs)
    k = _proj(h, w[:, qw:qw + kw], seq=seq, tn=kw, gain=_tile_heads(kn, SWA_KV_HEADS), rope_tabs=tabs)
    v = _proj(h, w[:, qw + kw:], seq=seq, tn=kw)
    o = _swa(q, k, v, sinks, batch=batch, seq=seq)
    return _outproj([o], w_out.astype(BF16), x)


def kernel(x, attn_norm, ffn_norm, ev_w_in, ev_forget_bias, ev_q_norm_a, ev_k_norm_a, ev_q_norm_b,
           ev_k_norm_b, ev_w_out, od_w_in, od_q_norm, od_k_norm, od_sinks, od_w_out,
           peer_w_query, peer_sub_keys, peer_down, peer_up):
    batch, seq, d_model = x.shape
    depth = attn_norm.shape[0]
    tabs = _rope_tables(seq)
    xt = x.reshape(batch * seq, d_model)
    for l in range(depth):
        i = l // 2
        if l % 2 == 0:
            xt = _even_mixer(xt, attn_norm[l], ev_w_in[i], ev_forget_bias[i], ev_q_norm_a[i],
                             ev_k_norm_a[i], ev_q_norm_b[i], ev_k_norm_b[i], ev_w_out[i], tabs,
                             batch=batch, seq=seq)
        else:
            xt = _odd_mixer(xt, attn_norm[l], od_w_in[i], od_q_norm[i], od_k_norm[i], od_sinks[i],
                            od_w_out[i], tabs, batch=batch, seq=seq)
        xt = _peer_layer(xt, ffn_norm[l], peer_w_query[l], peer_sub_keys[l], peer_down[l], peer_up[l])
    return xt.reshape(batch, seq, d_model)
```

```python
import functools

import numpy as np
import jax
import jax.numpy as jnp
from jax import lax
from jax.experimental import pallas as pl
from jax.experimental.pallas import tpu as pltpu

F32 = jnp.float32
BF16 = jnp.bfloat16

HEAD_DIM = 64
ROT_DIM = HEAD_DIM // 4
ROPE_THETA = 500000.0
ATTN_SCALE = HEAD_DIM ** -0.5
EPS = 1e-6
NEG_INF = -1e30

MOBA_HEADS = 8
FOX_HEADS = 8
MOBA_BLOCK = 256
MOBA_TOPK = 3
A_W = MOBA_HEADS * HEAD_DIM
B_W = FOX_HEADS * HEAD_DIM

SWA_Q_HEADS = 16
SWA_KV_HEADS = 2
SWA_WINDOW = 128

PEER_HEADS = 8
PEER_N_KEYS = 128
PEER_TOPK = 16
PEER_QUERY_DIM = 128

LANES = 128
PAIR_W = 2 * HEAD_DIM
VMEM_LIMIT = 48 * 1024 * 1024


def _cparams(sem):
    return pltpu.CompilerParams(dimension_semantics=sem, vmem_limit_bytes=VMEM_LIMIT)


def _dot_nt(a, b):
    return lax.dot_general(a, b, (((1,), (1,)), ((), ())), preferred_element_type=F32)


def _dot(a, b):
    return jnp.dot(a, b, preferred_element_type=F32)


def _split3(x):
    h1 = x.astype(BF16)
    r1 = x - h1.astype(F32)
    h2 = r1.astype(BF16)
    h3 = (r1 - h2.astype(F32)).astype(BF16)
    return h1, h2, h3


def _rmsnorm_kernel(x_ref, g_ref, o_ref):
    x = x_ref[...]
    ms = jnp.mean(x * x, axis=-1, keepdims=True)
    o_ref[...] = (x * lax.rsqrt(ms + EPS) * g_ref[...]).astype(o_ref.dtype)


def _rmsnorm(x, gain, tm=512):
    T, D = x.shape
    return pl.pallas_call(
        _rmsnorm_kernel,
        grid=(T // tm,),
        in_specs=[pl.BlockSpec((tm, D), lambda i: (i, 0)),
                  pl.BlockSpec((1, D), lambda i: (0, 0))],
        out_specs=pl.BlockSpec((tm, D), lambda i: (i, 0)),
        out_shape=jax.ShapeDtypeStruct((T, D), BF16),
        compiler_params=_cparams(("parallel",)),
        name="rmsnorm",
    )(x, gain.reshape(1, D))


def _proj_kernel(*refs, norm, rope, kmean, tn):
    it = iter(refs)
    h_ref, w_ref = next(it), next(it)
    gain_ref = next(it) if norm else None
    bd_ref = next(it) if norm else None
    if rope:
        c_ref, sa_ref, sb_ref = next(it), next(it), next(it)
    o_ref = next(it)
    km_ref = next(it) if kmean else None

    y = _dot(h_ref[...], w_ref[...])
    if norm:
        y2 = y * y
        bd = bd_ref[...]
        cols = []
        for c in range(tn // LANES):
            h1, h2, h3 = _split3(y2[:, c * LANES:(c + 1) * LANES])
            cols.append(_dot(h1, bd) + _dot(h2, bd) + _dot(h3, bd))
        ms = cols[0] if len(cols) == 1 else jnp.concatenate(cols, axis=1)
        y = y * lax.rsqrt(ms + EPS) * gain_ref[...]
    if rope:
        rep = tn // LANES
        tile = (lambda t: t) if rep == 1 else (lambda t: jnp.concatenate([t] * rep, axis=1))
        y = (y * tile(c_ref[...])
             + pltpu.roll(y, tn - ROT_DIM // 2, 1) * tile(sa_ref[...])
             + pltpu.roll(y, ROT_DIM // 2, 1) * tile(sb_ref[...]))
    o_ref[...] = y.astype(o_ref.dtype)
    if kmean:
        km_ref[0] = jnp.mean(y, axis=0, keepdims=True)


def _proj(h, w, *, seq, tn, tm=512, gain=None, rope_tabs=None, kmean=False):
    T, D = h.shape
    N = w.shape[1]
    norm = gain is not None
    rope = rope_tabs is not None
    nseq = seq // tm
    in_specs = [pl.BlockSpec((tm, D), lambda i, j: (i, 0)),
                pl.BlockSpec((D, tn), lambda i, j: (0, j))]
    args = [h, w]
    if norm:
        bd = np.kron(np.eye(LANES // HEAD_DIM), np.ones((HEAD_DIM, HEAD_DIM))) / HEAD_DIM
        in_specs += [pl.BlockSpec((1, tn), lambda i, j: (0, j)),
                     pl.BlockSpec((LANES, LANES), lambda i, j: (0, 0))]
        args += [gain.reshape(1, N).astype(F32), jnp.asarray(bd, BF16)]
    if rope:
        in_specs += [pl.BlockSpec((tm, LANES), lambda i, j: (i % nseq, 0))] * 3
        args += list(rope_tabs)
    out_specs = [pl.BlockSpec((tm, tn), lambda i, j: (i, j))]
    out_shape = [jax.ShapeDtypeStruct((T, N), BF16)]
    if kmean:
        out_specs.append(pl.BlockSpec((1, 1, tn), lambda i, j: (i, 0, j)))
        out_shape.append(jax.ShapeDtypeStruct((T // tm, 1, N), F32))
    res = pl.pallas_call(
        functools.partial(_proj_kernel, norm=norm, rope=rope, kmean=kmean, tn=tn),
        grid=(T // tm, N // tn),
        in_specs=in_specs, out_specs=out_specs, out_shape=out_shape,
        compiler_params=_cparams(("parallel", "parallel")),
        name="proj",
    )(*args)
    return res if kmean else res[0]


def _rope_tables(seq):
    half = ROT_DIM // 2
    inv_freq = jnp.power(ROPE_THETA, -jnp.arange(0, ROT_DIM, 2, dtype=F32) / ROT_DIM)
    ang = jnp.arange(seq, dtype=F32)[:, None] * inv_freq[None, :]
    cos, sin = jnp.cos(ang), jnp.sin(ang)
    one = jnp.ones((seq, HEAD_DIM - ROT_DIM), F32)
    zero = jnp.zeros((seq, HEAD_DIM - ROT_DIM), F32)
    z8 = jnp.zeros((seq, half), F32)
    c = jnp.concatenate([cos, cos, one], axis=1)
    sa = jnp.concatenate([-sin, z8, zero], axis=1)
    sb = jnp.concatenate([z8, sin, zero], axis=1)
    rep = LANES // HEAD_DIM
    return tuple(jnp.concatenate([t] * rep, axis=1) for t in (c, sa, sb))


def _gates_kernel(h_ref, wf_ref, b_ref, tri_ref, o_ref, carry_ref):
    @pl.when(pl.program_id(1) == 0)
    def _():
        carry_ref[...] = jnp.zeros_like(carry_ref)

    z = _dot_nt(wf_ref[...], h_ref[...]) + b_ref[...][:, :1]
    lf = jnp.minimum(z, 0.0) - jnp.log1p(jnp.exp(-jnp.abs(z)))
    tri = tri_ref[...]
    h1, h2, h3 = _split3(lf)
    cs = _dot(h1, tri) + _dot(h2, tri) + _dot(h3, tri) + carry_ref[...][:, :1]
    o_ref[0] = cs
    carry_ref[...] = jnp.broadcast_to(cs[:, -1:], carry_ref.shape)


def _fox_gates(h, wf_t, bias, *, batch, seq, tm=512):
    T, D = h.shape
    nh = wf_t.shape[0]
    nseq = seq // tm
    tri = jnp.asarray(np.triu(np.ones((tm, tm))), BF16)
    return pl.pallas_call(
        _gates_kernel,
        grid=(batch, nseq),
        in_specs=[pl.BlockSpec((tm, D), lambda b, s: (b * nseq + s, 0)),
                  pl.BlockSpec((nh, D), lambda b, s: (0, 0)),
                  pl.BlockSpec((nh, LANES), lambda b, s: (0, 0)),
                  pl.BlockSpec((tm, tm), lambda b, s: (0, 0))],
        out_specs=pl.BlockSpec((1, nh, tm), lambda b, s: (b, 0, s)),
        out_shape=jax.ShapeDtypeStruct((batch, nh, seq), F32),
        scratch_shapes=[pltpu.VMEM((nh, LANES), F32)],
        compiler_params=_cparams(("parallel", "arbitrary")),
        name="fox_gates",
    )(h, wf_t, jnp.broadcast_to(bias.astype(F32)[:, None], (nh, LANES)), tri)


def _lane_tile(x, width):
    rep = width // LANES
    return x if rep == 1 else jnp.concatenate([x] * rep, axis=1)


def _flash_init(m_ref, acc_ref):
    m_ref[...] = jnp.full(m_ref.shape, NEG_INF, F32)
    acc_ref[...] = jnp.zeros(acc_ref.shape, F32)


def _head_values(v):
    lane = lax.broadcasted_iota(jnp.int32, v.shape, 1)
    return [jnp.where((lane >= hh * HEAD_DIM) & (lane < (hh + 1) * HEAD_DIM), v, jnp.ones_like(v))
            for hh in range(2)]


def _flash_update(slot, s, v, m_ref, acc_ref):
    tk = s.shape[1]
    m_prev = m_ref[slot]
    m_new = jnp.maximum(m_prev, jnp.max(s, axis=1, keepdims=True))
    alpha = jnp.exp(m_prev - m_new)
    p = jnp.exp(s - _lane_tile(m_new, tk))
    acc_ref[slot] = alpha * acc_ref[slot] + _dot(p.astype(BF16), v)
    m_ref[slot] = m_new


def _flash_finish(lane, acc_ref):
    outs = []
    for hh in range(2):
        acc = acc_ref[hh]
        den = (1 - hh) * HEAD_DIM
        outs.append(acc / acc[:, den:den + 1])
    return jnp.where(lane < HEAD_DIM, outs[0], outs[1])


def _head_queries(q, lane):
    qs = q * ATTN_SCALE
    return [jnp.where((lane >= hh * HEAD_DIM) & (lane < (hh + 1) * HEAD_DIM), qs, jnp.zeros_like(qs))
            for hh in range(2)]


def _moba_kernel(q_ref, k_ref, v_ref, km_ref, o_ref, m_ref, acc_ref):
    tq = q_ref.shape[1]
    tk = tq
    qi = pl.program_id(2)
    lane = lax.broadcasted_iota(jnp.int32, (tq, LANES), 1)
    lane_f = lane.astype(F32)
    rowv = lax.broadcasted_iota(jnp.int32, (tq, LANES), 0)
    row_blk = 2 * qi + (rowv >= MOBA_BLOCK).astype(jnp.int32)
    row = lax.broadcasted_iota(jnp.int32, (tq, tk), 0)
    col = lax.broadcasted_iota(jnp.int32, (tq, tk), 1)
    qh = _head_queries(q_ref[0], lane)
    _flash_init(m_ref, acc_ref)

    sels = []
    for hh in range(2):
        gate = _dot_nt(qh[hh], km_ref[0])
        gate = jnp.where(lane < row_blk, gate, -jnp.inf)
        sel = jnp.zeros((tq, LANES), F32)
        for _ in range(MOBA_TOPK):
            m = jnp.max(gate, axis=1, keepdims=True)
            idx = jnp.min(jnp.where(gate == m, lane_f, float(LANES)), axis=1, keepdims=True)
            hit = lane_f == idx
            sel = jnp.where(hit & (m > -jnp.inf), 1.0, sel)
            gate = jnp.where(hit, -jnp.inf, gate)
        sels.append(sel)

    def chosen(sel, blk):
        return jnp.max(jnp.where(lane == blk, sel, 0.0), axis=1, keepdims=True) > 0.0

    start = pl.multiple_of(qi * tq, tq)
    k_d = k_ref[0, pl.ds(start, tk), :]
    v_d = _head_values(v_ref[0, pl.ds(start, tk), :])
    for hh in range(2):
        visible = (col >= MOBA_BLOCK) | (row < MOBA_BLOCK) | chosen(sels[hh], 2 * qi)
        s = jnp.where((col <= row) & visible, _dot_nt(qh[hh], k_d), NEG_INF)
        _flash_update(hh, s, v_d[hh], m_ref, acc_ref)

    def body(j, carry):
        off = pl.multiple_of(j * tk, tk)
        kj = k_ref[0, pl.ds(off, tk), :]
        vh = _head_values(v_ref[0, pl.ds(off, tk), :])
        for hh in range(2):
            s = _dot_nt(qh[hh], kj)
            s = jnp.concatenate(
                [jnp.where(chosen(sels[hh], 2 * j), s[:, :MOBA_BLOCK], NEG_INF),
                 jnp.where(chosen(sels[hh], 2 * j + 1), s[:, MOBA_BLOCK:], NEG_INF)], axis=1)
            _flash_update(hh, s, vh[hh], m_ref, acc_ref)
        return carry

    lax.fori_loop(0, qi, body, 0)
    o_ref[0] = _flash_finish(lane, acc_ref).astype(o_ref.dtype)


def _flash_scratch(tq):
    return [pltpu.VMEM((2, tq, LANES), F32)] * 2


def _moba(q, k, v, kmean, *, batch, seq):
    W = q.shape[1]
    tq = 2 * MOBA_BLOCK
    q3, k3, v3 = (t.reshape(batch, seq, W) for t in (q, k, v))
    out = pl.pallas_call(
        _moba_kernel,
        grid=(batch, W // PAIR_W, seq // tq),
        in_specs=[pl.BlockSpec((1, tq, PAIR_W), lambda b, p, i: (b, i, p)),
                  pl.BlockSpec((1, seq, PAIR_W), lambda b, p, i: (b, 0, p)),
                  pl.BlockSpec((1, seq, PAIR_W), lambda b, p, i: (b, 0, p)),
                  pl.BlockSpec((1, LANES, PAIR_W), lambda b, p, i: (b, 0, p))],
        out_specs=pl.BlockSpec((1, tq, PAIR_W), lambda b, p, i: (b, i, p)),
        out_shape=jax.ShapeDtypeStruct((batch, seq, W), BF16),
        scratch_shapes=_flash_scratch(tq),
        compiler_params=_cparams(("parallel", "parallel", "parallel")),
        name="moba",
    )(q3, k3, v3, kmean)
    return out.reshape(batch * seq, W)


def _fox_kernel(q_ref, k_ref, v_ref, g_ref, fcol_ref, frow_ref, o_ref, m_ref, acc_ref):
    tq = q_ref.shape[1]
    tk = tq
    pr = pl.program_id(1)
    qi = pl.program_id(2)
    lane = lax.broadcasted_iota(jnp.int32, (tq, LANES), 1)
    row = lax.broadcasted_iota(jnp.int32, (tq, tk), 0)
    col = lax.broadcasted_iota(jnp.int32, (tq, tk), 1)
    qh = _head_queries(q_ref[0], lane)
    fcol = fcol_ref[0]
    fq = [jnp.sum(jnp.where(lane == 2 * pr + hh, fcol, 0.0), axis=1, keepdims=True) for hh in range(2)]
    _flash_init(m_ref, acc_ref)

    def tile(off, diagonal):
        kj = k_ref[0, pl.ds(off, tk), :]
        vh = _head_values(v_ref[0, pl.ds(off, tk), :])
        fk_all = frow_ref[0, :, pl.ds(off, tk)]
        sub = lax.broadcasted_iota(jnp.int32, fk_all.shape, 0)
        for hh in range(2):
            fk = jnp.sum(jnp.where(sub == 2 * pr + hh, fk_all, 0.0), axis=0, keepdims=True)
            s = _dot_nt(qh[hh], kj) + fq[hh] - fk
            if diagonal:
                s = jnp.where(col <= row, s, NEG_INF)
            _flash_update(hh, s, vh[hh], m_ref, acc_ref)

    tile(pl.multiple_of(qi * tq, tq), True)

    def body(j, carry):
        tile(pl.multiple_of(j * tk, tk), False)
        return carry

    lax.fori_loop(0, qi, body, 0)
    o = _flash_finish(lane, acc_ref)
    o_ref[0] = (o * jax.nn.sigmoid(g_ref[0].astype(F32))).astype(o_ref.dtype)


def _fox(q, k, v, g, fcol, frow, *, batch, seq, tq=512):
    W = q.shape[1]
    nh = frow.shape[1]
    q3, k3, v3, g3 = (t.reshape(batch, seq, W) for t in (q, k, v, g))
    out = pl.pallas_call(
        _fox_kernel,
        grid=(batch, W // PAIR_W, seq // tq),
        in_specs=[pl.BlockSpec((1, tq, PAIR_W), lambda b, p, i: (b, i, p)),
                  pl.BlockSpec((1, seq, PAIR_W), lambda b, p, i: (b, 0, p)),
                  pl.BlockSpec((1, seq, PAIR_W), lambda b, p, i: (b, 0, p)),
                  pl.BlockSpec((1, tq, PAIR_W), lambda b, p, i: (b, i, p)),
                  pl.BlockSpec((1, tq, LANES), lambda b, p, i: (b, i, 0)),
                  pl.BlockSpec((1, nh, seq), lambda b, p, i: (b, 0, 0))],
        out_specs=pl.BlockSpec((1, tq, PAIR_W), lambda b, p, i: (b, i, p)),
        out_shape=jax.ShapeDtypeStruct((batch, seq, W), BF16),
        scratch_shapes=_flash_scratch(tq),
        compiler_params=_cparams(("parallel", "parallel", "parallel")),
        name="fox",
    )(q3, k3, v3, g3, fcol, frow)
    return out.reshape(batch * seq, W)


def _swa_kernel(q_ref, k_ref, v_ref, sink_ref, bias_ref, o_ref):
    tq = q_ref.shape[1]
    qi = pl.program_id(1)
    group = SWA_Q_HEADS // SWA_KV_HEADS
    tk = tq + SWA_WINDOW
    lane = lax.broadcasted_iota(jnp.int32, (tq, LANES), 1)
    kstart = pl.multiple_of(jnp.maximum(qi * tq - SWA_WINDOW, 0), SWA_WINDOW)
    k = k_ref[0, pl.ds(kstart, tk), :]
    vh = _head_values(v_ref[0, pl.ds(kstart, tk), :])
    bias = bias_ref[jnp.minimum(qi, 1)]
    bias = jnp.concatenate([bias] * group, axis=0)
    sink_tab = sink_ref[...]
    outs = [None] * SWA_Q_HEADS
    for c in range(SWA_KV_HEADS):
        pieces, sinks = [], []
        for g in range(group):
            head = c * group + g
            blk = q_ref[0, :, (head // 2) * PAIR_W:(head // 2 + 1) * PAIR_W] * ATTN_SCALE
            hh = head % 2
            qm = jnp.where((lane >= hh * HEAD_DIM) & (lane < (hh + 1) * HEAD_DIM), blk, jnp.zeros_like(blk))
            if hh != c:
                qm = pltpu.roll(qm.astype(F32), HEAD_DIM, 1).astype(BF16)
            pieces.append(qm)
            sinks.append(jnp.broadcast_to(sink_tab[head:head + 1, :], (tq, LANES)))
        qs = jnp.concatenate(pieces, axis=0)
        sink = jnp.concatenate(sinks, axis=0)
        s = _dot_nt(qs, k) + bias
        m = jnp.maximum(sink, jnp.max(s, axis=1, keepdims=True))
        p = jnp.exp(s - _lane_tile(m, tk))
        acc = _dot(p.astype(BF16), vh[c])
        den = pltpu.roll(acc, HEAD_DIM, 1) + jnp.exp(sink - m)
        o = acc / den
        for g in range(group):
            head = c * group + g
            oh = o[g * tq:(g + 1) * tq]
            outs[head] = oh if head % 2 == c else pltpu.roll(oh, HEAD_DIM, 1)
    for pp in range(SWA_Q_HEADS // 2):
        o_ref[0, :, pp * PAIR_W:(pp + 1) * PAIR_W] = jnp.where(
            lane < HEAD_DIM, outs[2 * pp], outs[2 * pp + 1]).astype(o_ref.dtype)


def _swa_bias(tq):
    r = np.arange(tq)[:, None]
    c = np.arange(tq + SWA_WINDOW)[None, :]
    tabs = []
    for key_offset in (0, SWA_WINDOW):
        dist = r + key_offset - c
        tabs.append(np.where((dist >= 0) & (dist < SWA_WINDOW), 0.0, NEG_INF))
    return jnp.asarray(np.stack(tabs), F32)


def _swa(q, k, v, sinks, *, batch, seq, tq=SWA_WINDOW):
    W = q.shape[1]
    tk = tq + SWA_WINDOW
    q3 = q.reshape(batch, seq, W)
    k3, v3 = (t.reshape(batch, seq, PAIR_W) for t in (k, v))
    sink_tab = jnp.broadcast_to(sinks.astype(F32)[:, None], (SWA_Q_HEADS, LANES))
    out = pl.pallas_call(
        _swa_kernel,
        grid=(batch, seq // tq),
        in_specs=[pl.BlockSpec((1, tq, W), lambda b, i: (b, i, 0)),
                  pl.BlockSpec((1, seq, PAIR_W), lambda b, i: (b, 0, 0)),
                  pl.BlockSpec((1, seq, PAIR_W), lambda b, i: (b, 0, 0)),
                  pl.BlockSpec((SWA_Q_HEADS, LANES), lambda b, i: (0, 0)),
                  pl.BlockSpec((2, tq, tk), lambda b, i: (0, 0, 0))],
        out_specs=pl.BlockSpec((1, tq, W), lambda b, i: (b, i, 0)),
        out_shape=jax.ShapeDtypeStruct((batch, seq, W), BF16),
        compiler_params=_cparams(("parallel", "parallel")),
        name="swa",
    )(q3, k3, v3, sink_tab, _swa_bias(tq))
    return out.reshape(batch * seq, W)


def _rms_normed(x, gain):
    ms = jnp.mean(x * x, axis=-1, keepdims=True)
    return (x * lax.rsqrt(ms + EPS) * gain).astype(BF16)


def _outproj_kernel(*refs, n_parts):
    parts = refs[:n_parts]
    w_ref, x_ref, g_ref, o_ref, h_ref = refs[n_parts:]
    y = x_ref[...]
    off = 0
    for p_ref in parts:
        kw = p_ref.shape[1]
        y = y + _dot(p_ref[...], w_ref[off:off + kw, :])
        off += kw
    o_ref[...] = y
    h_ref[...] = _rms_normed(y, g_ref[...])


def _outproj(parts, w, x, next_gain, tm=512):
    T, D = x.shape
    in_specs = [pl.BlockSpec((tm, p.shape[1]), lambda i: (i, 0)) for p in parts]
    in_specs += [pl.BlockSpec(w.shape, lambda i: (0, 0)),
                 pl.BlockSpec((tm, D), lambda i: (i, 0)),
                 pl.BlockSpec((1, D), lambda i: (0, 0))]
    row_spec = pl.BlockSpec((tm, D), lambda i: (i, 0))
    return pl.pallas_call(
        functools.partial(_outproj_kernel, n_parts=len(parts)),
        grid=(T // tm,),
        in_specs=in_specs,
        out_specs=[row_spec, row_spec],
        out_shape=[jax.ShapeDtypeStruct((T, D), F32), jax.ShapeDtypeStruct((T, D), BF16)],
        compiler_params=_cparams(("parallel",)),
        name="outproj",
    )(*parts, w, x, next_gain.reshape(1, D).astype(F32))


_CAND_ROWS = 80


def _cand_tables():
    pos = np.zeros((_CAND_ROWS,), np.float32)
    neg = np.zeros((_CAND_ROWS,), np.float32)
    r = 0
    for a, nb in ((0, 16), (1, 8), (2, 8), (3, 8), (4, 8), (5, 8), (6, 8), (7, 8)):
        for b in range(nb):
            pos[r] = a * PEER_TOPK + b
            neg[r] = 0.0 if (a + 1) * (b + 1) <= PEER_TOPK else -np.inf
            r += 1
    for a in range(8, 16):
        pos[r] = a * PEER_TOPK
        r += 1
    assert r == _CAND_ROWS
    tab = lambda t: jnp.asarray(np.broadcast_to(t[:, None], (_CAND_ROWS, LANES)).copy())
    return tab(pos), tab(neg)


_ROUTE_HEADS_PER_STEP = 2


def _extract_sorted(scores, by_key):
    nk, lanes = scores[0].shape
    kio = lax.broadcasted_iota(jnp.int32, (nk, lanes), 0).astype(F32)
    slot = lax.broadcasted_iota(jnp.int32, (PEER_TOPK, lanes), 0)

    def body(a, carry):
        here = slot == a
        out = []
        for (v, vals, aux), ranked in zip(carry, by_key):
            m = jnp.max(v, axis=0, keepdims=True)
            idx = jnp.min(jnp.where(v == m, kio, float(nk)), axis=0, keepdims=True)
            hit = kio == idx
            aux = jnp.where(hit, jnp.asarray(a, F32), aux) if ranked else jnp.where(here, idx, aux)
            out.append((jnp.where(hit, -jnp.inf, v), jnp.where(here, m, vals), aux))
        return tuple(out)

    small = jnp.zeros((PEER_TOPK, lanes), F32)
    unranked = jnp.full((nk, lanes), float(PEER_TOPK), F32)
    init = tuple((v, small, unranked if ranked else small) for v, ranked in zip(scores, by_key))
    return [(vals, aux) for _, vals, aux in lax.fori_loop(0, PEER_TOPK, body, init)]


def _route_kernel(h_ref, wq_ref, keys_ref, pos_ref, neg_ref,
                  c1_ref, e1_ref, r2_ref, e2_ref, qt_ref):
    half = PEER_QUERY_DIM // 2
    lanes = h_ref.shape[0]
    qt_ref[...] = _dot_nt(wq_ref[...], h_ref[...]).astype(BF16)
    pos = pos_ref[...]
    neg = neg_ref[...]
    slot = lax.broadcasted_iota(jnp.int32, (PEER_TOPK, lanes), 0)
    kio = lax.broadcasted_iota(jnp.int32, (PEER_N_KEYS, lanes), 0).astype(F32)
    nh = _ROUTE_HEADS_PER_STEP

    def heads_body(hp, _):
        heads = [hp * nh + i for i in range(nh)]
        scores = []
        for h in heads:
            r0 = pl.multiple_of(h * PEER_QUERY_DIM, PEER_QUERY_DIM)
            scores.append(_dot(keys_ref[2 * h], qt_ref[pl.ds(r0, half), :]))
            scores.append(_dot(keys_ref[2 * h + 1], qt_ref[pl.ds(r0 + half, half), :]))
        tops = []
        for i in range(nh):
            tops += _extract_sorted(scores[2 * i:2 * i + 2], [False, True])

        cands = []
        for i in range(nh):
            v1, v2 = tops[2 * i][0], tops[2 * i + 1][0]
            blocks = [v1[0:1] + v2[0:8], v1[0:1] + v2[8:16]]
            blocks += [v1[a:a + 1] + v2[0:8] for a in range(1, 8)]
            blocks += [v1[8:16] + v2[0:1]]
            cands.append(jnp.concatenate(blocks, axis=0) + neg)

        def pick(kk, carry):
            here = slot == kk
            out = []
            for cand, chosen, ts in carry:
                m = jnp.max(cand, axis=0, keepdims=True)
                first = jnp.min(jnp.where(cand == m, pos, 1e9), axis=0, keepdims=True)
                hit = pos == first
                out.append((jnp.where(hit, -jnp.inf, cand), jnp.where(hit, 1.0, chosen),
                            jnp.where(here, m, ts)))
            return tuple(out)

        picked = lax.fori_loop(0, PEER_TOPK, pick,
                               tuple((c, jnp.zeros_like(c), jnp.zeros((PEER_TOPK, lanes), F32)) for c in cands))
        for i, h in enumerate(heads):
            (v1, idx1), (v2, rank2) = tops[2 * i], tops[2 * i + 1]
            _, chosen, ts = picked[i]
            z = jnp.sum(jnp.exp(ts - ts[0:1]), axis=0, keepdims=True)
            counts = [jnp.sum(chosen[0:16], axis=0, keepdims=True)]
            counts += [jnp.sum(chosen[8 * a + 8:8 * a + 16], axis=0, keepdims=True) for a in range(1, 8)]
            counts += [chosen[72 + a:73 + a] for a in range(8)]
            c1 = jnp.zeros((PEER_N_KEYS, lanes), F32)
            for a in range(PEER_TOPK):
                c1 = jnp.where(kio == idx1[a:a + 1], counts[a], c1)
            c1_ref[h] = c1
            e1_ref[h] = jnp.exp(scores[2 * i] - v1[0:1]) / z
            r2_ref[h] = rank2.astype(BF16)
            e2_ref[h] = jnp.exp(scores[2 * i + 1] - v2[0:1]).astype(BF16)
        return 0

    lax.fori_loop(0, PEER_HEADS // nh, heads_body, 0)


def _peer_route(h2, wq_t, keys, tt=LANES):
    T, D = h2.shape
    pos, neg = _cand_tables()
    stat_spec = pl.BlockSpec((PEER_HEADS, PEER_N_KEYS, tt), lambda i: (0, 0, i))
    stat = lambda dt: jax.ShapeDtypeStruct((PEER_HEADS, PEER_N_KEYS, T), dt)
    return pl.pallas_call(
        _route_kernel,
        grid=(T // tt,),
        in_specs=[pl.BlockSpec((tt, D), lambda i: (i, 0)),
                  pl.BlockSpec(wq_t.shape, lambda i: (0, 0)),
                  pl.BlockSpec(keys.shape, lambda i: (0, 0, 0)),
                  pl.BlockSpec((_CAND_ROWS, LANES), lambda i: (0, 0)),
                  pl.BlockSpec((_CAND_ROWS, LANES), lambda i: (0, 0))],
        out_specs=[stat_spec] * 4,
        out_shape=[stat(F32), stat(F32), stat(BF16), stat(BF16)],
        scratch_shapes=[pltpu.VMEM((PEER_HEADS * PEER_QUERY_DIM, tt), BF16)],
        compiler_params=_cparams(("parallel",)),
        name="peer_route",
    )(h2, wq_t, keys, pos, neg)


_KEY_GROUP = 8
_UNITS = 2


def _build_gated(a_ref, p_ref, c1_ref, e1_ref, r2_ref, e2_ref, key0, g0, ng, lt):
    rep = PEER_N_KEYS // 16

    def rows16(row):
        blk = jnp.broadcast_to(row, (16, LANES)).astype(BF16)
        return jnp.concatenate([blk] * rep, axis=0)

    ls = slice(lt * LANES, (lt + 1) * LANES)
    w = [jnp.zeros((PEER_N_KEYS, LANES), BF16) for _ in range(ng)]
    for h in range(PEER_HEADS):
        c1 = c1_ref[h, pl.ds(key0, _KEY_GROUP), ls]
        e1 = e1_ref[h, pl.ds(key0, _KEY_GROUP), ls]
        r2 = r2_ref[h, :, ls]
        e2 = e2_ref[h, :, ls]
        for g in range(ng):
            thr = rows16(c1[g0 + g:g0 + g + 1])
            gate = rows16(e1[g0 + g:g0 + g + 1])
            w[g] = w[g] + jnp.where(r2 < thr, e2, jnp.zeros_like(e2)) * gate
    for g in range(g0, g0 + ng):
        rs = slice(g * PEER_N_KEYS, (g + 1) * PEER_N_KEYS)
        a = a_ref[rs, ls]
        gelu = 0.5 * a * (1.0 + lax.erf(a * (2.0 ** -0.5)))
        p_ref[rs, ls] = gelu.astype(BF16) * w[g - g0]


def _experts_kernel(h_ref, dn_ref, upt_ref, c1_ref, e1_ref, r2_ref, e2_ref, x_ref, *rest, emit_norm):
    if emit_norm:
        g_ref, o_ref, hn_ref, a_ref, p_ref, acc_ref = rest
    else:
        o_ref, a_ref, p_ref, acc_ref = rest
    te, tt = a_ref.shape
    d_model = acc_ref.shape[0]
    e = pl.program_id(1)
    n_tiles = pl.num_programs(1) - 1
    cur = e % 2
    ng = _KEY_GROUP // _UNITS
    n_lane = tt // LANES
    key0 = pl.multiple_of(jnp.minimum(e, n_tiles - 1) * _KEY_GROUP, _KEY_GROUP)

    def front_mm(u):
        rows = te // _UNITS
        rs = slice(u * rows, (u + 1) * rows)
        a_ref[rs, :] = _dot_nt(dn_ref[rs, :], h_ref[...])

    def back_mm(u):
        rows = d_model // _UNITS
        rs = slice(u * rows, (u + 1) * rows)
        acc_ref[rs, :] += _dot(upt_ref[rs, :], p_ref[1 - cur])

    def run(front, back):
        if front:
            front_mm(0)
        for u in range(_UNITS):
            for lt in range(n_lane):
                if front:
                    _build_gated(a_ref, p_ref.at[cur], c1_ref, e1_ref, r2_ref, e2_ref, key0, u * ng, ng, lt)
                if front and lt == 0 and u + 1 < _UNITS:
                    front_mm(u + 1)
                if back and lt == n_lane // 2:
                    back_mm(u)

    @pl.when(e == 0)
    def _():
        acc_ref[...] = jnp.zeros_like(acc_ref)
        run(True, False)

    @pl.when((e > 0) & (e < n_tiles))
    def _():
        run(True, True)

    @pl.when(e == n_tiles)
    def _():
        run(False, True)
        y = x_ref[...] + acc_ref[...].T
        o_ref[...] = y
        if emit_norm:
            hn_ref[...] = _rms_normed(y, g_ref[...])


def _peer_experts(h2, down, up_t, stats, x, next_gain, tt=512):
    T, D = h2.shape
    E = down.shape[0]
    te = _KEY_GROUP * PEER_N_KEYS
    n_tiles = E // te
    emit_norm = next_gain is not None
    stat_spec = pl.BlockSpec((PEER_HEADS, PEER_N_KEYS, tt), lambda i, e: (0, 0, i))
    row_spec = pl.BlockSpec((tt, D), lambda i, e: (i, 0))
    in_specs = [row_spec,
                pl.BlockSpec((te, D), lambda i, e: (jnp.minimum(e, n_tiles - 1), 0)),
                pl.BlockSpec((D, te), lambda i, e: (0, jnp.maximum(e - 1, 0))),
                stat_spec, stat_spec, stat_spec, stat_spec,
                row_spec]
    args = [h2, down, up_t, *stats, x]
    out_specs = [row_spec]
    out_shape = [jax.ShapeDtypeStruct((T, D), F32)]
    if emit_norm:
        in_specs.append(pl.BlockSpec((1, D), lambda i, e: (0, 0)))
        args.append(next_gain.reshape(1, D).astype(F32))
        out_specs.append(row_spec)
        out_shape.append(jax.ShapeDtypeStruct((T, D), BF16))
    res = pl.pallas_call(
        functools.partial(_experts_kernel, emit_norm=emit_norm),
        grid=(T // tt, n_tiles + 1),
        in_specs=in_specs, out_specs=out_specs, out_shape=out_shape,
        scratch_shapes=[pltpu.VMEM((te, tt), F32),
                        pltpu.VMEM((2, te, tt), BF16),
                        pltpu.VMEM((D, tt), F32)],
        compiler_params=_cparams(("parallel", "arbitrary")),
        name="peer_experts",
    )(*args)
    return (res[0], res[1]) if emit_norm else (res[0], None)


def _peer_layer(x, h2, next_gain, w_query, sub_keys, down, up):
    keys = sub_keys.reshape(PEER_HEADS * 2, PEER_N_KEYS, PEER_QUERY_DIM // 2).astype(BF16)
    stats = _peer_route(h2, w_query.T.astype(BF16), keys)
    return _peer_experts(h2, down.astype(BF16), up.T.astype(BF16), stats, x, next_gain)


def _tile_heads(g, n):
    return jnp.tile(g.astype(F32), n)


def _even_mixer(x, h, ffn_gain, w_in, f_bias, qn_a, kn_a, qn_b, kn_b, w_out, tabs, *, batch, seq):
    w = w_in.astype(BF16)
    o_qa, o_ka, o_va, o_qb, o_kb, o_vb, o_gb, o_fb = (
        0, A_W, 2 * A_W, 3 * A_W, 3 * A_W + B_W, 3 * A_W + 2 * B_W, 3 * A_W + 3 * B_W, 3 * A_W + 4 * B_W)
    gain_a = jnp.concatenate([_tile_heads(qn_a, MOBA_HEADS), _tile_heads(kn_a, MOBA_HEADS)])
    qk_a, km = _proj(h, w[:, o_qa:o_va], seq=seq, tn=A_W, tm=MOBA_BLOCK, gain=gain_a,
                     rope_tabs=tabs, kmean=True)
    nb = seq // MOBA_BLOCK
    kmean = km.reshape(batch, nb, 2 * A_W)[:, :, A_W:]
    kmean = jnp.pad(kmean, ((0, 0), (0, LANES - nb), (0, 0))).astype(BF16)
    gain_b = jnp.concatenate([_tile_heads(qn_b, FOX_HEADS), _tile_heads(kn_b, FOX_HEADS)])
    qk_b = _proj(h, w[:, o_qb:o_vb], seq=seq, tn=B_W, gain=gain_b)
    w_plain = jnp.concatenate([w[:, o_va:o_qb], w[:, o_vb:o_fb]], axis=1)
    vvg = _proj(h, w_plain, seq=seq, tn=A_W)
    va, vb, gb = vvg[:, :A_W], vvg[:, A_W:A_W + B_W], vvg[:, A_W + B_W:]
    frow = _fox_gates(h, w[:, o_fb:].T, f_bias, batch=batch, seq=seq)
    fcol = jnp.pad(frow.transpose(0, 2, 1), ((0, 0), (0, 0), (0, LANES - FOX_HEADS)))
    oa = _moba(qk_a[:, :A_W], qk_a[:, A_W:], va, kmean, batch=batch, seq=seq)
    ob = _fox(qk_b[:, :B_W], qk_b[:, B_W:], vb, gb, fcol, frow, batch=batch, seq=seq)
    return _outproj([oa, ob], w_out.astype(BF16), x, ffn_gain)


def _odd_mixer(x, h, ffn_gain, w_in, qn, kn, sinks, w_out, tabs, *, batch, seq):
    w = w_in.astype(BF16)
    qw = SWA_Q_HEADS * HEAD_DIM
    kw = SWA_KV_HEADS * HEAD_DIM
    q = _proj(h, w[:, :qw], seq=seq, tn=512, gain=_tile_heads(qn, SWA_Q_HEADS), rope_tabs=tabs)
    k = _proj(h, w[:, qw:qw + kw], seq=seq, tn=kw, gain=_tile_heads(kn, SWA_KV_HEADS), rope_tabs=tabs)
    v = _proj(h, w[:, qw + kw:], seq=seq, tn=kw)
    o = _swa(q, k, v, sinks, batch=batch, seq=seq)
    return _outproj([o], w_out.astype(BF16), x, ffn_gain)


def kernel(x, attn_norm, ffn_norm, ev_w_in, ev_forget_bias, ev_q_norm_a, ev_k_norm_a, ev_q_norm_b,
           ev_k_norm_b, ev_w_out, od_w_in, od_q_norm, od_k_norm, od_sinks, od_w_out,
           peer_w_query, peer_sub_keys, peer_down, peer_up):
    batch, seq, d_model = x.shape
    depth = attn_norm.shape[0]
    tabs = _rope_tables(seq)
    xt = x.reshape(batch * seq, d_model)
    h = _rmsnorm(xt, attn_norm[0])
    for l in range(depth):
        i = l // 2
        if l % 2 == 0:
            xt, h2 = _even_mixer(xt, h, ffn_norm[l], ev_w_in[i], ev_forget_bias[i], ev_q_norm_a[i],
                                 ev_k_norm_a[i], ev_q_norm_b[i], ev_k_norm_b[i], ev_w_out[i], tabs,
                                 batch=batch, seq=seq)
        else:
            xt, h2 = _odd_mixer(xt, h, ffn_norm[l], od_w_in[i], od_q_norm[i], od_k_norm[i], od_sinks[i],
                                od_w_out[i], tabs, batch=batch, seq=seq)
        next_gain = attn_norm[l + 1] if l + 1 < depth else None
        xt, h = _peer_layer(xt, h2, next_gain, peer_w_query[l], peer_sub_keys[l], peer_down[l], peer_up[l])
    return xt.reshape(batch, seq, d_model)
```

```python
import functools

import numpy as np
import jax
import jax.numpy as jnp
from jax import lax
from jax.experimental import pallas as pl
from jax.experimental.pallas import tpu as pltpu

F32 = jnp.float32
BF16 = jnp.bfloat16

HEAD_DIM = 64
ROT_DIM = HEAD_DIM // 4
ROPE_THETA = 500000.0
ATTN_SCALE = HEAD_DIM ** -0.5
EPS = 1e-6
NEG_INF = -1e30

MOBA_HEADS = 8
FOX_HEADS = 8
MOBA_BLOCK = 256
MOBA_TOPK = 3
A_W = MOBA_HEADS * HEAD_DIM
B_W = FOX_HEADS * HEAD_DIM

SWA_Q_HEADS = 16
SWA_KV_HEADS = 2
SWA_WINDOW = 128

PEER_HEADS = 8
PEER_N_KEYS = 128
PEER_TOPK = 16
PEER_QUERY_DIM = 128

LANES = 128
PAIR_W = 2 * HEAD_DIM
VMEM_LIMIT = 48 * 1024 * 1024


def _cparams(sem):
    return pltpu.CompilerParams(dimension_semantics=sem, vmem_limit_bytes=VMEM_LIMIT)


def _dot_nt(a, b):
    return lax.dot_general(a, b, (((1,), (1,)), ((), ())), preferred_element_type=F32)


def _dot(a, b):
    return jnp.dot(a, b, preferred_element_type=F32)


def _split3(x):
    h1 = x.astype(BF16)
    r1 = x - h1.astype(F32)
    h2 = r1.astype(BF16)
    h3 = (r1 - h2.astype(F32)).astype(BF16)
    return h1, h2, h3


def _rmsnorm_kernel(x_ref, g_ref, o_ref):
    x = x_ref[...]
    ms = jnp.mean(x * x, axis=-1, keepdims=True)
    o_ref[...] = (x * lax.rsqrt(ms + EPS) * g_ref[...]).astype(o_ref.dtype)


def _rmsnorm(x, gain, tm=512):
    T, D = x.shape
    return pl.pallas_call(
        _rmsnorm_kernel,
        grid=(T // tm,),
        in_specs=[pl.BlockSpec((tm, D), lambda i: (i, 0)),
                  pl.BlockSpec((1, D), lambda i: (0, 0))],
        out_specs=pl.BlockSpec((tm, D), lambda i: (i, 0)),
        out_shape=jax.ShapeDtypeStruct((T, D), BF16),
        compiler_params=_cparams(("parallel",)),
        name="rmsnorm",
    )(x, gain.reshape(1, D))


def _proj_kernel(*refs, norm, rope, kmean, tn):
    it = iter(refs)
    h_ref, w_ref = next(it), next(it)
    gain_ref = next(it) if norm else None
    bd_ref = next(it) if norm else None
    if rope:
        c_ref, sa_ref, sb_ref = next(it), next(it), next(it)
    o_ref = next(it)
    km_ref = next(it) if kmean else None

    y = _dot(h_ref[...], w_ref[...])
    if norm:
        y2 = y * y
        bd = bd_ref[...]
        cols = []
        for c in range(tn // LANES):
            h1, h2, h3 = _split3(y2[:, c * LANES:(c + 1) * LANES])
            cols.append(_dot(h1, bd) + _dot(h2, bd) + _dot(h3, bd))
        ms = cols[0] if len(cols) == 1 else jnp.concatenate(cols, axis=1)
        y = y * lax.rsqrt(ms + EPS) * gain_ref[...]
    if rope:
        rep = tn // LANES
        tile = (lambda t: t) if rep == 1 else (lambda t: jnp.concatenate([t] * rep, axis=1))
        y = (y * tile(c_ref[...])
             + pltpu.roll(y, tn - ROT_DIM // 2, 1) * tile(sa_ref[...])
             + pltpu.roll(y, ROT_DIM // 2, 1) * tile(sb_ref[...]))
    o_ref[...] = y.astype(o_ref.dtype)
    if kmean:
        km_ref[0] = jnp.mean(y, axis=0, keepdims=True)


def _proj(h, w, *, seq, tn, tm=512, gain=None, rope_tabs=None, kmean=False):
    T, D = h.shape
    N = w.shape[1]
    norm = gain is not None
    rope = rope_tabs is not None
    nseq = seq // tm
    in_specs = [pl.BlockSpec((tm, D), lambda i, j: (i, 0)),
                pl.BlockSpec((D, tn), lambda i, j: (0, j))]
    args = [h, w]
    if norm:
        bd = np.kron(np.eye(LANES // HEAD_DIM), np.ones((HEAD_DIM, HEAD_DIM))) / HEAD_DIM
        in_specs += [pl.BlockSpec((1, tn), lambda i, j: (0, j)),
                     pl.BlockSpec((LANES, LANES), lambda i, j: (0, 0))]
        args += [gain.reshape(1, N).astype(F32), jnp.asarray(bd, BF16)]
    if rope:
        in_specs += [pl.BlockSpec((tm, LANES), lambda i, j: (i % nseq, 0))] * 3
        args += list(rope_tabs)
    out_specs = [pl.BlockSpec((tm, tn), lambda i, j: (i, j))]
    out_shape = [jax.ShapeDtypeStruct((T, N), BF16)]
    if kmean:
        out_specs.append(pl.BlockSpec((1, 1, tn), lambda i, j: (i, 0, j)))
        out_shape.append(jax.ShapeDtypeStruct((T // tm, 1, N), F32))
    res = pl.pallas_call(
        functools.partial(_proj_kernel, norm=norm, rope=rope, kmean=kmean, tn=tn),
        grid=(T // tm, N // tn),
        in_specs=in_specs, out_specs=out_specs, out_shape=out_shape,
        compiler_params=_cparams(("parallel", "parallel")),
        name="proj",
    )(*args)
    return res if kmean else res[0]


def _rope_tables(seq):
    half = ROT_DIM // 2
    inv_freq = jnp.power(ROPE_THETA, -jnp.arange(0, ROT_DIM, 2, dtype=F32) / ROT_DIM)
    ang = jnp.arange(seq, dtype=F32)[:, None] * inv_freq[None, :]
    cos, sin = jnp.cos(ang), jnp.sin(ang)
    one = jnp.ones((seq, HEAD_DIM - ROT_DIM), F32)
    zero = jnp.zeros((seq, HEAD_DIM - ROT_DIM), F32)
    z8 = jnp.zeros((seq, half), F32)
    c = jnp.concatenate([cos, cos, one], axis=1)
    sa = jnp.concatenate([-sin, z8, zero], axis=1)
    sb = jnp.concatenate([z8, sin, zero], axis=1)
    rep = LANES // HEAD_DIM
    return tuple(jnp.concatenate([t] * rep, axis=1) for t in (c, sa, sb))


def _gates_kernel(h_ref, wf_ref, b_ref, tri_ref, o_ref, carry_ref):
    @pl.when(pl.program_id(1) == 0)
    def _():
        carry_ref[...] = jnp.zeros_like(carry_ref)

    z = _dot_nt(wf_ref[...], h_ref[...]) + b_ref[...][:, :1]
    lf = jnp.minimum(z, 0.0) - jnp.log1p(jnp.exp(-jnp.abs(z)))
    tri = tri_ref[...]
    h1, h2, h3 = _split3(lf)
    cs = _dot(h1, tri) + _dot(h2, tri) + _dot(h3, tri) + carry_ref[...][:, :1]
    o_ref[0] = cs
    carry_ref[...] = jnp.broadcast_to(cs[:, -1:], carry_ref.shape)


def _fox_gates(h, wf_t, bias, *, batch, seq, tm=512):
    T, D = h.shape
    nh = wf_t.shape[0]
    nseq = seq // tm
    tri = jnp.asarray(np.triu(np.ones((tm, tm))), BF16)
    return pl.pallas_call(
        _gates_kernel,
        grid=(batch, nseq),
        in_specs=[pl.BlockSpec((tm, D), lambda b, s: (b * nseq + s, 0)),
                  pl.BlockSpec((nh, D), lambda b, s: (0, 0)),
                  pl.BlockSpec((nh, LANES), lambda b, s: (0, 0)),
                  pl.BlockSpec((tm, tm), lambda b, s: (0, 0))],
        out_specs=pl.BlockSpec((1, nh, tm), lambda b, s: (b, 0, s)),
        out_shape=jax.ShapeDtypeStruct((batch, nh, seq), F32),
        scratch_shapes=[pltpu.VMEM((nh, LANES), F32)],
        compiler_params=_cparams(("parallel", "arbitrary")),
        name="fox_gates",
    )(h, wf_t, jnp.broadcast_to(bias.astype(F32)[:, None], (nh, LANES)), tri)


def _lane_tile(x, width):
    rep = width // LANES
    return x if rep == 1 else jnp.concatenate([x] * rep, axis=1)


def _flash_init(m_ref, acc_ref):
    m_ref[...] = jnp.full(m_ref.shape, NEG_INF, F32)
    acc_ref[...] = jnp.zeros(acc_ref.shape, F32)


def _head_values(v):
    lane = lax.broadcasted_iota(jnp.int32, v.shape, 1)
    return [jnp.where((lane >= hh * HEAD_DIM) & (lane < (hh + 1) * HEAD_DIM), v, jnp.ones_like(v))
            for hh in range(2)]


def _flash_update(slot, s, v, m_ref, acc_ref):
    tk = s.shape[1]
    m_prev = m_ref[slot]
    m_new = jnp.maximum(m_prev, jnp.max(s, axis=1, keepdims=True))
    alpha = jnp.exp(m_prev - m_new)
    p = jnp.exp(s - _lane_tile(m_new, tk))
    acc_ref[slot] = alpha * acc_ref[slot] + _dot(p.astype(BF16), v)
    m_ref[slot] = m_new


def _flash_finish(lane, acc_ref):
    outs = []
    for hh in range(2):
        acc = acc_ref[hh]
        den = (1 - hh) * HEAD_DIM
        outs.append(acc / acc[:, den:den + 1])
    return jnp.where(lane < HEAD_DIM, outs[0], outs[1])


def _head_queries(q, lane):
    qs = q * ATTN_SCALE
    return [jnp.where((lane >= hh * HEAD_DIM) & (lane < (hh + 1) * HEAD_DIM), qs, jnp.zeros_like(qs))
            for hh in range(2)]


def _moba_kernel(q_ref, k_ref, v_ref, km_ref, o_ref, m_ref, acc_ref):
    tq = q_ref.shape[1]
    tk = tq
    qi = pl.program_id(2)
    lane = lax.broadcasted_iota(jnp.int32, (tq, LANES), 1)
    lane_f = lane.astype(F32)
    rowv = lax.broadcasted_iota(jnp.int32, (tq, LANES), 0)
    row_blk = 2 * qi + (rowv >= MOBA_BLOCK).astype(jnp.int32)
    row = lax.broadcasted_iota(jnp.int32, (tq, tk), 0)
    col = lax.broadcasted_iota(jnp.int32, (tq, tk), 1)
    qh = _head_queries(q_ref[0], lane)
    _flash_init(m_ref, acc_ref)

    sels = []
    for hh in range(2):
        gate = _dot_nt(qh[hh], km_ref[0])
        gate = jnp.where(lane < row_blk, gate, -jnp.inf)
        sel = jnp.zeros((tq, LANES), F32)
        for _ in range(MOBA_TOPK):
            m = jnp.max(gate, axis=1, keepdims=True)
            idx = jnp.min(jnp.where(gate == m, lane_f, float(LANES)), axis=1, keepdims=True)
            hit = lane_f == idx
            sel = jnp.where(hit & (m > -jnp.inf), 1.0, sel)
            gate = jnp.where(hit, -jnp.inf, gate)
        sels.append(sel)

    def chosen(sel, blk):
        return jnp.max(jnp.where(lane == blk, sel, 0.0), axis=1, keepdims=True) > 0.0

    start = pl.multiple_of(qi * tq, tq)
    k_d = k_ref[0, pl.ds(start, tk), :]
    v_d = _head_values(v_ref[0, pl.ds(start, tk), :])
    for hh in range(2):
        visible = (col >= MOBA_BLOCK) | (row < MOBA_BLOCK) | chosen(sels[hh], 2 * qi)
        s = jnp.where((col <= row) & visible, _dot_nt(qh[hh], k_d), NEG_INF)
        _flash_update(hh, s, v_d[hh], m_ref, acc_ref)

    def body(j, carry):
        off = pl.multiple_of(j * tk, tk)
        kj = k_ref[0, pl.ds(off, tk), :]
        vh = _head_values(v_ref[0, pl.ds(off, tk), :])
        for hh in range(2):
            s = _dot_nt(qh[hh], kj)
            s = jnp.concatenate(
                [jnp.where(chosen(sels[hh], 2 * j), s[:, :MOBA_BLOCK], NEG_INF),
                 jnp.where(chosen(sels[hh], 2 * j + 1), s[:, MOBA_BLOCK:], NEG_INF)], axis=1)
            _flash_update(hh, s, vh[hh], m_ref, acc_ref)
        return carry

    lax.fori_loop(0, qi, body, 0)
    o_ref[0] = _flash_finish(lane, acc_ref).astype(o_ref.dtype)


def _flash_scratch(tq):
    return [pltpu.VMEM((2, tq, LANES), F32)] * 2


def _moba(q, k, v, kmean, *, batch, seq):
    W = q.shape[1]
    tq = 2 * MOBA_BLOCK
    q3, k3, v3 = (t.reshape(batch, seq, W) for t in (q, k, v))
    out = pl.pallas_call(
        _moba_kernel,
        grid=(batch, W // PAIR_W, seq // tq),
        in_specs=[pl.BlockSpec((1, tq, PAIR_W), lambda b, p, i: (b, i, p)),
                  pl.BlockSpec((1, seq, PAIR_W), lambda b, p, i: (b, 0, p)),
                  pl.BlockSpec((1, seq, PAIR_W), lambda b, p, i: (b, 0, p)),
                  pl.BlockSpec((1, LANES, PAIR_W), lambda b, p, i: (b, 0, p))],
        out_specs=pl.BlockSpec((1, tq, PAIR_W), lambda b, p, i: (b, i, p)),
        out_shape=jax.ShapeDtypeStruct((batch, seq, W), BF16),
        scratch_shapes=_flash_scratch(tq),
        compiler_params=_cparams(("parallel", "parallel", "parallel")),
        name="moba",
    )(q3, k3, v3, kmean)
    return out.reshape(batch * seq, W)


def _fox_kernel(q_ref, k_ref, v_ref, g_ref, fcol_ref, frow_ref, o_ref, m_ref, acc_ref):
    tq = q_ref.shape[1]
    tk = tq
    pr = pl.program_id(1)
    qi = pl.program_id(2)
    lane = lax.broadcasted_iota(jnp.int32, (tq, LANES), 1)
    row = lax.broadcasted_iota(jnp.int32, (tq, tk), 0)
    col = lax.broadcasted_iota(jnp.int32, (tq, tk), 1)
    qh = _head_queries(q_ref[0], lane)
    fcol = fcol_ref[0]
    fq = [jnp.sum(jnp.where(lane == 2 * pr + hh, fcol, 0.0), axis=1, keepdims=True) for hh in range(2)]
    _flash_init(m_ref, acc_ref)

    def tile(off, diagonal):
        kj = k_ref[0, pl.ds(off, tk), :]
        vh = _head_values(v_ref[0, pl.ds(off, tk), :])
        fk_all = frow_ref[0, :, pl.ds(off, tk)]
        sub = lax.broadcasted_iota(jnp.int32, fk_all.shape, 0)
        for hh in range(2):
            fk = jnp.sum(jnp.where(sub == 2 * pr + hh, fk_all, 0.0), axis=0, keepdims=True)
            s = _dot_nt(qh[hh], kj) + fq[hh] - fk
            if diagonal:
                s = jnp.where(col <= row, s, NEG_INF)
            _flash_update(hh, s, vh[hh], m_ref, acc_ref)

    tile(pl.multiple_of(qi * tq, tq), True)

    def body(j, carry):
        tile(pl.multiple_of(j * tk, tk), False)
        return carry

    lax.fori_loop(0, qi, body, 0)
    o = _flash_finish(lane, acc_ref)
    o_ref[0] = (o * jax.nn.sigmoid(g_ref[0].astype(F32))).astype(o_ref.dtype)


def _fox(q, k, v, g, fcol, frow, *, batch, seq, tq=512):
    W = q.shape[1]
    nh = frow.shape[1]
    q3, k3, v3, g3 = (t.reshape(batch, seq, W) for t in (q, k, v, g))
    out = pl.pallas_call(
        _fox_kernel,
        grid=(batch, W // PAIR_W, seq // tq),
        in_specs=[pl.BlockSpec((1, tq, PAIR_W), lambda b, p, i: (b, i, p)),
                  pl.BlockSpec((1, seq, PAIR_W), lambda b, p, i: (b, 0, p)),
                  pl.BlockSpec((1, seq, PAIR_W), lambda b, p, i: (b, 0, p)),
                  pl.BlockSpec((1, tq, PAIR_W), lambda b, p, i: (b, i, p)),
                  pl.BlockSpec((1, tq, LANES), lambda b, p, i: (b, i, 0)),
                  pl.BlockSpec((1, nh, seq), lambda b, p, i: (b, 0, 0))],
        out_specs=pl.BlockSpec((1, tq, PAIR_W), lambda b, p, i: (b, i, p)),
        out_shape=jax.ShapeDtypeStruct((batch, seq, W), BF16),
        scratch_shapes=_flash_scratch(tq),
        compiler_params=_cparams(("parallel", "parallel", "parallel")),
        name="fox",
    )(q3, k3, v3, g3, fcol, frow)
    return out.reshape(batch * seq, W)


def _swa_kernel(q_ref, k_ref, v_ref, sink_ref, bias_ref, o_ref):
    tq = q_ref.shape[1]
    qi = pl.program_id(1)
    group = SWA_Q_HEADS // SWA_KV_HEADS
    tk = tq + SWA_WINDOW
    lane = lax.broadcasted_iota(jnp.int32, (tq, LANES), 1)
    kstart = pl.multiple_of(jnp.maximum(qi * tq - SWA_WINDOW, 0), SWA_WINDOW)
    k = k_ref[0, pl.ds(kstart, tk), :]
    vh = _head_values(v_ref[0, pl.ds(kstart, tk), :])
    bias = bias_ref[jnp.minimum(qi, 1)]
    bias = jnp.concatenate([bias] * group, axis=0)
    sink_tab = sink_ref[...]
    outs = [None] * SWA_Q_HEADS
    for c in range(SWA_KV_HEADS):
        pieces, sinks = [], []
        for g in range(group):
            head = c * group + g
            blk = q_ref[0, :, (head // 2) * PAIR_W:(head // 2 + 1) * PAIR_W] * ATTN_SCALE
            hh = head % 2
            qm = jnp.where((lane >= hh * HEAD_DIM) & (lane < (hh + 1) * HEAD_DIM), blk, jnp.zeros_like(blk))
            if hh != c:
                qm = pltpu.roll(qm.astype(F32), HEAD_DIM, 1).astype(BF16)
            pieces.append(qm)
            sinks.append(jnp.broadcast_to(sink_tab[head:head + 1, :], (tq, LANES)))
        qs = jnp.concatenate(pieces, axis=0)
        sink = jnp.concatenate(sinks, axis=0)
        s = _dot_nt(qs, k) + bias
        m = jnp.maximum(sink, jnp.max(s, axis=1, keepdims=True))
        p = jnp.exp(s - _lane_tile(m, tk))
        acc = _dot(p.astype(BF16), vh[c])
        den = pltpu.roll(acc, HEAD_DIM, 1) + jnp.exp(sink - m)
        o = acc / den
        for g in range(group):
            head = c * group + g
            oh = o[g * tq:(g + 1) * tq]
            outs[head] = oh if head % 2 == c else pltpu.roll(oh, HEAD_DIM, 1)
    for pp in range(SWA_Q_HEADS // 2):
        o_ref[0, :, pp * PAIR_W:(pp + 1) * PAIR_W] = jnp.where(
            lane < HEAD_DIM, outs[2 * pp], outs[2 * pp + 1]).astype(o_ref.dtype)


def _swa_bias(tq):
    r = np.arange(tq)[:, None]
    c = np.arange(tq + SWA_WINDOW)[None, :]
    tabs = []
    for key_offset in (0, SWA_WINDOW):
        dist = r + key_offset - c
        tabs.append(np.where((dist >= 0) & (dist < SWA_WINDOW), 0.0, NEG_INF))
    return jnp.asarray(np.stack(tabs), F32)


def _swa(q, k, v, sinks, *, batch, seq, tq=SWA_WINDOW):
    W = q.shape[1]
    tk = tq + SWA_WINDOW
    q3 = q.reshape(batch, seq, W)
    k3, v3 = (t.reshape(batch, seq, PAIR_W) for t in (k, v))
    sink_tab = jnp.broadcast_to(sinks.astype(F32)[:, None], (SWA_Q_HEADS, LANES))
    out = pl.pallas_call(
        _swa_kernel,
        grid=(batch, seq // tq),
        in_specs=[pl.BlockSpec((1, tq, W), lambda b, i: (b, i, 0)),
                  pl.BlockSpec((1, seq, PAIR_W), lambda b, i: (b, 0, 0)),
                  pl.BlockSpec((1, seq, PAIR_W), lambda b, i: (b, 0, 0)),
                  pl.BlockSpec((SWA_Q_HEADS, LANES), lambda b, i: (0, 0)),
                  pl.BlockSpec((2, tq, tk), lambda b, i: (0, 0, 0))],
        out_specs=pl.BlockSpec((1, tq, W), lambda b, i: (b, i, 0)),
        out_shape=jax.ShapeDtypeStruct((batch, seq, W), BF16),
        compiler_params=_cparams(("parallel", "parallel")),
        name="swa",
    )(q3, k3, v3, sink_tab, _swa_bias(tq))
    return out.reshape(batch * seq, W)


def _rms_normed(x, gain):
    ms = jnp.mean(x * x, axis=-1, keepdims=True)
    return (x * lax.rsqrt(ms + EPS) * gain).astype(BF16)


def _outproj_kernel(*refs, n_parts):
    parts = refs[:n_parts]
    w_ref, x_ref, g_ref, o_ref, h_ref = refs[n_parts:]
    y = x_ref[...]
    off = 0
    for p_ref in parts:
        kw = p_ref.shape[1]
        y = y + _dot(p_ref[...], w_ref[off:off + kw, :])
        off += kw
    o_ref[...] = y
    h_ref[...] = _rms_normed(y, g_ref[...])


def _outproj(parts, w, x, next_gain, tm=512):
    T, D = x.shape
    in_specs = [pl.BlockSpec((tm, p.shape[1]), lambda i: (i, 0)) for p in parts]
    in_specs += [pl.BlockSpec(w.shape, lambda i: (0, 0)),
                 pl.BlockSpec((tm, D), lambda i: (i, 0)),
                 pl.BlockSpec((1, D), lambda i: (0, 0))]
    row_spec = pl.BlockSpec((tm, D), lambda i: (i, 0))
    return pl.pallas_call(
        functools.partial(_outproj_kernel, n_parts=len(parts)),
        grid=(T // tm,),
        in_specs=in_specs,
        out_specs=[row_spec, row_spec],
        out_shape=[jax.ShapeDtypeStruct((T, D), F32), jax.ShapeDtypeStruct((T, D), BF16)],
        compiler_params=_cparams(("parallel",)),
        name="outproj",
    )(*parts, w, x, next_gain.reshape(1, D).astype(F32))


_CAND_ROWS = 80


def _cand_tables():
    pos = np.zeros((_CAND_ROWS,), np.float32)
    neg = np.zeros((_CAND_ROWS,), np.float32)
    r = 0
    for a, nb in ((0, 16), (1, 8), (2, 8), (3, 8), (4, 8), (5, 8), (6, 8), (7, 8)):
        for b in range(nb):
            pos[r] = a * PEER_TOPK + b
            neg[r] = 0.0 if (a + 1) * (b + 1) <= PEER_TOPK else -np.inf
            r += 1
    for a in range(8, 16):
        pos[r] = a * PEER_TOPK
        r += 1
    assert r == _CAND_ROWS
    tab = lambda t: jnp.asarray(np.broadcast_to(t[:, None], (_CAND_ROWS, LANES)).copy())
    return tab(pos), tab(neg)


def _batcher_pairs(n):
    pairs, p = [], 1
    while p < n:
        k = p
        while k >= 1:
            for j in range(k % p, n - k, 2 * k):
                for i in range(min(k, n - j - k)):
                    if (i + j) // (2 * p) == (i + j + k) // (2 * p):
                        pairs.append((i + j, i + j + k))
            k //= 2
        p *= 2
    return pairs


_SORT16 = _batcher_pairs(PEER_TOPK)
_N_CAND_PIECES = 10
_SORT10 = [(i, j) for i, j in _SORT16 if j < _N_CAND_PIECES]
_SUBLANES = 8


def _compare_exchange(items, i, j):
    items[i], items[j] = jnp.maximum(items[i], items[j]), jnp.minimum(items[i], items[j])


def _top_sorted(pieces, pairs):
    items = list(pieces)
    for i, j in pairs:
        _compare_exchange(items, i, j)
    n = PEER_TOPK
    items += [jnp.full(items[0].shape, -jnp.inf, F32)] * (n - len(items))
    for shift in (4, 2, 1):
        items = [jnp.maximum(items[i], pltpu.roll(items[n - 1 - i], shift, 0)) for i in range(n)]
        d = n // 2
        while d >= 1:
            for i in range(n):
                if i & d == 0:
                    _compare_exchange(items, i, i + d)
            d //= 2
    return items


def _pieces(x):
    return [x[_SUBLANES * g:_SUBLANES * (g + 1)] for g in range(x.shape[0] // _SUBLANES)]


def _sublane_total(x):
    for shift in (4, 2, 1):
        x = x + pltpu.roll(x, shift, 0)
    return x


def _count_ge(pieces, thr):
    total = jnp.zeros(thr.shape, F32)
    for p in pieces:
        total = total + jnp.where(p >= thr, 1.0, 0.0)
    return _sublane_total(total)


def _route_head_fast(s1, s2):
    p1, p2 = _pieces(s1), _pieces(s2)
    v1 = _top_sorted(p1, _SORT16)
    v2 = _top_sorted(p2, _SORT16)
    sub = lax.broadcasted_iota(jnp.int32, v1[0].shape, 0)

    def spread(vals):
        out = vals[0]
        for r in range(1, _SUBLANES):
            out = jnp.where(sub == r, vals[r], out)
        return out

    v2_lo, v2_hi, v1_hi = spread(v2[:8]), spread(v2[8:]), spread(v1[8:])
    cands = [v1[0] + v2_lo, v1[0] + v2_hi, v1[1] + v2_lo]
    for a in range(2, 8):
        cands.append(jnp.where(sub < PEER_TOPK // (a + 1), v1[a] + v2_lo, -jnp.inf))
    cands.append(v1_hi + v2[0])
    ts = _top_sorted(cands, _SORT10)
    tau = ts[PEER_TOPK - 1]
    z = jnp.exp(ts[0] - ts[0])
    for kk in range(1, PEER_TOPK):
        z = z + jnp.exp(ts[kk] - ts[0])

    tied = (_count_ge(p1, v1[-1]) != float(PEER_TOPK)) | (_count_ge(p2, v2[-1]) != float(PEER_TOPK))
    tied = tied | (_count_ge(cands, tau) != float(PEER_TOPK))
    for b in range(PEER_TOPK - 1):
        tied = tied | (v1[b] == v1[b + 1]) | (v2[b] == v2[b + 1])

    cnt = []
    for a in range(PEER_TOPK):
        c = jnp.zeros(tau.shape, F32)
        for b in range(PEER_TOPK // (a + 1)):
            c = c + jnp.where(v1[a] + v2[b] >= tau, 1.0, 0.0)
        cnt.append(c)
    c1, r2 = [], []
    for x in p1:
        c = jnp.zeros(x.shape, F32)
        for a in range(PEER_TOPK):
            c = jnp.where(x == v1[a], cnt[a], c)
        c1.append(c)
    for x in p2:
        r = jnp.zeros(x.shape, F32)
        for b in range(PEER_TOPK):
            r = r + jnp.where(v2[b] > x, 1.0, 0.0)
        r2.append(r)
    inv_z = 1.0 / z
    e1 = [jnp.exp(x - v1[0]) * inv_z for x in p1]
    e2 = [jnp.exp(x - v2[0]) for x in p2]
    cat = lambda ps: jnp.concatenate(ps, axis=0)
    return (cat(c1), cat(e1), cat(r2), cat(e2)), tied


def _extract_sorted(scores, by_key):
    nk, lanes = scores[0].shape
    kio = lax.broadcasted_iota(jnp.int32, (nk, lanes), 0).astype(F32)
    slot = lax.broadcasted_iota(jnp.int32, (PEER_TOPK, lanes), 0)

    def body(a, carry):
        here = slot == a
        out = []
        for (v, vals, aux), ranked in zip(carry, by_key):
            m = jnp.max(v, axis=0, keepdims=True)
            idx = jnp.min(jnp.where(v == m, kio, float(nk)), axis=0, keepdims=True)
            hit = kio == idx
            aux = jnp.where(hit, jnp.asarray(a, F32), aux) if ranked else jnp.where(here, idx, aux)
            out.append((jnp.where(hit, -jnp.inf, v), jnp.where(here, m, vals), aux))
        return tuple(out)

    small = jnp.zeros((PEER_TOPK, lanes), F32)
    unranked = jnp.full((nk, lanes), float(PEER_TOPK), F32)
    init = tuple((v, small, unranked if ranked else small) for v, ranked in zip(scores, by_key))
    return [(vals, aux) for _, vals, aux in lax.fori_loop(0, PEER_TOPK, body, init)]


def _route_head_exact(s1, s2, pos, neg):
    lanes = s1.shape[1]
    slot = lax.broadcasted_iota(jnp.int32, (PEER_TOPK, lanes), 0)
    kio = lax.broadcasted_iota(jnp.int32, (PEER_N_KEYS, lanes), 0).astype(F32)
    (v1, idx1), (v2, rank2) = _extract_sorted([s1, s2], [False, True])
    blocks = [v1[0:1] + v2[0:8], v1[0:1] + v2[8:16]]
    blocks += [v1[a:a + 1] + v2[0:8] for a in range(1, 8)]
    blocks += [v1[8:16] + v2[0:1]]
    cand = jnp.concatenate(blocks, axis=0) + neg

    def pick(kk, carry):
        cand, chosen, ts = carry
        m = jnp.max(cand, axis=0, keepdims=True)
        first = jnp.min(jnp.where(cand == m, pos, 1e9), axis=0, keepdims=True)
        hit = pos == first
        return (jnp.where(hit, -jnp.inf, cand), jnp.where(hit, 1.0, chosen), jnp.where(slot == kk, m, ts))

    _, chosen, ts = lax.fori_loop(0, PEER_TOPK, pick,
                                  (cand, jnp.zeros_like(cand), jnp.zeros((PEER_TOPK, lanes), F32)))
    z = jnp.sum(jnp.exp(ts - ts[0:1]), axis=0, keepdims=True)
    counts = [jnp.sum(chosen[0:16], axis=0, keepdims=True)]
    counts += [jnp.sum(chosen[8 * a + 8:8 * a + 16], axis=0, keepdims=True) for a in range(1, 8)]
    counts += [chosen[72 + a:73 + a] for a in range(8)]
    c1 = jnp.zeros((PEER_N_KEYS, lanes), F32)
    for a in range(PEER_TOPK):
        c1 = jnp.where(kio == idx1[a:a + 1], counts[a], c1)
    return c1, jnp.exp(s1 - v1[0:1]) / z, rank2, jnp.exp(s2 - v2[0:1])


def _route_kernel(h_ref, wq_ref, keys_ref, pos_ref, neg_ref,
                  c1_ref, e1_ref, r2_ref, e2_ref, qt_ref, sc_ref):
    half = PEER_QUERY_DIM // 2
    qt_ref[...] = _dot_nt(wq_ref[...], h_ref[...]).astype(BF16)

    def store(h, maps):
        c1, e1, r2, e2 = maps
        c1_ref[h] = c1
        e1_ref[h] = e1
        r2_ref[h] = r2.astype(BF16)
        e2_ref[h] = e2.astype(BF16)

    def head_body(h, _):
        r0 = pl.multiple_of(h * PEER_QUERY_DIM, PEER_QUERY_DIM)
        sc_ref[0] = _dot(keys_ref[2 * h], qt_ref[pl.ds(r0, half), :])
        sc_ref[1] = _dot(keys_ref[2 * h + 1], qt_ref[pl.ds(r0 + half, half), :])
        maps, tied = _route_head_fast(sc_ref[0], sc_ref[1])
        any_tied = jnp.max(jnp.where(tied, 1.0, 0.0)) > 0.0

        @pl.when(any_tied)
        def _():
            store(h, _route_head_exact(sc_ref[0], sc_ref[1], pos_ref[...], neg_ref[...]))

        @pl.when(jnp.logical_not(any_tied))
        def _():
            store(h, maps)

        return 0

    lax.fori_loop(0, PEER_HEADS, head_body, 0)


def _peer_route(h2, wq_t, keys, tt=LANES):
    T, D = h2.shape
    pos, neg = _cand_tables()
    stat_spec = pl.BlockSpec((PEER_HEADS, PEER_N_KEYS, tt), lambda i: (0, 0, i))
    stat = lambda dt: jax.ShapeDtypeStruct((PEER_HEADS, PEER_N_KEYS, T), dt)
    return pl.pallas_call(
        _route_kernel,
        grid=(T // tt,),
        in_specs=[pl.BlockSpec((tt, D), lambda i: (i, 0)),
                  pl.BlockSpec(wq_t.shape, lambda i: (0, 0)),
                  pl.BlockSpec(keys.shape, lambda i: (0, 0, 0)),
                  pl.BlockSpec((_CAND_ROWS, LANES), lambda i: (0, 0)),
                  pl.BlockSpec((_CAND_ROWS, LANES), lambda i: (0, 0))],
        out_specs=[stat_spec] * 4,
        out_shape=[stat(F32), stat(F32), stat(BF16), stat(BF16)],
        scratch_shapes=[pltpu.VMEM((PEER_HEADS * PEER_QUERY_DIM, tt), BF16),
                        pltpu.VMEM((2, PEER_N_KEYS, tt), F32)],
        compiler_params=_cparams(("parallel",)),
        name="peer_route",
    )(h2, wq_t, keys, pos, neg)


_KEY_GROUP = 8
_UNITS = 2


def _build_gated(a_ref, p_ref, c1_ref, e1_ref, r2_ref, e2_ref, key0, g0, ng, lt):
    rep = PEER_N_KEYS // 16

    def rows16(row):
        blk = jnp.broadcast_to(row, (16, LANES)).astype(BF16)
        return jnp.concatenate([blk] * rep, axis=0)

    ls = slice(lt * LANES, (lt + 1) * LANES)
    w = [jnp.zeros((PEER_N_KEYS, LANES), BF16) for _ in range(ng)]
    for h in range(PEER_HEADS):
        c1 = c1_ref[h, pl.ds(key0, _KEY_GROUP), ls]
        e1 = e1_ref[h, pl.ds(key0, _KEY_GROUP), ls]
        r2 = r2_ref[h, :, ls]
        e2 = e2_ref[h, :, ls]
        for g in range(ng):
            thr = rows16(c1[g0 + g:g0 + g + 1])
            gate = rows16(e1[g0 + g:g0 + g + 1])
            w[g] = w[g] + jnp.where(r2 < thr, e2, jnp.zeros_like(e2)) * gate
    for g in range(g0, g0 + ng):
        rs = slice(g * PEER_N_KEYS, (g + 1) * PEER_N_KEYS)
        a = a_ref[rs, ls]
        gelu = 0.5 * a * (1.0 + lax.erf(a * (2.0 ** -0.5)))
        p_ref[rs, ls] = gelu.astype(BF16) * w[g - g0]


def _experts_kernel(h_ref, dn_ref, upt_ref, c1_ref, e1_ref, r2_ref, e2_ref, x_ref, *rest, emit_norm):
    if emit_norm:
        g_ref, o_ref, hn_ref, a_ref, p_ref, acc_ref = rest
    else:
        o_ref, a_ref, p_ref, acc_ref = rest
    te, tt = a_ref.shape
    d_model = acc_ref.shape[0]
    e = pl.program_id(1)
    n_tiles = pl.num_programs(1) - 1
    cur = e % 2
    ng = _KEY_GROUP // _UNITS
    n_lane = tt // LANES
    key0 = pl.multiple_of(jnp.minimum(e, n_tiles - 1) * _KEY_GROUP, _KEY_GROUP)

    def front_mm(u):
        rows = te // _UNITS
        rs = slice(u * rows, (u + 1) * rows)
        a_ref[rs, :] = _dot_nt(dn_ref[rs, :], h_ref[...])

    def back_mm(u):
        rows = d_model // _UNITS
        rs = slice(u * rows, (u + 1) * rows)
        acc_ref[rs, :] += _dot(upt_ref[rs, :], p_ref[1 - cur])

    def run(front, back):
        if front:
            front_mm(0)
        for u in range(_UNITS):
            for lt in range(n_lane):
                if front:
                    _build_gated(a_ref, p_ref.at[cur], c1_ref, e1_ref, r2_ref, e2_ref, key0, u * ng, ng, lt)
                if front and lt == 0 and u + 1 < _UNITS:
                    front_mm(u + 1)
                if back and lt == n_lane // 2:
                    back_mm(u)

    @pl.when(e == 0)
    def _():
        acc_ref[...] = jnp.zeros_like(acc_ref)
        run(True, False)

    @pl.when((e > 0) & (e < n_tiles))
    def _():
        run(True, True)

    @pl.when(e == n_tiles)
    def _():
        run(False, True)
        y = x_ref[...] + acc_ref[...].T
        o_ref[...] = y
        if emit_norm:
            hn_ref[...] = _rms_normed(y, g_ref[...])


def _peer_experts(h2, down, up_t, stats, x, next_gain, tt=512):
    T, D = h2.shape
    E = down.shape[0]
    te = _KEY_GROUP * PEER_N_KEYS
    n_tiles = E // te
    emit_norm = next_gain is not None
    stat_spec = pl.BlockSpec((PEER_HEADS, PEER_N_KEYS, tt), lambda i, e: (0, 0, i))
    row_spec = pl.BlockSpec((tt, D), lambda i, e: (i, 0))
    in_specs = [row_spec,
                pl.BlockSpec((te, D), lambda i, e: (jnp.minimum(e, n_tiles - 1), 0)),
                pl.BlockSpec((D, te), lambda i, e: (0, jnp.maximum(e - 1, 0))),
                stat_spec, stat_spec, stat_spec, stat_spec,
                row_spec]
    args = [h2, down, up_t, *stats, x]
    out_specs = [row_spec]
    out_shape = [jax.ShapeDtypeStruct((T, D), F32)]
    if emit_norm:
        in_specs.append(pl.BlockSpec((1, D), lambda i, e: (0, 0)))
        args.append(next_gain.reshape(1, D).astype(F32))
        out_specs.append(row_spec)
        out_shape.append(jax.ShapeDtypeStruct((T, D), BF16))
    res = pl.pallas_call(
        functools.partial(_experts_kernel, emit_norm=emit_norm),
        grid=(T // tt, n_tiles + 1),
        in_specs=in_specs, out_specs=out_specs, out_shape=out_shape,
        scratch_shapes=[pltpu.VMEM((te, tt), F32),
                        pltpu.VMEM((2, te, tt), BF16),
                        pltpu.VMEM((D, tt), F32)],
        compiler_params=_cparams(("parallel", "arbitrary")),
        name="peer_experts",
    )(*args)
    return (res[0], res[1]) if emit_norm else (res[0], None)


def _peer_layer(x, h2, next_gain, w_query, sub_keys, down, up):
    keys = sub_keys.reshape(PEER_HEADS * 2, PEER_N_KEYS, PEER_QUERY_DIM // 2).astype(BF16)
    stats = _peer_route(h2, w_query.T.astype(BF16), keys)
    return _peer_experts(h2, down.astype(BF16), up.T.astype(BF16), stats, x, next_gain)


def _tile_heads(g, n):
    return jnp.tile(g.astype(F32), n)


def _even_mixer(x, h, ffn_gain, w_in, f_bias, qn_a, kn_a, qn_b, kn_b, w_out, tabs, *, batch, seq):
    w = w_in.astype(BF16)
    o_qa, o_ka, o_va, o_qb, o_kb, o_vb, o_gb, o_fb = (
        0, A_W, 2 * A_W, 3 * A_W, 3 * A_W + B_W, 3 * A_W + 2 * B_W, 3 * A_W + 3 * B_W, 3 * A_W + 4 * B_W)
    gain_a = jnp.concatenate([_tile_heads(qn_a, MOBA_HEADS), _tile_heads(kn_a, MOBA_HEADS)])
    qk_a, km = _proj(h, w[:, o_qa:o_va], seq=seq, tn=A_W, tm=MOBA_BLOCK, gain=gain_a,
                     rope_tabs=tabs, kmean=True)
    nb = seq // MOBA_BLOCK
    kmean = km.reshape(batch, nb, 2 * A_W)[:, :, A_W:]
    kmean = jnp.pad(kmean, ((0, 0), (0, LANES - nb), (0, 0))).astype(BF16)
    gain_b = jnp.concatenate([_tile_heads(qn_b, FOX_HEADS), _tile_heads(kn_b, FOX_HEADS)])
    qk_b = _proj(h, w[:, o_qb:o_vb], seq=seq, tn=B_W, gain=gain_b)
    w_plain = jnp.concatenate([w[:, o_va:o_qb], w[:, o_vb:o_fb]], axis=1)
    vvg = _proj(h, w_plain, seq=seq, tn=A_W)
    va, vb, gb = vvg[:, :A_W], vvg[:, A_W:A_W + B_W], vvg[:, A_W + B_W:]
    frow = _fox_gates(h, w[:, o_fb:].T, f_bias, batch=batch, seq=seq)
    fcol = jnp.pad(frow.transpose(0, 2, 1), ((0, 0), (0, 0), (0, LANES - FOX_HEADS)))
    oa = _moba(qk_a[:, :A_W], qk_a[:, A_W:], va, kmean, batch=batch, seq=seq)
    ob = _fox(qk_b[:, :B_W], qk_b[:, B_W:], vb, gb, fcol, frow, batch=batch, seq=seq)
    return _outproj([oa, ob], w_out.astype(BF16), x, ffn_gain)


def _odd_mixer(x, h, ffn_gain, w_in, qn, kn, sinks, w_out, tabs, *, batch, seq):
    w = w_in.astype(BF16)
    qw = SWA_Q_HEADS * HEAD_DIM
    kw = SWA_KV_HEADS * HEAD_DIM
    q = _proj(h, w[:, :qw], seq=seq, tn=512, gain=_tile_heads(qn, SWA_Q_HEADS), rope_tabs=tabs)
    k = _proj(h, w[:, qw:qw + kw], seq=seq, tn=kw, gain=_tile_heads(kn, SWA_KV_HEADS), rope_tabs=tabs)
    v = _proj(h, w[:, qw + kw:], seq=seq, tn=kw)
    o = _swa(q, k, v, sinks, batch=batch, seq=seq)
    return _outproj([o], w_out.astype(BF16), x, ffn_gain)


def kernel(x, attn_norm, ffn_norm, ev_w_in, ev_forget_bias, ev_q_norm_a, ev_k_norm_a, ev_q_norm_b,
           ev_k_norm_b, ev_w_out, od_w_in, od_q_norm, od_k_norm, od_sinks, od_w_out,
           peer_w_query, peer_sub_keys, peer_down, peer_up):
    batch, seq, d_model = x.shape
    depth = attn_norm.shape[0]
    tabs = _rope_tables(seq)
    xt = x.reshape(batch * seq, d_model)
    h = _rmsnorm(xt, attn_norm[0])
    for l in range(depth):
        i = l // 2
        if l % 2 == 0:
            xt, h2 = _even_mixer(xt, h, ffn_norm[l], ev_w_in[i], ev_forget_bias[i], ev_q_norm_a[i],
                                 ev_k_norm_a[i], ev_q_norm_b[i], ev_k_norm_b[i], ev_w_out[i], tabs,
                                 batch=batch, seq=seq)
        else:
            xt, h2 = _odd_mixer(xt, h, ffn_norm[l], od_w_in[i], od_q_norm[i], od_k_norm[i], od_sinks[i],
                                od_w_out[i], tabs, batch=batch, seq=seq)
        next_gain = attn_norm[l + 1] if l + 1 < depth else None
        xt, h = _peer_layer(xt, h2, next_gain, peer_w_query[l], peer_sub_keys[l], peer_down[l], peer_up[l])
    return xt.reshape(batch, seq, d_model)
```

```python
import functools

import numpy as np
import jax
import jax.numpy as jnp
from jax import lax
from jax.experimental import pallas as pl
from jax.experimental.pallas import tpu as pltpu

F32 = jnp.float32
BF16 = jnp.bfloat16

HEAD_DIM = 64
ROT_DIM = HEAD_DIM // 4
ROPE_THETA = 500000.0
ATTN_SCALE = HEAD_DIM ** -0.5
EPS = 1e-6
NEG_INF = -1e30

MOBA_HEADS = 8
FOX_HEADS = 8
MOBA_BLOCK = 256
MOBA_TOPK = 3
A_W = MOBA_HEADS * HEAD_DIM
B_W = FOX_HEADS * HEAD_DIM

SWA_Q_HEADS = 16
SWA_KV_HEADS = 2
SWA_WINDOW = 128

PEER_HEADS = 8
PEER_N_KEYS = 128
PEER_TOPK = 16
PEER_QUERY_DIM = 128

LANES = 128
PAIR_W = 2 * HEAD_DIM
VMEM_LIMIT = 48 * 1024 * 1024


def _cparams(sem):
    return pltpu.CompilerParams(dimension_semantics=sem, vmem_limit_bytes=VMEM_LIMIT)


def _dot_nt(a, b):
    return lax.dot_general(a, b, (((1,), (1,)), ((), ())), preferred_element_type=F32)


def _dot(a, b):
    return jnp.dot(a, b, preferred_element_type=F32)


def _split3(x):
    h1 = x.astype(BF16)
    r1 = x - h1.astype(F32)
    h2 = r1.astype(BF16)
    h3 = (r1 - h2.astype(F32)).astype(BF16)
    return h1, h2, h3


def _rmsnorm_kernel(x_ref, g_ref, o_ref):
    x = x_ref[...]
    ms = jnp.mean(x * x, axis=-1, keepdims=True)
    o_ref[...] = (x * lax.rsqrt(ms + EPS) * g_ref[...]).astype(o_ref.dtype)


def _rmsnorm(x, gain, tm=512):
    T, D = x.shape
    return pl.pallas_call(
        _rmsnorm_kernel,
        grid=(T // tm,),
        in_specs=[pl.BlockSpec((tm, D), lambda i: (i, 0)),
                  pl.BlockSpec((1, D), lambda i: (0, 0))],
        out_specs=pl.BlockSpec((tm, D), lambda i: (i, 0)),
        out_shape=jax.ShapeDtypeStruct((T, D), BF16),
        compiler_params=_cparams(("parallel",)),
        name="rmsnorm",
    )(x, gain.reshape(1, D))


def _proj_kernel(*refs, norm, rope, kmean, tn):
    it = iter(refs)
    h_ref, w_ref = next(it), next(it)
    gain_ref = next(it) if norm else None
    bd_ref = next(it) if norm else None
    if rope:
        c_ref, sa_ref, sb_ref = next(it), next(it), next(it)
    o_ref = next(it)
    km_ref = next(it) if kmean else None

    y = _dot(h_ref[...], w_ref[...])
    if norm:
        y2 = y * y
        bd = bd_ref[...]
        cols = []
        for c in range(tn // LANES):
            h1, h2, h3 = _split3(y2[:, c * LANES:(c + 1) * LANES])
            cols.append(_dot(h1, bd) + _dot(h2, bd) + _dot(h3, bd))
        ms = cols[0] if len(cols) == 1 else jnp.concatenate(cols, axis=1)
        y = y * lax.rsqrt(ms + EPS) * gain_ref[...]
    if rope:
        rep = tn // LANES
        tile = (lambda t: t) if rep == 1 else (lambda t: jnp.concatenate([t] * rep, axis=1))
        y = (y * tile(c_ref[...])
             + pltpu.roll(y, tn - ROT_DIM // 2, 1) * tile(sa_ref[...])
             + pltpu.roll(y, ROT_DIM // 2, 1) * tile(sb_ref[...]))
    o_ref[...] = y.astype(o_ref.dtype)
    if kmean:
        km_ref[0] = jnp.mean(y, axis=0, keepdims=True)


def _proj(h, w, *, seq, tn, tm=512, gain=None, rope_tabs=None, kmean=False):
    T, D = h.shape
    N = w.shape[1]
    norm = gain is not None
    rope = rope_tabs is not None
    nseq = seq // tm
    in_specs = [pl.BlockSpec((tm, D), lambda i, j: (i, 0)),
                pl.BlockSpec((D, tn), lambda i, j: (0, j))]
    args = [h, w]
    if norm:
        bd = np.kron(np.eye(LANES // HEAD_DIM), np.ones((HEAD_DIM, HEAD_DIM))) / HEAD_DIM
        in_specs += [pl.BlockSpec((1, tn), lambda i, j: (0, j)),
                     pl.BlockSpec((LANES, LANES), lambda i, j: (0, 0))]
        args += [gain.reshape(1, N).astype(F32), jnp.asarray(bd, BF16)]
    if rope:
        in_specs += [pl.BlockSpec((tm, LANES), lambda i, j: (i % nseq, 0))] * 3
        args += list(rope_tabs)
    out_specs = [pl.BlockSpec((tm, tn), lambda i, j: (i, j))]
    out_shape = [jax.ShapeDtypeStruct((T, N), BF16)]
    if kmean:
        out_specs.append(pl.BlockSpec((1, 1, tn), lambda i, j: (i, 0, j)))
        out_shape.append(jax.ShapeDtypeStruct((T // tm, 1, N), F32))
    res = pl.pallas_call(
        functools.partial(_proj_kernel, norm=norm, rope=rope, kmean=kmean, tn=tn),
        grid=(T // tm, N // tn),
        in_specs=in_specs, out_specs=out_specs, out_shape=out_shape,
        compiler_params=_cparams(("parallel", "parallel")),
        name="proj",
    )(*args)
    return res if kmean else res[0]


def _rope_tables(seq):
    half = ROT_DIM // 2
    inv_freq = jnp.power(ROPE_THETA, -jnp.arange(0, ROT_DIM, 2, dtype=F32) / ROT_DIM)
    ang = jnp.arange(seq, dtype=F32)[:, None] * inv_freq[None, :]
    cos, sin = jnp.cos(ang), jnp.sin(ang)
    one = jnp.ones((seq, HEAD_DIM - ROT_DIM), F32)
    zero = jnp.zeros((seq, HEAD_DIM - ROT_DIM), F32)
    z8 = jnp.zeros((seq, half), F32)
    c = jnp.concatenate([cos, cos, one], axis=1)
    sa = jnp.concatenate([-sin, z8, zero], axis=1)
    sb = jnp.concatenate([z8, sin, zero], axis=1)
    rep = LANES // HEAD_DIM
    return tuple(jnp.concatenate([t] * rep, axis=1) for t in (c, sa, sb))


def _gates_kernel(h_ref, wf_ref, b_ref, tri_ref, o_ref, carry_ref):
    @pl.when(pl.program_id(1) == 0)
    def _():
        carry_ref[...] = jnp.zeros_like(carry_ref)

    z = _dot_nt(wf_ref[...], h_ref[...]) + b_ref[...][:, :1]
    lf = jnp.minimum(z, 0.0) - jnp.log1p(jnp.exp(-jnp.abs(z)))
    tri = tri_ref[...]
    h1, h2, h3 = _split3(lf)
    cs = _dot(h1, tri) + _dot(h2, tri) + _dot(h3, tri) + carry_ref[...][:, :1]
    o_ref[0] = cs
    carry_ref[...] = jnp.broadcast_to(cs[:, -1:], carry_ref.shape)


def _fox_gates(h, wf_t, bias, *, batch, seq, tm=512):
    T, D = h.shape
    nh = wf_t.shape[0]
    nseq = seq // tm
    tri = jnp.asarray(np.triu(np.ones((tm, tm))), BF16)
    return pl.pallas_call(
        _gates_kernel,
        grid=(batch, nseq),
        in_specs=[pl.BlockSpec((tm, D), lambda b, s: (b * nseq + s, 0)),
                  pl.BlockSpec((nh, D), lambda b, s: (0, 0)),
                  pl.BlockSpec((nh, LANES), lambda b, s: (0, 0)),
                  pl.BlockSpec((tm, tm), lambda b, s: (0, 0))],
        out_specs=pl.BlockSpec((1, nh, tm), lambda b, s: (b, 0, s)),
        out_shape=jax.ShapeDtypeStruct((batch, nh, seq), F32),
        scratch_shapes=[pltpu.VMEM((nh, LANES), F32)],
        compiler_params=_cparams(("parallel", "arbitrary")),
        name="fox_gates",
    )(h, wf_t, jnp.broadcast_to(bias.astype(F32)[:, None], (nh, LANES)), tri)


def _lane_tile(x, width):
    rep = width // LANES
    return x if rep == 1 else jnp.concatenate([x] * rep, axis=1)


def _flash_init(m_ref, acc_ref):
    m_ref[...] = jnp.full(m_ref.shape, NEG_INF, F32)
    acc_ref[...] = jnp.zeros(acc_ref.shape, F32)


def _head_values(v):
    lane = lax.broadcasted_iota(jnp.int32, v.shape, 1)
    return [jnp.where((lane >= hh * HEAD_DIM) & (lane < (hh + 1) * HEAD_DIM), v, jnp.ones_like(v))
            for hh in range(2)]


def _flash_update(slot, s, v, m_ref, acc_ref):
    tk = s.shape[1]
    m_prev = m_ref[slot]
    m_new = jnp.maximum(m_prev, jnp.max(s, axis=1, keepdims=True))
    alpha = jnp.exp(m_prev - m_new)
    p = jnp.exp(s - _lane_tile(m_new, tk))
    acc_ref[slot] = alpha * acc_ref[slot] + _dot(p.astype(BF16), v)
    m_ref[slot] = m_new


def _flash_finish(lane, acc_ref):
    outs = []
    for hh in range(2):
        acc = acc_ref[hh]
        den = (1 - hh) * HEAD_DIM
        outs.append(acc / acc[:, den:den + 1])
    return jnp.where(lane < HEAD_DIM, outs[0], outs[1])


def _flash_tiles(qi, tk, logits, values, m_ref, acc_ref):
    def absorb(s, off):
        vh = values(off)
        for hh in range(2):
            _flash_update(hh, s[hh], vh[hh], m_ref, acc_ref)

    def pair(off_a, off_b, b_diagonal):
        sa = logits(off_a, False)
        sb = logits(off_b, b_diagonal)
        absorb(sa, off_a)
        absorb(sb, off_b)

    def body(jj, carry):
        off = pl.multiple_of(2 * jj * tk, 2 * tk)
        pair(off, off + tk, False)
        return carry

    lax.fori_loop(0, qi // 2, body, 0)
    diag = pl.multiple_of(qi * tk, tk)

    @pl.when(qi % 2 == 1)
    def _():
        pair(diag - tk, diag, True)

    @pl.when(qi % 2 == 0)
    def _():
        absorb(logits(diag, True), diag)


def _head_queries(q, lane):
    qs = q * ATTN_SCALE
    return [jnp.where((lane >= hh * HEAD_DIM) & (lane < (hh + 1) * HEAD_DIM), qs, jnp.zeros_like(qs))
            for hh in range(2)]


def _moba_kernel(q_ref, k_ref, v_ref, km_ref, o_ref, m_ref, acc_ref):
    tq = q_ref.shape[1]
    tk = tq
    qi = pl.program_id(2)
    lane = lax.broadcasted_iota(jnp.int32, (tq, LANES), 1)
    lane_f = lane.astype(F32)
    rowv = lax.broadcasted_iota(jnp.int32, (tq, LANES), 0)
    row_blk = 2 * qi + (rowv >= MOBA_BLOCK).astype(jnp.int32)
    row = lax.broadcasted_iota(jnp.int32, (tq, tk), 0)
    col = lax.broadcasted_iota(jnp.int32, (tq, tk), 1)
    qh = _head_queries(q_ref[0], lane)
    _flash_init(m_ref, acc_ref)

    sels = []
    for hh in range(2):
        gate = _dot_nt(qh[hh], km_ref[0])
        gate = jnp.where(lane < row_blk, gate, -jnp.inf)
        sel = jnp.zeros((tq, LANES), F32)
        for _ in range(MOBA_TOPK):
            m = jnp.max(gate, axis=1, keepdims=True)
            idx = jnp.min(jnp.where(gate == m, lane_f, float(LANES)), axis=1, keepdims=True)
            hit = lane_f == idx
            sel = jnp.where(hit & (m > -jnp.inf), 1.0, sel)
            gate = jnp.where(hit, -jnp.inf, gate)
        sels.append(sel)

    def chosen(sel, blk):
        return jnp.max(jnp.where(lane == blk, sel, 0.0), axis=1, keepdims=True) > 0.0

    def logits(off, diagonal):
        kj = k_ref[0, pl.ds(off, tk), :]
        out = []
        for hh in range(2):
            s = _dot_nt(qh[hh], kj)
            if diagonal:
                visible = (col >= MOBA_BLOCK) | (row < MOBA_BLOCK) | chosen(sels[hh], 2 * qi)
                s = jnp.where((col <= row) & visible, s, NEG_INF)
            else:
                blk = 2 * (off // tk)
                s = jnp.concatenate(
                    [jnp.where(chosen(sels[hh], blk), s[:, :MOBA_BLOCK], NEG_INF),
                     jnp.where(chosen(sels[hh], blk + 1), s[:, MOBA_BLOCK:], NEG_INF)], axis=1)
            out.append(s)
        return out

    def values(off):
        return _head_values(v_ref[0, pl.ds(off, tk), :])

    _flash_tiles(qi, tk, logits, values, m_ref, acc_ref)
    o_ref[0] = _flash_finish(lane, acc_ref).astype(o_ref.dtype)


def _flash_scratch(tq):
    return [pltpu.VMEM((2, tq, LANES), F32)] * 2


def _moba(q, k, v, kmean, *, batch, seq):
    W = q.shape[1]
    tq = 2 * MOBA_BLOCK
    q3, k3, v3 = (t.reshape(batch, seq, W) for t in (q, k, v))
    out = pl.pallas_call(
        _moba_kernel,
        grid=(batch, W // PAIR_W, seq // tq),
        in_specs=[pl.BlockSpec((1, tq, PAIR_W), lambda b, p, i: (b, i, p)),
                  pl.BlockSpec((1, seq, PAIR_W), lambda b, p, i: (b, 0, p)),
                  pl.BlockSpec((1, seq, PAIR_W), lambda b, p, i: (b, 0, p)),
                  pl.BlockSpec((1, LANES, PAIR_W), lambda b, p, i: (b, 0, p))],
        out_specs=pl.BlockSpec((1, tq, PAIR_W), lambda b, p, i: (b, i, p)),
        out_shape=jax.ShapeDtypeStruct((batch, seq, W), BF16),
        scratch_shapes=_flash_scratch(tq),
        compiler_params=_cparams(("parallel", "parallel", "parallel")),
        name="moba",
    )(q3, k3, v3, kmean)
    return out.reshape(batch * seq, W)


def _fox_kernel(q_ref, k_ref, v_ref, g_ref, fcol_ref, frow_ref, o_ref, m_ref, acc_ref):
    tq = q_ref.shape[1]
    tk = tq
    pr = pl.program_id(1)
    qi = pl.program_id(2)
    lane = lax.broadcasted_iota(jnp.int32, (tq, LANES), 1)
    row = lax.broadcasted_iota(jnp.int32, (tq, tk), 0)
    col = lax.broadcasted_iota(jnp.int32, (tq, tk), 1)
    qh = _head_queries(q_ref[0], lane)
    fcol = fcol_ref[0]
    fq = [jnp.sum(jnp.where(lane == 2 * pr + hh, fcol, 0.0), axis=1, keepdims=True) for hh in range(2)]
    _flash_init(m_ref, acc_ref)

    def logits(off, diagonal):
        kj = k_ref[0, pl.ds(off, tk), :]
        fk_all = frow_ref[0, :, pl.ds(off, tk)]
        sub = lax.broadcasted_iota(jnp.int32, fk_all.shape, 0)
        out = []
        for hh in range(2):
            fk = jnp.sum(jnp.where(sub == 2 * pr + hh, fk_all, 0.0), axis=0, keepdims=True)
            s = _dot_nt(qh[hh], kj) + fq[hh] - fk
            out.append(jnp.where(col <= row, s, NEG_INF) if diagonal else s)
        return out

    def values(off):
        return _head_values(v_ref[0, pl.ds(off, tk), :])

    _flash_tiles(qi, tk, logits, values, m_ref, acc_ref)
    o = _flash_finish(lane, acc_ref)
    o_ref[0] = (o * jax.nn.sigmoid(g_ref[0].astype(F32))).astype(o_ref.dtype)


def _fox(q, k, v, g, fcol, frow, *, batch, seq, tq=512):
    W = q.shape[1]
    nh = frow.shape[1]
    q3, k3, v3, g3 = (t.reshape(batch, seq, W) for t in (q, k, v, g))
    out = pl.pallas_call(
        _fox_kernel,
        grid=(batch, W // PAIR_W, seq // tq),
        in_specs=[pl.BlockSpec((1, tq, PAIR_W), lambda b, p, i: (b, i, p)),
                  pl.BlockSpec((1, seq, PAIR_W), lambda b, p, i: (b, 0, p)),
                  pl.BlockSpec((1, seq, PAIR_W), lambda b, p, i: (b, 0, p)),
                  pl.BlockSpec((1, tq, PAIR_W), lambda b, p, i: (b, i, p)),
                  pl.BlockSpec((1, tq, LANES), lambda b, p, i: (b, i, 0)),
                  pl.BlockSpec((1, nh, seq), lambda b, p, i: (b, 0, 0))],
        out_specs=pl.BlockSpec((1, tq, PAIR_W), lambda b, p, i: (b, i, p)),
        out_shape=jax.ShapeDtypeStruct((batch, seq, W), BF16),
        scratch_shapes=_flash_scratch(tq),
        compiler_params=_cparams(("parallel", "parallel", "parallel")),
        name="fox",
    )(q3, k3, v3, g3, fcol, frow)
    return out.reshape(batch * seq, W)


def _swa_kernel(q_ref, k_ref, v_ref, sink_ref, bias_ref, o_ref):
    tq = q_ref.shape[1]
    qi = pl.program_id(1)
    group = SWA_Q_HEADS // SWA_KV_HEADS
    tk = tq + SWA_WINDOW
    lane = lax.broadcasted_iota(jnp.int32, (tq, LANES), 1)
    kstart = pl.multiple_of(jnp.maximum(qi * tq - SWA_WINDOW, 0), SWA_WINDOW)
    k = k_ref[0, pl.ds(kstart, tk), :]
    vh = _head_values(v_ref[0, pl.ds(kstart, tk), :])
    bias = bias_ref[jnp.minimum(qi, 1)]
    bias = jnp.concatenate([bias] * group, axis=0)
    sink_tab = sink_ref[...]
    outs = [None] * SWA_Q_HEADS
    for c in range(SWA_KV_HEADS):
        pieces, sinks = [], []
        for g in range(group):
            head = c * group + g
            blk = q_ref[0, :, (head // 2) * PAIR_W:(head // 2 + 1) * PAIR_W] * ATTN_SCALE
            hh = head % 2
            qm = jnp.where((lane >= hh * HEAD_DIM) & (lane < (hh + 1) * HEAD_DIM), blk, jnp.zeros_like(blk))
            if hh != c:
                qm = pltpu.roll(qm.astype(F32), HEAD_DIM, 1).astype(BF16)
            pieces.append(qm)
            sinks.append(jnp.broadcast_to(sink_tab[head:head + 1, :], (tq, LANES)))
        qs = jnp.concatenate(pieces, axis=0)
        sink = jnp.concatenate(sinks, axis=0)
        s = _dot_nt(qs, k) + bias
        m = jnp.maximum(sink, jnp.max(s, axis=1, keepdims=True))
        p = jnp.exp(s - _lane_tile(m, tk))
        acc = _dot(p.astype(BF16), vh[c])
        den = pltpu.roll(acc, HEAD_DIM, 1) + jnp.exp(sink - m)
        o = acc / den
        for g in range(group):
            head = c * group + g
            oh = o[g * tq:(g + 1) * tq]
            outs[head] = oh if head % 2 == c else pltpu.roll(oh, HEAD_DIM, 1)
    for pp in range(SWA_Q_HEADS // 2):
        o_ref[0, :, pp * PAIR_W:(pp + 1) * PAIR_W] = jnp.where(
            lane < HEAD_DIM, outs[2 * pp], outs[2 * pp + 1]).astype(o_ref.dtype)


def _swa_bias(tq):
    r = np.arange(tq)[:, None]
    c = np.arange(tq + SWA_WINDOW)[None, :]
    tabs = []
    for key_offset in (0, SWA_WINDOW):
        dist = r + key_offset - c
        tabs.append(np.where((dist >= 0) & (dist < SWA_WINDOW), 0.0, NEG_INF))
    return jnp.asarray(np.stack(tabs), F32)


def _swa(q, k, v, sinks, *, batch, seq, tq=SWA_WINDOW):
    W = q.shape[1]
    tk = tq + SWA_WINDOW
    q3 = q.reshape(batch, seq, W)
    k3, v3 = (t.reshape(batch, seq, PAIR_W) for t in (k, v))
    sink_tab = jnp.broadcast_to(sinks.astype(F32)[:, None], (SWA_Q_HEADS, LANES))
    out = pl.pallas_call(
        _swa_kernel,
        grid=(batch, seq // tq),
        in_specs=[pl.BlockSpec((1, tq, W), lambda b, i: (b, i, 0)),
                  pl.BlockSpec((1, seq, PAIR_W), lambda b, i: (b, 0, 0)),
                  pl.BlockSpec((1, seq, PAIR_W), lambda b, i: (b, 0, 0)),
                  pl.BlockSpec((SWA_Q_HEADS, LANES), lambda b, i: (0, 0)),
                  pl.BlockSpec((2, tq, tk), lambda b, i: (0, 0, 0))],
        out_specs=pl.BlockSpec((1, tq, W), lambda b, i: (b, i, 0)),
        out_shape=jax.ShapeDtypeStruct((batch, seq, W), BF16),
        compiler_params=_cparams(("parallel", "parallel")),
        name="swa",
    )(q3, k3, v3, sink_tab, _swa_bias(tq))
    return out.reshape(batch * seq, W)


def _rms_normed(x, gain):
    ms = jnp.mean(x * x, axis=-1, keepdims=True)
    return (x * lax.rsqrt(ms + EPS) * gain).astype(BF16)


def _outproj_kernel(*refs, n_parts):
    parts = refs[:n_parts]
    w_ref, x_ref, g_ref, o_ref, h_ref = refs[n_parts:]
    y = x_ref[...]
    off = 0
    for p_ref in parts:
        kw = p_ref.shape[1]
        y = y + _dot(p_ref[...], w_ref[off:off + kw, :])
        off += kw
    o_ref[...] = y
    h_ref[...] = _rms_normed(y, g_ref[...])


def _outproj(parts, w, x, next_gain, tm=512):
    T, D = x.shape
    in_specs = [pl.BlockSpec((tm, p.shape[1]), lambda i: (i, 0)) for p in parts]
    in_specs += [pl.BlockSpec(w.shape, lambda i: (0, 0)),
                 pl.BlockSpec((tm, D), lambda i: (i, 0)),
                 pl.BlockSpec((1, D), lambda i: (0, 0))]
    row_spec = pl.BlockSpec((tm, D), lambda i: (i, 0))
    return pl.pallas_call(
        functools.partial(_outproj_kernel, n_parts=len(parts)),
        grid=(T // tm,),
        in_specs=in_specs,
        out_specs=[row_spec, row_spec],
        out_shape=[jax.ShapeDtypeStruct((T, D), F32), jax.ShapeDtypeStruct((T, D), BF16)],
        compiler_params=_cparams(("parallel",)),
        name="outproj",
    )(*parts, w, x, next_gain.reshape(1, D).astype(F32))


_CAND_ROWS = 80


def _cand_tables():
    pos = np.zeros((_CAND_ROWS,), np.float32)
    neg = np.zeros((_CAND_ROWS,), np.float32)
    r = 0
    for a, nb in ((0, 16), (1, 8), (2, 8), (3, 8), (4, 8), (5, 8), (6, 8), (7, 8)):
        for b in range(nb):
            pos[r] = a * PEER_TOPK + b
            neg[r] = 0.0 if (a + 1) * (b + 1) <= PEER_TOPK else -np.inf
            r += 1
    for a in range(8, 16):
        pos[r] = a * PEER_TOPK
        r += 1
    assert r == _CAND_ROWS
    tab = lambda t: jnp.asarray(np.broadcast_to(t[:, None], (_CAND_ROWS, LANES)).copy())
    return tab(pos), tab(neg)


def _batcher_pairs(n):
    pairs, p = [], 1
    while p < n:
        k = p
        while k >= 1:
            for j in range(k % p, n - k, 2 * k):
                for i in range(min(k, n - j - k)):
                    if (i + j) // (2 * p) == (i + j + k) // (2 * p):
                        pairs.append((i + j, i + j + k))
            k //= 2
        p *= 2
    return pairs


_SORT16 = _batcher_pairs(PEER_TOPK)
_N_CAND_PIECES = 10
_SORT10 = [(i, j) for i, j in _SORT16 if j < _N_CAND_PIECES]
_SUBLANES = 8


def _compare_exchange(items, i, j):
    items[i], items[j] = jnp.maximum(items[i], items[j]), jnp.minimum(items[i], items[j])


def _top_sorted(pieces, pairs):
    items = list(pieces)
    for i, j in pairs:
        _compare_exchange(items, i, j)
    n = PEER_TOPK
    items += [jnp.full(items[0].shape, -jnp.inf, F32)] * (n - len(items))
    for shift in (4, 2, 1):
        items = [jnp.maximum(items[i], pltpu.roll(items[n - 1 - i], shift, 0)) for i in range(n)]
        d = n // 2
        while d >= 1:
            for i in range(n):
                if i & d == 0:
                    _compare_exchange(items, i, i + d)
            d //= 2
    return items


def _pieces(x):
    return [x[_SUBLANES * g:_SUBLANES * (g + 1)] for g in range(x.shape[0] // _SUBLANES)]


def _sublane_total(x):
    for shift in (4, 2, 1):
        x = x + pltpu.roll(x, shift, 0)
    return x


def _count_ge(pieces, thr):
    total = jnp.zeros(thr.shape, F32)
    for p in pieces:
        total = total + jnp.where(p >= thr, 1.0, 0.0)
    return _sublane_total(total)


def _route_head_fast(s1, s2):
    p1, p2 = _pieces(s1), _pieces(s2)
    v1 = _top_sorted(p1, _SORT16)
    v2 = _top_sorted(p2, _SORT16)
    sub = lax.broadcasted_iota(jnp.int32, v1[0].shape, 0)

    def spread(vals):
        out = vals[0]
        for r in range(1, _SUBLANES):
            out = jnp.where(sub == r, vals[r], out)
        return out

    v2_lo, v2_hi, v1_hi = spread(v2[:8]), spread(v2[8:]), spread(v1[8:])
    cands = [v1[0] + v2_lo, v1[0] + v2_hi, v1[1] + v2_lo]
    for a in range(2, 8):
        cands.append(jnp.where(sub < PEER_TOPK // (a + 1), v1[a] + v2_lo, -jnp.inf))
    cands.append(v1_hi + v2[0])
    ts = _top_sorted(cands, _SORT10)
    tau = ts[PEER_TOPK - 1]
    z = jnp.exp(ts[0] - ts[0])
    for kk in range(1, PEER_TOPK):
        z = z + jnp.exp(ts[kk] - ts[0])

    tied = (_count_ge(p1, v1[-1]) != float(PEER_TOPK)) | (_count_ge(p2, v2[-1]) != float(PEER_TOPK))
    tied = tied | (_count_ge(cands, tau) != float(PEER_TOPK))
    for b in range(PEER_TOPK - 1):
        tied = tied | (v1[b] == v1[b + 1]) | (v2[b] == v2[b + 1])

    cnt = []
    for a in range(PEER_TOPK):
        c = jnp.zeros(tau.shape, F32)
        for b in range(PEER_TOPK // (a + 1)):
            c = c + jnp.where(v1[a] + v2[b] >= tau, 1.0, 0.0)
        cnt.append(c)
    c1, r2 = [], []
    for x in p1:
        c = jnp.zeros(x.shape, F32)
        for a in range(PEER_TOPK):
            c = jnp.where(x == v1[a], cnt[a], c)
        c1.append(c)
    for x in p2:
        r = jnp.zeros(x.shape, F32)
        for b in range(PEER_TOPK):
            r = r + jnp.where(v2[b] > x, 1.0, 0.0)
        r2.append(r)
    inv_z = 1.0 / z
    e1 = [jnp.exp(x - v1[0]) * inv_z for x in p1]
    e2 = [jnp.exp(x - v2[0]) for x in p2]
    cat = lambda ps: jnp.concatenate(ps, axis=0)
    return (cat(c1), cat(e1), cat(r2), cat(e2)), tied


def _extract_sorted(scores, by_key):
    nk, lanes = scores[0].shape
    kio = lax.broadcasted_iota(jnp.int32, (nk, lanes), 0).astype(F32)
    slot = lax.broadcasted_iota(jnp.int32, (PEER_TOPK, lanes), 0)

    def body(a, carry):
        here = slot == a
        out = []
        for (v, vals, aux), ranked in zip(carry, by_key):
            m = jnp.max(v, axis=0, keepdims=True)
            idx = jnp.min(jnp.where(v == m, kio, float(nk)), axis=0, keepdims=True)
            hit = kio == idx
            aux = jnp.where(hit, jnp.asarray(a, F32), aux) if ranked else jnp.where(here, idx, aux)
            out.append((jnp.where(hit, -jnp.inf, v), jnp.where(here, m, vals), aux))
        return tuple(out)

    small = jnp.zeros((PEER_TOPK, lanes), F32)
    unranked = jnp.full((nk, lanes), float(PEER_TOPK), F32)
    init = tuple((v, small, unranked if ranked else small) for v, ranked in zip(scores, by_key))
    return [(vals, aux) for _, vals, aux in lax.fori_loop(0, PEER_TOPK, body, init)]


def _route_head_exact(s1, s2, pos, neg):
    lanes = s1.shape[1]
    slot = lax.broadcasted_iota(jnp.int32, (PEER_TOPK, lanes), 0)
    kio = lax.broadcasted_iota(jnp.int32, (PEER_N_KEYS, lanes), 0).astype(F32)
    (v1, idx1), (v2, rank2) = _extract_sorted([s1, s2], [False, True])
    blocks = [v1[0:1] + v2[0:8], v1[0:1] + v2[8:16]]
    blocks += [v1[a:a + 1] + v2[0:8] for a in range(1, 8)]
    blocks += [v1[8:16] + v2[0:1]]
    cand = jnp.concatenate(blocks, axis=0) + neg

    def pick(kk, carry):
        cand, chosen, ts = carry
        m = jnp.max(cand, axis=0, keepdims=True)
        first = jnp.min(jnp.where(cand == m, pos, 1e9), axis=0, keepdims=True)
        hit = pos == first
        return (jnp.where(hit, -jnp.inf, cand), jnp.where(hit, 1.0, chosen), jnp.where(slot == kk, m, ts))

    _, chosen, ts = lax.fori_loop(0, PEER_TOPK, pick,
                                  (cand, jnp.zeros_like(cand), jnp.zeros((PEER_TOPK, lanes), F32)))
    z = jnp.sum(jnp.exp(ts - ts[0:1]), axis=0, keepdims=True)
    counts = [jnp.sum(chosen[0:16], axis=0, keepdims=True)]
    counts += [jnp.sum(chosen[8 * a + 8:8 * a + 16], axis=0, keepdims=True) for a in range(1, 8)]
    counts += [chosen[72 + a:73 + a] for a in range(8)]
    c1 = jnp.zeros((PEER_N_KEYS, lanes), F32)
    for a in range(PEER_TOPK):
        c1 = jnp.where(kio == idx1[a:a + 1], counts[a], c1)
    return c1, jnp.exp(s1 - v1[0:1]) / z, rank2, jnp.exp(s2 - v2[0:1])


def _route_kernel(h_ref, wq_ref, keys_ref, pos_ref, neg_ref,
                  c1_ref, e1_ref, r2_ref, e2_ref, qt_ref, sc_ref):
    half = PEER_QUERY_DIM // 2
    qt_ref[...] = _dot_nt(wq_ref[...], h_ref[...]).astype(BF16)

    def store(h, maps):
        c1, e1, r2, e2 = maps
        c1_ref[h] = c1
        e1_ref[h] = e1
        r2_ref[h] = r2.astype(BF16)
        e2_ref[h] = e2.astype(BF16)

    def head_body(h, _):
        r0 = pl.multiple_of(h * PEER_QUERY_DIM, PEER_QUERY_DIM)
        sc_ref[0] = _dot(keys_ref[2 * h], qt_ref[pl.ds(r0, half), :])
        sc_ref[1] = _dot(keys_ref[2 * h + 1], qt_ref[pl.ds(r0 + half, half), :])
        maps, tied = _route_head_fast(sc_ref[0], sc_ref[1])
        any_tied = jnp.max(jnp.where(tied, 1.0, 0.0)) > 0.0

        @pl.when(any_tied)
        def _():
            store(h, _route_head_exact(sc_ref[0], sc_ref[1], pos_ref[...], neg_ref[...]))

        @pl.when(jnp.logical_not(any_tied))
        def _():
            store(h, maps)

        return 0

    lax.fori_loop(0, PEER_HEADS, head_body, 0)


def _peer_route(h2, wq_t, keys, tt=LANES):
    T, D = h2.shape
    pos, neg = _cand_tables()
    stat_spec = pl.BlockSpec((PEER_HEADS, PEER_N_KEYS, tt), lambda i: (0, 0, i))
    stat = lambda dt: jax.ShapeDtypeStruct((PEER_HEADS, PEER_N_KEYS, T), dt)
    return pl.pallas_call(
        _route_kernel,
        grid=(T // tt,),
        in_specs=[pl.BlockSpec((tt, D), lambda i: (i, 0)),
                  pl.BlockSpec(wq_t.shape, lambda i: (0, 0)),
                  pl.BlockSpec(keys.shape, lambda i: (0, 0, 0)),
                  pl.BlockSpec((_CAND_ROWS, LANES), lambda i: (0, 0)),
                  pl.BlockSpec((_CAND_ROWS, LANES), lambda i: (0, 0))],
        out_specs=[stat_spec] * 4,
        out_shape=[stat(F32), stat(F32), stat(BF16), stat(BF16)],
        scratch_shapes=[pltpu.VMEM((PEER_HEADS * PEER_QUERY_DIM, tt), BF16),
                        pltpu.VMEM((2, PEER_N_KEYS, tt), F32)],
        compiler_params=_cparams(("parallel",)),
        name="peer_route",
    )(h2, wq_t, keys, pos, neg)


_KEY_GROUP = 8
_UNITS = 2


def _build_gated(a_ref, p_ref, c1_ref, e1_ref, r2_ref, e2_ref, key0, g0, ng, lt):
    rep = PEER_N_KEYS // 16

    def rows16(row):
        blk = jnp.broadcast_to(row, (16, LANES)).astype(BF16)
        return jnp.concatenate([blk] * rep, axis=0)

    ls = slice(lt * LANES, (lt + 1) * LANES)
    w = [jnp.zeros((PEER_N_KEYS, LANES), BF16) for _ in range(ng)]
    for h in range(PEER_HEADS):
        c1 = c1_ref[h, pl.ds(key0, _KEY_GROUP), ls]
        e1 = e1_ref[h, pl.ds(key0, _KEY_GROUP), ls]
        r2 = r2_ref[h, :, ls]
        e2 = e2_ref[h, :, ls]
        for g in range(ng):
            thr = rows16(c1[g0 + g:g0 + g + 1])
            gate = rows16(e1[g0 + g:g0 + g + 1])
            w[g] = w[g] + jnp.where(r2 < thr, e2, jnp.zeros_like(e2)) * gate
    for g in range(g0, g0 + ng):
        rs = slice(g * PEER_N_KEYS, (g + 1) * PEER_N_KEYS)
        a = a_ref[rs, ls]
        gelu = 0.5 * a * (1.0 + lax.erf(a * (2.0 ** -0.5)))
        p_ref[rs, ls] = gelu.astype(BF16) * w[g - g0]


def _experts_kernel(h_ref, dn_ref, upt_ref, c1_ref, e1_ref, r2_ref, e2_ref, x_ref, *rest, emit_norm):
    if emit_norm:
        g_ref, o_ref, hn_ref, a_ref, p_ref, acc_ref = rest
    else:
        o_ref, a_ref, p_ref, acc_ref = rest
    te, tt = a_ref.shape
    d_model = acc_ref.shape[0]
    e = pl.program_id(1)
    n_tiles = pl.num_programs(1) - 1
    cur = e % 2
    ng = _KEY_GROUP // _UNITS
    n_lane = tt // LANES
    key0 = pl.multiple_of(jnp.minimum(e, n_tiles - 1) * _KEY_GROUP, _KEY_GROUP)

    def front_mm(u):
        rows = te // _UNITS
        rs = slice(u * rows, (u + 1) * rows)
        a_ref[rs, :] = _dot_nt(dn_ref[rs, :], h_ref[...])

    def back_mm(u):
        rows = d_model // _UNITS
        rs = slice(u * rows, (u + 1) * rows)
        acc_ref[rs, :] += _dot(upt_ref[rs, :], p_ref[1 - cur])

    def run(front, back):
        if front:
            front_mm(0)
        for u in range(_UNITS):
            for lt in range(n_lane):
                if front:
                    _build_gated(a_ref, p_ref.at[cur], c1_ref, e1_ref, r2_ref, e2_ref, key0, u * ng, ng, lt)
                if front and lt == 0 and u + 1 < _UNITS:
                    front_mm(u + 1)
                if back and lt == n_lane // 2:
                    back_mm(u)

    @pl.when(e == 0)
    def _():
        acc_ref[...] = jnp.zeros_like(acc_ref)
        run(True, False)

    @pl.when((e > 0) & (e < n_tiles))
    def _():
        run(True, True)

    @pl.when(e == n_tiles)
    def _():
        run(False, True)
        y = x_ref[...] + acc_ref[...].T
        o_ref[...] = y
        if emit_norm:
            hn_ref[...] = _rms_normed(y, g_ref[...])


def _peer_experts(h2, down, up_t, stats, x, next_gain, tt=512):
    T, D = h2.shape
    E = down.shape[0]
    te = _KEY_GROUP * PEER_N_KEYS
    n_tiles = E // te
    emit_norm = next_gain is not None
    stat_spec = pl.BlockSpec((PEER_HEADS, PEER_N_KEYS, tt), lambda i, e: (0, 0, i))
    row_spec = pl.BlockSpec((tt, D), lambda i, e: (i, 0))
    in_specs = [row_spec,
                pl.BlockSpec((te, D), lambda i, e: (jnp.minimum(e, n_tiles - 1), 0)),
                pl.BlockSpec((D, te), lambda i, e: (0, jnp.maximum(e - 1, 0))),
                stat_spec, stat_spec, stat_spec, stat_spec,
                row_spec]
    args = [h2, down, up_t, *stats, x]
    out_specs = [row_spec]
    out_shape = [jax.ShapeDtypeStruct((T, D), F32)]
    if emit_norm:
        in_specs.append(pl.BlockSpec((1, D), lambda i, e: (0, 0)))
        args.append(next_gain.reshape(1, D).astype(F32))
        out_specs.append(row_spec)
        out_shape.append(jax.ShapeDtypeStruct((T, D), BF16))
    res = pl.pallas_call(
        functools.partial(_experts_kernel, emit_norm=emit_norm),
        grid=(T // tt, n_tiles + 1),
        in_specs=in_specs, out_specs=out_specs, out_shape=out_shape,
        scratch_shapes=[pltpu.VMEM((te, tt), F32),
                        pltpu.VMEM((2, te, tt), BF16),
                        pltpu.VMEM((D, tt), F32)],
        compiler_params=_cparams(("parallel", "arbitrary")),
        name="peer_experts",
    )(*args)
    return (res[0], res[1]) if emit_norm else (res[0], None)


def _peer_layer(x, h2, next_gain, w_query, sub_keys, down, up):
    keys = sub_keys.reshape(PEER_HEADS * 2, PEER_N_KEYS, PEER_QUERY_DIM // 2).astype(BF16)
    stats = _peer_route(h2, w_query.T.astype(BF16), keys)
    return _peer_experts(h2, down.astype(BF16), up.T.astype(BF16), stats, x, next_gain)


def _tile_heads(g, n):
    return jnp.tile(g.astype(F32), n)


def _even_mixer(x, h, ffn_gain, w_in, f_bias, qn_a, kn_a, qn_b, kn_b, w_out, tabs, *, batch, seq):
    w = w_in.astype(BF16)
    o_qa, o_ka, o_va, o_qb, o_kb, o_vb, o_gb, o_fb = (
        0, A_W, 2 * A_W, 3 * A_W, 3 * A_W + B_W, 3 * A_W + 2 * B_W, 3 * A_W + 3 * B_W, 3 * A_W + 4 * B_W)
    gain_a = jnp.concatenate([_tile_heads(qn_a, MOBA_HEADS), _tile_heads(kn_a, MOBA_HEADS)])
    qk_a, km = _proj(h, w[:, o_qa:o_va], seq=seq, tn=A_W, tm=MOBA_BLOCK, gain=gain_a,
                     rope_tabs=tabs, kmean=True)
    nb = seq // MOBA_BLOCK
    kmean = km.reshape(batch, nb, 2 * A_W)[:, :, A_W:]
    kmean = jnp.pad(kmean, ((0, 0), (0, LANES - nb), (0, 0))).astype(BF16)
    gain_b = jnp.concatenate([_tile_heads(qn_b, FOX_HEADS), _tile_heads(kn_b, FOX_HEADS)])
    qk_b = _proj(h, w[:, o_qb:o_vb], seq=seq, tn=B_W, gain=gain_b)
    w_plain = jnp.concatenate([w[:, o_va:o_qb], w[:, o_vb:o_fb]], axis=1)
    vvg = _proj(h, w_plain, seq=seq, tn=A_W)
    va, vb, gb = vvg[:, :A_W], vvg[:, A_W:A_W + B_W], vvg[:, A_W + B_W:]
    frow = _fox_gates(h, w[:, o_fb:].T, f_bias, batch=batch, seq=seq)
    fcol = jnp.pad(frow.transpose(0, 2, 1), ((0, 0), (0, 0), (0, LANES - FOX_HEADS)))
    oa = _moba(qk_a[:, :A_W], qk_a[:, A_W:], va, kmean, batch=batch, seq=seq)
    ob = _fox(qk_b[:, :B_W], qk_b[:, B_W:], vb, gb, fcol, frow, batch=batch, seq=seq)
    return _outproj([oa, ob], w_out.astype(BF16), x, ffn_gain)


def _odd_mixer(x, h, ffn_gain, w_in, qn, kn, sinks, w_out, tabs, *, batch, seq):
    w = w_in.astype(BF16)
    qw = SWA_Q_HEADS * HEAD_DIM
    kw = SWA_KV_HEADS * HEAD_DIM
    q = _proj(h, w[:, :qw], seq=seq, tn=512, gain=_tile_heads(qn, SWA_Q_HEADS), rope_tabs=tabs)
    k = _proj(h, w[:, qw:qw + kw], seq=seq, tn=kw, gain=_tile_heads(kn, SWA_KV_HEADS), rope_tabs=tabs)
    v = _proj(h, w[:, qw + kw:], seq=seq, tn=kw)
    o = _swa(q, k, v, sinks, batch=batch, seq=seq)
    return _outproj([o], w_out.astype(BF16), x, ffn_gain)


def kernel(x, attn_norm, ffn_norm, ev_w_in, ev_forget_bias, ev_q_norm_a, ev_k_norm_a, ev_q_norm_b,
           ev_k_norm_b, ev_w_out, od_w_in, od_q_norm, od_k_norm, od_sinks, od_w_out,
           peer_w_query, peer_sub_keys, peer_down, peer_up):
    batch, seq, d_model = x.shape
    depth = attn_norm.shape[0]
    tabs = _rope_tables(seq)
    xt = x.reshape(batch * seq, d_model)
    h = _rmsnorm(xt, attn_norm[0])
    for l in range(depth):
        i = l // 2
        if l % 2 == 0:
            xt, h2 = _even_mixer(xt, h, ffn_norm[l], ev_w_in[i], ev_forget_bias[i], ev_q_norm_a[i],
                                 ev_k_norm_a[i], ev_q_norm_b[i], ev_k_norm_b[i], ev_w_out[i], tabs,
                                 batch=batch, seq=seq)
        else:
            xt, h2 = _odd_mixer(xt, h, ffn_norm[l], od_w_in[i], od_q_norm[i], od_k_norm[i], od_sinks[i],
                                od_w_out[i], tabs, batch=batch, seq=seq)
        next_gain = attn_norm[l + 1] if l + 1 < depth else None
        xt, h = _peer_layer(xt, h2, next_gain, peer_w_query[l], peer_sub_keys[l], peer_down[l], peer_up[l])
    return xt.reshape(batch, seq, d_model)
```

```python
import functools

import numpy as np
import jax
import jax.numpy as jnp
from jax import lax
from jax.experimental import pallas as pl
from jax.experimental.pallas import tpu as pltpu

F32 = jnp.float32
BF16 = jnp.bfloat16

HEAD_DIM = 64
ROT_DIM = HEAD_DIM // 4
ROPE_THETA = 500000.0
ATTN_SCALE = HEAD_DIM ** -0.5
EPS = 1e-6
NEG_INF = -1e30

MOBA_HEADS = 8
FOX_HEADS = 8
MOBA_BLOCK = 256
MOBA_TOPK = 3
A_W = MOBA_HEADS * HEAD_DIM
B_W = FOX_HEADS * HEAD_DIM

SWA_Q_HEADS = 16
SWA_KV_HEADS = 2
SWA_WINDOW = 128

PEER_HEADS = 8
PEER_N_KEYS = 128
PEER_TOPK = 16
PEER_QUERY_DIM = 128

LANES = 128
PAIR_W = 2 * HEAD_DIM
VMEM_LIMIT = 48 * 1024 * 1024


def _cparams(sem):
    return pltpu.CompilerParams(dimension_semantics=sem, vmem_limit_bytes=VMEM_LIMIT)


def _dot_nt(a, b):
    return lax.dot_general(a, b, (((1,), (1,)), ((), ())), preferred_element_type=F32)


def _dot(a, b):
    return jnp.dot(a, b, preferred_element_type=F32)


def _split3(x):
    h1 = x.astype(BF16)
    r1 = x - h1.astype(F32)
    h2 = r1.astype(BF16)
    h3 = (r1 - h2.astype(F32)).astype(BF16)
    return h1, h2, h3


def _rmsnorm_kernel(x_ref, g_ref, o_ref):
    x = x_ref[...]
    ms = jnp.mean(x * x, axis=-1, keepdims=True)
    o_ref[...] = (x * lax.rsqrt(ms + EPS) * g_ref[...]).astype(o_ref.dtype)


def _rmsnorm(x, gain, tm=512):
    T, D = x.shape
    return pl.pallas_call(
        _rmsnorm_kernel,
        grid=(T // tm,),
        in_specs=[pl.BlockSpec((tm, D), lambda i: (i, 0)),
                  pl.BlockSpec((1, D), lambda i: (0, 0))],
        out_specs=pl.BlockSpec((tm, D), lambda i: (i, 0)),
        out_shape=jax.ShapeDtypeStruct((T, D), BF16),
        compiler_params=_cparams(("parallel",)),
        name="rmsnorm",
    )(x, gain.reshape(1, D))


def _proj_kernel(*refs, norm, rope, kmean, tn):
    it = iter(refs)
    h_ref, w_ref = next(it), next(it)
    gain_ref = next(it) if norm else None
    bd_ref = next(it) if norm else None
    if rope:
        c_ref, sa_ref, sb_ref = next(it), next(it), next(it)
    o_ref = next(it)
    km_ref = next(it) if kmean else None

    y = _dot(h_ref[...], w_ref[...])
    if norm:
        y2 = y * y
        bd = bd_ref[...]
        cols = []
        for c in range(tn // LANES):
            h1, h2, h3 = _split3(y2[:, c * LANES:(c + 1) * LANES])
            cols.append(_dot(h1, bd) + _dot(h2, bd) + _dot(h3, bd))
        ms = cols[0] if len(cols) == 1 else jnp.concatenate(cols, axis=1)
        y = y * lax.rsqrt(ms + EPS) * gain_ref[...]
    if rope:
        rep = tn // LANES
        tile = (lambda t: t) if rep == 1 else (lambda t: jnp.concatenate([t] * rep, axis=1))
        y = (y * tile(c_ref[...])
             + pltpu.roll(y, tn - ROT_DIM // 2, 1) * tile(sa_ref[...])
             + pltpu.roll(y, ROT_DIM // 2, 1) * tile(sb_ref[...]))
    o_ref[...] = y.astype(o_ref.dtype)
    if kmean:
        km_ref[0] = jnp.mean(y, axis=0, keepdims=True)


def _proj(h, w, *, seq, tn, tm=512, gain=None, rope_tabs=None, kmean=False):
    T, D = h.shape
    N = w.shape[1]
    norm = gain is not None
    rope = rope_tabs is not None
    nseq = seq // tm
    in_specs = [pl.BlockSpec((tm, D), lambda i, j: (i, 0)),
                pl.BlockSpec((D, tn), lambda i, j: (0, j))]
    args = [h, w]
    if norm:
        bd = np.kron(np.eye(LANES // HEAD_DIM), np.ones((HEAD_DIM, HEAD_DIM))) / HEAD_DIM
        in_specs += [pl.BlockSpec((1, tn), lambda i, j: (0, j)),
                     pl.BlockSpec((LANES, LANES), lambda i, j: (0, 0))]
        args += [gain.reshape(1, N).astype(F32), jnp.asarray(bd, BF16)]
    if rope:
        in_specs += [pl.BlockSpec((tm, LANES), lambda i, j: (i % nseq, 0))] * 3
        args += list(rope_tabs)
    out_specs = [pl.BlockSpec((tm, tn), lambda i, j: (i, j))]
    out_shape = [jax.ShapeDtypeStruct((T, N), BF16)]
    if kmean:
        out_specs.append(pl.BlockSpec((1, 1, tn), lambda i, j: (i, 0, j)))
        out_shape.append(jax.ShapeDtypeStruct((T // tm, 1, N), F32))
    res = pl.pallas_call(
        functools.partial(_proj_kernel, norm=norm, rope=rope, kmean=kmean, tn=tn),
        grid=(T // tm, N // tn),
        in_specs=in_specs, out_specs=out_specs, out_shape=out_shape,
        compiler_params=_cparams(("parallel", "parallel")),
        name="proj",
    )(*args)
    return res if kmean else res[0]


def _rope_tables(seq):
    half = ROT_DIM // 2
    inv_freq = jnp.power(ROPE_THETA, -jnp.arange(0, ROT_DIM, 2, dtype=F32) / ROT_DIM)
    ang = jnp.arange(seq, dtype=F32)[:, None] * inv_freq[None, :]
    cos, sin = jnp.cos(ang), jnp.sin(ang)
    one = jnp.ones((seq, HEAD_DIM - ROT_DIM), F32)
    zero = jnp.zeros((seq, HEAD_DIM - ROT_DIM), F32)
    z8 = jnp.zeros((seq, half), F32)
    c = jnp.concatenate([cos, cos, one], axis=1)
    sa = jnp.concatenate([-sin, z8, zero], axis=1)
    sb = jnp.concatenate([z8, sin, zero], axis=1)
    rep = LANES // HEAD_DIM
    return tuple(jnp.concatenate([t] * rep, axis=1) for t in (c, sa, sb))


def _gates_kernel(h_ref, wf_ref, b_ref, tri_ref, o_ref, carry_ref):
    @pl.when(pl.program_id(1) == 0)
    def _():
        carry_ref[...] = jnp.zeros_like(carry_ref)

    z = _dot_nt(wf_ref[...], h_ref[...]) + b_ref[...][:, :1]
    lf = jnp.minimum(z, 0.0) - jnp.log1p(jnp.exp(-jnp.abs(z)))
    tri = tri_ref[...]
    h1, h2, h3 = _split3(lf)
    cs = _dot(h1, tri) + _dot(h2, tri) + _dot(h3, tri) + carry_ref[...][:, :1]
    o_ref[0] = cs
    carry_ref[...] = jnp.broadcast_to(cs[:, -1:], carry_ref.shape)


def _fox_gates(h, wf_t, bias, *, batch, seq, tm=512):
    T, D = h.shape
    nh = wf_t.shape[0]
    nseq = seq // tm
    tri = jnp.asarray(np.triu(np.ones((tm, tm))), BF16)
    return pl.pallas_call(
        _gates_kernel,
        grid=(batch, nseq),
        in_specs=[pl.BlockSpec((tm, D), lambda b, s: (b * nseq + s, 0)),
                  pl.BlockSpec((nh, D), lambda b, s: (0, 0)),
                  pl.BlockSpec((nh, LANES), lambda b, s: (0, 0)),
                  pl.BlockSpec((tm, tm), lambda b, s: (0, 0))],
        out_specs=pl.BlockSpec((1, nh, tm), lambda b, s: (b, 0, s)),
        out_shape=jax.ShapeDtypeStruct((batch, nh, seq), F32),
        scratch_shapes=[pltpu.VMEM((nh, LANES), F32)],
        compiler_params=_cparams(("parallel", "arbitrary")),
        name="fox_gates",
    )(h, wf_t, jnp.broadcast_to(bias.astype(F32)[:, None], (nh, LANES)), tri)


def _lane_tile(x, width):
    rep = width // LANES
    return x if rep == 1 else jnp.concatenate([x] * rep, axis=1)


def _flash_init(m_ref, acc_ref):
    m_ref[...] = jnp.full(m_ref.shape, NEG_INF, F32)
    acc_ref[...] = jnp.zeros(acc_ref.shape, F32)


def _head_values(v):
    lane = lax.broadcasted_iota(jnp.int32, v.shape, 1)
    return [jnp.where((lane >= hh * HEAD_DIM) & (lane < (hh + 1) * HEAD_DIM), v, jnp.ones_like(v))
            for hh in range(2)]


def _flash_update(slot, s, v, m_ref, acc_ref):
    tk = s.shape[1]
    m_prev = m_ref[slot]
    m_new = jnp.maximum(m_prev, jnp.max(s, axis=1, keepdims=True))
    alpha = jnp.exp(m_prev - m_new)
    p = jnp.exp(s - _lane_tile(m_new, tk))
    acc_ref[slot] = alpha * acc_ref[slot] + _dot(p.astype(BF16), v)
    m_ref[slot] = m_new


def _flash_finish(lane, acc_ref):
    outs = []
    for hh in range(2):
        acc = acc_ref[hh]
        den = (1 - hh) * HEAD_DIM
        outs.append(acc / acc[:, den:den + 1])
    return jnp.where(lane < HEAD_DIM, outs[0], outs[1])


def _flash_tiles(qi, tk, logits, values, m_ref, acc_ref):
    def absorb(s, off):
        vh = values(off)
        for hh in range(2):
            _flash_update(hh, s[hh], vh[hh], m_ref, acc_ref)

    def pair(off_a, off_b, b_diagonal):
        sa = logits(off_a, False)
        sb = logits(off_b, b_diagonal)
        absorb(sa, off_a)
        absorb(sb, off_b)

    def body(jj, carry):
        off = pl.multiple_of(2 * jj * tk, 2 * tk)
        pair(off, off + tk, False)
        return carry

    lax.fori_loop(0, qi // 2, body, 0)
    diag = pl.multiple_of(qi * tk, tk)

    @pl.when(qi % 2 == 1)
    def _():
        pair(diag - tk, diag, True)

    @pl.when(qi % 2 == 0)
    def _():
        absorb(logits(diag, True), diag)


def _head_queries(q, lane):
    qs = q * ATTN_SCALE
    return [jnp.where((lane >= hh * HEAD_DIM) & (lane < (hh + 1) * HEAD_DIM), qs, jnp.zeros_like(qs))
            for hh in range(2)]


def _moba_kernel(q_ref, k_ref, v_ref, km_ref, o_ref, m_ref, acc_ref):
    tq = q_ref.shape[1]
    tk = tq
    qi = pl.program_id(2)
    lane = lax.broadcasted_iota(jnp.int32, (tq, LANES), 1)
    lane_f = lane.astype(F32)
    rowv = lax.broadcasted_iota(jnp.int32, (tq, LANES), 0)
    row_blk = 2 * qi + (rowv >= MOBA_BLOCK).astype(jnp.int32)
    row = lax.broadcasted_iota(jnp.int32, (tq, tk), 0)
    col = lax.broadcasted_iota(jnp.int32, (tq, tk), 1)
    qh = _head_queries(q_ref[0], lane)
    _flash_init(m_ref, acc_ref)

    sels = []
    for hh in range(2):
        gate = _dot_nt(qh[hh], km_ref[0])
        gate = jnp.where(lane < row_blk, gate, -jnp.inf)
        sel = jnp.zeros((tq, LANES), F32)
        for _ in range(MOBA_TOPK):
            m = jnp.max(gate, axis=1, keepdims=True)
            idx = jnp.min(jnp.where(gate == m, lane_f, float(LANES)), axis=1, keepdims=True)
            hit = lane_f == idx
            sel = jnp.where(hit & (m > -jnp.inf), 1.0, sel)
            gate = jnp.where(hit, -jnp.inf, gate)
        sels.append(sel)

    def chosen(sel, blk):
        return jnp.max(jnp.where(lane == blk, sel, 0.0), axis=1, keepdims=True) > 0.0

    def logits(off, diagonal):
        kj = k_ref[0, pl.ds(off, tk), :]
        out = []
        for hh in range(2):
            s = _dot_nt(qh[hh], kj)
            if diagonal:
                visible = (col >= MOBA_BLOCK) | (row < MOBA_BLOCK) | chosen(sels[hh], 2 * qi)
                s = jnp.where((col <= row) & visible, s, NEG_INF)
            else:
                blk = 2 * (off // tk)
                s = jnp.concatenate(
                    [jnp.where(chosen(sels[hh], blk), s[:, :MOBA_BLOCK], NEG_INF),
                     jnp.where(chosen(sels[hh], blk + 1), s[:, MOBA_BLOCK:], NEG_INF)], axis=1)
            out.append(s)
        return out

    def values(off):
        return _head_values(v_ref[0, pl.ds(off, tk), :])

    _flash_tiles(qi, tk, logits, values, m_ref, acc_ref)
    o_ref[0] = _flash_finish(lane, acc_ref).astype(o_ref.dtype)


def _flash_scratch(tq):
    return [pltpu.VMEM((2, tq, LANES), F32)] * 2


def _moba(q, k, v, kmean, *, batch, seq):
    W = q.shape[1]
    tq = 2 * MOBA_BLOCK
    q3, k3, v3 = (t.reshape(batch, seq, W) for t in (q, k, v))
    out = pl.pallas_call(
        _moba_kernel,
        grid=(batch, W // PAIR_W, seq // tq),
        in_specs=[pl.BlockSpec((1, tq, PAIR_W), lambda b, p, i: (b, i, p)),
                  pl.BlockSpec((1, seq, PAIR_W), lambda b, p, i: (b, 0, p)),
                  pl.BlockSpec((1, seq, PAIR_W), lambda b, p, i: (b, 0, p)),
                  pl.BlockSpec((1, LANES, PAIR_W), lambda b, p, i: (b, 0, p))],
        out_specs=pl.BlockSpec((1, tq, PAIR_W), lambda b, p, i: (b, i, p)),
        out_shape=jax.ShapeDtypeStruct((batch, seq, W), BF16),
        scratch_shapes=_flash_scratch(tq),
        compiler_params=_cparams(("parallel", "parallel", "parallel")),
        name="moba",
    )(q3, k3, v3, kmean)
    return out.reshape(batch * seq, W)


def _fox_kernel(q_ref, k_ref, v_ref, g_ref, fcol_ref, frow_ref, o_ref, m_ref, acc_ref):
    tq = q_ref.shape[1]
    tk = tq
    pr = pl.program_id(1)
    qi = pl.program_id(2)
    lane = lax.broadcasted_iota(jnp.int32, (tq, LANES), 1)
    row = lax.broadcasted_iota(jnp.int32, (tq, tk), 0)
    col = lax.broadcasted_iota(jnp.int32, (tq, tk), 1)
    qh = _head_queries(q_ref[0], lane)
    fcol = fcol_ref[0]
    fq = [jnp.sum(jnp.where(lane == 2 * pr + hh, fcol, 0.0), axis=1, keepdims=True) for hh in range(2)]
    _flash_init(m_ref, acc_ref)

    def logits(off, diagonal):
        kj = k_ref[0, pl.ds(off, tk), :]
        fk_all = frow_ref[0, :, pl.ds(off, tk)]
        sub = lax.broadcasted_iota(jnp.int32, fk_all.shape, 0)
        out = []
        for hh in range(2):
            fk = jnp.sum(jnp.where(sub == 2 * pr + hh, fk_all, 0.0), axis=0, keepdims=True)
            s = _dot_nt(qh[hh], kj) + fq[hh] - fk
            out.append(jnp.where(col <= row, s, NEG_INF) if diagonal else s)
        return out

    def values(off):
        return _head_values(v_ref[0, pl.ds(off, tk), :])

    _flash_tiles(qi, tk, logits, values, m_ref, acc_ref)
    o = _flash_finish(lane, acc_ref)
    o_ref[0] = (o * jax.nn.sigmoid(g_ref[0].astype(F32))).astype(o_ref.dtype)


def _fox(q, k, v, g, fcol, frow, *, batch, seq, tq=512):
    W = q.shape[1]
    nh = frow.shape[1]
    q3, k3, v3, g3 = (t.reshape(batch, seq, W) for t in (q, k, v, g))
    out = pl.pallas_call(
        _fox_kernel,
        grid=(batch, W // PAIR_W, seq // tq),
        in_specs=[pl.BlockSpec((1, tq, PAIR_W), lambda b, p, i: (b, i, p)),
                  pl.BlockSpec((1, seq, PAIR_W), lambda b, p, i: (b, 0, p)),
                  pl.BlockSpec((1, seq, PAIR_W), lambda b, p, i: (b, 0, p)),
                  pl.BlockSpec((1, tq, PAIR_W), lambda b, p, i: (b, i, p)),
                  pl.BlockSpec((1, tq, LANES), lambda b, p, i: (b, i, 0)),
                  pl.BlockSpec((1, nh, seq), lambda b, p, i: (b, 0, 0))],
        out_specs=pl.BlockSpec((1, tq, PAIR_W), lambda b, p, i: (b, i, p)),
        out_shape=jax.ShapeDtypeStruct((batch, seq, W), BF16),
        scratch_shapes=_flash_scratch(tq),
        compiler_params=_cparams(("parallel", "parallel", "parallel")),
        name="fox",
    )(q3, k3, v3, g3, fcol, frow)
    return out.reshape(batch * seq, W)


def _swa_kernel(q_ref, k_ref, v_ref, sink_ref, bias_ref, o_ref):
    tq = q_ref.shape[1]
    qi = pl.program_id(1)
    group = SWA_Q_HEADS // SWA_KV_HEADS
    tk = tq + SWA_WINDOW
    lane = lax.broadcasted_iota(jnp.int32, (tq, LANES), 1)
    kstart = pl.multiple_of(jnp.maximum(qi * tq - SWA_WINDOW, 0), SWA_WINDOW)
    k = k_ref[0, pl.ds(kstart, tk), :]
    vh = _head_values(v_ref[0, pl.ds(kstart, tk), :])
    bias = bias_ref[jnp.minimum(qi, 1)]
    bias = jnp.concatenate([bias] * group, axis=0)
    sink_tab = sink_ref[...]
    outs = [None] * SWA_Q_HEADS
    for c in range(SWA_KV_HEADS):
        pieces, sinks = [], []
        for g in range(group):
            head = c * group + g
            blk = q_ref[0, :, (head // 2) * PAIR_W:(head // 2 + 1) * PAIR_W] * ATTN_SCALE
            hh = head % 2
            qm = jnp.where((lane >= hh * HEAD_DIM) & (lane < (hh + 1) * HEAD_DIM), blk, jnp.zeros_like(blk))
            if hh != c:
                qm = pltpu.roll(qm.astype(F32), HEAD_DIM, 1).astype(BF16)
            pieces.append(qm)
            sinks.append(jnp.broadcast_to(sink_tab[head:head + 1, :], (tq, LANES)))
        qs = jnp.concatenate(pieces, axis=0)
        sink = jnp.concatenate(sinks, axis=0)
        s = _dot_nt(qs, k) + bias
        m = jnp.maximum(sink, jnp.max(s, axis=1, keepdims=True))
        p = jnp.exp(s - _lane_tile(m, tk))
        acc = _dot(p.astype(BF16), vh[c])
        den = pltpu.roll(acc, HEAD_DIM, 1) + jnp.exp(sink - m)
        o = acc / den
        for g in range(group):
            head = c * group + g
            oh = o[g * tq:(g + 1) * tq]
            outs[head] = oh if head % 2 == c else pltpu.roll(oh, HEAD_DIM, 1)
    for pp in range(SWA_Q_HEADS // 2):
        o_ref[0, :, pp * PAIR_W:(pp + 1) * PAIR_W] = jnp.where(
            lane < HEAD_DIM, outs[2 * pp], outs[2 * pp + 1]).astype(o_ref.dtype)


def _swa_bias(tq):
    r = np.arange(tq)[:, None]
    c = np.arange(tq + SWA_WINDOW)[None, :]
    tabs = []
    for key_offset in (0, SWA_WINDOW):
        dist = r + key_offset - c
        tabs.append(np.where((dist >= 0) & (dist < SWA_WINDOW), 0.0, NEG_INF))
    return jnp.asarray(np.stack(tabs), F32)


def _swa(q, k, v, sinks, *, batch, seq, tq=SWA_WINDOW):
    W = q.shape[1]
    tk = tq + SWA_WINDOW
    q3 = q.reshape(batch, seq, W)
    k3, v3 = (t.reshape(batch, seq, PAIR_W) for t in (k, v))
    sink_tab = jnp.broadcast_to(sinks.astype(F32)[:, None], (SWA_Q_HEADS, LANES))
    out = pl.pallas_call(
        _swa_kernel,
        grid=(batch, seq // tq),
        in_specs=[pl.BlockSpec((1, tq, W), lambda b, i: (b, i, 0)),
                  pl.BlockSpec((1, seq, PAIR_W), lambda b, i: (b, 0, 0)),
                  pl.BlockSpec((1, seq, PAIR_W), lambda b, i: (b, 0, 0)),
                  pl.BlockSpec((SWA_Q_HEADS, LANES), lambda b, i: (0, 0)),
                  pl.BlockSpec((2, tq, tk), lambda b, i: (0, 0, 0))],
        out_specs=pl.BlockSpec((1, tq, W), lambda b, i: (b, i, 0)),
        out_shape=jax.ShapeDtypeStruct((batch, seq, W), BF16),
        compiler_params=_cparams(("parallel", "parallel")),
        name="swa",
    )(q3, k3, v3, sink_tab, _swa_bias(tq))
    return out.reshape(batch * seq, W)


def _rms_normed(x, gain):
    ms = jnp.mean(x * x, axis=-1, keepdims=True)
    return (x * lax.rsqrt(ms + EPS) * gain).astype(BF16)


def _outproj_kernel(*refs, n_parts):
    parts = refs[:n_parts]
    w_ref, x_ref, g_ref, o_ref, h_ref = refs[n_parts:]
    y = x_ref[...]
    off = 0
    for p_ref in parts:
        kw = p_ref.shape[1]
        y = y + _dot(p_ref[...], w_ref[off:off + kw, :])
        off += kw
    o_ref[...] = y
    h_ref[...] = _rms_normed(y, g_ref[...])


def _outproj(parts, w, x, next_gain, tm=512):
    T, D = x.shape
    in_specs = [pl.BlockSpec((tm, p.shape[1]), lambda i: (i, 0)) for p in parts]
    in_specs += [pl.BlockSpec(w.shape, lambda i: (0, 0)),
                 pl.BlockSpec((tm, D), lambda i: (i, 0)),
                 pl.BlockSpec((1, D), lambda i: (0, 0))]
    row_spec = pl.BlockSpec((tm, D), lambda i: (i, 0))
    return pl.pallas_call(
        functools.partial(_outproj_kernel, n_parts=len(parts)),
        grid=(T // tm,),
        in_specs=in_specs,
        out_specs=[row_spec, row_spec],
        out_shape=[jax.ShapeDtypeStruct((T, D), F32), jax.ShapeDtypeStruct((T, D), BF16)],
        compiler_params=_cparams(("parallel",)),
        name="outproj",
    )(*parts, w, x, next_gain.reshape(1, D).astype(F32))


_CAND_ROWS = 80


def _cand_tables(lanes):
    pos = np.zeros((_CAND_ROWS,), np.float32)
    neg = np.zeros((_CAND_ROWS,), np.float32)
    r = 0
    for a, nb in ((0, 16), (1, 8), (2, 8), (3, 8), (4, 8), (5, 8), (6, 8), (7, 8)):
        for b in range(nb):
            pos[r] = a * PEER_TOPK + b
            neg[r] = 0.0 if (a + 1) * (b + 1) <= PEER_TOPK else -np.inf
            r += 1
    for a in range(8, 16):
        pos[r] = a * PEER_TOPK
        r += 1
    assert r == _CAND_ROWS
    tab = lambda t: jnp.asarray(np.broadcast_to(t[:, None], (_CAND_ROWS, lanes)).copy())
    return tab(pos), tab(neg)


def _batcher_pairs(n):
    pairs, p = [], 1
    while p < n:
        k = p
        while k >= 1:
            for j in range(k % p, n - k, 2 * k):
                for i in range(min(k, n - j - k)):
                    if (i + j) // (2 * p) == (i + j + k) // (2 * p):
                        pairs.append((i + j, i + j + k))
            k //= 2
        p *= 2
    return pairs


_SORT16 = _batcher_pairs(PEER_TOPK)
_N_CAND_PIECES = 10
_SORT10 = [(i, j) for i, j in _SORT16 if j < _N_CAND_PIECES]
_SUBLANES = 8


def _compare_exchange(items, i, j):
    items[i], items[j] = jnp.maximum(items[i], items[j]), jnp.minimum(items[i], items[j])


def _top_sorted(pieces, pairs):
    items = list(pieces)
    for i, j in pairs:
        _compare_exchange(items, i, j)
    n = PEER_TOPK
    items += [jnp.full(items[0].shape, -jnp.inf, F32)] * (n - len(items))
    for shift in (4, 2, 1):
        items = [jnp.maximum(items[i], pltpu.roll(items[n - 1 - i], shift, 0)) for i in range(n)]
        d = n // 2
        while d >= 1:
            for i in range(n):
                if i & d == 0:
                    _compare_exchange(items, i, i + d)
            d //= 2
    return items


def _pieces(x):
    return [x[_SUBLANES * g:_SUBLANES * (g + 1)] for g in range(x.shape[0] // _SUBLANES)]


def _sublane_total(x):
    for shift in (4, 2, 1):
        x = x + pltpu.roll(x, shift, 0)
    return x


def _count_ge(pieces, thr):
    total = jnp.zeros(thr.shape, F32)
    for p in pieces:
        total = total + jnp.where(p >= thr, 1.0, 0.0)
    return _sublane_total(total)


def _route_head_fast(s1, s2):
    p1, p2 = _pieces(s1), _pieces(s2)
    v1 = _top_sorted(p1, _SORT16)
    v2 = _top_sorted(p2, _SORT16)
    sub = lax.broadcasted_iota(jnp.int32, v1[0].shape, 0)

    def spread(vals):
        out = vals[0]
        for r in range(1, _SUBLANES):
            out = jnp.where(sub == r, vals[r], out)
        return out

    v2_lo, v2_hi, v1_hi = spread(v2[:8]), spread(v2[8:]), spread(v1[8:])
    cands = [v1[0] + v2_lo, v1[0] + v2_hi, v1[1] + v2_lo]
    for a in range(2, 8):
        cands.append(jnp.where(sub < PEER_TOPK // (a + 1), v1[a] + v2_lo, -jnp.inf))
    cands.append(v1_hi + v2[0])
    ts = _top_sorted(cands, _SORT10)
    tau = ts[PEER_TOPK - 1]
    z = jnp.exp(ts[0] - ts[0])
    for kk in range(1, PEER_TOPK):
        z = z + jnp.exp(ts[kk] - ts[0])

    tied = (_count_ge(p1, v1[-1]) != float(PEER_TOPK)) | (_count_ge(p2, v2[-1]) != float(PEER_TOPK))
    tied = tied | (_count_ge(cands, tau) != float(PEER_TOPK))
    for b in range(PEER_TOPK - 1):
        tied = tied | (v1[b] == v1[b + 1]) | (v2[b] == v2[b + 1])

    cnt = []
    for a in range(PEER_TOPK):
        c = jnp.zeros(tau.shape, F32)
        for b in range(PEER_TOPK // (a + 1)):
            c = c + jnp.where(v1[a] + v2[b] >= tau, 1.0, 0.0)
        cnt.append(c)
    c1, r2 = [], []
    for x in p1:
        c = jnp.zeros(x.shape, F32)
        for a in range(PEER_TOPK):
            c = jnp.where(x == v1[a], cnt[a], c)
        c1.append(c)
    for x in p2:
        r = jnp.zeros(x.shape, F32)
        for b in range(PEER_TOPK):
            r = r + jnp.where(v2[b] > x, 1.0, 0.0)
        r2.append(r)
    inv_z = 1.0 / z
    e1 = [jnp.exp(x - v1[0]) * inv_z for x in p1]
    e2 = [jnp.exp(x - v2[0]) for x in p2]
    cat = lambda ps: jnp.concatenate(ps, axis=0)
    return (cat(c1), cat(e1), cat(r2), cat(e2)), tied


def _extract_sorted(scores, by_key):
    nk, lanes = scores[0].shape
    kio = lax.broadcasted_iota(jnp.int32, (nk, lanes), 0).astype(F32)
    slot = lax.broadcasted_iota(jnp.int32, (PEER_TOPK, lanes), 0)

    def body(a, carry):
        here = slot == a
        out = []
        for (v, vals, aux), ranked in zip(carry, by_key):
            m = jnp.max(v, axis=0, keepdims=True)
            idx = jnp.min(jnp.where(v == m, kio, float(nk)), axis=0, keepdims=True)
            hit = kio == idx
            aux = jnp.where(hit, jnp.asarray(a, F32), aux) if ranked else jnp.where(here, idx, aux)
            out.append((jnp.where(hit, -jnp.inf, v), jnp.where(here, m, vals), aux))
        return tuple(out)

    small = jnp.zeros((PEER_TOPK, lanes), F32)
    unranked = jnp.full((nk, lanes), float(PEER_TOPK), F32)
    init = tuple((v, small, unranked if ranked else small) for v, ranked in zip(scores, by_key))
    return [(vals, aux) for _, vals, aux in lax.fori_loop(0, PEER_TOPK, body, init)]


def _route_head_exact(s1, s2, pos, neg):
    lanes = s1.shape[1]
    slot = lax.broadcasted_iota(jnp.int32, (PEER_TOPK, lanes), 0)
    kio = lax.broadcasted_iota(jnp.int32, (PEER_N_KEYS, lanes), 0).astype(F32)
    (v1, idx1), (v2, rank2) = _extract_sorted([s1, s2], [False, True])
    blocks = [v1[0:1] + v2[0:8], v1[0:1] + v2[8:16]]
    blocks += [v1[a:a + 1] + v2[0:8] for a in range(1, 8)]
    blocks += [v1[8:16] + v2[0:1]]
    cand = jnp.concatenate(blocks, axis=0) + neg

    def pick(kk, carry):
        cand, chosen, ts = carry
        m = jnp.max(cand, axis=0, keepdims=True)
        first = jnp.min(jnp.where(cand == m, pos, 1e9), axis=0, keepdims=True)
        hit = pos == first
        return (jnp.where(hit, -jnp.inf, cand), jnp.where(hit, 1.0, chosen), jnp.where(slot == kk, m, ts))

    _, chosen, ts = lax.fori_loop(0, PEER_TOPK, pick,
                                  (cand, jnp.zeros_like(cand), jnp.zeros((PEER_TOPK, lanes), F32)))
    z = jnp.sum(jnp.exp(ts - ts[0:1]), axis=0, keepdims=True)
    counts = [jnp.sum(chosen[0:16], axis=0, keepdims=True)]
    counts += [jnp.sum(chosen[8 * a + 8:8 * a + 16], axis=0, keepdims=True) for a in range(1, 8)]
    counts += [chosen[72 + a:73 + a] for a in range(8)]
    c1 = jnp.zeros((PEER_N_KEYS, lanes), F32)
    for a in range(PEER_TOPK):
        c1 = jnp.where(kio == idx1[a:a + 1], counts[a], c1)
    return c1, jnp.exp(s1 - v1[0:1]) / z, rank2, jnp.exp(s2 - v2[0:1])


def _route_kernel(h_ref, wq_ref, keys_ref, pos_ref, neg_ref,
                  c1_ref, e1_ref, r2_ref, e2_ref, qt_ref, sc_ref):
    half = PEER_QUERY_DIM // 2
    qt_ref[...] = _dot_nt(wq_ref[...], h_ref[...]).astype(BF16)

    def store(h, maps):
        c1, e1, r2, e2 = maps
        c1_ref[h] = c1
        e1_ref[h] = e1
        r2_ref[h] = r2.astype(BF16)
        e2_ref[h] = e2.astype(BF16)

    def head_body(h, _):
        r0 = pl.multiple_of(h * PEER_QUERY_DIM, PEER_QUERY_DIM)
        sc_ref[0] = _dot(keys_ref[2 * h], qt_ref[pl.ds(r0, half), :])
        sc_ref[1] = _dot(keys_ref[2 * h + 1], qt_ref[pl.ds(r0 + half, half), :])
        maps, tied = _route_head_fast(sc_ref[0], sc_ref[1])
        any_tied = jnp.max(jnp.where(tied, 1.0, 0.0)) > 0.0

        @pl.when(any_tied)
        def _():
            store(h, _route_head_exact(sc_ref[0], sc_ref[1], pos_ref[...], neg_ref[...]))

        @pl.when(jnp.logical_not(any_tied))
        def _():
            store(h, maps)

        return 0

    lax.fori_loop(0, PEER_HEADS, head_body, 0)


def _peer_route(h2, wq_t, keys, tt=2 * LANES):
    T, D = h2.shape
    pos, neg = _cand_tables(tt)
    stat_spec = pl.BlockSpec((PEER_HEADS, PEER_N_KEYS, tt), lambda i: (0, 0, i))
    stat = lambda dt: jax.ShapeDtypeStruct((PEER_HEADS, PEER_N_KEYS, T), dt)
    return pl.pallas_call(
        _route_kernel,
        grid=(T // tt,),
        in_specs=[pl.BlockSpec((tt, D), lambda i: (i, 0)),
                  pl.BlockSpec(wq_t.shape, lambda i: (0, 0)),
                  pl.BlockSpec(keys.shape, lambda i: (0, 0, 0)),
                  pl.BlockSpec((_CAND_ROWS, tt), lambda i: (0, 0)),
                  pl.BlockSpec((_CAND_ROWS, tt), lambda i: (0, 0))],
        out_specs=[stat_spec] * 4,
        out_shape=[stat(F32), stat(F32), stat(BF16), stat(BF16)],
        scratch_shapes=[pltpu.VMEM((PEER_HEADS * PEER_QUERY_DIM, tt), BF16),
                        pltpu.VMEM((2, PEER_N_KEYS, tt), F32)],
        compiler_params=_cparams(("parallel",)),
        name="peer_route",
    )(h2, wq_t, keys, pos, neg)


_KEY_GROUP = 8
_UNITS = 2


def _build_gated(a_ref, p_ref, c1_ref, e1_ref, r2_ref, e2_ref, key0, g0, ng, lt):
    rep = PEER_N_KEYS // 16

    def rows16(row):
        blk = jnp.broadcast_to(row, (16, LANES)).astype(BF16)
        return jnp.concatenate([blk] * rep, axis=0)

    ls = slice(lt * LANES, (lt + 1) * LANES)
    w = [jnp.zeros((PEER_N_KEYS, LANES), BF16) for _ in range(ng)]
    for h in range(PEER_HEADS):
        c1 = c1_ref[h, pl.ds(key0, _KEY_GROUP), ls]
        e1 = e1_ref[h, pl.ds(key0, _KEY_GROUP), ls]
        r2 = r2_ref[h, :, ls]
        e2 = e2_ref[h, :, ls]
        for g in range(ng):
            thr = rows16(c1[g0 + g:g0 + g + 1])
            gate = rows16(e1[g0 + g:g0 + g + 1])
            w[g] = w[g] + jnp.where(r2 < thr, e2, jnp.zeros_like(e2)) * gate
    for g in range(g0, g0 + ng):
        rs = slice(g * PEER_N_KEYS, (g + 1) * PEER_N_KEYS)
        a = a_ref[rs, ls]
        gelu = 0.5 * a * (1.0 + lax.erf(a * (2.0 ** -0.5)))
        p_ref[rs, ls] = gelu.astype(BF16) * w[g - g0]


def _experts_kernel(h_ref, dn_ref, upt_ref, c1_ref, e1_ref, r2_ref, e2_ref, x_ref, *rest, emit_norm):
    if emit_norm:
        g_ref, o_ref, hn_ref, a_ref, p_ref, acc_ref = rest
    else:
        o_ref, a_ref, p_ref, acc_ref = rest
    te, tt = a_ref.shape
    d_model = acc_ref.shape[0]
    e = pl.program_id(1)
    n_tiles = pl.num_programs(1) - 1
    cur = e % 2
    ng = _KEY_GROUP // _UNITS
    n_lane = tt // LANES
    key0 = pl.multiple_of(jnp.minimum(e, n_tiles - 1) * _KEY_GROUP, _KEY_GROUP)

    def front_mm(u):
        rows = te // _UNITS
        rs = slice(u * rows, (u + 1) * rows)
        a_ref[rs, :] = _dot_nt(dn_ref[rs, :], h_ref[...])

    def back_mm(u):
        rows = d_model // _UNITS
        rs = slice(u * rows, (u + 1) * rows)
        acc_ref[rs, :] += _dot(upt_ref[rs, :], p_ref[1 - cur])

    def run(front, back):
        if front:
            front_mm(0)
        for u in range(_UNITS):
            for lt in range(n_lane):
                if front:
                    _build_gated(a_ref, p_ref.at[cur], c1_ref, e1_ref, r2_ref, e2_ref, key0, u * ng, ng, lt)
                if front and lt == 0 and u + 1 < _UNITS:
                    front_mm(u + 1)
                if back and lt == n_lane // 2:
                    back_mm(u)

    @pl.when(e == 0)
    def _():
        acc_ref[...] = jnp.zeros_like(acc_ref)
        run(True, False)

    @pl.when((e > 0) & (e < n_tiles))
    def _():
        run(True, True)

    @pl.when(e == n_tiles)
    def _():
        run(False, True)
        y = x_ref[...] + acc_ref[...].T
        o_ref[...] = y
        if emit_norm:
            hn_ref[...] = _rms_normed(y, g_ref[...])


def _peer_experts(h2, down, up_t, stats, x, next_gain, tt=512):
    T, D = h2.shape
    E = down.shape[0]
    te = _KEY_GROUP * PEER_N_KEYS
    n_tiles = E // te
    emit_norm = next_gain is not None
    stat_spec = pl.BlockSpec((PEER_HEADS, PEER_N_KEYS, tt), lambda i, e: (0, 0, i))
    row_spec = pl.BlockSpec((tt, D), lambda i, e: (i, 0))
    in_specs = [row_spec,
                pl.BlockSpec((te, D), lambda i, e: (jnp.minimum(e, n_tiles - 1), 0)),
                pl.BlockSpec((D, te), lambda i, e: (0, jnp.maximum(e - 1, 0))),
                stat_spec, stat_spec, stat_spec, stat_spec,
                row_spec]
    args = [h2, down, up_t, *stats, x]
    out_specs = [row_spec]
    out_shape = [jax.ShapeDtypeStruct((T, D), F32)]
    if emit_norm:
        in_specs.append(pl.BlockSpec((1, D), lambda i, e: (0, 0)))
        args.append(next_gain.reshape(1, D).astype(F32))
        out_specs.append(row_spec)
        out_shape.append(jax.ShapeDtypeStruct((T, D), BF16))
    res = pl.pallas_call(
        functools.partial(_experts_kernel, emit_norm=emit_norm),
        grid=(T // tt, n_tiles + 1),
        in_specs=in_specs, out_specs=out_specs, out_shape=out_shape,
        scratch_shapes=[pltpu.VMEM((te, tt), F32),
                        pltpu.VMEM((2, te, tt), BF16),
                        pltpu.VMEM((D, tt), F32)],
        compiler_params=_cparams(("parallel", "arbitrary")),
        name="peer_experts",
    )(*args)
    return (res[0], res[1]) if emit_norm else (res[0], None)


def _peer_layer(x, h2, next_gain, w_query, sub_keys, down, up):
    keys = sub_keys.reshape(PEER_HEADS * 2, PEER_N_KEYS, PEER_QUERY_DIM // 2).astype(BF16)
    stats = _peer_route(h2, w_query.T.astype(BF16), keys)
    return _peer_experts(h2, down.astype(BF16), up.T.astype(BF16), stats, x, next_gain)


def _tile_heads(g, n):
    return jnp.tile(g.astype(F32), n)


def _even_mixer(x, h, ffn_gain, w_in, f_bias, qn_a, kn_a, qn_b, kn_b, w_out, tabs, *, batch, seq):
    w = w_in.astype(BF16)
    o_qa, o_ka, o_va, o_qb, o_kb, o_vb, o_gb, o_fb = (
        0, A_W, 2 * A_W, 3 * A_W, 3 * A_W + B_W, 3 * A_W + 2 * B_W, 3 * A_W + 3 * B_W, 3 * A_W + 4 * B_W)
    gain_a = jnp.concatenate([_tile_heads(qn_a, MOBA_HEADS), _tile_heads(kn_a, MOBA_HEADS)])
    qk_a, km = _proj(h, w[:, o_qa:o_va], seq=seq, tn=A_W, tm=MOBA_BLOCK, gain=gain_a,
                     rope_tabs=tabs, kmean=True)
    nb = seq // MOBA_BLOCK
    kmean = km.reshape(batch, nb, 2 * A_W)[:, :, A_W:]
    kmean = jnp.pad(kmean, ((0, 0), (0, LANES - nb), (0, 0))).astype(BF16)
    gain_b = jnp.concatenate([_tile_heads(qn_b, FOX_HEADS), _tile_heads(kn_b, FOX_HEADS)])
    qk_b = _proj(h, w[:, o_qb:o_vb], seq=seq, tn=B_W, gain=gain_b)
    w_plain = jnp.concatenate([w[:, o_va:o_qb], w[:, o_vb:o_fb]], axis=1)
    vvg = _proj(h, w_plain, seq=seq, tn=A_W)
    va, vb, gb = vvg[:, :A_W], vvg[:, A_W:A_W + B_W], vvg[:, A_W + B_W:]
    frow = _fox_gates(h, w[:, o_fb:].T, f_bias, batch=batch, seq=seq)
    fcol = jnp.pad(frow.transpose(0, 2, 1), ((0, 0), (0, 0), (0, LANES - FOX_HEADS)))
    oa = _moba(qk_a[:, :A_W], qk_a[:, A_W:], va, kmean, batch=batch, seq=seq)
    ob = _fox(qk_b[:, :B_W], qk_b[:, B_W:], vb, gb, fcol, frow, batch=batch, seq=seq)
    return _outproj([oa, ob], w_out.astype(BF16), x, ffn_gain)


def _odd_mixer(x, h, ffn_gain, w_in, qn, kn, sinks, w_out, tabs, *, batch, seq):
    w = w_in.astype(BF16)
    qw = SWA_Q_HEADS * HEAD_DIM
    kw = SWA_KV_HEADS * HEAD_DIM
    q = _proj(h, w[:, :qw], seq=seq, tn=512, gain=_tile_heads(qn, SWA_Q_HEADS), rope_tabs=tabs)
    k = _proj(h, w[:, qw:qw + kw], seq=seq, tn=kw, gain=_tile_heads(kn, SWA_KV_HEADS), rope_tabs=tabs)
    v = _proj(h, w[:, qw + kw:], seq=seq, tn=kw)
    o = _swa(q, k, v, sinks, batch=batch, seq=seq)
    return _outproj([o], w_out.astype(BF16), x, ffn_gain)


def kernel(x, attn_norm, ffn_norm, ev_w_in, ev_forget_bias, ev_q_norm_a, ev_k_norm_a, ev_q_norm_b,
           ev_k_norm_b, ev_w_out, od_w_in, od_q_norm, od_k_norm, od_sinks, od_w_out,
           peer_w_query, peer_sub_keys, peer_down, peer_up):
    batch, seq, d_model = x.shape
    depth = attn_norm.shape[0]
    tabs = _rope_tables(seq)
    xt = x.reshape(batch * seq, d_model)
    h = _rmsnorm(xt, attn_norm[0])
    for l in range(depth):
        i = l // 2
        if l % 2 == 0:
            xt, h2 = _even_mixer(xt, h, ffn_norm[l], ev_w_in[i], ev_forget_bias[i], ev_q_norm_a[i],
                                 ev_k_norm_a[i], ev_q_norm_b[i], ev_k_norm_b[i], ev_w_out[i], tabs,
                                 batch=batch, seq=seq)
        else:
            xt, h2 = _odd_mixer(xt, h, ffn_norm[l], od_w_in[i], od_q_norm[i], od_k_norm[i], od_sinks[i],
                                od_w_out[i], tabs, batch=batch, seq=seq)
        next_gain = attn_norm[l + 1] if l + 1 < depth else None
        xt, h = _peer_layer(xt, h2, next_gain, peer_w_query[l], peer_sub_keys[l], peer_down[l], peer_up[l])
    return xt.reshape(batch, seq, d_model)
```

```python
import functools

import numpy as np
import jax
import jax.numpy as jnp
from jax import lax
from jax.experimental import pallas as pl
from jax.experimental.pallas import tpu as pltpu

F32 = jnp.float32
BF16 = jnp.bfloat16

HEAD_DIM = 64
ROT_DIM = HEAD_DIM // 4
ROPE_THETA = 500000.0
ATTN_SCALE = HEAD_DIM ** -0.5
EPS = 1e-6
NEG_INF = -1e30

MOBA_HEADS = 8
FOX_HEADS = 8
MOBA_BLOCK = 256
MOBA_TOPK = 3
A_W = MOBA_HEADS * HEAD_DIM
B_W = FOX_HEADS * HEAD_DIM

SWA_Q_HEADS = 16
SWA_KV_HEADS = 2
SWA_WINDOW = 128

PEER_HEADS = 8
PEER_N_KEYS = 128
PEER_TOPK = 16
PEER_QUERY_DIM = 128

LANES = 128
PAIR_W = 2 * HEAD_DIM
VMEM_LIMIT = 48 * 1024 * 1024


def _cparams(sem):
    return pltpu.CompilerParams(dimension_semantics=sem, vmem_limit_bytes=VMEM_LIMIT)


def _dot_nt(a, b):
    return lax.dot_general(a, b, (((1,), (1,)), ((), ())), preferred_element_type=F32)


def _dot(a, b):
    return jnp.dot(a, b, preferred_element_type=F32)


def _split3(x):
    h1 = x.astype(BF16)
    r1 = x - h1.astype(F32)
    h2 = r1.astype(BF16)
    h3 = (r1 - h2.astype(F32)).astype(BF16)
    return h1, h2, h3


def _rmsnorm_kernel(x_ref, g_ref, o_ref):
    x = x_ref[...]
    ms = jnp.mean(x * x, axis=-1, keepdims=True)
    o_ref[...] = (x * lax.rsqrt(ms + EPS) * g_ref[...]).astype(o_ref.dtype)


def _rmsnorm(x, gain, tm=512):
    T, D = x.shape
    return pl.pallas_call(
        _rmsnorm_kernel,
        grid=(T // tm,),
        in_specs=[pl.BlockSpec((tm, D), lambda i: (i, 0)),
                  pl.BlockSpec((1, D), lambda i: (0, 0))],
        out_specs=pl.BlockSpec((tm, D), lambda i: (i, 0)),
        out_shape=jax.ShapeDtypeStruct((T, D), BF16),
        compiler_params=_cparams(("parallel",)),
        name="rmsnorm",
    )(x, gain.reshape(1, D))


def _proj_kernel(*refs, norm, rope, kmean, tn):
    it = iter(refs)
    h_ref, w_ref = next(it), next(it)
    gain_ref = next(it) if norm else None
    bd_ref = next(it) if norm else None
    if rope:
        c_ref, sa_ref, sb_ref = next(it), next(it), next(it)
    o_ref = next(it)
    km_ref = next(it) if kmean else None

    y = _dot(h_ref[...], w_ref[...])
    if norm:
        y2 = y * y
        bd = bd_ref[...]
        cols = []
        for c in range(tn // LANES):
            h1, h2, h3 = _split3(y2[:, c * LANES:(c + 1) * LANES])
            cols.append(_dot(h1, bd) + _dot(h2, bd) + _dot(h3, bd))
        ms = cols[0] if len(cols) == 1 else jnp.concatenate(cols, axis=1)
        y = y * lax.rsqrt(ms + EPS) * gain_ref[...]
    if rope:
        rep = tn // LANES
        tile = (lambda t: t) if rep == 1 else (lambda t: jnp.concatenate([t] * rep, axis=1))
        y = (y * tile(c_ref[...])
             + pltpu.roll(y, tn - ROT_DIM // 2, 1) * tile(sa_ref[...])
             + pltpu.roll(y, ROT_DIM // 2, 1) * tile(sb_ref[...]))
    o_ref[...] = y.astype(o_ref.dtype)
    if kmean:
        km_ref[0] = jnp.mean(y, axis=0, keepdims=True)


def _proj(h, w, *, seq, tn, tm=512, gain=None, rope_tabs=None, kmean=False):
    T, D = h.shape
    N = w.shape[1]
    norm = gain is not None
    rope = rope_tabs is not None
    nseq = seq // tm
    in_specs = [pl.BlockSpec((tm, D), lambda i, j: (i, 0)),
                pl.BlockSpec((D, tn), lambda i, j: (0, j))]
    args = [h, w]
    if norm:
        bd = np.kron(np.eye(LANES // HEAD_DIM), np.ones((HEAD_DIM, HEAD_DIM))) / HEAD_DIM
        in_specs += [pl.BlockSpec((1, tn), lambda i, j: (0, j)),
                     pl.BlockSpec((LANES, LANES), lambda i, j: (0, 0))]
        args += [gain.reshape(1, N).astype(F32), jnp.asarray(bd, BF16)]
    if rope:
        in_specs += [pl.BlockSpec((tm, LANES), lambda i, j: (i % nseq, 0))] * 3
        args += list(rope_tabs)
    out_specs = [pl.BlockSpec((tm, tn), lambda i, j: (i, j))]
    out_shape = [jax.ShapeDtypeStruct((T, N), BF16)]
    if kmean:
        out_specs.append(pl.BlockSpec((1, 1, tn), lambda i, j: (i, 0, j)))
        out_shape.append(jax.ShapeDtypeStruct((T // tm, 1, N), F32))
    res = pl.pallas_call(
        functools.partial(_proj_kernel, norm=norm, rope=rope, kmean=kmean, tn=tn),
        grid=(T // tm, N // tn),
        in_specs=in_specs, out_specs=out_specs, out_shape=out_shape,
        compiler_params=_cparams(("parallel", "parallel")),
        name="proj",
    )(*args)
    return res if kmean else res[0]


def _rope_tables(seq):
    half = ROT_DIM // 2
    inv_freq = jnp.power(ROPE_THETA, -jnp.arange(0, ROT_DIM, 2, dtype=F32) / ROT_DIM)
    ang = jnp.arange(seq, dtype=F32)[:, None] * inv_freq[None, :]
    cos, sin = jnp.cos(ang), jnp.sin(ang)
    one = jnp.ones((seq, HEAD_DIM - ROT_DIM), F32)
    zero = jnp.zeros((seq, HEAD_DIM - ROT_DIM), F32)
    z8 = jnp.zeros((seq, half), F32)
    c = jnp.concatenate([cos, cos, one], axis=1)
    sa = jnp.concatenate([-sin, z8, zero], axis=1)
    sb = jnp.concatenate([z8, sin, zero], axis=1)
    rep = LANES // HEAD_DIM
    return tuple(jnp.concatenate([t] * rep, axis=1) for t in (c, sa, sb))


def _gates_kernel(h_ref, wf_ref, b_ref, tri_ref, o_ref, carry_ref):
    @pl.when(pl.program_id(1) == 0)
    def _():
        carry_ref[...] = jnp.zeros_like(carry_ref)

    z = _dot_nt(wf_ref[...], h_ref[...]) + b_ref[...][:, :1]
    lf = jnp.minimum(z, 0.0) - jnp.log1p(jnp.exp(-jnp.abs(z)))
    tri = tri_ref[...]
    h1, h2, h3 = _split3(lf)
    cs = _dot(h1, tri) + _dot(h2, tri) + _dot(h3, tri) + carry_ref[...][:, :1]
    o_ref[0] = cs
    carry_ref[...] = jnp.broadcast_to(cs[:, -1:], carry_ref.shape)


def _fox_gates(h, wf_t, bias, *, batch, seq, tm=512):
    T, D = h.shape
    nh = wf_t.shape[0]
    nseq = seq // tm
    tri = jnp.asarray(np.triu(np.ones((tm, tm))), BF16)
    return pl.pallas_call(
        _gates_kernel,
        grid=(batch, nseq),
        in_specs=[pl.BlockSpec((tm, D), lambda b, s: (b * nseq + s, 0)),
                  pl.BlockSpec((nh, D), lambda b, s: (0, 0)),
                  pl.BlockSpec((nh, LANES), lambda b, s: (0, 0)),
                  pl.BlockSpec((tm, tm), lambda b, s: (0, 0))],
        out_specs=pl.BlockSpec((1, nh, tm), lambda b, s: (b, 0, s)),
        out_shape=jax.ShapeDtypeStruct((batch, nh, seq), F32),
        scratch_shapes=[pltpu.VMEM((nh, LANES), F32)],
        compiler_params=_cparams(("parallel", "arbitrary")),
        name="fox_gates",
    )(h, wf_t, jnp.broadcast_to(bias.astype(F32)[:, None], (nh, LANES)), tri)


def _lane_tile(x, width):
    rep = width // LANES
    return x if rep == 1 else jnp.concatenate([x] * rep, axis=1)


def _flash_init(m_ref, acc_ref):
    m_ref[...] = jnp.full(m_ref.shape, NEG_INF, F32)
    acc_ref[...] = jnp.zeros(acc_ref.shape, F32)


def _head_values(v):
    lane = lax.broadcasted_iota(jnp.int32, v.shape, 1)
    return [jnp.where((lane >= hh * HEAD_DIM) & (lane < (hh + 1) * HEAD_DIM), v, jnp.ones_like(v))
            for hh in range(2)]


def _flash_update(slot, s, v, m_ref, acc_ref):
    tk = s.shape[1]
    m_prev = m_ref[slot]
    m_new = jnp.maximum(m_prev, jnp.max(s, axis=1, keepdims=True))
    alpha = jnp.exp(m_prev - m_new)
    p = jnp.exp(s - _lane_tile(m_new, tk))
    acc_ref[slot] = alpha * acc_ref[slot] + _dot(p.astype(BF16), v)
    m_ref[slot] = m_new


def _flash_finish(lane, acc_ref):
    outs = []
    for hh in range(2):
        acc = acc_ref[hh]
        den = (1 - hh) * HEAD_DIM
        outs.append(acc / acc[:, den:den + 1])
    return jnp.where(lane < HEAD_DIM, outs[0], outs[1])


def _flash_tiles(qi, tk, logits, values, m_ref, acc_ref):
    def absorb(s, off):
        vh = values(off)
        for hh in range(2):
            _flash_update(hh, s[hh], vh[hh], m_ref, acc_ref)

    def pair(off_a, off_b, b_diagonal):
        sa = logits(off_a, False)
        sb = logits(off_b, b_diagonal)
        absorb(sa, off_a)
        absorb(sb, off_b)

    def body(jj, carry):
        off = pl.multiple_of(2 * jj * tk, 2 * tk)
        pair(off, off + tk, False)
        return carry

    lax.fori_loop(0, qi // 2, body, 0)
    diag = pl.multiple_of(qi * tk, tk)

    @pl.when(qi % 2 == 1)
    def _():
        pair(diag - tk, diag, True)

    @pl.when(qi % 2 == 0)
    def _():
        absorb(logits(diag, True), diag)


def _head_queries(q, lane):
    qs = q * ATTN_SCALE
    return [jnp.where((lane >= hh * HEAD_DIM) & (lane < (hh + 1) * HEAD_DIM), qs, jnp.zeros_like(qs))
            for hh in range(2)]


def _moba_kernel(q_ref, k_ref, v_ref, km_ref, o_ref, m_ref, acc_ref):
    tq = q_ref.shape[1]
    tk = tq
    qi = pl.program_id(2)
    lane = lax.broadcasted_iota(jnp.int32, (tq, LANES), 1)
    lane_f = lane.astype(F32)
    rowv = lax.broadcasted_iota(jnp.int32, (tq, LANES), 0)
    row_blk = 2 * qi + (rowv >= MOBA_BLOCK).astype(jnp.int32)
    row = lax.broadcasted_iota(jnp.int32, (tq, tk), 0)
    col = lax.broadcasted_iota(jnp.int32, (tq, tk), 1)
    qh = _head_queries(q_ref[0], lane)
    _flash_init(m_ref, acc_ref)

    sels = []
    for hh in range(2):
        gate = _dot_nt(qh[hh], km_ref[0])
        gate = jnp.where(lane < row_blk, gate, -jnp.inf)
        sel = jnp.zeros((tq, LANES), F32)
        for _ in range(MOBA_TOPK):
            m = jnp.max(gate, axis=1, keepdims=True)
            idx = jnp.min(jnp.where(gate == m, lane_f, float(LANES)), axis=1, keepdims=True)
            hit = lane_f == idx
            sel = jnp.where(hit & (m > -jnp.inf), 1.0, sel)
            gate = jnp.where(hit, -jnp.inf, gate)
        sels.append(sel)

    def chosen(sel, blk):
        return jnp.max(jnp.where(lane == blk, sel, 0.0), axis=1, keepdims=True) > 0.0

    def logits(off, diagonal):
        kj = k_ref[0, pl.ds(off, tk), :]
        out = []
        for hh in range(2):
            s = _dot_nt(qh[hh], kj)
            if diagonal:
                visible = (col >= MOBA_BLOCK) | (row < MOBA_BLOCK) | chosen(sels[hh], 2 * qi)
                s = jnp.where((col <= row) & visible, s, NEG_INF)
            else:
                blk = 2 * (off // tk)
                s = jnp.concatenate(
                    [jnp.where(chosen(sels[hh], blk), s[:, :MOBA_BLOCK], NEG_INF),
                     jnp.where(chosen(sels[hh], blk + 1), s[:, MOBA_BLOCK:], NEG_INF)], axis=1)
            out.append(s)
        return out

    def values(off):
        return _head_values(v_ref[0, pl.ds(off, tk), :])

    _flash_tiles(qi, tk, logits, values, m_ref, acc_ref)
    o_ref[0] = _flash_finish(lane, acc_ref).astype(o_ref.dtype)


def _flash_scratch(tq):
    return [pltpu.VMEM((2, tq, LANES), F32)] * 2


def _moba(q, k, v, kmean, *, batch, seq):
    W = q.shape[1]
    tq = 2 * MOBA_BLOCK
    q3, k3, v3 = (t.reshape(batch, seq, W) for t in (q, k, v))
    out = pl.pallas_call(
        _moba_kernel,
        grid=(batch, W // PAIR_W, seq // tq),
        in_specs=[pl.BlockSpec((1, tq, PAIR_W), lambda b, p, i: (b, i, p)),
                  pl.BlockSpec((1, seq, PAIR_W), lambda b, p, i: (b, 0, p)),
                  pl.BlockSpec((1, seq, PAIR_W), lambda b, p, i: (b, 0, p)),
                  pl.BlockSpec((1, LANES, PAIR_W), lambda b, p, i: (b, 0, p))],
        out_specs=pl.BlockSpec((1, tq, PAIR_W), lambda b, p, i: (b, i, p)),
        out_shape=jax.ShapeDtypeStruct((batch, seq, W), BF16),
        scratch_shapes=_flash_scratch(tq),
        compiler_params=_cparams(("parallel", "parallel", "parallel")),
        name="moba",
    )(q3, k3, v3, kmean)
    return out.reshape(batch * seq, W)


def _fox_kernel(q_ref, k_ref, v_ref, g_ref, frow_ref, o_ref, m_ref, acc_ref):
    tq = q_ref.shape[1]
    tk = tq
    pr = pl.program_id(1)
    qi = pl.program_id(2)
    lane = lax.broadcasted_iota(jnp.int32, (tq, LANES), 1)
    row = lax.broadcasted_iota(jnp.int32, (tq, tk), 0)
    col = lax.broadcasted_iota(jnp.int32, (tq, tk), 1)
    qh = _head_queries(q_ref[0], lane)
    _flash_init(m_ref, acc_ref)

    def key_gates(off, width):
        f_all = frow_ref[0, :, pl.ds(off, width)]
        sub = lax.broadcasted_iota(jnp.int32, f_all.shape, 0)
        return [jnp.sum(jnp.where(sub == 2 * pr + hh, f_all, 0.0), axis=0, keepdims=True) for hh in range(2)]

    f_ref = [f[:, :1] for f in key_gates(pl.multiple_of(qi * tq, tq), LANES)]

    def logits(off, diagonal):
        kj = k_ref[0, pl.ds(off, tk), :]
        fk = key_gates(off, tk)
        out = []
        for hh in range(2):
            s = _dot_nt(qh[hh], kj) - (fk[hh] - f_ref[hh])
            out.append(jnp.where(col <= row, s, NEG_INF) if diagonal else s)
        return out

    def values(off):
        return _head_values(v_ref[0, pl.ds(off, tk), :])

    _flash_tiles(qi, tk, logits, values, m_ref, acc_ref)
    o = _flash_finish(lane, acc_ref)
    o_ref[0] = (o * jax.nn.sigmoid(g_ref[0].astype(F32))).astype(o_ref.dtype)


def _fox(q, k, v, g, frow, *, batch, seq, tq=512):
    W = q.shape[1]
    nh = frow.shape[1]
    q3, k3, v3, g3 = (t.reshape(batch, seq, W) for t in (q, k, v, g))
    out = pl.pallas_call(
        _fox_kernel,
        grid=(batch, W // PAIR_W, seq // tq),
        in_specs=[pl.BlockSpec((1, tq, PAIR_W), lambda b, p, i: (b, i, p)),
                  pl.BlockSpec((1, seq, PAIR_W), lambda b, p, i: (b, 0, p)),
                  pl.BlockSpec((1, seq, PAIR_W), lambda b, p, i: (b, 0, p)),
                  pl.BlockSpec((1, tq, PAIR_W), lambda b, p, i: (b, i, p)),
                  pl.BlockSpec((1, nh, seq), lambda b, p, i: (b, 0, 0))],
        out_specs=pl.BlockSpec((1, tq, PAIR_W), lambda b, p, i: (b, i, p)),
        out_shape=jax.ShapeDtypeStruct((batch, seq, W), BF16),
        scratch_shapes=_flash_scratch(tq),
        compiler_params=_cparams(("parallel", "parallel", "parallel")),
        name="fox",
    )(q3, k3, v3, g3, frow)
    return out.reshape(batch * seq, W)


def _swa_kernel(q_ref, k_ref, v_ref, sink_ref, bias_ref, o_ref):
    tq = q_ref.shape[1]
    qi = pl.program_id(1)
    group = SWA_Q_HEADS // SWA_KV_HEADS
    tk = tq + SWA_WINDOW
    lane = lax.broadcasted_iota(jnp.int32, (tq, LANES), 1)
    kstart = pl.multiple_of(jnp.maximum(qi * tq - SWA_WINDOW, 0), SWA_WINDOW)
    k = k_ref[0, pl.ds(kstart, tk), :]
    vh = _head_values(v_ref[0, pl.ds(kstart, tk), :])
    bias = bias_ref[jnp.minimum(qi, 1)]
    bias = jnp.concatenate([bias] * group, axis=0)
    sink_tab = sink_ref[...]
    outs = [None] * SWA_Q_HEADS
    for c in range(SWA_KV_HEADS):
        pieces, sinks = [], []
        for g in range(group):
            head = c * group + g
            blk = q_ref[0, :, (head // 2) * PAIR_W:(head // 2 + 1) * PAIR_W] * ATTN_SCALE
            hh = head % 2
            qm = jnp.where((lane >= hh * HEAD_DIM) & (lane < (hh + 1) * HEAD_DIM), blk, jnp.zeros_like(blk))
            if hh != c:
                qm = pltpu.roll(qm.astype(F32), HEAD_DIM, 1).astype(BF16)
            pieces.append(qm)
            sinks.append(jnp.broadcast_to(sink_tab[head:head + 1, :], (tq, LANES)))
        qs = jnp.concatenate(pieces, axis=0)
        sink = jnp.concatenate(sinks, axis=0)
        s = _dot_nt(qs, k) + bias
        m = jnp.maximum(sink, jnp.max(s, axis=1, keepdims=True))
        p = jnp.exp(s - _lane_tile(m, tk))
        acc = _dot(p.astype(BF16), vh[c])
        den = pltpu.roll(acc, HEAD_DIM, 1) + jnp.exp(sink - m)
        o = acc / den
        for g in range(group):
            head = c * group + g
            oh = o[g * tq:(g + 1) * tq]
            outs[head] = oh if head % 2 == c else pltpu.roll(oh, HEAD_DIM, 1)
    for pp in range(SWA_Q_HEADS // 2):
        o_ref[0, :, pp * PAIR_W:(pp + 1) * PAIR_W] = jnp.where(
            lane < HEAD_DIM, outs[2 * pp], outs[2 * pp + 1]).astype(o_ref.dtype)


def _swa_bias(tq):
    r = np.arange(tq)[:, None]
    c = np.arange(tq + SWA_WINDOW)[None, :]
    tabs = []
    for key_offset in (0, SWA_WINDOW):
        dist = r + key_offset - c
        tabs.append(np.where((dist >= 0) & (dist < SWA_WINDOW), 0.0, NEG_INF))
    return jnp.asarray(np.stack(tabs), F32)


def _swa(q, k, v, sinks, *, batch, seq, tq=SWA_WINDOW):
    W = q.shape[1]
    tk = tq + SWA_WINDOW
    q3 = q.reshape(batch, seq, W)
    k3, v3 = (t.reshape(batch, seq, PAIR_W) for t in (k, v))
    sink_tab = jnp.broadcast_to(sinks.astype(F32)[:, None], (SWA_Q_HEADS, LANES))
    out = pl.pallas_call(
        _swa_kernel,
        grid=(batch, seq // tq),
        in_specs=[pl.BlockSpec((1, tq, W), lambda b, i: (b, i, 0)),
                  pl.BlockSpec((1, seq, PAIR_W), lambda b, i: (b, 0, 0)),
                  pl.BlockSpec((1, seq, PAIR_W), lambda b, i: (b, 0, 0)),
                  pl.BlockSpec((SWA_Q_HEADS, LANES), lambda b, i: (0, 0)),
                  pl.BlockSpec((2, tq, tk), lambda b, i: (0, 0, 0))],
        out_specs=pl.BlockSpec((1, tq, W), lambda b, i: (b, i, 0)),
        out_shape=jax.ShapeDtypeStruct((batch, seq, W), BF16),
        compiler_params=_cparams(("parallel", "parallel")),
        name="swa",
    )(q3, k3, v3, sink_tab, _swa_bias(tq))
    return out.reshape(batch * seq, W)


def _rms_normed(x, gain):
    ms = jnp.mean(x * x, axis=-1, keepdims=True)
    return (x * lax.rsqrt(ms + EPS) * gain).astype(BF16)


def _outproj_kernel(*refs, n_parts):
    parts = refs[:n_parts]
    w_ref, x_ref, g_ref, o_ref, h_ref = refs[n_parts:]
    y = x_ref[...]
    off = 0
    for p_ref in parts:
        kw = p_ref.shape[1]
        y = y + _dot(p_ref[...], w_ref[off:off + kw, :])
        off += kw
    o_ref[...] = y
    h_ref[...] = _rms_normed(y, g_ref[...])


def _outproj(parts, w, x, next_gain, tm=512):
    T, D = x.shape
    in_specs = [pl.BlockSpec((tm, p.shape[1]), lambda i: (i, 0)) for p in parts]
    in_specs += [pl.BlockSpec(w.shape, lambda i: (0, 0)),
                 pl.BlockSpec((tm, D), lambda i: (i, 0)),
                 pl.BlockSpec((1, D), lambda i: (0, 0))]
    row_spec = pl.BlockSpec((tm, D), lambda i: (i, 0))
    return pl.pallas_call(
        functools.partial(_outproj_kernel, n_parts=len(parts)),
        grid=(T // tm,),
        in_specs=in_specs,
        out_specs=[row_spec, row_spec],
        out_shape=[jax.ShapeDtypeStruct((T, D), F32), jax.ShapeDtypeStruct((T, D), BF16)],
        compiler_params=_cparams(("parallel",)),
        name="outproj",
    )(*parts, w, x, next_gain.reshape(1, D).astype(F32))


_CAND_ROWS = 80


def _cand_tables(lanes):
    pos = np.zeros((_CAND_ROWS,), np.float32)
    neg = np.zeros((_CAND_ROWS,), np.float32)
    r = 0
    for a, nb in ((0, 16), (1, 8), (2, 8), (3, 8), (4, 8), (5, 8), (6, 8), (7, 8)):
        for b in range(nb):
            pos[r] = a * PEER_TOPK + b
            neg[r] = 0.0 if (a + 1) * (b + 1) <= PEER_TOPK else -np.inf
            r += 1
    for a in range(8, 16):
        pos[r] = a * PEER_TOPK
        r += 1
    assert r == _CAND_ROWS
    tab = lambda t: jnp.asarray(np.broadcast_to(t[:, None], (_CAND_ROWS, lanes)).copy())
    return tab(pos), tab(neg)


def _batcher_pairs(n):
    pairs, p = [], 1
    while p < n:
        k = p
        while k >= 1:
            for j in range(k % p, n - k, 2 * k):
                for i in range(min(k, n - j - k)):
                    if (i + j) // (2 * p) == (i + j + k) // (2 * p):
                        pairs.append((i + j, i + j + k))
            k //= 2
        p *= 2
    return pairs


_SORT16 = _batcher_pairs(PEER_TOPK)
_N_CAND_PIECES = 10
_SORT10 = [(i, j) for i, j in _SORT16 if j < _N_CAND_PIECES]
_SUBLANES = 8


def _compare_exchange(items, i, j):
    items[i], items[j] = jnp.maximum(items[i], items[j]), jnp.minimum(items[i], items[j])


def _top_sorted(pieces, pairs):
    items = list(pieces)
    for i, j in pairs:
        _compare_exchange(items, i, j)
    n = PEER_TOPK
    items += [jnp.full(items[0].shape, -jnp.inf, F32)] * (n - len(items))
    for shift in (4, 2, 1):
        items = [jnp.maximum(items[i], pltpu.roll(items[n - 1 - i], shift, 0)) for i in range(n)]
        d = n // 2
        while d >= 1:
            for i in range(n):
                if i & d == 0:
                    _compare_exchange(items, i, i + d)
            d //= 2
    return items


def _pieces(x):
    return [x[_SUBLANES * g:_SUBLANES * (g + 1)] for g in range(x.shape[0] // _SUBLANES)]


def _sublane_total(x):
    for shift in (4, 2, 1):
        x = x + pltpu.roll(x, shift, 0)
    return x


def _count_ge(pieces, thr):
    total = jnp.zeros(thr.shape, F32)
    for p in pieces:
        total = total + jnp.where(p >= thr, 1.0, 0.0)
    return _sublane_total(total)


def _route_head_fast(s1, s2):
    p1, p2 = _pieces(s1), _pieces(s2)
    v1 = _top_sorted(p1, _SORT16)
    v2 = _top_sorted(p2, _SORT16)
    sub = lax.broadcasted_iota(jnp.int32, v1[0].shape, 0)

    def spread(vals):
        out = vals[0]
        for r in range(1, _SUBLANES):
            out = jnp.where(sub == r, vals[r], out)
        return out

    v2_lo, v2_hi, v1_hi = spread(v2[:8]), spread(v2[8:]), spread(v1[8:])
    cands = [v1[0] + v2_lo, v1[0] + v2_hi, v1[1] + v2_lo]
    for a in range(2, 8):
        cands.append(jnp.where(sub < PEER_TOPK // (a + 1), v1[a] + v2_lo, -jnp.inf))
    cands.append(v1_hi + v2[0])
    ts = _top_sorted(cands, _SORT10)
    tau = ts[PEER_TOPK - 1]
    z = jnp.exp(ts[0] - ts[0])
    for kk in range(1, PEER_TOPK):
        z = z + jnp.exp(ts[kk] - ts[0])

    tied = (_count_ge(p1, v1[-1]) != float(PEER_TOPK)) | (_count_ge(p2, v2[-1]) != float(PEER_TOPK))
    tied = tied | (_count_ge(cands, tau) != float(PEER_TOPK))
    for b in range(PEER_TOPK - 1):
        tied = tied | (v1[b] == v1[b + 1]) | (v2[b] == v2[b + 1])

    cnt = []
    for a in range(PEER_TOPK):
        c = jnp.zeros(tau.shape, F32)
        for b in range(PEER_TOPK // (a + 1)):
            c = c + jnp.where(v1[a] + v2[b] >= tau, 1.0, 0.0)
        cnt.append(c)
    c1, r2 = [], []
    for x in p1:
        c = jnp.zeros(x.shape, F32)
        for a in range(PEER_TOPK):
            c = jnp.where(x == v1[a], cnt[a], c)
        c1.append(c)
    for x in p2:
        r = jnp.zeros(x.shape, F32)
        for b in range(PEER_TOPK):
            r = r + jnp.where(v2[b] > x, 1.0, 0.0)
        r2.append(r)
    inv_z = 1.0 / z
    e1 = [jnp.exp(x - v1[0]) * inv_z for x in p1]
    e2 = [jnp.exp(x - v2[0]) for x in p2]
    cat = lambda ps: jnp.concatenate(ps, axis=0)
    return (cat(c1), cat(e1), cat(r2), cat(e2)), tied


def _extract_sorted(scores, by_key):
    nk, lanes = scores[0].shape
    kio = lax.broadcasted_iota(jnp.int32, (nk, lanes), 0).astype(F32)
    slot = lax.broadcasted_iota(jnp.int32, (PEER_TOPK, lanes), 0)

    def body(a, carry):
        here = slot == a
        out = []
        for (v, vals, aux), ranked in zip(carry, by_key):
            m = jnp.max(v, axis=0, keepdims=True)
            idx = jnp.min(jnp.where(v == m, kio, float(nk)), axis=0, keepdims=True)
            hit = kio == idx
            aux = jnp.where(hit, jnp.asarray(a, F32), aux) if ranked else jnp.where(here, idx, aux)
            out.append((jnp.where(hit, -jnp.inf, v), jnp.where(here, m, vals), aux))
        return tuple(out)

    small = jnp.zeros((PEER_TOPK, lanes), F32)
    unranked = jnp.full((nk, lanes), float(PEER_TOPK), F32)
    init = tuple((v, small, unranked if ranked else small) for v, ranked in zip(scores, by_key))
    return [(vals, aux) for _, vals, aux in lax.fori_loop(0, PEER_TOPK, body, init)]


def _route_head_exact(s1, s2, pos, neg):
    lanes = s1.shape[1]
    slot = lax.broadcasted_iota(jnp.int32, (PEER_TOPK, lanes), 0)
    kio = lax.broadcasted_iota(jnp.int32, (PEER_N_KEYS, lanes), 0).astype(F32)
    (v1, idx1), (v2, rank2) = _extract_sorted([s1, s2], [False, True])
    blocks = [v1[0:1] + v2[0:8], v1[0:1] + v2[8:16]]
    blocks += [v1[a:a + 1] + v2[0:8] for a in range(1, 8)]
    blocks += [v1[8:16] + v2[0:1]]
    cand = jnp.concatenate(blocks, axis=0) + neg

    def pick(kk, carry):
        cand, chosen, ts = carry
        m = jnp.max(cand, axis=0, keepdims=True)
        first = jnp.min(jnp.where(cand == m, pos, 1e9), axis=0, keepdims=True)
        hit = pos == first
        return (jnp.where(hit, -jnp.inf, cand), jnp.where(hit, 1.0, chosen), jnp.where(slot == kk, m, ts))

    _, chosen, ts = lax.fori_loop(0, PEER_TOPK, pick,
                                  (cand, jnp.zeros_like(cand), jnp.zeros((PEER_TOPK, lanes), F32)))
    z = jnp.sum(jnp.exp(ts - ts[0:1]), axis=0, keepdims=True)
    counts = [jnp.sum(chosen[0:16], axis=0, keepdims=True)]
    counts += [jnp.sum(chosen[8 * a + 8:8 * a + 16], axis=0, keepdims=True) for a in range(1, 8)]
    counts += [chosen[72 + a:73 + a] for a in range(8)]
    c1 = jnp.zeros((PEER_N_KEYS, lanes), F32)
    for a in range(PEER_TOPK):
        c1 = jnp.where(kio == idx1[a:a + 1], counts[a], c1)
    return c1, jnp.exp(s1 - v1[0:1]) / z, rank2, jnp.exp(s2 - v2[0:1])


def _route_kernel(h_ref, wq_ref, keys_ref, pos_ref, neg_ref,
                  c1_ref, e1_ref, r2_ref, e2_ref, qt_ref, sc_ref):
    half = PEER_QUERY_DIM // 2
    qt_ref[...] = _dot_nt(wq_ref[...], h_ref[...]).astype(BF16)

    def store(h, maps):
        c1, e1, r2, e2 = maps
        c1_ref[h] = c1
        e1_ref[h] = e1
        r2_ref[h] = r2.astype(BF16)
        e2_ref[h] = e2.astype(BF16)

    def head_body(h, _):
        r0 = pl.multiple_of(h * PEER_QUERY_DIM, PEER_QUERY_DIM)
        sc_ref[0] = _dot(keys_ref[2 * h], qt_ref[pl.ds(r0, half), :])
        sc_ref[1] = _dot(keys_ref[2 * h + 1], qt_ref[pl.ds(r0 + half, half), :])
        maps, tied = _route_head_fast(sc_ref[0], sc_ref[1])
        any_tied = jnp.max(jnp.where(tied, 1.0, 0.0)) > 0.0

        @pl.when(any_tied)
        def _():
            store(h, _route_head_exact(sc_ref[0], sc_ref[1], pos_ref[...], neg_ref[...]))

        @pl.when(jnp.logical_not(any_tied))
        def _():
            store(h, maps)

        return 0

    lax.fori_loop(0, PEER_HEADS, head_body, 0)


def _peer_route(h2, wq_t, keys, tt=2 * LANES):
    T, D = h2.shape
    pos, neg = _cand_tables(tt)
    stat_spec = pl.BlockSpec((PEER_HEADS, PEER_N_KEYS, tt), lambda i: (0, 0, i))
    stat = lambda dt: jax.ShapeDtypeStruct((PEER_HEADS, PEER_N_KEYS, T), dt)
    return pl.pallas_call(
        _route_kernel,
        grid=(T // tt,),
        in_specs=[pl.BlockSpec((tt, D), lambda i: (i, 0)),
                  pl.BlockSpec(wq_t.shape, lambda i: (0, 0)),
                  pl.BlockSpec(keys.shape, lambda i: (0, 0, 0)),
                  pl.BlockSpec((_CAND_ROWS, tt), lambda i: (0, 0)),
                  pl.BlockSpec((_CAND_ROWS, tt), lambda i: (0, 0))],
        out_specs=[stat_spec] * 4,
        out_shape=[stat(F32), stat(F32), stat(BF16), stat(BF16)],
        scratch_shapes=[pltpu.VMEM((PEER_HEADS * PEER_QUERY_DIM, tt), BF16),
                        pltpu.VMEM((2, PEER_N_KEYS, tt), F32)],
        compiler_params=_cparams(("parallel",)),
        name="peer_route",
    )(h2, wq_t, keys, pos, neg)


_KEY_GROUP = 8
_UNITS = 2


def _build_gated(a_ref, p_ref, c1_ref, e1_ref, r2_ref, e2_ref, key0, g0, ng, lt):
    rep = PEER_N_KEYS // 16

    def rows16(row):
        blk = jnp.broadcast_to(row, (16, LANES)).astype(BF16)
        return jnp.concatenate([blk] * rep, axis=0)

    ls = slice(lt * LANES, (lt + 1) * LANES)
    w = [jnp.zeros((PEER_N_KEYS, LANES), BF16) for _ in range(ng)]
    for h in range(PEER_HEADS):
        c1 = c1_ref[h, pl.ds(key0, _KEY_GROUP), ls]
        e1 = e1_ref[h, pl.ds(key0, _KEY_GROUP), ls]
        r2 = r2_ref[h, :, ls]
        e2 = e2_ref[h, :, ls]
        for g in range(ng):
            thr = rows16(c1[g0 + g:g0 + g + 1])
            gate = rows16(e1[g0 + g:g0 + g + 1])
            w[g] = w[g] + jnp.where(r2 < thr, e2, jnp.zeros_like(e2)) * gate
    for g in range(g0, g0 + ng):
        rs = slice(g * PEER_N_KEYS, (g + 1) * PEER_N_KEYS)
        a = a_ref[rs, ls]
        gelu = 0.5 * a * (1.0 + lax.erf(a * (2.0 ** -0.5)))
        p_ref[rs, ls] = gelu.astype(BF16) * w[g - g0]


def _experts_kernel(h_ref, dn_ref, upt_ref, c1_ref, e1_ref, r2_ref, e2_ref, x_ref, *rest, emit_norm):
    if emit_norm:
        g_ref, o_ref, hn_ref, a_ref, p_ref, acc_ref = rest
    else:
        o_ref, a_ref, p_ref, acc_ref = rest
    te, tt = a_ref.shape
    d_model = acc_ref.shape[0]
    e = pl.program_id(1)
    n_tiles = pl.num_programs(1) - 1
    cur = e % 2
    ng = _KEY_GROUP // _UNITS
    n_lane = tt // LANES
    key0 = pl.multiple_of(jnp.minimum(e, n_tiles - 1) * _KEY_GROUP, _KEY_GROUP)

    def front_mm(u):
        rows = te // _UNITS
        rs = slice(u * rows, (u + 1) * rows)
        a_ref[rs, :] = _dot_nt(dn_ref[rs, :], h_ref[...])

    def back_mm(u):
        rows = d_model // _UNITS
        rs = slice(u * rows, (u + 1) * rows)
        acc_ref[rs, :] += _dot(upt_ref[rs, :], p_ref[1 - cur])

    def run(front, back):
        if front:
            front_mm(0)
        for u in range(_UNITS):
            for lt in range(n_lane):
                if front:
                    _build_gated(a_ref, p_ref.at[cur], c1_ref, e1_ref, r2_ref, e2_ref, key0, u * ng, ng, lt)
                if front and lt == 0 and u + 1 < _UNITS:
                    front_mm(u + 1)
                if back and lt == n_lane // 2:
                    back_mm(u)

    @pl.when(e == 0)
    def _():
        acc_ref[...] = jnp.zeros_like(acc_ref)
        run(True, False)

    @pl.when((e > 0) & (e < n_tiles))
    def _():
        run(True, True)

    @pl.when(e == n_tiles)
    def _():
        run(False, True)
        y = x_ref[...] + acc_ref[...].T
        o_ref[...] = y
        if emit_norm:
            hn_ref[...] = _rms_normed(y, g_ref[...])


def _peer_experts(h2, down, up_t, stats, x, next_gain, tt=512):
    T, D = h2.shape
    E = down.shape[0]
    te = _KEY_GROUP * PEER_N_KEYS
    n_tiles = E // te
    emit_norm = next_gain is not None
    stat_spec = pl.BlockSpec((PEER_HEADS, PEER_N_KEYS, tt), lambda i, e: (0, 0, i))
    row_spec = pl.BlockSpec((tt, D), lambda i, e: (i, 0))
    in_specs = [row_spec,
                pl.BlockSpec((te, D), lambda i, e: (jnp.minimum(e, n_tiles - 1), 0)),
                pl.BlockSpec((None, D, te), lambda i, e: (jnp.maximum(e - 1, 0), 0, 0)),
                stat_spec, stat_spec, stat_spec, stat_spec,
                row_spec]
    args = [h2, down, up_t, *stats, x]
    out_specs = [row_spec]
    out_shape = [jax.ShapeDtypeStruct((T, D), F32)]
    if emit_norm:
        in_specs.append(pl.BlockSpec((1, D), lambda i, e: (0, 0)))
        args.append(next_gain.reshape(1, D).astype(F32))
        out_specs.append(row_spec)
        out_shape.append(jax.ShapeDtypeStruct((T, D), BF16))
    res = pl.pallas_call(
        functools.partial(_experts_kernel, emit_norm=emit_norm),
        grid=(T // tt, n_tiles + 1),
        in_specs=in_specs, out_specs=out_specs, out_shape=out_shape,
        scratch_shapes=[pltpu.VMEM((te, tt), F32),
                        pltpu.VMEM((2, te, tt), BF16),
                        pltpu.VMEM((D, tt), F32)],
        compiler_params=_cparams(("parallel", "arbitrary")),
        name="peer_experts",
    )(*args)
    return (res[0], res[1]) if emit_norm else (res[0], None)


def _peer_layer(x, h2, next_gain, w_query, sub_keys, down, up):
    keys = sub_keys.reshape(PEER_HEADS * 2, PEER_N_KEYS, PEER_QUERY_DIM // 2).astype(BF16)
    stats = _peer_route(h2, w_query.T.astype(BF16), keys)
    te = _KEY_GROUP * PEER_N_KEYS
    up_t = up.reshape(up.shape[0] // te, te, up.shape[1]).transpose(0, 2, 1).astype(BF16)
    return _peer_experts(h2, down.astype(BF16), up_t, stats, x, next_gain)


def _tile_heads(g, n):
    return jnp.tile(g.astype(F32), n)


def _even_mixer(x, h, ffn_gain, w_in, f_bias, qn_a, kn_a, qn_b, kn_b, w_out, tabs, *, batch, seq):
    w = w_in.astype(BF16)
    o_qa, o_ka, o_va, o_qb, o_kb, o_vb, o_gb, o_fb = (
        0, A_W, 2 * A_W, 3 * A_W, 3 * A_W + B_W, 3 * A_W + 2 * B_W, 3 * A_W + 3 * B_W, 3 * A_W + 4 * B_W)
    gain_a = jnp.concatenate([_tile_heads(qn_a, MOBA_HEADS), _tile_heads(kn_a, MOBA_HEADS)])
    qk_a, km = _proj(h, w[:, o_qa:o_va], seq=seq, tn=A_W, tm=MOBA_BLOCK, gain=gain_a,
                     rope_tabs=tabs, kmean=True)
    nb = seq // MOBA_BLOCK
    kmean = km.reshape(batch, nb, 2 * A_W)[:, :, A_W:]
    kmean = jnp.pad(kmean, ((0, 0), (0, LANES - nb), (0, 0))).astype(BF16)
    gain_b = jnp.concatenate([_tile_heads(qn_b, FOX_HEADS), _tile_heads(kn_b, FOX_HEADS)])
    qk_b = _proj(h, w[:, o_qb:o_vb], seq=seq, tn=B_W, gain=gain_b)
    w_plain = jnp.concatenate([w[:, o_va:o_qb], w[:, o_vb:o_fb]], axis=1)
    vvg = _proj(h, w_plain, seq=seq, tn=A_W)
    va, vb, gb = vvg[:, :A_W], vvg[:, A_W:A_W + B_W], vvg[:, A_W + B_W:]
    frow = _fox_gates(h, w[:, o_fb:].T, f_bias, batch=batch, seq=seq)
    oa = _moba(qk_a[:, :A_W], qk_a[:, A_W:], va, kmean, batch=batch, seq=seq)
    ob = _fox(qk_b[:, :B_W], qk_b[:, B_W:], vb, gb, frow, batch=batch, seq=seq)
    return _outproj([oa, ob], w_out.astype(BF16), x, ffn_gain)


def _odd_mixer(x, h, ffn_gain, w_in, qn, kn, sinks, w_out, tabs, *, batch, seq):
    w = w_in.astype(BF16)
    qw = SWA_Q_HEADS * HEAD_DIM
    kw = SWA_KV_HEADS * HEAD_DIM
    q = _proj(h, w[:, :qw], seq=seq, tn=512, gain=_tile_heads(qn, SWA_Q_HEADS), rope_tabs=tabs)
    k = _proj(h, w[:, qw:qw + kw], seq=seq, tn=kw, gain=_tile_heads(kn, SWA_KV_HEADS), rope_tabs=tabs)
    v = _proj(h, w[:, qw + kw:], seq=seq, tn=kw)
    o = _swa(q, k, v, sinks, batch=batch, seq=seq)
    return _outproj([o], w_out.astype(BF16), x, ffn_gain)


def kernel(x, attn_norm, ffn_norm, ev_w_in, ev_forget_bias, ev_q_norm_a, ev_k_norm_a, ev_q_norm_b,
           ev_k_norm_b, ev_w_out, od_w_in, od_q_norm, od_k_norm, od_sinks, od_w_out,
           peer_w_query, peer_sub_keys, peer_down, peer_up):
    batch, seq, d_model = x.shape
    depth = attn_norm.shape[0]
    tabs = _rope_tables(seq)
    xt = x.reshape(batch * seq, d_model)
    h = _rmsnorm(xt, attn_norm[0])
    for l in range(depth):
        i = l // 2
        if l % 2 == 0:
            xt, h2 = _even_mixer(xt, h, ffn_norm[l], ev_w_in[i], ev_forget_bias[i], ev_q_norm_a[i],
                                 ev_k_norm_a[i], ev_q_norm_b[i], ev_k_norm_b[i], ev_w_out[i], tabs,
                                 batch=batch, seq=seq)
        else:
            xt, h2 = _odd_mixer(xt, h, ffn_norm[l], od_w_in[i], od_q_norm[i], od_k_norm[i], od_sinks[i],
                                od_w_out[i], tabs, batch=batch, seq=seq)
        next_gain = attn_norm[l + 1] if l + 1 < depth else None
        xt, h = _peer_layer(xt, h2, next_gain, peer_w_query[l], peer_sub_keys[l], peer_down[l], peer_up[l])
    return xt.reshape(batch, seq, d_model)
```

```python
import functools

import numpy as np
import jax
import jax.numpy as jnp
from jax import lax
from jax.experimental import pallas as pl
from jax.experimental.pallas import tpu as pltpu

F32 = jnp.float32
BF16 = jnp.bfloat16

HEAD_DIM = 64
ROT_DIM = HEAD_DIM // 4
ROPE_THETA = 500000.0
ATTN_SCALE = HEAD_DIM ** -0.5
EPS = 1e-6
NEG_INF = -1e30

MOBA_HEADS = 8
FOX_HEADS = 8
MOBA_BLOCK = 256
MOBA_TOPK = 3
A_W = MOBA_HEADS * HEAD_DIM
B_W = FOX_HEADS * HEAD_DIM

SWA_Q_HEADS = 16
SWA_KV_HEADS = 2
SWA_WINDOW = 128

PEER_HEADS = 8
PEER_N_KEYS = 128
PEER_TOPK = 16
PEER_QUERY_DIM = 128

LANES = 128
PAIR_W = 2 * HEAD_DIM
VMEM_LIMIT = 48 * 1024 * 1024
VMEM_LIMIT_EXPERTS = 58 * 1024 * 1024


def _cparams(sem, vmem_limit=VMEM_LIMIT):
    return pltpu.CompilerParams(dimension_semantics=sem, vmem_limit_bytes=vmem_limit)


def _dot_nt(a, b):
    return lax.dot_general(a, b, (((1,), (1,)), ((), ())), preferred_element_type=F32)


def _dot(a, b):
    return jnp.dot(a, b, preferred_element_type=F32)


def _split3(x):
    h1 = x.astype(BF16)
    r1 = x - h1.astype(F32)
    h2 = r1.astype(BF16)
    h3 = (r1 - h2.astype(F32)).astype(BF16)
    return h1, h2, h3


def _rmsnorm_kernel(x_ref, g_ref, o_ref):
    x = x_ref[...]
    ms = jnp.mean(x * x, axis=-1, keepdims=True)
    o_ref[...] = (x * lax.rsqrt(ms + EPS) * g_ref[...]).astype(o_ref.dtype)


def _rmsnorm(x, gain, tm=512):
    T, D = x.shape
    return pl.pallas_call(
        _rmsnorm_kernel,
        grid=(T // tm,),
        in_specs=[pl.BlockSpec((tm, D), lambda i: (i, 0)),
                  pl.BlockSpec((1, D), lambda i: (0, 0))],
        out_specs=pl.BlockSpec((tm, D), lambda i: (i, 0)),
        out_shape=jax.ShapeDtypeStruct((T, D), BF16),
        compiler_params=_cparams(("parallel",)),
        name="rmsnorm",
    )(x, gain.reshape(1, D))


def _proj_kernel(*refs, norm, rope, kmean, tn):
    it = iter(refs)
    h_ref, w_ref = next(it), next(it)
    gain_ref = next(it) if norm else None
    bd_ref = next(it) if norm else None
    if rope:
        c_ref, sa_ref, sb_ref = next(it), next(it), next(it)
    o_ref = next(it)
    km_ref = next(it) if kmean else None

    y = _dot(h_ref[...], w_ref[...])
    if norm:
        y2 = y * y
        bd = bd_ref[...]
        cols = []
        for c in range(tn // LANES):
            h1, h2, h3 = _split3(y2[:, c * LANES:(c + 1) * LANES])
            cols.append(_dot(h1, bd) + _dot(h2, bd) + _dot(h3, bd))
        ms = cols[0] if len(cols) == 1 else jnp.concatenate(cols, axis=1)
        y = y * lax.rsqrt(ms + EPS) * gain_ref[...]
    if rope:
        rep = tn // LANES
        tile = (lambda t: t) if rep == 1 else (lambda t: jnp.concatenate([t] * rep, axis=1))
        y = (y * tile(c_ref[...])
             + pltpu.roll(y, tn - ROT_DIM // 2, 1) * tile(sa_ref[...])
             + pltpu.roll(y, ROT_DIM // 2, 1) * tile(sb_ref[...]))
    o_ref[...] = y.astype(o_ref.dtype)
    if kmean:
        km_ref[0] = jnp.mean(y, axis=0, keepdims=True)


def _proj(h, w, *, seq, tn, tm=512, gain=None, rope_tabs=None, kmean=False):
    T, D = h.shape
    N = w.shape[1]
    norm = gain is not None
    rope = rope_tabs is not None
    nseq = seq // tm
    in_specs = [pl.BlockSpec((tm, D), lambda i, j: (i, 0)),
                pl.BlockSpec((D, tn), lambda i, j: (0, j))]
    args = [h, w]
    if norm:
        bd = np.kron(np.eye(LANES // HEAD_DIM), np.ones((HEAD_DIM, HEAD_DIM))) / HEAD_DIM
        in_specs += [pl.BlockSpec((1, tn), lambda i, j: (0, j)),
                     pl.BlockSpec((LANES, LANES), lambda i, j: (0, 0))]
        args += [gain.reshape(1, N).astype(F32), jnp.asarray(bd, BF16)]
    if rope:
        in_specs += [pl.BlockSpec((tm, LANES), lambda i, j: (i % nseq, 0))] * 3
        args += list(rope_tabs)
    out_specs = [pl.BlockSpec((tm, tn), lambda i, j: (i, j))]
    out_shape = [jax.ShapeDtypeStruct((T, N), BF16)]
    if kmean:
        out_specs.append(pl.BlockSpec((1, 1, tn), lambda i, j: (i, 0, j)))
        out_shape.append(jax.ShapeDtypeStruct((T // tm, 1, N), F32))
    res = pl.pallas_call(
        functools.partial(_proj_kernel, norm=norm, rope=rope, kmean=kmean, tn=tn),
        grid=(T // tm, N // tn),
        in_specs=in_specs, out_specs=out_specs, out_shape=out_shape,
        compiler_params=_cparams(("parallel", "parallel")),
        name="proj",
    )(*args)
    return res if kmean else res[0]


def _rope_tables(seq):
    half = ROT_DIM // 2
    inv_freq = jnp.power(ROPE_THETA, -jnp.arange(0, ROT_DIM, 2, dtype=F32) / ROT_DIM)
    ang = jnp.arange(seq, dtype=F32)[:, None] * inv_freq[None, :]
    cos, sin = jnp.cos(ang), jnp.sin(ang)
    one = jnp.ones((seq, HEAD_DIM - ROT_DIM), F32)
    zero = jnp.zeros((seq, HEAD_DIM - ROT_DIM), F32)
    z8 = jnp.zeros((seq, half), F32)
    c = jnp.concatenate([cos, cos, one], axis=1)
    sa = jnp.concatenate([-sin, z8, zero], axis=1)
    sb = jnp.concatenate([z8, sin, zero], axis=1)
    rep = LANES // HEAD_DIM
    return tuple(jnp.concatenate([t] * rep, axis=1) for t in (c, sa, sb))


def _gates_kernel(h_ref, wf_ref, b_ref, tri_ref, o_ref, carry_ref):
    @pl.when(pl.program_id(1) == 0)
    def _():
        carry_ref[...] = jnp.zeros_like(carry_ref)

    z = _dot_nt(wf_ref[...], h_ref[...]) + b_ref[...][:, :1]
    lf = jnp.minimum(z, 0.0) - jnp.log1p(jnp.exp(-jnp.abs(z)))
    tri = tri_ref[...]
    h1, h2, h3 = _split3(lf)
    cs = _dot(h1, tri) + _dot(h2, tri) + _dot(h3, tri) + carry_ref[...][:, :1]
    o_ref[0] = cs
    carry_ref[...] = jnp.broadcast_to(cs[:, -1:], carry_ref.shape)


def _fox_gates(h, wf_t, bias, *, batch, seq, tm=512):
    T, D = h.shape
    nh = wf_t.shape[0]
    nseq = seq // tm
    tri = jnp.asarray(np.triu(np.ones((tm, tm))), BF16)
    return pl.pallas_call(
        _gates_kernel,
        grid=(batch, nseq),
        in_specs=[pl.BlockSpec((tm, D), lambda b, s: (b * nseq + s, 0)),
                  pl.BlockSpec((nh, D), lambda b, s: (0, 0)),
                  pl.BlockSpec((nh, LANES), lambda b, s: (0, 0)),
                  pl.BlockSpec((tm, tm), lambda b, s: (0, 0))],
        out_specs=pl.BlockSpec((1, nh, tm), lambda b, s: (b, 0, s)),
        out_shape=jax.ShapeDtypeStruct((batch, nh, seq), F32),
        scratch_shapes=[pltpu.VMEM((nh, LANES), F32)],
        compiler_params=_cparams(("parallel", "arbitrary")),
        name="fox_gates",
    )(h, wf_t, jnp.broadcast_to(bias.astype(F32)[:, None], (nh, LANES)), tri)


def _lane_tile(x, width):
    rep = width // LANES
    return x if rep == 1 else jnp.concatenate([x] * rep, axis=1)


def _flash_init(m_ref, acc_ref):
    m_ref[...] = jnp.full(m_ref.shape, NEG_INF, F32)
    acc_ref[...] = jnp.zeros(acc_ref.shape, F32)


def _head_values(v):
    lane = lax.broadcasted_iota(jnp.int32, v.shape, 1)
    return [jnp.where((lane >= hh * HEAD_DIM) & (lane < (hh + 1) * HEAD_DIM), v, jnp.ones_like(v))
            for hh in range(2)]


def _flash_update(slot, s, v, m_ref, acc_ref):
    tk = s.shape[1]
    m_prev = m_ref[slot]
    m_new = jnp.maximum(m_prev, jnp.max(s, axis=1, keepdims=True))
    alpha = jnp.exp(m_prev - m_new)
    p = jnp.exp(s - _lane_tile(m_new, tk))
    acc_ref[slot] = alpha * acc_ref[slot] + _dot(p.astype(BF16), v)
    m_ref[slot] = m_new


def _flash_finish(lane, acc_ref):
    outs = []
    for hh in range(2):
        acc = acc_ref[hh]
        den = (1 - hh) * HEAD_DIM
        outs.append(acc / acc[:, den:den + 1])
    return jnp.where(lane < HEAD_DIM, outs[0], outs[1])


def _flash_tiles(qi, tk, logits, values, m_ref, acc_ref):
    def absorb(s, off):
        vh = values(off)
        for hh in range(2):
            _flash_update(hh, s[hh], vh[hh], m_ref, acc_ref)

    def pair(off_a, off_b, b_diagonal):
        sa = logits(off_a, False)
        sb = logits(off_b, b_diagonal)
        absorb(sa, off_a)
        absorb(sb, off_b)

    def body(jj, carry):
        off = pl.multiple_of(2 * jj * tk, 2 * tk)
        pair(off, off + tk, False)
        return carry

    lax.fori_loop(0, qi // 2, body, 0)
    diag = pl.multiple_of(qi * tk, tk)

    @pl.when(qi % 2 == 1)
    def _():
        pair(diag - tk, diag, True)

    @pl.when(qi % 2 == 0)
    def _():
        absorb(logits(diag, True), diag)


def _head_queries(q, lane):
    qs = q * ATTN_SCALE
    return [jnp.where((lane >= hh * HEAD_DIM) & (lane < (hh + 1) * HEAD_DIM), qs, jnp.zeros_like(qs))
            for hh in range(2)]


def _moba_kernel(q_ref, k_ref, v_ref, km_ref, o_ref, m_ref, acc_ref):
    tq = q_ref.shape[1]
    tk = tq
    qi = pl.program_id(2)
    lane = lax.broadcasted_iota(jnp.int32, (tq, LANES), 1)
    lane_f = lane.astype(F32)
    rowv = lax.broadcasted_iota(jnp.int32, (tq, LANES), 0)
    row_blk = 2 * qi + (rowv >= MOBA_BLOCK).astype(jnp.int32)
    row = lax.broadcasted_iota(jnp.int32, (tq, tk), 0)
    col = lax.broadcasted_iota(jnp.int32, (tq, tk), 1)
    qh = _head_queries(q_ref[0], lane)
    _flash_init(m_ref, acc_ref)

    sels = []
    for hh in range(2):
        gate = _dot_nt(qh[hh], km_ref[0])
        gate = jnp.where(lane < row_blk, gate, -jnp.inf)
        sel = jnp.zeros((tq, LANES), F32)
        for _ in range(MOBA_TOPK):
            m = jnp.max(gate, axis=1, keepdims=True)
            idx = jnp.min(jnp.where(gate == m, lane_f, float(LANES)), axis=1, keepdims=True)
            hit = lane_f == idx
            sel = jnp.where(hit & (m > -jnp.inf), 1.0, sel)
            gate = jnp.where(hit, -jnp.inf, gate)
        sels.append(sel)

    def chosen(sel, blk):
        return jnp.max(jnp.where(lane == blk, sel, 0.0), axis=1, keepdims=True) > 0.0

    def logits(off, diagonal):
        kj = k_ref[0, pl.ds(off, tk), :]
        out = []
        for hh in range(2):
            s = _dot_nt(qh[hh], kj)
            if diagonal:
                visible = (col >= MOBA_BLOCK) | (row < MOBA_BLOCK) | chosen(sels[hh], 2 * qi)
                s = jnp.where((col <= row) & visible, s, NEG_INF)
            else:
                blk = 2 * (off // tk)
                s = jnp.concatenate(
                    [jnp.where(chosen(sels[hh], blk), s[:, :MOBA_BLOCK], NEG_INF),
                     jnp.where(chosen(sels[hh], blk + 1), s[:, MOBA_BLOCK:], NEG_INF)], axis=1)
            out.append(s)
        return out

    def values(off):
        return _head_values(v_ref[0, pl.ds(off, tk), :])

    _flash_tiles(qi, tk, logits, values, m_ref, acc_ref)
    o_ref[0] = _flash_finish(lane, acc_ref).astype(o_ref.dtype)


def _flash_scratch(tq):
    return [pltpu.VMEM((2, tq, LANES), F32)] * 2


def _moba(q, k, v, kmean, *, batch, seq):
    W = q.shape[1]
    tq = 2 * MOBA_BLOCK
    q3, k3, v3 = (t.reshape(batch, seq, W) for t in (q, k, v))
    out = pl.pallas_call(
        _moba_kernel,
        grid=(batch, W // PAIR_W, seq // tq),
        in_specs=[pl.BlockSpec((1, tq, PAIR_W), lambda b, p, i: (b, i, p)),
                  pl.BlockSpec((1, seq, PAIR_W), lambda b, p, i: (b, 0, p)),
                  pl.BlockSpec((1, seq, PAIR_W), lambda b, p, i: (b, 0, p)),
                  pl.BlockSpec((1, LANES, PAIR_W), lambda b, p, i: (b, 0, p))],
        out_specs=pl.BlockSpec((1, tq, PAIR_W), lambda b, p, i: (b, i, p)),
        out_shape=jax.ShapeDtypeStruct((batch, seq, W), BF16),
        scratch_shapes=_flash_scratch(tq),
        compiler_params=_cparams(("parallel", "parallel", "parallel")),
        name="moba",
    )(q3, k3, v3, kmean)
    return out.reshape(batch * seq, W)


def _fox_kernel(q_ref, k_ref, v_ref, g_ref, frow_ref, o_ref, m_ref, acc_ref):
    tq = q_ref.shape[1]
    tk = tq
    pr = pl.program_id(1)
    qi = pl.program_id(2)
    lane = lax.broadcasted_iota(jnp.int32, (tq, LANES), 1)
    row = lax.broadcasted_iota(jnp.int32, (tq, tk), 0)
    col = lax.broadcasted_iota(jnp.int32, (tq, tk), 1)
    qh = _head_queries(q_ref[0], lane)
    _flash_init(m_ref, acc_ref)

    def key_gates(off, width):
        f_all = frow_ref[0, :, pl.ds(off, width)]
        sub = lax.broadcasted_iota(jnp.int32, f_all.shape, 0)
        return [jnp.sum(jnp.where(sub == 2 * pr + hh, f_all, 0.0), axis=0, keepdims=True) for hh in range(2)]

    f_ref = [f[:, :1] for f in key_gates(pl.multiple_of(qi * tq, tq), LANES)]

    def logits(off, diagonal):
        kj = k_ref[0, pl.ds(off, tk), :]
        fk = key_gates(off, tk)
        out = []
        for hh in range(2):
            s = _dot_nt(qh[hh], kj) - (fk[hh] - f_ref[hh])
            out.append(jnp.where(col <= row, s, NEG_INF) if diagonal else s)
        return out

    def values(off):
        return _head_values(v_ref[0, pl.ds(off, tk), :])

    _flash_tiles(qi, tk, logits, values, m_ref, acc_ref)
    o = _flash_finish(lane, acc_ref)
    o_ref[0] = (o * jax.nn.sigmoid(g_ref[0].astype(F32))).astype(o_ref.dtype)


def _fox(q, k, v, g, frow, *, batch, seq, tq=512):
    W = q.shape[1]
    nh = frow.shape[1]
    q3, k3, v3, g3 = (t.reshape(batch, seq, W) for t in (q, k, v, g))
    out = pl.pallas_call(
        _fox_kernel,
        grid=(batch, W // PAIR_W, seq // tq),
        in_specs=[pl.BlockSpec((1, tq, PAIR_W), lambda b, p, i: (b, i, p)),
                  pl.BlockSpec((1, seq, PAIR_W), lambda b, p, i: (b, 0, p)),
                  pl.BlockSpec((1, seq, PAIR_W), lambda b, p, i: (b, 0, p)),
                  pl.BlockSpec((1, tq, PAIR_W), lambda b, p, i: (b, i, p)),
                  pl.BlockSpec((1, nh, seq), lambda b, p, i: (b, 0, 0))],
        out_specs=pl.BlockSpec((1, tq, PAIR_W), lambda b, p, i: (b, i, p)),
        out_shape=jax.ShapeDtypeStruct((batch, seq, W), BF16),
        scratch_shapes=_flash_scratch(tq),
        compiler_params=_cparams(("parallel", "parallel", "parallel")),
        name="fox",
    )(q3, k3, v3, g3, frow)
    return out.reshape(batch * seq, W)


def _swa_kernel(q_ref, k_ref, v_ref, sink_ref, bias_ref, o_ref):
    tq = q_ref.shape[1]
    qi = pl.program_id(1)
    group = SWA_Q_HEADS // SWA_KV_HEADS
    tk = tq + SWA_WINDOW
    lane = lax.broadcasted_iota(jnp.int32, (tq, LANES), 1)
    kstart = pl.multiple_of(jnp.maximum(qi * tq - SWA_WINDOW, 0), SWA_WINDOW)
    k = k_ref[0, pl.ds(kstart, tk), :]
    vh = _head_values(v_ref[0, pl.ds(kstart, tk), :])
    bias = bias_ref[jnp.minimum(qi, 1)]
    bias = jnp.concatenate([bias] * group, axis=0)
    sink_tab = sink_ref[...]
    outs = [None] * SWA_Q_HEADS
    for c in range(SWA_KV_HEADS):
        pieces, sinks = [], []
        for g in range(group):
            head = c * group + g
            blk = q_ref[0, :, (head // 2) * PAIR_W:(head // 2 + 1) * PAIR_W] * ATTN_SCALE
            hh = head % 2
            qm = jnp.where((lane >= hh * HEAD_DIM) & (lane < (hh + 1) * HEAD_DIM), blk, jnp.zeros_like(blk))
            if hh != c:
                qm = pltpu.roll(qm.astype(F32), HEAD_DIM, 1).astype(BF16)
            pieces.append(qm)
            sinks.append(jnp.broadcast_to(sink_tab[head:head + 1, :], (tq, LANES)))
        qs = jnp.concatenate(pieces, axis=0)
        sink = jnp.concatenate(sinks, axis=0)
        s = _dot_nt(qs, k) + bias
        m = jnp.maximum(sink, jnp.max(s, axis=1, keepdims=True))
        p = jnp.exp(s - _lane_tile(m, tk))
        acc = _dot(p.astype(BF16), vh[c])
        den = pltpu.roll(acc, HEAD_DIM, 1) + jnp.exp(sink - m)
        o = acc / den
        for g in range(group):
            head = c * group + g
            oh = o[g * tq:(g + 1) * tq]
            outs[head] = oh if head % 2 == c else pltpu.roll(oh, HEAD_DIM, 1)
    for pp in range(SWA_Q_HEADS // 2):
        o_ref[0, :, pp * PAIR_W:(pp + 1) * PAIR_W] = jnp.where(
            lane < HEAD_DIM, outs[2 * pp], outs[2 * pp + 1]).astype(o_ref.dtype)


def _swa_bias(tq):
    r = np.arange(tq)[:, None]
    c = np.arange(tq + SWA_WINDOW)[None, :]
    tabs = []
    for key_offset in (0, SWA_WINDOW):
        dist = r + key_offset - c
        tabs.append(np.where((dist >= 0) & (dist < SWA_WINDOW), 0.0, NEG_INF))
    return jnp.asarray(np.stack(tabs), F32)


def _swa(q, k, v, sinks, *, batch, seq, tq=SWA_WINDOW):
    W = q.shape[1]
    tk = tq + SWA_WINDOW
    q3 = q.reshape(batch, seq, W)
    k3, v3 = (t.reshape(batch, seq, PAIR_W) for t in (k, v))
    sink_tab = jnp.broadcast_to(sinks.astype(F32)[:, None], (SWA_Q_HEADS, LANES))
    out = pl.pallas_call(
        _swa_kernel,
        grid=(batch, seq // tq),
        in_specs=[pl.BlockSpec((1, tq, W), lambda b, i: (b, i, 0)),
                  pl.BlockSpec((1, seq, PAIR_W), lambda b, i: (b, 0, 0)),
                  pl.BlockSpec((1, seq, PAIR_W), lambda b, i: (b, 0, 0)),
                  pl.BlockSpec((SWA_Q_HEADS, LANES), lambda b, i: (0, 0)),
                  pl.BlockSpec((2, tq, tk), lambda b, i: (0, 0, 0))],
        out_specs=pl.BlockSpec((1, tq, W), lambda b, i: (b, i, 0)),
        out_shape=jax.ShapeDtypeStruct((batch, seq, W), BF16),
        compiler_params=_cparams(("parallel", "parallel")),
        name="swa",
    )(q3, k3, v3, sink_tab, _swa_bias(tq))
    return out.reshape(batch * seq, W)


def _rms_normed(x, gain):
    ms = jnp.mean(x * x, axis=-1, keepdims=True)
    return (x * lax.rsqrt(ms + EPS) * gain).astype(BF16)


def _outproj_kernel(*refs, n_parts):
    parts = refs[:n_parts]
    w_ref, x_ref, g_ref, o_ref, h_ref = refs[n_parts:]
    y = x_ref[...]
    off = 0
    for p_ref in parts:
        kw = p_ref.shape[1]
        y = y + _dot(p_ref[...], w_ref[off:off + kw, :])
        off += kw
    o_ref[...] = y
    h_ref[...] = _rms_normed(y, g_ref[...])


def _outproj(parts, w, x, next_gain, tm=512):
    T, D = x.shape
    in_specs = [pl.BlockSpec((tm, p.shape[1]), lambda i: (i, 0)) for p in parts]
    in_specs += [pl.BlockSpec(w.shape, lambda i: (0, 0)),
                 pl.BlockSpec((tm, D), lambda i: (i, 0)),
                 pl.BlockSpec((1, D), lambda i: (0, 0))]
    row_spec = pl.BlockSpec((tm, D), lambda i: (i, 0))
    return pl.pallas_call(
        functools.partial(_outproj_kernel, n_parts=len(parts)),
        grid=(T // tm,),
        in_specs=in_specs,
        out_specs=[row_spec, row_spec],
        out_shape=[jax.ShapeDtypeStruct((T, D), F32), jax.ShapeDtypeStruct((T, D), BF16)],
        compiler_params=_cparams(("parallel",)),
        name="outproj",
    )(*parts, w, x, next_gain.reshape(1, D).astype(F32))


_CAND_ROWS = 80


def _cand_tables(lanes):
    pos = np.zeros((_CAND_ROWS,), np.float32)
    neg = np.zeros((_CAND_ROWS,), np.float32)
    r = 0
    for a, nb in ((0, 16), (1, 8), (2, 8), (3, 8), (4, 8), (5, 8), (6, 8), (7, 8)):
        for b in range(nb):
            pos[r] = a * PEER_TOPK + b
            neg[r] = 0.0 if (a + 1) * (b + 1) <= PEER_TOPK else -np.inf
            r += 1
    for a in range(8, 16):
        pos[r] = a * PEER_TOPK
        r += 1
    assert r == _CAND_ROWS
    tab = lambda t: jnp.asarray(np.broadcast_to(t[:, None], (_CAND_ROWS, lanes)).copy())
    return tab(pos), tab(neg)


def _batcher_pairs(n):
    pairs, p = [], 1
    while p < n:
        k = p
        while k >= 1:
            for j in range(k % p, n - k, 2 * k):
                for i in range(min(k, n - j - k)):
                    if (i + j) // (2 * p) == (i + j + k) // (2 * p):
                        pairs.append((i + j, i + j + k))
            k //= 2
        p *= 2
    return pairs


_SORT16 = _batcher_pairs(PEER_TOPK)
_N_CAND_PIECES = 10
_SORT10 = [(i, j) for i, j in _SORT16 if j < _N_CAND_PIECES]
_SUBLANES = 8


def _compare_exchange(items, i, j):
    items[i], items[j] = jnp.maximum(items[i], items[j]), jnp.minimum(items[i], items[j])


def _top_sorted(pieces, pairs):
    items = list(pieces)
    for i, j in pairs:
        _compare_exchange(items, i, j)
    n = PEER_TOPK
    items += [jnp.full(items[0].shape, -jnp.inf, F32)] * (n - len(items))
    for shift in (4, 2, 1):
        items = [jnp.maximum(items[i], pltpu.roll(items[n - 1 - i], shift, 0)) for i in range(n)]
        d = n // 2
        while d >= 1:
            for i in range(n):
                if i & d == 0:
                    _compare_exchange(items, i, i + d)
            d //= 2
    return items


def _pieces(x):
    return [x[_SUBLANES * g:_SUBLANES * (g + 1)] for g in range(x.shape[0] // _SUBLANES)]


def _sublane_total(x):
    for shift in (4, 2, 1):
        x = x + pltpu.roll(x, shift, 0)
    return x


def _count_ge(pieces, thr):
    total = jnp.zeros(thr.shape, F32)
    for p in pieces:
        total = total + jnp.where(p >= thr, 1.0, 0.0)
    return _sublane_total(total)


def _route_head_fast(s1, s2):
    p1, p2 = _pieces(s1), _pieces(s2)
    v1 = _top_sorted(p1, _SORT16)
    v2 = _top_sorted(p2, _SORT16)
    sub = lax.broadcasted_iota(jnp.int32, v1[0].shape, 0)

    def spread(vals):
        out = vals[0]
        for r in range(1, _SUBLANES):
            out = jnp.where(sub == r, vals[r], out)
        return out

    v2_lo, v2_hi, v1_hi = spread(v2[:8]), spread(v2[8:]), spread(v1[8:])
    cands = [v1[0] + v2_lo, v1[0] + v2_hi, v1[1] + v2_lo]
    for a in range(2, 8):
        cands.append(jnp.where(sub < PEER_TOPK // (a + 1), v1[a] + v2_lo, -jnp.inf))
    cands.append(v1_hi + v2[0])
    ts = _top_sorted(cands, _SORT10)
    tau = ts[PEER_TOPK - 1]
    z = jnp.exp(ts[0] - ts[0])
    for kk in range(1, PEER_TOPK):
        z = z + jnp.exp(ts[kk] - ts[0])

    tied = (_count_ge(p1, v1[-1]) != float(PEER_TOPK)) | (_count_ge(p2, v2[-1]) != float(PEER_TOPK))
    tied = tied | (_count_ge(cands, tau) != float(PEER_TOPK))
    for b in range(PEER_TOPK - 1):
        tied = tied | (v1[b] == v1[b + 1]) | (v2[b] == v2[b + 1])

    cnt = []
    for a in range(PEER_TOPK):
        c = jnp.zeros(tau.shape, F32)
        for b in range(PEER_TOPK // (a + 1)):
            c = c + jnp.where(v1[a] + v2[b] >= tau, 1.0, 0.0)
        cnt.append(c)
    c1, r2 = [], []
    for x in p1:
        c = jnp.zeros(x.shape, F32)
        for a in range(PEER_TOPK):
            c = jnp.where(x == v1[a], cnt[a], c)
        c1.append(c)
    for x in p2:
        r = jnp.zeros(x.shape, F32)
        for b in range(PEER_TOPK):
            r = r + jnp.where(v2[b] > x, 1.0, 0.0)
        r2.append(r)
    inv_z = 1.0 / z
    e1 = [jnp.exp(x - v1[0]) * inv_z for x in p1]
    e2 = [jnp.exp(x - v2[0]) for x in p2]
    cat = lambda ps: jnp.concatenate(ps, axis=0)
    return (cat(c1), cat(e1), cat(r2), cat(e2)), tied


def _extract_sorted(scores, by_key):
    nk, lanes = scores[0].shape
    kio = lax.broadcasted_iota(jnp.int32, (nk, lanes), 0).astype(F32)
    slot = lax.broadcasted_iota(jnp.int32, (PEER_TOPK, lanes), 0)

    def body(a, carry):
        here = slot == a
        out = []
        for (v, vals, aux), ranked in zip(carry, by_key):
            m = jnp.max(v, axis=0, keepdims=True)
            idx = jnp.min(jnp.where(v == m, kio, float(nk)), axis=0, keepdims=True)
            hit = kio == idx
            aux = jnp.where(hit, jnp.asarray(a, F32), aux) if ranked else jnp.where(here, idx, aux)
            out.append((jnp.where(hit, -jnp.inf, v), jnp.where(here, m, vals), aux))
        return tuple(out)

    small = jnp.zeros((PEER_TOPK, lanes), F32)
    unranked = jnp.full((nk, lanes), float(PEER_TOPK), F32)
    init = tuple((v, small, unranked if ranked else small) for v, ranked in zip(scores, by_key))
    return [(vals, aux) for _, vals, aux in lax.fori_loop(0, PEER_TOPK, body, init)]


def _route_head_exact(s1, s2, pos, neg):
    lanes = s1.shape[1]
    slot = lax.broadcasted_iota(jnp.int32, (PEER_TOPK, lanes), 0)
    kio = lax.broadcasted_iota(jnp.int32, (PEER_N_KEYS, lanes), 0).astype(F32)
    (v1, idx1), (v2, rank2) = _extract_sorted([s1, s2], [False, True])
    blocks = [v1[0:1] + v2[0:8], v1[0:1] + v2[8:16]]
    blocks += [v1[a:a + 1] + v2[0:8] for a in range(1, 8)]
    blocks += [v1[8:16] + v2[0:1]]
    cand = jnp.concatenate(blocks, axis=0) + neg

    def pick(kk, carry):
        cand, chosen, ts = carry
        m = jnp.max(cand, axis=0, keepdims=True)
        first = jnp.min(jnp.where(cand == m, pos, 1e9), axis=0, keepdims=True)
        hit = pos == first
        return (jnp.where(hit, -jnp.inf, cand), jnp.where(hit, 1.0, chosen), jnp.where(slot == kk, m, ts))

    _, chosen, ts = lax.fori_loop(0, PEER_TOPK, pick,
                                  (cand, jnp.zeros_like(cand), jnp.zeros((PEER_TOPK, lanes), F32)))
    z = jnp.sum(jnp.exp(ts - ts[0:1]), axis=0, keepdims=True)
    counts = [jnp.sum(chosen[0:16], axis=0, keepdims=True)]
    counts += [jnp.sum(chosen[8 * a + 8:8 * a + 16], axis=0, keepdims=True) for a in range(1, 8)]
    counts += [chosen[72 + a:73 + a] for a in range(8)]
    c1 = jnp.zeros((PEER_N_KEYS, lanes), F32)
    for a in range(PEER_TOPK):
        c1 = jnp.where(kio == idx1[a:a + 1], counts[a], c1)
    return c1, jnp.exp(s1 - v1[0:1]) / z, rank2, jnp.exp(s2 - v2[0:1])


def _route_kernel(h_ref, wq_ref, keys_ref, pos_ref, neg_ref,
                  c1_ref, e1_ref, r2_ref, e2_ref, qt_ref, sc_ref):
    half = PEER_QUERY_DIM // 2
    qt_ref[...] = _dot_nt(wq_ref[...], h_ref[...]).astype(BF16)

    def store(h, maps):
        c1, e1, r2, e2 = maps
        c1_ref[h] = c1
        e1_ref[h] = e1
        r2_ref[h] = r2.astype(BF16)
        e2_ref[h] = e2.astype(BF16)

    def head_body(h, _):
        r0 = pl.multiple_of(h * PEER_QUERY_DIM, PEER_QUERY_DIM)
        sc_ref[0] = _dot(keys_ref[2 * h], qt_ref[pl.ds(r0, half), :])
        sc_ref[1] = _dot(keys_ref[2 * h + 1], qt_ref[pl.ds(r0 + half, half), :])
        maps, tied = _route_head_fast(sc_ref[0], sc_ref[1])
        any_tied = jnp.max(jnp.where(tied, 1.0, 0.0)) > 0.0

        @pl.when(any_tied)
        def _():
            store(h, _route_head_exact(sc_ref[0], sc_ref[1], pos_ref[...], neg_ref[...]))

        @pl.when(jnp.logical_not(any_tied))
        def _():
            store(h, maps)

        return 0

    lax.fori_loop(0, PEER_HEADS, head_body, 0)


def _peer_route(h2, wq_t, keys, tt=2 * LANES):
    T, D = h2.shape
    pos, neg = _cand_tables(tt)
    stat_spec = pl.BlockSpec((PEER_HEADS, PEER_N_KEYS, tt), lambda i: (0, 0, i))
    stat = lambda dt: jax.ShapeDtypeStruct((PEER_HEADS, PEER_N_KEYS, T), dt)
    return pl.pallas_call(
        _route_kernel,
        grid=(T // tt,),
        in_specs=[pl.BlockSpec((tt, D), lambda i: (i, 0)),
                  pl.BlockSpec(wq_t.shape, lambda i: (0, 0)),
                  pl.BlockSpec(keys.shape, lambda i: (0, 0, 0)),
                  pl.BlockSpec((_CAND_ROWS, tt), lambda i: (0, 0)),
                  pl.BlockSpec((_CAND_ROWS, tt), lambda i: (0, 0))],
        out_specs=[stat_spec] * 4,
        out_shape=[stat(F32), stat(F32), stat(BF16), stat(BF16)],
        scratch_shapes=[pltpu.VMEM((PEER_HEADS * PEER_QUERY_DIM, tt), BF16),
                        pltpu.VMEM((2, PEER_N_KEYS, tt), F32)],
        compiler_params=_cparams(("parallel",)),
        name="peer_route",
    )(h2, wq_t, keys, pos, neg)


_KEY_GROUP = 16
_UNITS = 4
_DRAIN_PIECES = 2


def _build_gated(a_ref, p_ref, c1_ref, e1_ref, r2_ref, e2_ref, key0, g0, ng, lt):
    rep = PEER_N_KEYS // 16

    def rows16(row):
        blk = jnp.broadcast_to(row, (16, LANES)).astype(BF16)
        return jnp.concatenate([blk] * rep, axis=0)

    ls = slice(lt * LANES, (lt + 1) * LANES)
    w = [jnp.zeros((PEER_N_KEYS, LANES), BF16) for _ in range(ng)]
    for h in range(PEER_HEADS):
        c1 = c1_ref[h, pl.ds(key0, _KEY_GROUP), ls]
        e1 = e1_ref[h, pl.ds(key0, _KEY_GROUP), ls]
        r2 = r2_ref[h, :, ls]
        e2 = e2_ref[h, :, ls]
        for g in range(ng):
            thr = rows16(c1[g0 + g:g0 + g + 1])
            gate = rows16(e1[g0 + g:g0 + g + 1])
            w[g] = w[g] + jnp.where(r2 < thr, e2, jnp.zeros_like(e2)) * gate
    for g in range(g0, g0 + ng):
        rs = slice(g * PEER_N_KEYS, (g + 1) * PEER_N_KEYS)
        a = a_ref[rs, ls]
        gelu = 0.5 * a * (1.0 + lax.erf(a * (2.0 ** -0.5)))
        p_ref[rs, ls] = gelu.astype(BF16) * w[g - g0]


def _experts_kernel(h_ref, dn_ref, upt_ref, c1_ref, e1_ref, r2_ref, e2_ref, x_ref, *rest, emit_norm):
    if emit_norm:
        g_ref, o_ref, hn_ref, a_ref, p_ref, acc_ref = rest
    else:
        o_ref, a_ref, p_ref, acc_ref = rest
    te, tt = a_ref.shape
    d_model = acc_ref.shape[0]
    e = pl.program_id(1)
    n_tiles = pl.num_programs(1) - 1
    cur = e % 2
    ng = _KEY_GROUP // _UNITS
    n_lane = tt // LANES
    key0 = pl.multiple_of(jnp.minimum(e, n_tiles - 1) * _KEY_GROUP, _KEY_GROUP)

    def front_mm(u):
        rows = te // _UNITS
        rs = slice(u * rows, (u + 1) * rows)
        a_ref[rs, :] = _dot_nt(dn_ref[rs, :], h_ref[...])

    def back_mm(r):
        rows = d_model // _DRAIN_PIECES
        rs = slice(r * rows, (r + 1) * rows)
        acc_ref[rs, :] += _dot(upt_ref[rs, :], p_ref[1 - cur])

    units_per_drain = _UNITS // _DRAIN_PIECES

    def run(front, back):
        if front:
            front_mm(0)
        for u in range(_UNITS):
            for lt in range(n_lane):
                if front:
                    _build_gated(a_ref, p_ref.at[cur], c1_ref, e1_ref, r2_ref, e2_ref, key0, u * ng, ng, lt)
                if front and lt == 0 and u + 1 < _UNITS:
                    front_mm(u + 1)
                if back and lt == n_lane // 2 and (u + 1) % units_per_drain == 0:
                    back_mm(u // units_per_drain)

    @pl.when(e == 0)
    def _():
        acc_ref[...] = jnp.zeros_like(acc_ref)
        run(True, False)

    @pl.when((e > 0) & (e < n_tiles))
    def _():
        run(True, True)

    @pl.when(e == n_tiles)
    def _():
        run(False, True)
        y = x_ref[...] + acc_ref[...].T
        o_ref[...] = y
        if emit_norm:
            hn_ref[...] = _rms_normed(y, g_ref[...])


def _peer_experts(h2, down, up_t, stats, x, next_gain, tt=512):
    T, D = h2.shape
    E = down.shape[0]
    te = _KEY_GROUP * PEER_N_KEYS
    n_tiles = E // te
    emit_norm = next_gain is not None
    stat_spec = pl.BlockSpec((PEER_HEADS, PEER_N_KEYS, tt), lambda i, e: (0, 0, i))
    row_spec = pl.BlockSpec((tt, D), lambda i, e: (i, 0))
    in_specs = [row_spec,
                pl.BlockSpec((te, D), lambda i, e: (jnp.minimum(e, n_tiles - 1), 0)),
                pl.BlockSpec((None, D, te), lambda i, e: (jnp.maximum(e - 1, 0), 0, 0)),
                stat_spec, stat_spec, stat_spec, stat_spec,
                row_spec]
    args = [h2, down, up_t, *stats, x]
    out_specs = [row_spec]
    out_shape = [jax.ShapeDtypeStruct((T, D), F32)]
    if emit_norm:
        in_specs.append(pl.BlockSpec((1, D), lambda i, e: (0, 0)))
        args.append(next_gain.reshape(1, D).astype(F32))
        out_specs.append(row_spec)
        out_shape.append(jax.ShapeDtypeStruct((T, D), BF16))
    res = pl.pallas_call(
        functools.partial(_experts_kernel, emit_norm=emit_norm),
        grid=(T // tt, n_tiles + 1),
        in_specs=in_specs, out_specs=out_specs, out_shape=out_shape,
        scratch_shapes=[pltpu.VMEM((te, tt), F32),
                        pltpu.VMEM((2, te, tt), BF16),
                        pltpu.VMEM((D, tt), F32)],
        compiler_params=_cparams(("parallel", "arbitrary"), VMEM_LIMIT_EXPERTS),
        name="peer_experts",
    )(*args)
    return (res[0], res[1]) if emit_norm else (res[0], None)


def _peer_layer(x, h2, next_gain, w_query, sub_keys, down, up):
    keys = sub_keys.reshape(PEER_HEADS * 2, PEER_N_KEYS, PEER_QUERY_DIM // 2).astype(BF16)
    stats = _peer_route(h2, w_query.T.astype(BF16), keys)
    te = _KEY_GROUP * PEER_N_KEYS
    up_t = up.reshape(up.shape[0] // te, te, up.shape[1]).transpose(0, 2, 1).astype(BF16)
    return _peer_experts(h2, down.astype(BF16), up_t, stats, x, next_gain)


def _tile_heads(g, n):
    return jnp.tile(g.astype(F32), n)


def _even_mixer(x, h, ffn_gain, w_in, f_bias, qn_a, kn_a, qn_b, kn_b, w_out, tabs, *, batch, seq):
    w = w_in.astype(BF16)
    o_qa, o_ka, o_va, o_qb, o_kb, o_vb, o_gb, o_fb = (
        0, A_W, 2 * A_W, 3 * A_W, 3 * A_W + B_W, 3 * A_W + 2 * B_W, 3 * A_W + 3 * B_W, 3 * A_W + 4 * B_W)
    gain_a = jnp.concatenate([_tile_heads(qn_a, MOBA_HEADS), _tile_heads(kn_a, MOBA_HEADS)])
    qk_a, km = _proj(h, w[:, o_qa:o_va], seq=seq, tn=A_W, tm=MOBA_BLOCK, gain=gain_a,
                     rope_tabs=tabs, kmean=True)
    nb = seq // MOBA_BLOCK
    kmean = km.reshape(batch, nb, 2 * A_W)[:, :, A_W:]
    kmean = jnp.pad(kmean, ((0, 0), (0, LANES - nb), (0, 0))).astype(BF16)
    gain_b = jnp.concatenate([_tile_heads(qn_b, FOX_HEADS), _tile_heads(kn_b, FOX_HEADS)])
    qk_b = _proj(h, w[:, o_qb:o_vb], seq=seq, tn=B_W, gain=gain_b)
    w_plain = jnp.concatenate([w[:, o_va:o_qb], w[:, o_vb:o_fb]], axis=1)
    vvg = _proj(h, w_plain, seq=seq, tn=A_W)
    va, vb, gb = vvg[:, :A_W], vvg[:, A_W:A_W + B_W], vvg[:, A_W + B_W:]
    frow = _fox_gates(h, w[:, o_fb:].T, f_bias, batch=batch, seq=seq)
    oa = _moba(qk_a[:, :A_W], qk_a[:, A_W:], va, kmean, batch=batch, seq=seq)
    ob = _fox(qk_b[:, :B_W], qk_b[:, B_W:], vb, gb, frow, batch=batch, seq=seq)
    return _outproj([oa, ob], w_out.astype(BF16), x, ffn_gain)


def _odd_mixer(x, h, ffn_gain, w_in, qn, kn, sinks, w_out, tabs, *, batch, seq):
    w = w_in.astype(BF16)
    qw = SWA_Q_HEADS * HEAD_DIM
    kw = SWA_KV_HEADS * HEAD_DIM
    q = _proj(h, w[:, :qw], seq=seq, tn=512, gain=_tile_heads(qn, SWA_Q_HEADS), rope_tabs=tabs)
    k = _proj(h, w[:, qw:qw + kw], seq=seq, tn=kw, gain=_tile_heads(kn, SWA_KV_HEADS), rope_tabs=tabs)
    v = _proj(h, w[:, qw + kw:], seq=seq, tn=kw)
    o = _swa(q, k, v, sinks, batch=batch, seq=seq)
    return _outproj([o], w_out.astype(BF16), x, ffn_gain)


def kernel(x, attn_norm, ffn_norm, ev_w_in, ev_forget_bias, ev_q_norm_a, ev_k_norm_a, ev_q_norm_b,
           ev_k_norm_b, ev_w_out, od_w_in, od_q_norm, od_k_norm, od_sinks, od_w_out,
           peer_w_query, peer_sub_keys, peer_down, peer_up):
    batch, seq, d_model = x.shape
    depth = attn_norm.shape[0]
    tabs = _rope_tables(seq)
    xt = x.reshape(batch * seq, d_model)
    h = _rmsnorm(xt, attn_norm[0])
    for l in range(depth):
        i = l // 2
        if l % 2 == 0:
            xt, h2 = _even_mixer(xt, h, ffn_norm[l], ev_w_in[i], ev_forget_bias[i], ev_q_norm_a[i],
                                 ev_k_norm_a[i], ev_q_norm_b[i], ev_k_norm_b[i], ev_w_out[i], tabs,
                                 batch=batch, seq=seq)
        else:
            xt, h2 = _odd_mixer(xt, h, ffn_norm[l], od_w_in[i], od_q_norm[i], od_k_norm[i], od_sinks[i],
                                od_w_out[i], tabs, batch=batch, seq=seq)
        next_gain = attn_norm[l + 1] if l + 1 < depth else None
        xt, h = _peer_layer(xt, h2, next_gain, peer_w_query[l], peer_sub_keys[l], peer_down[l], peer_up[l])
    return xt.reshape(batch, seq, d_model)
```

```python
import functools

import numpy as np
import jax
import jax.numpy as jnp
from jax import lax
from jax.experimental import pallas as pl
from jax.experimental.pallas import tpu as pltpu

F32 = jnp.float32
BF16 = jnp.bfloat16

HEAD_DIM = 64
ROT_DIM = HEAD_DIM // 4
ROPE_THETA = 500000.0
ATTN_SCALE = HEAD_DIM ** -0.5
EPS = 1e-6
NEG_INF = -1e30

MOBA_HEADS = 8
FOX_HEADS = 8
MOBA_BLOCK = 256
MOBA_TOPK = 3
A_W = MOBA_HEADS * HEAD_DIM
B_W = FOX_HEADS * HEAD_DIM

SWA_Q_HEADS = 16
SWA_KV_HEADS = 2
SWA_WINDOW = 128

PEER_HEADS = 8
PEER_N_KEYS = 128
PEER_TOPK = 16
PEER_QUERY_DIM = 128

LANES = 128
PAIR_W = 2 * HEAD_DIM
VMEM_LIMIT = 48 * 1024 * 1024
VMEM_LIMIT_EXPERTS = 58 * 1024 * 1024


def _cparams(sem, vmem_limit=VMEM_LIMIT):
    return pltpu.CompilerParams(dimension_semantics=sem, vmem_limit_bytes=vmem_limit)


def _dot_nt(a, b):
    return lax.dot_general(a, b, (((1,), (1,)), ((), ())), preferred_element_type=F32)


def _dot(a, b):
    return jnp.dot(a, b, preferred_element_type=F32)


def _split3(x):
    h1 = x.astype(BF16)
    r1 = x - h1.astype(F32)
    h2 = r1.astype(BF16)
    h3 = (r1 - h2.astype(F32)).astype(BF16)
    return h1, h2, h3


def _rmsnorm_kernel(x_ref, g_ref, o_ref):
    x = x_ref[...]
    ms = jnp.mean(x * x, axis=-1, keepdims=True)
    o_ref[...] = (x * lax.rsqrt(ms + EPS) * g_ref[...]).astype(o_ref.dtype)


def _rmsnorm(x, gain, tm=512):
    T, D = x.shape
    return pl.pallas_call(
        _rmsnorm_kernel,
        grid=(T // tm,),
        in_specs=[pl.BlockSpec((tm, D), lambda i: (i, 0)),
                  pl.BlockSpec((1, D), lambda i: (0, 0))],
        out_specs=pl.BlockSpec((tm, D), lambda i: (i, 0)),
        out_shape=jax.ShapeDtypeStruct((T, D), BF16),
        compiler_params=_cparams(("parallel",)),
        name="rmsnorm",
    )(x, gain.reshape(1, D))


def _proj_kernel(*refs, norm, rope, kmean, tn):
    it = iter(refs)
    h_ref, w_ref = next(it), next(it)
    gain_ref = next(it) if norm else None
    bd_ref = next(it) if norm else None
    if rope:
        c_ref, sa_ref, sb_ref = next(it), next(it), next(it)
    o_ref = next(it)
    km_ref = next(it) if kmean else None

    y = _dot(h_ref[...], w_ref[...])
    if norm:
        y2 = y * y
        bd = bd_ref[...]
        cols = []
        for c in range(tn // LANES):
            h1, h2, h3 = _split3(y2[:, c * LANES:(c + 1) * LANES])
            cols.append(_dot(h1, bd) + _dot(h2, bd) + _dot(h3, bd))
        ms = cols[0] if len(cols) == 1 else jnp.concatenate(cols, axis=1)
        y = y * lax.rsqrt(ms + EPS) * gain_ref[...]
    if rope:
        rep = tn // LANES
        tile = (lambda t: t) if rep == 1 else (lambda t: jnp.concatenate([t] * rep, axis=1))
        y = (y * tile(c_ref[...])
             + pltpu.roll(y, tn - ROT_DIM // 2, 1) * tile(sa_ref[...])
             + pltpu.roll(y, ROT_DIM // 2, 1) * tile(sb_ref[...]))
    o_ref[...] = y.astype(o_ref.dtype)
    if kmean:
        km_ref[0] = jnp.mean(y, axis=0, keepdims=True)


def _proj(h, w, *, seq, tn, tm=512, gain=None, rope_tabs=None, kmean=False):
    T, D = h.shape
    N = w.shape[1]
    norm = gain is not None
    rope = rope_tabs is not None
    nseq = seq // tm
    in_specs = [pl.BlockSpec((tm, D), lambda i, j: (i, 0)),
                pl.BlockSpec((D, tn), lambda i, j: (0, j))]
    args = [h, w]
    if norm:
        bd = np.kron(np.eye(LANES // HEAD_DIM), np.ones((HEAD_DIM, HEAD_DIM))) / HEAD_DIM
        in_specs += [pl.BlockSpec((1, tn), lambda i, j: (0, j)),
                     pl.BlockSpec((LANES, LANES), lambda i, j: (0, 0))]
        args += [gain.reshape(1, N).astype(F32), jnp.asarray(bd, BF16)]
    if rope:
        in_specs += [pl.BlockSpec((tm, LANES), lambda i, j: (i % nseq, 0))] * 3
        args += list(rope_tabs)
    out_specs = [pl.BlockSpec((tm, tn), lambda i, j: (i, j))]
    out_shape = [jax.ShapeDtypeStruct((T, N), BF16)]
    if kmean:
        out_specs.append(pl.BlockSpec((1, 1, tn), lambda i, j: (i, 0, j)))
        out_shape.append(jax.ShapeDtypeStruct((T // tm, 1, N), F32))
    res = pl.pallas_call(
        functools.partial(_proj_kernel, norm=norm, rope=rope, kmean=kmean, tn=tn),
        grid=(T // tm, N // tn),
        in_specs=in_specs, out_specs=out_specs, out_shape=out_shape,
        compiler_params=_cparams(("parallel", "parallel")),
        name="proj",
    )(*args)
    return res if kmean else res[0]


def _rope_tables(seq):
    half = ROT_DIM // 2
    inv_freq = jnp.power(ROPE_THETA, -jnp.arange(0, ROT_DIM, 2, dtype=F32) / ROT_DIM)
    ang = jnp.arange(seq, dtype=F32)[:, None] * inv_freq[None, :]
    cos, sin = jnp.cos(ang), jnp.sin(ang)
    one = jnp.ones((seq, HEAD_DIM - ROT_DIM), F32)
    zero = jnp.zeros((seq, HEAD_DIM - ROT_DIM), F32)
    z8 = jnp.zeros((seq, half), F32)
    c = jnp.concatenate([cos, cos, one], axis=1)
    sa = jnp.concatenate([-sin, z8, zero], axis=1)
    sb = jnp.concatenate([z8, sin, zero], axis=1)
    rep = LANES // HEAD_DIM
    return tuple(jnp.concatenate([t] * rep, axis=1) for t in (c, sa, sb))


def _gates_kernel(h_ref, wf_ref, b_ref, tri_ref, o_ref, carry_ref):
    @pl.when(pl.program_id(1) == 0)
    def _():
        carry_ref[...] = jnp.zeros_like(carry_ref)

    z = _dot_nt(wf_ref[...], h_ref[...]) + b_ref[...][:, :1]
    lf = jnp.minimum(z, 0.0) - jnp.log1p(jnp.exp(-jnp.abs(z)))
    tri = tri_ref[...]
    h1, h2, h3 = _split3(lf)
    cs = _dot(h1, tri) + _dot(h2, tri) + _dot(h3, tri) + carry_ref[...][:, :1]
    o_ref[0] = cs
    carry_ref[...] = jnp.broadcast_to(cs[:, -1:], carry_ref.shape)


def _fox_gates(h, wf_t, bias, *, batch, seq, tm=512):
    T, D = h.shape
    nh = wf_t.shape[0]
    nseq = seq // tm
    tri = jnp.asarray(np.triu(np.ones((tm, tm))), BF16)
    return pl.pallas_call(
        _gates_kernel,
        grid=(batch, nseq),
        in_specs=[pl.BlockSpec((tm, D), lambda b, s: (b * nseq + s, 0)),
                  pl.BlockSpec((nh, D), lambda b, s: (0, 0)),
                  pl.BlockSpec((nh, LANES), lambda b, s: (0, 0)),
                  pl.BlockSpec((tm, tm), lambda b, s: (0, 0))],
        out_specs=pl.BlockSpec((1, nh, tm), lambda b, s: (b, 0, s)),
        out_shape=jax.ShapeDtypeStruct((batch, nh, seq), F32),
        scratch_shapes=[pltpu.VMEM((nh, LANES), F32)],
        compiler_params=_cparams(("parallel", "arbitrary")),
        name="fox_gates",
    )(h, wf_t, jnp.broadcast_to(bias.astype(F32)[:, None], (nh, LANES)), tri)


def _lane_tile(x, width):
    rep = width // LANES
    return x if rep == 1 else jnp.concatenate([x] * rep, axis=1)


def _flash_init(m_ref, acc_ref):
    m_ref[...] = jnp.full(m_ref.shape, NEG_INF, F32)
    acc_ref[...] = jnp.zeros(acc_ref.shape, F32)


def _head_values(v):
    lane = lax.broadcasted_iota(jnp.int32, v.shape, 1)
    return [jnp.where((lane >= hh * HEAD_DIM) & (lane < (hh + 1) * HEAD_DIM), v, jnp.ones_like(v))
            for hh in range(2)]


def _flash_update(slot, s, v, m_ref, acc_ref):
    tk = s.shape[1]
    m_prev = m_ref[slot]
    m_new = jnp.maximum(m_prev, jnp.max(s, axis=1, keepdims=True))
    alpha = jnp.exp(m_prev - m_new)
    p = jnp.exp(s - _lane_tile(m_new, tk))
    acc_ref[slot] = alpha * acc_ref[slot] + _dot(p.astype(BF16), v)
    m_ref[slot] = m_new


def _flash_finish(lane, acc_ref):
    outs = []
    for hh in range(2):
        acc = acc_ref[hh]
        den = (1 - hh) * HEAD_DIM
        outs.append(acc / acc[:, den:den + 1])
    return jnp.where(lane < HEAD_DIM, outs[0], outs[1])


def _flash_tiles(qi, tk, logits, values, m_ref, acc_ref):
    def absorb(s, off):
        vh = values(off)
        for hh in range(2):
            _flash_update(hh, s[hh], vh[hh], m_ref, acc_ref)

    def pair(off_a, off_b, b_diagonal):
        sa = logits(off_a, False)
        sb = logits(off_b, b_diagonal)
        absorb(sa, off_a)
        absorb(sb, off_b)

    def body(jj, carry):
        off = pl.multiple_of(2 * jj * tk, 2 * tk)
        pair(off, off + tk, False)
        return carry

    lax.fori_loop(0, qi // 2, body, 0)
    diag = pl.multiple_of(qi * tk, tk)

    @pl.when(qi % 2 == 1)
    def _():
        pair(diag - tk, diag, True)

    @pl.when(qi % 2 == 0)
    def _():
        absorb(logits(diag, True), diag)


def _head_queries(q, lane):
    qs = q * ATTN_SCALE
    return [jnp.where((lane >= hh * HEAD_DIM) & (lane < (hh + 1) * HEAD_DIM), qs, jnp.zeros_like(qs))
            for hh in range(2)]


def _moba_kernel(q_ref, k_ref, v_ref, km_ref, o_ref, m_ref, acc_ref):
    tq = q_ref.shape[1]
    tk = tq
    qi = pl.program_id(2)
    lane = lax.broadcasted_iota(jnp.int32, (tq, LANES), 1)
    lane_f = lane.astype(F32)
    rowv = lax.broadcasted_iota(jnp.int32, (tq, LANES), 0)
    row_blk = 2 * qi + (rowv >= MOBA_BLOCK).astype(jnp.int32)
    row = lax.broadcasted_iota(jnp.int32, (tq, tk), 0)
    col = lax.broadcasted_iota(jnp.int32, (tq, tk), 1)
    qh = _head_queries(q_ref[0], lane)
    _flash_init(m_ref, acc_ref)

    sels = []
    for hh in range(2):
        gate = _dot_nt(qh[hh], km_ref[0])
        gate = jnp.where(lane < row_blk, gate, -jnp.inf)
        sel = jnp.zeros((tq, LANES), F32)
        for _ in range(MOBA_TOPK):
            m = jnp.max(gate, axis=1, keepdims=True)
            idx = jnp.min(jnp.where(gate == m, lane_f, float(LANES)), axis=1, keepdims=True)
            hit = lane_f == idx
            sel = jnp.where(hit & (m > -jnp.inf), 1.0, sel)
            gate = jnp.where(hit, -jnp.inf, gate)
        sels.append(sel)

    def chosen(sel, blk):
        return jnp.max(jnp.where(lane == blk, sel, 0.0), axis=1, keepdims=True) > 0.0

    def logits(off, diagonal):
        kj = k_ref[0, pl.ds(off, tk), :]
        out = []
        for hh in range(2):
            s = _dot_nt(qh[hh], kj)
            if diagonal:
                visible = (col >= MOBA_BLOCK) | (row < MOBA_BLOCK) | chosen(sels[hh], 2 * qi)
                s = jnp.where((col <= row) & visible, s, NEG_INF)
            else:
                blk = 2 * (off // tk)
                s = jnp.concatenate(
                    [jnp.where(chosen(sels[hh], blk), s[:, :MOBA_BLOCK], NEG_INF),
                     jnp.where(chosen(sels[hh], blk + 1), s[:, MOBA_BLOCK:], NEG_INF)], axis=1)
            out.append(s)
        return out

    def values(off):
        return _head_values(v_ref[0, pl.ds(off, tk), :])

    _flash_tiles(qi, tk, logits, values, m_ref, acc_ref)
    o_ref[0] = _flash_finish(lane, acc_ref).astype(o_ref.dtype)


def _flash_scratch(tq):
    return [pltpu.VMEM((2, tq, LANES), F32)] * 2


def _moba(q, k, v, kmean, *, batch, seq):
    W = q.shape[1]
    tq = 2 * MOBA_BLOCK
    q3, k3, v3 = (t.reshape(batch, seq, W) for t in (q, k, v))
    out = pl.pallas_call(
        _moba_kernel,
        grid=(batch, W // PAIR_W, seq // tq),
        in_specs=[pl.BlockSpec((1, tq, PAIR_W), lambda b, p, i: (b, i, p)),
                  pl.BlockSpec((1, seq, PAIR_W), lambda b, p, i: (b, 0, p)),
                  pl.BlockSpec((1, seq, PAIR_W), lambda b, p, i: (b, 0, p)),
                  pl.BlockSpec((1, LANES, PAIR_W), lambda b, p, i: (b, 0, p))],
        out_specs=pl.BlockSpec((1, tq, PAIR_W), lambda b, p, i: (b, i, p)),
        out_shape=jax.ShapeDtypeStruct((batch, seq, W), BF16),
        scratch_shapes=_flash_scratch(tq),
        compiler_params=_cparams(("parallel", "parallel", "parallel")),
        name="moba",
    )(q3, k3, v3, kmean)
    return out.reshape(batch * seq, W)


def _fox_kernel(q_ref, k_ref, v_ref, g_ref, frow_ref, o_ref, m_ref, acc_ref):
    tq = q_ref.shape[1]
    tk = tq
    pr = pl.program_id(1)
    qi = pl.program_id(2)
    lane = lax.broadcasted_iota(jnp.int32, (tq, LANES), 1)
    row = lax.broadcasted_iota(jnp.int32, (tq, tk), 0)
    col = lax.broadcasted_iota(jnp.int32, (tq, tk), 1)
    qh = _head_queries(q_ref[0], lane)
    _flash_init(m_ref, acc_ref)

    def key_gates(off, width):
        f_all = frow_ref[0, :, pl.ds(off, width)]
        sub = lax.broadcasted_iota(jnp.int32, f_all.shape, 0)
        return [jnp.sum(jnp.where(sub == 2 * pr + hh, f_all, 0.0), axis=0, keepdims=True) for hh in range(2)]

    f_ref = [f[:, :1] for f in key_gates(pl.multiple_of(qi * tq, tq), LANES)]

    def logits(off, diagonal):
        kj = k_ref[0, pl.ds(off, tk), :]
        fk = key_gates(off, tk)
        out = []
        for hh in range(2):
            s = _dot_nt(qh[hh], kj) - (fk[hh] - f_ref[hh])
            out.append(jnp.where(col <= row, s, NEG_INF) if diagonal else s)
        return out

    def values(off):
        return _head_values(v_ref[0, pl.ds(off, tk), :])

    _flash_tiles(qi, tk, logits, values, m_ref, acc_ref)
    o = _flash_finish(lane, acc_ref)
    o_ref[0] = (o * jax.nn.sigmoid(g_ref[0].astype(F32))).astype(o_ref.dtype)


def _fox(q, k, v, g, frow, *, batch, seq, tq=512):
    W = q.shape[1]
    nh = frow.shape[1]
    q3, k3, v3, g3 = (t.reshape(batch, seq, W) for t in (q, k, v, g))
    out = pl.pallas_call(
        _fox_kernel,
        grid=(batch, W // PAIR_W, seq // tq),
        in_specs=[pl.BlockSpec((1, tq, PAIR_W), lambda b, p, i: (b, i, p)),
                  pl.BlockSpec((1, seq, PAIR_W), lambda b, p, i: (b, 0, p)),
                  pl.BlockSpec((1, seq, PAIR_W), lambda b, p, i: (b, 0, p)),
                  pl.BlockSpec((1, tq, PAIR_W), lambda b, p, i: (b, i, p)),
                  pl.BlockSpec((1, nh, seq), lambda b, p, i: (b, 0, 0))],
        out_specs=pl.BlockSpec((1, tq, PAIR_W), lambda b, p, i: (b, i, p)),
        out_shape=jax.ShapeDtypeStruct((batch, seq, W), BF16),
        scratch_shapes=_flash_scratch(tq),
        compiler_params=_cparams(("parallel", "parallel", "parallel")),
        name="fox",
    )(q3, k3, v3, g3, frow)
    return out.reshape(batch * seq, W)


def _swa_kernel(q_ref, k_ref, v_ref, sink_ref, bias_ref, o_ref):
    tq = q_ref.shape[1]
    qi = pl.program_id(1)
    group = SWA_Q_HEADS // SWA_KV_HEADS
    tk = tq + SWA_WINDOW
    lane = lax.broadcasted_iota(jnp.int32, (tq, LANES), 1)
    kstart = pl.multiple_of(jnp.maximum(qi * tq - SWA_WINDOW, 0), SWA_WINDOW)
    k = k_ref[0, pl.ds(kstart, tk), :]
    vh = _head_values(v_ref[0, pl.ds(kstart, tk), :])
    swap = lambda t: pltpu.roll(t.astype(F32), HEAD_DIM, 1).astype(BF16)
    k_by_half = [k, swap(k)]
    bias = bias_ref[jnp.minimum(qi, 1)]
    bias = jnp.concatenate([bias] * (group // 2), axis=0)
    sink_tab = sink_ref[...]

    chains = []
    for c in range(SWA_KV_HEADS):
        v_by_half = [vh[c], swap(vh[c])]
        for swapped in range(2):
            hh = c if not swapped else 1 - c
            heads = [h for h in range(c * group, (c + 1) * group) if h % 2 == hh]
            pieces, sinks = [], []
            for head in heads:
                blk = q_ref[0, :, (head // 2) * PAIR_W:(head // 2 + 1) * PAIR_W] * ATTN_SCALE
                pieces.append(jnp.where((lane >= hh * HEAD_DIM) & (lane < (hh + 1) * HEAD_DIM),
                                        blk, jnp.zeros_like(blk)))
                sinks.append(jnp.broadcast_to(sink_tab[head:head + 1, :], (tq, LANES)))
            s = _dot_nt(jnp.concatenate(pieces, axis=0), k_by_half[swapped]) + bias
            chains.append((heads, s, jnp.concatenate(sinks, axis=0), v_by_half[swapped]))

    outs = [None] * SWA_Q_HEADS
    for heads, s, sink, v in chains:
        m = jnp.maximum(sink, jnp.max(s, axis=1, keepdims=True))
        p = jnp.exp(s - _lane_tile(m, tk))
        acc = _dot(p.astype(BF16), v)
        den = pltpu.roll(acc, HEAD_DIM, 1) + jnp.exp(sink - m)
        o = acc / den
        for idx, head in enumerate(heads):
            outs[head] = o[idx * tq:(idx + 1) * tq]
    for pp in range(SWA_Q_HEADS // 2):
        o_ref[0, :, pp * PAIR_W:(pp + 1) * PAIR_W] = jnp.where(
            lane < HEAD_DIM, outs[2 * pp], outs[2 * pp + 1]).astype(o_ref.dtype)


def _swa_bias(tq):
    r = np.arange(tq)[:, None]
    c = np.arange(tq + SWA_WINDOW)[None, :]
    tabs = []
    for key_offset in (0, SWA_WINDOW):
        dist = r + key_offset - c
        tabs.append(np.where((dist >= 0) & (dist < SWA_WINDOW), 0.0, NEG_INF))
    return jnp.asarray(np.stack(tabs), F32)


def _swa(q, k, v, sinks, *, batch, seq, tq=SWA_WINDOW):
    W = q.shape[1]
    tk = tq + SWA_WINDOW
    q3 = q.reshape(batch, seq, W)
    k3, v3 = (t.reshape(batch, seq, PAIR_W) for t in (k, v))
    sink_tab = jnp.broadcast_to(sinks.astype(F32)[:, None], (SWA_Q_HEADS, LANES))
    out = pl.pallas_call(
        _swa_kernel,
        grid=(batch, seq // tq),
        in_specs=[pl.BlockSpec((1, tq, W), lambda b, i: (b, i, 0)),
                  pl.BlockSpec((1, seq, PAIR_W), lambda b, i: (b, 0, 0)),
                  pl.BlockSpec((1, seq, PAIR_W), lambda b, i: (b, 0, 0)),
                  pl.BlockSpec((SWA_Q_HEADS, LANES), lambda b, i: (0, 0)),
                  pl.BlockSpec((2, tq, tk), lambda b, i: (0, 0, 0))],
        out_specs=pl.BlockSpec((1, tq, W), lambda b, i: (b, i, 0)),
        out_shape=jax.ShapeDtypeStruct((batch, seq, W), BF16),
        compiler_params=_cparams(("parallel", "parallel")),
        name="swa",
    )(q3, k3, v3, sink_tab, _swa_bias(tq))
    return out.reshape(batch * seq, W)


def _rms_normed(x, gain):
    ms = jnp.mean(x * x, axis=-1, keepdims=True)
    return (x * lax.rsqrt(ms + EPS) * gain).astype(BF16)


def _outproj_kernel(*refs, n_parts):
    parts = refs[:n_parts]
    w_ref, x_ref, g_ref, o_ref, h_ref = refs[n_parts:]
    y = x_ref[...]
    off = 0
    for p_ref in parts:
        kw = p_ref.shape[1]
        y = y + _dot(p_ref[...], w_ref[off:off + kw, :])
        off += kw
    o_ref[...] = y
    h_ref[...] = _rms_normed(y, g_ref[...])


def _outproj(parts, w, x, next_gain, tm=512):
    T, D = x.shape
    in_specs = [pl.BlockSpec((tm, p.shape[1]), lambda i: (i, 0)) for p in parts]
    in_specs += [pl.BlockSpec(w.shape, lambda i: (0, 0)),
                 pl.BlockSpec((tm, D), lambda i: (i, 0)),
                 pl.BlockSpec((1, D), lambda i: (0, 0))]
    row_spec = pl.BlockSpec((tm, D), lambda i: (i, 0))
    return pl.pallas_call(
        functools.partial(_outproj_kernel, n_parts=len(parts)),
        grid=(T // tm,),
        in_specs=in_specs,
        out_specs=[row_spec, row_spec],
        out_shape=[jax.ShapeDtypeStruct((T, D), F32), jax.ShapeDtypeStruct((T, D), BF16)],
        compiler_params=_cparams(("parallel",)),
        name="outproj",
    )(*parts, w, x, next_gain.reshape(1, D).astype(F32))


_CAND_ROWS = 80


def _cand_tables(lanes):
    pos = np.zeros((_CAND_ROWS,), np.float32)
    neg = np.zeros((_CAND_ROWS,), np.float32)
    r = 0
    for a, nb in ((0, 16), (1, 8), (2, 8), (3, 8), (4, 8), (5, 8), (6, 8), (7, 8)):
        for b in range(nb):
            pos[r] = a * PEER_TOPK + b
            neg[r] = 0.0 if (a + 1) * (b + 1) <= PEER_TOPK else -np.inf
            r += 1
    for a in range(8, 16):
        pos[r] = a * PEER_TOPK
        r += 1
    assert r == _CAND_ROWS
    tab = lambda t: jnp.asarray(np.broadcast_to(t[:, None], (_CAND_ROWS, lanes)).copy())
    return tab(pos), tab(neg)


def _batcher_pairs(n):
    pairs, p = [], 1
    while p < n:
        k = p
        while k >= 1:
            for j in range(k % p, n - k, 2 * k):
                for i in range(min(k, n - j - k)):
                    if (i + j) // (2 * p) == (i + j + k) // (2 * p):
                        pairs.append((i + j, i + j + k))
            k //= 2
        p *= 2
    return pairs


_SORT16 = _batcher_pairs(PEER_TOPK)
_N_CAND_PIECES = 10
_SORT10 = [(i, j) for i, j in _SORT16 if j < _N_CAND_PIECES]
_SUBLANES = 8


def _compare_exchange(items, i, j):
    items[i], items[j] = jnp.maximum(items[i], items[j]), jnp.minimum(items[i], items[j])


def _top_sorted(pieces, pairs):
    items = list(pieces)
    for i, j in pairs:
        _compare_exchange(items, i, j)
    n = PEER_TOPK
    items += [jnp.full(items[0].shape, -jnp.inf, F32)] * (n - len(items))
    for shift in (4, 2, 1):
        items = [jnp.maximum(items[i], pltpu.roll(items[n - 1 - i], shift, 0)) for i in range(n)]
        d = n // 2
        while d >= 1:
            for i in range(n):
                if i & d == 0:
                    _compare_exchange(items, i, i + d)
            d //= 2
    return items


def _pieces(x):
    return [x[_SUBLANES * g:_SUBLANES * (g + 1)] for g in range(x.shape[0] // _SUBLANES)]


def _sublane_total(x):
    for shift in (4, 2, 1):
        x = x + pltpu.roll(x, shift, 0)
    return x


def _count_ge(pieces, thr):
    total = jnp.zeros(thr.shape, F32)
    for p in pieces:
        total = total + jnp.where(p >= thr, 1.0, 0.0)
    return _sublane_total(total)


def _route_head_fast(s1, s2):
    p1, p2 = _pieces(s1), _pieces(s2)
    v1 = _top_sorted(p1, _SORT16)
    v2 = _top_sorted(p2, _SORT16)
    sub = lax.broadcasted_iota(jnp.int32, v1[0].shape, 0)

    def spread(vals):
        out = vals[0]
        for r in range(1, _SUBLANES):
            out = jnp.where(sub == r, vals[r], out)
        return out

    v2_lo, v2_hi, v1_hi = spread(v2[:8]), spread(v2[8:]), spread(v1[8:])
    cands = [v1[0] + v2_lo, v1[0] + v2_hi, v1[1] + v2_lo]
    for a in range(2, 8):
        cands.append(jnp.where(sub < PEER_TOPK // (a + 1), v1[a] + v2_lo, -jnp.inf))
    cands.append(v1_hi + v2[0])
    ts = _top_sorted(cands, _SORT10)
    tau = ts[PEER_TOPK - 1]
    z = jnp.exp(ts[0] - ts[0])
    for kk in range(1, PEER_TOPK):
        z = z + jnp.exp(ts[kk] - ts[0])

    tied = (_count_ge(p1, v1[-1]) != float(PEER_TOPK)) | (_count_ge(p2, v2[-1]) != float(PEER_TOPK))
    tied = tied | (_count_ge(cands, tau) != float(PEER_TOPK))
    for b in range(PEER_TOPK - 1):
        tied = tied | (v1[b] == v1[b + 1]) | (v2[b] == v2[b + 1])

    cnt = []
    for a in range(PEER_TOPK):
        c = jnp.zeros(tau.shape, F32)
        for b in range(PEER_TOPK // (a + 1)):
            c = c + jnp.where(v1[a] + v2[b] >= tau, 1.0, 0.0)
        cnt.append(c)
    c1, r2 = [], []
    for x in p1:
        c = jnp.zeros(x.shape, F32)
        for a in range(PEER_TOPK):
            c = jnp.where(x == v1[a], cnt[a], c)
        c1.append(c)
    for x in p2:
        r = jnp.zeros(x.shape, F32)
        for b in range(PEER_TOPK):
            r = r + jnp.where(v2[b] > x, 1.0, 0.0)
        r2.append(r)
    inv_z = 1.0 / z
    e1 = [jnp.exp(x - v1[0]) * inv_z for x in p1]
    e2 = [jnp.exp(x - v2[0]) for x in p2]
    cat = lambda ps: jnp.concatenate(ps, axis=0)
    return (cat(c1), cat(e1), cat(r2), cat(e2)), tied


def _extract_sorted(scores, by_key):
    nk, lanes = scores[0].shape
    kio = lax.broadcasted_iota(jnp.int32, (nk, lanes), 0).astype(F32)
    slot = lax.broadcasted_iota(jnp.int32, (PEER_TOPK, lanes), 0)

    def body(a, carry):
        here = slot == a
        out = []
        for (v, vals, aux), ranked in zip(carry, by_key):
            m = jnp.max(v, axis=0, keepdims=True)
            idx = jnp.min(jnp.where(v == m, kio, float(nk)), axis=0, keepdims=True)
            hit = kio == idx
            aux = jnp.where(hit, jnp.asarray(a, F32), aux) if ranked else jnp.where(here, idx, aux)
            out.append((jnp.where(hit, -jnp.inf, v), jnp.where(here, m, vals), aux))
        return tuple(out)

    small = jnp.zeros((PEER_TOPK, lanes), F32)
    unranked = jnp.full((nk, lanes), float(PEER_TOPK), F32)
    init = tuple((v, small, unranked if ranked else small) for v, ranked in zip(scores, by_key))
    return [(vals, aux) for _, vals, aux in lax.fori_loop(0, PEER_TOPK, body, init)]


def _route_head_exact(s1, s2, pos, neg):
    lanes = s1.shape[1]
    slot = lax.broadcasted_iota(jnp.int32, (PEER_TOPK, lanes), 0)
    kio = lax.broadcasted_iota(jnp.int32, (PEER_N_KEYS, lanes), 0).astype(F32)
    (v1, idx1), (v2, rank2) = _extract_sorted([s1, s2], [False, True])
    blocks = [v1[0:1] + v2[0:8], v1[0:1] + v2[8:16]]
    blocks += [v1[a:a + 1] + v2[0:8] for a in range(1, 8)]
    blocks += [v1[8:16] + v2[0:1]]
    cand = jnp.concatenate(blocks, axis=0) + neg

    def pick(kk, carry):
        cand, chosen, ts = carry
        m = jnp.max(cand, axis=0, keepdims=True)
        first = jnp.min(jnp.where(cand == m, pos, 1e9), axis=0, keepdims=True)
        hit = pos == first
        return (jnp.where(hit, -jnp.inf, cand), jnp.where(hit, 1.0, chosen), jnp.where(slot == kk, m, ts))

    _, chosen, ts = lax.fori_loop(0, PEER_TOPK, pick,
                                  (cand, jnp.zeros_like(cand), jnp.zeros((PEER_TOPK, lanes), F32)))
    z = jnp.sum(jnp.exp(ts - ts[0:1]), axis=0, keepdims=True)
    counts = [jnp.sum(chosen[0:16], axis=0, keepdims=True)]
    counts += [jnp.sum(chosen[8 * a + 8:8 * a + 16], axis=0, keepdims=True) for a in range(1, 8)]
    counts += [chosen[72 + a:73 + a] for a in range(8)]
    c1 = jnp.zeros((PEER_N_KEYS, lanes), F32)
    for a in range(PEER_TOPK):
        c1 = jnp.where(kio == idx1[a:a + 1], counts[a], c1)
    return c1, jnp.exp(s1 - v1[0:1]) / z, rank2, jnp.exp(s2 - v2[0:1])


def _route_kernel(h_ref, wq_ref, keys_ref, pos_ref, neg_ref,
                  c1_ref, e1_ref, r2_ref, e2_ref, qt_ref, sc_ref):
    half = PEER_QUERY_DIM // 2
    qt_ref[...] = _dot_nt(wq_ref[...], h_ref[...]).astype(BF16)

    def store(h, maps):
        c1, e1, r2, e2 = maps
        c1_ref[h] = c1
        e1_ref[h] = e1
        r2_ref[h] = r2.astype(BF16)
        e2_ref[h] = e2.astype(BF16)

    def head_body(h, _):
        r0 = pl.multiple_of(h * PEER_QUERY_DIM, PEER_QUERY_DIM)
        sc_ref[0] = _dot(keys_ref[2 * h], qt_ref[pl.ds(r0, half), :])
        sc_ref[1] = _dot(keys_ref[2 * h + 1], qt_ref[pl.ds(r0 + half, half), :])
        maps, tied = _route_head_fast(sc_ref[0], sc_ref[1])
        any_tied = jnp.max(jnp.where(tied, 1.0, 0.0)) > 0.0

        @pl.when(any_tied)
        def _():
            store(h, _route_head_exact(sc_ref[0], sc_ref[1], pos_ref[...], neg_ref[...]))

        @pl.when(jnp.logical_not(any_tied))
        def _():
            store(h, maps)

        return 0

    lax.fori_loop(0, PEER_HEADS, head_body, 0)


def _peer_route(h2, wq_t, keys, tt=2 * LANES):
    T, D = h2.shape
    pos, neg = _cand_tables(tt)
    stat_spec = pl.BlockSpec((PEER_HEADS, PEER_N_KEYS, tt), lambda i: (0, 0, i))
    stat = lambda dt: jax.ShapeDtypeStruct((PEER_HEADS, PEER_N_KEYS, T), dt)
    return pl.pallas_call(
        _route_kernel,
        grid=(T // tt,),
        in_specs=[pl.BlockSpec((tt, D), lambda i: (i, 0)),
                  pl.BlockSpec(wq_t.shape, lambda i: (0, 0)),
                  pl.BlockSpec(keys.shape, lambda i: (0, 0, 0)),
                  pl.BlockSpec((_CAND_ROWS, tt), lambda i: (0, 0)),
                  pl.BlockSpec((_CAND_ROWS, tt), lambda i: (0, 0))],
        out_specs=[stat_spec] * 4,
        out_shape=[stat(F32), stat(F32), stat(BF16), stat(BF16)],
        scratch_shapes=[pltpu.VMEM((PEER_HEADS * PEER_QUERY_DIM, tt), BF16),
                        pltpu.VMEM((2, PEER_N_KEYS, tt), F32)],
        compiler_params=_cparams(("parallel",)),
        name="peer_route",
    )(h2, wq_t, keys, pos, neg)


_KEY_GROUP = 16
_UNITS = 4
_DRAIN_PIECES = 2


def _build_gated(a_ref, p_ref, c1_ref, e1_ref, r2_ref, e2_ref, key0, g0, ng, lt):
    rep = PEER_N_KEYS // 16

    def rows16(row):
        blk = jnp.broadcast_to(row, (16, LANES)).astype(BF16)
        return jnp.concatenate([blk] * rep, axis=0)

    ls = slice(lt * LANES, (lt + 1) * LANES)
    w = [jnp.zeros((PEER_N_KEYS, LANES), BF16) for _ in range(ng)]
    for h in range(PEER_HEADS):
        c1 = c1_ref[h, pl.ds(key0, _KEY_GROUP), ls]
        e1 = e1_ref[h, pl.ds(key0, _KEY_GROUP), ls]
        r2 = r2_ref[h, :, ls]
        e2 = e2_ref[h, :, ls]
        for g in range(ng):
            thr = rows16(c1[g0 + g:g0 + g + 1])
            gate = rows16(e1[g0 + g:g0 + g + 1])
            w[g] = w[g] + jnp.where(r2 < thr, e2, jnp.zeros_like(e2)) * gate
    for g in range(g0, g0 + ng):
        rs = slice(g * PEER_N_KEYS, (g + 1) * PEER_N_KEYS)
        a = a_ref[rs, ls]
        gelu = 0.5 * a * (1.0 + lax.erf(a * (2.0 ** -0.5)))
        p_ref[rs, ls] = gelu.astype(BF16) * w[g - g0]


def _experts_kernel(h_ref, dn_ref, upt_ref, c1_ref, e1_ref, r2_ref, e2_ref, x_ref, *rest, emit_norm):
    if emit_norm:
        g_ref, o_ref, hn_ref, a_ref, p_ref, acc_ref = rest
    else:
        o_ref, a_ref, p_ref, acc_ref = rest
    te, tt = a_ref.shape
    d_model = acc_ref.shape[0]
    e = pl.program_id(1)
    n_tiles = pl.num_programs(1) - 1
    cur = e % 2
    ng = _KEY_GROUP // _UNITS
    n_lane = tt // LANES
    key0 = pl.multiple_of(jnp.minimum(e, n_tiles - 1) * _KEY_GROUP, _KEY_GROUP)

    def front_mm(u):
        rows = te // _UNITS
        rs = slice(u * rows, (u + 1) * rows)
        a_ref[rs, :] = _dot_nt(dn_ref[rs, :], h_ref[...])

    def back_mm(r):
        rows = d_model // _DRAIN_PIECES
        rs = slice(r * rows, (r + 1) * rows)
        acc_ref[rs, :] += _dot(upt_ref[rs, :], p_ref[1 - cur])

    units_per_drain = _UNITS // _DRAIN_PIECES

    def run(front, back):
        if front:
            front_mm(0)
        for u in range(_UNITS):
            for lt in range(n_lane):
                if front:
                    _build_gated(a_ref, p_ref.at[cur], c1_ref, e1_ref, r2_ref, e2_ref, key0, u * ng, ng, lt)
                if front and lt == 0 and u + 1 < _UNITS:
                    front_mm(u + 1)
                if back and lt == n_lane // 2 and (u + 1) % units_per_drain == 0:
                    back_mm(u // units_per_drain)

    @pl.when(e == 0)
    def _():
        acc_ref[...] = jnp.zeros_like(acc_ref)
        run(True, False)

    @pl.when((e > 0) & (e < n_tiles))
    def _():
        run(True, True)

    @pl.when(e == n_tiles)
    def _():
        run(False, True)
        y = x_ref[...] + acc_ref[...].T
        o_ref[...] = y
        if emit_norm:
            hn_ref[...] = _rms_normed(y, g_ref[...])


def _peer_experts(h2, down, up_t, stats, x, next_gain, tt=512):
    T, D = h2.shape
    E = down.shape[0]
    te = _KEY_GROUP * PEER_N_KEYS
    n_tiles = E // te
    emit_norm = next_gain is not None
    stat_spec = pl.BlockSpec((PEER_HEADS, PEER_N_KEYS, tt), lambda i, e: (0, 0, i))
    row_spec = pl.BlockSpec((tt, D), lambda i, e: (i, 0))
    in_specs = [row_spec,
                pl.BlockSpec((te, D), lambda i, e: (jnp.minimum(e, n_tiles - 1), 0)),
                pl.BlockSpec((None, D, te), lambda i, e: (jnp.maximum(e - 1, 0), 0, 0)),
                stat_spec, stat_spec, stat_spec, stat_spec,
                row_spec]
    args = [h2, down, up_t, *stats, x]
    out_specs = [row_spec]
    out_shape = [jax.ShapeDtypeStruct((T, D), F32)]
    if emit_norm:
        in_specs.append(pl.BlockSpec((1, D), lambda i, e: (0, 0)))
        args.append(next_gain.reshape(1, D).astype(F32))
        out_specs.append(row_spec)
        out_shape.append(jax.ShapeDtypeStruct((T, D), BF16))
    res = pl.pallas_call(
        functools.partial(_experts_kernel, emit_norm=emit_norm),
        grid=(T // tt, n_tiles + 1),
        in_specs=in_specs, out_specs=out_specs, out_shape=out_shape,
        scratch_shapes=[pltpu.VMEM((te, tt), F32),
                        pltpu.VMEM((2, te, tt), BF16),
                        pltpu.VMEM((D, tt), F32)],
        compiler_params=_cparams(("parallel", "arbitrary"), VMEM_LIMIT_EXPERTS),
        name="peer_experts",
    )(*args)
    return (res[0], res[1]) if emit_norm else (res[0], None)


def _peer_layer(x, h2, next_gain, w_query, sub_keys, down, up):
    keys = sub_keys.reshape(PEER_HEADS * 2, PEER_N_KEYS, PEER_QUERY_DIM // 2).astype(BF16)
    stats = _peer_route(h2, w_query.T.astype(BF16), keys)
    te = _KEY_GROUP * PEER_N_KEYS
    up_t = up.reshape(up.shape[0] // te, te, up.shape[1]).transpose(0, 2, 1).astype(BF16)
    return _peer_experts(h2, down.astype(BF16), up_t, stats, x, next_gain)


def _tile_heads(g, n):
    return jnp.tile(g.astype(F32), n)


def _even_mixer(x, h, ffn_gain, w_in, f_bias, qn_a, kn_a, qn_b, kn_b, w_out, tabs, *, batch, seq):
    w = w_in.astype(BF16)
    o_qa, o_ka, o_va, o_qb, o_kb, o_vb, o_gb, o_fb = (
        0, A_W, 2 * A_W, 3 * A_W, 3 * A_W + B_W, 3 * A_W + 2 * B_W, 3 * A_W + 3 * B_W, 3 * A_W + 4 * B_W)
    gain_a = jnp.concatenate([_tile_heads(qn_a, MOBA_HEADS), _tile_heads(kn_a, MOBA_HEADS)])
    qk_a, km = _proj(h, w[:, o_qa:o_va], seq=seq, tn=A_W, tm=MOBA_BLOCK, gain=gain_a,
                     rope_tabs=tabs, kmean=True)
    nb = seq // MOBA_BLOCK
    kmean = km.reshape(batch, nb, 2 * A_W)[:, :, A_W:]
    kmean = jnp.pad(kmean, ((0, 0), (0, LANES - nb), (0, 0))).astype(BF16)
    gain_b = jnp.concatenate([_tile_heads(qn_b, FOX_HEADS), _tile_heads(kn_b, FOX_HEADS)])
    qk_b = _proj(h, w[:, o_qb:o_vb], seq=seq, tn=B_W, gain=gain_b)
    w_plain = jnp.concatenate([w[:, o_va:o_qb], w[:, o_vb:o_fb]], axis=1)
    vvg = _proj(h, w_plain, seq=seq, tn=A_W)
    va, vb, gb = vvg[:, :A_W], vvg[:, A_W:A_W + B_W], vvg[:, A_W + B_W:]
    frow = _fox_gates(h, w[:, o_fb:].T, f_bias, batch=batch, seq=seq)
    oa = _moba(qk_a[:, :A_W], qk_a[:, A_W:], va, kmean, batch=batch, seq=seq)
    ob = _fox(qk_b[:, :B_W], qk_b[:, B_W:], vb, gb, frow, batch=batch, seq=seq)
    return _outproj([oa, ob], w_out.astype(BF16), x, ffn_gain)


def _odd_mixer(x, h, ffn_gain, w_in, qn, kn, sinks, w_out, tabs, *, batch, seq):
    w = w_in.astype(BF16)
    qw = SWA_Q_HEADS * HEAD_DIM
    kw = SWA_KV_HEADS * HEAD_DIM
    q = _proj(h, w[:, :qw], seq=seq, tn=512, gain=_tile_heads(qn, SWA_Q_HEADS), rope_tabs=tabs)
    k = _proj(h, w[:, qw:qw + kw], seq=seq, tn=kw, gain=_tile_heads(kn, SWA_KV_HEADS), rope_tabs=tabs)
    v = _proj(h, w[:, qw + kw:], seq=seq, tn=kw)
    o = _swa(q, k, v, sinks, batch=batch, seq=seq)
    return _outproj([o], w_out.astype(BF16), x, ffn_gain)


def kernel(x, attn_norm, ffn_norm, ev_w_in, ev_forget_bias, ev_q_norm_a, ev_k_norm_a, ev_q_norm_b,
           ev_k_norm_b, ev_w_out, od_w_in, od_q_norm, od_k_norm, od_sinks, od_w_out,
           peer_w_query, peer_sub_keys, peer_down, peer_up):
    batch, seq, d_model = x.shape
    depth = attn_norm.shape[0]
    tabs = _rope_tables(seq)
    xt = x.reshape(batch * seq, d_model)
    h = _rmsnorm(xt, attn_norm[0])
    for l in range(depth):
        i = l // 2
        if l % 2 == 0:
            xt, h2 = _even_mixer(xt, h, ffn_norm[l], ev_w_in[i], ev_forget_bias[i], ev_q_norm_a[i],
                                 ev_k_norm_a[i], ev_q_norm_b[i], ev_k_norm_b[i], ev_w_out[i], tabs,
                                 batch=batch, seq=seq)
        else:
            xt, h2 = _odd_mixer(xt, h, ffn_norm[l], od_w_in[i], od_q_norm[i], od_k_norm[i], od_sinks[i],
                                od_w_out[i], tabs, batch=batch, seq=seq)
        next_gain = attn_norm[l + 1] if l + 1 < depth else None
        xt, h = _peer_layer(xt, h2, next_gain, peer_w_query[l], peer_sub_keys[l], peer_down[l], peer_up[l])
    return xt.reshape(batch, seq, d_model)
```

```python
import functools

import numpy as np
import jax
import jax.numpy as jnp
from jax import lax
from jax.experimental import pallas as pl
from jax.experimental.pallas import tpu as pltpu

F32 = jnp.float32
BF16 = jnp.bfloat16

HEAD_DIM = 64
ROT_DIM = HEAD_DIM // 4
ROPE_THETA = 500000.0
ATTN_SCALE = HEAD_DIM ** -0.5
EPS = 1e-6
NEG_INF = -1e30

MOBA_HEADS = 8
FOX_HEADS = 8
MOBA_BLOCK = 256
MOBA_TOPK = 3
A_W = MOBA_HEADS * HEAD_DIM
B_W = FOX_HEADS * HEAD_DIM

SWA_Q_HEADS = 16
SWA_KV_HEADS = 2
SWA_WINDOW = 128

PEER_HEADS = 8
PEER_N_KEYS = 128
PEER_TOPK = 16
PEER_QUERY_DIM = 128

LANES = 128
PAIR_W = 2 * HEAD_DIM
VMEM_LIMIT = 48 * 1024 * 1024
VMEM_LIMIT_EXPERTS = 58 * 1024 * 1024


def _cparams(sem, vmem_limit=VMEM_LIMIT):
    return pltpu.CompilerParams(dimension_semantics=sem, vmem_limit_bytes=vmem_limit)


def _dot_nt(a, b):
    return lax.dot_general(a, b, (((1,), (1,)), ((), ())), preferred_element_type=F32)


def _dot(a, b):
    return jnp.dot(a, b, preferred_element_type=F32)


def _split2(x):
    h1 = x.astype(BF16)
    return h1, (x - h1.astype(F32)).astype(BF16)


def _split3(x):
    h1 = x.astype(BF16)
    r1 = x - h1.astype(F32)
    h2 = r1.astype(BF16)
    h3 = (r1 - h2.astype(F32)).astype(BF16)
    return h1, h2, h3


def _rmsnorm_kernel(x_ref, g_ref, o_ref):
    x = x_ref[...]
    ms = jnp.mean(x * x, axis=-1, keepdims=True)
    o_ref[...] = (x * lax.rsqrt(ms + EPS) * g_ref[...]).astype(o_ref.dtype)


def _rmsnorm(x, gain, tm=512):
    T, D = x.shape
    return pl.pallas_call(
        _rmsnorm_kernel,
        grid=(T // tm,),
        in_specs=[pl.BlockSpec((tm, D), lambda i: (i, 0)),
                  pl.BlockSpec((1, D), lambda i: (0, 0))],
        out_specs=pl.BlockSpec((tm, D), lambda i: (i, 0)),
        out_shape=jax.ShapeDtypeStruct((T, D), BF16),
        compiler_params=_cparams(("parallel",)),
        name="rmsnorm",
    )(x, gain.reshape(1, D))


def _proj_kernel(*refs, norm, rope, kmean, tn):
    it = iter(refs)
    h_ref, w_ref = next(it), next(it)
    gain_ref = next(it) if norm else None
    bd_ref = next(it) if norm else None
    if rope:
        c_ref, sa_ref, sb_ref = next(it), next(it), next(it)
    o_ref = next(it)
    km_ref = next(it) if kmean else None

    y = _dot(h_ref[...], w_ref[...])
    if norm:
        y2 = y * y
        bd = bd_ref[...]
        cols = []
        tm = y.shape[0]
        for c in range(tn // LANES):
            h1, h2 = _split2(y2[:, c * LANES:(c + 1) * LANES])
            t = _dot(jnp.concatenate([h1, h2], axis=0), bd)
            cols.append(t[:tm] + t[tm:])
        ms = cols[0] if len(cols) == 1 else jnp.concatenate(cols, axis=1)
        y = y * lax.rsqrt(ms + EPS) * gain_ref[...]
    if rope:
        rep = tn // LANES
        tile = (lambda t: t) if rep == 1 else (lambda t: jnp.concatenate([t] * rep, axis=1))
        y = (y * tile(c_ref[...])
             + pltpu.roll(y, tn - ROT_DIM // 2, 1) * tile(sa_ref[...])
             + pltpu.roll(y, ROT_DIM // 2, 1) * tile(sb_ref[...]))
    o_ref[...] = y.astype(o_ref.dtype)
    if kmean:
        for r in range(km_ref.shape[0]):
            km_ref[r] = jnp.mean(y[r * MOBA_BLOCK:(r + 1) * MOBA_BLOCK], axis=0, keepdims=True)


def _proj(h, w, *, seq, tn, tm=512, gain=None, rope_tabs=None, kmean=False):
    T, D = h.shape
    N = w.shape[1]
    norm = gain is not None
    rope = rope_tabs is not None
    nseq = seq // tm
    in_specs = [pl.BlockSpec((tm, D), lambda i, j: (i, 0)),
                pl.BlockSpec((D, tn), lambda i, j: (0, j))]
    args = [h, w]
    if norm:
        bd = np.kron(np.eye(LANES // HEAD_DIM), np.ones((HEAD_DIM, HEAD_DIM))) / HEAD_DIM
        in_specs += [pl.BlockSpec((1, tn), lambda i, j: (0, j)),
                     pl.BlockSpec((LANES, LANES), lambda i, j: (0, 0))]
        args += [gain.reshape(1, N).astype(F32), jnp.asarray(bd, BF16)]
    if rope:
        in_specs += [pl.BlockSpec((tm, LANES), lambda i, j: (i % nseq, 0))] * 3
        args += list(rope_tabs)
    out_specs = [pl.BlockSpec((tm, tn), lambda i, j: (i, j))]
    out_shape = [jax.ShapeDtypeStruct((T, N), BF16)]
    if kmean:
        out_specs.append(pl.BlockSpec((tm // MOBA_BLOCK, 1, tn), lambda i, j: (i, 0, j)))
        out_shape.append(jax.ShapeDtypeStruct((T // MOBA_BLOCK, 1, N), F32))
    res = pl.pallas_call(
        functools.partial(_proj_kernel, norm=norm, rope=rope, kmean=kmean, tn=tn),
        grid=(T // tm, N // tn),
        in_specs=in_specs, out_specs=out_specs, out_shape=out_shape,
        compiler_params=_cparams(("parallel", "parallel")),
        name="proj",
    )(*args)
    return res if kmean else res[0]


def _rope_tables(seq):
    half = ROT_DIM // 2
    inv_freq = jnp.power(ROPE_THETA, -jnp.arange(0, ROT_DIM, 2, dtype=F32) / ROT_DIM)
    ang = jnp.arange(seq, dtype=F32)[:, None] * inv_freq[None, :]
    cos, sin = jnp.cos(ang), jnp.sin(ang)
    one = jnp.ones((seq, HEAD_DIM - ROT_DIM), F32)
    zero = jnp.zeros((seq, HEAD_DIM - ROT_DIM), F32)
    z8 = jnp.zeros((seq, half), F32)
    c = jnp.concatenate([cos, cos, one], axis=1)
    sa = jnp.concatenate([-sin, z8, zero], axis=1)
    sb = jnp.concatenate([z8, sin, zero], axis=1)
    rep = LANES // HEAD_DIM
    return tuple(jnp.concatenate([t] * rep, axis=1) for t in (c, sa, sb))


def _gates_kernel(h_ref, wf_ref, b_ref, tri_ref, o_ref, carry_ref):
    @pl.when(pl.program_id(1) == 0)
    def _():
        carry_ref[...] = jnp.zeros_like(carry_ref)

    z = _dot_nt(wf_ref[...], h_ref[...]) + b_ref[...][:, :1]
    lf = jnp.minimum(z, 0.0) - jnp.log1p(jnp.exp(-jnp.abs(z)))
    tri = tri_ref[...]
    h1, h2, h3 = _split3(lf)
    cs = _dot(h1, tri) + _dot(h2, tri) + _dot(h3, tri) + carry_ref[...][:, :1]
    o_ref[0] = cs
    carry_ref[...] = jnp.broadcast_to(cs[:, -1:], carry_ref.shape)


def _fox_gates(h, wf_t, bias, *, batch, seq, tm=512):
    T, D = h.shape
    nh = wf_t.shape[0]
    nseq = seq // tm
    tri = jnp.asarray(np.triu(np.ones((tm, tm))), BF16)
    return pl.pallas_call(
        _gates_kernel,
        grid=(batch, nseq),
        in_specs=[pl.BlockSpec((tm, D), lambda b, s: (b * nseq + s, 0)),
                  pl.BlockSpec((nh, D), lambda b, s: (0, 0)),
                  pl.BlockSpec((nh, LANES), lambda b, s: (0, 0)),
                  pl.BlockSpec((tm, tm), lambda b, s: (0, 0))],
        out_specs=pl.BlockSpec((1, nh, tm), lambda b, s: (b, 0, s)),
        out_shape=jax.ShapeDtypeStruct((batch, nh, seq), F32),
        scratch_shapes=[pltpu.VMEM((nh, LANES), F32)],
        compiler_params=_cparams(("parallel", "arbitrary")),
        name="fox_gates",
    )(h, wf_t, jnp.broadcast_to(bias.astype(F32)[:, None], (nh, LANES)), tri)


def _lane_tile(x, width):
    rep = width // LANES
    return x if rep == 1 else jnp.concatenate([x] * rep, axis=1)


def _flash_init(m_ref, acc_ref):
    m_ref[...] = jnp.full(m_ref.shape, NEG_INF, F32)
    acc_ref[...] = jnp.zeros(acc_ref.shape, F32)


def _head_values(v):
    lane = lax.broadcasted_iota(jnp.int32, v.shape, 1)
    return [jnp.where((lane >= hh * HEAD_DIM) & (lane < (hh + 1) * HEAD_DIM), v, jnp.ones_like(v))
            for hh in range(2)]


def _flash_update(slot, s, v, m_ref, acc_ref):
    tk = s.shape[1]
    m_prev = m_ref[slot]
    m_new = jnp.maximum(m_prev, jnp.max(s, axis=1, keepdims=True))
    alpha = jnp.exp(m_prev - m_new)
    p = jnp.exp(s - _lane_tile(m_new, tk))
    acc_ref[slot] = alpha * acc_ref[slot] + _dot(p.astype(BF16), v)
    m_ref[slot] = m_new


def _flash_finish(lane, acc_ref):
    outs = []
    for hh in range(2):
        acc = acc_ref[hh]
        den = (1 - hh) * HEAD_DIM
        outs.append(acc / acc[:, den:den + 1])
    return jnp.where(lane < HEAD_DIM, outs[0], outs[1])


def _flash_tiles(qi, tk, logits, values, m_ref, acc_ref):
    def absorb(s, off):
        vh = values(off)
        for hh in range(2):
            _flash_update(hh, s[hh], vh[hh], m_ref, acc_ref)

    def pair(off_a, off_b, b_diagonal):
        sa = logits(off_a, False)
        sb = logits(off_b, b_diagonal)
        absorb(sa, off_a)
        absorb(sb, off_b)

    def body(jj, carry):
        off = pl.multiple_of(2 * jj * tk, 2 * tk)
        pair(off, off + tk, False)
        return carry

    lax.fori_loop(0, qi // 2, body, 0)
    diag = pl.multiple_of(qi * tk, tk)

    @pl.when(qi % 2 == 1)
    def _():
        pair(diag - tk, diag, True)

    @pl.when(qi % 2 == 0)
    def _():
        absorb(logits(diag, True), diag)


def _head_queries(q, lane):
    qs = q * ATTN_SCALE
    return [jnp.where((lane >= hh * HEAD_DIM) & (lane < (hh + 1) * HEAD_DIM), qs, jnp.zeros_like(qs))
            for hh in range(2)]


def _moba_kernel(q_ref, k_ref, v_ref, km_ref, o_ref, m_ref, acc_ref):
    tq = q_ref.shape[1]
    tk = tq
    qi = pl.program_id(2)
    lane = lax.broadcasted_iota(jnp.int32, (tq, LANES), 1)
    lane_f = lane.astype(F32)
    rowv = lax.broadcasted_iota(jnp.int32, (tq, LANES), 0)
    row_blk = 2 * qi + (rowv >= MOBA_BLOCK).astype(jnp.int32)
    row = lax.broadcasted_iota(jnp.int32, (tq, tk), 0)
    col = lax.broadcasted_iota(jnp.int32, (tq, tk), 1)
    qh = _head_queries(q_ref[0], lane)
    _flash_init(m_ref, acc_ref)

    sels = []
    for hh in range(2):
        gate = _dot_nt(qh[hh], km_ref[0])
        gate = jnp.where(lane < row_blk, gate, -jnp.inf)
        sel = jnp.zeros((tq, LANES), F32)
        for _ in range(MOBA_TOPK):
            m = jnp.max(gate, axis=1, keepdims=True)
            idx = jnp.min(jnp.where(gate == m, lane_f, float(LANES)), axis=1, keepdims=True)
            hit = lane_f == idx
            sel = jnp.where(hit & (m > -jnp.inf), 1.0, sel)
            gate = jnp.where(hit, -jnp.inf, gate)
        sels.append(sel)

    def chosen(sel, blk):
        return jnp.max(jnp.where(lane == blk, sel, 0.0), axis=1, keepdims=True) > 0.0

    def logits(off, diagonal):
        kj = k_ref[0, pl.ds(off, tk), :]
        out = []
        for hh in range(2):
            s = _dot_nt(qh[hh], kj)
            if diagonal:
                visible = (col >= MOBA_BLOCK) | (row < MOBA_BLOCK) | chosen(sels[hh], 2 * qi)
                s = jnp.where((col <= row) & visible, s, NEG_INF)
            else:
                blk = 2 * (off // tk)
                s = jnp.concatenate(
                    [jnp.where(chosen(sels[hh], blk), s[:, :MOBA_BLOCK], NEG_INF),
                     jnp.where(chosen(sels[hh], blk + 1), s[:, MOBA_BLOCK:], NEG_INF)], axis=1)
            out.append(s)
        return out

    def values(off):
        return _head_values(v_ref[0, pl.ds(off, tk), :])

    _flash_tiles(qi, tk, logits, values, m_ref, acc_ref)
    o_ref[0] = _flash_finish(lane, acc_ref).astype(o_ref.dtype)


def _flash_scratch(tq):
    return [pltpu.VMEM((2, tq, LANES), F32)] * 2


def _moba(q, k, v, kmean, *, batch, seq):
    W = q.shape[1]
    tq = 2 * MOBA_BLOCK
    q3, k3, v3 = (t.reshape(batch, seq, W) for t in (q, k, v))
    out = pl.pallas_call(
        _moba_kernel,
        grid=(batch, W // PAIR_W, seq // tq),
        in_specs=[pl.BlockSpec((1, tq, PAIR_W), lambda b, p, i: (b, i, p)),
                  pl.BlockSpec((1, seq, PAIR_W), lambda b, p, i: (b, 0, p)),
                  pl.BlockSpec((1, seq, PAIR_W), lambda b, p, i: (b, 0, p)),
                  pl.BlockSpec((1, LANES, PAIR_W), lambda b, p, i: (b, 0, p))],
        out_specs=pl.BlockSpec((1, tq, PAIR_W), lambda b, p, i: (b, i, p)),
        out_shape=jax.ShapeDtypeStruct((batch, seq, W), BF16),
        scratch_shapes=_flash_scratch(tq),
        compiler_params=_cparams(("parallel", "parallel", "parallel")),
        name="moba",
    )(q3, k3, v3, kmean)
    return out.reshape(batch * seq, W)


def _fox_kernel(q_ref, k_ref, v_ref, g_ref, frow_ref, o_ref, m_ref, acc_ref):
    tq = q_ref.shape[1]
    tk = tq
    pr = pl.program_id(1)
    qi = pl.program_id(2)
    lane = lax.broadcasted_iota(jnp.int32, (tq, LANES), 1)
    row = lax.broadcasted_iota(jnp.int32, (tq, tk), 0)
    col = lax.broadcasted_iota(jnp.int32, (tq, tk), 1)
    qh = _head_queries(q_ref[0], lane)
    _flash_init(m_ref, acc_ref)

    def key_gates(off, width):
        f_all = frow_ref[0, :, pl.ds(off, width)]
        sub = lax.broadcasted_iota(jnp.int32, f_all.shape, 0)
        return [jnp.sum(jnp.where(sub == 2 * pr + hh, f_all, 0.0), axis=0, keepdims=True) for hh in range(2)]

    f_ref = [f[:, :1] for f in key_gates(pl.multiple_of(qi * tq, tq), LANES)]

    def logits(off, diagonal):
        kj = k_ref[0, pl.ds(off, tk), :]
        fk = key_gates(off, tk)
        out = []
        for hh in range(2):
            s = _dot_nt(qh[hh], kj) - (fk[hh] - f_ref[hh])
            out.append(jnp.where(col <= row, s, NEG_INF) if diagonal else s)
        return out

    def values(off):
        return _head_values(v_ref[0, pl.ds(off, tk), :])

    _flash_tiles(qi, tk, logits, values, m_ref, acc_ref)
    o = _flash_finish(lane, acc_ref)
    o_ref[0] = (o * jax.nn.sigmoid(g_ref[0].astype(F32))).astype(o_ref.dtype)


def _fox(q, k, v, g, frow, *, batch, seq, tq=512):
    W = q.shape[1]
    nh = frow.shape[1]
    q3, k3, v3, g3 = (t.reshape(batch, seq, W) for t in (q, k, v, g))
    out = pl.pallas_call(
        _fox_kernel,
        grid=(batch, W // PAIR_W, seq // tq),
        in_specs=[pl.BlockSpec((1, tq, PAIR_W), lambda b, p, i: (b, i, p)),
                  pl.BlockSpec((1, seq, PAIR_W), lambda b, p, i: (b, 0, p)),
                  pl.BlockSpec((1, seq, PAIR_W), lambda b, p, i: (b, 0, p)),
                  pl.BlockSpec((1, tq, PAIR_W), lambda b, p, i: (b, i, p)),
                  pl.BlockSpec((1, nh, seq), lambda b, p, i: (b, 0, 0))],
        out_specs=pl.BlockSpec((1, tq, PAIR_W), lambda b, p, i: (b, i, p)),
        out_shape=jax.ShapeDtypeStruct((batch, seq, W), BF16),
        scratch_shapes=_flash_scratch(tq),
        compiler_params=_cparams(("parallel", "parallel", "parallel")),
        name="fox",
    )(q3, k3, v3, g3, frow)
    return out.reshape(batch * seq, W)


def _swa_kernel(q_ref, k_ref, v_ref, sink_ref, bias_ref, o_ref):
    tq = q_ref.shape[1]
    qi = pl.program_id(1)
    group = SWA_Q_HEADS // SWA_KV_HEADS
    tk = tq + SWA_WINDOW
    lane = lax.broadcasted_iota(jnp.int32, (tq, LANES), 1)
    kstart = pl.multiple_of(jnp.maximum(qi * tq - SWA_WINDOW, 0), SWA_WINDOW)
    k = k_ref[0, pl.ds(kstart, tk), :]
    vh = _head_values(v_ref[0, pl.ds(kstart, tk), :])
    swap = lambda t: pltpu.roll(t.astype(F32), HEAD_DIM, 1).astype(BF16)
    k_by_half = [k, swap(k)]
    bias = bias_ref[jnp.minimum(qi, 1)]
    bias = jnp.concatenate([bias] * (group // 2), axis=0)
    sink_tab = sink_ref[...]

    chains = []
    for c in range(SWA_KV_HEADS):
        v_by_half = [vh[c], swap(vh[c])]
        for swapped in range(2):
            hh = c if not swapped else 1 - c
            heads = [h for h in range(c * group, (c + 1) * group) if h % 2 == hh]
            pieces, sinks = [], []
            for head in heads:
                blk = q_ref[0, :, (head // 2) * PAIR_W:(head // 2 + 1) * PAIR_W] * ATTN_SCALE
                pieces.append(jnp.where((lane >= hh * HEAD_DIM) & (lane < (hh + 1) * HEAD_DIM),
                                        blk, jnp.zeros_like(blk)))
                sinks.append(jnp.broadcast_to(sink_tab[head:head + 1, :], (tq, LANES)))
            s = _dot_nt(jnp.concatenate(pieces, axis=0), k_by_half[swapped]) + bias
            chains.append((heads, s, jnp.concatenate(sinks, axis=0), v_by_half[swapped]))

    outs = [None] * SWA_Q_HEADS
    for heads, s, sink, v in chains:
        m = jnp.maximum(sink, jnp.max(s, axis=1, keepdims=True))
        p = jnp.exp(s - _lane_tile(m, tk))
        acc = _dot(p.astype(BF16), v)
        den = pltpu.roll(acc, HEAD_DIM, 1) + jnp.exp(sink - m)
        o = acc / den
        for idx, head in enumerate(heads):
            outs[head] = o[idx * tq:(idx + 1) * tq]
    for pp in range(SWA_Q_HEADS // 2):
        o_ref[0, :, pp * PAIR_W:(pp + 1) * PAIR_W] = jnp.where(
            lane < HEAD_DIM, outs[2 * pp], outs[2 * pp + 1]).astype(o_ref.dtype)


def _swa_bias(tq):
    r = np.arange(tq)[:, None]
    c = np.arange(tq + SWA_WINDOW)[None, :]
    tabs = []
    for key_offset in (0, SWA_WINDOW):
        dist = r + key_offset - c
        tabs.append(np.where((dist >= 0) & (dist < SWA_WINDOW), 0.0, NEG_INF))
    return jnp.asarray(np.stack(tabs), F32)


def _swa(q, k, v, sinks, *, batch, seq, tq=SWA_WINDOW):
    W = q.shape[1]
    tk = tq + SWA_WINDOW
    q3 = q.reshape(batch, seq, W)
    k3, v3 = (t.reshape(batch, seq, PAIR_W) for t in (k, v))
    sink_tab = jnp.broadcast_to(sinks.astype(F32)[:, None], (SWA_Q_HEADS, LANES))
    out = pl.pallas_call(
        _swa_kernel,
        grid=(batch, seq // tq),
        in_specs=[pl.BlockSpec((1, tq, W), lambda b, i: (b, i, 0)),
                  pl.BlockSpec((1, seq, PAIR_W), lambda b, i: (b, 0, 0)),
                  pl.BlockSpec((1, seq, PAIR_W), lambda b, i: (b, 0, 0)),
                  pl.BlockSpec((SWA_Q_HEADS, LANES), lambda b, i: (0, 0)),
                  pl.BlockSpec((2, tq, tk), lambda b, i: (0, 0, 0))],
        out_specs=pl.BlockSpec((1, tq, W), lambda b, i: (b, i, 0)),
        out_shape=jax.ShapeDtypeStruct((batch, seq, W), BF16),
        compiler_params=_cparams(("parallel", "parallel")),
        name="swa",
    )(q3, k3, v3, sink_tab, _swa_bias(tq))
    return out.reshape(batch * seq, W)


def _rms_normed(x, gain):
    ms = jnp.mean(x * x, axis=-1, keepdims=True)
    return (x * lax.rsqrt(ms + EPS) * gain).astype(BF16)


def _outproj_kernel(*refs, n_parts):
    parts = refs[:n_parts]
    w_ref, x_ref, g_ref, o_ref, h_ref = refs[n_parts:]
    y = x_ref[...]
    off = 0
    for p_ref in parts:
        kw = p_ref.shape[1]
        y = y + _dot(p_ref[...], w_ref[off:off + kw, :])
        off += kw
    o_ref[...] = y
    h_ref[...] = _rms_normed(y, g_ref[...])


def _outproj(parts, w, x, next_gain, tm=512):
    T, D = x.shape
    in_specs = [pl.BlockSpec((tm, p.shape[1]), lambda i: (i, 0)) for p in parts]
    in_specs += [pl.BlockSpec(w.shape, lambda i: (0, 0)),
                 pl.BlockSpec((tm, D), lambda i: (i, 0)),
                 pl.BlockSpec((1, D), lambda i: (0, 0))]
    row_spec = pl.BlockSpec((tm, D), lambda i: (i, 0))
    return pl.pallas_call(
        functools.partial(_outproj_kernel, n_parts=len(parts)),
        grid=(T // tm,),
        in_specs=in_specs,
        out_specs=[row_spec, row_spec],
        out_shape=[jax.ShapeDtypeStruct((T, D), F32), jax.ShapeDtypeStruct((T, D), BF16)],
        compiler_params=_cparams(("parallel",)),
        name="outproj",
    )(*parts, w, x, next_gain.reshape(1, D).astype(F32))


_CAND_ROWS = 80


def _cand_tables(lanes):
    pos = np.zeros((_CAND_ROWS,), np.float32)
    neg = np.zeros((_CAND_ROWS,), np.float32)
    r = 0
    for a, nb in ((0, 16), (1, 8), (2, 8), (3, 8), (4, 8), (5, 8), (6, 8), (7, 8)):
        for b in range(nb):
            pos[r] = a * PEER_TOPK + b
            neg[r] = 0.0 if (a + 1) * (b + 1) <= PEER_TOPK else -np.inf
            r += 1
    for a in range(8, 16):
        pos[r] = a * PEER_TOPK
        r += 1
    assert r == _CAND_ROWS
    tab = lambda t: jnp.asarray(np.broadcast_to(t[:, None], (_CAND_ROWS, lanes)).copy())
    return tab(pos), tab(neg)


def _batcher_pairs(n):
    pairs, p = [], 1
    while p < n:
        k = p
        while k >= 1:
            for j in range(k % p, n - k, 2 * k):
                for i in range(min(k, n - j - k)):
                    if (i + j) // (2 * p) == (i + j + k) // (2 * p):
                        pairs.append((i + j, i + j + k))
            k //= 2
        p *= 2
    return pairs


_SORT16 = _batcher_pairs(PEER_TOPK)
_N_CAND_PIECES = 10
_SORT10 = [(i, j) for i, j in _SORT16 if j < _N_CAND_PIECES]
_SUBLANES = 8


def _compare_exchange(items, i, j):
    items[i], items[j] = jnp.maximum(items[i], items[j]), jnp.minimum(items[i], items[j])


def _top_sorted(pieces, pairs):
    items = list(pieces)
    for i, j in pairs:
        _compare_exchange(items, i, j)
    n = PEER_TOPK
    items += [jnp.full(items[0].shape, -jnp.inf, F32)] * (n - len(items))
    for shift in (4, 2, 1):
        items = [jnp.maximum(items[i], pltpu.roll(items[n - 1 - i], shift, 0)) for i in range(n)]
        d = n // 2
        while d >= 1:
            for i in range(n):
                if i & d == 0:
                    _compare_exchange(items, i, i + d)
            d //= 2
    return items


def _pieces(x):
    return [x[_SUBLANES * g:_SUBLANES * (g + 1)] for g in range(x.shape[0] // _SUBLANES)]


def _sublane_total(x):
    for shift in (4, 2, 1):
        x = x + pltpu.roll(x, shift, 0)
    return x


def _count_ge(pieces, thr):
    total = jnp.zeros(thr.shape, F32)
    for p in pieces:
        total = total + jnp.where(p >= thr, 1.0, 0.0)
    return _sublane_total(total)


def _route_head_fast(s1, s2):
    p1, p2 = _pieces(s1), _pieces(s2)
    v1 = _top_sorted(p1, _SORT16)
    v2 = _top_sorted(p2, _SORT16)
    sub = lax.broadcasted_iota(jnp.int32, v1[0].shape, 0)

    def spread(vals):
        out = vals[0]
        for r in range(1, _SUBLANES):
            out = jnp.where(sub == r, vals[r], out)
        return out

    v2_lo, v2_hi, v1_hi = spread(v2[:8]), spread(v2[8:]), spread(v1[8:])
    cands = [v1[0] + v2_lo, v1[0] + v2_hi, v1[1] + v2_lo]
    for a in range(2, 8):
        cands.append(jnp.where(sub < PEER_TOPK // (a + 1), v1[a] + v2_lo, -jnp.inf))
    cands.append(v1_hi + v2[0])
    ts = _top_sorted(cands, _SORT10)
    tau = ts[PEER_TOPK - 1]
    z = jnp.exp(ts[0] - ts[0])
    for kk in range(1, PEER_TOPK):
        z = z + jnp.exp(ts[kk] - ts[0])

    tied = (_count_ge(p1, v1[-1]) != float(PEER_TOPK)) | (_count_ge(p2, v2[-1]) != float(PEER_TOPK))
    tied = tied | (_count_ge(cands, tau) != float(PEER_TOPK))
    for b in range(PEER_TOPK - 1):
        tied = tied | (v1[b] == v1[b + 1]) | (v2[b] == v2[b + 1])

    cnt = []
    for a in range(PEER_TOPK):
        c = jnp.zeros(tau.shape, F32)
        for b in range(PEER_TOPK // (a + 1)):
            c = c + jnp.where(v1[a] + v2[b] >= tau, 1.0, 0.0)
        cnt.append(c)
    c1, r2 = [], []
    for x in p1:
        c = jnp.zeros(x.shape, F32)
        for a in range(PEER_TOPK):
            c = jnp.where(x == v1[a], cnt[a], c)
        c1.append(c)
    for x in p2:
        r = jnp.zeros(x.shape, F32)
        for b in range(PEER_TOPK):
            r = r + jnp.where(v2[b] > x, 1.0, 0.0)
        r2.append(r)
    inv_z = 1.0 / z
    e1 = [jnp.exp(x - v1[0]) * inv_z for x in p1]
    e2 = [jnp.exp(x - v2[0]) for x in p2]
    cat = lambda ps: jnp.concatenate(ps, axis=0)
    return (cat(c1), cat(e1), cat(r2), cat(e2)), tied


def _extract_sorted(scores, by_key):
    nk, lanes = scores[0].shape
    kio = lax.broadcasted_iota(jnp.int32, (nk, lanes), 0).astype(F32)
    slot = lax.broadcasted_iota(jnp.int32, (PEER_TOPK, lanes), 0)

    def body(a, carry):
        here = slot == a
        out = []
        for (v, vals, aux), ranked in zip(carry, by_key):
            m = jnp.max(v, axis=0, keepdims=True)
            idx = jnp.min(jnp.where(v == m, kio, float(nk)), axis=0, keepdims=True)
            hit = kio == idx
            aux = jnp.where(hit, jnp.asarray(a, F32), aux) if ranked else jnp.where(here, idx, aux)
            out.append((jnp.where(hit, -jnp.inf, v), jnp.where(here, m, vals), aux))
        return tuple(out)

    small = jnp.zeros((PEER_TOPK, lanes), F32)
    unranked = jnp.full((nk, lanes), float(PEER_TOPK), F32)
    init = tuple((v, small, unranked if ranked else small) for v, ranked in zip(scores, by_key))
    return [(vals, aux) for _, vals, aux in lax.fori_loop(0, PEER_TOPK, body, init)]


def _route_head_exact(s1, s2, pos, neg):
    lanes = s1.shape[1]
    slot = lax.broadcasted_iota(jnp.int32, (PEER_TOPK, lanes), 0)
    kio = lax.broadcasted_iota(jnp.int32, (PEER_N_KEYS, lanes), 0).astype(F32)
    (v1, idx1), (v2, rank2) = _extract_sorted([s1, s2], [False, True])
    blocks = [v1[0:1] + v2[0:8], v1[0:1] + v2[8:16]]
    blocks += [v1[a:a + 1] + v2[0:8] for a in range(1, 8)]
    blocks += [v1[8:16] + v2[0:1]]
    cand = jnp.concatenate(blocks, axis=0) + neg

    def pick(kk, carry):
        cand, chosen, ts = carry
        m = jnp.max(cand, axis=0, keepdims=True)
        first = jnp.min(jnp.where(cand == m, pos, 1e9), axis=0, keepdims=True)
        hit = pos == first
        return (jnp.where(hit, -jnp.inf, cand), jnp.where(hit, 1.0, chosen), jnp.where(slot == kk, m, ts))

    _, chosen, ts = lax.fori_loop(0, PEER_TOPK, pick,
                                  (cand, jnp.zeros_like(cand), jnp.zeros((PEER_TOPK, lanes), F32)))
    z = jnp.sum(jnp.exp(ts - ts[0:1]), axis=0, keepdims=True)
    counts = [jnp.sum(chosen[0:16], axis=0, keepdims=True)]
    counts += [jnp.sum(chosen[8 * a + 8:8 * a + 16], axis=0, keepdims=True) for a in range(1, 8)]
    counts += [chosen[72 + a:73 + a] for a in range(8)]
    c1 = jnp.zeros((PEER_N_KEYS, lanes), F32)
    for a in range(PEER_TOPK):
        c1 = jnp.where(kio == idx1[a:a + 1], counts[a], c1)
    return c1, jnp.exp(s1 - v1[0:1]) / z, rank2, jnp.exp(s2 - v2[0:1])


def _route_kernel(h_ref, wq_ref, keys_ref, pos_ref, neg_ref,
                  c1_ref, e1_ref, r2_ref, e2_ref, qt_ref, sc_ref):
    half = PEER_QUERY_DIM // 2
    qt_ref[...] = _dot_nt(wq_ref[...], h_ref[...]).astype(BF16)

    def store(h, maps):
        c1, e1, r2, e2 = maps
        c1_ref[h] = c1
        e1_ref[h] = e1
        r2_ref[h] = r2.astype(BF16)
        e2_ref[h] = e2.astype(BF16)

    def head_body(h, _):
        r0 = pl.multiple_of(h * PEER_QUERY_DIM, PEER_QUERY_DIM)
        sc_ref[0] = _dot(keys_ref[2 * h], qt_ref[pl.ds(r0, half), :])
        sc_ref[1] = _dot(keys_ref[2 * h + 1], qt_ref[pl.ds(r0 + half, half), :])
        maps, tied = _route_head_fast(sc_ref[0], sc_ref[1])
        any_tied = jnp.max(jnp.where(tied, 1.0, 0.0)) > 0.0

        @pl.when(any_tied)
        def _():
            store(h, _route_head_exact(sc_ref[0], sc_ref[1], pos_ref[...], neg_ref[...]))

        @pl.when(jnp.logical_not(any_tied))
        def _():
            store(h, maps)

        return 0

    lax.fori_loop(0, PEER_HEADS, head_body, 0)


def _peer_route(h2, wq_t, keys, tt=2 * LANES):
    T, D = h2.shape
    pos, neg = _cand_tables(tt)
    stat_spec = pl.BlockSpec((PEER_HEADS, PEER_N_KEYS, tt), lambda i: (0, 0, i))
    stat = lambda dt: jax.ShapeDtypeStruct((PEER_HEADS, PEER_N_KEYS, T), dt)
    return pl.pallas_call(
        _route_kernel,
        grid=(T // tt,),
        in_specs=[pl.BlockSpec((tt, D), lambda i: (i, 0)),
                  pl.BlockSpec(wq_t.shape, lambda i: (0, 0)),
                  pl.BlockSpec(keys.shape, lambda i: (0, 0, 0)),
                  pl.BlockSpec((_CAND_ROWS, tt), lambda i: (0, 0)),
                  pl.BlockSpec((_CAND_ROWS, tt), lambda i: (0, 0))],
        out_specs=[stat_spec] * 4,
        out_shape=[stat(F32), stat(F32), stat(BF16), stat(BF16)],
        scratch_shapes=[pltpu.VMEM((PEER_HEADS * PEER_QUERY_DIM, tt), BF16),
                        pltpu.VMEM((2, PEER_N_KEYS, tt), F32)],
        compiler_params=_cparams(("parallel",)),
        name="peer_route",
    )(h2, wq_t, keys, pos, neg)


_KEY_GROUP = 16
_UNITS = 4
_DRAIN_PIECES = 2


def _build_gated(a_ref, p_ref, c1_ref, e1_ref, r2_ref, e2_ref, key0, g0, ng, lt):
    rep = PEER_N_KEYS // 16

    def rows16(row):
        blk = jnp.broadcast_to(row, (16, LANES)).astype(BF16)
        return jnp.concatenate([blk] * rep, axis=0)

    ls = slice(lt * LANES, (lt + 1) * LANES)
    w = [jnp.zeros((PEER_N_KEYS, LANES), BF16) for _ in range(ng)]
    for h in range(PEER_HEADS):
        c1 = c1_ref[h, pl.ds(key0, _KEY_GROUP), ls]
        e1 = e1_ref[h, pl.ds(key0, _KEY_GROUP), ls]
        r2 = r2_ref[h, :, ls]
        e2 = e2_ref[h, :, ls]
        for g in range(ng):
            thr = rows16(c1[g0 + g:g0 + g + 1])
            gate = rows16(e1[g0 + g:g0 + g + 1])
            w[g] = w[g] + jnp.where(r2 < thr, e2, jnp.zeros_like(e2)) * gate
    for g in range(g0, g0 + ng):
        rs = slice(g * PEER_N_KEYS, (g + 1) * PEER_N_KEYS)
        a = a_ref[rs, ls]
        gelu = 0.5 * a * (1.0 + lax.erf(a * (2.0 ** -0.5)))
        p_ref[rs, ls] = gelu.astype(BF16) * w[g - g0]


def _experts_kernel(h_ref, dn_ref, upt_ref, c1_ref, e1_ref, r2_ref, e2_ref, x_ref, *rest, emit_norm):
    if emit_norm:
        g_ref, o_ref, hn_ref, a_ref, p_ref, acc_ref = rest
    else:
        o_ref, a_ref, p_ref, acc_ref = rest
    te, tt = a_ref.shape
    d_model = acc_ref.shape[0]
    e = pl.program_id(1)
    n_tiles = pl.num_programs(1) - 1
    cur = e % 2
    ng = _KEY_GROUP // _UNITS
    n_lane = tt // LANES
    key0 = pl.multiple_of(jnp.minimum(e, n_tiles - 1) * _KEY_GROUP, _KEY_GROUP)

    def front_mm(u):
        rows = te // _UNITS
        rs = slice(u * rows, (u + 1) * rows)
        a_ref[rs, :] = _dot_nt(dn_ref[rs, :], h_ref[...])

    def back_mm(r):
        rows = d_model // _DRAIN_PIECES
        rs = slice(r * rows, (r + 1) * rows)
        acc_ref[rs, :] += _dot(upt_ref[rs, :], p_ref[1 - cur])

    units_per_drain = _UNITS // _DRAIN_PIECES

    def run(front, back):
        if front:
            front_mm(0)
        for u in range(_UNITS):
            for lt in range(n_lane):
                if front:
                    _build_gated(a_ref, p_ref.at[cur], c1_ref, e1_ref, r2_ref, e2_ref, key0, u * ng, ng, lt)
                if front and lt == 0 and u + 1 < _UNITS:
                    front_mm(u + 1)
                if back and lt == n_lane // 2 and (u + 1) % units_per_drain == 0:
                    back_mm(u // units_per_drain)

    @pl.when(e == 0)
    def _():
        acc_ref[...] = jnp.zeros_like(acc_ref)
        run(True, False)

    @pl.when((e > 0) & (e < n_tiles))
    def _():
        run(True, True)

    @pl.when(e == n_tiles)
    def _():
        run(False, True)
        y = x_ref[...] + acc_ref[...].T
        o_ref[...] = y
        if emit_norm:
            hn_ref[...] = _rms_normed(y, g_ref[...])


def _peer_experts(h2, down, up_t, stats, x, next_gain, tt=512):
    T, D = h2.shape
    E = down.shape[0]
    te = _KEY_GROUP * PEER_N_KEYS
    n_tiles = E // te
    emit_norm = next_gain is not None
    stat_spec = pl.BlockSpec((PEER_HEADS, PEER_N_KEYS, tt), lambda i, e: (0, 0, i))
    row_spec = pl.BlockSpec((tt, D), lambda i, e: (i, 0))
    in_specs = [row_spec,
                pl.BlockSpec((te, D), lambda i, e: (jnp.minimum(e, n_tiles - 1), 0)),
                pl.BlockSpec((None, D, te), lambda i, e: (jnp.maximum(e - 1, 0), 0, 0)),
                stat_spec, stat_spec, stat_spec, stat_spec,
                row_spec]
    args = [h2, down, up_t, *stats, x]
    out_specs = [row_spec]
    out_shape = [jax.ShapeDtypeStruct((T, D), F32)]
    if emit_norm:
        in_specs.append(pl.BlockSpec((1, D), lambda i, e: (0, 0)))
        args.append(next_gain.reshape(1, D).astype(F32))
        out_specs.append(row_spec)
        out_shape.append(jax.ShapeDtypeStruct((T, D), BF16))
    res = pl.pallas_call(
        functools.partial(_experts_kernel, emit_norm=emit_norm),
        grid=(T // tt, n_tiles + 1),
        in_specs=in_specs, out_specs=out_specs, out_shape=out_shape,
        scratch_shapes=[pltpu.VMEM((te, tt), F32),
                        pltpu.VMEM((2, te, tt), BF16),
                        pltpu.VMEM((D, tt), F32)],
        compiler_params=_cparams(("parallel", "arbitrary"), VMEM_LIMIT_EXPERTS),
        name="peer_experts",
    )(*args)
    return (res[0], res[1]) if emit_norm else (res[0], None)


def _peer_layer(x, h2, next_gain, w_query, sub_keys, down, up):
    keys = sub_keys.reshape(PEER_HEADS * 2, PEER_N_KEYS, PEER_QUERY_DIM // 2).astype(BF16)
    stats = _peer_route(h2, w_query.T.astype(BF16), keys)
    te = _KEY_GROUP * PEER_N_KEYS
    up_t = up.reshape(up.shape[0] // te, te, up.shape[1]).transpose(0, 2, 1).astype(BF16)
    return _peer_experts(h2, down.astype(BF16), up_t, stats, x, next_gain)


def _tile_heads(g, n):
    return jnp.tile(g.astype(F32), n)


def _even_mixer(x, h, ffn_gain, w_in, f_bias, qn_a, kn_a, qn_b, kn_b, w_out, tabs, *, batch, seq):
    w = w_in.astype(BF16)
    o_qa, o_ka, o_va, o_qb, o_kb, o_vb, o_gb, o_fb = (
        0, A_W, 2 * A_W, 3 * A_W, 3 * A_W + B_W, 3 * A_W + 2 * B_W, 3 * A_W + 3 * B_W, 3 * A_W + 4 * B_W)
    gain_a = jnp.concatenate([_tile_heads(qn_a, MOBA_HEADS), _tile_heads(kn_a, MOBA_HEADS)])
    qk_a, km = _proj(h, w[:, o_qa:o_va], seq=seq, tn=A_W, gain=gain_a, rope_tabs=tabs, kmean=True)
    nb = seq // MOBA_BLOCK
    kmean = km.reshape(batch, nb, 2 * A_W)[:, :, A_W:]
    kmean = jnp.pad(kmean, ((0, 0), (0, LANES - nb), (0, 0))).astype(BF16)
    gain_b = jnp.concatenate([_tile_heads(qn_b, FOX_HEADS), _tile_heads(kn_b, FOX_HEADS)])
    qk_b = _proj(h, w[:, o_qb:o_vb], seq=seq, tn=B_W, gain=gain_b)
    w_plain = jnp.concatenate([w[:, o_va:o_qb], w[:, o_vb:o_fb]], axis=1)
    vvg = _proj(h, w_plain, seq=seq, tn=A_W)
    va, vb, gb = vvg[:, :A_W], vvg[:, A_W:A_W + B_W], vvg[:, A_W + B_W:]
    frow = _fox_gates(h, w[:, o_fb:].T, f_bias, batch=batch, seq=seq)
    oa = _moba(qk_a[:, :A_W], qk_a[:, A_W:], va, kmean, batch=batch, seq=seq)
    ob = _fox(qk_b[:, :B_W], qk_b[:, B_W:], vb, gb, frow, batch=batch, seq=seq)
    return _outproj([oa, ob], w_out.astype(BF16), x, ffn_gain)


def _odd_mixer(x, h, ffn_gain, w_in, qn, kn, sinks, w_out, tabs, *, batch, seq):
    w = w_in.astype(BF16)
    qw = SWA_Q_HEADS * HEAD_DIM
    kw = SWA_KV_HEADS * HEAD_DIM
    q = _proj(h, w[:, :qw], seq=seq, tn=512, gain=_tile_heads(qn, SWA_Q_HEADS), rope_tabs=tabs)
    k = _proj(h, w[:, qw:qw + kw], seq=seq, tn=kw, gain=_tile_heads(kn, SWA_KV_HEADS), rope_tabs=tabs)
    v = _proj(h, w[:, qw + kw:], seq=seq, tn=kw)
    o = _swa(q, k, v, sinks, batch=batch, seq=seq)
    return _outproj([o], w_out.astype(BF16), x, ffn_gain)


def kernel(x, attn_norm, ffn_norm, ev_w_in, ev_forget_bias, ev_q_norm_a, ev_k_norm_a, ev_q_norm_b,
           ev_k_norm_b, ev_w_out, od_w_in, od_q_norm, od_k_norm, od_sinks, od_w_out,
           peer_w_query, peer_sub_keys, peer_down, peer_up):
    batch, seq, d_model = x.shape
    depth = attn_norm.shape[0]
    tabs = _rope_tables(seq)
    xt = x.reshape(batch * seq, d_model)
    h = _rmsnorm(xt, attn_norm[0])
    for l in range(depth):
        i = l // 2
        if l % 2 == 0:
            xt, h2 = _even_mixer(xt, h, ffn_norm[l], ev_w_in[i], ev_forget_bias[i], ev_q_norm_a[i],
                                 ev_k_norm_a[i], ev_q_norm_b[i], ev_k_norm_b[i], ev_w_out[i], tabs,
                                 batch=batch, seq=seq)
        else:
            xt, h2 = _odd_mixer(xt, h, ffn_norm[l], od_w_in[i], od_q_norm[i], od_k_norm[i], od_sinks[i],
                                od_w_out[i], tabs, batch=batch, seq=seq)
        next_gain = attn_norm[l + 1] if l + 1 < depth else None
        xt, h = _peer_layer(xt, h2, next_gain, peer_w_query[l], peer_sub_keys[l], peer_down[l], peer_up[l])
    return xt.reshape(batch, seq, d_model)
```

```python
import functools

import numpy as np
import jax
import jax.numpy as jnp
from jax import lax
from jax.experimental import pallas as pl
from jax.experimental.pallas import tpu as pltpu

F32 = jnp.float32
BF16 = jnp.bfloat16

HEAD_DIM = 64
ROT_DIM = HEAD_DIM // 4
ROPE_THETA = 500000.0
ATTN_SCALE = HEAD_DIM ** -0.5
EPS = 1e-6
NEG_INF = -1e30

MOBA_HEADS = 8
FOX_HEADS = 8
MOBA_BLOCK = 256
MOBA_TOPK = 3
A_W = MOBA_HEADS * HEAD_DIM
B_W = FOX_HEADS * HEAD_DIM

SWA_Q_HEADS = 16
SWA_KV_HEADS = 2
SWA_WINDOW = 128

PEER_HEADS = 8
PEER_N_KEYS = 128
PEER_TOPK = 16
PEER_QUERY_DIM = 128

LANES = 128
PAIR_W = 2 * HEAD_DIM
VMEM_LIMIT = 48 * 1024 * 1024
VMEM_LIMIT_EXPERTS = 58 * 1024 * 1024


def _cparams(sem, vmem_limit=VMEM_LIMIT):
    return pltpu.CompilerParams(dimension_semantics=sem, vmem_limit_bytes=vmem_limit)


def _dot_nt(a, b):
    return lax.dot_general(a, b, (((1,), (1,)), ((), ())), preferred_element_type=F32)


def _dot(a, b):
    return jnp.dot(a, b, preferred_element_type=F32)


def _split2(x):
    h1 = x.astype(BF16)
    return h1, (x - h1.astype(F32)).astype(BF16)


def _split3(x):
    h1 = x.astype(BF16)
    r1 = x - h1.astype(F32)
    h2 = r1.astype(BF16)
    h3 = (r1 - h2.astype(F32)).astype(BF16)
    return h1, h2, h3


def _rmsnorm_kernel(x_ref, g_ref, o_ref):
    x = x_ref[...]
    ms = jnp.mean(x * x, axis=-1, keepdims=True)
    o_ref[...] = (x * lax.rsqrt(ms + EPS) * g_ref[...]).astype(o_ref.dtype)


def _rmsnorm(x, gain, tm=512):
    T, D = x.shape
    return pl.pallas_call(
        _rmsnorm_kernel,
        grid=(T // tm,),
        in_specs=[pl.BlockSpec((tm, D), lambda i: (i, 0)),
                  pl.BlockSpec((1, D), lambda i: (0, 0))],
        out_specs=pl.BlockSpec((tm, D), lambda i: (i, 0)),
        out_shape=jax.ShapeDtypeStruct((T, D), BF16),
        compiler_params=_cparams(("parallel",)),
        name="rmsnorm",
    )(x, gain.reshape(1, D))


def _proj_kernel(*refs, norm, rope, kmean, tn):
    it = iter(refs)
    h_ref, w_ref = next(it), next(it)
    gain_ref = next(it) if norm else None
    bd_ref = next(it) if norm else None
    if rope:
        c_ref, sa_ref, sb_ref = next(it), next(it), next(it)
    o_ref = next(it)
    km_ref = next(it) if kmean else None

    y = _dot(h_ref[...], w_ref[...])
    if norm:
        y2 = y * y
        bd = bd_ref[...]
        cols = []
        tm = y.shape[0]
        for c in range(tn // LANES):
            h1, h2 = _split2(y2[:, c * LANES:(c + 1) * LANES])
            t = _dot(jnp.concatenate([h1, h2], axis=0), bd)
            cols.append(t[:tm] + t[tm:])
        ms = cols[0] if len(cols) == 1 else jnp.concatenate(cols, axis=1)
        y = y * lax.rsqrt(ms + EPS) * gain_ref[...]
    if rope:
        rep = tn // LANES
        tile = (lambda t: t) if rep == 1 else (lambda t: jnp.concatenate([t] * rep, axis=1))
        y = (y * tile(c_ref[...])
             + pltpu.roll(y, tn - ROT_DIM // 2, 1) * tile(sa_ref[...])
             + pltpu.roll(y, ROT_DIM // 2, 1) * tile(sb_ref[...]))
    o_ref[...] = y.astype(o_ref.dtype)
    if kmean:
        for r in range(km_ref.shape[0]):
            km_ref[r] = jnp.mean(y[r * MOBA_BLOCK:(r + 1) * MOBA_BLOCK], axis=0, keepdims=True)


def _proj(h, w, *, seq, tn, tm=512, gain=None, rope_tabs=None, kmean=False):
    T, D = h.shape
    N = w.shape[1]
    norm = gain is not None
    rope = rope_tabs is not None
    nseq = seq // tm
    in_specs = [pl.BlockSpec((tm, D), lambda i, j: (i, 0)),
                pl.BlockSpec((D, tn), lambda i, j: (0, j))]
    args = [h, w]
    if norm:
        bd = np.kron(np.eye(LANES // HEAD_DIM), np.ones((HEAD_DIM, HEAD_DIM))) / HEAD_DIM
        in_specs += [pl.BlockSpec((1, tn), lambda i, j: (0, j)),
                     pl.BlockSpec((LANES, LANES), lambda i, j: (0, 0))]
        args += [gain.reshape(1, N).astype(F32), jnp.asarray(bd, BF16)]
    if rope:
        in_specs += [pl.BlockSpec((tm, LANES), lambda i, j: (i % nseq, 0))] * 3
        args += list(rope_tabs)
    out_specs = [pl.BlockSpec((tm, tn), lambda i, j: (i, j))]
    out_shape = [jax.ShapeDtypeStruct((T, N), BF16)]
    if kmean:
        out_specs.append(pl.BlockSpec((tm // MOBA_BLOCK, 1, tn), lambda i, j: (i, 0, j)))
        out_shape.append(jax.ShapeDtypeStruct((T // MOBA_BLOCK, 1, N), F32))
    res = pl.pallas_call(
        functools.partial(_proj_kernel, norm=norm, rope=rope, kmean=kmean, tn=tn),
        grid=(T // tm, N // tn),
        in_specs=in_specs, out_specs=out_specs, out_shape=out_shape,
        compiler_params=_cparams(("parallel", "parallel")),
        name="proj",
    )(*args)
    return res if kmean else res[0]


def _rope_tables(seq):
    half = ROT_DIM // 2
    inv_freq = jnp.power(ROPE_THETA, -jnp.arange(0, ROT_DIM, 2, dtype=F32) / ROT_DIM)
    ang = jnp.arange(seq, dtype=F32)[:, None] * inv_freq[None, :]
    cos, sin = jnp.cos(ang), jnp.sin(ang)
    one = jnp.ones((seq, HEAD_DIM - ROT_DIM), F32)
    zero = jnp.zeros((seq, HEAD_DIM - ROT_DIM), F32)
    z8 = jnp.zeros((seq, half), F32)
    c = jnp.concatenate([cos, cos, one], axis=1)
    sa = jnp.concatenate([-sin, z8, zero], axis=1)
    sb = jnp.concatenate([z8, sin, zero], axis=1)
    rep = LANES // HEAD_DIM
    return tuple(jnp.concatenate([t] * rep, axis=1) for t in (c, sa, sb))


def _gates_kernel(h_ref, wf_ref, b_ref, tri_ref, o_ref, carry_ref):
    @pl.when(pl.program_id(1) == 0)
    def _():
        carry_ref[...] = jnp.zeros_like(carry_ref)

    z = _dot_nt(wf_ref[...], h_ref[...]) + b_ref[...][:, :1]
    lf = jnp.minimum(z, 0.0) - jnp.log1p(jnp.exp(-jnp.abs(z)))
    tri = tri_ref[...]
    h1, h2, h3 = _split3(lf)
    cs = _dot(h1, tri) + _dot(h2, tri) + _dot(h3, tri) + carry_ref[...][:, :1]
    o_ref[0] = cs
    carry_ref[...] = jnp.broadcast_to(cs[:, -1:], carry_ref.shape)


def _fox_gates(h, wf_t, bias, *, batch, seq, tm=512):
    T, D = h.shape
    nh = wf_t.shape[0]
    nseq = seq // tm
    tri = jnp.asarray(np.triu(np.ones((tm, tm))), BF16)
    return pl.pallas_call(
        _gates_kernel,
        grid=(batch, nseq),
        in_specs=[pl.BlockSpec((tm, D), lambda b, s: (b * nseq + s, 0)),
                  pl.BlockSpec((nh, D), lambda b, s: (0, 0)),
                  pl.BlockSpec((nh, LANES), lambda b, s: (0, 0)),
                  pl.BlockSpec((tm, tm), lambda b, s: (0, 0))],
        out_specs=pl.BlockSpec((1, nh, tm), lambda b, s: (b, 0, s)),
        out_shape=jax.ShapeDtypeStruct((batch, nh, seq), F32),
        scratch_shapes=[pltpu.VMEM((nh, LANES), F32)],
        compiler_params=_cparams(("parallel", "arbitrary")),
        name="fox_gates",
    )(h, wf_t, jnp.broadcast_to(bias.astype(F32)[:, None], (nh, LANES)), tri)


def _lane_tile(x, width):
    rep = width // LANES
    return x if rep == 1 else jnp.concatenate([x] * rep, axis=1)


def _flash_init(m_ref, acc_ref):
    m_ref[...] = jnp.full(m_ref.shape, NEG_INF, F32)
    acc_ref[...] = jnp.zeros(acc_ref.shape, F32)


def _head_values(v):
    lane = lax.broadcasted_iota(jnp.int32, v.shape, 1)
    return [jnp.where((lane >= hh * HEAD_DIM) & (lane < (hh + 1) * HEAD_DIM), v, jnp.ones_like(v))
            for hh in range(2)]


def _flash_update(slot, s, v, m_ref, acc_ref):
    tk = s.shape[1]
    m_prev = m_ref[slot]
    m_new = jnp.maximum(m_prev, jnp.max(s, axis=1, keepdims=True))
    alpha = jnp.exp(m_prev - m_new)
    p = jnp.exp(s - _lane_tile(m_new, tk))
    acc_ref[slot] = alpha * acc_ref[slot] + _dot(p.astype(BF16), v)
    m_ref[slot] = m_new


def _flash_finish(lane, acc_ref):
    outs = []
    for hh in range(2):
        acc = acc_ref[hh]
        den = (1 - hh) * HEAD_DIM
        outs.append(acc / acc[:, den:den + 1])
    return jnp.where(lane < HEAD_DIM, outs[0], outs[1])


def _flash_tiles(qi, tk, logits, values, m_ref, acc_ref):
    def absorb(s, off):
        vh = values(off)
        for hh in range(2):
            _flash_update(hh, s[hh], vh[hh], m_ref, acc_ref)

    def pair(off_a, off_b, b_diagonal):
        sa = logits(off_a, False)
        sb = logits(off_b, b_diagonal)
        absorb(sa, off_a)
        absorb(sb, off_b)

    def body(jj, carry):
        off = pl.multiple_of(2 * jj * tk, 2 * tk)
        pair(off, off + tk, False)
        return carry

    lax.fori_loop(0, qi // 2, body, 0)
    diag = pl.multiple_of(qi * tk, tk)

    @pl.when(qi % 2 == 1)
    def _():
        pair(diag - tk, diag, True)

    @pl.when(qi % 2 == 0)
    def _():
        absorb(logits(diag, True), diag)


def _head_queries(q, lane):
    qs = q * ATTN_SCALE
    return [jnp.where((lane >= hh * HEAD_DIM) & (lane < (hh + 1) * HEAD_DIM), qs, jnp.zeros_like(qs))
            for hh in range(2)]


def _moba_kernel(q_ref, k_ref, v_ref, km_ref, o_ref, m_ref, acc_ref):
    tq = q_ref.shape[1]
    tk = tq
    qi = pl.program_id(2)
    lane = lax.broadcasted_iota(jnp.int32, (tq, LANES), 1)
    lane_f = lane.astype(F32)
    rowv = lax.broadcasted_iota(jnp.int32, (tq, LANES), 0)
    row_blk = 2 * qi + (rowv >= MOBA_BLOCK).astype(jnp.int32)
    row = lax.broadcasted_iota(jnp.int32, (tq, tk), 0)
    col = lax.broadcasted_iota(jnp.int32, (tq, tk), 1)
    qh = _head_queries(q_ref[0], lane)
    _flash_init(m_ref, acc_ref)

    sels = []
    for hh in range(2):
        gate = _dot_nt(qh[hh], km_ref[0])
        gate = jnp.where(lane < row_blk, gate, -jnp.inf)
        sel = jnp.zeros((tq, LANES), F32)
        for _ in range(MOBA_TOPK):
            m = jnp.max(gate, axis=1, keepdims=True)
            idx = jnp.min(jnp.where(gate == m, lane_f, float(LANES)), axis=1, keepdims=True)
            hit = lane_f == idx
            sel = jnp.where(hit & (m > -jnp.inf), 1.0, sel)
            gate = jnp.where(hit, -jnp.inf, gate)
        sels.append(sel)

    def chosen(sel, blk):
        return jnp.max(jnp.where(lane == blk, sel, 0.0), axis=1, keepdims=True) > 0.0

    def logits(off, diagonal):
        kj = k_ref[0, pl.ds(off, tk), :]
        out = []
        for hh in range(2):
            s = _dot_nt(qh[hh], kj)
            if diagonal:
                visible = (col >= MOBA_BLOCK) | (row < MOBA_BLOCK) | chosen(sels[hh], 2 * qi)
                s = jnp.where((col <= row) & visible, s, NEG_INF)
            else:
                blk = 2 * (off // tk)
                s = jnp.concatenate(
                    [jnp.where(chosen(sels[hh], blk), s[:, :MOBA_BLOCK], NEG_INF),
                     jnp.where(chosen(sels[hh], blk + 1), s[:, MOBA_BLOCK:], NEG_INF)], axis=1)
            out.append(s)
        return out

    def values(off):
        return _head_values(v_ref[0, pl.ds(off, tk), :])

    _flash_tiles(qi, tk, logits, values, m_ref, acc_ref)
    o_ref[0] = _flash_finish(lane, acc_ref).astype(o_ref.dtype)


def _flash_scratch(tq):
    return [pltpu.VMEM((2, tq, LANES), F32)] * 2


def _moba(q, k, v, kmean, *, batch, seq):
    W = q.shape[1]
    tq = 2 * MOBA_BLOCK
    q3, k3, v3 = (t.reshape(batch, seq, W) for t in (q, k, v))
    out = pl.pallas_call(
        _moba_kernel,
        grid=(batch, W // PAIR_W, seq // tq),
        in_specs=[pl.BlockSpec((1, tq, PAIR_W), lambda b, p, i: (b, i, p)),
                  pl.BlockSpec((1, seq, PAIR_W), lambda b, p, i: (b, 0, p)),
                  pl.BlockSpec((1, seq, PAIR_W), lambda b, p, i: (b, 0, p)),
                  pl.BlockSpec((1, LANES, PAIR_W), lambda b, p, i: (b, 0, p))],
        out_specs=pl.BlockSpec((1, tq, PAIR_W), lambda b, p, i: (b, i, p)),
        out_shape=jax.ShapeDtypeStruct((batch, seq, W), BF16),
        scratch_shapes=_flash_scratch(tq),
        compiler_params=_cparams(("parallel", "parallel", "parallel")),
        name="moba",
    )(q3, k3, v3, kmean)
    return out.reshape(batch * seq, W)


_EXP_UNDERFLOW = 112.0


def _fox_kernel(q_ref, k_ref, v_ref, g_ref, frow_ref, far_ref, o_ref, m_ref, acc_ref):
    tq = q_ref.shape[1]
    tk = tq
    pr = pl.program_id(1)
    qi = pl.program_id(2)
    lane = lax.broadcasted_iota(jnp.int32, (tq, LANES), 1)
    row = lax.broadcasted_iota(jnp.int32, (tq, tk), 0)
    col = lax.broadcasted_iota(jnp.int32, (tq, tk), 1)
    qh = _head_queries(q_ref[0], lane)
    _flash_init(m_ref, acc_ref)

    def key_gates(off, width):
        f_all = frow_ref[0, :, pl.ds(off, width)]
        sub = lax.broadcasted_iota(jnp.int32, f_all.shape, 0)
        return [jnp.sum(jnp.where(sub == 2 * pr + hh, f_all, 0.0), axis=0, keepdims=True) for hh in range(2)]

    f_ref = [f[:, :1] for f in key_gates(pl.multiple_of(qi * tq, tq), LANES)]

    def logits(off, diagonal):
        kj = k_ref[0, pl.ds(off, tk), :]
        fk = key_gates(off, tk)
        out = []
        for hh in range(2):
            s = _dot_nt(qh[hh], kj) - (fk[hh] - f_ref[hh])
            out.append(jnp.where(col <= row, s, NEG_INF) if diagonal else s)
        return out

    def absorb(s, off):
        vh = _head_values(v_ref[0, pl.ds(off, tk), :])
        for hh in range(2):
            _flash_update(hh, s[hh], vh[hh], m_ref, acc_ref)

    def vanishes(j):
        tail = frow_ref[0, :, pl.ds(pl.multiple_of((j + 1) * tk - LANES, LANES), LANES)]
        sub = lax.broadcasted_iota(jnp.int32, tail.shape, 0)
        last = lax.broadcasted_iota(jnp.int32, tail.shape, 1) == LANES - 1
        gap = tail - jnp.where(sub == 2 * pr, f_ref[0], f_ref[1]) - far_ref[...][:, :1]
        mine = last & ((sub == 2 * pr) | (sub == 2 * pr + 1))
        return jnp.min(jnp.where(mine, gap, jnp.inf)) >= 0.0

    absorb(logits(pl.multiple_of(qi * tk, tk), True), pl.multiple_of(qi * tk, tk))

    def more(j):
        return jnp.logical_and(j >= 1, jnp.logical_not(vanishes(jnp.maximum(j, 0))))

    def pair(j):
        off_a = pl.multiple_of(j * tk, tk)
        off_b = pl.multiple_of((j - 1) * tk, tk)
        sa = logits(off_a, False)
        sb = logits(off_b, False)
        absorb(sa, off_a)
        absorb(sb, off_b)
        return j - 2

    j = lax.while_loop(more, pair, qi - 1)

    @pl.when(jnp.logical_and(j == 0, jnp.logical_not(vanishes(0))))
    def _():
        absorb(logits(0, False), 0)

    o = _flash_finish(lane, acc_ref)
    o_ref[0] = (o * jax.nn.sigmoid(g_ref[0].astype(F32))).astype(o_ref.dtype)


def _fox(q, k, v, g, frow, logit_bound, *, batch, seq, tq=512):
    far = jnp.full((1, LANES), 2.0 * logit_bound + _EXP_UNDERFLOW, F32)
    W = q.shape[1]
    nh = frow.shape[1]
    q3, k3, v3, g3 = (t.reshape(batch, seq, W) for t in (q, k, v, g))
    out = pl.pallas_call(
        _fox_kernel,
        grid=(batch, W // PAIR_W, seq // tq),
        in_specs=[pl.BlockSpec((1, tq, PAIR_W), lambda b, p, i: (b, i, p)),
                  pl.BlockSpec((1, seq, PAIR_W), lambda b, p, i: (b, 0, p)),
                  pl.BlockSpec((1, seq, PAIR_W), lambda b, p, i: (b, 0, p)),
                  pl.BlockSpec((1, tq, PAIR_W), lambda b, p, i: (b, i, p)),
                  pl.BlockSpec((1, nh, seq), lambda b, p, i: (b, 0, 0)),
                  pl.BlockSpec((1, LANES), lambda b, p, i: (0, 0))],
        out_specs=pl.BlockSpec((1, tq, PAIR_W), lambda b, p, i: (b, i, p)),
        out_shape=jax.ShapeDtypeStruct((batch, seq, W), BF16),
        scratch_shapes=_flash_scratch(tq),
        compiler_params=_cparams(("parallel", "parallel", "parallel")),
        name="fox",
    )(q3, k3, v3, g3, frow, far)
    return out.reshape(batch * seq, W)


def _swa_kernel(q_ref, k_ref, v_ref, sink_ref, bias_ref, o_ref):
    tq = q_ref.shape[1]
    qi = pl.program_id(1)
    group = SWA_Q_HEADS // SWA_KV_HEADS
    tk = tq + SWA_WINDOW
    lane = lax.broadcasted_iota(jnp.int32, (tq, LANES), 1)
    kstart = pl.multiple_of(jnp.maximum(qi * tq - SWA_WINDOW, 0), SWA_WINDOW)
    k = k_ref[0, pl.ds(kstart, tk), :]
    vh = _head_values(v_ref[0, pl.ds(kstart, tk), :])
    swap = lambda t: pltpu.roll(t.astype(F32), HEAD_DIM, 1).astype(BF16)
    k_by_half = [k, swap(k)]
    bias = bias_ref[jnp.minimum(qi, 1)]
    bias = jnp.concatenate([bias] * (group // 2), axis=0)
    sink_tab = sink_ref[...]

    chains = []
    for c in range(SWA_KV_HEADS):
        v_by_half = [vh[c], swap(vh[c])]
        for swapped in range(2):
            hh = c if not swapped else 1 - c
            heads = [h for h in range(c * group, (c + 1) * group) if h % 2 == hh]
            pieces, sinks = [], []
            for head in heads:
                blk = q_ref[0, :, (head // 2) * PAIR_W:(head // 2 + 1) * PAIR_W] * ATTN_SCALE
                pieces.append(jnp.where((lane >= hh * HEAD_DIM) & (lane < (hh + 1) * HEAD_DIM),
                                        blk, jnp.zeros_like(blk)))
                sinks.append(jnp.broadcast_to(sink_tab[head:head + 1, :], (tq, LANES)))
            s = _dot_nt(jnp.concatenate(pieces, axis=0), k_by_half[swapped]) + bias
            chains.append((heads, s, jnp.concatenate(sinks, axis=0), v_by_half[swapped]))

    outs = [None] * SWA_Q_HEADS
    for heads, s, sink, v in chains:
        m = jnp.maximum(sink, jnp.max(s, axis=1, keepdims=True))
        p = jnp.exp(s - _lane_tile(m, tk))
        acc = _dot(p.astype(BF16), v)
        den = pltpu.roll(acc, HEAD_DIM, 1) + jnp.exp(sink - m)
        o = acc / den
        for idx, head in enumerate(heads):
            outs[head] = o[idx * tq:(idx + 1) * tq]
    for pp in range(SWA_Q_HEADS // 2):
        o_ref[0, :, pp * PAIR_W:(pp + 1) * PAIR_W] = jnp.where(
            lane < HEAD_DIM, outs[2 * pp], outs[2 * pp + 1]).astype(o_ref.dtype)


def _swa_bias(tq):
    r = np.arange(tq)[:, None]
    c = np.arange(tq + SWA_WINDOW)[None, :]
    tabs = []
    for key_offset in (0, SWA_WINDOW):
        dist = r + key_offset - c
        tabs.append(np.where((dist >= 0) & (dist < SWA_WINDOW), 0.0, NEG_INF))
    return jnp.asarray(np.stack(tabs), F32)


def _swa(q, k, v, sinks, *, batch, seq, tq=SWA_WINDOW):
    W = q.shape[1]
    tk = tq + SWA_WINDOW
    q3 = q.reshape(batch, seq, W)
    k3, v3 = (t.reshape(batch, seq, PAIR_W) for t in (k, v))
    sink_tab = jnp.broadcast_to(sinks.astype(F32)[:, None], (SWA_Q_HEADS, LANES))
    out = pl.pallas_call(
        _swa_kernel,
        grid=(batch, seq // tq),
        in_specs=[pl.BlockSpec((1, tq, W), lambda b, i: (b, i, 0)),
                  pl.BlockSpec((1, seq, PAIR_W), lambda b, i: (b, 0, 0)),
                  pl.BlockSpec((1, seq, PAIR_W), lambda b, i: (b, 0, 0)),
                  pl.BlockSpec((SWA_Q_HEADS, LANES), lambda b, i: (0, 0)),
                  pl.BlockSpec((2, tq, tk), lambda b, i: (0, 0, 0))],
        out_specs=pl.BlockSpec((1, tq, W), lambda b, i: (b, i, 0)),
        out_shape=jax.ShapeDtypeStruct((batch, seq, W), BF16),
        compiler_params=_cparams(("parallel", "parallel")),
        name="swa",
    )(q3, k3, v3, sink_tab, _swa_bias(tq))
    return out.reshape(batch * seq, W)


def _rms_normed(x, gain):
    ms = jnp.mean(x * x, axis=-1, keepdims=True)
    return (x * lax.rsqrt(ms + EPS) * gain).astype(BF16)


def _outproj_kernel(*refs, n_parts):
    parts = refs[:n_parts]
    w_ref, x_ref, g_ref, o_ref, h_ref = refs[n_parts:]
    y = x_ref[...]
    off = 0
    for p_ref in parts:
        kw = p_ref.shape[1]
        y = y + _dot(p_ref[...], w_ref[off:off + kw, :])
        off += kw
    o_ref[...] = y
    h_ref[...] = _rms_normed(y, g_ref[...])


def _outproj(parts, w, x, next_gain, tm=512):
    T, D = x.shape
    in_specs = [pl.BlockSpec((tm, p.shape[1]), lambda i: (i, 0)) for p in parts]
    in_specs += [pl.BlockSpec(w.shape, lambda i: (0, 0)),
                 pl.BlockSpec((tm, D), lambda i: (i, 0)),
                 pl.BlockSpec((1, D), lambda i: (0, 0))]
    row_spec = pl.BlockSpec((tm, D), lambda i: (i, 0))
    return pl.pallas_call(
        functools.partial(_outproj_kernel, n_parts=len(parts)),
        grid=(T // tm,),
        in_specs=in_specs,
        out_specs=[row_spec, row_spec],
        out_shape=[jax.ShapeDtypeStruct((T, D), F32), jax.ShapeDtypeStruct((T, D), BF16)],
        compiler_params=_cparams(("parallel",)),
        name="outproj",
    )(*parts, w, x, next_gain.reshape(1, D).astype(F32))


_CAND_ROWS = 80


def _cand_tables(lanes):
    pos = np.zeros((_CAND_ROWS,), np.float32)
    neg = np.zeros((_CAND_ROWS,), np.float32)
    r = 0
    for a, nb in ((0, 16), (1, 8), (2, 8), (3, 8), (4, 8), (5, 8), (6, 8), (7, 8)):
        for b in range(nb):
            pos[r] = a * PEER_TOPK + b
            neg[r] = 0.0 if (a + 1) * (b + 1) <= PEER_TOPK else -np.inf
            r += 1
    for a in range(8, 16):
        pos[r] = a * PEER_TOPK
        r += 1
    assert r == _CAND_ROWS
    tab = lambda t: jnp.asarray(np.broadcast_to(t[:, None], (_CAND_ROWS, lanes)).copy())
    return tab(pos), tab(neg)


def _batcher_pairs(n):
    pairs, p = [], 1
    while p < n:
        k = p
        while k >= 1:
            for j in range(k % p, n - k, 2 * k):
                for i in range(min(k, n - j - k)):
                    if (i + j) // (2 * p) == (i + j + k) // (2 * p):
                        pairs.append((i + j, i + j + k))
            k //= 2
        p *= 2
    return pairs


_SORT16 = _batcher_pairs(PEER_TOPK)
_N_CAND_PIECES = 10
_SORT10 = [(i, j) for i, j in _SORT16 if j < _N_CAND_PIECES]
_SUBLANES = 8


def _compare_exchange(items, i, j):
    items[i], items[j] = jnp.maximum(items[i], items[j]), jnp.minimum(items[i], items[j])


def _top_sorted(pieces, pairs):
    items = list(pieces)
    for i, j in pairs:
        _compare_exchange(items, i, j)
    n = PEER_TOPK
    items += [jnp.full(items[0].shape, -jnp.inf, F32)] * (n - len(items))
    for shift in (4, 2, 1):
        items = [jnp.maximum(items[i], pltpu.roll(items[n - 1 - i], shift, 0)) for i in range(n)]
        d = n // 2
        while d >= 1:
            for i in range(n):
                if i & d == 0:
                    _compare_exchange(items, i, i + d)
            d //= 2
    return items


def _pieces(x):
    return [x[_SUBLANES * g:_SUBLANES * (g + 1)] for g in range(x.shape[0] // _SUBLANES)]


def _sublane_total(x):
    for shift in (4, 2, 1):
        x = x + pltpu.roll(x, shift, 0)
    return x


def _count_ge(pieces, thr):
    total = jnp.zeros(thr.shape, F32)
    for p in pieces:
        total = total + jnp.where(p >= thr, 1.0, 0.0)
    return _sublane_total(total)


def _route_head_fast(s1, s2):
    p1, p2 = _pieces(s1), _pieces(s2)
    v1 = _top_sorted(p1, _SORT16)
    v2 = _top_sorted(p2, _SORT16)
    sub = lax.broadcasted_iota(jnp.int32, v1[0].shape, 0)

    def spread(vals):
        out = vals[0]
        for r in range(1, _SUBLANES):
            out = jnp.where(sub == r, vals[r], out)
        return out

    v2_lo, v2_hi, v1_hi = spread(v2[:8]), spread(v2[8:]), spread(v1[8:])
    cands = [v1[0] + v2_lo, v1[0] + v2_hi, v1[1] + v2_lo]
    for a in range(2, 8):
        cands.append(jnp.where(sub < PEER_TOPK // (a + 1), v1[a] + v2_lo, -jnp.inf))
    cands.append(v1_hi + v2[0])
    ts = _top_sorted(cands, _SORT10)
    tau = ts[PEER_TOPK - 1]
    z = jnp.exp(ts[0] - ts[0])
    for kk in range(1, PEER_TOPK):
        z = z + jnp.exp(ts[kk] - ts[0])

    tied = (_count_ge(p1, v1[-1]) != float(PEER_TOPK)) | (_count_ge(p2, v2[-1]) != float(PEER_TOPK))
    tied = tied | (_count_ge(cands, tau) != float(PEER_TOPK))
    for b in range(PEER_TOPK - 1):
        tied = tied | (v1[b] == v1[b + 1]) | (v2[b] == v2[b + 1])

    cnt = []
    for a in range(PEER_TOPK):
        c = jnp.zeros(tau.shape, F32)
        for b in range(PEER_TOPK // (a + 1)):
            c = c + jnp.where(v1[a] + v2[b] >= tau, 1.0, 0.0)
        cnt.append(c)
    c1, r2 = [], []
    for x in p1:
        c = jnp.zeros(x.shape, F32)
        for a in range(PEER_TOPK):
            c = jnp.where(x == v1[a], cnt[a], c)
        c1.append(c)
    for x in p2:
        r = jnp.zeros(x.shape, F32)
        for b in range(PEER_TOPK):
            r = r + jnp.where(v2[b] > x, 1.0, 0.0)
        r2.append(r)
    inv_z = 1.0 / z
    e1 = [jnp.exp(x - v1[0]) * inv_z for x in p1]
    e2 = [jnp.exp(x - v2[0]) for x in p2]
    cat = lambda ps: jnp.concatenate(ps, axis=0)
    return (cat(c1), cat(e1), cat(r2), cat(e2)), tied


def _extract_sorted(scores, by_key):
    nk, lanes = scores[0].shape
    kio = lax.broadcasted_iota(jnp.int32, (nk, lanes), 0).astype(F32)
    slot = lax.broadcasted_iota(jnp.int32, (PEER_TOPK, lanes), 0)

    def body(a, carry):
        here = slot == a
        out = []
        for (v, vals, aux), ranked in zip(carry, by_key):
            m = jnp.max(v, axis=0, keepdims=True)
            idx = jnp.min(jnp.where(v == m, kio, float(nk)), axis=0, keepdims=True)
            hit = kio == idx
            aux = jnp.where(hit, jnp.asarray(a, F32), aux) if ranked else jnp.where(here, idx, aux)
            out.append((jnp.where(hit, -jnp.inf, v), jnp.where(here, m, vals), aux))
        return tuple(out)

    small = jnp.zeros((PEER_TOPK, lanes), F32)
    unranked = jnp.full((nk, lanes), float(PEER_TOPK), F32)
    init = tuple((v, small, unranked if ranked else small) for v, ranked in zip(scores, by_key))
    return [(vals, aux) for _, vals, aux in lax.fori_loop(0, PEER_TOPK, body, init)]


def _route_head_exact(s1, s2, pos, neg):
    lanes = s1.shape[1]
    slot = lax.broadcasted_iota(jnp.int32, (PEER_TOPK, lanes), 0)
    kio = lax.broadcasted_iota(jnp.int32, (PEER_N_KEYS, lanes), 0).astype(F32)
    (v1, idx1), (v2, rank2) = _extract_sorted([s1, s2], [False, True])
    blocks = [v1[0:1] + v2[0:8], v1[0:1] + v2[8:16]]
    blocks += [v1[a:a + 1] + v2[0:8] for a in range(1, 8)]
    blocks += [v1[8:16] + v2[0:1]]
    cand = jnp.concatenate(blocks, axis=0) + neg

    def pick(kk, carry):
        cand, chosen, ts = carry
        m = jnp.max(cand, axis=0, keepdims=True)
        first = jnp.min(jnp.where(cand == m, pos, 1e9), axis=0, keepdims=True)
        hit = pos == first
        return (jnp.where(hit, -jnp.inf, cand), jnp.where(hit, 1.0, chosen), jnp.where(slot == kk, m, ts))

    _, chosen, ts = lax.fori_loop(0, PEER_TOPK, pick,
                                  (cand, jnp.zeros_like(cand), jnp.zeros((PEER_TOPK, lanes), F32)))
    z = jnp.sum(jnp.exp(ts - ts[0:1]), axis=0, keepdims=True)
    counts = [jnp.sum(chosen[0:16], axis=0, keepdims=True)]
    counts += [jnp.sum(chosen[8 * a + 8:8 * a + 16], axis=0, keepdims=True) for a in range(1, 8)]
    counts += [chosen[72 + a:73 + a] for a in range(8)]
    c1 = jnp.zeros((PEER_N_KEYS, lanes), F32)
    for a in range(PEER_TOPK):
        c1 = jnp.where(kio == idx1[a:a + 1], counts[a], c1)
    return c1, jnp.exp(s1 - v1[0:1]) / z, rank2, jnp.exp(s2 - v2[0:1])


def _route_kernel(h_ref, wq_ref, keys_ref, pos_ref, neg_ref,
                  c1_ref, e1_ref, r2_ref, e2_ref, qt_ref, sc_ref):
    half = PEER_QUERY_DIM // 2
    qt_ref[...] = _dot_nt(wq_ref[...], h_ref[...]).astype(BF16)

    def store(h, maps):
        c1, e1, r2, e2 = maps
        c1_ref[h] = c1
        e1_ref[h] = e1
        r2_ref[h] = r2.astype(BF16)
        e2_ref[h] = e2.astype(BF16)

    def head_body(h, _):
        r0 = pl.multiple_of(h * PEER_QUERY_DIM, PEER_QUERY_DIM)
        sc_ref[0] = _dot(keys_ref[2 * h], qt_ref[pl.ds(r0, half), :])
        sc_ref[1] = _dot(keys_ref[2 * h + 1], qt_ref[pl.ds(r0 + half, half), :])
        maps, tied = _route_head_fast(sc_ref[0], sc_ref[1])
        any_tied = jnp.max(jnp.where(tied, 1.0, 0.0)) > 0.0

        @pl.when(any_tied)
        def _():
            store(h, _route_head_exact(sc_ref[0], sc_ref[1], pos_ref[...], neg_ref[...]))

        @pl.when(jnp.logical_not(any_tied))
        def _():
            store(h, maps)

        return 0

    lax.fori_loop(0, PEER_HEADS, head_body, 0)


def _peer_route(h2, wq_t, keys, tt=2 * LANES):
    T, D = h2.shape
    pos, neg = _cand_tables(tt)
    stat_spec = pl.BlockSpec((PEER_HEADS, PEER_N_KEYS, tt), lambda i: (0, 0, i))
    stat = lambda dt: jax.ShapeDtypeStruct((PEER_HEADS, PEER_N_KEYS, T), dt)
    return pl.pallas_call(
        _route_kernel,
        grid=(T // tt,),
        in_specs=[pl.BlockSpec((tt, D), lambda i: (i, 0)),
                  pl.BlockSpec(wq_t.shape, lambda i: (0, 0)),
                  pl.BlockSpec(keys.shape, lambda i: (0, 0, 0)),
                  pl.BlockSpec((_CAND_ROWS, tt), lambda i: (0, 0)),
                  pl.BlockSpec((_CAND_ROWS, tt), lambda i: (0, 0))],
        out_specs=[stat_spec] * 4,
        out_shape=[stat(F32), stat(F32), stat(BF16), stat(BF16)],
        scratch_shapes=[pltpu.VMEM((PEER_HEADS * PEER_QUERY_DIM, tt), BF16),
                        pltpu.VMEM((2, PEER_N_KEYS, tt), F32)],
        compiler_params=_cparams(("parallel",)),
        name="peer_route",
    )(h2, wq_t, keys, pos, neg)


_KEY_GROUP = 16
_UNITS = 4
_DRAIN_PIECES = 2


def _build_gated(a_ref, p_ref, c1_ref, e1_ref, r2_ref, e2_ref, key0, g0, ng, lt):
    rep = PEER_N_KEYS // 16

    def rows16(row):
        blk = jnp.broadcast_to(row, (16, LANES)).astype(BF16)
        return jnp.concatenate([blk] * rep, axis=0)

    ls = slice(lt * LANES, (lt + 1) * LANES)
    w = [jnp.zeros((PEER_N_KEYS, LANES), BF16) for _ in range(ng)]
    for h in range(PEER_HEADS):
        c1 = c1_ref[h, pl.ds(key0, _KEY_GROUP), ls]
        e1 = e1_ref[h, pl.ds(key0, _KEY_GROUP), ls]
        r2 = r2_ref[h, :, ls]
        e2 = e2_ref[h, :, ls]
        for g in range(ng):
            thr = rows16(c1[g0 + g:g0 + g + 1])
            gate = rows16(e1[g0 + g:g0 + g + 1])
            w[g] = w[g] + jnp.where(r2 < thr, e2, jnp.zeros_like(e2)) * gate
    for g in range(g0, g0 + ng):
        rs = slice(g * PEER_N_KEYS, (g + 1) * PEER_N_KEYS)
        a = a_ref[rs, ls]
        gelu = 0.5 * a * (1.0 + lax.erf(a * (2.0 ** -0.5)))
        p_ref[rs, ls] = gelu.astype(BF16) * w[g - g0]


def _experts_kernel(h_ref, dn_ref, upt_ref, c1_ref, e1_ref, r2_ref, e2_ref, x_ref, *rest, emit_norm):
    if emit_norm:
        g_ref, o_ref, hn_ref, a_ref, p_ref, acc_ref = rest
    else:
        o_ref, a_ref, p_ref, acc_ref = rest
    te, tt = a_ref.shape
    d_model = acc_ref.shape[0]
    e = pl.program_id(1)
    n_tiles = pl.num_programs(1) - 1
    cur = e % 2
    ng = _KEY_GROUP // _UNITS
    n_lane = tt // LANES
    key0 = pl.multiple_of(jnp.minimum(e, n_tiles - 1) * _KEY_GROUP, _KEY_GROUP)

    def front_mm(u):
        rows = te // _UNITS
        rs = slice(u * rows, (u + 1) * rows)
        a_ref[rs, :] = _dot_nt(dn_ref[rs, :], h_ref[...])

    def back_mm(r):
        rows = d_model // _DRAIN_PIECES
        rs = slice(r * rows, (r + 1) * rows)
        acc_ref[rs, :] += _dot(upt_ref[rs, :], p_ref[1 - cur])

    units_per_drain = _UNITS // _DRAIN_PIECES

    def run(front, back):
        if front:
            front_mm(0)
        for u in range(_UNITS):
            for lt in range(n_lane):
                if front:
                    _build_gated(a_ref, p_ref.at[cur], c1_ref, e1_ref, r2_ref, e2_ref, key0, u * ng, ng, lt)
                if front and lt == 0 and u + 1 < _UNITS:
                    front_mm(u + 1)
                if back and lt == n_lane // 2 and (u + 1) % units_per_drain == 0:
                    back_mm(u // units_per_drain)

    @pl.when(e == 0)
    def _():
        acc_ref[...] = jnp.zeros_like(acc_ref)
        run(True, False)

    @pl.when((e > 0) & (e < n_tiles))
    def _():
        run(True, True)

    @pl.when(e == n_tiles)
    def _():
        run(False, True)
        y = x_ref[...] + acc_ref[...].T
        o_ref[...] = y
        if emit_norm:
            hn_ref[...] = _rms_normed(y, g_ref[...])


def _peer_experts(h2, down, up_t, stats, x, next_gain, tt=512):
    T, D = h2.shape
    E = down.shape[0]
    te = _KEY_GROUP * PEER_N_KEYS
    n_tiles = E // te
    emit_norm = next_gain is not None
    stat_spec = pl.BlockSpec((PEER_HEADS, PEER_N_KEYS, tt), lambda i, e: (0, 0, i))
    row_spec = pl.BlockSpec((tt, D), lambda i, e: (i, 0))
    in_specs = [row_spec,
                pl.BlockSpec((te, D), lambda i, e: (jnp.minimum(e, n_tiles - 1), 0)),
                pl.BlockSpec((None, D, te), lambda i, e: (jnp.maximum(e - 1, 0), 0, 0)),
                stat_spec, stat_spec, stat_spec, stat_spec,
                row_spec]
    args = [h2, down, up_t, *stats, x]
    out_specs = [row_spec]
    out_shape = [jax.ShapeDtypeStruct((T, D), F32)]
    if emit_norm:
        in_specs.append(pl.BlockSpec((1, D), lambda i, e: (0, 0)))
        args.append(next_gain.reshape(1, D).astype(F32))
        out_specs.append(row_spec)
        out_shape.append(jax.ShapeDtypeStruct((T, D), BF16))
    res = pl.pallas_call(
        functools.partial(_experts_kernel, emit_norm=emit_norm),
        grid=(T // tt, n_tiles + 1),
        in_specs=in_specs, out_specs=out_specs, out_shape=out_shape,
        scratch_shapes=[pltpu.VMEM((te, tt), F32),
                        pltpu.VMEM((2, te, tt), BF16),
                        pltpu.VMEM((D, tt), F32)],
        compiler_params=_cparams(("parallel", "arbitrary"), VMEM_LIMIT_EXPERTS),
        name="peer_experts",
    )(*args)
    return (res[0], res[1]) if emit_norm else (res[0], None)


def _peer_layer(x, h2, next_gain, w_query, sub_keys, down, up):
    keys = sub_keys.reshape(PEER_HEADS * 2, PEER_N_KEYS, PEER_QUERY_DIM // 2).astype(BF16)
    stats = _peer_route(h2, w_query.T.astype(BF16), keys)
    te = _KEY_GROUP * PEER_N_KEYS
    up_t = up.reshape(up.shape[0] // te, te, up.shape[1]).transpose(0, 2, 1).astype(BF16)
    return _peer_experts(h2, down.astype(BF16), up_t, stats, x, next_gain)


def _tile_heads(g, n):
    return jnp.tile(g.astype(F32), n)


def _even_mixer(x, h, ffn_gain, w_in, f_bias, qn_a, kn_a, qn_b, kn_b, w_out, tabs, *, batch, seq):
    w = w_in.astype(BF16)
    o_qa, o_ka, o_va, o_qb, o_kb, o_vb, o_gb, o_fb = (
        0, A_W, 2 * A_W, 3 * A_W, 3 * A_W + B_W, 3 * A_W + 2 * B_W, 3 * A_W + 3 * B_W, 3 * A_W + 4 * B_W)
    gain_a = jnp.concatenate([_tile_heads(qn_a, MOBA_HEADS), _tile_heads(kn_a, MOBA_HEADS)])
    qk_a, km = _proj(h, w[:, o_qa:o_va], seq=seq, tn=A_W, gain=gain_a, rope_tabs=tabs, kmean=True)
    nb = seq // MOBA_BLOCK
    kmean = km.reshape(batch, nb, 2 * A_W)[:, :, A_W:]
    kmean = jnp.pad(kmean, ((0, 0), (0, LANES - nb), (0, 0))).astype(BF16)
    gain_b = jnp.concatenate([_tile_heads(qn_b, FOX_HEADS), _tile_heads(kn_b, FOX_HEADS)])
    qk_b = _proj(h, w[:, o_qb:o_vb], seq=seq, tn=B_W, gain=gain_b)
    w_plain = jnp.concatenate([w[:, o_va:o_qb], w[:, o_vb:o_fb]], axis=1)
    vvg = _proj(h, w_plain, seq=seq, tn=A_W)
    va, vb, gb = vvg[:, :A_W], vvg[:, A_W:A_W + B_W], vvg[:, A_W + B_W:]
    frow = _fox_gates(h, w[:, o_fb:].T, f_bias, batch=batch, seq=seq)
    oa = _moba(qk_a[:, :A_W], qk_a[:, A_W:], va, kmean, batch=batch, seq=seq)
    logit_bound = 1.01 * ATTN_SCALE * HEAD_DIM * jnp.max(jnp.abs(qn_b)) * jnp.max(jnp.abs(kn_b)) + 0.01
    ob = _fox(qk_b[:, :B_W], qk_b[:, B_W:], vb, gb, frow, logit_bound, batch=batch, seq=seq)
    return _outproj([oa, ob], w_out.astype(BF16), x, ffn_gain)


def _odd_mixer(x, h, ffn_gain, w_in, qn, kn, sinks, w_out, tabs, *, batch, seq):
    w = w_in.astype(BF16)
    qw = SWA_Q_HEADS * HEAD_DIM
    kw = SWA_KV_HEADS * HEAD_DIM
    q = _proj(h, w[:, :qw], seq=seq, tn=512, gain=_tile_heads(qn, SWA_Q_HEADS), rope_tabs=tabs)
    k = _proj(h, w[:, qw:qw + kw], seq=seq, tn=kw, gain=_tile_heads(kn, SWA_KV_HEADS), rope_tabs=tabs)
    v = _proj(h, w[:, qw + kw:], seq=seq, tn=kw)
    o = _swa(q, k, v, sinks, batch=batch, seq=seq)
    return _outproj([o], w_out.astype(BF16), x, ffn_gain)


def kernel(x, attn_norm, ffn_norm, ev_w_in, ev_forget_bias, ev_q_norm_a, ev_k_norm_a, ev_q_norm_b,
           ev_k_norm_b, ev_w_out, od_w_in, od_q_norm, od_k_norm, od_sinks, od_w_out,
           peer_w_query, peer_sub_keys, peer_down, peer_up):
    batch, seq, d_model = x.shape
    depth = attn_norm.shape[0]
    tabs = _rope_tables(seq)
    xt = x.reshape(batch * seq, d_model)
    h = _rmsnorm(xt, attn_norm[0])
    for l in range(depth):
        i = l // 2
        if l % 2 == 0:
            xt, h2 = _even_mixer(xt, h, ffn_norm[l], ev_w_in[i], ev_forget_bias[i], ev_q_norm_a[i],
                                 ev_k_norm_a[i], ev_q_norm_b[i], ev_k_norm_b[i], ev_w_out[i], tabs,
                                 batch=batch, seq=seq)
        else:
            xt, h2 = _odd_mixer(xt, h, ffn_norm[l], od_w_in[i], od_q_norm[i], od_k_norm[i], od_sinks[i],
                                od_w_out[i], tabs, batch=batch, seq=seq)
        next_gain = attn_norm[l + 1] if l + 1 < depth else None
        xt, h = _peer_layer(xt, h2, next_gain, peer_w_query[l], peer_sub_keys[l], peer_down[l], peer_up[l])
    return xt.reshape(batch, seq, d_model)
```

```python
import functools

import numpy as np
import jax
import jax.numpy as jnp
from jax import lax
from jax.experimental import pallas as pl
from jax.experimental.pallas import tpu as pltpu

F32 = jnp.float32
BF16 = jnp.bfloat16

HEAD_DIM = 64
ROT_DIM = HEAD_DIM // 4
ROPE_THETA = 500000.0
ATTN_SCALE = HEAD_DIM ** -0.5
EPS = 1e-6
NEG_INF = -1e30

MOBA_HEADS = 8
FOX_HEADS = 8
MOBA_BLOCK = 256
MOBA_TOPK = 3
A_W = MOBA_HEADS * HEAD_DIM
B_W = FOX_HEADS * HEAD_DIM

SWA_Q_HEADS = 16
SWA_KV_HEADS = 2
SWA_WINDOW = 128

PEER_HEADS = 8
PEER_N_KEYS = 128
PEER_TOPK = 16
PEER_QUERY_DIM = 128

LANES = 128
PAIR_W = 2 * HEAD_DIM
VMEM_LIMIT = 48 * 1024 * 1024
VMEM_LIMIT_EXPERTS = 58 * 1024 * 1024


def _cparams(sem, vmem_limit=VMEM_LIMIT):
    return pltpu.CompilerParams(dimension_semantics=sem, vmem_limit_bytes=vmem_limit)


def _dot_nt(a, b):
    return lax.dot_general(a, b, (((1,), (1,)), ((), ())), preferred_element_type=F32)


def _dot(a, b):
    return jnp.dot(a, b, preferred_element_type=F32)


def _split2(x):
    h1 = x.astype(BF16)
    return h1, (x - h1.astype(F32)).astype(BF16)


def _split3(x):
    h1 = x.astype(BF16)
    r1 = x - h1.astype(F32)
    h2 = r1.astype(BF16)
    h3 = (r1 - h2.astype(F32)).astype(BF16)
    return h1, h2, h3


def _rmsnorm_kernel(x_ref, g_ref, o_ref):
    x = x_ref[...]
    ms = jnp.mean(x * x, axis=-1, keepdims=True)
    o_ref[...] = (x * lax.rsqrt(ms + EPS) * g_ref[...]).astype(o_ref.dtype)


def _rmsnorm(x, gain, tm=512):
    T, D = x.shape
    return pl.pallas_call(
        _rmsnorm_kernel,
        grid=(T // tm,),
        in_specs=[pl.BlockSpec((tm, D), lambda i: (i, 0)),
                  pl.BlockSpec((1, D), lambda i: (0, 0))],
        out_specs=pl.BlockSpec((tm, D), lambda i: (i, 0)),
        out_shape=jax.ShapeDtypeStruct((T, D), BF16),
        compiler_params=_cparams(("parallel",)),
        name="rmsnorm",
    )(x, gain.reshape(1, D))


def _proj_kernel(*refs, norm, rope, kmean, tn):
    it = iter(refs)
    h_ref, w_ref = next(it), next(it)
    gain_ref = next(it) if norm else None
    bd_ref = next(it) if norm else None
    if rope:
        c_ref, sa_ref, sb_ref = next(it), next(it), next(it)
    o_ref = next(it)
    km_ref = next(it) if kmean else None

    y = _dot(h_ref[...], w_ref[...])
    if norm:
        y2 = y * y
        bd = bd_ref[...]
        cols = []
        tm = y.shape[0]
        for c in range(tn // LANES):
            h1, h2 = _split2(y2[:, c * LANES:(c + 1) * LANES])
            t = _dot(jnp.concatenate([h1, h2], axis=0), bd)
            cols.append(t[:tm] + t[tm:])
        ms = cols[0] if len(cols) == 1 else jnp.concatenate(cols, axis=1)
        y = y * lax.rsqrt(ms + EPS) * gain_ref[...]
    if rope:
        rep = tn // LANES
        tile = (lambda t: t) if rep == 1 else (lambda t: jnp.concatenate([t] * rep, axis=1))
        y = (y * tile(c_ref[...])
             + pltpu.roll(y, tn - ROT_DIM // 2, 1) * tile(sa_ref[...])
             + pltpu.roll(y, ROT_DIM // 2, 1) * tile(sb_ref[...]))
    o_ref[...] = y.astype(o_ref.dtype)
    if kmean:
        for r in range(km_ref.shape[0]):
            km_ref[r] = jnp.mean(y[r * MOBA_BLOCK:(r + 1) * MOBA_BLOCK], axis=0, keepdims=True)


def _proj(h, w, *, seq, tn, tm=512, gain=None, rope_tabs=None, kmean=False):
    T, D = h.shape
    N = w.shape[1]
    norm = gain is not None
    rope = rope_tabs is not None
    nseq = seq // tm
    in_specs = [pl.BlockSpec((tm, D), lambda i, j: (i, 0)),
                pl.BlockSpec((D, tn), lambda i, j: (0, j))]
    args = [h, w]
    if norm:
        bd = np.kron(np.eye(LANES // HEAD_DIM), np.ones((HEAD_DIM, HEAD_DIM))) / HEAD_DIM
        in_specs += [pl.BlockSpec((1, tn), lambda i, j: (0, j)),
                     pl.BlockSpec((LANES, LANES), lambda i, j: (0, 0))]
        args += [gain.reshape(1, N).astype(F32), jnp.asarray(bd, BF16)]
    if rope:
        in_specs += [pl.BlockSpec((tm, LANES), lambda i, j: (i % nseq, 0))] * 3
        args += list(rope_tabs)
    out_specs = [pl.BlockSpec((tm, tn), lambda i, j: (i, j))]
    out_shape = [jax.ShapeDtypeStruct((T, N), BF16)]
    if kmean:
        out_specs.append(pl.BlockSpec((tm // MOBA_BLOCK, 1, tn), lambda i, j: (i, 0, j)))
        out_shape.append(jax.ShapeDtypeStruct((T // MOBA_BLOCK, 1, N), F32))
    res = pl.pallas_call(
        functools.partial(_proj_kernel, norm=norm, rope=rope, kmean=kmean, tn=tn),
        grid=(T // tm, N // tn),
        in_specs=in_specs, out_specs=out_specs, out_shape=out_shape,
        compiler_params=_cparams(("parallel", "parallel")),
        name="proj",
    )(*args)
    return res if kmean else res[0]


def _rope_tables(seq):
    half = ROT_DIM // 2
    inv_freq = jnp.power(ROPE_THETA, -jnp.arange(0, ROT_DIM, 2, dtype=F32) / ROT_DIM)
    ang = jnp.arange(seq, dtype=F32)[:, None] * inv_freq[None, :]
    cos, sin = jnp.cos(ang), jnp.sin(ang)
    one = jnp.ones((seq, HEAD_DIM - ROT_DIM), F32)
    zero = jnp.zeros((seq, HEAD_DIM - ROT_DIM), F32)
    z8 = jnp.zeros((seq, half), F32)
    c = jnp.concatenate([cos, cos, one], axis=1)
    sa = jnp.concatenate([-sin, z8, zero], axis=1)
    sb = jnp.concatenate([z8, sin, zero], axis=1)
    rep = LANES // HEAD_DIM
    return tuple(jnp.concatenate([t] * rep, axis=1) for t in (c, sa, sb))


def _gates_kernel(h_ref, wf_ref, b_ref, tri_ref, o_ref, carry_ref):
    @pl.when(pl.program_id(1) == 0)
    def _():
        carry_ref[...] = jnp.zeros_like(carry_ref)

    z = _dot_nt(wf_ref[...], h_ref[...]) + b_ref[...][:, :1]
    lf = jnp.minimum(z, 0.0) - jnp.log1p(jnp.exp(-jnp.abs(z)))
    tri = tri_ref[...]
    h1, h2, h3 = _split3(lf)
    cs = _dot(h1, tri) + _dot(h2, tri) + _dot(h3, tri) + carry_ref[...][:, :1]
    o_ref[0] = cs
    carry_ref[...] = jnp.broadcast_to(cs[:, -1:], carry_ref.shape)


def _fox_gates(h, wf_t, bias, *, batch, seq, tm=512):
    T, D = h.shape
    nh = wf_t.shape[0]
    nseq = seq // tm
    tri = jnp.asarray(np.triu(np.ones((tm, tm))), BF16)
    return pl.pallas_call(
        _gates_kernel,
        grid=(batch, nseq),
        in_specs=[pl.BlockSpec((tm, D), lambda b, s: (b * nseq + s, 0)),
                  pl.BlockSpec((nh, D), lambda b, s: (0, 0)),
                  pl.BlockSpec((nh, LANES), lambda b, s: (0, 0)),
                  pl.BlockSpec((tm, tm), lambda b, s: (0, 0))],
        out_specs=pl.BlockSpec((1, nh, tm), lambda b, s: (b, 0, s)),
        out_shape=jax.ShapeDtypeStruct((batch, nh, seq), F32),
        scratch_shapes=[pltpu.VMEM((nh, LANES), F32)],
        compiler_params=_cparams(("parallel", "arbitrary")),
        name="fox_gates",
    )(h, wf_t, jnp.broadcast_to(bias.astype(F32)[:, None], (nh, LANES)), tri)


def _lane_tile(x, width):
    rep = width // LANES
    return x if rep == 1 else jnp.concatenate([x] * rep, axis=1)


def _flash_init(m_ref, acc_ref):
    m_ref[...] = jnp.full(m_ref.shape, NEG_INF, F32)
    acc_ref[...] = jnp.zeros(acc_ref.shape, F32)


def _head_values(v):
    lane = lax.broadcasted_iota(jnp.int32, v.shape, 1)
    return [jnp.where((lane >= hh * HEAD_DIM) & (lane < (hh + 1) * HEAD_DIM), v, jnp.ones_like(v))
            for hh in range(2)]


def _flash_update(slot, s, v, m_ref, acc_ref):
    tk = s.shape[1]
    m_prev = m_ref[slot]
    m_new = jnp.maximum(m_prev, jnp.max(s, axis=1, keepdims=True))
    alpha = jnp.exp(m_prev - m_new)
    p = jnp.exp(s - _lane_tile(m_new, tk))
    acc_ref[slot] = alpha * acc_ref[slot] + _dot(p.astype(BF16), v)
    m_ref[slot] = m_new


def _flash_finish(lane, acc_ref):
    outs = []
    for hh in range(2):
        acc = acc_ref[hh]
        den = (1 - hh) * HEAD_DIM
        outs.append(acc / acc[:, den:den + 1])
    return jnp.where(lane < HEAD_DIM, outs[0], outs[1])


def _flash_tiles(qi, tk, logits, values, m_ref, acc_ref):
    def absorb(s, off):
        vh = values(off)
        for hh in range(2):
            _flash_update(hh, s[hh], vh[hh], m_ref, acc_ref)

    def pair(off_a, off_b, b_diagonal):
        sa = logits(off_a, False)
        sb = logits(off_b, b_diagonal)
        absorb(sa, off_a)
        absorb(sb, off_b)

    def body(jj, carry):
        off = pl.multiple_of(2 * jj * tk, 2 * tk)
        pair(off, off + tk, False)
        return carry

    lax.fori_loop(0, qi // 2, body, 0)
    diag = pl.multiple_of(qi * tk, tk)

    @pl.when(qi % 2 == 1)
    def _():
        pair(diag - tk, diag, True)

    @pl.when(qi % 2 == 0)
    def _():
        absorb(logits(diag, True), diag)


def _head_queries(q, lane):
    qs = q * ATTN_SCALE
    return [jnp.where((lane >= hh * HEAD_DIM) & (lane < (hh + 1) * HEAD_DIM), qs, jnp.zeros_like(qs))
            for hh in range(2)]


def _moba_kernel(q_ref, k_ref, v_ref, km_ref, o_ref, m_ref, acc_ref):
    tq = q_ref.shape[1]
    tk = tq
    qi = pl.program_id(2)
    lane = lax.broadcasted_iota(jnp.int32, (tq, LANES), 1)
    lane_f = lane.astype(F32)
    rowv = lax.broadcasted_iota(jnp.int32, (tq, LANES), 0)
    row_blk = 2 * qi + (rowv >= MOBA_BLOCK).astype(jnp.int32)
    row = lax.broadcasted_iota(jnp.int32, (tq, tk), 0)
    col = lax.broadcasted_iota(jnp.int32, (tq, tk), 1)
    qh = _head_queries(q_ref[0], lane)
    _flash_init(m_ref, acc_ref)

    sels = []
    for hh in range(2):
        gate = _dot_nt(qh[hh], km_ref[0])
        gate = jnp.where(lane < row_blk, gate, -jnp.inf)
        sel = jnp.zeros((tq, LANES), F32)
        for _ in range(MOBA_TOPK):
            m = jnp.max(gate, axis=1, keepdims=True)
            idx = jnp.min(jnp.where(gate == m, lane_f, float(LANES)), axis=1, keepdims=True)
            hit = lane_f == idx
            sel = jnp.where(hit & (m > -jnp.inf), 1.0, sel)
            gate = jnp.where(hit, -jnp.inf, gate)
        sels.append(sel)

    def chosen(sel, blk):
        return jnp.max(jnp.where(lane == blk, sel, 0.0), axis=1, keepdims=True) > 0.0

    def logits(off, diagonal):
        kj = k_ref[0, pl.ds(off, tk), :]
        out = []
        for hh in range(2):
            s = _dot_nt(qh[hh], kj)
            if diagonal:
                visible = (col >= MOBA_BLOCK) | (row < MOBA_BLOCK) | chosen(sels[hh], 2 * qi)
                s = jnp.where((col <= row) & visible, s, NEG_INF)
            else:
                blk = 2 * (off // tk)
                s = jnp.concatenate(
                    [jnp.where(chosen(sels[hh], blk), s[:, :MOBA_BLOCK], NEG_INF),
                     jnp.where(chosen(sels[hh], blk + 1), s[:, MOBA_BLOCK:], NEG_INF)], axis=1)
            out.append(s)
        return out

    def values(off):
        return _head_values(v_ref[0, pl.ds(off, tk), :])

    _flash_tiles(qi, tk, logits, values, m_ref, acc_ref)
    o_ref[0] = _flash_finish(lane, acc_ref).astype(o_ref.dtype)


def _flash_scratch(tq):
    return [pltpu.VMEM((2, tq, LANES), F32)] * 2


def _moba(q, k, v, kmean, *, batch, seq):
    W = q.shape[1]
    tq = 2 * MOBA_BLOCK
    q3, k3, v3 = (t.reshape(batch, seq, W) for t in (q, k, v))
    out = pl.pallas_call(
        _moba_kernel,
        grid=(batch, W // PAIR_W, seq // tq),
        in_specs=[pl.BlockSpec((1, tq, PAIR_W), lambda b, p, i: (b, i, p)),
                  pl.BlockSpec((1, seq, PAIR_W), lambda b, p, i: (b, 0, p)),
                  pl.BlockSpec((1, seq, PAIR_W), lambda b, p, i: (b, 0, p)),
                  pl.BlockSpec((1, LANES, PAIR_W), lambda b, p, i: (b, 0, p))],
        out_specs=pl.BlockSpec((1, tq, PAIR_W), lambda b, p, i: (b, i, p)),
        out_shape=jax.ShapeDtypeStruct((batch, seq, W), BF16),
        scratch_shapes=_flash_scratch(tq),
        compiler_params=_cparams(("parallel", "parallel", "parallel")),
        name="moba",
    )(q3, k3, v3, kmean)
    return out.reshape(batch * seq, W)


_EXP_UNDERFLOW = 112.0


def _fox_kernel(q_ref, k_ref, v_ref, g_ref, frow_ref, far_ref, o_ref, m_ref, acc_ref):
    tq = q_ref.shape[1]
    tk = tq
    pr = pl.program_id(1)
    qi = pl.program_id(2)
    lane = lax.broadcasted_iota(jnp.int32, (tq, LANES), 1)
    row = lax.broadcasted_iota(jnp.int32, (tq, tk), 0)
    col = lax.broadcasted_iota(jnp.int32, (tq, tk), 1)
    qh = _head_queries(q_ref[0], lane)
    _flash_init(m_ref, acc_ref)

    def key_gates(off, width):
        f_all = frow_ref[0, :, pl.ds(off, width)]
        sub = lax.broadcasted_iota(jnp.int32, f_all.shape, 0)
        return [jnp.sum(jnp.where(sub == 2 * pr + hh, f_all, 0.0), axis=0, keepdims=True) for hh in range(2)]

    f_ref = [f[:, :1] for f in key_gates(pl.multiple_of(qi * tq, tq), LANES)]

    def logits(off, diagonal):
        kj = k_ref[0, pl.ds(off, tk), :]
        fk = key_gates(off, tk)
        out = []
        for hh in range(2):
            s = _dot_nt(qh[hh], kj) - (fk[hh] - f_ref[hh])
            out.append(jnp.where(col <= row, s, NEG_INF) if diagonal else s)
        return out

    def absorb(s, off):
        vh = _head_values(v_ref[0, pl.ds(off, tk), :])
        for hh in range(2):
            _flash_update(hh, s[hh], vh[hh], m_ref, acc_ref)

    def vanishes(j):
        tail = frow_ref[0, :, pl.ds(pl.multiple_of((j + 1) * tk - LANES, LANES), LANES)]
        sub = lax.broadcasted_iota(jnp.int32, tail.shape, 0)
        last = lax.broadcasted_iota(jnp.int32, tail.shape, 1) == LANES - 1
        gap = tail - jnp.where(sub == 2 * pr, f_ref[0], f_ref[1]) - far_ref[...][:, :1]
        mine = last & ((sub == 2 * pr) | (sub == 2 * pr + 1))
        return jnp.min(jnp.where(mine, gap, jnp.inf)) >= 0.0

    absorb(logits(pl.multiple_of(qi * tk, tk), True), pl.multiple_of(qi * tk, tk))

    def more(j):
        return jnp.logical_and(j >= 1, jnp.logical_not(vanishes(jnp.maximum(j, 0))))

    def pair(j):
        off_a = pl.multiple_of(j * tk, tk)
        off_b = pl.multiple_of((j - 1) * tk, tk)
        sa = logits(off_a, False)
        sb = logits(off_b, False)
        absorb(sa, off_a)
        absorb(sb, off_b)
        return j - 2

    j = lax.while_loop(more, pair, qi - 1)

    @pl.when(jnp.logical_and(j == 0, jnp.logical_not(vanishes(0))))
    def _():
        absorb(logits(0, False), 0)

    o = _flash_finish(lane, acc_ref)
    o_ref[0] = (o * jax.nn.sigmoid(g_ref[0].astype(F32))).astype(o_ref.dtype)


def _fox(q, k, v, g, frow, logit_bound, *, batch, seq, tq=512):
    far = jnp.full((1, LANES), 2.0 * logit_bound + _EXP_UNDERFLOW, F32)
    W = q.shape[1]
    nh = frow.shape[1]
    q3, k3, v3, g3 = (t.reshape(batch, seq, W) for t in (q, k, v, g))
    out = pl.pallas_call(
        _fox_kernel,
        grid=(batch, W // PAIR_W, seq // tq),
        in_specs=[pl.BlockSpec((1, tq, PAIR_W), lambda b, p, i: (b, i, p)),
                  pl.BlockSpec((1, seq, PAIR_W), lambda b, p, i: (b, 0, p)),
                  pl.BlockSpec((1, seq, PAIR_W), lambda b, p, i: (b, 0, p)),
                  pl.BlockSpec((1, tq, PAIR_W), lambda b, p, i: (b, i, p)),
                  pl.BlockSpec((1, nh, seq), lambda b, p, i: (b, 0, 0)),
                  pl.BlockSpec((1, LANES), lambda b, p, i: (0, 0))],
        out_specs=pl.BlockSpec((1, tq, PAIR_W), lambda b, p, i: (b, i, p)),
        out_shape=jax.ShapeDtypeStruct((batch, seq, W), BF16),
        scratch_shapes=_flash_scratch(tq),
        compiler_params=_cparams(("parallel", "parallel", "parallel")),
        name="fox",
    )(q3, k3, v3, g3, frow, far)
    return out.reshape(batch * seq, W)


def _swa_kernel(q_ref, k_ref, v_ref, sink_ref, bias_ref, o_ref):
    tq = q_ref.shape[1]
    qi = pl.program_id(1)
    group = SWA_Q_HEADS // SWA_KV_HEADS
    tk = tq + SWA_WINDOW
    lane = lax.broadcasted_iota(jnp.int32, (tq, LANES), 1)
    kstart = pl.multiple_of(jnp.maximum(qi * tq - SWA_WINDOW, 0), SWA_WINDOW)
    k = k_ref[0, pl.ds(kstart, tk), :]
    vh = _head_values(v_ref[0, pl.ds(kstart, tk), :])
    swap = lambda t: pltpu.roll(t.astype(F32), HEAD_DIM, 1).astype(BF16)
    k_by_half = [k, swap(k)]
    bias = bias_ref[jnp.minimum(qi, 1)]
    bias = jnp.concatenate([bias] * (group // 2), axis=0)
    sink_tab = sink_ref[...]

    chains = []
    for c in range(SWA_KV_HEADS):
        v_by_half = [vh[c], swap(vh[c])]
        for swapped in range(2):
            hh = c if not swapped else 1 - c
            heads = [h for h in range(c * group, (c + 1) * group) if h % 2 == hh]
            pieces, sinks = [], []
            for head in heads:
                blk = q_ref[0, :, (head // 2) * PAIR_W:(head // 2 + 1) * PAIR_W] * ATTN_SCALE
                pieces.append(jnp.where((lane >= hh * HEAD_DIM) & (lane < (hh + 1) * HEAD_DIM),
                                        blk, jnp.zeros_like(blk)))
                sinks.append(jnp.broadcast_to(sink_tab[head:head + 1, :], (tq, LANES)))
            s = _dot_nt(jnp.concatenate(pieces, axis=0), k_by_half[swapped]) + bias
            chains.append((heads, s, jnp.concatenate(sinks, axis=0), v_by_half[swapped]))

    outs = [None] * SWA_Q_HEADS
    for heads, s, sink, v in chains:
        m = jnp.maximum(sink, jnp.max(s, axis=1, keepdims=True))
        p = jnp.exp(s - _lane_tile(m, tk))
        acc = _dot(p.astype(BF16), v)
        den = pltpu.roll(acc, HEAD_DIM, 1) + jnp.exp(sink - m)
        o = acc / den
        for idx, head in enumerate(heads):
            outs[head] = o[idx * tq:(idx + 1) * tq]
    for pp in range(SWA_Q_HEADS // 2):
        o_ref[0, :, pp * PAIR_W:(pp + 1) * PAIR_W] = jnp.where(
            lane < HEAD_DIM, outs[2 * pp], outs[2 * pp + 1]).astype(o_ref.dtype)


def _swa_bias(tq):
    r = np.arange(tq)[:, None]
    c = np.arange(tq + SWA_WINDOW)[None, :]
    tabs = []
    for key_offset in (0, SWA_WINDOW):
        dist = r + key_offset - c
        tabs.append(np.where((dist >= 0) & (dist < SWA_WINDOW), 0.0, NEG_INF))
    return jnp.asarray(np.stack(tabs), F32)


def _swa(q, k, v, sinks, *, batch, seq, tq=SWA_WINDOW):
    W = q.shape[1]
    tk = tq + SWA_WINDOW
    q3 = q.reshape(batch, seq, W)
    k3, v3 = (t.reshape(batch, seq, PAIR_W) for t in (k, v))
    sink_tab = jnp.broadcast_to(sinks.astype(F32)[:, None], (SWA_Q_HEADS, LANES))
    out = pl.pallas_call(
        _swa_kernel,
        grid=(batch, seq // tq),
        in_specs=[pl.BlockSpec((1, tq, W), lambda b, i: (b, i, 0)),
                  pl.BlockSpec((1, seq, PAIR_W), lambda b, i: (b, 0, 0)),
                  pl.BlockSpec((1, seq, PAIR_W), lambda b, i: (b, 0, 0)),
                  pl.BlockSpec((SWA_Q_HEADS, LANES), lambda b, i: (0, 0)),
                  pl.BlockSpec((2, tq, tk), lambda b, i: (0, 0, 0))],
        out_specs=pl.BlockSpec((1, tq, W), lambda b, i: (b, i, 0)),
        out_shape=jax.ShapeDtypeStruct((batch, seq, W), BF16),
        compiler_params=_cparams(("parallel", "parallel")),
        name="swa",
    )(q3, k3, v3, sink_tab, _swa_bias(tq))
    return out.reshape(batch * seq, W)


def _rms_normed(x, gain):
    ms = jnp.mean(x * x, axis=-1, keepdims=True)
    return (x * lax.rsqrt(ms + EPS) * gain).astype(BF16)


def _outproj_kernel(*refs, n_parts):
    parts = refs[:n_parts]
    w_ref, x_ref, g_ref, o_ref, h_ref = refs[n_parts:]
    y = x_ref[...]
    off = 0
    for p_ref in parts:
        kw = p_ref.shape[1]
        y = y + _dot(p_ref[...], w_ref[off:off + kw, :])
        off += kw
    o_ref[...] = y
    h_ref[...] = _rms_normed(y, g_ref[...])


def _outproj(parts, w, x, next_gain, tm=512):
    T, D = x.shape
    in_specs = [pl.BlockSpec((tm, p.shape[1]), lambda i: (i, 0)) for p in parts]
    in_specs += [pl.BlockSpec(w.shape, lambda i: (0, 0)),
                 pl.BlockSpec((tm, D), lambda i: (i, 0)),
                 pl.BlockSpec((1, D), lambda i: (0, 0))]
    row_spec = pl.BlockSpec((tm, D), lambda i: (i, 0))
    return pl.pallas_call(
        functools.partial(_outproj_kernel, n_parts=len(parts)),
        grid=(T // tm,),
        in_specs=in_specs,
        out_specs=[row_spec, row_spec],
        out_shape=[jax.ShapeDtypeStruct((T, D), F32), jax.ShapeDtypeStruct((T, D), BF16)],
        compiler_params=_cparams(("parallel",)),
        name="outproj",
    )(*parts, w, x, next_gain.reshape(1, D).astype(F32))


_CAND_ROWS = 80


def _cand_tables(lanes):
    pos = np.zeros((_CAND_ROWS,), np.float32)
    neg = np.zeros((_CAND_ROWS,), np.float32)
    r = 0
    for a, nb in ((0, 16), (1, 8), (2, 8), (3, 8), (4, 8), (5, 8), (6, 8), (7, 8)):
        for b in range(nb):
            pos[r] = a * PEER_TOPK + b
            neg[r] = 0.0 if (a + 1) * (b + 1) <= PEER_TOPK else -np.inf
            r += 1
    for a in range(8, 16):
        pos[r] = a * PEER_TOPK
        r += 1
    assert r == _CAND_ROWS
    tab = lambda t: jnp.asarray(np.broadcast_to(t[:, None], (_CAND_ROWS, lanes)).copy())
    return tab(pos), tab(neg)


def _batcher_pairs(n):
    pairs, p = [], 1
    while p < n:
        k = p
        while k >= 1:
            for j in range(k % p, n - k, 2 * k):
                for i in range(min(k, n - j - k)):
                    if (i + j) // (2 * p) == (i + j + k) // (2 * p):
                        pairs.append((i + j, i + j + k))
            k //= 2
        p *= 2
    return pairs


_SORT16 = _batcher_pairs(PEER_TOPK)
_N_CAND_PIECES = 10
_SORT10 = [(i, j) for i, j in _SORT16 if j < _N_CAND_PIECES]
_SUBLANES = 8


def _compare_exchange(items, i, j):
    items[i], items[j] = jnp.maximum(items[i], items[j]), jnp.minimum(items[i], items[j])


def _top_sorted(pieces, pairs):
    items = list(pieces)
    for i, j in pairs:
        _compare_exchange(items, i, j)
    n = PEER_TOPK
    items += [jnp.full(items[0].shape, -jnp.inf, F32)] * (n - len(items))
    for shift in (4, 2, 1):
        items = [jnp.maximum(items[i], pltpu.roll(items[n - 1 - i], shift, 0)) for i in range(n)]
        d = n // 2
        while d >= 1:
            for i in range(n):
                if i & d == 0:
                    _compare_exchange(items, i, i + d)
            d //= 2
    return items


def _pieces(x):
    return [x[_SUBLANES * g:_SUBLANES * (g + 1)] for g in range(x.shape[0] // _SUBLANES)]


def _sublane_total(x):
    for shift in (4, 2, 1):
        x = x + pltpu.roll(x, shift, 0)
    return x


def _count_ge(pieces, thr):
    total = jnp.zeros(thr.shape, F32)
    for p in pieces:
        total = total + jnp.where(p >= thr, 1.0, 0.0)
    return _sublane_total(total)


def _route_head_fast(s1, s2):
    p1, p2 = _pieces(s1), _pieces(s2)
    v1 = _top_sorted(p1, _SORT16)
    v2 = _top_sorted(p2, _SORT16)
    sub = lax.broadcasted_iota(jnp.int32, v1[0].shape, 0)

    def spread(vals):
        out = vals[0]
        for r in range(1, _SUBLANES):
            out = jnp.where(sub == r, vals[r], out)
        return out

    v2_lo, v2_hi, v1_hi = spread(v2[:8]), spread(v2[8:]), spread(v1[8:])
    cands = [v1[0] + v2_lo, v1[0] + v2_hi, v1[1] + v2_lo]
    for a in range(2, 8):
        cands.append(jnp.where(sub < PEER_TOPK // (a + 1), v1[a] + v2_lo, -jnp.inf))
    cands.append(v1_hi + v2[0])
    ts = _top_sorted(cands, _SORT10)
    tau = ts[PEER_TOPK - 1]
    z = jnp.exp(ts[0] - ts[0])
    for kk in range(1, PEER_TOPK):
        z = z + jnp.exp(ts[kk] - ts[0])

    tied = (_count_ge(p1, v1[-1]) != float(PEER_TOPK)) | (_count_ge(p2, v2[-1]) != float(PEER_TOPK))
    tied = tied | (_count_ge(cands, tau) != float(PEER_TOPK))
    for b in range(PEER_TOPK - 1):
        tied = tied | (v1[b] == v1[b + 1]) | (v2[b] == v2[b + 1])

    cnt = []
    for a in range(PEER_TOPK):
        c = jnp.zeros(tau.shape, F32)
        for b in range(PEER_TOPK // (a + 1)):
            c = c + jnp.where(v1[a] + v2[b] >= tau, 1.0, 0.0)
        cnt.append(c)
    c1, r2 = [], []
    for x in p1:
        c = jnp.zeros(x.shape, F32)
        for a in range(PEER_TOPK):
            c = jnp.where(x == v1[a], cnt[a], c)
        c1.append(c)
    for x in p2:
        r = jnp.zeros(x.shape, F32)
        for b in range(PEER_TOPK):
            r = r + jnp.where(v2[b] > x, 1.0, 0.0)
        r2.append(r)
    inv_z = 1.0 / z
    e1 = [jnp.exp(x - v1[0]) * inv_z for x in p1]
    e2 = [jnp.exp(x - v2[0]) for x in p2]
    cat = lambda ps: jnp.concatenate(ps, axis=0)
    return (cat(c1), cat(e1), cat(r2), cat(e2)), tied


def _extract_sorted(scores, by_key):
    nk, lanes = scores[0].shape
    kio = lax.broadcasted_iota(jnp.int32, (nk, lanes), 0).astype(F32)
    slot = lax.broadcasted_iota(jnp.int32, (PEER_TOPK, lanes), 0)

    def body(a, carry):
        here = slot == a
        out = []
        for (v, vals, aux), ranked in zip(carry, by_key):
            m = jnp.max(v, axis=0, keepdims=True)
            idx = jnp.min(jnp.where(v == m, kio, float(nk)), axis=0, keepdims=True)
            hit = kio == idx
            aux = jnp.where(hit, jnp.asarray(a, F32), aux) if ranked else jnp.where(here, idx, aux)
            out.append((jnp.where(hit, -jnp.inf, v), jnp.where(here, m, vals), aux))
        return tuple(out)

    small = jnp.zeros((PEER_TOPK, lanes), F32)
    unranked = jnp.full((nk, lanes), float(PEER_TOPK), F32)
    init = tuple((v, small, unranked if ranked else small) for v, ranked in zip(scores, by_key))
    return [(vals, aux) for _, vals, aux in lax.fori_loop(0, PEER_TOPK, body, init)]


def _route_head_exact(s1, s2, pos, neg):
    lanes = s1.shape[1]
    slot = lax.broadcasted_iota(jnp.int32, (PEER_TOPK, lanes), 0)
    kio = lax.broadcasted_iota(jnp.int32, (PEER_N_KEYS, lanes), 0).astype(F32)
    (v1, idx1), (v2, rank2) = _extract_sorted([s1, s2], [False, True])
    blocks = [v1[0:1] + v2[0:8], v1[0:1] + v2[8:16]]
    blocks += [v1[a:a + 1] + v2[0:8] for a in range(1, 8)]
    blocks += [v1[8:16] + v2[0:1]]
    cand = jnp.concatenate(blocks, axis=0) + neg

    def pick(kk, carry):
        cand, chosen, ts = carry
        m = jnp.max(cand, axis=0, keepdims=True)
        first = jnp.min(jnp.where(cand == m, pos, 1e9), axis=0, keepdims=True)
        hit = pos == first
        return (jnp.where(hit, -jnp.inf, cand), jnp.where(hit, 1.0, chosen), jnp.where(slot == kk, m, ts))

    _, chosen, ts = lax.fori_loop(0, PEER_TOPK, pick,
                                  (cand, jnp.zeros_like(cand), jnp.zeros((PEER_TOPK, lanes), F32)))
    z = jnp.sum(jnp.exp(ts - ts[0:1]), axis=0, keepdims=True)
    counts = [jnp.sum(chosen[0:16], axis=0, keepdims=True)]
    counts += [jnp.sum(chosen[8 * a + 8:8 * a + 16], axis=0, keepdims=True) for a in range(1, 8)]
    counts += [chosen[72 + a:73 + a] for a in range(8)]
    c1 = jnp.zeros((PEER_N_KEYS, lanes), F32)
    for a in range(PEER_TOPK):
        c1 = jnp.where(kio == idx1[a:a + 1], counts[a], c1)
    return c1, jnp.exp(s1 - v1[0:1]) / z, rank2, jnp.exp(s2 - v2[0:1])


def _route_kernel(h_ref, wq_ref, keys_ref, pos_ref, neg_ref,
                  c1_ref, e1_ref, r2_ref, e2_ref, qt_ref, sc_ref):
    half = PEER_QUERY_DIM // 2
    qt_ref[...] = _dot_nt(wq_ref[...], h_ref[...]).astype(BF16)

    def store(h, maps):
        c1, e1, r2, e2 = maps
        for lt in range(c1.shape[1] // LANES):
            ls = slice(lt * LANES, (lt + 1) * LANES)
            c1_ref[h, lt] = c1[:, ls]
            e1_ref[h, lt] = e1[:, ls]
            r2_ref[h, lt] = r2[:, ls].astype(BF16)
            e2_ref[h, lt] = e2[:, ls].astype(BF16)

    def head_body(h, _):
        r0 = pl.multiple_of(h * PEER_QUERY_DIM, PEER_QUERY_DIM)
        sc_ref[0] = _dot(keys_ref[2 * h], qt_ref[pl.ds(r0, half), :])
        sc_ref[1] = _dot(keys_ref[2 * h + 1], qt_ref[pl.ds(r0 + half, half), :])
        maps, tied = _route_head_fast(sc_ref[0], sc_ref[1])
        any_tied = jnp.max(jnp.where(tied, 1.0, 0.0)) > 0.0

        @pl.when(any_tied)
        def _():
            store(h, _route_head_exact(sc_ref[0], sc_ref[1], pos_ref[...], neg_ref[...]))

        @pl.when(jnp.logical_not(any_tied))
        def _():
            store(h, maps)

        return 0

    lax.fori_loop(0, PEER_HEADS, head_body, 0)


def _peer_route(h2, wq_t, keys, tt=2 * LANES):
    T, D = h2.shape
    pos, neg = _cand_tables(tt)
    stat_spec = pl.BlockSpec((PEER_HEADS, tt // LANES, PEER_N_KEYS, LANES), lambda i: (0, i, 0, 0))
    stat = lambda dt: jax.ShapeDtypeStruct((PEER_HEADS, T // LANES, PEER_N_KEYS, LANES), dt)
    return pl.pallas_call(
        _route_kernel,
        grid=(T // tt,),
        in_specs=[pl.BlockSpec((tt, D), lambda i: (i, 0)),
                  pl.BlockSpec(wq_t.shape, lambda i: (0, 0)),
                  pl.BlockSpec(keys.shape, lambda i: (0, 0, 0)),
                  pl.BlockSpec((_CAND_ROWS, tt), lambda i: (0, 0)),
                  pl.BlockSpec((_CAND_ROWS, tt), lambda i: (0, 0))],
        out_specs=[stat_spec] * 4,
        out_shape=[stat(F32), stat(F32), stat(BF16), stat(BF16)],
        scratch_shapes=[pltpu.VMEM((PEER_HEADS * PEER_QUERY_DIM, tt), BF16),
                        pltpu.VMEM((2, PEER_N_KEYS, tt), F32)],
        compiler_params=_cparams(("parallel",)),
        name="peer_route",
    )(h2, wq_t, keys, pos, neg)


_KEY_GROUP = 16
_UNITS = 4
_DRAIN_PIECES = 2


def _build_gated(a_ref, p_ref, c1_ref, e1_ref, r2_ref, e2_ref, key0, g0, ng, lt):
    rep = PEER_N_KEYS // 16

    def rows16(row):
        blk = jnp.broadcast_to(row, (16, LANES)).astype(BF16)
        return jnp.concatenate([blk] * rep, axis=0)

    ls = slice(lt * LANES, (lt + 1) * LANES)
    w = [jnp.zeros((PEER_N_KEYS, LANES), BF16) for _ in range(ng)]
    for h in range(PEER_HEADS):
        c1 = c1_ref[h, lt, pl.ds(key0, _KEY_GROUP), :]
        e1 = e1_ref[h, lt, pl.ds(key0, _KEY_GROUP), :]
        r2 = r2_ref[h, lt]
        e2 = e2_ref[h, lt]
        for g in range(ng):
            thr = rows16(c1[g0 + g:g0 + g + 1])
            gate = rows16(e1[g0 + g:g0 + g + 1])
            w[g] = w[g] + jnp.where(r2 < thr, e2, jnp.zeros_like(e2)) * gate
    for g in range(g0, g0 + ng):
        rs = slice(g * PEER_N_KEYS, (g + 1) * PEER_N_KEYS)
        a = a_ref[rs, ls]
        gelu = 0.5 * a * (1.0 + lax.erf(a * (2.0 ** -0.5)))
        p_ref[rs, ls] = gelu.astype(BF16) * w[g - g0]


def _experts_kernel(h_ref, dn_ref, upt_ref, c1_ref, e1_ref, r2_ref, e2_ref, x_ref, *rest, emit_norm):
    if emit_norm:
        g_ref, o_ref, hn_ref, a_ref, p_ref, acc_ref = rest
    else:
        o_ref, a_ref, p_ref, acc_ref = rest
    te, tt = a_ref.shape
    d_model = acc_ref.shape[0]
    e = pl.program_id(1)
    n_tiles = pl.num_programs(1) - 1
    cur = e % 2
    ng = _KEY_GROUP // _UNITS
    n_lane = tt // LANES
    key0 = pl.multiple_of(jnp.minimum(e, n_tiles - 1) * _KEY_GROUP, _KEY_GROUP)

    def front_mm(u):
        rows = te // _UNITS
        rs = slice(u * rows, (u + 1) * rows)
        a_ref[rs, :] = _dot_nt(dn_ref[rs, :], h_ref[...])

    def back_mm(r):
        rows = d_model // _DRAIN_PIECES
        rs = slice(r * rows, (r + 1) * rows)
        acc_ref[rs, :] += _dot(upt_ref[rs, :], p_ref[1 - cur])

    units_per_drain = _UNITS // _DRAIN_PIECES

    def run(front, back):
        if front:
            front_mm(0)
        for u in range(_UNITS):
            for lt in range(n_lane):
                if front:
                    _build_gated(a_ref, p_ref.at[cur], c1_ref, e1_ref, r2_ref, e2_ref, key0, u * ng, ng, lt)
                if front and lt == 0 and u + 1 < _UNITS:
                    front_mm(u + 1)
                if back and lt == n_lane // 2 and (u + 1) % units_per_drain == 0:
                    back_mm(u // units_per_drain)

    @pl.when(e == 0)
    def _():
        acc_ref[...] = jnp.zeros_like(acc_ref)
        run(True, False)

    @pl.when((e > 0) & (e < n_tiles))
    def _():
        run(True, True)

    @pl.when(e == n_tiles)
    def _():
        run(False, True)
        y = x_ref[...] + acc_ref[...].T
        o_ref[...] = y
        if emit_norm:
            hn_ref[...] = _rms_normed(y, g_ref[...])


def _peer_experts(h2, down, up_t, stats, x, next_gain, tt=512):
    T, D = h2.shape
    E = down.shape[0]
    te = _KEY_GROUP * PEER_N_KEYS
    n_tiles = E // te
    emit_norm = next_gain is not None
    stat_spec = pl.BlockSpec((PEER_HEADS, tt // LANES, PEER_N_KEYS, LANES), lambda i, e: (0, i, 0, 0))
    row_spec = pl.BlockSpec((tt, D), lambda i, e: (i, 0))
    in_specs = [row_spec,
                pl.BlockSpec((te, D), lambda i, e: (jnp.minimum(e, n_tiles - 1), 0)),
                pl.BlockSpec((None, D, te), lambda i, e: (jnp.maximum(e - 1, 0), 0, 0)),
                stat_spec, stat_spec, stat_spec, stat_spec,
                row_spec]
    args = [h2, down, up_t, *stats, x]
    out_specs = [row_spec]
    out_shape = [jax.ShapeDtypeStruct((T, D), F32)]
    if emit_norm:
        in_specs.append(pl.BlockSpec((1, D), lambda i, e: (0, 0)))
        args.append(next_gain.reshape(1, D).astype(F32))
        out_specs.append(row_spec)
        out_shape.append(jax.ShapeDtypeStruct((T, D), BF16))
    res = pl.pallas_call(
        functools.partial(_experts_kernel, emit_norm=emit_norm),
        grid=(T // tt, n_tiles + 1),
        in_specs=in_specs, out_specs=out_specs, out_shape=out_shape,
        scratch_shapes=[pltpu.VMEM((te, tt), F32),
                        pltpu.VMEM((2, te, tt), BF16),
                        pltpu.VMEM((D, tt), F32)],
        compiler_params=_cparams(("parallel", "arbitrary"), VMEM_LIMIT_EXPERTS),
        name="peer_experts",
    )(*args)
    return (res[0], res[1]) if emit_norm else (res[0], None)


def _peer_layer(x, h2, next_gain, w_query, sub_keys, down, up):
    keys = sub_keys.reshape(PEER_HEADS * 2, PEER_N_KEYS, PEER_QUERY_DIM // 2).astype(BF16)
    stats = _peer_route(h2, w_query.T.astype(BF16), keys)
    te = _KEY_GROUP * PEER_N_KEYS
    up_t = up.reshape(up.shape[0] // te, te, up.shape[1]).transpose(0, 2, 1).astype(BF16)
    return _peer_experts(h2, down.astype(BF16), up_t, stats, x, next_gain)


def _tile_heads(g, n):
    return jnp.tile(g.astype(F32), n)


def _even_mixer(x, h, ffn_gain, w_in, f_bias, qn_a, kn_a, qn_b, kn_b, w_out, tabs, *, batch, seq):
    w = w_in.astype(BF16)
    o_qa, o_ka, o_va, o_qb, o_kb, o_vb, o_gb, o_fb = (
        0, A_W, 2 * A_W, 3 * A_W, 3 * A_W + B_W, 3 * A_W + 2 * B_W, 3 * A_W + 3 * B_W, 3 * A_W + 4 * B_W)
    gain_a = jnp.concatenate([_tile_heads(qn_a, MOBA_HEADS), _tile_heads(kn_a, MOBA_HEADS)])
    qk_a, km = _proj(h, w[:, o_qa:o_va], seq=seq, tn=A_W, gain=gain_a, rope_tabs=tabs, kmean=True)
    nb = seq // MOBA_BLOCK
    kmean = km.reshape(batch, nb, 2 * A_W)[:, :, A_W:]
    kmean = jnp.pad(kmean, ((0, 0), (0, LANES - nb), (0, 0))).astype(BF16)
    gain_b = jnp.concatenate([_tile_heads(qn_b, FOX_HEADS), _tile_heads(kn_b, FOX_HEADS)])
    qk_b = _proj(h, w[:, o_qb:o_vb], seq=seq, tn=B_W, gain=gain_b)
    w_plain = jnp.concatenate([w[:, o_va:o_qb], w[:, o_vb:o_fb]], axis=1)
    vvg = _proj(h, w_plain, seq=seq, tn=A_W)
    va, vb, gb = vvg[:, :A_W], vvg[:, A_W:A_W + B_W], vvg[:, A_W + B_W:]
    frow = _fox_gates(h, w[:, o_fb:].T, f_bias, batch=batch, seq=seq)
    oa = _moba(qk_a[:, :A_W], qk_a[:, A_W:], va, kmean, batch=batch, seq=seq)
    logit_bound = 1.01 * ATTN_SCALE * HEAD_DIM * jnp.max(jnp.abs(qn_b)) * jnp.max(jnp.abs(kn_b)) + 0.01
    ob = _fox(qk_b[:, :B_W], qk_b[:, B_W:], vb, gb, frow, logit_bound, batch=batch, seq=seq)
    return _outproj([oa, ob], w_out.astype(BF16), x, ffn_gain)


def _odd_mixer(x, h, ffn_gain, w_in, qn, kn, sinks, w_out, tabs, *, batch, seq):
    w = w_in.astype(BF16)
    qw = SWA_Q_HEADS * HEAD_DIM
    kw = SWA_KV_HEADS * HEAD_DIM
    q = _proj(h, w[:, :qw], seq=seq, tn=512, gain=_tile_heads(qn, SWA_Q_HEADS), rope_tabs=tabs)
    k = _proj(h, w[:, qw:qw + kw], seq=seq, tn=kw, gain=_tile_heads(kn, SWA_KV_HEADS), rope_tabs=tabs)
    v = _proj(h, w[:, qw + kw:], seq=seq, tn=kw)
    o = _swa(q, k, v, sinks, batch=batch, seq=seq)
    return _outproj([o], w_out.astype(BF16), x, ffn_gain)


def kernel(x, attn_norm, ffn_norm, ev_w_in, ev_forget_bias, ev_q_norm_a, ev_k_norm_a, ev_q_norm_b,
           ev_k_norm_b, ev_w_out, od_w_in, od_q_norm, od_k_norm, od_sinks, od_w_out,
           peer_w_query, peer_sub_keys, peer_down, peer_up):
    batch, seq, d_model = x.shape
    depth = attn_norm.shape[0]
    tabs = _rope_tables(seq)
    xt = x.reshape(batch * seq, d_model)
    h = _rmsnorm(xt, attn_norm[0])
    for l in range(depth):
        i = l // 2
        if l % 2 == 0:
            xt, h2 = _even_mixer(xt, h, ffn_norm[l], ev_w_in[i], ev_forget_bias[i], ev_q_norm_a[i],
                                 ev_k_norm_a[i], ev_q_norm_b[i], ev_k_norm_b[i], ev_w_out[i], tabs,
                                 batch=batch, seq=seq)
        else:
            xt, h2 = _odd_mixer(xt, h, ffn_norm[l], od_w_in[i], od_q_norm[i], od_k_norm[i], od_sinks[i],
                                od_w_out[i], tabs, batch=batch, seq=seq)
        next_gain = attn_norm[l + 1] if l + 1 < depth else None
        xt, h = _peer_layer(xt, h2, next_gain, peer_w_query[l], peer_sub_keys[l], peer_down[l], peer_up[l])
    return xt.reshape(batch, seq, d_model)
```

```python
import functools

import numpy as np
import jax
import jax.numpy as jnp
from jax import lax
from jax.experimental import pallas as pl
from jax.experimental.pallas import tpu as pltpu

F32 = jnp.float32
BF16 = jnp.bfloat16

HEAD_DIM = 64
ROT_DIM = HEAD_DIM // 4
ROPE_THETA = 500000.0
ATTN_SCALE = HEAD_DIM ** -0.5
EPS = 1e-6
NEG_INF = -1e30

MOBA_HEADS = 8
FOX_HEADS = 8
MOBA_BLOCK = 256
MOBA_TOPK = 3
A_W = MOBA_HEADS * HEAD_DIM
B_W = FOX_HEADS * HEAD_DIM

SWA_Q_HEADS = 16
SWA_KV_HEADS = 2
SWA_WINDOW = 128

PEER_HEADS = 8
PEER_N_KEYS = 128
PEER_TOPK = 16
PEER_QUERY_DIM = 128

LANES = 128
BF16_ROWS = 16
PAIR_W = 2 * HEAD_DIM

ROW_TILE = 512
FLASH_TILE = 2 * MOBA_BLOCK
ROUTE_TILE = 2 * LANES
EXPERT_TOKEN_TILE = 512
VMEM_LIMIT = 48 * 1024 * 1024
VMEM_LIMIT_EXPERTS = 58 * 1024 * 1024


def _cparams(sem, vmem_limit=VMEM_LIMIT):
    return pltpu.CompilerParams(dimension_semantics=sem, vmem_limit_bytes=vmem_limit)


def _dot_nt(a, b):
    return lax.dot_general(a, b, (((1,), (1,)), ((), ())), preferred_element_type=F32)


def _dot(a, b):
    return jnp.dot(a, b, preferred_element_type=F32)


def _split2(x):
    h1 = x.astype(BF16)
    return h1, (x - h1.astype(F32)).astype(BF16)


def _split3(x):
    h1 = x.astype(BF16)
    r1 = x - h1.astype(F32)
    h2 = r1.astype(BF16)
    h3 = (r1 - h2.astype(F32)).astype(BF16)
    return h1, h2, h3


def _rmsnorm_kernel(x_ref, g_ref, o_ref):
    x = x_ref[...]
    ms = jnp.mean(x * x, axis=-1, keepdims=True)
    o_ref[...] = (x * lax.rsqrt(ms + EPS) * g_ref[...]).astype(o_ref.dtype)


def _rmsnorm(x, gain, tm=ROW_TILE):
    T, D = x.shape
    return pl.pallas_call(
        _rmsnorm_kernel,
        grid=(T // tm,),
        in_specs=[pl.BlockSpec((tm, D), lambda i: (i, 0)),
                  pl.BlockSpec((1, D), lambda i: (0, 0))],
        out_specs=pl.BlockSpec((tm, D), lambda i: (i, 0)),
        out_shape=jax.ShapeDtypeStruct((T, D), BF16),
        compiler_params=_cparams(("parallel",)),
        name="rmsnorm",
    )(x, gain.reshape(1, D))


def _proj_kernel(*refs, norm, rope, kmean, tn):
    it = iter(refs)
    h_ref, w_ref = next(it), next(it)
    gain_ref = next(it) if norm else None
    bd_ref = next(it) if norm else None
    if rope:
        c_ref, sa_ref, sb_ref = next(it), next(it), next(it)
    o_ref = next(it)
    km_ref = next(it) if kmean else None

    y = _dot(h_ref[...], w_ref[...])
    if norm:
        y2 = y * y
        bd = bd_ref[...]
        cols = []
        tm = y.shape[0]
        for c in range(tn // LANES):
            h1, h2 = _split2(y2[:, c * LANES:(c + 1) * LANES])
            t = _dot(jnp.concatenate([h1, h2], axis=0), bd)
            cols.append(t[:tm] + t[tm:])
        ms = cols[0] if len(cols) == 1 else jnp.concatenate(cols, axis=1)
        y = y * lax.rsqrt(ms + EPS) * gain_ref[...]
    if rope:
        rep = tn // LANES
        tile = (lambda t: t) if rep == 1 else (lambda t: jnp.concatenate([t] * rep, axis=1))
        y = (y * tile(c_ref[...])
             + pltpu.roll(y, tn - ROT_DIM // 2, 1) * tile(sa_ref[...])
             + pltpu.roll(y, ROT_DIM // 2, 1) * tile(sb_ref[...]))
    o_ref[...] = y.astype(o_ref.dtype)
    if kmean:
        for r in range(km_ref.shape[0]):
            km_ref[r] = jnp.mean(y[r * MOBA_BLOCK:(r + 1) * MOBA_BLOCK], axis=0, keepdims=True)


def _proj(h, w, *, seq, tn, tm=ROW_TILE, gain=None, rope_tabs=None, kmean=False):
    T, D = h.shape
    N = w.shape[1]
    norm = gain is not None
    rope = rope_tabs is not None
    nseq = seq // tm
    in_specs = [pl.BlockSpec((tm, D), lambda i, j: (i, 0)),
                pl.BlockSpec((D, tn), lambda i, j: (0, j))]
    args = [h, w]
    if norm:
        bd = np.kron(np.eye(LANES // HEAD_DIM), np.ones((HEAD_DIM, HEAD_DIM))) / HEAD_DIM
        in_specs += [pl.BlockSpec((1, tn), lambda i, j: (0, j)),
                     pl.BlockSpec((LANES, LANES), lambda i, j: (0, 0))]
        args += [gain.reshape(1, N).astype(F32), jnp.asarray(bd, BF16)]
    if rope:
        in_specs += [pl.BlockSpec((tm, LANES), lambda i, j: (i % nseq, 0))] * 3
        args += list(rope_tabs)
    out_specs = [pl.BlockSpec((tm, tn), lambda i, j: (i, j))]
    out_shape = [jax.ShapeDtypeStruct((T, N), BF16)]
    if kmean:
        out_specs.append(pl.BlockSpec((tm // MOBA_BLOCK, 1, tn), lambda i, j: (i, 0, j)))
        out_shape.append(jax.ShapeDtypeStruct((T // MOBA_BLOCK, 1, N), F32))
    res = pl.pallas_call(
        functools.partial(_proj_kernel, norm=norm, rope=rope, kmean=kmean, tn=tn),
        grid=(T // tm, N // tn),
        in_specs=in_specs, out_specs=out_specs, out_shape=out_shape,
        compiler_params=_cparams(("parallel", "parallel")),
        name="proj",
    )(*args)
    return res if kmean else res[0]


def _rope_tables(seq):
    half = ROT_DIM // 2
    inv_freq = jnp.power(ROPE_THETA, -jnp.arange(0, ROT_DIM, 2, dtype=F32) / ROT_DIM)
    ang = jnp.arange(seq, dtype=F32)[:, None] * inv_freq[None, :]
    cos, sin = jnp.cos(ang), jnp.sin(ang)
    one = jnp.ones((seq, HEAD_DIM - ROT_DIM), F32)
    zero = jnp.zeros((seq, HEAD_DIM - ROT_DIM), F32)
    z8 = jnp.zeros((seq, half), F32)
    c = jnp.concatenate([cos, cos, one], axis=1)
    sa = jnp.concatenate([-sin, z8, zero], axis=1)
    sb = jnp.concatenate([z8, sin, zero], axis=1)
    rep = LANES // HEAD_DIM
    return tuple(jnp.concatenate([t] * rep, axis=1) for t in (c, sa, sb))


def _gates_kernel(h_ref, wf_ref, b_ref, tri_ref, o_ref, carry_ref):
    @pl.when(pl.program_id(1) == 0)
    def _():
        carry_ref[...] = jnp.zeros_like(carry_ref)

    z = _dot_nt(wf_ref[...], h_ref[...]) + b_ref[...][:, :1]
    lf = jnp.minimum(z, 0.0) - jnp.log1p(jnp.exp(-jnp.abs(z)))
    tri = tri_ref[...]
    h1, h2, h3 = _split3(lf)
    cs = _dot(h1, tri) + _dot(h2, tri) + _dot(h3, tri) + carry_ref[...][:, :1]
    o_ref[0] = cs
    carry_ref[...] = jnp.broadcast_to(cs[:, -1:], carry_ref.shape)


def _fox_gates(h, wf_t, bias, *, batch, seq, tm=ROW_TILE):
    T, D = h.shape
    nh = wf_t.shape[0]
    nseq = seq // tm
    tri = jnp.asarray(np.triu(np.ones((tm, tm))), BF16)
    return pl.pallas_call(
        _gates_kernel,
        grid=(batch, nseq),
        in_specs=[pl.BlockSpec((tm, D), lambda b, s: (b * nseq + s, 0)),
                  pl.BlockSpec((nh, D), lambda b, s: (0, 0)),
                  pl.BlockSpec((nh, LANES), lambda b, s: (0, 0)),
                  pl.BlockSpec((tm, tm), lambda b, s: (0, 0))],
        out_specs=pl.BlockSpec((1, nh, tm), lambda b, s: (b, 0, s)),
        out_shape=jax.ShapeDtypeStruct((batch, nh, seq), F32),
        scratch_shapes=[pltpu.VMEM((nh, LANES), F32)],
        compiler_params=_cparams(("parallel", "arbitrary")),
        name="fox_gates",
    )(h, wf_t, jnp.broadcast_to(bias.astype(F32)[:, None], (nh, LANES)), tri)


def _lane_tile(x, width):
    rep = width // LANES
    return x if rep == 1 else jnp.concatenate([x] * rep, axis=1)


def _flash_init(m_ref, acc_ref):
    m_ref[...] = jnp.full(m_ref.shape, NEG_INF, F32)
    acc_ref[...] = jnp.zeros(acc_ref.shape, F32)


def _head_values(v):
    lane = lax.broadcasted_iota(jnp.int32, v.shape, 1)
    return [jnp.where((lane >= hh * HEAD_DIM) & (lane < (hh + 1) * HEAD_DIM), v, jnp.ones_like(v))
            for hh in range(2)]


def _flash_update(slot, s, v, m_ref, acc_ref):
    tk = s.shape[1]
    m_prev = m_ref[slot]
    m_new = jnp.maximum(m_prev, jnp.max(s, axis=1, keepdims=True))
    alpha = jnp.exp(m_prev - m_new)
    p = jnp.exp(s - _lane_tile(m_new, tk))
    acc_ref[slot] = alpha * acc_ref[slot] + _dot(p.astype(BF16), v)
    m_ref[slot] = m_new


def _flash_finish(lane, acc_ref):
    outs = []
    for hh in range(2):
        acc = acc_ref[hh]
        den = (1 - hh) * HEAD_DIM
        outs.append(acc / acc[:, den:den + 1])
    return jnp.where(lane < HEAD_DIM, outs[0], outs[1])


def _flash_tiles(qi, tk, logits, values, m_ref, acc_ref):
    def absorb(s, off):
        vh = values(off)
        for hh in range(2):
            _flash_update(hh, s[hh], vh[hh], m_ref, acc_ref)

    def pair(off_a, off_b, b_diagonal):
        sa = logits(off_a, False)
        sb = logits(off_b, b_diagonal)
        absorb(sa, off_a)
        absorb(sb, off_b)

    def body(jj, carry):
        off = pl.multiple_of(2 * jj * tk, 2 * tk)
        pair(off, off + tk, False)
        return carry

    lax.fori_loop(0, qi // 2, body, 0)
    diag = pl.multiple_of(qi * tk, tk)

    @pl.when(qi % 2 == 1)
    def _():
        pair(diag - tk, diag, True)

    @pl.when(qi % 2 == 0)
    def _():
        absorb(logits(diag, True), diag)


def _head_queries(q, lane):
    qs = q * ATTN_SCALE
    return [jnp.where((lane >= hh * HEAD_DIM) & (lane < (hh + 1) * HEAD_DIM), qs, jnp.zeros_like(qs))
            for hh in range(2)]


def _moba_kernel(q_ref, k_ref, v_ref, km_ref, o_ref, m_ref, acc_ref):
    tq = q_ref.shape[1]
    tk = tq
    qi = pl.program_id(2)
    lane = lax.broadcasted_iota(jnp.int32, (tq, LANES), 1)
    lane_f = lane.astype(F32)
    rowv = lax.broadcasted_iota(jnp.int32, (tq, LANES), 0)
    row_blk = 2 * qi + (rowv >= MOBA_BLOCK).astype(jnp.int32)
    row = lax.broadcasted_iota(jnp.int32, (tq, tk), 0)
    col = lax.broadcasted_iota(jnp.int32, (tq, tk), 1)
    qh = _head_queries(q_ref[0], lane)
    _flash_init(m_ref, acc_ref)

    sels = []
    for hh in range(2):
        gate = _dot_nt(qh[hh], km_ref[0])
        gate = jnp.where(lane < row_blk, gate, -jnp.inf)
        sel = jnp.zeros((tq, LANES), F32)
        for _ in range(MOBA_TOPK):
            m = jnp.max(gate, axis=1, keepdims=True)
            idx = jnp.min(jnp.where(gate == m, lane_f, float(LANES)), axis=1, keepdims=True)
            hit = lane_f == idx
            sel = jnp.where(hit & (m > -jnp.inf), 1.0, sel)
            gate = jnp.where(hit, -jnp.inf, gate)
        sels.append(sel)

    def chosen(sel, blk):
        return jnp.max(jnp.where(lane == blk, sel, 0.0), axis=1, keepdims=True) > 0.0

    def chosen_pair(sel, blk):
        weight = jnp.where(lane == blk, 1.0, jnp.where(lane == blk + 1, 2.0, 0.0))
        code = jnp.sum(sel * weight, axis=1, keepdims=True)
        return (code == 1.0) | (code == 3.0), code >= 2.0

    def logits(off, diagonal):
        kj = k_ref[0, pl.ds(off, tk), :]
        out = []
        for hh in range(2):
            s = _dot_nt(qh[hh], kj)
            if diagonal:
                visible = (col >= MOBA_BLOCK) | (row < MOBA_BLOCK) | chosen(sels[hh], 2 * qi)
                s = jnp.where((col <= row) & visible, s, NEG_INF)
            else:
                first, second = chosen_pair(sels[hh], 2 * (off // tk))
                s = jnp.concatenate([jnp.where(first, s[:, :MOBA_BLOCK], NEG_INF),
                                     jnp.where(second, s[:, MOBA_BLOCK:], NEG_INF)], axis=1)
            out.append(s)
        return out

    def values(off):
        return _head_values(v_ref[0, pl.ds(off, tk), :])

    _flash_tiles(qi, tk, logits, values, m_ref, acc_ref)
    o_ref[0] = _flash_finish(lane, acc_ref).astype(o_ref.dtype)


def _flash_scratch(tq):
    return [pltpu.VMEM((2, tq, LANES), F32)] * 2


def _moba(q, k, v, kmean, *, batch, seq):
    W = q.shape[1]
    tq = FLASH_TILE
    q3, k3, v3 = (t.reshape(batch, seq, W) for t in (q, k, v))
    out = pl.pallas_call(
        _moba_kernel,
        grid=(batch, W // PAIR_W, seq // tq),
        in_specs=[pl.BlockSpec((1, tq, PAIR_W), lambda b, p, i: (b, i, p)),
                  pl.BlockSpec((1, seq, PAIR_W), lambda b, p, i: (b, 0, p)),
                  pl.BlockSpec((1, seq, PAIR_W), lambda b, p, i: (b, 0, p)),
                  pl.BlockSpec((1, LANES, PAIR_W), lambda b, p, i: (b, 0, p))],
        out_specs=pl.BlockSpec((1, tq, PAIR_W), lambda b, p, i: (b, i, p)),
        out_shape=jax.ShapeDtypeStruct((batch, seq, W), BF16),
        scratch_shapes=_flash_scratch(tq),
        compiler_params=_cparams(("parallel", "parallel", "parallel")),
        name="moba",
    )(q3, k3, v3, kmean)
    return out.reshape(batch * seq, W)


_EXP_UNDERFLOW = 112.0


def _fox_kernel(q_ref, k_ref, v_ref, g_ref, frow_ref, far_ref, o_ref, m_ref, acc_ref):
    tq = q_ref.shape[1]
    tk = tq
    pr = pl.program_id(1)
    qi = pl.program_id(2)
    lane = lax.broadcasted_iota(jnp.int32, (tq, LANES), 1)
    row = lax.broadcasted_iota(jnp.int32, (tq, tk), 0)
    col = lax.broadcasted_iota(jnp.int32, (tq, tk), 1)
    qh = _head_queries(q_ref[0], lane)
    _flash_init(m_ref, acc_ref)

    def key_gates(off, width):
        f_all = frow_ref[0, :, pl.ds(off, width)]
        sub = lax.broadcasted_iota(jnp.int32, f_all.shape, 0)
        return [jnp.sum(jnp.where(sub == 2 * pr + hh, f_all, 0.0), axis=0, keepdims=True) for hh in range(2)]

    f_ref = [f[:, :1] for f in key_gates(pl.multiple_of(qi * tq, tq), LANES)]

    def logits(off, diagonal):
        kj = k_ref[0, pl.ds(off, tk), :]
        fk = key_gates(off, tk)
        out = []
        for hh in range(2):
            s = _dot_nt(qh[hh], kj) - (fk[hh] - f_ref[hh])
            out.append(jnp.where(col <= row, s, NEG_INF) if diagonal else s)
        return out

    def absorb(s, off):
        vh = _head_values(v_ref[0, pl.ds(off, tk), :])
        for hh in range(2):
            _flash_update(hh, s[hh], vh[hh], m_ref, acc_ref)

    def vanishes(j):
        tail = frow_ref[0, :, pl.ds(pl.multiple_of((j + 1) * tk - LANES, LANES), LANES)]
        sub = lax.broadcasted_iota(jnp.int32, tail.shape, 0)
        last = lax.broadcasted_iota(jnp.int32, tail.shape, 1) == LANES - 1
        gap = tail - jnp.where(sub == 2 * pr, f_ref[0], f_ref[1]) - far_ref[...][:, :1]
        mine = last & ((sub == 2 * pr) | (sub == 2 * pr + 1))
        return jnp.min(jnp.where(mine, gap, jnp.inf)) >= 0.0

    absorb(logits(pl.multiple_of(qi * tk, tk), True), pl.multiple_of(qi * tk, tk))

    def more(j):
        return jnp.logical_and(j >= 1, jnp.logical_not(vanishes(jnp.maximum(j, 0))))

    def pair(j):
        off_a = pl.multiple_of(j * tk, tk)
        off_b = pl.multiple_of((j - 1) * tk, tk)
        sa = logits(off_a, False)
        sb = logits(off_b, False)
        absorb(sa, off_a)
        absorb(sb, off_b)
        return j - 2

    j = lax.while_loop(more, pair, qi - 1)

    @pl.when(jnp.logical_and(j == 0, jnp.logical_not(vanishes(0))))
    def _():
        absorb(logits(0, False), 0)

    o = _flash_finish(lane, acc_ref)
    o_ref[0] = (o * jax.nn.sigmoid(g_ref[0].astype(F32))).astype(o_ref.dtype)


def _fox(q, k, v, g, frow, logit_bound, *, batch, seq, tq=FLASH_TILE):
    far = jnp.full((1, LANES), 2.0 * logit_bound + _EXP_UNDERFLOW, F32)
    W = q.shape[1]
    nh = frow.shape[1]
    q3, k3, v3, g3 = (t.reshape(batch, seq, W) for t in (q, k, v, g))
    out = pl.pallas_call(
        _fox_kernel,
        grid=(batch, W // PAIR_W, seq // tq),
        in_specs=[pl.BlockSpec((1, tq, PAIR_W), lambda b, p, i: (b, i, p)),
                  pl.BlockSpec((1, seq, PAIR_W), lambda b, p, i: (b, 0, p)),
                  pl.BlockSpec((1, seq, PAIR_W), lambda b, p, i: (b, 0, p)),
                  pl.BlockSpec((1, tq, PAIR_W), lambda b, p, i: (b, i, p)),
                  pl.BlockSpec((1, nh, seq), lambda b, p, i: (b, 0, 0)),
                  pl.BlockSpec((1, LANES), lambda b, p, i: (0, 0))],
        out_specs=pl.BlockSpec((1, tq, PAIR_W), lambda b, p, i: (b, i, p)),
        out_shape=jax.ShapeDtypeStruct((batch, seq, W), BF16),
        scratch_shapes=_flash_scratch(tq),
        compiler_params=_cparams(("parallel", "parallel", "parallel")),
        name="fox",
    )(q3, k3, v3, g3, frow, far)
    return out.reshape(batch * seq, W)


def _swa_kernel(q_ref, k_ref, v_ref, sink_ref, bias_ref, o_ref):
    tq = q_ref.shape[1]
    qi = pl.program_id(1)
    group = SWA_Q_HEADS // SWA_KV_HEADS
    tk = tq + SWA_WINDOW
    lane = lax.broadcasted_iota(jnp.int32, (tq, LANES), 1)
    kstart = pl.multiple_of(jnp.maximum(qi * tq - SWA_WINDOW, 0), SWA_WINDOW)
    k = k_ref[0, pl.ds(kstart, tk), :]
    vh = _head_values(v_ref[0, pl.ds(kstart, tk), :])
    swap = lambda t: pltpu.roll(t.astype(F32), HEAD_DIM, 1).astype(BF16)
    k_by_half = [k, swap(k)]
    bias = bias_ref[jnp.minimum(qi, 1)]
    bias = jnp.concatenate([bias] * (group // 2), axis=0)
    sink_tab = sink_ref[...]

    chains = []
    for c in range(SWA_KV_HEADS):
        v_by_half = [vh[c], swap(vh[c])]
        for swapped in range(2):
            hh = c if not swapped else 1 - c
            heads = [h for h in range(c * group, (c + 1) * group) if h % 2 == hh]
            pieces, sinks = [], []
            for head in heads:
                blk = q_ref[0, :, (head // 2) * PAIR_W:(head // 2 + 1) * PAIR_W] * ATTN_SCALE
                pieces.append(jnp.where((lane >= hh * HEAD_DIM) & (lane < (hh + 1) * HEAD_DIM),
                                        blk, jnp.zeros_like(blk)))
                sinks.append(jnp.broadcast_to(sink_tab[head:head + 1, :], (tq, LANES)))
            s = _dot_nt(jnp.concatenate(pieces, axis=0), k_by_half[swapped]) + bias
            chains.append((heads, s, jnp.concatenate(sinks, axis=0), v_by_half[swapped]))

    outs = [None] * SWA_Q_HEADS
    for heads, s, sink, v in chains:
        m = jnp.maximum(sink, jnp.max(s, axis=1, keepdims=True))
        p = jnp.exp(s - _lane_tile(m, tk))
        acc = _dot(p.astype(BF16), v)
        den = pltpu.roll(acc, HEAD_DIM, 1) + jnp.exp(sink - m)
        o = acc / den
        for idx, head in enumerate(heads):
            outs[head] = o[idx * tq:(idx + 1) * tq]
    for pp in range(SWA_Q_HEADS // 2):
        o_ref[0, :, pp * PAIR_W:(pp + 1) * PAIR_W] = jnp.where(
            lane < HEAD_DIM, outs[2 * pp], outs[2 * pp + 1]).astype(o_ref.dtype)


def _swa_bias(tq):
    r = np.arange(tq)[:, None]
    c = np.arange(tq + SWA_WINDOW)[None, :]
    tabs = []
    for key_offset in (0, SWA_WINDOW):
        dist = r + key_offset - c
        tabs.append(np.where((dist >= 0) & (dist < SWA_WINDOW), 0.0, NEG_INF))
    return jnp.asarray(np.stack(tabs), F32)


def _swa(q, k, v, sinks, *, batch, seq, tq=SWA_WINDOW):
    W = q.shape[1]
    tk = tq + SWA_WINDOW
    q3 = q.reshape(batch, seq, W)
    k3, v3 = (t.reshape(batch, seq, PAIR_W) for t in (k, v))
    sink_tab = jnp.broadcast_to(sinks.astype(F32)[:, None], (SWA_Q_HEADS, LANES))
    out = pl.pallas_call(
        _swa_kernel,
        grid=(batch, seq // tq),
        in_specs=[pl.BlockSpec((1, tq, W), lambda b, i: (b, i, 0)),
                  pl.BlockSpec((1, seq, PAIR_W), lambda b, i: (b, 0, 0)),
                  pl.BlockSpec((1, seq, PAIR_W), lambda b, i: (b, 0, 0)),
                  pl.BlockSpec((SWA_Q_HEADS, LANES), lambda b, i: (0, 0)),
                  pl.BlockSpec((2, tq, tk), lambda b, i: (0, 0, 0))],
        out_specs=pl.BlockSpec((1, tq, W), lambda b, i: (b, i, 0)),
        out_shape=jax.ShapeDtypeStruct((batch, seq, W), BF16),
        compiler_params=_cparams(("parallel", "parallel")),
        name="swa",
    )(q3, k3, v3, sink_tab, _swa_bias(tq))
    return out.reshape(batch * seq, W)


def _rms_normed(x, gain):
    ms = jnp.mean(x * x, axis=-1, keepdims=True)
    return (x * lax.rsqrt(ms + EPS) * gain).astype(BF16)


def _outproj_kernel(*refs, n_parts):
    parts = refs[:n_parts]
    w_ref, x_ref, g_ref, o_ref, h_ref = refs[n_parts:]
    y = x_ref[...]
    off = 0
    for p_ref in parts:
        kw = p_ref.shape[1]
        y = y + _dot(p_ref[...], w_ref[off:off + kw, :])
        off += kw
    o_ref[...] = y
    h_ref[...] = _rms_normed(y, g_ref[...])


def _outproj(parts, w, x, next_gain, tm=ROW_TILE):
    T, D = x.shape
    in_specs = [pl.BlockSpec((tm, p.shape[1]), lambda i: (i, 0)) for p in parts]
    in_specs += [pl.BlockSpec(w.shape, lambda i: (0, 0)),
                 pl.BlockSpec((tm, D), lambda i: (i, 0)),
                 pl.BlockSpec((1, D), lambda i: (0, 0))]
    row_spec = pl.BlockSpec((tm, D), lambda i: (i, 0))
    return pl.pallas_call(
        functools.partial(_outproj_kernel, n_parts=len(parts)),
        grid=(T // tm,),
        in_specs=in_specs,
        out_specs=[row_spec, row_spec],
        out_shape=[jax.ShapeDtypeStruct((T, D), F32), jax.ShapeDtypeStruct((T, D), BF16)],
        compiler_params=_cparams(("parallel",)),
        name="outproj",
    )(*parts, w, x, next_gain.reshape(1, D).astype(F32))


_CAND_ROWS = 80


def _cand_tables(lanes):
    pos = np.zeros((_CAND_ROWS,), np.float32)
    neg = np.zeros((_CAND_ROWS,), np.float32)
    r = 0
    for a, nb in ((0, 16), (1, 8), (2, 8), (3, 8), (4, 8), (5, 8), (6, 8), (7, 8)):
        for b in range(nb):
            pos[r] = a * PEER_TOPK + b
            neg[r] = 0.0 if (a + 1) * (b + 1) <= PEER_TOPK else -np.inf
            r += 1
    for a in range(8, 16):
        pos[r] = a * PEER_TOPK
        r += 1
    assert r == _CAND_ROWS
    tab = lambda t: jnp.asarray(np.broadcast_to(t[:, None], (_CAND_ROWS, lanes)).copy())
    return tab(pos), tab(neg)


def _batcher_pairs(n):
    pairs, p = [], 1
    while p < n:
        k = p
        while k >= 1:
            for j in range(k % p, n - k, 2 * k):
                for i in range(min(k, n - j - k)):
                    if (i + j) // (2 * p) == (i + j + k) // (2 * p):
                        pairs.append((i + j, i + j + k))
            k //= 2
        p *= 2
    return pairs


_SORT16 = _batcher_pairs(PEER_TOPK)
_N_CAND_PIECES = 10
_SORT10 = [(i, j) for i, j in _SORT16 if j < _N_CAND_PIECES]
_SUBLANES = 8


def _compare_exchange(items, i, j):
    items[i], items[j] = jnp.maximum(items[i], items[j]), jnp.minimum(items[i], items[j])


def _top_sorted(pieces, pairs):
    items = list(pieces)
    for i, j in pairs:
        _compare_exchange(items, i, j)
    n = PEER_TOPK
    items += [jnp.full(items[0].shape, -jnp.inf, F32)] * (n - len(items))
    for shift in (4, 2, 1):
        items = [jnp.maximum(items[i], pltpu.roll(items[n - 1 - i], shift, 0)) for i in range(n)]
        d = n // 2
        while d >= 1:
            for i in range(n):
                if i & d == 0:
                    _compare_exchange(items, i, i + d)
            d //= 2
    return items


def _pieces(x):
    return [x[_SUBLANES * g:_SUBLANES * (g + 1)] for g in range(x.shape[0] // _SUBLANES)]


def _sublane_total(x):
    for shift in (4, 2, 1):
        x = x + pltpu.roll(x, shift, 0)
    return x


def _count_ge(pieces, thr):
    total = jnp.zeros(thr.shape, F32)
    for p in pieces:
        total = total + jnp.where(p >= thr, 1.0, 0.0)
    return _sublane_total(total)


def _route_head_fast(s1, s2):
    p1, p2 = _pieces(s1), _pieces(s2)
    v1 = _top_sorted(p1, _SORT16)
    v2 = _top_sorted(p2, _SORT16)
    sub = lax.broadcasted_iota(jnp.int32, v1[0].shape, 0)

    def spread(vals):
        out = vals[0]
        for r in range(1, _SUBLANES):
            out = jnp.where(sub == r, vals[r], out)
        return out

    v2_lo, v2_hi, v1_hi = spread(v2[:8]), spread(v2[8:]), spread(v1[8:])
    cands = [v1[0] + v2_lo, v1[0] + v2_hi, v1[1] + v2_lo]
    for a in range(2, 8):
        cands.append(jnp.where(sub < PEER_TOPK // (a + 1), v1[a] + v2_lo, -jnp.inf))
    cands.append(v1_hi + v2[0])
    ts = _top_sorted(cands, _SORT10)
    tau = ts[PEER_TOPK - 1]
    z = jnp.exp(ts[0] - ts[0])
    for kk in range(1, PEER_TOPK):
        z = z + jnp.exp(ts[kk] - ts[0])

    tied = (_count_ge(p1, v1[-1]) != float(PEER_TOPK)) | (_count_ge(p2, v2[-1]) != float(PEER_TOPK))
    tied = tied | (_count_ge(cands, tau) != float(PEER_TOPK))
    for b in range(PEER_TOPK - 1):
        tied = tied | (v1[b] == v1[b + 1]) | (v2[b] == v2[b + 1])

    cnt = []
    for a in range(PEER_TOPK):
        c = jnp.zeros(tau.shape, F32)
        for b in range(PEER_TOPK // (a + 1)):
            c = c + jnp.where(v1[a] + v2[b] >= tau, 1.0, 0.0)
        cnt.append(c)
    c1, r2 = [], []
    for x in p1:
        c = jnp.zeros(x.shape, F32)
        for a in range(PEER_TOPK):
            c = jnp.where(x == v1[a], cnt[a], c)
        c1.append(c)
    for x in p2:
        r = jnp.zeros(x.shape, F32)
        for b in range(PEER_TOPK):
            r = r + jnp.where(v2[b] > x, 1.0, 0.0)
        r2.append(r)
    inv_z = 1.0 / z
    e1 = [jnp.exp(x - v1[0]) * inv_z for x in p1]
    e2 = [jnp.exp(x - v2[0]) for x in p2]
    cat = lambda ps: jnp.concatenate(ps, axis=0)
    return (cat(c1), cat(e1), cat(r2), cat(e2)), tied


def _extract_sorted(scores, by_key):
    nk, lanes = scores[0].shape
    kio = lax.broadcasted_iota(jnp.int32, (nk, lanes), 0).astype(F32)
    slot = lax.broadcasted_iota(jnp.int32, (PEER_TOPK, lanes), 0)

    def body(a, carry):
        here = slot == a
        out = []
        for (v, vals, aux), ranked in zip(carry, by_key):
            m = jnp.max(v, axis=0, keepdims=True)
            idx = jnp.min(jnp.where(v == m, kio, float(nk)), axis=0, keepdims=True)
            hit = kio == idx
            aux = jnp.where(hit, jnp.asarray(a, F32), aux) if ranked else jnp.where(here, idx, aux)
            out.append((jnp.where(hit, -jnp.inf, v), jnp.where(here, m, vals), aux))
        return tuple(out)

    small = jnp.zeros((PEER_TOPK, lanes), F32)
    unranked = jnp.full((nk, lanes), float(PEER_TOPK), F32)
    init = tuple((v, small, unranked if ranked else small) for v, ranked in zip(scores, by_key))
    return [(vals, aux) for _, vals, aux in lax.fori_loop(0, PEER_TOPK, body, init)]


def _route_head_exact(s1, s2, pos, neg):
    lanes = s1.shape[1]
    slot = lax.broadcasted_iota(jnp.int32, (PEER_TOPK, lanes), 0)
    kio = lax.broadcasted_iota(jnp.int32, (PEER_N_KEYS, lanes), 0).astype(F32)
    (v1, idx1), (v2, rank2) = _extract_sorted([s1, s2], [False, True])
    blocks = [v1[0:1] + v2[0:8], v1[0:1] + v2[8:16]]
    blocks += [v1[a:a + 1] + v2[0:8] for a in range(1, 8)]
    blocks += [v1[8:16] + v2[0:1]]
    cand = jnp.concatenate(blocks, axis=0) + neg

    def pick(kk, carry):
        cand, chosen, ts = carry
        m = jnp.max(cand, axis=0, keepdims=True)
        first = jnp.min(jnp.where(cand == m, pos, 1e9), axis=0, keepdims=True)
        hit = pos == first
        return (jnp.where(hit, -jnp.inf, cand), jnp.where(hit, 1.0, chosen), jnp.where(slot == kk, m, ts))

    _, chosen, ts = lax.fori_loop(0, PEER_TOPK, pick,
                                  (cand, jnp.zeros_like(cand), jnp.zeros((PEER_TOPK, lanes), F32)))
    z = jnp.sum(jnp.exp(ts - ts[0:1]), axis=0, keepdims=True)
    counts = [jnp.sum(chosen[0:16], axis=0, keepdims=True)]
    counts += [jnp.sum(chosen[8 * a + 8:8 * a + 16], axis=0, keepdims=True) for a in range(1, 8)]
    counts += [chosen[72 + a:73 + a] for a in range(8)]
    c1 = jnp.zeros((PEER_N_KEYS, lanes), F32)
    for a in range(PEER_TOPK):
        c1 = jnp.where(kio == idx1[a:a + 1], counts[a], c1)
    return c1, jnp.exp(s1 - v1[0:1]) / z, rank2, jnp.exp(s2 - v2[0:1])


def _route_kernel(h_ref, wq_ref, keys_ref, pos_ref, neg_ref,
                  c1_ref, e1_ref, r2_ref, e2_ref, qt_ref, sc_ref):
    half = PEER_QUERY_DIM // 2
    qt_ref[...] = _dot_nt(wq_ref[...], h_ref[...]).astype(BF16)

    def store(h, maps):
        c1, e1, r2, e2 = maps
        c1_ref[h] = c1
        e1_ref[h] = e1
        r2_ref[h] = r2.astype(BF16)
        e2_ref[h] = e2.astype(BF16)

    def head_body(h, _):
        r0 = pl.multiple_of(h * PEER_QUERY_DIM, PEER_QUERY_DIM)
        sc_ref[0] = _dot(keys_ref[2 * h], qt_ref[pl.ds(r0, half), :])
        sc_ref[1] = _dot(keys_ref[2 * h + 1], qt_ref[pl.ds(r0 + half, half), :])
        maps, tied = _route_head_fast(sc_ref[0], sc_ref[1])
        any_tied = jnp.max(jnp.where(tied, 1.0, 0.0)) > 0.0

        @pl.when(any_tied)
        def _():
            store(h, _route_head_exact(sc_ref[0], sc_ref[1], pos_ref[...], neg_ref[...]))

        @pl.when(jnp.logical_not(any_tied))
        def _():
            store(h, maps)

        return 0

    lax.fori_loop(0, PEER_HEADS, head_body, 0)


def _peer_route(h2, wq_t, keys, tt=ROUTE_TILE):
    T, D = h2.shape
    pos, neg = _cand_tables(tt)
    stat_spec = pl.BlockSpec((PEER_HEADS, PEER_N_KEYS, tt), lambda i: (0, 0, i))
    stat = lambda dt: jax.ShapeDtypeStruct((PEER_HEADS, PEER_N_KEYS, T), dt)
    return pl.pallas_call(
        _route_kernel,
        grid=(T // tt,),
        in_specs=[pl.BlockSpec((tt, D), lambda i: (i, 0)),
                  pl.BlockSpec(wq_t.shape, lambda i: (0, 0)),
                  pl.BlockSpec(keys.shape, lambda i: (0, 0, 0)),
                  pl.BlockSpec((_CAND_ROWS, tt), lambda i: (0, 0)),
                  pl.BlockSpec((_CAND_ROWS, tt), lambda i: (0, 0))],
        out_specs=[stat_spec] * 4,
        out_shape=[stat(F32), stat(F32), stat(BF16), stat(BF16)],
        scratch_shapes=[pltpu.VMEM((PEER_HEADS * PEER_QUERY_DIM, tt), BF16),
                        pltpu.VMEM((2, PEER_N_KEYS, tt), F32)],
        compiler_params=_cparams(("parallel",)),
        name="peer_route",
    )(h2, wq_t, keys, pos, neg)


_KEY_GROUP = 16
_UNITS = 4
_DRAIN_PIECES = 2


def _build_gated(a_ref, p_ref, c1_ref, e1_ref, r2_ref, e2_ref, key0, g0, ng, lt):
    rep = PEER_N_KEYS // BF16_ROWS

    def rows16(row):
        blk = jnp.broadcast_to(row, (BF16_ROWS, LANES)).astype(BF16)
        return jnp.concatenate([blk] * rep, axis=0)

    ls = slice(lt * LANES, (lt + 1) * LANES)
    w = [jnp.zeros((PEER_N_KEYS, LANES), BF16) for _ in range(ng)]
    for h in range(PEER_HEADS):
        c1 = c1_ref[h, pl.ds(key0, _KEY_GROUP), ls]
        e1 = e1_ref[h, pl.ds(key0, _KEY_GROUP), ls]
        r2 = r2_ref[h, :, ls]
        e2 = e2_ref[h, :, ls]
        for g in range(ng):
            thr = rows16(c1[g0 + g:g0 + g + 1])
            gate = rows16(e1[g0 + g:g0 + g + 1])
            w[g] = w[g] + jnp.where(r2 < thr, e2, jnp.zeros_like(e2)) * gate
    for g in range(g0, g0 + ng):
        rs = slice(g * PEER_N_KEYS, (g + 1) * PEER_N_KEYS)
        a = a_ref[rs, ls]
        gelu = 0.5 * a * (1.0 + lax.erf(a * (2.0 ** -0.5)))
        p_ref[rs, ls] = gelu.astype(BF16) * w[g - g0]


def _experts_kernel(h_ref, dn_ref, upt_ref, c1_ref, e1_ref, r2_ref, e2_ref, x_ref, *rest, emit_norm):
    if emit_norm:
        g_ref, o_ref, hn_ref, a_ref, p_ref, acc_ref = rest
    else:
        o_ref, a_ref, p_ref, acc_ref = rest
    te, tt = a_ref.shape
    d_model = acc_ref.shape[0]
    e = pl.program_id(1)
    n_tiles = pl.num_programs(1) - 1
    cur = e % 2
    ng = _KEY_GROUP // _UNITS
    n_lane = tt // LANES
    key0 = pl.multiple_of(jnp.minimum(e, n_tiles - 1) * _KEY_GROUP, _KEY_GROUP)

    def front_mm(u):
        rows = te // _UNITS
        rs = slice(u * rows, (u + 1) * rows)
        a_ref[rs, :] = _dot_nt(dn_ref[rs, :], h_ref[...])

    def back_mm(r):
        rows = d_model // _DRAIN_PIECES
        rs = slice(r * rows, (r + 1) * rows)
        acc_ref[rs, :] += _dot(upt_ref[rs, :], p_ref[1 - cur])

    units_per_drain = _UNITS // _DRAIN_PIECES

    def run(front, back):
        if front:
            front_mm(0)
        for u in range(_UNITS):
            for lt in range(n_lane):
                if front:
                    _build_gated(a_ref, p_ref.at[cur], c1_ref, e1_ref, r2_ref, e2_ref, key0, u * ng, ng, lt)
                if front and lt == 0 and u + 1 < _UNITS:
                    front_mm(u + 1)
                if back and lt == n_lane // 2 and (u + 1) % units_per_drain == 0:
                    back_mm(u // units_per_drain)

    @pl.when(e == 0)
    def _():
        acc_ref[...] = jnp.zeros_like(acc_ref)
        run(True, False)

    @pl.when((e > 0) & (e < n_tiles))
    def _():
        run(True, True)

    @pl.when(e == n_tiles)
    def _():
        run(False, True)
        y = x_ref[...] + acc_ref[...].T
        o_ref[...] = y
        if emit_norm:
            hn_ref[...] = _rms_normed(y, g_ref[...])


def _peer_experts(h2, down, up_t, stats, x, next_gain, tt=EXPERT_TOKEN_TILE):
    T, D = h2.shape
    E = down.shape[0]
    te = _KEY_GROUP * PEER_N_KEYS
    n_tiles = E // te
    emit_norm = next_gain is not None
    stat_spec = pl.BlockSpec((PEER_HEADS, PEER_N_KEYS, tt), lambda i, e: (0, 0, i))
    row_spec = pl.BlockSpec((tt, D), lambda i, e: (i, 0))
    in_specs = [row_spec,
                pl.BlockSpec((te, D), lambda i, e: (jnp.minimum(e, n_tiles - 1), 0)),
                pl.BlockSpec((None, D, te), lambda i, e: (jnp.maximum(e - 1, 0), 0, 0)),
                stat_spec, stat_spec, stat_spec, stat_spec,
                row_spec]
    args = [h2, down, up_t, *stats, x]
    out_specs = [row_spec]
    out_shape = [jax.ShapeDtypeStruct((T, D), F32)]
    if emit_norm:
        in_specs.append(pl.BlockSpec((1, D), lambda i, e: (0, 0)))
        args.append(next_gain.reshape(1, D).astype(F32))
        out_specs.append(row_spec)
        out_shape.append(jax.ShapeDtypeStruct((T, D), BF16))
    res = pl.pallas_call(
        functools.partial(_experts_kernel, emit_norm=emit_norm),
        grid=(T // tt, n_tiles + 1),
        in_specs=in_specs, out_specs=out_specs, out_shape=out_shape,
        scratch_shapes=[pltpu.VMEM((te, tt), F32),
                        pltpu.VMEM((2, te, tt), BF16),
                        pltpu.VMEM((D, tt), F32)],
        compiler_params=_cparams(("parallel", "arbitrary"), VMEM_LIMIT_EXPERTS),
        name="peer_experts",
    )(*args)
    return (res[0], res[1]) if emit_norm else (res[0], None)


def _peer_layer(x, h2, next_gain, w_query, sub_keys, down, up):
    keys = sub_keys.reshape(PEER_HEADS * 2, PEER_N_KEYS, PEER_QUERY_DIM // 2).astype(BF16)
    stats = _peer_route(h2, w_query.T.astype(BF16), keys)
    te = _KEY_GROUP * PEER_N_KEYS
    up_t = up.reshape(up.shape[0] // te, te, up.shape[1]).transpose(0, 2, 1).astype(BF16)
    return _peer_experts(h2, down.astype(BF16), up_t, stats, x, next_gain)


def _tile_heads(g, n):
    return jnp.tile(g.astype(F32), n)


def _even_mixer(x, h, ffn_gain, w_in, f_bias, qn_a, kn_a, qn_b, kn_b, w_out, tabs, *, batch, seq):
    w = w_in.astype(BF16)
    o_qa, o_ka, o_va, o_qb, o_kb, o_vb, o_gb, o_fb = (
        0, A_W, 2 * A_W, 3 * A_W, 3 * A_W + B_W, 3 * A_W + 2 * B_W, 3 * A_W + 3 * B_W, 3 * A_W + 4 * B_W)
    gain_a = jnp.concatenate([_tile_heads(qn_a, MOBA_HEADS), _tile_heads(kn_a, MOBA_HEADS)])
    qk_a, km = _proj(h, w[:, o_qa:o_va], seq=seq, tn=A_W, gain=gain_a, rope_tabs=tabs, kmean=True)
    nb = seq // MOBA_BLOCK
    kmean = km.reshape(batch, nb, 2 * A_W)[:, :, A_W:]
    kmean = jnp.pad(kmean, ((0, 0), (0, LANES - nb), (0, 0))).astype(BF16)
    gain_b = jnp.concatenate([_tile_heads(qn_b, FOX_HEADS), _tile_heads(kn_b, FOX_HEADS)])
    qk_b = _proj(h, w[:, o_qb:o_vb], seq=seq, tn=B_W, gain=gain_b)
    w_plain = jnp.concatenate([w[:, o_va:o_qb], w[:, o_vb:o_fb]], axis=1)
    vvg = _proj(h, w_plain, seq=seq, tn=A_W)
    va, vb, gb = vvg[:, :A_W], vvg[:, A_W:A_W + B_W], vvg[:, A_W + B_W:]
    frow = _fox_gates(h, w[:, o_fb:].T, f_bias, batch=batch, seq=seq)
    oa = _moba(qk_a[:, :A_W], qk_a[:, A_W:], va, kmean, batch=batch, seq=seq)
    logit_bound = 1.01 * ATTN_SCALE * HEAD_DIM * jnp.max(jnp.abs(qn_b)) * jnp.max(jnp.abs(kn_b)) + 0.01
    ob = _fox(qk_b[:, :B_W], qk_b[:, B_W:], vb, gb, frow, logit_bound, batch=batch, seq=seq)
    return _outproj([oa, ob], w_out.astype(BF16), x, ffn_gain)


def _odd_mixer(x, h, ffn_gain, w_in, qn, kn, sinks, w_out, tabs, *, batch, seq):
    w = w_in.astype(BF16)
    qw = SWA_Q_HEADS * HEAD_DIM
    kw = SWA_KV_HEADS * HEAD_DIM
    q = _proj(h, w[:, :qw], seq=seq, tn=512, gain=_tile_heads(qn, SWA_Q_HEADS), rope_tabs=tabs)
    k = _proj(h, w[:, qw:qw + kw], seq=seq, tn=kw, gain=_tile_heads(kn, SWA_KV_HEADS), rope_tabs=tabs)
    v = _proj(h, w[:, qw + kw:], seq=seq, tn=kw)
    o = _swa(q, k, v, sinks, batch=batch, seq=seq)
    return _outproj([o], w_out.astype(BF16), x, ffn_gain)


def kernel(x, attn_norm, ffn_norm, ev_w_in, ev_forget_bias, ev_q_norm_a, ev_k_norm_a, ev_q_norm_b,
           ev_k_norm_b, ev_w_out, od_w_in, od_q_norm, od_k_norm, od_sinks, od_w_out,
           peer_w_query, peer_sub_keys, peer_down, peer_up):
    batch, seq, d_model = x.shape
    depth = attn_norm.shape[0]
    tabs = _rope_tables(seq)
    xt = x.reshape(batch * seq, d_model)
    h = _rmsnorm(xt, attn_norm[0])
    for l in range(depth):
        i = l // 2
        if l % 2 == 0:
            xt, h2 = _even_mixer(xt, h, ffn_norm[l], ev_w_in[i], ev_forget_bias[i], ev_q_norm_a[i],
                                 ev_k_norm_a[i], ev_q_norm_b[i], ev_k_norm_b[i], ev_w_out[i], tabs,
                                 batch=batch, seq=seq)
        else:
            xt, h2 = _odd_mixer(xt, h, ffn_norm[l], od_w_in[i], od_q_norm[i], od_k_norm[i], od_sinks[i],
                                od_w_out[i], tabs, batch=batch, seq=seq)
        next_gain = attn_norm[l + 1] if l + 1 < depth else None
        xt, h = _peer_layer(xt, h2, next_gain, peer_w_query[l], peer_sub_keys[l], peer_down[l], peer_up[l])
    return xt.reshape(batch, seq, d_model)
```

```python
import functools

import numpy as np
import jax
import jax.numpy as jnp
from jax import lax
from jax.experimental import pallas as pl
from jax.experimental.pallas import tpu as pltpu

F32 = jnp.float32
BF16 = jnp.bfloat16

HEAD_DIM = 64
ROT_DIM = HEAD_DIM // 4
ROPE_THETA = 500000.0
ATTN_SCALE = HEAD_DIM ** -0.5
EPS = 1e-6
NEG_INF = -1e30

MOBA_HEADS = 8
FOX_HEADS = 8
MOBA_BLOCK = 256
MOBA_TOPK = 3
A_W = MOBA_HEADS * HEAD_DIM
B_W = FOX_HEADS * HEAD_DIM

SWA_Q_HEADS = 16
SWA_KV_HEADS = 2
SWA_WINDOW = 128

PEER_HEADS = 8
PEER_N_KEYS = 128
PEER_TOPK = 16
PEER_QUERY_DIM = 128

LANES = 128
BF16_ROWS = 16
PAIR_W = 2 * HEAD_DIM

ROW_TILE = 512
FLASH_TILE = 2 * MOBA_BLOCK
ROUTE_TILE = 2 * LANES
EXPERT_TOKEN_TILE = 512
VMEM_LIMIT = 48 * 1024 * 1024
VMEM_LIMIT_EXPERTS = 58 * 1024 * 1024


def _cparams(sem, vmem_limit=VMEM_LIMIT):
    return pltpu.CompilerParams(dimension_semantics=sem, vmem_limit_bytes=vmem_limit)


def _dot_nt(a, b):
    return lax.dot_general(a, b, (((1,), (1,)), ((), ())), preferred_element_type=F32)


def _dot(a, b):
    return jnp.dot(a, b, preferred_element_type=F32)


def _split2(x):
    h1 = x.astype(BF16)
    return h1, (x - h1.astype(F32)).astype(BF16)


def _split3(x):
    h1 = x.astype(BF16)
    r1 = x - h1.astype(F32)
    h2 = r1.astype(BF16)
    h3 = (r1 - h2.astype(F32)).astype(BF16)
    return h1, h2, h3


def _rmsnorm_kernel(x_ref, g_ref, o_ref):
    x = x_ref[...]
    ms = jnp.mean(x * x, axis=-1, keepdims=True)
    o_ref[...] = (x * lax.rsqrt(ms + EPS) * g_ref[...]).astype(o_ref.dtype)


def _rmsnorm(x, gain, tm=ROW_TILE):
    T, D = x.shape
    return pl.pallas_call(
        _rmsnorm_kernel,
        grid=(T // tm,),
        in_specs=[pl.BlockSpec((tm, D), lambda i: (i, 0)),
                  pl.BlockSpec((1, D), lambda i: (0, 0))],
        out_specs=pl.BlockSpec((tm, D), lambda i: (i, 0)),
        out_shape=jax.ShapeDtypeStruct((T, D), BF16),
        compiler_params=_cparams(("parallel",)),
        name="rmsnorm",
    )(x, gain.reshape(1, D))


def _proj_kernel(*refs, norm, rope, kmean, tn):
    it = iter(refs)
    h_ref, w_ref = next(it), next(it)
    gain_ref = next(it) if norm else None
    bd_ref = next(it) if norm else None
    if rope:
        c_ref, sa_ref, sb_ref = next(it), next(it), next(it)
    o_ref = next(it)
    km_ref = next(it) if kmean else None

    y = _dot(h_ref[...], w_ref[...])
    if norm:
        y2 = y * y
        bd = bd_ref[...]
        cols = []
        tm = y.shape[0]
        for c in range(tn // LANES):
            h1, h2 = _split2(y2[:, c * LANES:(c + 1) * LANES])
            t = _dot(jnp.concatenate([h1, h2], axis=0), bd)
            cols.append(t[:tm] + t[tm:])
        ms = cols[0] if len(cols) == 1 else jnp.concatenate(cols, axis=1)
        y = y * lax.rsqrt(ms + EPS) * gain_ref[...]
    if rope:
        rep = tn // LANES
        tile = (lambda t: t) if rep == 1 else (lambda t: jnp.concatenate([t] * rep, axis=1))
        y = (y * tile(c_ref[...])
             + pltpu.roll(y, tn - ROT_DIM // 2, 1) * tile(sa_ref[...])
             + pltpu.roll(y, ROT_DIM // 2, 1) * tile(sb_ref[...]))
    o_ref[...] = y.astype(o_ref.dtype)
    if kmean:
        for r in range(km_ref.shape[0]):
            km_ref[r] = jnp.mean(y[r * MOBA_BLOCK:(r + 1) * MOBA_BLOCK], axis=0, keepdims=True)


def _proj(h, w, *, seq, tn, tm=ROW_TILE, gain=None, rope_tabs=None, kmean=False):
    T, D = h.shape
    N = w.shape[1]
    norm = gain is not None
    rope = rope_tabs is not None
    nseq = seq // tm
    in_specs = [pl.BlockSpec((tm, D), lambda i, j: (i, 0)),
                pl.BlockSpec((D, tn), lambda i, j: (0, j))]
    args = [h, w]
    if norm:
        bd = np.kron(np.eye(LANES // HEAD_DIM), np.ones((HEAD_DIM, HEAD_DIM))) / HEAD_DIM
        in_specs += [pl.BlockSpec((1, tn), lambda i, j: (0, j)),
                     pl.BlockSpec((LANES, LANES), lambda i, j: (0, 0))]
        args += [gain.reshape(1, N).astype(F32), jnp.asarray(bd, BF16)]
    if rope:
        in_specs += [pl.BlockSpec((tm, LANES), lambda i, j: (i % nseq, 0))] * 3
        args += list(rope_tabs)
    out_specs = [pl.BlockSpec((tm, tn), lambda i, j: (i, j))]
    out_shape = [jax.ShapeDtypeStruct((T, N), BF16)]
    if kmean:
        out_specs.append(pl.BlockSpec((tm // MOBA_BLOCK, 1, tn), lambda i, j: (i, 0, j)))
        out_shape.append(jax.ShapeDtypeStruct((T // MOBA_BLOCK, 1, N), F32))
    res = pl.pallas_call(
        functools.partial(_proj_kernel, norm=norm, rope=rope, kmean=kmean, tn=tn),
        grid=(T // tm, N // tn),
        in_specs=in_specs, out_specs=out_specs, out_shape=out_shape,
        compiler_params=_cparams(("parallel", "parallel")),
        name="proj",
    )(*args)
    return res if kmean else res[0]


def _rope_tables(seq):
    half = ROT_DIM // 2
    inv_freq = jnp.power(ROPE_THETA, -jnp.arange(0, ROT_DIM, 2, dtype=F32) / ROT_DIM)
    ang = jnp.arange(seq, dtype=F32)[:, None] * inv_freq[None, :]
    cos, sin = jnp.cos(ang), jnp.sin(ang)
    one = jnp.ones((seq, HEAD_DIM - ROT_DIM), F32)
    zero = jnp.zeros((seq, HEAD_DIM - ROT_DIM), F32)
    z8 = jnp.zeros((seq, half), F32)
    c = jnp.concatenate([cos, cos, one], axis=1)
    sa = jnp.concatenate([-sin, z8, zero], axis=1)
    sb = jnp.concatenate([z8, sin, zero], axis=1)
    rep = LANES // HEAD_DIM
    return tuple(jnp.concatenate([t] * rep, axis=1) for t in (c, sa, sb))


def _gates_kernel(h_ref, wf_ref, b_ref, tri_ref, o_ref, carry_ref):
    @pl.when(pl.program_id(1) == 0)
    def _():
        carry_ref[...] = jnp.zeros_like(carry_ref)

    z = _dot_nt(wf_ref[...], h_ref[...]) + b_ref[...][:, :1]
    lf = jnp.minimum(z, 0.0) - jnp.log1p(jnp.exp(-jnp.abs(z)))
    tri = tri_ref[...]
    h1, h2, h3 = _split3(lf)
    cs = _dot(h1, tri) + _dot(h2, tri) + _dot(h3, tri) + carry_ref[...][:, :1]
    o_ref[0] = cs
    carry_ref[...] = jnp.broadcast_to(cs[:, -1:], carry_ref.shape)


def _fox_gates(h, wf_t, bias, *, batch, seq, tm=ROW_TILE):
    T, D = h.shape
    nh = wf_t.shape[0]
    nseq = seq // tm
    tri = jnp.asarray(np.triu(np.ones((tm, tm))), BF16)
    return pl.pallas_call(
        _gates_kernel,
        grid=(batch, nseq),
        in_specs=[pl.BlockSpec((tm, D), lambda b, s: (b * nseq + s, 0)),
                  pl.BlockSpec((nh, D), lambda b, s: (0, 0)),
                  pl.BlockSpec((nh, LANES), lambda b, s: (0, 0)),
                  pl.BlockSpec((tm, tm), lambda b, s: (0, 0))],
        out_specs=pl.BlockSpec((1, nh, tm), lambda b, s: (b, 0, s)),
        out_shape=jax.ShapeDtypeStruct((batch, nh, seq), F32),
        scratch_shapes=[pltpu.VMEM((nh, LANES), F32)],
        compiler_params=_cparams(("parallel", "arbitrary")),
        name="fox_gates",
    )(h, wf_t, jnp.broadcast_to(bias.astype(F32)[:, None], (nh, LANES)), tri)


def _lane_tile(x, width):
    rep = width // LANES
    return x if rep == 1 else jnp.concatenate([x] * rep, axis=1)


def _flash_init(m_ref, acc_ref):
    m_ref[...] = jnp.full(m_ref.shape, NEG_INF, F32)
    acc_ref[...] = jnp.zeros(acc_ref.shape, F32)


def _head_values(v):
    lane = lax.broadcasted_iota(jnp.int32, v.shape, 1)
    return [jnp.where((lane >= hh * HEAD_DIM) & (lane < (hh + 1) * HEAD_DIM), v, jnp.ones_like(v))
            for hh in range(2)]


def _flash_update(slot, s, v, m_ref, acc_ref):
    tk = s.shape[1]
    m_prev = m_ref[slot]
    m_new = jnp.maximum(m_prev, jnp.max(s, axis=1, keepdims=True))
    alpha = jnp.exp(m_prev - m_new)
    p = jnp.exp(s - _lane_tile(m_new, tk))
    acc_ref[slot] = alpha * acc_ref[slot] + _dot(p.astype(BF16), v)
    m_ref[slot] = m_new


def _flash_finish(lane, acc_ref):
    outs = []
    for hh in range(2):
        acc = acc_ref[hh]
        den = (1 - hh) * HEAD_DIM
        outs.append(acc / acc[:, den:den + 1])
    return jnp.where(lane < HEAD_DIM, outs[0], outs[1])


def _flash_tiles(qi, tk, logits, values, m_ref, acc_ref):
    def absorb(s, off):
        vh = values(off)
        for hh in range(2):
            _flash_update(hh, s[hh], vh[hh], m_ref, acc_ref)

    def pair(off_a, off_b, b_diagonal):
        sa = logits(off_a, False)
        sb = logits(off_b, b_diagonal)
        absorb(sa, off_a)
        absorb(sb, off_b)

    def body(jj, carry):
        off = pl.multiple_of(2 * jj * tk, 2 * tk)
        pair(off, off + tk, False)
        return carry

    lax.fori_loop(0, qi // 2, body, 0)
    diag = pl.multiple_of(qi * tk, tk)

    @pl.when(qi % 2 == 1)
    def _():
        pair(diag - tk, diag, True)

    @pl.when(qi % 2 == 0)
    def _():
        absorb(logits(diag, True), diag)


def _head_queries(q, lane):
    qs = q * ATTN_SCALE
    return [jnp.where((lane >= hh * HEAD_DIM) & (lane < (hh + 1) * HEAD_DIM), qs, jnp.zeros_like(qs))
            for hh in range(2)]


def _moba_kernel(q_ref, k_ref, v_ref, km_ref, o_ref, m_ref, acc_ref):
    tq = q_ref.shape[1]
    tk = tq
    qi = pl.program_id(2)
    lane = lax.broadcasted_iota(jnp.int32, (tq, LANES), 1)
    lane_f = lane.astype(F32)
    rowv = lax.broadcasted_iota(jnp.int32, (tq, LANES), 0)
    row_blk = 2 * qi + (rowv >= MOBA_BLOCK).astype(jnp.int32)
    row = lax.broadcasted_iota(jnp.int32, (tq, tk), 0)
    col = lax.broadcasted_iota(jnp.int32, (tq, tk), 1)
    qh = _head_queries(q_ref[0], lane)
    _flash_init(m_ref, acc_ref)

    sels = []
    for hh in range(2):
        gate = _dot_nt(qh[hh], km_ref[0])
        gate = jnp.where(lane < row_blk, gate, -jnp.inf)
        sel = jnp.zeros((tq, LANES), F32)
        for _ in range(MOBA_TOPK):
            m = jnp.max(gate, axis=1, keepdims=True)
            idx = jnp.min(jnp.where(gate == m, lane_f, float(LANES)), axis=1, keepdims=True)
            hit = lane_f == idx
            sel = jnp.where(hit & (m > -jnp.inf), 1.0, sel)
            gate = jnp.where(hit, -jnp.inf, gate)
        sels.append(sel)

    def chosen(sel, blk):
        return jnp.max(jnp.where(lane == blk, sel, 0.0), axis=1, keepdims=True) > 0.0

    def logits(off, diagonal):
        kj = k_ref[0, pl.ds(off, tk), :]
        out = []
        for hh in range(2):
            s = _dot_nt(qh[hh], kj)
            if diagonal:
                visible = (col >= MOBA_BLOCK) | (row < MOBA_BLOCK) | chosen(sels[hh], 2 * qi)
                s = jnp.where((col <= row) & visible, s, NEG_INF)
            else:
                blk = 2 * (off // tk)
                s = jnp.concatenate(
                    [jnp.where(chosen(sels[hh], blk), s[:, :MOBA_BLOCK], NEG_INF),
                     jnp.where(chosen(sels[hh], blk + 1), s[:, MOBA_BLOCK:], NEG_INF)], axis=1)
            out.append(s)
        return out

    def values(off):
        return _head_values(v_ref[0, pl.ds(off, tk), :])

    _flash_tiles(qi, tk, logits, values, m_ref, acc_ref)
    o_ref[0] = _flash_finish(lane, acc_ref).astype(o_ref.dtype)


def _flash_scratch(tq):
    return [pltpu.VMEM((2, tq, LANES), F32)] * 2


def _moba(q, k, v, kmean, *, batch, seq):
    W = q.shape[1]
    tq = FLASH_TILE
    q3, k3, v3 = (t.reshape(batch, seq, W) for t in (q, k, v))
    out = pl.pallas_call(
        _moba_kernel,
        grid=(batch, W // PAIR_W, seq // tq),
        in_specs=[pl.BlockSpec((1, tq, PAIR_W), lambda b, p, i: (b, i, p)),
                  pl.BlockSpec((1, seq, PAIR_W), lambda b, p, i: (b, 0, p)),
                  pl.BlockSpec((1, seq, PAIR_W), lambda b, p, i: (b, 0, p)),
                  pl.BlockSpec((1, LANES, PAIR_W), lambda b, p, i: (b, 0, p))],
        out_specs=pl.BlockSpec((1, tq, PAIR_W), lambda b, p, i: (b, i, p)),
        out_shape=jax.ShapeDtypeStruct((batch, seq, W), BF16),
        scratch_shapes=_flash_scratch(tq),
        compiler_params=_cparams(("parallel", "parallel", "parallel")),
        name="moba",
    )(q3, k3, v3, kmean)
    return out.reshape(batch * seq, W)


_EXP_UNDERFLOW = 112.0


def _fox_kernel(q_ref, k_ref, v_ref, g_ref, frow_ref, far_ref, o_ref, m_ref, acc_ref):
    tq = q_ref.shape[1]
    tk = tq
    pr = pl.program_id(1)
    qi = pl.program_id(2)
    lane = lax.broadcasted_iota(jnp.int32, (tq, LANES), 1)
    row = lax.broadcasted_iota(jnp.int32, (tq, tk), 0)
    col = lax.broadcasted_iota(jnp.int32, (tq, tk), 1)
    qh = _head_queries(q_ref[0], lane)
    _flash_init(m_ref, acc_ref)

    def key_gates(off, width):
        f_all = frow_ref[0, :, pl.ds(off, width)]
        sub = lax.broadcasted_iota(jnp.int32, f_all.shape, 0)
        return [jnp.sum(jnp.where(sub == 2 * pr + hh, f_all, 0.0), axis=0, keepdims=True) for hh in range(2)]

    f_ref = [f[:, :1] for f in key_gates(pl.multiple_of(qi * tq, tq), LANES)]

    def logits(off, diagonal):
        kj = k_ref[0, pl.ds(off, tk), :]
        fk = key_gates(off, tk)
        out = []
        for hh in range(2):
            s = _dot_nt(qh[hh], kj) - (fk[hh] - f_ref[hh])
            out.append(jnp.where(col <= row, s, NEG_INF) if diagonal else s)
        return out

    def absorb(s, off):
        vh = _head_values(v_ref[0, pl.ds(off, tk), :])
        for hh in range(2):
            _flash_update(hh, s[hh], vh[hh], m_ref, acc_ref)

    def vanishes(j):
        tail = frow_ref[0, :, pl.ds(pl.multiple_of((j + 1) * tk - LANES, LANES), LANES)]
        sub = lax.broadcasted_iota(jnp.int32, tail.shape, 0)
        last = lax.broadcasted_iota(jnp.int32, tail.shape, 1) == LANES - 1
        gap = tail - jnp.where(sub == 2 * pr, f_ref[0], f_ref[1]) - far_ref[...][:, :1]
        mine = last & ((sub == 2 * pr) | (sub == 2 * pr + 1))
        return jnp.min(jnp.where(mine, gap, jnp.inf)) >= 0.0

    absorb(logits(pl.multiple_of(qi * tk, tk), True), pl.multiple_of(qi * tk, tk))

    def more(j):
        return jnp.logical_and(j >= 1, jnp.logical_not(vanishes(jnp.maximum(j, 0))))

    def pair(j):
        off_a = pl.multiple_of(j * tk, tk)
        off_b = pl.multiple_of((j - 1) * tk, tk)
        sa = logits(off_a, False)
        sb = logits(off_b, False)
        absorb(sa, off_a)
        absorb(sb, off_b)
        return j - 2

    j = lax.while_loop(more, pair, qi - 1)

    @pl.when(jnp.logical_and(j == 0, jnp.logical_not(vanishes(0))))
    def _():
        absorb(logits(0, False), 0)

    o = _flash_finish(lane, acc_ref)
    o_ref[0] = (o * jax.nn.sigmoid(g_ref[0].astype(F32))).astype(o_ref.dtype)


def _fox(q, k, v, g, frow, logit_bound, *, batch, seq, tq=FLASH_TILE):
    far = jnp.full((1, LANES), 2.0 * logit_bound + _EXP_UNDERFLOW, F32)
    W = q.shape[1]
    nh = frow.shape[1]
    q3, k3, v3, g3 = (t.reshape(batch, seq, W) for t in (q, k, v, g))
    out = pl.pallas_call(
        _fox_kernel,
        grid=(batch, W // PAIR_W, seq // tq),
        in_specs=[pl.BlockSpec((1, tq, PAIR_W), lambda b, p, i: (b, i, p)),
                  pl.BlockSpec((1, seq, PAIR_W), lambda b, p, i: (b, 0, p)),
                  pl.BlockSpec((1, seq, PAIR_W), lambda b, p, i: (b, 0, p)),
                  pl.BlockSpec((1, tq, PAIR_W), lambda b, p, i: (b, i, p)),
                  pl.BlockSpec((1, nh, seq), lambda b, p, i: (b, 0, 0)),
                  pl.BlockSpec((1, LANES), lambda b, p, i: (0, 0))],
        out_specs=pl.BlockSpec((1, tq, PAIR_W), lambda b, p, i: (b, i, p)),
        out_shape=jax.ShapeDtypeStruct((batch, seq, W), BF16),
        scratch_shapes=_flash_scratch(tq),
        compiler_params=_cparams(("parallel", "parallel", "parallel")),
        name="fox",
    )(q3, k3, v3, g3, frow, far)
    return out.reshape(batch * seq, W)


def _swa_kernel(q_ref, k_ref, v_ref, sink_ref, bias_ref, o_ref):
    tq = q_ref.shape[1]
    qi = pl.program_id(1)
    group = SWA_Q_HEADS // SWA_KV_HEADS
    tk = tq + SWA_WINDOW
    lane = lax.broadcasted_iota(jnp.int32, (tq, LANES), 1)
    kstart = pl.multiple_of(jnp.maximum(qi * tq - SWA_WINDOW, 0), SWA_WINDOW)
    k = k_ref[0, pl.ds(kstart, tk), :]
    vh = _head_values(v_ref[0, pl.ds(kstart, tk), :])
    swap = lambda t: pltpu.roll(t.astype(F32), HEAD_DIM, 1).astype(BF16)
    k_by_half = [k, swap(k)]
    bias = bias_ref[jnp.minimum(qi, 1)]
    bias = jnp.concatenate([bias] * (group // 2), axis=0)
    sink_tab = sink_ref[...]

    chains = []
    for c in range(SWA_KV_HEADS):
        v_by_half = [vh[c], swap(vh[c])]
        for swapped in range(2):
            hh = c if not swapped else 1 - c
            heads = [h for h in range(c * group, (c + 1) * group) if h % 2 == hh]
            pieces, sinks = [], []
            for head in heads:
                blk = q_ref[0, :, (head // 2) * PAIR_W:(head // 2 + 1) * PAIR_W] * ATTN_SCALE
                pieces.append(jnp.where((lane >= hh * HEAD_DIM) & (lane < (hh + 1) * HEAD_DIM),
                                        blk, jnp.zeros_like(blk)))
                sinks.append(jnp.broadcast_to(sink_tab[head:head + 1, :], (tq, LANES)))
            s = _dot_nt(jnp.concatenate(pieces, axis=0), k_by_half[swapped]) + bias
            chains.append((heads, s, jnp.concatenate(sinks, axis=0), v_by_half[swapped]))

    outs = [None] * SWA_Q_HEADS
    for heads, s, sink, v in chains:
        m = jnp.maximum(sink, jnp.max(s, axis=1, keepdims=True))
        p = jnp.exp(s - _lane_tile(m, tk))
        acc = _dot(p.astype(BF16), v)
        den = pltpu.roll(acc, HEAD_DIM, 1) + jnp.exp(sink - m)
        o = acc / den
        for idx, head in enumerate(heads):
            outs[head] = o[idx * tq:(idx + 1) * tq]
    for pp in range(SWA_Q_HEADS // 2):
        o_ref[0, :, pp * PAIR_W:(pp + 1) * PAIR_W] = jnp.where(
            lane < HEAD_DIM, outs[2 * pp], outs[2 * pp + 1]).astype(o_ref.dtype)


def _swa_bias(tq):
    r = np.arange(tq)[:, None]
    c = np.arange(tq + SWA_WINDOW)[None, :]
    tabs = []
    for key_offset in (0, SWA_WINDOW):
        dist = r + key_offset - c
        tabs.append(np.where((dist >= 0) & (dist < SWA_WINDOW), 0.0, NEG_INF))
    return jnp.asarray(np.stack(tabs), F32)


def _swa(q, k, v, sinks, *, batch, seq, tq=SWA_WINDOW):
    W = q.shape[1]
    tk = tq + SWA_WINDOW
    q3 = q.reshape(batch, seq, W)
    k3, v3 = (t.reshape(batch, seq, PAIR_W) for t in (k, v))
    sink_tab = jnp.broadcast_to(sinks.astype(F32)[:, None], (SWA_Q_HEADS, LANES))
    out = pl.pallas_call(
        _swa_kernel,
        grid=(batch, seq // tq),
        in_specs=[pl.BlockSpec((1, tq, W), lambda b, i: (b, i, 0)),
                  pl.BlockSpec((1, seq, PAIR_W), lambda b, i: (b, 0, 0)),
                  pl.BlockSpec((1, seq, PAIR_W), lambda b, i: (b, 0, 0)),
                  pl.BlockSpec((SWA_Q_HEADS, LANES), lambda b, i: (0, 0)),
                  pl.BlockSpec((2, tq, tk), lambda b, i: (0, 0, 0))],
        out_specs=pl.BlockSpec((1, tq, W), lambda b, i: (b, i, 0)),
        out_shape=jax.ShapeDtypeStruct((batch, seq, W), BF16),
        compiler_params=_cparams(("parallel", "parallel")),
        name="swa",
    )(q3, k3, v3, sink_tab, _swa_bias(tq))
    return out.reshape(batch * seq, W)


def _rms_normed(x, gain):
    ms = jnp.mean(x * x, axis=-1, keepdims=True)
    return (x * lax.rsqrt(ms + EPS) * gain).astype(BF16)


def _outproj_kernel(*refs, n_parts):
    parts = refs[:n_parts]
    w_ref, x_ref, g_ref, o_ref, h_ref = refs[n_parts:]
    y = x_ref[...]
    off = 0
    for p_ref in parts:
        kw = p_ref.shape[1]
        y = y + _dot(p_ref[...], w_ref[off:off + kw, :])
        off += kw
    o_ref[...] = y
    h_ref[...] = _rms_normed(y, g_ref[...])


def _outproj(parts, w, x, next_gain, tm=ROW_TILE):
    T, D = x.shape
    in_specs = [pl.BlockSpec((tm, p.shape[1]), lambda i: (i, 0)) for p in parts]
    in_specs += [pl.BlockSpec(w.shape, lambda i: (0, 0)),
                 pl.BlockSpec((tm, D), lambda i: (i, 0)),
                 pl.BlockSpec((1, D), lambda i: (0, 0))]
    row_spec = pl.BlockSpec((tm, D), lambda i: (i, 0))
    return pl.pallas_call(
        functools.partial(_outproj_kernel, n_parts=len(parts)),
        grid=(T // tm,),
        in_specs=in_specs,
        out_specs=[row_spec, row_spec],
        out_shape=[jax.ShapeDtypeStruct((T, D), F32), jax.ShapeDtypeStruct((T, D), BF16)],
        compiler_params=_cparams(("parallel",)),
        name="outproj",
    )(*parts, w, x, next_gain.reshape(1, D).astype(F32))


_CAND_ROWS = 80


def _cand_tables(lanes):
    pos = np.zeros((_CAND_ROWS,), np.float32)
    neg = np.zeros((_CAND_ROWS,), np.float32)
    r = 0
    for a, nb in ((0, 16), (1, 8), (2, 8), (3, 8), (4, 8), (5, 8), (6, 8), (7, 8)):
        for b in range(nb):
            pos[r] = a * PEER_TOPK + b
            neg[r] = 0.0 if (a + 1) * (b + 1) <= PEER_TOPK else -np.inf
            r += 1
    for a in range(8, 16):
        pos[r] = a * PEER_TOPK
        r += 1
    assert r == _CAND_ROWS
    tab = lambda t: jnp.asarray(np.broadcast_to(t[:, None], (_CAND_ROWS, lanes)).copy())
    return tab(pos), tab(neg)


def _batcher_pairs(n):
    pairs, p = [], 1
    while p < n:
        k = p
        while k >= 1:
            for j in range(k % p, n - k, 2 * k):
                for i in range(min(k, n - j - k)):
                    if (i + j) // (2 * p) == (i + j + k) // (2 * p):
                        pairs.append((i + j, i + j + k))
            k //= 2
        p *= 2
    return pairs


_SORT16 = _batcher_pairs(PEER_TOPK)
_N_CAND_PIECES = 10
_SORT10 = [(i, j) for i, j in _SORT16 if j < _N_CAND_PIECES]
_SUBLANES = 8


def _compare_exchange(items, i, j):
    items[i], items[j] = jnp.maximum(items[i], items[j]), jnp.minimum(items[i], items[j])


def _top_sorted(pieces, pairs):
    items = list(pieces)
    for i, j in pairs:
        _compare_exchange(items, i, j)
    n = PEER_TOPK
    items += [jnp.full(items[0].shape, -jnp.inf, F32)] * (n - len(items))
    for shift in (4, 2, 1):
        items = [jnp.maximum(items[i], pltpu.roll(items[n - 1 - i], shift, 0)) for i in range(n)]
        d = n // 2
        while d >= 1:
            for i in range(n):
                if i & d == 0:
                    _compare_exchange(items, i, i + d)
            d //= 2
    return items


def _pieces(x):
    return [x[_SUBLANES * g:_SUBLANES * (g + 1)] for g in range(x.shape[0] // _SUBLANES)]


def _sublane_total(x):
    for shift in (4, 2, 1):
        x = x + pltpu.roll(x, shift, 0)
    return x


def _count_ge(pieces, thr):
    total = jnp.zeros(thr.shape, F32)
    for p in pieces:
        total = total + jnp.where(p >= thr, 1.0, 0.0)
    return _sublane_total(total)


def _route_head_fast(s1, s2):
    p1, p2 = _pieces(s1), _pieces(s2)
    v1 = _top_sorted(p1, _SORT16)
    v2 = _top_sorted(p2, _SORT16)
    sub = lax.broadcasted_iota(jnp.int32, v1[0].shape, 0)

    def spread(vals):
        out = vals[0]
        for r in range(1, _SUBLANES):
            out = jnp.where(sub == r, vals[r], out)
        return out

    v2_lo, v2_hi, v1_hi = spread(v2[:8]), spread(v2[8:]), spread(v1[8:])
    cands = [v1[0] + v2_lo, v1[0] + v2_hi, v1[1] + v2_lo]
    for a in range(2, 8):
        cands.append(jnp.where(sub < PEER_TOPK // (a + 1), v1[a] + v2_lo, -jnp.inf))
    cands.append(v1_hi + v2[0])
    ts = _top_sorted(cands, _SORT10)
    tau = ts[PEER_TOPK - 1]
    z = jnp.exp(ts[0] - ts[0])
    for kk in range(1, PEER_TOPK):
        z = z + jnp.exp(ts[kk] - ts[0])

    tied = (_count_ge(p1, v1[-1]) != float(PEER_TOPK)) | (_count_ge(p2, v2[-1]) != float(PEER_TOPK))
    tied = tied | (_count_ge(cands, tau) != float(PEER_TOPK))
    for b in range(PEER_TOPK - 1):
        tied = tied | (v1[b] == v1[b + 1]) | (v2[b] == v2[b + 1])

    cnt = []
    for a in range(PEER_TOPK):
        c = jnp.zeros(tau.shape, F32)
        for b in range(PEER_TOPK // (a + 1)):
            c = c + jnp.where(v1[a] + v2[b] >= tau, 1.0, 0.0)
        cnt.append(c)
    c1, r2 = [], []
    for x in p1:
        c = jnp.zeros(x.shape, F32)
        for a in range(PEER_TOPK):
            c = jnp.where(x == v1[a], cnt[a], c)
        c1.append(c)
    for x in p2:
        r = jnp.zeros(x.shape, F32)
        for b in range(PEER_TOPK):
            r = r + jnp.where(v2[b] > x, 1.0, 0.0)
        r2.append(r)
    inv_z = 1.0 / z
    e1 = [jnp.exp(x - v1[0]) * inv_z for x in p1]
    e2 = [jnp.exp(x - v2[0]) for x in p2]
    cat = lambda ps: jnp.concatenate(ps, axis=0)
    return (cat(c1), cat(e1), cat(r2), cat(e2)), tied


def _extract_sorted(scores, by_key):
    nk, lanes = scores[0].shape
    kio = lax.broadcasted_iota(jnp.int32, (nk, lanes), 0).astype(F32)
    slot = lax.broadcasted_iota(jnp.int32, (PEER_TOPK, lanes), 0)

    def body(a, carry):
        here = slot == a
        out = []
        for (v, vals, aux), ranked in zip(carry, by_key):
            m = jnp.max(v, axis=0, keepdims=True)
            idx = jnp.min(jnp.where(v == m, kio, float(nk)), axis=0, keepdims=True)
            hit = kio == idx
            aux = jnp.where(hit, jnp.asarray(a, F32), aux) if ranked else jnp.where(here, idx, aux)
            out.append((jnp.where(hit, -jnp.inf, v), jnp.where(here, m, vals), aux))
        return tuple(out)

    small = jnp.zeros((PEER_TOPK, lanes), F32)
    unranked = jnp.full((nk, lanes), float(PEER_TOPK), F32)
    init = tuple((v, small, unranked if ranked else small) for v, ranked in zip(scores, by_key))
    return [(vals, aux) for _, vals, aux in lax.fori_loop(0, PEER_TOPK, body, init)]


def _route_head_exact(s1, s2, pos, neg):
    lanes = s1.shape[1]
    slot = lax.broadcasted_iota(jnp.int32, (PEER_TOPK, lanes), 0)
    kio = lax.broadcasted_iota(jnp.int32, (PEER_N_KEYS, lanes), 0).astype(F32)
    (v1, idx1), (v2, rank2) = _extract_sorted([s1, s2], [False, True])
    blocks = [v1[0:1] + v2[0:8], v1[0:1] + v2[8:16]]
    blocks += [v1[a:a + 1] + v2[0:8] for a in range(1, 8)]
    blocks += [v1[8:16] + v2[0:1]]
    cand = jnp.concatenate(blocks, axis=0) + neg

    def pick(kk, carry):
        cand, chosen, ts = carry
        m = jnp.max(cand, axis=0, keepdims=True)
        first = jnp.min(jnp.where(cand == m, pos, 1e9), axis=0, keepdims=True)
        hit = pos == first
        return (jnp.where(hit, -jnp.inf, cand), jnp.where(hit, 1.0, chosen), jnp.where(slot == kk, m, ts))

    _, chosen, ts = lax.fori_loop(0, PEER_TOPK, pick,
                                  (cand, jnp.zeros_like(cand), jnp.zeros((PEER_TOPK, lanes), F32)))
    z = jnp.sum(jnp.exp(ts - ts[0:1]), axis=0, keepdims=True)
    counts = [jnp.sum(chosen[0:16], axis=0, keepdims=True)]
    counts += [jnp.sum(chosen[8 * a + 8:8 * a + 16], axis=0, keepdims=True) for a in range(1, 8)]
    counts += [chosen[72 + a:73 + a] for a in range(8)]
    c1 = jnp.zeros((PEER_N_KEYS, lanes), F32)
    for a in range(PEER_TOPK):
        c1 = jnp.where(kio == idx1[a:a + 1], counts[a], c1)
    return c1, jnp.exp(s1 - v1[0:1]) / z, rank2, jnp.exp(s2 - v2[0:1])


def _route_kernel(h_ref, wq_ref, keys_ref, pos_ref, neg_ref,
                  c1_ref, e1_ref, r2_ref, e2_ref, qt_ref, sc_ref):
    half = PEER_QUERY_DIM // 2
    qt_ref[...] = _dot_nt(wq_ref[...], h_ref[...]).astype(BF16)

    def store(h, maps):
        c1, e1, r2, e2 = maps
        c1_ref[h] = c1
        e1_ref[h] = e1
        r2_ref[h] = r2.astype(BF16)
        e2_ref[h] = e2.astype(BF16)

    def head_body(h, _):
        r0 = pl.multiple_of(h * PEER_QUERY_DIM, PEER_QUERY_DIM)
        sc_ref[0] = _dot(keys_ref[2 * h], qt_ref[pl.ds(r0, half), :])
        sc_ref[1] = _dot(keys_ref[2 * h + 1], qt_ref[pl.ds(r0 + half, half), :])
        maps, tied = _route_head_fast(sc_ref[0], sc_ref[1])
        any_tied = jnp.max(jnp.where(tied, 1.0, 0.0)) > 0.0

        @pl.when(any_tied)
        def _():
            store(h, _route_head_exact(sc_ref[0], sc_ref[1], pos_ref[...], neg_ref[...]))

        @pl.when(jnp.logical_not(any_tied))
        def _():
            store(h, maps)

        return 0

    lax.fori_loop(0, PEER_HEADS, head_body, 0)


def _peer_route(h2, wq_t, keys, tt=ROUTE_TILE):
    T, D = h2.shape
    pos, neg = _cand_tables(tt)
    stat_spec = pl.BlockSpec((PEER_HEADS, PEER_N_KEYS, tt), lambda i: (0, 0, i))
    stat = lambda dt: jax.ShapeDtypeStruct((PEER_HEADS, PEER_N_KEYS, T), dt)
    return pl.pallas_call(
        _route_kernel,
        grid=(T // tt,),
        in_specs=[pl.BlockSpec((tt, D), lambda i: (i, 0)),
                  pl.BlockSpec(wq_t.shape, lambda i: (0, 0)),
                  pl.BlockSpec(keys.shape, lambda i: (0, 0, 0)),
                  pl.BlockSpec((_CAND_ROWS, tt), lambda i: (0, 0)),
                  pl.BlockSpec((_CAND_ROWS, tt), lambda i: (0, 0))],
        out_specs=[stat_spec] * 4,
        out_shape=[stat(F32), stat(F32), stat(BF16), stat(BF16)],
        scratch_shapes=[pltpu.VMEM((PEER_HEADS * PEER_QUERY_DIM, tt), BF16),
                        pltpu.VMEM((2, PEER_N_KEYS, tt), F32)],
        compiler_params=_cparams(("parallel",)),
        name="peer_route",
    )(h2, wq_t, keys, pos, neg)


_KEY_GROUP = 16
_UNITS = 4
_DRAIN_PIECES = 2


def _build_gated(a_ref, p_ref, c1_ref, e1_ref, r2_ref, e2_ref, key0, g0, ng, lt):
    rep = PEER_N_KEYS // BF16_ROWS

    def rows16(row):
        blk = jnp.broadcast_to(row, (BF16_ROWS, LANES)).astype(BF16)
        return jnp.concatenate([blk] * rep, axis=0)

    ls = slice(lt * LANES, (lt + 1) * LANES)
    w = [jnp.zeros((PEER_N_KEYS, LANES), BF16) for _ in range(ng)]
    for h in range(PEER_HEADS):
        c1 = c1_ref[h, pl.ds(key0, _KEY_GROUP), ls]
        e1 = e1_ref[h, pl.ds(key0, _KEY_GROUP), ls]
        r2 = r2_ref[h, :, ls]
        e2 = e2_ref[h, :, ls]
        for g in range(ng):
            thr = rows16(c1[g0 + g:g0 + g + 1])
            gate = rows16(e1[g0 + g:g0 + g + 1])
            w[g] = w[g] + jnp.where(r2 < thr, e2, jnp.zeros_like(e2)) * gate
    for g in range(g0, g0 + ng):
        rs = slice(g * PEER_N_KEYS, (g + 1) * PEER_N_KEYS)
        a = a_ref[rs, ls]
        gelu = 0.5 * a * (1.0 + lax.erf(a * (2.0 ** -0.5)))
        p_ref[rs, ls] = gelu.astype(BF16) * w[g - g0]


def _experts_kernel(h_ref, dn_ref, upt_ref, c1_ref, e1_ref, r2_ref, e2_ref, x_ref, *rest, emit_norm):
    if emit_norm:
        g_ref, o_ref, hn_ref, a_ref, p_ref, acc_ref = rest
    else:
        o_ref, a_ref, p_ref, acc_ref = rest
    te, tt = a_ref.shape
    d_model = acc_ref.shape[0]
    e = pl.program_id(1)
    n_tiles = pl.num_programs(1) - 1
    cur = e % 2
    ng = _KEY_GROUP // _UNITS
    n_lane = tt // LANES
    key0 = pl.multiple_of(jnp.minimum(e, n_tiles - 1) * _KEY_GROUP, _KEY_GROUP)

    def front_mm(u):
        rows = te // _UNITS
        rs = slice(u * rows, (u + 1) * rows)
        a_ref[rs, :] = _dot_nt(dn_ref[rs, :], h_ref[...])

    def back_mm(r):
        rows = d_model // _DRAIN_PIECES
        rs = slice(r * rows, (r + 1) * rows)
        acc_ref[rs, :] += _dot(upt_ref[rs, :], p_ref[1 - cur])

    units_per_drain = _UNITS // _DRAIN_PIECES

    def run(front, back):
        if front:
            front_mm(0)
        for u in range(_UNITS):
            for lt in range(n_lane):
                if front:
                    _build_gated(a_ref, p_ref.at[cur], c1_ref, e1_ref, r2_ref, e2_ref, key0, u * ng, ng, lt)
                if front and lt == 0 and u + 1 < _UNITS:
                    front_mm(u + 1)
                if back and lt == n_lane // 2 and (u + 1) % units_per_drain == 0:
                    back_mm(u // units_per_drain)

    @pl.when(e == 0)
    def _():
        acc_ref[...] = jnp.zeros_like(acc_ref)
        run(True, False)

    @pl.when((e > 0) & (e < n_tiles))
    def _():
        run(True, True)

    @pl.when(e == n_tiles)
    def _():
        run(False, True)
        y = x_ref[...] + acc_ref[...].T
        o_ref[...] = y
        if emit_norm:
            hn_ref[...] = _rms_normed(y, g_ref[...])


def _peer_experts(h2, down, up_t, stats, x, next_gain, tt=EXPERT_TOKEN_TILE):
    T, D = h2.shape
    E = down.shape[0]
    te = _KEY_GROUP * PEER_N_KEYS
    n_tiles = E // te
    emit_norm = next_gain is not None
    stat_spec = pl.BlockSpec((PEER_HEADS, PEER_N_KEYS, tt), lambda i, e: (0, 0, i))
    row_spec = pl.BlockSpec((tt, D), lambda i, e: (i, 0))
    in_specs = [row_spec,
                pl.BlockSpec((te, D), lambda i, e: (jnp.minimum(e, n_tiles - 1), 0)),
                pl.BlockSpec((None, D, te), lambda i, e: (jnp.maximum(e - 1, 0), 0, 0)),
                stat_spec, stat_spec, stat_spec, stat_spec,
                row_spec]
    args = [h2, down, up_t, *stats, x]
    out_specs = [row_spec]
    out_shape = [jax.ShapeDtypeStruct((T, D), F32)]
    if emit_norm:
        in_specs.append(pl.BlockSpec((1, D), lambda i, e: (0, 0)))
        args.append(next_gain.reshape(1, D).astype(F32))
        out_specs.append(row_spec)
        out_shape.append(jax.ShapeDtypeStruct((T, D), BF16))
    res = pl.pallas_call(
        functools.partial(_experts_kernel, emit_norm=emit_norm),
        grid=(T // tt, n_tiles + 1),
        in_specs=in_specs, out_specs=out_specs, out_shape=out_shape,
        scratch_shapes=[pltpu.VMEM((te, tt), F32),
                        pltpu.VMEM((2, te, tt), BF16),
                        pltpu.VMEM((D, tt), F32)],
        compiler_params=_cparams(("parallel", "arbitrary"), VMEM_LIMIT_EXPERTS),
        name="peer_experts",
    )(*args)
    return (res[0], res[1]) if emit_norm else (res[0], None)


def _peer_layer(x, h2, next_gain, w_query, sub_keys, down, up):
    keys = sub_keys.reshape(PEER_HEADS * 2, PEER_N_KEYS, PEER_QUERY_DIM // 2).astype(BF16)
    stats = _peer_route(h2, w_query.T.astype(BF16), keys)
    te = _KEY_GROUP * PEER_N_KEYS
    up_t = up.reshape(up.shape[0] // te, te, up.shape[1]).transpose(0, 2, 1).astype(BF16)
    return _peer_experts(h2, down.astype(BF16), up_t, stats, x, next_gain)


def _tile_heads(g, n):
    return jnp.tile(g.astype(F32), n)


def _even_mixer(x, h, ffn_gain, w_in, f_bias, qn_a, kn_a, qn_b, kn_b, w_out, tabs, *, batch, seq):
    w = w_in.astype(BF16)
    o_qa, o_ka, o_va, o_qb, o_kb, o_vb, o_gb, o_fb = (
        0, A_W, 2 * A_W, 3 * A_W, 3 * A_W + B_W, 3 * A_W + 2 * B_W, 3 * A_W + 3 * B_W, 3 * A_W + 4 * B_W)
    gain_a = jnp.concatenate([_tile_heads(qn_a, MOBA_HEADS), _tile_heads(kn_a, MOBA_HEADS)])
    qk_a, km = _proj(h, w[:, o_qa:o_va], seq=seq, tn=A_W, gain=gain_a, rope_tabs=tabs, kmean=True)
    nb = seq // MOBA_BLOCK
    kmean = km.reshape(batch, nb, 2 * A_W)[:, :, A_W:]
    kmean = jnp.pad(kmean, ((0, 0), (0, LANES - nb), (0, 0))).astype(BF16)
    gain_b = jnp.concatenate([_tile_heads(qn_b, FOX_HEADS), _tile_heads(kn_b, FOX_HEADS)])
    qk_b = _proj(h, w[:, o_qb:o_vb], seq=seq, tn=B_W, gain=gain_b)
    w_plain = jnp.concatenate([w[:, o_va:o_qb], w[:, o_vb:o_fb]], axis=1)
    vvg = _proj(h, w_plain, seq=seq, tn=A_W)
    va, vb, gb = vvg[:, :A_W], vvg[:, A_W:A_W + B_W], vvg[:, A_W + B_W:]
    frow = _fox_gates(h, w[:, o_fb:].T, f_bias, batch=batch, seq=seq)
    oa = _moba(qk_a[:, :A_W], qk_a[:, A_W:], va, kmean, batch=batch, seq=seq)
    logit_bound = 1.01 * ATTN_SCALE * HEAD_DIM * jnp.max(jnp.abs(qn_b)) * jnp.max(jnp.abs(kn_b)) + 0.01
    ob = _fox(qk_b[:, :B_W], qk_b[:, B_W:], vb, gb, frow, logit_bound, batch=batch, seq=seq)
    return _outproj([oa, ob], w_out.astype(BF16), x, ffn_gain)


def _odd_mixer(x, h, ffn_gain, w_in, qn, kn, sinks, w_out, tabs, *, batch, seq):
    w = w_in.astype(BF16)
    qw = SWA_Q_HEADS * HEAD_DIM
    kw = SWA_KV_HEADS * HEAD_DIM
    q = _proj(h, w[:, :qw], seq=seq, tn=512, gain=_tile_heads(qn, SWA_Q_HEADS), rope_tabs=tabs)
    k = _proj(h, w[:, qw:qw + kw], seq=seq, tn=kw, gain=_tile_heads(kn, SWA_KV_HEADS), rope_tabs=tabs)
    v = _proj(h, w[:, qw + kw:], seq=seq, tn=kw)
    o = _swa(q, k, v, sinks, batch=batch, seq=seq)
    return _outproj([o], w_out.astype(BF16), x, ffn_gain)


def kernel(x, attn_norm, ffn_norm, ev_w_in, ev_forget_bias, ev_q_norm_a, ev_k_norm_a, ev_q_norm_b,
           ev_k_norm_b, ev_w_out, od_w_in, od_q_norm, od_k_norm, od_sinks, od_w_out,
           peer_w_query, peer_sub_keys, peer_down, peer_up):
    batch, seq, d_model = x.shape
    depth = attn_norm.shape[0]
    tabs = _rope_tables(seq)
    xt = x.reshape(batch * seq, d_model)
    h = _rmsnorm(xt, attn_norm[0])
    for l in range(depth):
        i = l // 2
        if l % 2 == 0:
            xt, h2 = _even_mixer(xt, h, ffn_norm[l], ev_w_in[i], ev_forget_bias[i], ev_q_norm_a[i],
                                 ev_k_norm_a[i], ev_q_norm_b[i], ev_k_norm_b[i], ev_w_out[i], tabs,
                                 batch=batch, seq=seq)
        else:
            xt, h2 = _odd_mixer(xt, h, ffn_norm[l], od_w_in[i], od_q_norm[i], od_k_norm[i], od_sinks[i],
                                od_w_out[i], tabs, batch=batch, seq=seq)
        next_gain = attn_norm[l + 1] if l + 1 < depth else None
        xt, h = _peer_layer(xt, h2, next_gain, peer_w_query[l], peer_sub_keys[l], peer_down[l], peer_up[l])
    return xt.reshape(batch, seq, d_model)
```

```python
import functools

import numpy as np
import jax
import jax.numpy as jnp
from jax import lax
from jax.experimental import pallas as pl
from jax.experimental.pallas import tpu as pltpu

F32 = jnp.float32
BF16 = jnp.bfloat16

HEAD_DIM = 64
ROT_DIM = HEAD_DIM // 4
ROPE_THETA = 500000.0
ATTN_SCALE = HEAD_DIM ** -0.5
EPS = 1e-6
NEG_INF = -1e30

MOBA_HEADS = 8
FOX_HEADS = 8
MOBA_BLOCK = 256
MOBA_TOPK = 3
A_W = MOBA_HEADS * HEAD_DIM
B_W = FOX_HEADS * HEAD_DIM

SWA_Q_HEADS = 16
SWA_KV_HEADS = 2
SWA_WINDOW = 128

PEER_HEADS = 8
PEER_N_KEYS = 128
PEER_TOPK = 16
PEER_QUERY_DIM = 128

LANES = 128
BF16_ROWS = 16
PAIR_W = 2 * HEAD_DIM

ROW_TILE = 512
FLASH_TILE = 2 * MOBA_BLOCK
ROUTE_TILE = 2 * LANES
EXPERT_TOKEN_TILE = 512
VMEM_LIMIT = 48 * 1024 * 1024
VMEM_SPILL_MARGIN = 4 * 1024 * 1024


def _cparams(sem, vmem_limit=VMEM_LIMIT):
    return pltpu.CompilerParams(dimension_semantics=sem, vmem_limit_bytes=vmem_limit)


def _dot_nt(a, b):
    return lax.dot_general(a, b, (((1,), (1,)), ((), ())), preferred_element_type=F32)


def _dot(a, b):
    return jnp.dot(a, b, preferred_element_type=F32)


def _split2(x):
    h1 = x.astype(BF16)
    return h1, (x - h1.astype(F32)).astype(BF16)


def _split3(x):
    h1 = x.astype(BF16)
    r1 = x - h1.astype(F32)
    h2 = r1.astype(BF16)
    h3 = (r1 - h2.astype(F32)).astype(BF16)
    return h1, h2, h3


def _rmsnorm_kernel(x_ref, g_ref, o_ref):
    x = x_ref[...]
    ms = jnp.mean(x * x, axis=-1, keepdims=True)
    o_ref[...] = (x * lax.rsqrt(ms + EPS) * g_ref[...]).astype(o_ref.dtype)


def _rmsnorm(x, gain, tm=ROW_TILE):
    T, D = x.shape
    return pl.pallas_call(
        _rmsnorm_kernel,
        grid=(T // tm,),
        in_specs=[pl.BlockSpec((tm, D), lambda i: (i, 0)),
                  pl.BlockSpec((1, D), lambda i: (0, 0))],
        out_specs=pl.BlockSpec((tm, D), lambda i: (i, 0)),
        out_shape=jax.ShapeDtypeStruct((T, D), BF16),
        compiler_params=_cparams(("parallel",)),
        name="rmsnorm",
    )(x, gain.reshape(1, D))


def _proj_kernel(*refs, norm, rope, kmean, tn):
    it = iter(refs)
    h_ref, w_ref = next(it), next(it)
    gain_ref = next(it) if norm else None
    bd_ref = next(it) if norm else None
    if rope:
        c_ref, sa_ref, sb_ref = next(it), next(it), next(it)
    o_ref = next(it)
    km_ref = next(it) if kmean else None

    y = _dot(h_ref[...], w_ref[...])
    if norm:
        y2 = y * y
        bd = bd_ref[...]
        cols = []
        tm = y.shape[0]
        for c in range(tn // LANES):
            h1, h2 = _split2(y2[:, c * LANES:(c + 1) * LANES])
            t = _dot(jnp.concatenate([h1, h2], axis=0), bd)
            cols.append(t[:tm] + t[tm:])
        ms = cols[0] if len(cols) == 1 else jnp.concatenate(cols, axis=1)
        y = y * lax.rsqrt(ms + EPS) * gain_ref[...]
    if rope:
        rep = tn // LANES
        tile = (lambda t: t) if rep == 1 else (lambda t: jnp.concatenate([t] * rep, axis=1))
        y = (y * tile(c_ref[...])
             + pltpu.roll(y, tn - ROT_DIM // 2, 1) * tile(sa_ref[...])
             + pltpu.roll(y, ROT_DIM // 2, 1) * tile(sb_ref[...]))
    o_ref[...] = y.astype(o_ref.dtype)
    if kmean:
        for r in range(km_ref.shape[0]):
            km_ref[r] = jnp.mean(y[r * MOBA_BLOCK:(r + 1) * MOBA_BLOCK], axis=0, keepdims=True)


def _proj(h, w, *, seq, tn, tm=ROW_TILE, gain=None, rope_tabs=None, kmean=False):
    T, D = h.shape
    N = w.shape[1]
    norm = gain is not None
    rope = rope_tabs is not None
    nseq = seq // tm
    in_specs = [pl.BlockSpec((tm, D), lambda i, j: (i, 0)),
                pl.BlockSpec((D, tn), lambda i, j: (0, j))]
    args = [h, w]
    if norm:
        bd = np.kron(np.eye(LANES // HEAD_DIM), np.ones((HEAD_DIM, HEAD_DIM))) / HEAD_DIM
        in_specs += [pl.BlockSpec((1, tn), lambda i, j: (0, j)),
                     pl.BlockSpec((LANES, LANES), lambda i, j: (0, 0))]
        args += [gain.reshape(1, N).astype(F32), jnp.asarray(bd, BF16)]
    if rope:
        in_specs += [pl.BlockSpec((tm, LANES), lambda i, j: (i % nseq, 0))] * 3
        args += list(rope_tabs)
    out_specs = [pl.BlockSpec((tm, tn), lambda i, j: (i, j))]
    out_shape = [jax.ShapeDtypeStruct((T, N), BF16)]
    if kmean:
        out_specs.append(pl.BlockSpec((tm // MOBA_BLOCK, 1, tn), lambda i, j: (i, 0, j)))
        out_shape.append(jax.ShapeDtypeStruct((T // MOBA_BLOCK, 1, N), F32))
    res = pl.pallas_call(
        functools.partial(_proj_kernel, norm=norm, rope=rope, kmean=kmean, tn=tn),
        grid=(T // tm, N // tn),
        in_specs=in_specs, out_specs=out_specs, out_shape=out_shape,
        compiler_params=_cparams(("parallel", "parallel")),
        name="proj",
    )(*args)
    return res if kmean else res[0]


def _rope_tables(seq):
    half = ROT_DIM // 2
    inv_freq = jnp.power(ROPE_THETA, -jnp.arange(0, ROT_DIM, 2, dtype=F32) / ROT_DIM)
    ang = jnp.arange(seq, dtype=F32)[:, None] * inv_freq[None, :]
    cos, sin = jnp.cos(ang), jnp.sin(ang)
    one = jnp.ones((seq, HEAD_DIM - ROT_DIM), F32)
    zero = jnp.zeros((seq, HEAD_DIM - ROT_DIM), F32)
    z8 = jnp.zeros((seq, half), F32)
    c = jnp.concatenate([cos, cos, one], axis=1)
    sa = jnp.concatenate([-sin, z8, zero], axis=1)
    sb = jnp.concatenate([z8, sin, zero], axis=1)
    rep = LANES // HEAD_DIM
    return tuple(jnp.concatenate([t] * rep, axis=1) for t in (c, sa, sb))


def _gates_kernel(h_ref, wf_ref, b_ref, tri_ref, o_ref, carry_ref):
    @pl.when(pl.program_id(1) == 0)
    def _():
        carry_ref[...] = jnp.zeros_like(carry_ref)

    z = _dot_nt(wf_ref[...], h_ref[...]) + b_ref[...][:, :1]
    lf = jnp.minimum(z, 0.0) - jnp.log1p(jnp.exp(-jnp.abs(z)))
    tri = tri_ref[...]
    h1, h2, h3 = _split3(lf)
    cs = _dot(h1, tri) + _dot(h2, tri) + _dot(h3, tri) + carry_ref[...][:, :1]
    o_ref[0] = cs
    carry_ref[...] = jnp.broadcast_to(cs[:, -1:], carry_ref.shape)


def _fox_gates(h, wf_t, bias, *, batch, seq, tm=ROW_TILE):
    T, D = h.shape
    nh = wf_t.shape[0]
    nseq = seq // tm
    tri = jnp.asarray(np.triu(np.ones((tm, tm))), BF16)
    return pl.pallas_call(
        _gates_kernel,
        grid=(batch, nseq),
        in_specs=[pl.BlockSpec((tm, D), lambda b, s: (b * nseq + s, 0)),
                  pl.BlockSpec((nh, D), lambda b, s: (0, 0)),
                  pl.BlockSpec((nh, LANES), lambda b, s: (0, 0)),
                  pl.BlockSpec((tm, tm), lambda b, s: (0, 0))],
        out_specs=pl.BlockSpec((1, nh, tm), lambda b, s: (b, 0, s)),
        out_shape=jax.ShapeDtypeStruct((batch, nh, seq), F32),
        scratch_shapes=[pltpu.VMEM((nh, LANES), F32)],
        compiler_params=_cparams(("parallel", "arbitrary")),
        name="fox_gates",
    )(h, wf_t, jnp.broadcast_to(bias.astype(F32)[:, None], (nh, LANES)), tri)


def _lane_tile(x, width):
    rep = width // LANES
    return x if rep == 1 else jnp.concatenate([x] * rep, axis=1)


def _flash_init(m_ref, acc_ref):
    m_ref[...] = jnp.full(m_ref.shape, NEG_INF, F32)
    acc_ref[...] = jnp.zeros(acc_ref.shape, F32)


def _head_values(v):
    lane = lax.broadcasted_iota(jnp.int32, v.shape, 1)
    return [jnp.where((lane >= hh * HEAD_DIM) & (lane < (hh + 1) * HEAD_DIM), v, jnp.ones_like(v))
            for hh in range(2)]


def _flash_update(slot, s, v, m_ref, acc_ref):
    tk = s.shape[1]
    m_prev = m_ref[slot]
    m_new = jnp.maximum(m_prev, jnp.max(s, axis=1, keepdims=True))
    alpha = jnp.exp(m_prev - m_new)
    p = jnp.exp(s - _lane_tile(m_new, tk))
    acc_ref[slot] = alpha * acc_ref[slot] + _dot(p.astype(BF16), v)
    m_ref[slot] = m_new


def _flash_finish(lane, acc_ref):
    outs = []
    for hh in range(2):
        acc = acc_ref[hh]
        den = (1 - hh) * HEAD_DIM
        outs.append(acc / acc[:, den:den + 1])
    return jnp.where(lane < HEAD_DIM, outs[0], outs[1])


def _flash_tiles(qi, tk, logits, values, m_ref, acc_ref):
    def absorb(s, off):
        vh = values(off)
        for hh in range(2):
            _flash_update(hh, s[hh], vh[hh], m_ref, acc_ref)

    def pair(off_a, off_b, b_diagonal):
        sa = logits(off_a, False)
        sb = logits(off_b, b_diagonal)
        absorb(sa, off_a)
        absorb(sb, off_b)

    def body(jj, carry):
        off = pl.multiple_of(2 * jj * tk, 2 * tk)
        pair(off, off + tk, False)
        return carry

    lax.fori_loop(0, qi // 2, body, 0)
    diag = pl.multiple_of(qi * tk, tk)

    @pl.when(qi % 2 == 1)
    def _():
        pair(diag - tk, diag, True)

    @pl.when(qi % 2 == 0)
    def _():
        absorb(logits(diag, True), diag)


def _head_queries(q, lane):
    qs = q * ATTN_SCALE
    return [jnp.where((lane >= hh * HEAD_DIM) & (lane < (hh + 1) * HEAD_DIM), qs, jnp.zeros_like(qs))
            for hh in range(2)]


def _moba_kernel(q_ref, k_ref, v_ref, km_ref, o_ref, m_ref, acc_ref):
    tq = q_ref.shape[1]
    tk = tq
    qi = pl.program_id(2)
    lane = lax.broadcasted_iota(jnp.int32, (tq, LANES), 1)
    lane_f = lane.astype(F32)
    rowv = lax.broadcasted_iota(jnp.int32, (tq, LANES), 0)
    row_blk = 2 * qi + (rowv >= MOBA_BLOCK).astype(jnp.int32)
    row = lax.broadcasted_iota(jnp.int32, (tq, tk), 0)
    col = lax.broadcasted_iota(jnp.int32, (tq, tk), 1)
    qh = _head_queries(q_ref[0], lane)
    _flash_init(m_ref, acc_ref)

    sels = []
    for hh in range(2):
        gate = _dot_nt(qh[hh], km_ref[0])
        gate = jnp.where(lane < row_blk, gate, -jnp.inf)
        sel = jnp.zeros((tq, LANES), F32)
        for _ in range(MOBA_TOPK):
            m = jnp.max(gate, axis=1, keepdims=True)
            idx = jnp.min(jnp.where(gate == m, lane_f, float(LANES)), axis=1, keepdims=True)
            hit = lane_f == idx
            sel = jnp.where(hit & (m > -jnp.inf), 1.0, sel)
            gate = jnp.where(hit, -jnp.inf, gate)
        sels.append(sel)

    def chosen(sel, blk):
        return jnp.max(jnp.where(lane == blk, sel, 0.0), axis=1, keepdims=True) > 0.0

    def logits(off, diagonal):
        kj = k_ref[0, pl.ds(off, tk), :]
        out = []
        for hh in range(2):
            s = _dot_nt(qh[hh], kj)
            if diagonal:
                visible = (col >= MOBA_BLOCK) | (row < MOBA_BLOCK) | chosen(sels[hh], 2 * qi)
                s = jnp.where((col <= row) & visible, s, NEG_INF)
            else:
                blk = 2 * (off // tk)
                s = jnp.concatenate(
                    [jnp.where(chosen(sels[hh], blk), s[:, :MOBA_BLOCK], NEG_INF),
                     jnp.where(chosen(sels[hh], blk + 1), s[:, MOBA_BLOCK:], NEG_INF)], axis=1)
            out.append(s)
        return out

    def values(off):
        return _head_values(v_ref[0, pl.ds(off, tk), :])

    _flash_tiles(qi, tk, logits, values, m_ref, acc_ref)
    o_ref[0] = _flash_finish(lane, acc_ref).astype(o_ref.dtype)


def _flash_scratch(tq):
    return [pltpu.VMEM((2, tq, LANES), F32)] * 2


def _moba(q, k, v, kmean, *, batch, seq):
    W = q.shape[1]
    tq = FLASH_TILE
    q3, k3, v3 = (t.reshape(batch, seq, W) for t in (q, k, v))
    out = pl.pallas_call(
        _moba_kernel,
        grid=(batch, W // PAIR_W, seq // tq),
        in_specs=[pl.BlockSpec((1, tq, PAIR_W), lambda b, p, i: (b, i, p)),
                  pl.BlockSpec((1, seq, PAIR_W), lambda b, p, i: (b, 0, p)),
                  pl.BlockSpec((1, seq, PAIR_W), lambda b, p, i: (b, 0, p)),
                  pl.BlockSpec((1, LANES, PAIR_W), lambda b, p, i: (b, 0, p))],
        out_specs=pl.BlockSpec((1, tq, PAIR_W), lambda b, p, i: (b, i, p)),
        out_shape=jax.ShapeDtypeStruct((batch, seq, W), BF16),
        scratch_shapes=_flash_scratch(tq),
        compiler_params=_cparams(("parallel", "parallel", "parallel")),
        name="moba",
    )(q3, k3, v3, kmean)
    return out.reshape(batch * seq, W)


_EXP_UNDERFLOW = 112.0


def _fox_kernel(q_ref, k_ref, v_ref, g_ref, frow_ref, far_ref, o_ref, m_ref, acc_ref):
    tq = q_ref.shape[1]
    tk = tq
    pr = pl.program_id(1)
    qi = pl.program_id(2)
    lane = lax.broadcasted_iota(jnp.int32, (tq, LANES), 1)
    row = lax.broadcasted_iota(jnp.int32, (tq, tk), 0)
    col = lax.broadcasted_iota(jnp.int32, (tq, tk), 1)
    qh = _head_queries(q_ref[0], lane)
    _flash_init(m_ref, acc_ref)

    def key_gates(off, width):
        f_all = frow_ref[0, :, pl.ds(off, width)]
        sub = lax.broadcasted_iota(jnp.int32, f_all.shape, 0)
        return [jnp.sum(jnp.where(sub == 2 * pr + hh, f_all, 0.0), axis=0, keepdims=True) for hh in range(2)]

    f_ref = [f[:, :1] for f in key_gates(pl.multiple_of(qi * tq, tq), LANES)]

    def logits(off, diagonal):
        kj = k_ref[0, pl.ds(off, tk), :]
        fk = key_gates(off, tk)
        out = []
        for hh in range(2):
            s = _dot_nt(qh[hh], kj) - (fk[hh] - f_ref[hh])
            out.append(jnp.where(col <= row, s, NEG_INF) if diagonal else s)
        return out

    def absorb(s, off):
        vh = _head_values(v_ref[0, pl.ds(off, tk), :])
        for hh in range(2):
            _flash_update(hh, s[hh], vh[hh], m_ref, acc_ref)

    def vanishes(j):
        tail = frow_ref[0, :, pl.ds(pl.multiple_of((j + 1) * tk - LANES, LANES), LANES)]
        sub = lax.broadcasted_iota(jnp.int32, tail.shape, 0)
        last = lax.broadcasted_iota(jnp.int32, tail.shape, 1) == LANES - 1
        gap = tail - jnp.where(sub == 2 * pr, f_ref[0], f_ref[1]) - far_ref[...][:, :1]
        mine = last & ((sub == 2 * pr) | (sub == 2 * pr + 1))
        return jnp.min(jnp.where(mine, gap, jnp.inf)) >= 0.0

    absorb(logits(pl.multiple_of(qi * tk, tk), True), pl.multiple_of(qi * tk, tk))

    def more(j):
        return jnp.logical_and(j >= 1, jnp.logical_not(vanishes(jnp.maximum(j, 0))))

    def pair(j):
        off_a = pl.multiple_of(j * tk, tk)
        off_b = pl.multiple_of((j - 1) * tk, tk)
        sa = logits(off_a, False)
        sb = logits(off_b, False)
        absorb(sa, off_a)
        absorb(sb, off_b)
        return j - 2

    j = lax.while_loop(more, pair, qi - 1)

    @pl.when(jnp.logical_and(j == 0, jnp.logical_not(vanishes(0))))
    def _():
        absorb(logits(0, False), 0)

    o = _flash_finish(lane, acc_ref)
    o_ref[0] = (o * jax.nn.sigmoid(g_ref[0].astype(F32))).astype(o_ref.dtype)


def _fox(q, k, v, g, frow, logit_bound, *, batch, seq, tq=FLASH_TILE):
    far = jnp.full((1, LANES), 2.0 * logit_bound + _EXP_UNDERFLOW, F32)
    W = q.shape[1]
    nh = frow.shape[1]
    q3, k3, v3, g3 = (t.reshape(batch, seq, W) for t in (q, k, v, g))
    out = pl.pallas_call(
        _fox_kernel,
        grid=(batch, W // PAIR_W, seq // tq),
        in_specs=[pl.BlockSpec((1, tq, PAIR_W), lambda b, p, i: (b, i, p)),
                  pl.BlockSpec((1, seq, PAIR_W), lambda b, p, i: (b, 0, p)),
                  pl.BlockSpec((1, seq, PAIR_W), lambda b, p, i: (b, 0, p)),
                  pl.BlockSpec((1, tq, PAIR_W), lambda b, p, i: (b, i, p)),
                  pl.BlockSpec((1, nh, seq), lambda b, p, i: (b, 0, 0)),
                  pl.BlockSpec((1, LANES), lambda b, p, i: (0, 0))],
        out_specs=pl.BlockSpec((1, tq, PAIR_W), lambda b, p, i: (b, i, p)),
        out_shape=jax.ShapeDtypeStruct((batch, seq, W), BF16),
        scratch_shapes=_flash_scratch(tq),
        compiler_params=_cparams(("parallel", "parallel", "parallel")),
        name="fox",
    )(q3, k3, v3, g3, frow, far)
    return out.reshape(batch * seq, W)


def _swa_kernel(q_ref, k_ref, v_ref, sink_ref, bias_ref, o_ref):
    tq = q_ref.shape[1]
    qi = pl.program_id(1)
    group = SWA_Q_HEADS // SWA_KV_HEADS
    tk = tq + SWA_WINDOW
    lane = lax.broadcasted_iota(jnp.int32, (tq, LANES), 1)
    kstart = pl.multiple_of(jnp.maximum(qi * tq - SWA_WINDOW, 0), SWA_WINDOW)
    k = k_ref[0, pl.ds(kstart, tk), :]
    vh = _head_values(v_ref[0, pl.ds(kstart, tk), :])
    swap = lambda t: pltpu.roll(t.astype(F32), HEAD_DIM, 1).astype(BF16)
    k_by_half = [k, swap(k)]
    bias = bias_ref[jnp.minimum(qi, 1)]
    bias = jnp.concatenate([bias] * (group // 2), axis=0)
    sink_tab = sink_ref[...]

    chains = []
    for c in range(SWA_KV_HEADS):
        v_by_half = [vh[c], swap(vh[c])]
        for swapped in range(2):
            hh = c if not swapped else 1 - c
            heads = [h for h in range(c * group, (c + 1) * group) if h % 2 == hh]
            pieces, sinks = [], []
            for head in heads:
                blk = q_ref[0, :, (head // 2) * PAIR_W:(head // 2 + 1) * PAIR_W] * ATTN_SCALE
                pieces.append(jnp.where((lane >= hh * HEAD_DIM) & (lane < (hh + 1) * HEAD_DIM),
                                        blk, jnp.zeros_like(blk)))
                sinks.append(jnp.broadcast_to(sink_tab[head:head + 1, :], (tq, LANES)))
            s = _dot_nt(jnp.concatenate(pieces, axis=0), k_by_half[swapped]) + bias
            chains.append((heads, s, jnp.concatenate(sinks, axis=0), v_by_half[swapped]))

    outs = [None] * SWA_Q_HEADS
    for heads, s, sink, v in chains:
        m = jnp.maximum(sink, jnp.max(s, axis=1, keepdims=True))
        p = jnp.exp(s - _lane_tile(m, tk))
        acc = _dot(p.astype(BF16), v)
        den = pltpu.roll(acc, HEAD_DIM, 1) + jnp.exp(sink - m)
        o = acc / den
        for idx, head in enumerate(heads):
            outs[head] = o[idx * tq:(idx + 1) * tq]
    for pp in range(SWA_Q_HEADS // 2):
        o_ref[0, :, pp * PAIR_W:(pp + 1) * PAIR_W] = jnp.where(
            lane < HEAD_DIM, outs[2 * pp], outs[2 * pp + 1]).astype(o_ref.dtype)


def _swa_bias(tq):
    r = np.arange(tq)[:, None]
    c = np.arange(tq + SWA_WINDOW)[None, :]
    tabs = []
    for key_offset in (0, SWA_WINDOW):
        dist = r + key_offset - c
        tabs.append(np.where((dist >= 0) & (dist < SWA_WINDOW), 0.0, NEG_INF))
    return jnp.asarray(np.stack(tabs), F32)


def _swa(q, k, v, sinks, *, batch, seq, tq=SWA_WINDOW):
    W = q.shape[1]
    tk = tq + SWA_WINDOW
    q3 = q.reshape(batch, seq, W)
    k3, v3 = (t.reshape(batch, seq, PAIR_W) for t in (k, v))
    sink_tab = jnp.broadcast_to(sinks.astype(F32)[:, None], (SWA_Q_HEADS, LANES))
    out = pl.pallas_call(
        _swa_kernel,
        grid=(batch, seq // tq),
        in_specs=[pl.BlockSpec((1, tq, W), lambda b, i: (b, i, 0)),
                  pl.BlockSpec((1, seq, PAIR_W), lambda b, i: (b, 0, 0)),
                  pl.BlockSpec((1, seq, PAIR_W), lambda b, i: (b, 0, 0)),
                  pl.BlockSpec((SWA_Q_HEADS, LANES), lambda b, i: (0, 0)),
                  pl.BlockSpec((2, tq, tk), lambda b, i: (0, 0, 0))],
        out_specs=pl.BlockSpec((1, tq, W), lambda b, i: (b, i, 0)),
        out_shape=jax.ShapeDtypeStruct((batch, seq, W), BF16),
        compiler_params=_cparams(("parallel", "parallel")),
        name="swa",
    )(q3, k3, v3, sink_tab, _swa_bias(tq))
    return out.reshape(batch * seq, W)


def _rms_normed(x, gain):
    ms = jnp.mean(x * x, axis=-1, keepdims=True)
    return (x * lax.rsqrt(ms + EPS) * gain).astype(BF16)


def _outproj_kernel(*refs, n_parts):
    parts = refs[:n_parts]
    w_ref, x_ref, g_ref, o_ref, h_ref = refs[n_parts:]
    y = x_ref[...]
    off = 0
    for p_ref in parts:
        kw = p_ref.shape[1]
        y = y + _dot(p_ref[...], w_ref[off:off + kw, :])
        off += kw
    o_ref[...] = y
    h_ref[...] = _rms_normed(y, g_ref[...])


def _outproj(parts, w, x, next_gain, tm=ROW_TILE):
    T, D = x.shape
    in_specs = [pl.BlockSpec((tm, p.shape[1]), lambda i: (i, 0)) for p in parts]
    in_specs += [pl.BlockSpec(w.shape, lambda i: (0, 0)),
                 pl.BlockSpec((tm, D), lambda i: (i, 0)),
                 pl.BlockSpec((1, D), lambda i: (0, 0))]
    row_spec = pl.BlockSpec((tm, D), lambda i: (i, 0))
    return pl.pallas_call(
        functools.partial(_outproj_kernel, n_parts=len(parts)),
        grid=(T // tm,),
        in_specs=in_specs,
        out_specs=[row_spec, row_spec],
        out_shape=[jax.ShapeDtypeStruct((T, D), F32), jax.ShapeDtypeStruct((T, D), BF16)],
        compiler_params=_cparams(("parallel",)),
        name="outproj",
    )(*parts, w, x, next_gain.reshape(1, D).astype(F32))


_CAND_ROWS = 80


def _cand_tables(lanes):
    pos = np.zeros((_CAND_ROWS,), np.float32)
    neg = np.zeros((_CAND_ROWS,), np.float32)
    r = 0
    for a, nb in ((0, 16), (1, 8), (2, 8), (3, 8), (4, 8), (5, 8), (6, 8), (7, 8)):
        for b in range(nb):
            pos[r] = a * PEER_TOPK + b
            neg[r] = 0.0 if (a + 1) * (b + 1) <= PEER_TOPK else -np.inf
            r += 1
    for a in range(8, 16):
        pos[r] = a * PEER_TOPK
        r += 1
    assert r == _CAND_ROWS
    tab = lambda t: jnp.asarray(np.broadcast_to(t[:, None], (_CAND_ROWS, lanes)).copy())
    return tab(pos), tab(neg)


def _batcher_pairs(n):
    pairs, p = [], 1
    while p < n:
        k = p
        while k >= 1:
            for j in range(k % p, n - k, 2 * k):
                for i in range(min(k, n - j - k)):
                    if (i + j) // (2 * p) == (i + j + k) // (2 * p):
                        pairs.append((i + j, i + j + k))
            k //= 2
        p *= 2
    return pairs


_SORT16 = _batcher_pairs(PEER_TOPK)
_SUBLANES = 8
_N_CAND_PIECES = _CAND_ROWS // _SUBLANES
_SORT10 = [(i, j) for i, j in _SORT16 if j < _N_CAND_PIECES]


def _compare_exchange(items, i, j):
    items[i], items[j] = jnp.maximum(items[i], items[j]), jnp.minimum(items[i], items[j])


def _top_sorted(pieces, pairs):
    items = list(pieces)
    for i, j in pairs:
        _compare_exchange(items, i, j)
    n = PEER_TOPK
    items += [jnp.full(items[0].shape, -jnp.inf, F32)] * (n - len(items))
    for shift in (4, 2, 1):
        items = [jnp.maximum(items[i], pltpu.roll(items[n - 1 - i], shift, 0)) for i in range(n)]
        d = n // 2
        while d >= 1:
            for i in range(n):
                if i & d == 0:
                    _compare_exchange(items, i, i + d)
            d //= 2
    return items


def _pieces(x):
    return [x[_SUBLANES * g:_SUBLANES * (g + 1)] for g in range(x.shape[0] // _SUBLANES)]


def _sublane_total(x):
    for shift in (4, 2, 1):
        x = x + pltpu.roll(x, shift, 0)
    return x


def _count_ge(pieces, thr):
    total = jnp.zeros(thr.shape, F32)
    for p in pieces:
        total = total + jnp.where(p >= thr, 1.0, 0.0)
    return _sublane_total(total)


def _route_head_fast(s1, s2):
    p1, p2 = _pieces(s1), _pieces(s2)
    v1 = _top_sorted(p1, _SORT16)
    v2 = _top_sorted(p2, _SORT16)
    sub = lax.broadcasted_iota(jnp.int32, v1[0].shape, 0)

    def spread(vals):
        out = vals[0]
        for r in range(1, _SUBLANES):
            out = jnp.where(sub == r, vals[r], out)
        return out

    v2_lo, v2_hi, v1_hi = spread(v2[:8]), spread(v2[8:]), spread(v1[8:])
    cands = [v1[0] + v2_lo, v1[0] + v2_hi, v1[1] + v2_lo]
    for a in range(2, 8):
        cands.append(jnp.where(sub < PEER_TOPK // (a + 1), v1[a] + v2_lo, -jnp.inf))
    cands.append(v1_hi + v2[0])
    ts = _top_sorted(cands, _SORT10)
    tau = ts[PEER_TOPK - 1]
    z = jnp.exp(ts[0] - ts[0])
    for kk in range(1, PEER_TOPK):
        z = z + jnp.exp(ts[kk] - ts[0])

    tied = (_count_ge(p1, v1[-1]) != float(PEER_TOPK)) | (_count_ge(p2, v2[-1]) != float(PEER_TOPK))
    tied = tied | (_count_ge(cands, tau) != float(PEER_TOPK))
    for b in range(PEER_TOPK - 1):
        tied = tied | (v1[b] == v1[b + 1]) | (v2[b] == v2[b + 1])

    cnt = []
    for a in range(PEER_TOPK):
        c = jnp.zeros(tau.shape, F32)
        for b in range(PEER_TOPK // (a + 1)):
            c = c + jnp.where(v1[a] + v2[b] >= tau, 1.0, 0.0)
        cnt.append(c)
    c1, r2 = [], []
    for x in p1:
        c = jnp.zeros(x.shape, F32)
        for a in range(PEER_TOPK):
            c = jnp.where(x == v1[a], cnt[a], c)
        c1.append(c)
    for x in p2:
        r = jnp.zeros(x.shape, F32)
        for b in range(PEER_TOPK):
            r = r + jnp.where(v2[b] > x, 1.0, 0.0)
        r2.append(r)
    inv_z = 1.0 / z
    e1 = [jnp.exp(x - v1[0]) * inv_z for x in p1]
    e2 = [jnp.exp(x - v2[0]) for x in p2]
    cat = lambda ps: jnp.concatenate(ps, axis=0)
    return (cat(c1), cat(e1), cat(r2), cat(e2)), tied


def _extract_sorted(scores, by_key):
    nk, lanes = scores[0].shape
    kio = lax.broadcasted_iota(jnp.int32, (nk, lanes), 0).astype(F32)
    slot = lax.broadcasted_iota(jnp.int32, (PEER_TOPK, lanes), 0)

    def body(a, carry):
        here = slot == a
        out = []
        for (v, vals, aux), ranked in zip(carry, by_key):
            m = jnp.max(v, axis=0, keepdims=True)
            idx = jnp.min(jnp.where(v == m, kio, float(nk)), axis=0, keepdims=True)
            hit = kio == idx
            aux = jnp.where(hit, jnp.asarray(a, F32), aux) if ranked else jnp.where(here, idx, aux)
            out.append((jnp.where(hit, -jnp.inf, v), jnp.where(here, m, vals), aux))
        return tuple(out)

    small = jnp.zeros((PEER_TOPK, lanes), F32)
    unranked = jnp.full((nk, lanes), float(PEER_TOPK), F32)
    init = tuple((v, small, unranked if ranked else small) for v, ranked in zip(scores, by_key))
    return [(vals, aux) for _, vals, aux in lax.fori_loop(0, PEER_TOPK, body, init)]


def _route_head_exact(s1, s2, pos, neg):
    lanes = s1.shape[1]
    slot = lax.broadcasted_iota(jnp.int32, (PEER_TOPK, lanes), 0)
    kio = lax.broadcasted_iota(jnp.int32, (PEER_N_KEYS, lanes), 0).astype(F32)
    (v1, idx1), (v2, rank2) = _extract_sorted([s1, s2], [False, True])
    blocks = [v1[0:1] + v2[0:8], v1[0:1] + v2[8:16]]
    blocks += [v1[a:a + 1] + v2[0:8] for a in range(1, 8)]
    blocks += [v1[8:16] + v2[0:1]]
    cand = jnp.concatenate(blocks, axis=0) + neg

    def pick(kk, carry):
        cand, chosen, ts = carry
        m = jnp.max(cand, axis=0, keepdims=True)
        first = jnp.min(jnp.where(cand == m, pos, 1e9), axis=0, keepdims=True)
        hit = pos == first
        return (jnp.where(hit, -jnp.inf, cand), jnp.where(hit, 1.0, chosen), jnp.where(slot == kk, m, ts))

    _, chosen, ts = lax.fori_loop(0, PEER_TOPK, pick,
                                  (cand, jnp.zeros_like(cand), jnp.zeros((PEER_TOPK, lanes), F32)))
    z = jnp.sum(jnp.exp(ts - ts[0:1]), axis=0, keepdims=True)
    counts = [jnp.sum(chosen[0:16], axis=0, keepdims=True)]
    counts += [jnp.sum(chosen[8 * a + 8:8 * a + 16], axis=0, keepdims=True) for a in range(1, 8)]
    counts += [chosen[72 + a:73 + a] for a in range(8)]
    c1 = jnp.zeros((PEER_N_KEYS, lanes), F32)
    for a in range(PEER_TOPK):
        c1 = jnp.where(kio == idx1[a:a + 1], counts[a], c1)
    return c1, jnp.exp(s1 - v1[0:1]) / z, rank2, jnp.exp(s2 - v2[0:1])


def _route_kernel(h_ref, wq_ref, keys_ref, pos_ref, neg_ref,
                  c1_ref, e1_ref, r2_ref, e2_ref, qt_ref, sc_ref):
    half = PEER_QUERY_DIM // 2
    qt_ref[...] = _dot_nt(wq_ref[...], h_ref[...]).astype(BF16)

    def store(h, maps):
        c1, e1, r2, e2 = maps
        c1_ref[h] = c1
        e1_ref[h] = e1
        r2_ref[h] = r2.astype(BF16)
        e2_ref[h] = e2.astype(BF16)

    def head_body(h, _):
        r0 = pl.multiple_of(h * PEER_QUERY_DIM, PEER_QUERY_DIM)
        sc_ref[0] = _dot(keys_ref[2 * h], qt_ref[pl.ds(r0, half), :])
        sc_ref[1] = _dot(keys_ref[2 * h + 1], qt_ref[pl.ds(r0 + half, half), :])
        maps, tied = _route_head_fast(sc_ref[0], sc_ref[1])
        any_tied = jnp.max(jnp.where(tied, 1.0, 0.0)) > 0.0

        @pl.when(any_tied)
        def _():
            store(h, _route_head_exact(sc_ref[0], sc_ref[1], pos_ref[...], neg_ref[...]))

        @pl.when(jnp.logical_not(any_tied))
        def _():
            store(h, maps)

        return 0

    lax.fori_loop(0, PEER_HEADS, head_body, 0)


def _peer_route(h2, wq_t, keys, tt=ROUTE_TILE):
    T, D = h2.shape
    pos, neg = _cand_tables(tt)
    stat_spec = pl.BlockSpec((PEER_HEADS, PEER_N_KEYS, tt), lambda i: (0, 0, i))
    stat = lambda dt: jax.ShapeDtypeStruct((PEER_HEADS, PEER_N_KEYS, T), dt)
    return pl.pallas_call(
        _route_kernel,
        grid=(T // tt,),
        in_specs=[pl.BlockSpec((tt, D), lambda i: (i, 0)),
                  pl.BlockSpec(wq_t.shape, lambda i: (0, 0)),
                  pl.BlockSpec(keys.shape, lambda i: (0, 0, 0)),
                  pl.BlockSpec((_CAND_ROWS, tt), lambda i: (0, 0)),
                  pl.BlockSpec((_CAND_ROWS, tt), lambda i: (0, 0))],
        out_specs=[stat_spec] * 4,
        out_shape=[stat(F32), stat(F32), stat(BF16), stat(BF16)],
        scratch_shapes=[pltpu.VMEM((PEER_HEADS * PEER_QUERY_DIM, tt), BF16),
                        pltpu.VMEM((2, PEER_N_KEYS, tt), F32)],
        compiler_params=_cparams(("parallel",)),
        name="peer_route",
    )(h2, wq_t, keys, pos, neg)


_KEY_GROUP = 16
_UNITS = 4
_DRAIN_PIECES = 2


def _build_gated(a_ref, p_ref, c1_ref, e1_ref, r2_ref, e2_ref, key0, g0, ng, lt):
    rep = PEER_N_KEYS // BF16_ROWS

    def rows16(row):
        blk = jnp.broadcast_to(row, (BF16_ROWS, LANES)).astype(BF16)
        return jnp.concatenate([blk] * rep, axis=0)

    ls = slice(lt * LANES, (lt + 1) * LANES)
    w = [jnp.zeros((PEER_N_KEYS, LANES), BF16) for _ in range(ng)]
    for h in range(PEER_HEADS):
        c1 = c1_ref[h, pl.ds(key0, _KEY_GROUP), ls]
        e1 = e1_ref[h, pl.ds(key0, _KEY_GROUP), ls]
        r2 = r2_ref[h, :, ls]
        e2 = e2_ref[h, :, ls]
        for g in range(ng):
            thr = rows16(c1[g0 + g:g0 + g + 1])
            gate = rows16(e1[g0 + g:g0 + g + 1])
            w[g] = w[g] + jnp.where(r2 < thr, e2, jnp.zeros_like(e2)) * gate
    for g in range(g0, g0 + ng):
        rs = slice(g * PEER_N_KEYS, (g + 1) * PEER_N_KEYS)
        a = a_ref[rs, ls]
        gelu = 0.5 * a * (1.0 + lax.erf(a * (2.0 ** -0.5)))
        p_ref[rs, ls] = gelu.astype(BF16) * w[g - g0]


def _experts_kernel(h_ref, dn_ref, upt_ref, c1_ref, e1_ref, r2_ref, e2_ref, x_ref, *rest, emit_norm):
    if emit_norm:
        g_ref, o_ref, hn_ref, a_ref, p_ref, acc_ref = rest
    else:
        o_ref, a_ref, p_ref, acc_ref = rest
    te, tt = a_ref.shape
    d_model = acc_ref.shape[0]
    e = pl.program_id(1)
    n_tiles = pl.num_programs(1) - 1
    cur = e % 2
    ng = _KEY_GROUP // _UNITS
    n_lane = tt // LANES
    key0 = pl.multiple_of(jnp.minimum(e, n_tiles - 1) * _KEY_GROUP, _KEY_GROUP)

    def front_mm(u):
        rows = te // _UNITS
        rs = slice(u * rows, (u + 1) * rows)
        a_ref[rs, :] = _dot_nt(dn_ref[rs, :], h_ref[...])

    def back_mm(r):
        rows = d_model // _DRAIN_PIECES
        rs = slice(r * rows, (r + 1) * rows)
        acc_ref[rs, :] += _dot(upt_ref[rs, :], p_ref[1 - cur])

    units_per_drain = _UNITS // _DRAIN_PIECES

    def run(front, back):
        if front:
            front_mm(0)
        for u in range(_UNITS):
            for lt in range(n_lane):
                if front:
                    _build_gated(a_ref, p_ref.at[cur], c1_ref, e1_ref, r2_ref, e2_ref, key0, u * ng, ng, lt)
                if front and lt == 0 and u + 1 < _UNITS:
                    front_mm(u + 1)
                if back and lt == n_lane // 2 and (u + 1) % units_per_drain == 0:
                    back_mm(u // units_per_drain)

    @pl.when(e == 0)
    def _():
        acc_ref[...] = jnp.zeros_like(acc_ref)
        run(True, False)

    @pl.when((e > 0) & (e < n_tiles))
    def _():
        run(True, True)

    @pl.when(e == n_tiles)
    def _():
        run(False, True)
        y = x_ref[...] + acc_ref[...].T
        o_ref[...] = y
        if emit_norm:
            hn_ref[...] = _rms_normed(y, g_ref[...])


def _peer_experts(h2, down, up_t, stats, x, next_gain, tt=EXPERT_TOKEN_TILE):
    T, D = h2.shape
    E = down.shape[0]
    te = _KEY_GROUP * PEER_N_KEYS
    n_tiles = E // te
    emit_norm = next_gain is not None
    stat_spec = pl.BlockSpec((PEER_HEADS, PEER_N_KEYS, tt), lambda i, e: (0, 0, i))
    row_spec = pl.BlockSpec((tt, D), lambda i, e: (i, 0))
    in_specs = [row_spec,
                pl.BlockSpec((te, D), lambda i, e: (jnp.minimum(e, n_tiles - 1), 0)),
                pl.BlockSpec((None, D, te), lambda i, e: (jnp.maximum(e - 1, 0), 0, 0)),
                stat_spec, stat_spec, stat_spec, stat_spec,
                row_spec]
    args = [h2, down, up_t, *stats, x]
    out_specs = [row_spec]
    out_shape = [jax.ShapeDtypeStruct((T, D), F32)]
    if emit_norm:
        in_specs.append(pl.BlockSpec((1, D), lambda i, e: (0, 0)))
        args.append(next_gain.reshape(1, D).astype(F32))
        out_specs.append(row_spec)
        out_shape.append(jax.ShapeDtypeStruct((T, D), BF16))
    stat_bytes = 2 * PEER_HEADS * PEER_N_KEYS * tt * (4 + 2)
    block_bytes = (tt * D * 2 + 2 * te * D * 2 + stat_bytes + tt * D * 4
                   + tt * D * 4 + (tt * D * 2 + D * 4 if emit_norm else 0))
    scratch_bytes = te * tt * 4 + 2 * te * tt * 2 + D * tt * 4
    res = pl.pallas_call(
        functools.partial(_experts_kernel, emit_norm=emit_norm),
        grid=(T // tt, n_tiles + 1),
        in_specs=in_specs, out_specs=out_specs, out_shape=out_shape,
        scratch_shapes=[pltpu.VMEM((te, tt), F32),
                        pltpu.VMEM((2, te, tt), BF16),
                        pltpu.VMEM((D, tt), F32)],
        compiler_params=_cparams(("parallel", "arbitrary"), 2 * block_bytes + scratch_bytes + VMEM_SPILL_MARGIN),
        name="peer_experts",
    )(*args)
    return (res[0], res[1]) if emit_norm else (res[0], None)


def _peer_layer(x, h2, next_gain, w_query, sub_keys, down, up):
    keys = sub_keys.reshape(PEER_HEADS * 2, PEER_N_KEYS, PEER_QUERY_DIM // 2).astype(BF16)
    stats = _peer_route(h2, w_query.T.astype(BF16), keys)
    te = _KEY_GROUP * PEER_N_KEYS
    up_t = up.reshape(up.shape[0] // te, te, up.shape[1]).transpose(0, 2, 1).astype(BF16)
    return _peer_experts(h2, down.astype(BF16), up_t, stats, x, next_gain)


def _tile_heads(g, n):
    return jnp.tile(g.astype(F32), n)


def _even_mixer(x, h, ffn_gain, w_in, f_bias, qn_a, kn_a, qn_b, kn_b, w_out, tabs, *, batch, seq):
    w = w_in.astype(BF16)
    o_qa, o_ka, o_va, o_qb, o_kb, o_vb, o_gb, o_fb = (
        0, A_W, 2 * A_W, 3 * A_W, 3 * A_W + B_W, 3 * A_W + 2 * B_W, 3 * A_W + 3 * B_W, 3 * A_W + 4 * B_W)
    gain_a = jnp.concatenate([_tile_heads(qn_a, MOBA_HEADS), _tile_heads(kn_a, MOBA_HEADS)])
    qk_a, km = _proj(h, w[:, o_qa:o_va], seq=seq, tn=A_W, gain=gain_a, rope_tabs=tabs, kmean=True)
    nb = seq // MOBA_BLOCK
    kmean = km.reshape(batch, nb, 2 * A_W)[:, :, A_W:]
    kmean = jnp.pad(kmean, ((0, 0), (0, LANES - nb), (0, 0))).astype(BF16)
    gain_b = jnp.concatenate([_tile_heads(qn_b, FOX_HEADS), _tile_heads(kn_b, FOX_HEADS)])
    qk_b = _proj(h, w[:, o_qb:o_vb], seq=seq, tn=B_W, gain=gain_b)
    w_plain = jnp.concatenate([w[:, o_va:o_qb], w[:, o_vb:o_fb]], axis=1)
    vvg = _proj(h, w_plain, seq=seq, tn=A_W)
    va, vb, gb = vvg[:, :A_W], vvg[:, A_W:A_W + B_W], vvg[:, A_W + B_W:]
    frow = _fox_gates(h, w[:, o_fb:].T, f_bias, batch=batch, seq=seq)
    oa = _moba(qk_a[:, :A_W], qk_a[:, A_W:], va, kmean, batch=batch, seq=seq)
    logit_bound = 1.01 * ATTN_SCALE * HEAD_DIM * jnp.max(jnp.abs(qn_b)) * jnp.max(jnp.abs(kn_b)) + 0.01
    ob = _fox(qk_b[:, :B_W], qk_b[:, B_W:], vb, gb, frow, logit_bound, batch=batch, seq=seq)
    return _outproj([oa, ob], w_out.astype(BF16), x, ffn_gain)


def _odd_mixer(x, h, ffn_gain, w_in, qn, kn, sinks, w_out, tabs, *, batch, seq):
    w = w_in.astype(BF16)
    qw = SWA_Q_HEADS * HEAD_DIM
    kw = SWA_KV_HEADS * HEAD_DIM
    q = _proj(h, w[:, :qw], seq=seq, tn=512, gain=_tile_heads(qn, SWA_Q_HEADS), rope_tabs=tabs)
    k = _proj(h, w[:, qw:qw + kw], seq=seq, tn=kw, gain=_tile_heads(kn, SWA_KV_HEADS), rope_tabs=tabs)
    v = _proj(h, w[:, qw + kw:], seq=seq, tn=kw)
    o = _swa(q, k, v, sinks, batch=batch, seq=seq)
    return _outproj([o], w_out.astype(BF16), x, ffn_gain)


def kernel(x, attn_norm, ffn_norm, ev_w_in, ev_forget_bias, ev_q_norm_a, ev_k_norm_a, ev_q_norm_b,
           ev_k_norm_b, ev_w_out, od_w_in, od_q_norm, od_k_norm, od_sinks, od_w_out,
           peer_w_query, peer_sub_keys, peer_down, peer_up):
    batch, seq, d_model = x.shape
    depth = attn_norm.shape[0]
    tabs = _rope_tables(seq)
    xt = x.reshape(batch * seq, d_model)
    h = _rmsnorm(xt, attn_norm[0])
    for l in range(depth):
        i = l // 2
        if l % 2 == 0:
            xt, h2 = _even_mixer(xt, h, ffn_norm[l], ev_w_in[i], ev_forget_bias[i], ev_q_norm_a[i],
                                 ev_k_norm_a[i], ev_q_norm_b[i], ev_k_norm_b[i], ev_w_out[i], tabs,
                                 batch=batch, seq=seq)
        else:
            xt, h2 = _odd_mixer(xt, h, ffn_norm[l], od_w_in[i], od_q_norm[i], od_k_norm[i], od_sinks[i],
                                od_w_out[i], tabs, batch=batch, seq=seq)
        next_gain = attn_norm[l + 1] if l + 1 < depth else None
        xt, h = _peer_layer(xt, h2, next_gain, peer_w_query[l], peer_sub_keys[l], peer_down[l], peer_up[l])
    return xt.reshape(batch, seq, d_model)
```

```python
import functools

import numpy as np
import jax
import jax.numpy as jnp
from jax import lax
from jax.experimental import pallas as pl
from jax.experimental.pallas import tpu as pltpu

F32 = jnp.float32
BF16 = jnp.bfloat16

HEAD_DIM = 64
ROT_DIM = HEAD_DIM // 4
ROPE_THETA = 500000.0
ATTN_SCALE = HEAD_DIM ** -0.5
EPS = 1e-6
NEG_INF = -1e30

MOBA_HEADS = 8
FOX_HEADS = 8
MOBA_BLOCK = 256
MOBA_TOPK = 3
A_W = MOBA_HEADS * HEAD_DIM
B_W = FOX_HEADS * HEAD_DIM

SWA_Q_HEADS = 16
SWA_KV_HEADS = 2
SWA_WINDOW = 128

PEER_HEADS = 8
PEER_N_KEYS = 128
PEER_TOPK = 16
PEER_QUERY_DIM = 128

LANES = 128
BF16_ROWS = 16
PAIR_W = 2 * HEAD_DIM

ROW_TILE = 512
FLASH_TILE = 2 * MOBA_BLOCK
ROUTE_TILE = 2 * LANES
EXPERT_TOKEN_TILE = 512
VMEM_LIMIT = 48 * 1024 * 1024
VMEM_SPILL_MARGIN = 4 * 1024 * 1024


def _cparams(sem, vmem_limit=VMEM_LIMIT):
    return pltpu.CompilerParams(dimension_semantics=sem, vmem_limit_bytes=vmem_limit)


def _dot_nt(a, b):
    return lax.dot_general(a, b, (((1,), (1,)), ((), ())), preferred_element_type=F32)


def _dot(a, b):
    return jnp.dot(a, b, preferred_element_type=F32)


def _split2(x):
    h1 = x.astype(BF16)
    return h1, (x - h1.astype(F32)).astype(BF16)


def _split3(x):
    h1 = x.astype(BF16)
    r1 = x - h1.astype(F32)
    h2 = r1.astype(BF16)
    h3 = (r1 - h2.astype(F32)).astype(BF16)
    return h1, h2, h3


def _rmsnorm_kernel(x_ref, g_ref, o_ref):
    x = x_ref[...]
    ms = jnp.mean(x * x, axis=-1, keepdims=True)
    o_ref[...] = (x * lax.rsqrt(ms + EPS) * g_ref[...]).astype(o_ref.dtype)


def _rmsnorm(x, gain, tm=ROW_TILE):
    T, D = x.shape
    return pl.pallas_call(
        _rmsnorm_kernel,
        grid=(T // tm,),
        in_specs=[pl.BlockSpec((tm, D), lambda i: (i, 0)),
                  pl.BlockSpec((1, D), lambda i: (0, 0))],
        out_specs=pl.BlockSpec((tm, D), lambda i: (i, 0)),
        out_shape=jax.ShapeDtypeStruct((T, D), BF16),
        compiler_params=_cparams(("parallel",)),
        name="rmsnorm",
    )(x, gain.reshape(1, D))


def _proj_kernel(*refs, norm, rope, kmean, tn):
    it = iter(refs)
    h_ref, w_ref = next(it), next(it)
    gain_ref = next(it) if norm else None
    bd_ref = next(it) if norm else None
    if rope:
        c_ref, sa_ref, sb_ref = next(it), next(it), next(it)
    o_ref = next(it)
    km_ref = next(it) if kmean else None

    y = _dot(h_ref[...], w_ref[...])
    if norm:
        y2 = y * y
        bd = bd_ref[...]
        cols = []
        tm = y.shape[0]
        for c in range(tn // LANES):
            h1, h2 = _split2(y2[:, c * LANES:(c + 1) * LANES])
            t = _dot(jnp.concatenate([h1, h2], axis=0), bd)
            cols.append(t[:tm] + t[tm:])
        ms = cols[0] if len(cols) == 1 else jnp.concatenate(cols, axis=1)
        y = y * lax.rsqrt(ms + EPS) * gain_ref[...]
    if rope:
        rep = tn // LANES
        tile = (lambda t: t) if rep == 1 else (lambda t: jnp.concatenate([t] * rep, axis=1))
        y = (y * tile(c_ref[...])
             + pltpu.roll(y, tn - ROT_DIM // 2, 1) * tile(sa_ref[...])
             + pltpu.roll(y, ROT_DIM // 2, 1) * tile(sb_ref[...]))
    o_ref[...] = y.astype(o_ref.dtype)
    if kmean:
        for r in range(km_ref.shape[0]):
            km_ref[r] = jnp.mean(y[r * MOBA_BLOCK:(r + 1) * MOBA_BLOCK], axis=0, keepdims=True)


def _proj(h, w, *, seq, tn, tm=ROW_TILE, gain=None, rope_tabs=None, kmean=False):
    T, D = h.shape
    N = w.shape[1]
    norm = gain is not None
    rope = rope_tabs is not None
    nseq = seq // tm
    in_specs = [pl.BlockSpec((tm, D), lambda i, j: (i, 0)),
                pl.BlockSpec((D, tn), lambda i, j: (0, j))]
    args = [h, w]
    if norm:
        bd = np.kron(np.eye(LANES // HEAD_DIM), np.ones((HEAD_DIM, HEAD_DIM))) / HEAD_DIM
        in_specs += [pl.BlockSpec((1, tn), lambda i, j: (0, j)),
                     pl.BlockSpec((LANES, LANES), lambda i, j: (0, 0))]
        args += [gain.reshape(1, N).astype(F32), jnp.asarray(bd, BF16)]
    if rope:
        in_specs += [pl.BlockSpec((tm, LANES), lambda i, j: (i % nseq, 0))] * 3
        args += list(rope_tabs)
    out_specs = [pl.BlockSpec((tm, tn), lambda i, j: (i, j))]
    out_shape = [jax.ShapeDtypeStruct((T, N), BF16)]
    if kmean:
        out_specs.append(pl.BlockSpec((tm // MOBA_BLOCK, 1, tn), lambda i, j: (i, 0, j)))
        out_shape.append(jax.ShapeDtypeStruct((T // MOBA_BLOCK, 1, N), F32))
    res = pl.pallas_call(
        functools.partial(_proj_kernel, norm=norm, rope=rope, kmean=kmean, tn=tn),
        grid=(T // tm, N // tn),
        in_specs=in_specs, out_specs=out_specs, out_shape=out_shape,
        compiler_params=_cparams(("parallel", "parallel")),
        name="proj",
    )(*args)
    return res if kmean else res[0]


def _rope_tables(seq):
    half = ROT_DIM // 2
    inv_freq = jnp.power(ROPE_THETA, -jnp.arange(0, ROT_DIM, 2, dtype=F32) / ROT_DIM)
    ang = jnp.arange(seq, dtype=F32)[:, None] * inv_freq[None, :]
    cos, sin = jnp.cos(ang), jnp.sin(ang)
    one = jnp.ones((seq, HEAD_DIM - ROT_DIM), F32)
    zero = jnp.zeros((seq, HEAD_DIM - ROT_DIM), F32)
    z8 = jnp.zeros((seq, half), F32)
    c = jnp.concatenate([cos, cos, one], axis=1)
    sa = jnp.concatenate([-sin, z8, zero], axis=1)
    sb = jnp.concatenate([z8, sin, zero], axis=1)
    rep = LANES // HEAD_DIM
    return tuple(jnp.concatenate([t] * rep, axis=1) for t in (c, sa, sb))


def _gates_kernel(h_ref, wf_ref, b_ref, tri_ref, o_ref, carry_ref):
    @pl.when(pl.program_id(1) == 0)
    def _():
        carry_ref[...] = jnp.zeros_like(carry_ref)

    z = _dot_nt(wf_ref[...], h_ref[...]) + b_ref[...][:, :1]
    lf = jnp.minimum(z, 0.0) - jnp.log1p(jnp.exp(-jnp.abs(z)))
    tri = tri_ref[...]
    h1, h2, h3 = _split3(lf)
    cs = _dot(h1, tri) + _dot(h2, tri) + _dot(h3, tri) + carry_ref[...][:, :1]
    o_ref[0] = cs
    carry_ref[...] = jnp.broadcast_to(cs[:, -1:], carry_ref.shape)


def _fox_gates(h, wf_t, bias, *, batch, seq, tm=ROW_TILE):
    T, D = h.shape
    nh = wf_t.shape[0]
    nseq = seq // tm
    tri = jnp.asarray(np.triu(np.ones((tm, tm))), BF16)
    return pl.pallas_call(
        _gates_kernel,
        grid=(batch, nseq),
        in_specs=[pl.BlockSpec((tm, D), lambda b, s: (b * nseq + s, 0)),
                  pl.BlockSpec((nh, D), lambda b, s: (0, 0)),
                  pl.BlockSpec((nh, LANES), lambda b, s: (0, 0)),
                  pl.BlockSpec((tm, tm), lambda b, s: (0, 0))],
        out_specs=pl.BlockSpec((1, nh, tm), lambda b, s: (b, 0, s)),
        out_shape=jax.ShapeDtypeStruct((batch, nh, seq), F32),
        scratch_shapes=[pltpu.VMEM((nh, LANES), F32)],
        compiler_params=_cparams(("parallel", "arbitrary")),
        name="fox_gates",
    )(h, wf_t, jnp.broadcast_to(bias.astype(F32)[:, None], (nh, LANES)), tri)


def _lane_tile(x, width):
    rep = width // LANES
    return x if rep == 1 else jnp.concatenate([x] * rep, axis=1)


def _flash_init(m_ref, acc_ref):
    m_ref[...] = jnp.full(m_ref.shape, NEG_INF, F32)
    acc_ref[...] = jnp.zeros(acc_ref.shape, F32)


def _head_values(v):
    lane = lax.broadcasted_iota(jnp.int32, v.shape, 1)
    return [jnp.where((lane >= hh * HEAD_DIM) & (lane < (hh + 1) * HEAD_DIM), v, jnp.ones_like(v))
            for hh in range(2)]


def _flash_update(slot, s, v, m_ref, acc_ref):
    tk = s.shape[1]
    m_prev = m_ref[slot]
    m_new = jnp.maximum(m_prev, jnp.max(s, axis=1, keepdims=True))
    alpha = jnp.exp(m_prev - m_new)
    p = jnp.exp(s - _lane_tile(m_new, tk))
    acc_ref[slot] = alpha * acc_ref[slot] + _dot(p.astype(BF16), v)
    m_ref[slot] = m_new


def _flash_finish(lane, acc_ref):
    outs = []
    for hh in range(2):
        acc = acc_ref[hh]
        den = (1 - hh) * HEAD_DIM
        outs.append(acc / acc[:, den:den + 1])
    return jnp.where(lane < HEAD_DIM, outs[0], outs[1])


def _flash_tiles(qi, tk, logits, values, m_ref, acc_ref):
    def absorb(s, off):
        vh = values(off)
        for hh in range(2):
            _flash_update(hh, s[hh], vh[hh], m_ref, acc_ref)

    def pair(off_a, off_b, b_diagonal):
        sa = logits(off_a, False)
        sb = logits(off_b, b_diagonal)
        absorb(sa, off_a)
        absorb(sb, off_b)

    def body(jj, carry):
        off = pl.multiple_of(2 * jj * tk, 2 * tk)
        pair(off, off + tk, False)
        return carry

    lax.fori_loop(0, qi // 2, body, 0)
    diag = pl.multiple_of(qi * tk, tk)

    @pl.when(qi % 2 == 1)
    def _():
        pair(diag - tk, diag, True)

    @pl.when(qi % 2 == 0)
    def _():
        absorb(logits(diag, True), diag)


def _head_queries(q, lane):
    qs = q * ATTN_SCALE
    return [jnp.where((lane >= hh * HEAD_DIM) & (lane < (hh + 1) * HEAD_DIM), qs, jnp.zeros_like(qs))
            for hh in range(2)]


def _moba_kernel(q_ref, k_ref, v_ref, km_ref, o_ref, m_ref, acc_ref):
    tq = q_ref.shape[1]
    tk = tq
    qi = pl.program_id(2)
    lane = lax.broadcasted_iota(jnp.int32, (tq, LANES), 1)
    lane_f = lane.astype(F32)
    rowv = lax.broadcasted_iota(jnp.int32, (tq, LANES), 0)
    row_blk = 2 * qi + (rowv >= MOBA_BLOCK).astype(jnp.int32)
    row = lax.broadcasted_iota(jnp.int32, (tq, tk), 0)
    col = lax.broadcasted_iota(jnp.int32, (tq, tk), 1)
    qh = _head_queries(q_ref[0], lane)
    _flash_init(m_ref, acc_ref)

    sels = []
    for hh in range(2):
        gate = _dot_nt(qh[hh], km_ref[0])
        gate = jnp.where(lane < row_blk, gate, -jnp.inf)
        sel = jnp.zeros((tq, LANES), F32)
        for _ in range(MOBA_TOPK):
            m = jnp.max(gate, axis=1, keepdims=True)
            idx = jnp.min(jnp.where(gate == m, lane_f, float(LANES)), axis=1, keepdims=True)
            hit = lane_f == idx
            sel = jnp.where(hit & (m > -jnp.inf), 1.0, sel)
            gate = jnp.where(hit, -jnp.inf, gate)
        sels.append(sel)

    def chosen(sel, blk):
        return jnp.max(jnp.where(lane == blk, sel, 0.0), axis=1, keepdims=True) > 0.0

    def logits(off, diagonal):
        kj = k_ref[0, pl.ds(off, tk), :]
        out = []
        for hh in range(2):
            s = _dot_nt(qh[hh], kj)
            if diagonal:
                visible = (col >= MOBA_BLOCK) | (row < MOBA_BLOCK) | chosen(sels[hh], 2 * qi)
                s = jnp.where((col <= row) & visible, s, NEG_INF)
            else:
                blk = 2 * (off // tk)
                s = jnp.concatenate(
                    [jnp.where(chosen(sels[hh], blk), s[:, :MOBA_BLOCK], NEG_INF),
                     jnp.where(chosen(sels[hh], blk + 1), s[:, MOBA_BLOCK:], NEG_INF)], axis=1)
            out.append(s)
        return out

    def values(off):
        return _head_values(v_ref[0, pl.ds(off, tk), :])

    _flash_tiles(qi, tk, logits, values, m_ref, acc_ref)
    o_ref[0] = _flash_finish(lane, acc_ref).astype(o_ref.dtype)


def _flash_scratch(tq):
    return [pltpu.VMEM((2, tq, LANES), F32)] * 2


def _moba(q, k, v, kmean, *, batch, seq):
    W = q.shape[1]
    tq = FLASH_TILE
    q3, k3, v3 = (t.reshape(batch, seq, W) for t in (q, k, v))
    out = pl.pallas_call(
        _moba_kernel,
        grid=(batch, W // PAIR_W, seq // tq),
        in_specs=[pl.BlockSpec((1, tq, PAIR_W), lambda b, p, i: (b, i, p)),
                  pl.BlockSpec((1, seq, PAIR_W), lambda b, p, i: (b, 0, p)),
                  pl.BlockSpec((1, seq, PAIR_W), lambda b, p, i: (b, 0, p)),
                  pl.BlockSpec((1, LANES, PAIR_W), lambda b, p, i: (b, 0, p))],
        out_specs=pl.BlockSpec((1, tq, PAIR_W), lambda b, p, i: (b, i, p)),
        out_shape=jax.ShapeDtypeStruct((batch, seq, W), BF16),
        scratch_shapes=_flash_scratch(tq),
        compiler_params=_cparams(("parallel", "parallel", "parallel")),
        name="moba",
    )(q3, k3, v3, kmean)
    return out.reshape(batch * seq, W)


_EXP_UNDERFLOW = 112.0


def _fox_kernel(q_ref, k_ref, v_ref, g_ref, frow_ref, far_ref, o_ref, m_ref, acc_ref):
    tq = q_ref.shape[1]
    tk = tq
    pr = pl.program_id(1)
    qi = pl.program_id(2)
    lane = lax.broadcasted_iota(jnp.int32, (tq, LANES), 1)
    row = lax.broadcasted_iota(jnp.int32, (tq, tk), 0)
    col = lax.broadcasted_iota(jnp.int32, (tq, tk), 1)
    qh = _head_queries(q_ref[0], lane)
    _flash_init(m_ref, acc_ref)

    def key_gates(off, width):
        f_all = frow_ref[0, :, pl.ds(off, width)]
        sub = lax.broadcasted_iota(jnp.int32, f_all.shape, 0)
        return [jnp.sum(jnp.where(sub == 2 * pr + hh, f_all, 0.0), axis=0, keepdims=True) for hh in range(2)]

    f_ref = [f[:, :1] for f in key_gates(pl.multiple_of(qi * tq, tq), LANES)]

    def logits(off, diagonal):
        kj = k_ref[0, pl.ds(off, tk), :]
        fk = key_gates(off, tk)
        out = []
        for hh in range(2):
            s = _dot_nt(qh[hh], kj) - (fk[hh] - f_ref[hh])
            out.append(jnp.where(col <= row, s, NEG_INF) if diagonal else s)
        return out

    def absorb(s, off):
        vh = _head_values(v_ref[0, pl.ds(off, tk), :])
        for hh in range(2):
            _flash_update(hh, s[hh], vh[hh], m_ref, acc_ref)

    def vanishes(j):
        tail = frow_ref[0, :, pl.ds(pl.multiple_of((j + 1) * tk - LANES, LANES), LANES)]
        sub = lax.broadcasted_iota(jnp.int32, tail.shape, 0)
        last = lax.broadcasted_iota(jnp.int32, tail.shape, 1) == LANES - 1
        gap = tail - jnp.where(sub == 2 * pr, f_ref[0], f_ref[1]) - far_ref[...][:, :1]
        mine = last & ((sub == 2 * pr) | (sub == 2 * pr + 1))
        return jnp.min(jnp.where(mine, gap, jnp.inf)) >= 0.0

    absorb(logits(pl.multiple_of(qi * tk, tk), True), pl.multiple_of(qi * tk, tk))

    def more(j):
        return jnp.logical_and(j >= 1, jnp.logical_not(vanishes(jnp.maximum(j, 0))))

    def pair(j):
        off_a = pl.multiple_of(j * tk, tk)
        off_b = pl.multiple_of((j - 1) * tk, tk)
        sa = logits(off_a, False)
        sb = logits(off_b, False)
        absorb(sa, off_a)
        absorb(sb, off_b)
        return j - 2

    j = lax.while_loop(more, pair, qi - 1)

    @pl.when(jnp.logical_and(j == 0, jnp.logical_not(vanishes(0))))
    def _():
        absorb(logits(0, False), 0)

    o = _flash_finish(lane, acc_ref)
    o_ref[0] = (o * jax.nn.sigmoid(g_ref[0].astype(F32))).astype(o_ref.dtype)


def _fox(q, k, v, g, frow, logit_bound, *, batch, seq, tq=FLASH_TILE):
    far = jnp.full((1, LANES), 2.0 * logit_bound + _EXP_UNDERFLOW, F32)
    W = q.shape[1]
    nh = frow.shape[1]
    q3, k3, v3, g3 = (t.reshape(batch, seq, W) for t in (q, k, v, g))
    out = pl.pallas_call(
        _fox_kernel,
        grid=(batch, W // PAIR_W, seq // tq),
        in_specs=[pl.BlockSpec((1, tq, PAIR_W), lambda b, p, i: (b, i, p)),
                  pl.BlockSpec((1, seq, PAIR_W), lambda b, p, i: (b, 0, p)),
                  pl.BlockSpec((1, seq, PAIR_W), lambda b, p, i: (b, 0, p)),
                  pl.BlockSpec((1, tq, PAIR_W), lambda b, p, i: (b, i, p)),
                  pl.BlockSpec((1, nh, seq), lambda b, p, i: (b, 0, 0)),
                  pl.BlockSpec((1, LANES), lambda b, p, i: (0, 0))],
        out_specs=pl.BlockSpec((1, tq, PAIR_W), lambda b, p, i: (b, i, p)),
        out_shape=jax.ShapeDtypeStruct((batch, seq, W), BF16),
        scratch_shapes=_flash_scratch(tq),
        compiler_params=_cparams(("parallel", "parallel", "parallel")),
        name="fox",
    )(q3, k3, v3, g3, frow, far)
    return out.reshape(batch * seq, W)


def _swa_kernel(q_ref, k_ref, v_ref, sink_ref, bias_ref, o_ref):
    tq = q_ref.shape[1]
    qi = pl.program_id(1)
    group = SWA_Q_HEADS // SWA_KV_HEADS
    tk = tq + SWA_WINDOW
    lane = lax.broadcasted_iota(jnp.int32, (tq, LANES), 1)
    kstart = pl.multiple_of(jnp.maximum(qi * tq - SWA_WINDOW, 0), SWA_WINDOW)
    k = k_ref[0, pl.ds(kstart, tk), :]
    vh = _head_values(v_ref[0, pl.ds(kstart, tk), :])
    swap = lambda t: pltpu.roll(t.astype(F32), HEAD_DIM, 1).astype(BF16)
    k_by_half = [k, swap(k)]
    bias = bias_ref[jnp.minimum(qi, 1)]
    bias = jnp.concatenate([bias] * (group // 2), axis=0)
    sink_tab = sink_ref[...]

    chains = []
    for c in range(SWA_KV_HEADS):
        v_by_half = [vh[c], swap(vh[c])]
        for swapped in range(2):
            hh = c if not swapped else 1 - c
            heads = [h for h in range(c * group, (c + 1) * group) if h % 2 == hh]
            pieces, sinks = [], []
            for head in heads:
                blk = q_ref[0, :, (head // 2) * PAIR_W:(head // 2 + 1) * PAIR_W] * ATTN_SCALE
                pieces.append(jnp.where((lane >= hh * HEAD_DIM) & (lane < (hh + 1) * HEAD_DIM),
                                        blk, jnp.zeros_like(blk)))
                sinks.append(jnp.broadcast_to(sink_tab[head:head + 1, :], (tq, LANES)))
            s = _dot_nt(jnp.concatenate(pieces, axis=0), k_by_half[swapped]) + bias
            chains.append((heads, s, jnp.concatenate(sinks, axis=0), v_by_half[swapped]))

    outs = [None] * SWA_Q_HEADS
    for heads, s, sink, v in chains:
        m = jnp.maximum(sink, jnp.max(s, axis=1, keepdims=True))
        p = jnp.exp(s - _lane_tile(m, tk))
        acc = _dot(p.astype(BF16), v)
        den = pltpu.roll(acc, HEAD_DIM, 1) + jnp.exp(sink - m)
        o = acc / den
        for idx, head in enumerate(heads):
            outs[head] = o[idx * tq:(idx + 1) * tq]
    for pp in range(SWA_Q_HEADS // 2):
        o_ref[0, :, pp * PAIR_W:(pp + 1) * PAIR_W] = jnp.where(
            lane < HEAD_DIM, outs[2 * pp], outs[2 * pp + 1]).astype(o_ref.dtype)


def _swa_bias(tq):
    r = np.arange(tq)[:, None]
    c = np.arange(tq + SWA_WINDOW)[None, :]
    tabs = []
    for key_offset in (0, SWA_WINDOW):
        dist = r + key_offset - c
        tabs.append(np.where((dist >= 0) & (dist < SWA_WINDOW), 0.0, NEG_INF))
    return jnp.asarray(np.stack(tabs), F32)


def _swa(q, k, v, sinks, *, batch, seq, tq=SWA_WINDOW):
    W = q.shape[1]
    tk = tq + SWA_WINDOW
    q3 = q.reshape(batch, seq, W)
    k3, v3 = (t.reshape(batch, seq, PAIR_W) for t in (k, v))
    sink_tab = jnp.broadcast_to(sinks.astype(F32)[:, None], (SWA_Q_HEADS, LANES))
    out = pl.pallas_call(
        _swa_kernel,
        grid=(batch, seq // tq),
        in_specs=[pl.BlockSpec((1, tq, W), lambda b, i: (b, i, 0)),
                  pl.BlockSpec((1, seq, PAIR_W), lambda b, i: (b, 0, 0)),
                  pl.BlockSpec((1, seq, PAIR_W), lambda b, i: (b, 0, 0)),
                  pl.BlockSpec((SWA_Q_HEADS, LANES), lambda b, i: (0, 0)),
                  pl.BlockSpec((2, tq, tk), lambda b, i: (0, 0, 0))],
        out_specs=pl.BlockSpec((1, tq, W), lambda b, i: (b, i, 0)),
        out_shape=jax.ShapeDtypeStruct((batch, seq, W), BF16),
        compiler_params=_cparams(("parallel", "parallel")),
        name="swa",
    )(q3, k3, v3, sink_tab, _swa_bias(tq))
    return out.reshape(batch * seq, W)


def _rms_normed(x, gain):
    ms = jnp.mean(x * x, axis=-1, keepdims=True)
    return (x * lax.rsqrt(ms + EPS) * gain).astype(BF16)


def _outproj_kernel(*refs, n_parts):
    parts = refs[:n_parts]
    w_ref, x_ref, g_ref, o_ref, h_ref = refs[n_parts:]
    y = x_ref[...]
    off = 0
    for p_ref in parts:
        kw = p_ref.shape[1]
        y = y + _dot(p_ref[...], w_ref[off:off + kw, :])
        off += kw
    o_ref[...] = y
    h_ref[...] = _rms_normed(y, g_ref[...])


def _outproj(parts, w, x, next_gain, tm=ROW_TILE):
    T, D = x.shape
    in_specs = [pl.BlockSpec((tm, p.shape[1]), lambda i: (i, 0)) for p in parts]
    in_specs += [pl.BlockSpec(w.shape, lambda i: (0, 0)),
                 pl.BlockSpec((tm, D), lambda i: (i, 0)),
                 pl.BlockSpec((1, D), lambda i: (0, 0))]
    row_spec = pl.BlockSpec((tm, D), lambda i: (i, 0))
    return pl.pallas_call(
        functools.partial(_outproj_kernel, n_parts=len(parts)),
        grid=(T // tm,),
        in_specs=in_specs,
        out_specs=[row_spec, row_spec],
        out_shape=[jax.ShapeDtypeStruct((T, D), F32), jax.ShapeDtypeStruct((T, D), BF16)],
        compiler_params=_cparams(("parallel",)),
        name="outproj",
    )(*parts, w, x, next_gain.reshape(1, D).astype(F32))


_CAND_ROWS = 80


def _cand_tables(lanes):
    pos = np.zeros((_CAND_ROWS,), np.float32)
    neg = np.zeros((_CAND_ROWS,), np.float32)
    r = 0
    for a, nb in ((0, 16), (1, 8), (2, 8), (3, 8), (4, 8), (5, 8), (6, 8), (7, 8)):
        for b in range(nb):
            pos[r] = a * PEER_TOPK + b
            neg[r] = 0.0 if (a + 1) * (b + 1) <= PEER_TOPK else -np.inf
            r += 1
    for a in range(8, 16):
        pos[r] = a * PEER_TOPK
        r += 1
    assert r == _CAND_ROWS
    tab = lambda t: jnp.asarray(np.broadcast_to(t[:, None], (_CAND_ROWS, lanes)).copy())
    return tab(pos), tab(neg)


def _batcher_pairs(n):
    pairs, p = [], 1
    while p < n:
        k = p
        while k >= 1:
            for j in range(k % p, n - k, 2 * k):
                for i in range(min(k, n - j - k)):
                    if (i + j) // (2 * p) == (i + j + k) // (2 * p):
                        pairs.append((i + j, i + j + k))
            k //= 2
        p *= 2
    return pairs


_SORT16 = _batcher_pairs(PEER_TOPK)
_SUBLANES = 8
_N_CAND_PIECES = _CAND_ROWS // _SUBLANES
_SORT10 = [(i, j) for i, j in _SORT16 if j < _N_CAND_PIECES]


def _compare_exchange(items, i, j):
    items[i], items[j] = jnp.maximum(items[i], items[j]), jnp.minimum(items[i], items[j])


def _top_sorted(pieces, pairs):
    items = list(pieces)
    for i, j in pairs:
        _compare_exchange(items, i, j)
    n = PEER_TOPK
    items += [jnp.full(items[0].shape, -jnp.inf, F32)] * (n - len(items))
    for shift in (4, 2, 1):
        items = [jnp.maximum(items[i], pltpu.roll(items[n - 1 - i], shift, 0)) for i in range(n)]
        d = n // 2
        while d >= 1:
            for i in range(n):
                if i & d == 0:
                    _compare_exchange(items, i, i + d)
            d //= 2
    return items


def _pieces(x):
    return [x[_SUBLANES * g:_SUBLANES * (g + 1)] for g in range(x.shape[0] // _SUBLANES)]


def _sublane_total(x):
    for shift in (4, 2, 1):
        x = x + pltpu.roll(x, shift, 0)
    return x


def _count_ge(pieces, thr):
    total = jnp.zeros(thr.shape, F32)
    for p in pieces:
        total = total + jnp.where(p >= thr, 1.0, 0.0)
    return _sublane_total(total)


def _route_head_fast(s1, s2):
    p1, p2 = _pieces(s1), _pieces(s2)
    v1 = _top_sorted(p1, _SORT16)
    v2 = _top_sorted(p2, _SORT16)
    sub = lax.broadcasted_iota(jnp.int32, v1[0].shape, 0)

    def spread(vals):
        out = vals[0]
        for r in range(1, _SUBLANES):
            out = jnp.where(sub == r, vals[r], out)
        return out

    v2_lo, v2_hi, v1_hi = spread(v2[:8]), spread(v2[8:]), spread(v1[8:])
    cands = [v1[0] + v2_lo, v1[0] + v2_hi, v1[1] + v2_lo]
    for a in range(2, 8):
        cands.append(jnp.where(sub < PEER_TOPK // (a + 1), v1[a] + v2_lo, -jnp.inf))
    cands.append(v1_hi + v2[0])
    ts = _top_sorted(cands, _SORT10)
    tau = ts[PEER_TOPK - 1]
    z = jnp.exp(ts[0] - ts[0])
    for kk in range(1, PEER_TOPK):
        z = z + jnp.exp(ts[kk] - ts[0])

    tied = (_count_ge(p1, v1[-1]) != float(PEER_TOPK)) | (_count_ge(p2, v2[-1]) != float(PEER_TOPK))
    tied = tied | (_count_ge(cands, tau) != float(PEER_TOPK))
    for b in range(PEER_TOPK - 1):
        tied = tied | (v1[b] == v1[b + 1]) | (v2[b] == v2[b + 1])

    cnt = []
    for a in range(PEER_TOPK):
        c = jnp.zeros(tau.shape, F32)
        for b in range(PEER_TOPK // (a + 1)):
            c = c + jnp.where(v1[a] + v2[b] >= tau, 1.0, 0.0)
        cnt.append(c)
    c1, r2 = [], []
    for x in p1:
        c = jnp.zeros(x.shape, F32)
        for a in range(PEER_TOPK):
            c = jnp.where(x == v1[a], cnt[a], c)
        c1.append(c)
    for x in p2:
        r = jnp.zeros(x.shape, F32)
        for b in range(PEER_TOPK):
            r = r + jnp.where(v2[b] > x, 1.0, 0.0)
        r2.append(r)
    inv_z = 1.0 / z
    e1 = [jnp.exp(x - v1[0]) * inv_z for x in p1]
    e2 = [jnp.exp(x - v2[0]) for x in p2]
    cat = lambda ps: jnp.concatenate(ps, axis=0)
    return (cat(c1), cat(e1), cat(r2), cat(e2)), tied


def _extract_sorted(scores, by_key):
    nk, lanes = scores[0].shape
    kio = lax.broadcasted_iota(jnp.int32, (nk, lanes), 0).astype(F32)
    slot = lax.broadcasted_iota(jnp.int32, (PEER_TOPK, lanes), 0)

    def body(a, carry):
        here = slot == a
        out = []
        for (v, vals, aux), ranked in zip(carry, by_key):
            m = jnp.max(v, axis=0, keepdims=True)
            idx = jnp.min(jnp.where(v == m, kio, float(nk)), axis=0, keepdims=True)
            hit = kio == idx
            aux = jnp.where(hit, jnp.asarray(a, F32), aux) if ranked else jnp.where(here, idx, aux)
            out.append((jnp.where(hit, -jnp.inf, v), jnp.where(here, m, vals), aux))
        return tuple(out)

    small = jnp.zeros((PEER_TOPK, lanes), F32)
    unranked = jnp.full((nk, lanes), float(PEER_TOPK), F32)
    init = tuple((v, small, unranked if ranked else small) for v, ranked in zip(scores, by_key))
    return [(vals, aux) for _, vals, aux in lax.fori_loop(0, PEER_TOPK, body, init)]


def _route_head_exact(s1, s2, pos, neg):
    lanes = s1.shape[1]
    slot = lax.broadcasted_iota(jnp.int32, (PEER_TOPK, lanes), 0)
    kio = lax.broadcasted_iota(jnp.int32, (PEER_N_KEYS, lanes), 0).astype(F32)
    (v1, idx1), (v2, rank2) = _extract_sorted([s1, s2], [False, True])
    blocks = [v1[0:1] + v2[0:8], v1[0:1] + v2[8:16]]
    blocks += [v1[a:a + 1] + v2[0:8] for a in range(1, 8)]
    blocks += [v1[8:16] + v2[0:1]]
    cand = jnp.concatenate(blocks, axis=0) + neg

    def pick(kk, carry):
        cand, chosen, ts = carry
        m = jnp.max(cand, axis=0, keepdims=True)
        first = jnp.min(jnp.where(cand == m, pos, 1e9), axis=0, keepdims=True)
        hit = pos == first
        return (jnp.where(hit, -jnp.inf, cand), jnp.where(hit, 1.0, chosen), jnp.where(slot == kk, m, ts))

    _, chosen, ts = lax.fori_loop(0, PEER_TOPK, pick,
                                  (cand, jnp.zeros_like(cand), jnp.zeros((PEER_TOPK, lanes), F32)))
    z = jnp.sum(jnp.exp(ts - ts[0:1]), axis=0, keepdims=True)
    counts = [jnp.sum(chosen[0:16], axis=0, keepdims=True)]
    counts += [jnp.sum(chosen[8 * a + 8:8 * a + 16], axis=0, keepdims=True) for a in range(1, 8)]
    counts += [chosen[72 + a:73 + a] for a in range(8)]
    c1 = jnp.zeros((PEER_N_KEYS, lanes), F32)
    for a in range(PEER_TOPK):
        c1 = jnp.where(kio == idx1[a:a + 1], counts[a], c1)
    return c1, jnp.exp(s1 - v1[0:1]) / z, rank2, jnp.exp(s2 - v2[0:1])


def _route_kernel(h_ref, wq_ref, keys_ref, pos_ref, neg_ref,
                  c1_ref, e1_ref, r2_ref, e2_ref, qt_ref, sc_ref):
    half = PEER_QUERY_DIM // 2
    qt_ref[...] = _dot_nt(wq_ref[...], h_ref[...]).astype(BF16)

    def store(h, maps):
        c1, e1, r2, e2 = maps
        c1_ref[h] = c1
        e1_ref[h] = e1
        r2_ref[h] = r2.astype(BF16)
        e2_ref[h] = e2.astype(BF16)

    def head_body(h, _):
        r0 = pl.multiple_of(h * PEER_QUERY_DIM, PEER_QUERY_DIM)
        sc_ref[0] = _dot(keys_ref[2 * h], qt_ref[pl.ds(r0, half), :])
        sc_ref[1] = _dot(keys_ref[2 * h + 1], qt_ref[pl.ds(r0 + half, half), :])
        maps, tied = _route_head_fast(sc_ref[0], sc_ref[1])
        any_tied = jnp.max(jnp.where(tied, 1.0, 0.0)) > 0.0

        @pl.when(any_tied)
        def _():
            store(h, _route_head_exact(sc_ref[0], sc_ref[1], pos_ref[...], neg_ref[...]))

        @pl.when(jnp.logical_not(any_tied))
        def _():
            store(h, maps)

        return 0

    lax.fori_loop(0, PEER_HEADS, head_body, 0)


def _peer_route(h2, wq_t, keys, tt=ROUTE_TILE):
    T, D = h2.shape
    pos, neg = _cand_tables(tt)
    stat_spec = pl.BlockSpec((PEER_HEADS, PEER_N_KEYS, tt), lambda i: (0, 0, i))
    stat = lambda dt: jax.ShapeDtypeStruct((PEER_HEADS, PEER_N_KEYS, T), dt)
    return pl.pallas_call(
        _route_kernel,
        grid=(T // tt,),
        in_specs=[pl.BlockSpec((tt, D), lambda i: (i, 0)),
                  pl.BlockSpec(wq_t.shape, lambda i: (0, 0)),
                  pl.BlockSpec(keys.shape, lambda i: (0, 0, 0)),
                  pl.BlockSpec((_CAND_ROWS, tt), lambda i: (0, 0)),
                  pl.BlockSpec((_CAND_ROWS, tt), lambda i: (0, 0))],
        out_specs=[stat_spec] * 4,
        out_shape=[stat(F32), stat(F32), stat(BF16), stat(BF16)],
        scratch_shapes=[pltpu.VMEM((PEER_HEADS * PEER_QUERY_DIM, tt), BF16),
                        pltpu.VMEM((2, PEER_N_KEYS, tt), F32)],
        compiler_params=_cparams(("parallel",)),
        name="peer_route",
    )(h2, wq_t, keys, pos, neg)


_KEY_GROUP = 16
_UNITS = 8
_FRONT_PIECES = 4
_DRAIN_PIECES = 2


def _build_gated(a_ref, p_ref, c1_ref, e1_ref, r2_ref, e2_ref, key0, g0, ng, lt):
    rep = PEER_N_KEYS // BF16_ROWS

    def rows16(row):
        blk = jnp.broadcast_to(row, (BF16_ROWS, LANES)).astype(BF16)
        return jnp.concatenate([blk] * rep, axis=0)

    ls = slice(lt * LANES, (lt + 1) * LANES)
    w = [jnp.zeros((PEER_N_KEYS, LANES), BF16) for _ in range(ng)]
    for h in range(PEER_HEADS):
        c1 = c1_ref[h, pl.ds(key0, _KEY_GROUP), ls]
        e1 = e1_ref[h, pl.ds(key0, _KEY_GROUP), ls]
        r2 = r2_ref[h, :, ls]
        e2 = e2_ref[h, :, ls]
        for g in range(ng):
            thr = rows16(c1[g0 + g:g0 + g + 1])
            gate = rows16(e1[g0 + g:g0 + g + 1])
            w[g] = w[g] + jnp.where(r2 < thr, e2, jnp.zeros_like(e2)) * gate
    for g in range(g0, g0 + ng):
        rs = slice(g * PEER_N_KEYS, (g + 1) * PEER_N_KEYS)
        a = a_ref[rs, ls]
        gelu = 0.5 * a * (1.0 + lax.erf(a * (2.0 ** -0.5)))
        p_ref[rs, ls] = gelu.astype(BF16) * w[g - g0]


def _experts_kernel(h_ref, dn_ref, upt_ref, c1_ref, e1_ref, r2_ref, e2_ref, x_ref, *rest, emit_norm):
    if emit_norm:
        g_ref, o_ref, hn_ref, a_ref, p_ref, acc_ref = rest
    else:
        o_ref, a_ref, p_ref, acc_ref = rest
    te, tt = a_ref.shape
    d_model = acc_ref.shape[0]
    e = pl.program_id(1)
    n_tiles = pl.num_programs(1) - 1
    cur = e % 2
    ng = _KEY_GROUP // _UNITS
    n_lane = tt // LANES
    key0 = pl.multiple_of(jnp.minimum(e, n_tiles - 1) * _KEY_GROUP, _KEY_GROUP)

    def front_mm(r):
        rows = te // _FRONT_PIECES
        rs = slice(r * rows, (r + 1) * rows)
        a_ref[rs, :] = _dot_nt(dn_ref[rs, :], h_ref[...])

    def back_mm(r):
        rows = d_model // _DRAIN_PIECES
        rs = slice(r * rows, (r + 1) * rows)
        acc_ref[rs, :] += _dot(upt_ref[rs, :], p_ref[1 - cur])

    units_per_drain = _UNITS // _DRAIN_PIECES
    units_per_front = _UNITS // _FRONT_PIECES

    def run(front, back):
        if front:
            front_mm(0)
        for u in range(_UNITS):
            for lt in range(n_lane):
                if front:
                    _build_gated(a_ref, p_ref.at[cur], c1_ref, e1_ref, r2_ref, e2_ref, key0, u * ng, ng, lt)
                if front and lt == 0 and u % units_per_front == 0 and u // units_per_front + 1 < _FRONT_PIECES:
                    front_mm(u // units_per_front + 1)
                if back and lt == n_lane // 2 and (u + 1) % units_per_drain == 0:
                    back_mm(u // units_per_drain)

    @pl.when(e == 0)
    def _():
        acc_ref[...] = jnp.zeros_like(acc_ref)
        run(True, False)

    @pl.when((e > 0) & (e < n_tiles))
    def _():
        run(True, True)

    @pl.when(e == n_tiles)
    def _():
        run(False, True)
        y = x_ref[...] + acc_ref[...].T
        o_ref[...] = y
        if emit_norm:
            hn_ref[...] = _rms_normed(y, g_ref[...])


def _peer_experts(h2, down, up_t, stats, x, next_gain, tt=EXPERT_TOKEN_TILE):
    T, D = h2.shape
    E = down.shape[0]
    te = _KEY_GROUP * PEER_N_KEYS
    n_tiles = E // te
    emit_norm = next_gain is not None
    stat_spec = pl.BlockSpec((PEER_HEADS, PEER_N_KEYS, tt), lambda i, e: (0, 0, i))
    row_spec = pl.BlockSpec((tt, D), lambda i, e: (i, 0))
    in_specs = [row_spec,
                pl.BlockSpec((te, D), lambda i, e: (jnp.minimum(e, n_tiles - 1), 0)),
                pl.BlockSpec((None, D, te), lambda i, e: (jnp.maximum(e - 1, 0), 0, 0)),
                stat_spec, stat_spec, stat_spec, stat_spec,
                row_spec]
    args = [h2, down, up_t, *stats, x]
    out_specs = [row_spec]
    out_shape = [jax.ShapeDtypeStruct((T, D), F32)]
    if emit_norm:
        in_specs.append(pl.BlockSpec((1, D), lambda i, e: (0, 0)))
        args.append(next_gain.reshape(1, D).astype(F32))
        out_specs.append(row_spec)
        out_shape.append(jax.ShapeDtypeStruct((T, D), BF16))
    stat_bytes = 2 * PEER_HEADS * PEER_N_KEYS * tt * (4 + 2)
    block_bytes = (tt * D * 2 + 2 * te * D * 2 + stat_bytes + tt * D * 4
                   + tt * D * 4 + (tt * D * 2 + D * 4 if emit_norm else 0))
    scratch_bytes = te * tt * 4 + 2 * te * tt * 2 + D * tt * 4
    res = pl.pallas_call(
        functools.partial(_experts_kernel, emit_norm=emit_norm),
        grid=(T // tt, n_tiles + 1),
        in_specs=in_specs, out_specs=out_specs, out_shape=out_shape,
        scratch_shapes=[pltpu.VMEM((te, tt), F32),
                        pltpu.VMEM((2, te, tt), BF16),
                        pltpu.VMEM((D, tt), F32)],
        compiler_params=_cparams(("parallel", "arbitrary"), 2 * block_bytes + scratch_bytes + VMEM_SPILL_MARGIN),
        name="peer_experts",
    )(*args)
    return (res[0], res[1]) if emit_norm else (res[0], None)


def _peer_layer(x, h2, next_gain, w_query, sub_keys, down, up):
    keys = sub_keys.reshape(PEER_HEADS * 2, PEER_N_KEYS, PEER_QUERY_DIM // 2).astype(BF16)
    stats = _peer_route(h2, w_query.T.astype(BF16), keys)
    te = _KEY_GROUP * PEER_N_KEYS
    up_t = up.reshape(up.shape[0] // te, te, up.shape[1]).transpose(0, 2, 1).astype(BF16)
    return _peer_experts(h2, down.astype(BF16), up_t, stats, x, next_gain)


def _tile_heads(g, n):
    return jnp.tile(g.astype(F32), n)


def _even_mixer(x, h, ffn_gain, w_in, f_bias, qn_a, kn_a, qn_b, kn_b, w_out, tabs, *, batch, seq):
    w = w_in.astype(BF16)
    o_qa, o_ka, o_va, o_qb, o_kb, o_vb, o_gb, o_fb = (
        0, A_W, 2 * A_W, 3 * A_W, 3 * A_W + B_W, 3 * A_W + 2 * B_W, 3 * A_W + 3 * B_W, 3 * A_W + 4 * B_W)
    gain_a = jnp.concatenate([_tile_heads(qn_a, MOBA_HEADS), _tile_heads(kn_a, MOBA_HEADS)])
    qk_a, km = _proj(h, w[:, o_qa:o_va], seq=seq, tn=A_W, gain=gain_a, rope_tabs=tabs, kmean=True)
    nb = seq // MOBA_BLOCK
    kmean = km.reshape(batch, nb, 2 * A_W)[:, :, A_W:]
    kmean = jnp.pad(kmean, ((0, 0), (0, LANES - nb), (0, 0))).astype(BF16)
    gain_b = jnp.concatenate([_tile_heads(qn_b, FOX_HEADS), _tile_heads(kn_b, FOX_HEADS)])
    qk_b = _proj(h, w[:, o_qb:o_vb], seq=seq, tn=B_W, gain=gain_b)
    w_plain = jnp.concatenate([w[:, o_va:o_qb], w[:, o_vb:o_fb]], axis=1)
    vvg = _proj(h, w_plain, seq=seq, tn=A_W)
    va, vb, gb = vvg[:, :A_W], vvg[:, A_W:A_W + B_W], vvg[:, A_W + B_W:]
    frow = _fox_gates(h, w[:, o_fb:].T, f_bias, batch=batch, seq=seq)
    oa = _moba(qk_a[:, :A_W], qk_a[:, A_W:], va, kmean, batch=batch, seq=seq)
    logit_bound = 1.01 * ATTN_SCALE * HEAD_DIM * jnp.max(jnp.abs(qn_b)) * jnp.max(jnp.abs(kn_b)) + 0.01
    ob = _fox(qk_b[:, :B_W], qk_b[:, B_W:], vb, gb, frow, logit_bound, batch=batch, seq=seq)
    return _outproj([oa, ob], w_out.astype(BF16), x, ffn_gain)


def _odd_mixer(x, h, ffn_gain, w_in, qn, kn, sinks, w_out, tabs, *, batch, seq):
    w = w_in.astype(BF16)
    qw = SWA_Q_HEADS * HEAD_DIM
    kw = SWA_KV_HEADS * HEAD_DIM
    q = _proj(h, w[:, :qw], seq=seq, tn=512, gain=_tile_heads(qn, SWA_Q_HEADS), rope_tabs=tabs)
    k = _proj(h, w[:, qw:qw + kw], seq=seq, tn=kw, gain=_tile_heads(kn, SWA_KV_HEADS), rope_tabs=tabs)
    v = _proj(h, w[:, qw + kw:], seq=seq, tn=kw)
    o = _swa(q, k, v, sinks, batch=batch, seq=seq)
    return _outproj([o], w_out.astype(BF16), x, ffn_gain)


def kernel(x, attn_norm, ffn_norm, ev_w_in, ev_forget_bias, ev_q_norm_a, ev_k_norm_a, ev_q_norm_b,
           ev_k_norm_b, ev_w_out, od_w_in, od_q_norm, od_k_norm, od_sinks, od_w_out,
           peer_w_query, peer_sub_keys, peer_down, peer_up):
    batch, seq, d_model = x.shape
    depth = attn_norm.shape[0]
    tabs = _rope_tables(seq)
    xt = x.reshape(batch * seq, d_model)
    h = _rmsnorm(xt, attn_norm[0])
    for l in range(depth):
        i = l // 2
        if l % 2 == 0:
            xt, h2 = _even_mixer(xt, h, ffn_norm[l], ev_w_in[i], ev_forget_bias[i], ev_q_norm_a[i],
                                 ev_k_norm_a[i], ev_q_norm_b[i], ev_k_norm_b[i], ev_w_out[i], tabs,
                                 batch=batch, seq=seq)
        else:
            xt, h2 = _odd_mixer(xt, h, ffn_norm[l], od_w_in[i], od_q_norm[i], od_k_norm[i], od_sinks[i],
                                od_w_out[i], tabs, batch=batch, seq=seq)
        next_gain = attn_norm[l + 1] if l + 1 < depth else None
        xt, h = _peer_layer(xt, h2, next_gain, peer_w_query[l], peer_sub_keys[l], peer_down[l], peer_up[l])
    return xt.reshape(batch, seq, d_model)
```

```python
import functools

import numpy as np
import jax
import jax.numpy as jnp
from jax import lax
from jax.experimental import pallas as pl
from jax.experimental.pallas import tpu as pltpu

F32 = jnp.float32
BF16 = jnp.bfloat16

HEAD_DIM = 64
ROT_DIM = HEAD_DIM // 4
ROPE_THETA = 500000.0
ATTN_SCALE = HEAD_DIM ** -0.5
EPS = 1e-6
NEG_INF = -1e30

MOBA_HEADS = 8
FOX_HEADS = 8
MOBA_BLOCK = 256
MOBA_TOPK = 3
A_W = MOBA_HEADS * HEAD_DIM
B_W = FOX_HEADS * HEAD_DIM

SWA_Q_HEADS = 16
SWA_KV_HEADS = 2
SWA_WINDOW = 128

PEER_HEADS = 8
PEER_N_KEYS = 128
PEER_TOPK = 16
PEER_QUERY_DIM = 128

LANES = 128
BF16_ROWS = 16
PAIR_W = 2 * HEAD_DIM

ROW_TILE = 512
FLASH_TILE = 2 * MOBA_BLOCK
ROUTE_TILE = 2 * LANES
EXPERT_TOKEN_TILE = 512
VMEM_LIMIT = 48 * 1024 * 1024
VMEM_SPILL_MARGIN = 4 * 1024 * 1024


def _cparams(sem, vmem_limit=VMEM_LIMIT):
    return pltpu.CompilerParams(dimension_semantics=sem, vmem_limit_bytes=vmem_limit)


def _dot_nt(a, b):
    return lax.dot_general(a, b, (((1,), (1,)), ((), ())), preferred_element_type=F32)


def _dot(a, b):
    return jnp.dot(a, b, preferred_element_type=F32)


def _split2(x):
    h1 = x.astype(BF16)
    return h1, (x - h1.astype(F32)).astype(BF16)


def _split3(x):
    h1 = x.astype(BF16)
    r1 = x - h1.astype(F32)
    h2 = r1.astype(BF16)
    h3 = (r1 - h2.astype(F32)).astype(BF16)
    return h1, h2, h3


def _rmsnorm_kernel(x_ref, g_ref, o_ref):
    x = x_ref[...]
    ms = jnp.mean(x * x, axis=-1, keepdims=True)
    o_ref[...] = (x * lax.rsqrt(ms + EPS) * g_ref[...]).astype(o_ref.dtype)


def _rmsnorm(x, gain, tm=ROW_TILE):
    T, D = x.shape
    return pl.pallas_call(
        _rmsnorm_kernel,
        grid=(T // tm,),
        in_specs=[pl.BlockSpec((tm, D), lambda i: (i, 0)),
                  pl.BlockSpec((1, D), lambda i: (0, 0))],
        out_specs=pl.BlockSpec((tm, D), lambda i: (i, 0)),
        out_shape=jax.ShapeDtypeStruct((T, D), BF16),
        compiler_params=_cparams(("parallel",)),
        name="rmsnorm",
    )(x, gain.reshape(1, D))


def _proj_kernel(*refs, norm, rope, kmean, tn):
    it = iter(refs)
    h_ref, w_ref = next(it), next(it)
    gain_ref = next(it) if norm else None
    bd_ref = next(it) if norm else None
    if rope:
        c_ref, sa_ref, sb_ref = next(it), next(it), next(it)
    o_ref = next(it)
    km_ref = next(it) if kmean else None

    y = _dot(h_ref[...], w_ref[...])
    if norm:
        y2 = y * y
        bd = bd_ref[...]
        cols = []
        tm = y.shape[0]
        for c in range(tn // LANES):
            h1, h2 = _split2(y2[:, c * LANES:(c + 1) * LANES])
            t = _dot(jnp.concatenate([h1, h2], axis=0), bd)
            cols.append(t[:tm] + t[tm:])
        ms = cols[0] if len(cols) == 1 else jnp.concatenate(cols, axis=1)
        y = y * lax.rsqrt(ms + EPS) * gain_ref[...]
    if rope:
        rep = tn // LANES
        tile = (lambda t: t) if rep == 1 else (lambda t: jnp.concatenate([t] * rep, axis=1))
        y = (y * tile(c_ref[...])
             + pltpu.roll(y, tn - ROT_DIM // 2, 1) * tile(sa_ref[...])
             + pltpu.roll(y, ROT_DIM // 2, 1) * tile(sb_ref[...]))
    o_ref[...] = y.astype(o_ref.dtype)
    if kmean:
        for r in range(km_ref.shape[0]):
            km_ref[r] = jnp.mean(y[r * MOBA_BLOCK:(r + 1) * MOBA_BLOCK], axis=0, keepdims=True)


def _proj(h, w, *, seq, tn, tm=ROW_TILE, gain=None, rope_tabs=None, kmean=False):
    T, D = h.shape
    N = w.shape[1]
    norm = gain is not None
    rope = rope_tabs is not None
    nseq = seq // tm
    in_specs = [pl.BlockSpec((tm, D), lambda i, j: (i, 0)),
                pl.BlockSpec((D, tn), lambda i, j: (0, j))]
    args = [h, w]
    if norm:
        bd = np.kron(np.eye(LANES // HEAD_DIM), np.ones((HEAD_DIM, HEAD_DIM))) / HEAD_DIM
        in_specs += [pl.BlockSpec((1, tn), lambda i, j: (0, j)),
                     pl.BlockSpec((LANES, LANES), lambda i, j: (0, 0))]
        args += [gain.reshape(1, N).astype(F32), jnp.asarray(bd, BF16)]
    if rope:
        in_specs += [pl.BlockSpec((tm, LANES), lambda i, j: (i % nseq, 0))] * 3
        args += list(rope_tabs)
    out_specs = [pl.BlockSpec((tm, tn), lambda i, j: (i, j))]
    out_shape = [jax.ShapeDtypeStruct((T, N), BF16)]
    if kmean:
        out_specs.append(pl.BlockSpec((tm // MOBA_BLOCK, 1, tn), lambda i, j: (i, 0, j)))
        out_shape.append(jax.ShapeDtypeStruct((T // MOBA_BLOCK, 1, N), F32))
    res = pl.pallas_call(
        functools.partial(_proj_kernel, norm=norm, rope=rope, kmean=kmean, tn=tn),
        grid=(T // tm, N // tn),
        in_specs=in_specs, out_specs=out_specs, out_shape=out_shape,
        compiler_params=_cparams(("parallel", "parallel")),
        name="proj",
    )(*args)
    return res if kmean else res[0]


def _rope_tables(seq):
    half = ROT_DIM // 2
    inv_freq = jnp.power(ROPE_THETA, -jnp.arange(0, ROT_DIM, 2, dtype=F32) / ROT_DIM)
    ang = jnp.arange(seq, dtype=F32)[:, None] * inv_freq[None, :]
    cos, sin = jnp.cos(ang), jnp.sin(ang)
    one = jnp.ones((seq, HEAD_DIM - ROT_DIM), F32)
    zero = jnp.zeros((seq, HEAD_DIM - ROT_DIM), F32)
    z8 = jnp.zeros((seq, half), F32)
    c = jnp.concatenate([cos, cos, one], axis=1)
    sa = jnp.concatenate([-sin, z8, zero], axis=1)
    sb = jnp.concatenate([z8, sin, zero], axis=1)
    rep = LANES // HEAD_DIM
    return tuple(jnp.concatenate([t] * rep, axis=1) for t in (c, sa, sb))


def _gates_kernel(h_ref, wf_ref, b_ref, tri_ref, o_ref, carry_ref):
    @pl.when(pl.program_id(1) == 0)
    def _():
        carry_ref[...] = jnp.zeros_like(carry_ref)

    z = _dot_nt(wf_ref[...], h_ref[...]) + b_ref[...][:, :1]
    lf = jnp.minimum(z, 0.0) - jnp.log1p(jnp.exp(-jnp.abs(z)))
    tri = tri_ref[...]
    h1, h2, h3 = _split3(lf)
    cs = _dot(h1, tri) + _dot(h2, tri) + _dot(h3, tri) + carry_ref[...][:, :1]
    o_ref[0] = cs
    carry_ref[...] = jnp.broadcast_to(cs[:, -1:], carry_ref.shape)


def _fox_gates(h, wf_t, bias, *, batch, seq, tm=ROW_TILE):
    T, D = h.shape
    nh = wf_t.shape[0]
    nseq = seq // tm
    tri = jnp.asarray(np.triu(np.ones((tm, tm))), BF16)
    return pl.pallas_call(
        _gates_kernel,
        grid=(batch, nseq),
        in_specs=[pl.BlockSpec((tm, D), lambda b, s: (b * nseq + s, 0)),
                  pl.BlockSpec((nh, D), lambda b, s: (0, 0)),
                  pl.BlockSpec((nh, LANES), lambda b, s: (0, 0)),
                  pl.BlockSpec((tm, tm), lambda b, s: (0, 0))],
        out_specs=pl.BlockSpec((1, nh, tm), lambda b, s: (b, 0, s)),
        out_shape=jax.ShapeDtypeStruct((batch, nh, seq), F32),
        scratch_shapes=[pltpu.VMEM((nh, LANES), F32)],
        compiler_params=_cparams(("parallel", "arbitrary")),
        name="fox_gates",
    )(h, wf_t, jnp.broadcast_to(bias.astype(F32)[:, None], (nh, LANES)), tri)


def _lane_tile(x, width):
    rep = width // LANES
    return x if rep == 1 else jnp.concatenate([x] * rep, axis=1)


def _flash_init(m_ref, acc_ref):
    m_ref[...] = jnp.full(m_ref.shape, NEG_INF, F32)
    acc_ref[...] = jnp.zeros(acc_ref.shape, F32)


def _head_values(v):
    lane = lax.broadcasted_iota(jnp.int32, v.shape, 1)
    return [jnp.where((lane >= hh * HEAD_DIM) & (lane < (hh + 1) * HEAD_DIM), v, jnp.ones_like(v))
            for hh in range(2)]


def _flash_update(slot, s, v, m_ref, acc_ref):
    tk = s.shape[1]
    m_prev = m_ref[slot]
    m_new = jnp.maximum(m_prev, jnp.max(s, axis=1, keepdims=True))
    alpha = jnp.exp(m_prev - m_new)
    p = jnp.exp(s - _lane_tile(m_new, tk))
    acc_ref[slot] = alpha * acc_ref[slot] + _dot(p.astype(BF16), v)
    m_ref[slot] = m_new


def _flash_finish(lane, acc_ref):
    outs = []
    for hh in range(2):
        acc = acc_ref[hh]
        den = (1 - hh) * HEAD_DIM
        outs.append(acc / acc[:, den:den + 1])
    return jnp.where(lane < HEAD_DIM, outs[0], outs[1])


def _flash_tiles(qi, tk, logits, values, m_ref, acc_ref):
    def absorb(s, off):
        vh = values(off)
        for hh in range(2):
            _flash_update(hh, s[hh], vh[hh], m_ref, acc_ref)

    def pair(off_a, off_b, b_diagonal):
        sa = logits(off_a, False)
        sb = logits(off_b, b_diagonal)
        absorb(sa, off_a)
        absorb(sb, off_b)

    def body(jj, carry):
        off = pl.multiple_of(2 * jj * tk, 2 * tk)
        pair(off, off + tk, False)
        return carry

    lax.fori_loop(0, qi // 2, body, 0)
    diag = pl.multiple_of(qi * tk, tk)

    @pl.when(qi % 2 == 1)
    def _():
        pair(diag - tk, diag, True)

    @pl.when(qi % 2 == 0)
    def _():
        absorb(logits(diag, True), diag)


def _head_queries(q, lane):
    qs = q * ATTN_SCALE
    return [jnp.where((lane >= hh * HEAD_DIM) & (lane < (hh + 1) * HEAD_DIM), qs, jnp.zeros_like(qs))
            for hh in range(2)]


def _moba_kernel(q_ref, k_ref, v_ref, km_ref, o_ref, m_ref, acc_ref):
    tq = q_ref.shape[1]
    tk = tq
    qi = pl.program_id(2)
    lane = lax.broadcasted_iota(jnp.int32, (tq, LANES), 1)
    lane_f = lane.astype(F32)
    rowv = lax.broadcasted_iota(jnp.int32, (tq, LANES), 0)
    row_blk = 2 * qi + (rowv >= MOBA_BLOCK).astype(jnp.int32)
    row = lax.broadcasted_iota(jnp.int32, (tq, tk), 0)
    col = lax.broadcasted_iota(jnp.int32, (tq, tk), 1)
    qh = _head_queries(q_ref[0], lane)
    _flash_init(m_ref, acc_ref)

    sels = []
    for hh in range(2):
        gate = _dot_nt(qh[hh], km_ref[0])
        gate = jnp.where(lane < row_blk, gate, -jnp.inf)
        sel = jnp.zeros((tq, LANES), F32)
        for _ in range(MOBA_TOPK):
            m = jnp.max(gate, axis=1, keepdims=True)
            idx = jnp.min(jnp.where(gate == m, lane_f, float(LANES)), axis=1, keepdims=True)
            hit = lane_f == idx
            sel = jnp.where(hit & (m > -jnp.inf), 1.0, sel)
            gate = jnp.where(hit, -jnp.inf, gate)
        sels.append(sel)

    def chosen(sel, blk):
        return jnp.max(jnp.where(lane == blk, sel, 0.0), axis=1, keepdims=True) > 0.0

    def logits(off, diagonal):
        kj = k_ref[0, pl.ds(off, tk), :]
        out = []
        for hh in range(2):
            s = _dot_nt(qh[hh], kj)
            if diagonal:
                visible = (col >= MOBA_BLOCK) | (row < MOBA_BLOCK) | chosen(sels[hh], 2 * qi)
                s = jnp.where((col <= row) & visible, s, NEG_INF)
            else:
                blk = 2 * (off // tk)
                s = jnp.concatenate(
                    [jnp.where(chosen(sels[hh], blk), s[:, :MOBA_BLOCK], NEG_INF),
                     jnp.where(chosen(sels[hh], blk + 1), s[:, MOBA_BLOCK:], NEG_INF)], axis=1)
            out.append(s)
        return out

    def values(off):
        return _head_values(v_ref[0, pl.ds(off, tk), :])

    _flash_tiles(qi, tk, logits, values, m_ref, acc_ref)
    o_ref[0] = _flash_finish(lane, acc_ref).astype(o_ref.dtype)


def _flash_scratch(tq):
    return [pltpu.VMEM((2, tq, LANES), F32)] * 2


def _moba(q, k, v, kmean, *, batch, seq):
    W = q.shape[1]
    tq = FLASH_TILE
    q3, k3, v3 = (t.reshape(batch, seq, W) for t in (q, k, v))
    out = pl.pallas_call(
        _moba_kernel,
        grid=(batch, W // PAIR_W, seq // tq),
        in_specs=[pl.BlockSpec((1, tq, PAIR_W), lambda b, p, i: (b, i, p)),
                  pl.BlockSpec((1, seq, PAIR_W), lambda b, p, i: (b, 0, p)),
                  pl.BlockSpec((1, seq, PAIR_W), lambda b, p, i: (b, 0, p)),
                  pl.BlockSpec((1, LANES, PAIR_W), lambda b, p, i: (b, 0, p))],
        out_specs=pl.BlockSpec((1, tq, PAIR_W), lambda b, p, i: (b, i, p)),
        out_shape=jax.ShapeDtypeStruct((batch, seq, W), BF16),
        scratch_shapes=_flash_scratch(tq),
        compiler_params=_cparams(("parallel", "parallel", "parallel")),
        name="moba",
    )(q3, k3, v3, kmean)
    return out.reshape(batch * seq, W)


_EXP_UNDERFLOW = 106.0


def _fox_kernel(q_ref, k_ref, v_ref, g_ref, frow_ref, far_ref, o_ref, m_ref, acc_ref):
    tq = q_ref.shape[1]
    tk = tq
    pr = pl.program_id(1)
    qi = pl.program_id(2)
    lane = lax.broadcasted_iota(jnp.int32, (tq, LANES), 1)
    row = lax.broadcasted_iota(jnp.int32, (tq, tk), 0)
    col = lax.broadcasted_iota(jnp.int32, (tq, tk), 1)
    qh = _head_queries(q_ref[0], lane)
    _flash_init(m_ref, acc_ref)

    def key_gates(off, width):
        f_all = frow_ref[0, :, pl.ds(off, width)]
        sub = lax.broadcasted_iota(jnp.int32, f_all.shape, 0)
        return [jnp.sum(jnp.where(sub == 2 * pr + hh, f_all, 0.0), axis=0, keepdims=True) for hh in range(2)]

    f_ref = [f[:, :1] for f in key_gates(pl.multiple_of(qi * tq, tq), LANES)]

    def logits(off, diagonal):
        kj = k_ref[0, pl.ds(off, tk), :]
        fk = key_gates(off, tk)
        out = []
        for hh in range(2):
            s = _dot_nt(qh[hh], kj) - (fk[hh] - f_ref[hh])
            out.append(jnp.where(col <= row, s, NEG_INF) if diagonal else s)
        return out

    def absorb(s, off):
        vh = _head_values(v_ref[0, pl.ds(off, tk), :])
        for hh in range(2):
            _flash_update(hh, s[hh], vh[hh], m_ref, acc_ref)

    def vanishes(j):
        tail = frow_ref[0, :, pl.ds(pl.multiple_of((j + 1) * tk - LANES, LANES), LANES)]
        sub = lax.broadcasted_iota(jnp.int32, tail.shape, 0)
        last = lax.broadcasted_iota(jnp.int32, tail.shape, 1) == LANES - 1
        gap = tail - jnp.where(sub == 2 * pr, f_ref[0], f_ref[1]) - far_ref[...][:, :1]
        mine = last & ((sub == 2 * pr) | (sub == 2 * pr + 1))
        return jnp.min(jnp.where(mine, gap, jnp.inf)) >= 0.0

    absorb(logits(pl.multiple_of(qi * tk, tk), True), pl.multiple_of(qi * tk, tk))

    def more(j):
        return jnp.logical_and(j >= 1, jnp.logical_not(vanishes(jnp.maximum(j, 0))))

    def pair(j):
        off_a = pl.multiple_of(j * tk, tk)
        off_b = pl.multiple_of((j - 1) * tk, tk)
        sa = logits(off_a, False)
        sb = logits(off_b, False)
        absorb(sa, off_a)
        absorb(sb, off_b)
        return j - 2

    j = lax.while_loop(more, pair, qi - 1)

    @pl.when(jnp.logical_and(j == 0, jnp.logical_not(vanishes(0))))
    def _():
        absorb(logits(0, False), 0)

    o = _flash_finish(lane, acc_ref)
    o_ref[0] = (o * jax.nn.sigmoid(g_ref[0].astype(F32))).astype(o_ref.dtype)


def _fox(q, k, v, g, frow, logit_bound, *, batch, seq, tq=FLASH_TILE):
    far = jnp.full((1, LANES), 2.0 * logit_bound + _EXP_UNDERFLOW, F32)
    W = q.shape[1]
    nh = frow.shape[1]
    q3, k3, v3, g3 = (t.reshape(batch, seq, W) for t in (q, k, v, g))
    out = pl.pallas_call(
        _fox_kernel,
        grid=(batch, W // PAIR_W, seq // tq),
        in_specs=[pl.BlockSpec((1, tq, PAIR_W), lambda b, p, i: (b, i, p)),
                  pl.BlockSpec((1, seq, PAIR_W), lambda b, p, i: (b, 0, p)),
                  pl.BlockSpec((1, seq, PAIR_W), lambda b, p, i: (b, 0, p)),
                  pl.BlockSpec((1, tq, PAIR_W), lambda b, p, i: (b, i, p)),
                  pl.BlockSpec((1, nh, seq), lambda b, p, i: (b, 0, 0)),
                  pl.BlockSpec((1, LANES), lambda b, p, i: (0, 0))],
        out_specs=pl.BlockSpec((1, tq, PAIR_W), lambda b, p, i: (b, i, p)),
        out_shape=jax.ShapeDtypeStruct((batch, seq, W), BF16),
        scratch_shapes=_flash_scratch(tq),
        compiler_params=_cparams(("parallel", "parallel", "parallel")),
        name="fox",
    )(q3, k3, v3, g3, frow, far)
    return out.reshape(batch * seq, W)


def _swa_kernel(q_ref, k_ref, v_ref, sink_ref, bias_ref, o_ref):
    tq = q_ref.shape[1]
    qi = pl.program_id(1)
    group = SWA_Q_HEADS // SWA_KV_HEADS
    tk = tq + SWA_WINDOW
    lane = lax.broadcasted_iota(jnp.int32, (tq, LANES), 1)
    kstart = pl.multiple_of(jnp.maximum(qi * tq - SWA_WINDOW, 0), SWA_WINDOW)
    k = k_ref[0, pl.ds(kstart, tk), :]
    vh = _head_values(v_ref[0, pl.ds(kstart, tk), :])
    swap = lambda t: pltpu.roll(t.astype(F32), HEAD_DIM, 1).astype(BF16)
    k_by_half = [k, swap(k)]
    bias = bias_ref[jnp.minimum(qi, 1)]
    bias = jnp.concatenate([bias] * (group // 2), axis=0)
    sink_tab = sink_ref[...]

    chains = []
    for c in range(SWA_KV_HEADS):
        v_by_half = [vh[c], swap(vh[c])]
        for swapped in range(2):
            hh = c if not swapped else 1 - c
            heads = [h for h in range(c * group, (c + 1) * group) if h % 2 == hh]
            pieces, sinks = [], []
            for head in heads:
                blk = q_ref[0, :, (head // 2) * PAIR_W:(head // 2 + 1) * PAIR_W] * ATTN_SCALE
                pieces.append(jnp.where((lane >= hh * HEAD_DIM) & (lane < (hh + 1) * HEAD_DIM),
                                        blk, jnp.zeros_like(blk)))
                sinks.append(jnp.broadcast_to(sink_tab[head:head + 1, :], (tq, LANES)))
            s = _dot_nt(jnp.concatenate(pieces, axis=0), k_by_half[swapped]) + bias
            chains.append((heads, s, jnp.concatenate(sinks, axis=0), v_by_half[swapped]))

    outs = [None] * SWA_Q_HEADS
    for heads, s, sink, v in chains:
        m = jnp.maximum(sink, jnp.max(s, axis=1, keepdims=True))
        p = jnp.exp(s - _lane_tile(m, tk))
        acc = _dot(p.astype(BF16), v)
        den = pltpu.roll(acc, HEAD_DIM, 1) + jnp.exp(sink - m)
        o = acc / den
        for idx, head in enumerate(heads):
            outs[head] = o[idx * tq:(idx + 1) * tq]
    for pp in range(SWA_Q_HEADS // 2):
        o_ref[0, :, pp * PAIR_W:(pp + 1) * PAIR_W] = jnp.where(
            lane < HEAD_DIM, outs[2 * pp], outs[2 * pp + 1]).astype(o_ref.dtype)


def _swa_bias(tq):
    r = np.arange(tq)[:, None]
    c = np.arange(tq + SWA_WINDOW)[None, :]
    tabs = []
    for key_offset in (0, SWA_WINDOW):
        dist = r + key_offset - c
        tabs.append(np.where((dist >= 0) & (dist < SWA_WINDOW), 0.0, NEG_INF))
    return jnp.asarray(np.stack(tabs), F32)


def _swa(q, k, v, sinks, *, batch, seq, tq=SWA_WINDOW):
    W = q.shape[1]
    tk = tq + SWA_WINDOW
    q3 = q.reshape(batch, seq, W)
    k3, v3 = (t.reshape(batch, seq, PAIR_W) for t in (k, v))
    sink_tab = jnp.broadcast_to(sinks.astype(F32)[:, None], (SWA_Q_HEADS, LANES))
    out = pl.pallas_call(
        _swa_kernel,
        grid=(batch, seq // tq),
        in_specs=[pl.BlockSpec((1, tq, W), lambda b, i: (b, i, 0)),
                  pl.BlockSpec((1, seq, PAIR_W), lambda b, i: (b, 0, 0)),
                  pl.BlockSpec((1, seq, PAIR_W), lambda b, i: (b, 0, 0)),
                  pl.BlockSpec((SWA_Q_HEADS, LANES), lambda b, i: (0, 0)),
                  pl.BlockSpec((2, tq, tk), lambda b, i: (0, 0, 0))],
        out_specs=pl.BlockSpec((1, tq, W), lambda b, i: (b, i, 0)),
        out_shape=jax.ShapeDtypeStruct((batch, seq, W), BF16),
        compiler_params=_cparams(("parallel", "parallel")),
        name="swa",
    )(q3, k3, v3, sink_tab, _swa_bias(tq))
    return out.reshape(batch * seq, W)


def _rms_normed(x, gain):
    ms = jnp.mean(x * x, axis=-1, keepdims=True)
    return (x * lax.rsqrt(ms + EPS) * gain).astype(BF16)


def _outproj_kernel(*refs, n_parts):
    parts = refs[:n_parts]
    w_ref, x_ref, g_ref, o_ref, h_ref = refs[n_parts:]
    y = x_ref[...]
    off = 0
    for p_ref in parts:
        kw = p_ref.shape[1]
        y = y + _dot(p_ref[...], w_ref[off:off + kw, :])
        off += kw
    o_ref[...] = y
    h_ref[...] = _rms_normed(y, g_ref[...])


def _outproj(parts, w, x, next_gain, tm=ROW_TILE):
    T, D = x.shape
    in_specs = [pl.BlockSpec((tm, p.shape[1]), lambda i: (i, 0)) for p in parts]
    in_specs += [pl.BlockSpec(w.shape, lambda i: (0, 0)),
                 pl.BlockSpec((tm, D), lambda i: (i, 0)),
                 pl.BlockSpec((1, D), lambda i: (0, 0))]
    row_spec = pl.BlockSpec((tm, D), lambda i: (i, 0))
    return pl.pallas_call(
        functools.partial(_outproj_kernel, n_parts=len(parts)),
        grid=(T // tm,),
        in_specs=in_specs,
        out_specs=[row_spec, row_spec],
        out_shape=[jax.ShapeDtypeStruct((T, D), F32), jax.ShapeDtypeStruct((T, D), BF16)],
        compiler_params=_cparams(("parallel",)),
        name="outproj",
    )(*parts, w, x, next_gain.reshape(1, D).astype(F32))


_CAND_ROWS = 80


def _cand_tables(lanes):
    pos = np.zeros((_CAND_ROWS,), np.float32)
    neg = np.zeros((_CAND_ROWS,), np.float32)
    r = 0
    for a, nb in ((0, 16), (1, 8), (2, 8), (3, 8), (4, 8), (5, 8), (6, 8), (7, 8)):
        for b in range(nb):
            pos[r] = a * PEER_TOPK + b
            neg[r] = 0.0 if (a + 1) * (b + 1) <= PEER_TOPK else -np.inf
            r += 1
    for a in range(8, 16):
        pos[r] = a * PEER_TOPK
        r += 1
    assert r == _CAND_ROWS
    tab = lambda t: jnp.asarray(np.broadcast_to(t[:, None], (_CAND_ROWS, lanes)).copy())
    return tab(pos), tab(neg)


def _batcher_pairs(n):
    pairs, p = [], 1
    while p < n:
        k = p
        while k >= 1:
            for j in range(k % p, n - k, 2 * k):
                for i in range(min(k, n - j - k)):
                    if (i + j) // (2 * p) == (i + j + k) // (2 * p):
                        pairs.append((i + j, i + j + k))
            k //= 2
        p *= 2
    return pairs


_SORT16 = _batcher_pairs(PEER_TOPK)
_SUBLANES = 8
_N_CAND_PIECES = _CAND_ROWS // _SUBLANES
_SORT10 = [(i, j) for i, j in _SORT16 if j < _N_CAND_PIECES]


def _compare_exchange(items, i, j):
    items[i], items[j] = jnp.maximum(items[i], items[j]), jnp.minimum(items[i], items[j])


def _top_sorted(pieces, pairs):
    items = list(pieces)
    for i, j in pairs:
        _compare_exchange(items, i, j)
    n = PEER_TOPK
    items += [jnp.full(items[0].shape, -jnp.inf, F32)] * (n - len(items))
    for shift in (4, 2, 1):
        items = [jnp.maximum(items[i], pltpu.roll(items[n - 1 - i], shift, 0)) for i in range(n)]
        d = n // 2
        while d >= 1:
            for i in range(n):
                if i & d == 0:
                    _compare_exchange(items, i, i + d)
            d //= 2
    return items


def _pieces(x):
    return [x[_SUBLANES * g:_SUBLANES * (g + 1)] for g in range(x.shape[0] // _SUBLANES)]


def _sublane_total(x):
    for shift in (4, 2, 1):
        x = x + pltpu.roll(x, shift, 0)
    return x


def _count_ge(pieces, thr):
    total = jnp.zeros(thr.shape, F32)
    for p in pieces:
        total = total + jnp.where(p >= thr, 1.0, 0.0)
    return _sublane_total(total)


def _route_head_fast(s1, s2):
    p1, p2 = _pieces(s1), _pieces(s2)
    v1 = _top_sorted(p1, _SORT16)
    v2 = _top_sorted(p2, _SORT16)
    sub = lax.broadcasted_iota(jnp.int32, v1[0].shape, 0)

    def spread(vals):
        out = vals[0]
        for r in range(1, _SUBLANES):
            out = jnp.where(sub == r, vals[r], out)
        return out

    v2_lo, v2_hi, v1_hi = spread(v2[:8]), spread(v2[8:]), spread(v1[8:])
    cands = [v1[0] + v2_lo, v1[0] + v2_hi, v1[1] + v2_lo]
    for a in range(2, 8):
        cands.append(jnp.where(sub < PEER_TOPK // (a + 1), v1[a] + v2_lo, -jnp.inf))
    cands.append(v1_hi + v2[0])
    ts = _top_sorted(cands, _SORT10)
    tau = ts[PEER_TOPK - 1]
    z = jnp.exp(ts[0] - ts[0])
    for kk in range(1, PEER_TOPK):
        z = z + jnp.exp(ts[kk] - ts[0])

    tied = (_count_ge(p1, v1[-1]) != float(PEER_TOPK)) | (_count_ge(p2, v2[-1]) != float(PEER_TOPK))
    tied = tied | (_count_ge(cands, tau) != float(PEER_TOPK))
    for b in range(PEER_TOPK - 1):
        tied = tied | (v1[b] == v1[b + 1]) | (v2[b] == v2[b + 1])

    cnt = []
    for a in range(PEER_TOPK):
        c = jnp.zeros(tau.shape, F32)
        for b in range(PEER_TOPK // (a + 1)):
            c = c + jnp.where(v1[a] + v2[b] >= tau, 1.0, 0.0)
        cnt.append(c)
    c1, r2 = [], []
    for x in p1:
        c = jnp.zeros(x.shape, F32)
        for a in range(PEER_TOPK):
            c = jnp.where(x == v1[a], cnt[a], c)
        c1.append(c)
    for x in p2:
        r = jnp.zeros(x.shape, F32)
        for b in range(PEER_TOPK):
            r = r + jnp.where(v2[b] > x, 1.0, 0.0)
        r2.append(r)
    inv_z = 1.0 / z
    e1 = [jnp.exp(x - v1[0]) * inv_z for x in p1]
    e2 = [jnp.exp(x - v2[0]) for x in p2]
    cat = lambda ps: jnp.concatenate(ps, axis=0)
    return (cat(c1), cat(e1), cat(r2), cat(e2)), tied


def _extract_sorted(scores, by_key):
    nk, lanes = scores[0].shape
    kio = lax.broadcasted_iota(jnp.int32, (nk, lanes), 0).astype(F32)
    slot = lax.broadcasted_iota(jnp.int32, (PEER_TOPK, lanes), 0)

    def body(a, carry):
        here = slot == a
        out = []
        for (v, vals, aux), ranked in zip(carry, by_key):
            m = jnp.max(v, axis=0, keepdims=True)
            idx = jnp.min(jnp.where(v == m, kio, float(nk)), axis=0, keepdims=True)
            hit = kio == idx
            aux = jnp.where(hit, jnp.asarray(a, F32), aux) if ranked else jnp.where(here, idx, aux)
            out.append((jnp.where(hit, -jnp.inf, v), jnp.where(here, m, vals), aux))
        return tuple(out)

    small = jnp.zeros((PEER_TOPK, lanes), F32)
    unranked = jnp.full((nk, lanes), float(PEER_TOPK), F32)
    init = tuple((v, small, unranked if ranked else small) for v, ranked in zip(scores, by_key))
    return [(vals, aux) for _, vals, aux in lax.fori_loop(0, PEER_TOPK, body, init)]


def _route_head_exact(s1, s2, pos, neg):
    lanes = s1.shape[1]
    slot = lax.broadcasted_iota(jnp.int32, (PEER_TOPK, lanes), 0)
    kio = lax.broadcasted_iota(jnp.int32, (PEER_N_KEYS, lanes), 0).astype(F32)
    (v1, idx1), (v2, rank2) = _extract_sorted([s1, s2], [False, True])
    blocks = [v1[0:1] + v2[0:8], v1[0:1] + v2[8:16]]
    blocks += [v1[a:a + 1] + v2[0:8] for a in range(1, 8)]
    blocks += [v1[8:16] + v2[0:1]]
    cand = jnp.concatenate(blocks, axis=0) + neg

    def pick(kk, carry):
        cand, chosen, ts = carry
        m = jnp.max(cand, axis=0, keepdims=True)
        first = jnp.min(jnp.where(cand == m, pos, 1e9), axis=0, keepdims=True)
        hit = pos == first
        return (jnp.where(hit, -jnp.inf, cand), jnp.where(hit, 1.0, chosen), jnp.where(slot == kk, m, ts))

    _, chosen, ts = lax.fori_loop(0, PEER_TOPK, pick,
                                  (cand, jnp.zeros_like(cand), jnp.zeros((PEER_TOPK, lanes), F32)))
    z = jnp.sum(jnp.exp(ts - ts[0:1]), axis=0, keepdims=True)
    counts = [jnp.sum(chosen[0:16], axis=0, keepdims=True)]
    counts += [jnp.sum(chosen[8 * a + 8:8 * a + 16], axis=0, keepdims=True) for a in range(1, 8)]
    counts += [chosen[72 + a:73 + a] for a in range(8)]
    c1 = jnp.zeros((PEER_N_KEYS, lanes), F32)
    for a in range(PEER_TOPK):
        c1 = jnp.where(kio == idx1[a:a + 1], counts[a], c1)
    return c1, jnp.exp(s1 - v1[0:1]) / z, rank2, jnp.exp(s2 - v2[0:1])


def _route_kernel(h_ref, wq_ref, keys_ref, pos_ref, neg_ref,
                  c1_ref, e1_ref, r2_ref, e2_ref, qt_ref, sc_ref):
    half = PEER_QUERY_DIM // 2
    qt_ref[...] = _dot_nt(wq_ref[...], h_ref[...]).astype(BF16)

    def store(h, maps):
        c1, e1, r2, e2 = maps
        c1_ref[h] = c1
        e1_ref[h] = e1
        r2_ref[h] = r2.astype(BF16)
        e2_ref[h] = e2.astype(BF16)

    def head_body(h, _):
        r0 = pl.multiple_of(h * PEER_QUERY_DIM, PEER_QUERY_DIM)
        sc_ref[0] = _dot(keys_ref[2 * h], qt_ref[pl.ds(r0, half), :])
        sc_ref[1] = _dot(keys_ref[2 * h + 1], qt_ref[pl.ds(r0 + half, half), :])
        maps, tied = _route_head_fast(sc_ref[0], sc_ref[1])
        any_tied = jnp.max(jnp.where(tied, 1.0, 0.0)) > 0.0

        @pl.when(any_tied)
        def _():
            store(h, _route_head_exact(sc_ref[0], sc_ref[1], pos_ref[...], neg_ref[...]))

        @pl.when(jnp.logical_not(any_tied))
        def _():
            store(h, maps)

        return 0

    lax.fori_loop(0, PEER_HEADS, head_body, 0)


def _peer_route(h2, wq_t, keys, tt=ROUTE_TILE):
    T, D = h2.shape
    pos, neg = _cand_tables(tt)
    stat_spec = pl.BlockSpec((PEER_HEADS, PEER_N_KEYS, tt), lambda i: (0, 0, i))
    stat = lambda dt: jax.ShapeDtypeStruct((PEER_HEADS, PEER_N_KEYS, T), dt)
    return pl.pallas_call(
        _route_kernel,
        grid=(T // tt,),
        in_specs=[pl.BlockSpec((tt, D), lambda i: (i, 0)),
                  pl.BlockSpec(wq_t.shape, lambda i: (0, 0)),
                  pl.BlockSpec(keys.shape, lambda i: (0, 0, 0)),
                  pl.BlockSpec((_CAND_ROWS, tt), lambda i: (0, 0)),
                  pl.BlockSpec((_CAND_ROWS, tt), lambda i: (0, 0))],
        out_specs=[stat_spec] * 4,
        out_shape=[stat(F32), stat(F32), stat(BF16), stat(BF16)],
        scratch_shapes=[pltpu.VMEM((PEER_HEADS * PEER_QUERY_DIM, tt), BF16),
                        pltpu.VMEM((2, PEER_N_KEYS, tt), F32)],
        compiler_params=_cparams(("parallel",)),
        name="peer_route",
    )(h2, wq_t, keys, pos, neg)


_KEY_GROUP = 16
_UNITS = 4
_DRAIN_PIECES = 2


def _build_gated(a_ref, p_ref, c1_ref, e1_ref, r2_ref, e2_ref, key0, g0, ng, lt):
    rep = PEER_N_KEYS // BF16_ROWS

    def rows16(row):
        blk = jnp.broadcast_to(row, (BF16_ROWS, LANES)).astype(BF16)
        return jnp.concatenate([blk] * rep, axis=0)

    ls = slice(lt * LANES, (lt + 1) * LANES)
    w = [jnp.zeros((PEER_N_KEYS, LANES), BF16) for _ in range(ng)]
    for h in range(PEER_HEADS):
        c1 = c1_ref[h, pl.ds(key0, _KEY_GROUP), ls]
        e1 = e1_ref[h, pl.ds(key0, _KEY_GROUP), ls]
        r2 = r2_ref[h, :, ls]
        e2 = e2_ref[h, :, ls]
        for g in range(ng):
            thr = rows16(c1[g0 + g:g0 + g + 1])
            gate = rows16(e1[g0 + g:g0 + g + 1])
            w[g] = w[g] + jnp.where(r2 < thr, e2, jnp.zeros_like(e2)) * gate
    for g in range(g0, g0 + ng):
        rs = slice(g * PEER_N_KEYS, (g + 1) * PEER_N_KEYS)
        a = a_ref[rs, ls]
        gelu = 0.5 * a * (1.0 + lax.erf(a * (2.0 ** -0.5)))
        p_ref[rs, ls] = gelu.astype(BF16) * w[g - g0]


def _experts_kernel(h_ref, dn_ref, upt_ref, c1_ref, e1_ref, r2_ref, e2_ref, x_ref, *rest, emit_norm):
    if emit_norm:
        g_ref, o_ref, hn_ref, a_ref, p_ref, acc_ref = rest
    else:
        o_ref, a_ref, p_ref, acc_ref = rest
    te, tt = a_ref.shape
    d_model = acc_ref.shape[0]
    e = pl.program_id(1)
    n_tiles = pl.num_programs(1) - 1
    cur = e % 2
    ng = _KEY_GROUP // _UNITS
    n_lane = tt // LANES
    key0 = pl.multiple_of(jnp.minimum(e, n_tiles - 1) * _KEY_GROUP, _KEY_GROUP)

    def front_mm(u):
        rows = te // _UNITS
        rs = slice(u * rows, (u + 1) * rows)
        a_ref[rs, :] = _dot_nt(dn_ref[rs, :], h_ref[...])

    def back_mm(r):
        rows = d_model // _DRAIN_PIECES
        rs = slice(r * rows, (r + 1) * rows)
        acc_ref[rs, :] += _dot(upt_ref[rs, :], p_ref[1 - cur])

    units_per_drain = _UNITS // _DRAIN_PIECES

    def run(front, back):
        if front:
            front_mm(0)
        for u in range(_UNITS):
            for lt in range(n_lane):
                if front:
                    _build_gated(a_ref, p_ref.at[cur], c1_ref, e1_ref, r2_ref, e2_ref, key0, u * ng, ng, lt)
                if front and lt == 0 and u + 1 < _UNITS:
                    front_mm(u + 1)
                if back and lt == n_lane // 2 and (u + 1) % units_per_drain == 0:
                    back_mm(u // units_per_drain)

    @pl.when(e == 0)
    def _():
        acc_ref[...] = jnp.zeros_like(acc_ref)
        run(True, False)

    @pl.when((e > 0) & (e < n_tiles))
    def _():
        run(True, True)

    @pl.when(e == n_tiles)
    def _():
        run(False, True)
        y = x_ref[...] + acc_ref[...].T
        o_ref[...] = y
        if emit_norm:
            hn_ref[...] = _rms_normed(y, g_ref[...])


def _peer_experts(h2, down, up_t, stats, x, next_gain, tt=EXPERT_TOKEN_TILE):
    T, D = h2.shape
    E = down.shape[0]
    te = _KEY_GROUP * PEER_N_KEYS
    n_tiles = E // te
    emit_norm = next_gain is not None
    stat_spec = pl.BlockSpec((PEER_HEADS, PEER_N_KEYS, tt), lambda i, e: (0, 0, i))
    row_spec = pl.BlockSpec((tt, D), lambda i, e: (i, 0))
    in_specs = [row_spec,
                pl.BlockSpec((te, D), lambda i, e: (jnp.minimum(e, n_tiles - 1), 0)),
                pl.BlockSpec((None, D, te), lambda i, e: (jnp.maximum(e - 1, 0), 0, 0)),
                stat_spec, stat_spec, stat_spec, stat_spec,
                row_spec]
    args = [h2, down, up_t, *stats, x]
    out_specs = [row_spec]
    out_shape = [jax.ShapeDtypeStruct((T, D), F32)]
    if emit_norm:
        in_specs.append(pl.BlockSpec((1, D), lambda i, e: (0, 0)))
        args.append(next_gain.reshape(1, D).astype(F32))
        out_specs.append(row_spec)
        out_shape.append(jax.ShapeDtypeStruct((T, D), BF16))
    stat_bytes = 2 * PEER_HEADS * PEER_N_KEYS * tt * (4 + 2)
    block_bytes = (tt * D * 2 + 2 * te * D * 2 + stat_bytes + tt * D * 4
                   + tt * D * 4 + (tt * D * 2 + D * 4 if emit_norm else 0))
    scratch_bytes = te * tt * 4 + 2 * te * tt * 2 + D * tt * 4
    res = pl.pallas_call(
        functools.partial(_experts_kernel, emit_norm=emit_norm),
        grid=(T // tt, n_tiles + 1),
        in_specs=in_specs, out_specs=out_specs, out_shape=out_shape,
        scratch_shapes=[pltpu.VMEM((te, tt), F32),
                        pltpu.VMEM((2, te, tt), BF16),
                        pltpu.VMEM((D, tt), F32)],
        compiler_params=_cparams(("parallel", "arbitrary"), 2 * block_bytes + scratch_bytes + VMEM_SPILL_MARGIN),
        name="peer_experts",
    )(*args)
    return (res[0], res[1]) if emit_norm else (res[0], None)


def _peer_layer(x, h2, next_gain, w_query, sub_keys, down, up):
    keys = sub_keys.reshape(PEER_HEADS * 2, PEER_N_KEYS, PEER_QUERY_DIM // 2).astype(BF16)
    stats = _peer_route(h2, w_query.T.astype(BF16), keys)
    te = _KEY_GROUP * PEER_N_KEYS
    up_t = up.reshape(up.shape[0] // te, te, up.shape[1]).transpose(0, 2, 1).astype(BF16)
    return _peer_experts(h2, down.astype(BF16), up_t, stats, x, next_gain)


def _tile_heads(g, n):
    return jnp.tile(g.astype(F32), n)


def _even_mixer(x, h, ffn_gain, w_in, f_bias, qn_a, kn_a, qn_b, kn_b, w_out, tabs, *, batch, seq):
    w = w_in.astype(BF16)
    o_qa, o_ka, o_va, o_qb, o_kb, o_vb, o_gb, o_fb = (
        0, A_W, 2 * A_W, 3 * A_W, 3 * A_W + B_W, 3 * A_W + 2 * B_W, 3 * A_W + 3 * B_W, 3 * A_W + 4 * B_W)
    gain_a = jnp.concatenate([_tile_heads(qn_a, MOBA_HEADS), _tile_heads(kn_a, MOBA_HEADS)])
    qk_a, km = _proj(h, w[:, o_qa:o_va], seq=seq, tn=A_W, gain=gain_a, rope_tabs=tabs, kmean=True)
    nb = seq // MOBA_BLOCK
    kmean = km.reshape(batch, nb, 2 * A_W)[:, :, A_W:]
    kmean = jnp.pad(kmean, ((0, 0), (0, LANES - nb), (0, 0))).astype(BF16)
    gain_b = jnp.concatenate([_tile_heads(qn_b, FOX_HEADS), _tile_heads(kn_b, FOX_HEADS)])
    qk_b = _proj(h, w[:, o_qb:o_vb], seq=seq, tn=B_W, gain=gain_b)
    w_plain = jnp.concatenate([w[:, o_va:o_qb], w[:, o_vb:o_fb]], axis=1)
    vvg = _proj(h, w_plain, seq=seq, tn=A_W)
    va, vb, gb = vvg[:, :A_W], vvg[:, A_W:A_W + B_W], vvg[:, A_W + B_W:]
    frow = _fox_gates(h, w[:, o_fb:].T, f_bias, batch=batch, seq=seq)
    oa = _moba(qk_a[:, :A_W], qk_a[:, A_W:], va, kmean, batch=batch, seq=seq)
    logit_bound = 1.01 * ATTN_SCALE * HEAD_DIM * jnp.max(jnp.abs(qn_b)) * jnp.max(jnp.abs(kn_b)) + 0.01
    ob = _fox(qk_b[:, :B_W], qk_b[:, B_W:], vb, gb, frow, logit_bound, batch=batch, seq=seq)
    return _outproj([oa, ob], w_out.astype(BF16), x, ffn_gain)


def _odd_mixer(x, h, ffn_gain, w_in, qn, kn, sinks, w_out, tabs, *, batch, seq):
    w = w_in.astype(BF16)
    qw = SWA_Q_HEADS * HEAD_DIM
    kw = SWA_KV_HEADS * HEAD_DIM
    q = _proj(h, w[:, :qw], seq=seq, tn=512, gain=_tile_heads(qn, SWA_Q_HEADS), rope_tabs=tabs)
    k = _proj(h, w[:, qw:qw + kw], seq=seq, tn=kw, gain=_tile_heads(kn, SWA_KV_HEADS), rope_tabs=tabs)
    v = _proj(h, w[:, qw + kw:], seq=seq, tn=kw)
    o = _swa(q, k, v, sinks, batch=batch, seq=seq)
    return _outproj([o], w_out.astype(BF16), x, ffn_gain)


def kernel(x, attn_norm, ffn_norm, ev_w_in, ev_forget_bias, ev_q_norm_a, ev_k_norm_a, ev_q_norm_b,
           ev_k_norm_b, ev_w_out, od_w_in, od_q_norm, od_k_norm, od_sinks, od_w_out,
           peer_w_query, peer_sub_keys, peer_down, peer_up):
    batch, seq, d_model = x.shape
    depth = attn_norm.shape[0]
    tabs = _rope_tables(seq)
    xt = x.reshape(batch * seq, d_model)
    h = _rmsnorm(xt, attn_norm[0])
    for l in range(depth):
        i = l // 2
        if l % 2 == 0:
            xt, h2 = _even_mixer(xt, h, ffn_norm[l], ev_w_in[i], ev_forget_bias[i], ev_q_norm_a[i],
                                 ev_k_norm_a[i], ev_q_norm_b[i], ev_k_norm_b[i], ev_w_out[i], tabs,
                                 batch=batch, seq=seq)
        else:
            xt, h2 = _odd_mixer(xt, h, ffn_norm[l], od_w_in[i], od_q_norm[i], od_k_norm[i], od_sinks[i],
                                od_w_out[i], tabs, batch=batch, seq=seq)
        next_gain = attn_norm[l + 1] if l + 1 < depth else None
        xt, h = _peer_layer(xt, h2, next_gain, peer_w_query[l], peer_sub_keys[l], peer_down[l], peer_up[l])
    return xt.reshape(batch, seq, d_model)
```

```python
import functools

import numpy as np
import jax
import jax.numpy as jnp
from jax import lax
from jax.experimental import pallas as pl
from jax.experimental.pallas import tpu as pltpu

F32 = jnp.float32
BF16 = jnp.bfloat16

HEAD_DIM = 64
ROT_DIM = HEAD_DIM // 4
ROPE_THETA = 500000.0
ATTN_SCALE = HEAD_DIM ** -0.5
EPS = 1e-6
NEG_INF = -1e30

MOBA_HEADS = 8
FOX_HEADS = 8
MOBA_BLOCK = 256
MOBA_TOPK = 3
A_W = MOBA_HEADS * HEAD_DIM
B_W = FOX_HEADS * HEAD_DIM

SWA_Q_HEADS = 16
SWA_KV_HEADS = 2
SWA_WINDOW = 128

PEER_HEADS = 8
PEER_N_KEYS = 128
PEER_TOPK = 16
PEER_QUERY_DIM = 128

LANES = 128
BF16_ROWS = 16
PAIR_W = 2 * HEAD_DIM

ROW_TILE = 512
FLASH_TILE = 2 * MOBA_BLOCK
ROUTE_TILE = 4 * LANES
EXPERT_TOKEN_TILE = 512
VMEM_LIMIT = 48 * 1024 * 1024
VMEM_SPILL_MARGIN = 4 * 1024 * 1024


def _cparams(sem, vmem_limit=VMEM_LIMIT):
    return pltpu.CompilerParams(dimension_semantics=sem, vmem_limit_bytes=vmem_limit)


def _dot_nt(a, b):
    return lax.dot_general(a, b, (((1,), (1,)), ((), ())), preferred_element_type=F32)


def _dot(a, b):
    return jnp.dot(a, b, preferred_element_type=F32)


def _split2(x):
    h1 = x.astype(BF16)
    return h1, (x - h1.astype(F32)).astype(BF16)


def _split3(x):
    h1 = x.astype(BF16)
    r1 = x - h1.astype(F32)
    h2 = r1.astype(BF16)
    h3 = (r1 - h2.astype(F32)).astype(BF16)
    return h1, h2, h3


def _rmsnorm_kernel(x_ref, g_ref, o_ref):
    x = x_ref[...]
    ms = jnp.mean(x * x, axis=-1, keepdims=True)
    o_ref[...] = (x * lax.rsqrt(ms + EPS) * g_ref[...]).astype(o_ref.dtype)


def _rmsnorm(x, gain, tm=ROW_TILE):
    T, D = x.shape
    return pl.pallas_call(
        _rmsnorm_kernel,
        grid=(T // tm,),
        in_specs=[pl.BlockSpec((tm, D), lambda i: (i, 0)),
                  pl.BlockSpec((1, D), lambda i: (0, 0))],
        out_specs=pl.BlockSpec((tm, D), lambda i: (i, 0)),
        out_shape=jax.ShapeDtypeStruct((T, D), BF16),
        compiler_params=_cparams(("parallel",)),
        name="rmsnorm",
    )(x, gain.reshape(1, D))


def _proj_kernel(*refs, norm, rope, kmean, tn):
    it = iter(refs)
    h_ref, w_ref = next(it), next(it)
    gain_ref = next(it) if norm else None
    bd_ref = next(it) if norm else None
    if rope:
        c_ref, sa_ref, sb_ref = next(it), next(it), next(it)
    o_ref = next(it)
    km_ref = next(it) if kmean else None

    y = _dot(h_ref[...], w_ref[...])
    if norm:
        y2 = y * y
        bd = bd_ref[...]
        cols = []
        tm = y.shape[0]
        for c in range(tn // LANES):
            h1, h2 = _split2(y2[:, c * LANES:(c + 1) * LANES])
            t = _dot(jnp.concatenate([h1, h2], axis=0), bd)
            cols.append(t[:tm] + t[tm:])
        ms = cols[0] if len(cols) == 1 else jnp.concatenate(cols, axis=1)
        y = y * lax.rsqrt(ms + EPS) * gain_ref[...]
    if rope:
        rep = tn // LANES
        tile = (lambda t: t) if rep == 1 else (lambda t: jnp.concatenate([t] * rep, axis=1))
        y = (y * tile(c_ref[...])
             + pltpu.roll(y, tn - ROT_DIM // 2, 1) * tile(sa_ref[...])
             + pltpu.roll(y, ROT_DIM // 2, 1) * tile(sb_ref[...]))
    o_ref[...] = y.astype(o_ref.dtype)
    if kmean:
        for r in range(km_ref.shape[0]):
            km_ref[r] = jnp.mean(y[r * MOBA_BLOCK:(r + 1) * MOBA_BLOCK], axis=0, keepdims=True)


def _proj(h, w, *, seq, tn, tm=ROW_TILE, gain=None, rope_tabs=None, kmean=False):
    T, D = h.shape
    N = w.shape[1]
    norm = gain is not None
    rope = rope_tabs is not None
    nseq = seq // tm
    in_specs = [pl.BlockSpec((tm, D), lambda i, j: (i, 0)),
                pl.BlockSpec((D, tn), lambda i, j: (0, j))]
    args = [h, w]
    if norm:
        bd = np.kron(np.eye(LANES // HEAD_DIM), np.ones((HEAD_DIM, HEAD_DIM))) / HEAD_DIM
        in_specs += [pl.BlockSpec((1, tn), lambda i, j: (0, j)),
                     pl.BlockSpec((LANES, LANES), lambda i, j: (0, 0))]
        args += [gain.reshape(1, N).astype(F32), jnp.asarray(bd, BF16)]
    if rope:
        in_specs += [pl.BlockSpec((tm, LANES), lambda i, j: (i % nseq, 0))] * 3
        args += list(rope_tabs)
    out_specs = [pl.BlockSpec((tm, tn), lambda i, j: (i, j))]
    out_shape = [jax.ShapeDtypeStruct((T, N), BF16)]
    if kmean:
        out_specs.append(pl.BlockSpec((tm // MOBA_BLOCK, 1, tn), lambda i, j: (i, 0, j)))
        out_shape.append(jax.ShapeDtypeStruct((T // MOBA_BLOCK, 1, N), F32))
    res = pl.pallas_call(
        functools.partial(_proj_kernel, norm=norm, rope=rope, kmean=kmean, tn=tn),
        grid=(T // tm, N // tn),
        in_specs=in_specs, out_specs=out_specs, out_shape=out_shape,
        compiler_params=_cparams(("parallel", "parallel")),
        name="proj",
    )(*args)
    return res if kmean else res[0]


def _rope_tables(seq):
    half = ROT_DIM // 2
    inv_freq = jnp.power(ROPE_THETA, -jnp.arange(0, ROT_DIM, 2, dtype=F32) / ROT_DIM)
    ang = jnp.arange(seq, dtype=F32)[:, None] * inv_freq[None, :]
    cos, sin = jnp.cos(ang), jnp.sin(ang)
    one = jnp.ones((seq, HEAD_DIM - ROT_DIM), F32)
    zero = jnp.zeros((seq, HEAD_DIM - ROT_DIM), F32)
    z8 = jnp.zeros((seq, half), F32)
    c = jnp.concatenate([cos, cos, one], axis=1)
    sa = jnp.concatenate([-sin, z8, zero], axis=1)
    sb = jnp.concatenate([z8, sin, zero], axis=1)
    rep = LANES // HEAD_DIM
    return tuple(jnp.concatenate([t] * rep, axis=1) for t in (c, sa, sb))


def _gates_kernel(h_ref, wf_ref, b_ref, tri_ref, o_ref, carry_ref):
    @pl.when(pl.program_id(1) == 0)
    def _():
        carry_ref[...] = jnp.zeros_like(carry_ref)

    z = _dot_nt(wf_ref[...], h_ref[...]) + b_ref[...][:, :1]
    lf = jnp.minimum(z, 0.0) - jnp.log1p(jnp.exp(-jnp.abs(z)))
    tri = tri_ref[...]
    h1, h2, h3 = _split3(lf)
    cs = _dot(h1, tri) + _dot(h2, tri) + _dot(h3, tri) + carry_ref[...][:, :1]
    o_ref[0] = cs
    carry_ref[...] = jnp.broadcast_to(cs[:, -1:], carry_ref.shape)


def _fox_gates(h, wf_t, bias, *, batch, seq, tm=ROW_TILE):
    T, D = h.shape
    nh = wf_t.shape[0]
    nseq = seq // tm
    tri = jnp.asarray(np.triu(np.ones((tm, tm))), BF16)
    return pl.pallas_call(
        _gates_kernel,
        grid=(batch, nseq),
        in_specs=[pl.BlockSpec((tm, D), lambda b, s: (b * nseq + s, 0)),
                  pl.BlockSpec((nh, D), lambda b, s: (0, 0)),
                  pl.BlockSpec((nh, LANES), lambda b, s: (0, 0)),
                  pl.BlockSpec((tm, tm), lambda b, s: (0, 0))],
        out_specs=pl.BlockSpec((1, nh, tm), lambda b, s: (b, 0, s)),
        out_shape=jax.ShapeDtypeStruct((batch, nh, seq), F32),
        scratch_shapes=[pltpu.VMEM((nh, LANES), F32)],
        compiler_params=_cparams(("parallel", "arbitrary")),
        name="fox_gates",
    )(h, wf_t, jnp.broadcast_to(bias.astype(F32)[:, None], (nh, LANES)), tri)


def _lane_tile(x, width):
    rep = width // LANES
    return x if rep == 1 else jnp.concatenate([x] * rep, axis=1)


def _flash_init(m_ref, acc_ref):
    m_ref[...] = jnp.full(m_ref.shape, NEG_INF, F32)
    acc_ref[...] = jnp.zeros(acc_ref.shape, F32)


def _head_values(v):
    lane = lax.broadcasted_iota(jnp.int32, v.shape, 1)
    return [jnp.where((lane >= hh * HEAD_DIM) & (lane < (hh + 1) * HEAD_DIM), v, jnp.ones_like(v))
            for hh in range(2)]


def _flash_update(slot, s, v, m_ref, acc_ref):
    tk = s.shape[1]
    m_prev = m_ref[slot]
    m_new = jnp.maximum(m_prev, jnp.max(s, axis=1, keepdims=True))
    alpha = jnp.exp(m_prev - m_new)
    p = jnp.exp(s - _lane_tile(m_new, tk))
    acc_ref[slot] = alpha * acc_ref[slot] + _dot(p.astype(BF16), v)
    m_ref[slot] = m_new


def _flash_finish(lane, acc_ref):
    outs = []
    for hh in range(2):
        acc = acc_ref[hh]
        den = (1 - hh) * HEAD_DIM
        outs.append(acc / acc[:, den:den + 1])
    return jnp.where(lane < HEAD_DIM, outs[0], outs[1])


def _flash_tiles(qi, tk, logits, values, m_ref, acc_ref):
    def absorb(s, off):
        vh = values(off)
        for hh in range(2):
            _flash_update(hh, s[hh], vh[hh], m_ref, acc_ref)

    def pair(off_a, off_b, b_diagonal):
        sa = logits(off_a, False)
        sb = logits(off_b, b_diagonal)
        absorb(sa, off_a)
        absorb(sb, off_b)

    def body(jj, carry):
        off = pl.multiple_of(2 * jj * tk, 2 * tk)
        pair(off, off + tk, False)
        return carry

    lax.fori_loop(0, qi // 2, body, 0)
    diag = pl.multiple_of(qi * tk, tk)

    @pl.when(qi % 2 == 1)
    def _():
        pair(diag - tk, diag, True)

    @pl.when(qi % 2 == 0)
    def _():
        absorb(logits(diag, True), diag)


def _head_queries(q, lane):
    qs = q * ATTN_SCALE
    return [jnp.where((lane >= hh * HEAD_DIM) & (lane < (hh + 1) * HEAD_DIM), qs, jnp.zeros_like(qs))
            for hh in range(2)]


def _moba_kernel(q_ref, k_ref, v_ref, km_ref, o_ref, m_ref, acc_ref):
    tq = q_ref.shape[1]
    tk = tq
    qi = pl.program_id(2)
    lane = lax.broadcasted_iota(jnp.int32, (tq, LANES), 1)
    lane_f = lane.astype(F32)
    rowv = lax.broadcasted_iota(jnp.int32, (tq, LANES), 0)
    row_blk = 2 * qi + (rowv >= MOBA_BLOCK).astype(jnp.int32)
    row = lax.broadcasted_iota(jnp.int32, (tq, tk), 0)
    col = lax.broadcasted_iota(jnp.int32, (tq, tk), 1)
    qh = _head_queries(q_ref[0], lane)
    _flash_init(m_ref, acc_ref)

    sels = []
    for hh in range(2):
        gate = _dot_nt(qh[hh], km_ref[0])
        gate = jnp.where(lane < row_blk, gate, -jnp.inf)
        sel = jnp.zeros((tq, LANES), F32)
        for _ in range(MOBA_TOPK):
            m = jnp.max(gate, axis=1, keepdims=True)
            idx = jnp.min(jnp.where(gate == m, lane_f, float(LANES)), axis=1, keepdims=True)
            hit = lane_f == idx
            sel = jnp.where(hit & (m > -jnp.inf), 1.0, sel)
            gate = jnp.where(hit, -jnp.inf, gate)
        sels.append(sel)

    def chosen(sel, blk):
        return jnp.max(jnp.where(lane == blk, sel, 0.0), axis=1, keepdims=True) > 0.0

    def logits(off, diagonal):
        kj = k_ref[0, pl.ds(off, tk), :]
        out = []
        for hh in range(2):
            s = _dot_nt(qh[hh], kj)
            if diagonal:
                visible = (col >= MOBA_BLOCK) | (row < MOBA_BLOCK) | chosen(sels[hh], 2 * qi)
                s = jnp.where((col <= row) & visible, s, NEG_INF)
            else:
                blk = 2 * (off // tk)
                s = jnp.concatenate(
                    [jnp.where(chosen(sels[hh], blk), s[:, :MOBA_BLOCK], NEG_INF),
                     jnp.where(chosen(sels[hh], blk + 1), s[:, MOBA_BLOCK:], NEG_INF)], axis=1)
            out.append(s)
        return out

    def values(off):
        return _head_values(v_ref[0, pl.ds(off, tk), :])

    _flash_tiles(qi, tk, logits, values, m_ref, acc_ref)
    o_ref[0] = _flash_finish(lane, acc_ref).astype(o_ref.dtype)


def _flash_scratch(tq):
    return [pltpu.VMEM((2, tq, LANES), F32)] * 2


def _moba(q, k, v, kmean, *, batch, seq):
    W = q.shape[1]
    tq = FLASH_TILE
    q3, k3, v3 = (t.reshape(batch, seq, W) for t in (q, k, v))
    out = pl.pallas_call(
        _moba_kernel,
        grid=(batch, W // PAIR_W, seq // tq),
        in_specs=[pl.BlockSpec((1, tq, PAIR_W), lambda b, p, i: (b, i, p)),
                  pl.BlockSpec((1, seq, PAIR_W), lambda b, p, i: (b, 0, p)),
                  pl.BlockSpec((1, seq, PAIR_W), lambda b, p, i: (b, 0, p)),
                  pl.BlockSpec((1, LANES, PAIR_W), lambda b, p, i: (b, 0, p))],
        out_specs=pl.BlockSpec((1, tq, PAIR_W), lambda b, p, i: (b, i, p)),
        out_shape=jax.ShapeDtypeStruct((batch, seq, W), BF16),
        scratch_shapes=_flash_scratch(tq),
        compiler_params=_cparams(("parallel", "parallel", "parallel")),
        name="moba",
    )(q3, k3, v3, kmean)
    return out.reshape(batch * seq, W)


_EXP_UNDERFLOW = 112.0


def _fox_kernel(q_ref, k_ref, v_ref, g_ref, frow_ref, far_ref, o_ref, m_ref, acc_ref):
    tq = q_ref.shape[1]
    tk = tq
    pr = pl.program_id(1)
    qi = pl.program_id(2)
    lane = lax.broadcasted_iota(jnp.int32, (tq, LANES), 1)
    row = lax.broadcasted_iota(jnp.int32, (tq, tk), 0)
    col = lax.broadcasted_iota(jnp.int32, (tq, tk), 1)
    qh = _head_queries(q_ref[0], lane)
    _flash_init(m_ref, acc_ref)

    def key_gates(off, width):
        f_all = frow_ref[0, :, pl.ds(off, width)]
        sub = lax.broadcasted_iota(jnp.int32, f_all.shape, 0)
        return [jnp.sum(jnp.where(sub == 2 * pr + hh, f_all, 0.0), axis=0, keepdims=True) for hh in range(2)]

    f_ref = [f[:, :1] for f in key_gates(pl.multiple_of(qi * tq, tq), LANES)]

    def logits(off, diagonal):
        kj = k_ref[0, pl.ds(off, tk), :]
        fk = key_gates(off, tk)
        out = []
        for hh in range(2):
            s = _dot_nt(qh[hh], kj) - (fk[hh] - f_ref[hh])
            out.append(jnp.where(col <= row, s, NEG_INF) if diagonal else s)
        return out

    def absorb(s, off):
        vh = _head_values(v_ref[0, pl.ds(off, tk), :])
        for hh in range(2):
            _flash_update(hh, s[hh], vh[hh], m_ref, acc_ref)

    def vanishes(j):
        tail = frow_ref[0, :, pl.ds(pl.multiple_of((j + 1) * tk - LANES, LANES), LANES)]
        sub = lax.broadcasted_iota(jnp.int32, tail.shape, 0)
        last = lax.broadcasted_iota(jnp.int32, tail.shape, 1) == LANES - 1
        gap = tail - jnp.where(sub == 2 * pr, f_ref[0], f_ref[1]) - far_ref[...][:, :1]
        mine = last & ((sub == 2 * pr) | (sub == 2 * pr + 1))
        return jnp.min(jnp.where(mine, gap, jnp.inf)) >= 0.0

    absorb(logits(pl.multiple_of(qi * tk, tk), True), pl.multiple_of(qi * tk, tk))

    def more(j):
        return jnp.logical_and(j >= 1, jnp.logical_not(vanishes(jnp.maximum(j, 0))))

    def pair(j):
        off_a = pl.multiple_of(j * tk, tk)
        off_b = pl.multiple_of((j - 1) * tk, tk)
        sa = logits(off_a, False)
        sb = logits(off_b, False)
        absorb(sa, off_a)
        absorb(sb, off_b)
        return j - 2

    j = lax.while_loop(more, pair, qi - 1)

    @pl.when(jnp.logical_and(j == 0, jnp.logical_not(vanishes(0))))
    def _():
        absorb(logits(0, False), 0)

    o = _flash_finish(lane, acc_ref)
    o_ref[0] = (o * jax.nn.sigmoid(g_ref[0].astype(F32))).astype(o_ref.dtype)


def _fox(q, k, v, g, frow, logit_bound, *, batch, seq, tq=FLASH_TILE):
    far = jnp.full((1, LANES), 2.0 * logit_bound + _EXP_UNDERFLOW, F32)
    W = q.shape[1]
    nh = frow.shape[1]
    q3, k3, v3, g3 = (t.reshape(batch, seq, W) for t in (q, k, v, g))
    out = pl.pallas_call(
        _fox_kernel,
        grid=(batch, W // PAIR_W, seq // tq),
        in_specs=[pl.BlockSpec((1, tq, PAIR_W), lambda b, p, i: (b, i, p)),
                  pl.BlockSpec((1, seq, PAIR_W), lambda b, p, i: (b, 0, p)),
                  pl.BlockSpec((1, seq, PAIR_W), lambda b, p, i: (b, 0, p)),
                  pl.BlockSpec((1, tq, PAIR_W), lambda b, p, i: (b, i, p)),
                  pl.BlockSpec((1, nh, seq), lambda b, p, i: (b, 0, 0)),
                  pl.BlockSpec((1, LANES), lambda b, p, i: (0, 0))],
        out_specs=pl.BlockSpec((1, tq, PAIR_W), lambda b, p, i: (b, i, p)),
        out_shape=jax.ShapeDtypeStruct((batch, seq, W), BF16),
        scratch_shapes=_flash_scratch(tq),
        compiler_params=_cparams(("parallel", "parallel", "parallel")),
        name="fox",
    )(q3, k3, v3, g3, frow, far)
    return out.reshape(batch * seq, W)


def _swa_kernel(q_ref, k_ref, v_ref, sink_ref, bias_ref, o_ref):
    tq = q_ref.shape[1]
    qi = pl.program_id(1)
    group = SWA_Q_HEADS // SWA_KV_HEADS
    tk = tq + SWA_WINDOW
    lane = lax.broadcasted_iota(jnp.int32, (tq, LANES), 1)
    kstart = pl.multiple_of(jnp.maximum(qi * tq - SWA_WINDOW, 0), SWA_WINDOW)
    k = k_ref[0, pl.ds(kstart, tk), :]
    vh = _head_values(v_ref[0, pl.ds(kstart, tk), :])
    swap = lambda t: pltpu.roll(t.astype(F32), HEAD_DIM, 1).astype(BF16)
    k_by_half = [k, swap(k)]
    bias = bias_ref[jnp.minimum(qi, 1)]
    bias = jnp.concatenate([bias] * (group // 2), axis=0)
    sink_tab = sink_ref[...]

    chains = []
    for c in range(SWA_KV_HEADS):
        v_by_half = [vh[c], swap(vh[c])]
        for swapped in range(2):
            hh = c if not swapped else 1 - c
            heads = [h for h in range(c * group, (c + 1) * group) if h % 2 == hh]
            pieces, sinks = [], []
            for head in heads:
                blk = q_ref[0, :, (head // 2) * PAIR_W:(head // 2 + 1) * PAIR_W] * ATTN_SCALE
                pieces.append(jnp.where((lane >= hh * HEAD_DIM) & (lane < (hh + 1) * HEAD_DIM),
                                        blk, jnp.zeros_like(blk)))
                sinks.append(jnp.broadcast_to(sink_tab[head:head + 1, :], (tq, LANES)))
            s = _dot_nt(jnp.concatenate(pieces, axis=0), k_by_half[swapped]) + bias
            chains.append((heads, s, jnp.concatenate(sinks, axis=0), v_by_half[swapped]))

    outs = [None] * SWA_Q_HEADS
    for heads, s, sink, v in chains:
        m = jnp.maximum(sink, jnp.max(s, axis=1, keepdims=True))
        p = jnp.exp(s - _lane_tile(m, tk))
        acc = _dot(p.astype(BF16), v)
        den = pltpu.roll(acc, HEAD_DIM, 1) + jnp.exp(sink - m)
        o = acc / den
        for idx, head in enumerate(heads):
            outs[head] = o[idx * tq:(idx + 1) * tq]
    for pp in range(SWA_Q_HEADS // 2):
        o_ref[0, :, pp * PAIR_W:(pp + 1) * PAIR_W] = jnp.where(
            lane < HEAD_DIM, outs[2 * pp], outs[2 * pp + 1]).astype(o_ref.dtype)


def _swa_bias(tq):
    r = np.arange(tq)[:, None]
    c = np.arange(tq + SWA_WINDOW)[None, :]
    tabs = []
    for key_offset in (0, SWA_WINDOW):
        dist = r + key_offset - c
        tabs.append(np.where((dist >= 0) & (dist < SWA_WINDOW), 0.0, NEG_INF))
    return jnp.asarray(np.stack(tabs), F32)


def _swa(q, k, v, sinks, *, batch, seq, tq=SWA_WINDOW):
    W = q.shape[1]
    tk = tq + SWA_WINDOW
    q3 = q.reshape(batch, seq, W)
    k3, v3 = (t.reshape(batch, seq, PAIR_W) for t in (k, v))
    sink_tab = jnp.broadcast_to(sinks.astype(F32)[:, None], (SWA_Q_HEADS, LANES))
    out = pl.pallas_call(
        _swa_kernel,
        grid=(batch, seq // tq),
        in_specs=[pl.BlockSpec((1, tq, W), lambda b, i: (b, i, 0)),
                  pl.BlockSpec((1, seq, PAIR_W), lambda b, i: (b, 0, 0)),
                  pl.BlockSpec((1, seq, PAIR_W), lambda b, i: (b, 0, 0)),
                  pl.BlockSpec((SWA_Q_HEADS, LANES), lambda b, i: (0, 0)),
                  pl.BlockSpec((2, tq, tk), lambda b, i: (0, 0, 0))],
        out_specs=pl.BlockSpec((1, tq, W), lambda b, i: (b, i, 0)),
        out_shape=jax.ShapeDtypeStruct((batch, seq, W), BF16),
        compiler_params=_cparams(("parallel", "parallel")),
        name="swa",
    )(q3, k3, v3, sink_tab, _swa_bias(tq))
    return out.reshape(batch * seq, W)


def _rms_normed(x, gain):
    ms = jnp.mean(x * x, axis=-1, keepdims=True)
    return (x * lax.rsqrt(ms + EPS) * gain).astype(BF16)


def _outproj_kernel(*refs, n_parts):
    parts = refs[:n_parts]
    w_ref, x_ref, g_ref, o_ref, h_ref = refs[n_parts:]
    y = x_ref[...]
    off = 0
    for p_ref in parts:
        kw = p_ref.shape[1]
        y = y + _dot(p_ref[...], w_ref[off:off + kw, :])
        off += kw
    o_ref[...] = y
    h_ref[...] = _rms_normed(y, g_ref[...])


def _outproj(parts, w, x, next_gain, tm=ROW_TILE):
    T, D = x.shape
    in_specs = [pl.BlockSpec((tm, p.shape[1]), lambda i: (i, 0)) for p in parts]
    in_specs += [pl.BlockSpec(w.shape, lambda i: (0, 0)),
                 pl.BlockSpec((tm, D), lambda i: (i, 0)),
                 pl.BlockSpec((1, D), lambda i: (0, 0))]
    row_spec = pl.BlockSpec((tm, D), lambda i: (i, 0))
    return pl.pallas_call(
        functools.partial(_outproj_kernel, n_parts=len(parts)),
        grid=(T // tm,),
        in_specs=in_specs,
        out_specs=[row_spec, row_spec],
        out_shape=[jax.ShapeDtypeStruct((T, D), F32), jax.ShapeDtypeStruct((T, D), BF16)],
        compiler_params=_cparams(("parallel",)),
        name="outproj",
    )(*parts, w, x, next_gain.reshape(1, D).astype(F32))


_CAND_ROWS = 80


def _cand_tables(lanes):
    pos = np.zeros((_CAND_ROWS,), np.float32)
    neg = np.zeros((_CAND_ROWS,), np.float32)
    r = 0
    for a, nb in ((0, 16), (1, 8), (2, 8), (3, 8), (4, 8), (5, 8), (6, 8), (7, 8)):
        for b in range(nb):
            pos[r] = a * PEER_TOPK + b
            neg[r] = 0.0 if (a + 1) * (b + 1) <= PEER_TOPK else -np.inf
            r += 1
    for a in range(8, 16):
        pos[r] = a * PEER_TOPK
        r += 1
    assert r == _CAND_ROWS
    tab = lambda t: jnp.asarray(np.broadcast_to(t[:, None], (_CAND_ROWS, lanes)).copy())
    return tab(pos), tab(neg)


def _batcher_pairs(n):
    pairs, p = [], 1
    while p < n:
        k = p
        while k >= 1:
            for j in range(k % p, n - k, 2 * k):
                for i in range(min(k, n - j - k)):
                    if (i + j) // (2 * p) == (i + j + k) // (2 * p):
                        pairs.append((i + j, i + j + k))
            k //= 2
        p *= 2
    return pairs


_SORT16 = _batcher_pairs(PEER_TOPK)
_SUBLANES = 8
_N_CAND_PIECES = _CAND_ROWS // _SUBLANES
_SORT10 = [(i, j) for i, j in _SORT16 if j < _N_CAND_PIECES]


def _compare_exchange(items, i, j):
    items[i], items[j] = jnp.maximum(items[i], items[j]), jnp.minimum(items[i], items[j])


def _top_sorted(pieces, pairs):
    items = list(pieces)
    for i, j in pairs:
        _compare_exchange(items, i, j)
    n = PEER_TOPK
    items += [jnp.full(items[0].shape, -jnp.inf, F32)] * (n - len(items))
    for shift in (4, 2, 1):
        items = [jnp.maximum(items[i], pltpu.roll(items[n - 1 - i], shift, 0)) for i in range(n)]
        d = n // 2
        while d >= 1:
            for i in range(n):
                if i & d == 0:
                    _compare_exchange(items, i, i + d)
            d //= 2
    return items


def _pieces(x):
    return [x[_SUBLANES * g:_SUBLANES * (g + 1)] for g in range(x.shape[0] // _SUBLANES)]


def _sublane_total(x):
    for shift in (4, 2, 1):
        x = x + pltpu.roll(x, shift, 0)
    return x


def _count_ge(pieces, thr):
    total = jnp.zeros(thr.shape, F32)
    for p in pieces:
        total = total + jnp.where(p >= thr, 1.0, 0.0)
    return _sublane_total(total)


def _route_head_fast(s1, s2):
    p1, p2 = _pieces(s1), _pieces(s2)
    v1 = _top_sorted(p1, _SORT16)
    v2 = _top_sorted(p2, _SORT16)
    sub = lax.broadcasted_iota(jnp.int32, v1[0].shape, 0)

    def spread(vals):
        out = vals[0]
        for r in range(1, _SUBLANES):
            out = jnp.where(sub == r, vals[r], out)
        return out

    v2_lo, v2_hi, v1_hi = spread(v2[:8]), spread(v2[8:]), spread(v1[8:])
    cands = [v1[0] + v2_lo, v1[0] + v2_hi, v1[1] + v2_lo]
    for a in range(2, 8):
        cands.append(jnp.where(sub < PEER_TOPK // (a + 1), v1[a] + v2_lo, -jnp.inf))
    cands.append(v1_hi + v2[0])
    ts = _top_sorted(cands, _SORT10)
    tau = ts[PEER_TOPK - 1]
    z = jnp.exp(ts[0] - ts[0])
    for kk in range(1, PEER_TOPK):
        z = z + jnp.exp(ts[kk] - ts[0])

    tied = (_count_ge(p1, v1[-1]) != float(PEER_TOPK)) | (_count_ge(p2, v2[-1]) != float(PEER_TOPK))
    tied = tied | (_count_ge(cands, tau) != float(PEER_TOPK))
    for b in range(PEER_TOPK - 1):
        tied = tied | (v1[b] == v1[b + 1]) | (v2[b] == v2[b + 1])

    cnt = []
    for a in range(PEER_TOPK):
        c = jnp.zeros(tau.shape, F32)
        for b in range(PEER_TOPK // (a + 1)):
            c = c + jnp.where(v1[a] + v2[b] >= tau, 1.0, 0.0)
        cnt.append(c)
    c1, r2 = [], []
    for x in p1:
        c = jnp.zeros(x.shape, F32)
        for a in range(PEER_TOPK):
            c = jnp.where(x == v1[a], cnt[a], c)
        c1.append(c)
    for x in p2:
        r = jnp.zeros(x.shape, F32)
        for b in range(PEER_TOPK):
            r = r + jnp.where(v2[b] > x, 1.0, 0.0)
        r2.append(r)
    inv_z = 1.0 / z
    e1 = [jnp.exp(x - v1[0]) * inv_z for x in p1]
    e2 = [jnp.exp(x - v2[0]) for x in p2]
    cat = lambda ps: jnp.concatenate(ps, axis=0)
    return (cat(c1), cat(e1), cat(r2), cat(e2)), tied


def _extract_sorted(scores, by_key):
    nk, lanes = scores[0].shape
    kio = lax.broadcasted_iota(jnp.int32, (nk, lanes), 0).astype(F32)
    slot = lax.broadcasted_iota(jnp.int32, (PEER_TOPK, lanes), 0)

    def body(a, carry):
        here = slot == a
        out = []
        for (v, vals, aux), ranked in zip(carry, by_key):
            m = jnp.max(v, axis=0, keepdims=True)
            idx = jnp.min(jnp.where(v == m, kio, float(nk)), axis=0, keepdims=True)
            hit = kio == idx
            aux = jnp.where(hit, jnp.asarray(a, F32), aux) if ranked else jnp.where(here, idx, aux)
            out.append((jnp.where(hit, -jnp.inf, v), jnp.where(here, m, vals), aux))
        return tuple(out)

    small = jnp.zeros((PEER_TOPK, lanes), F32)
    unranked = jnp.full((nk, lanes), float(PEER_TOPK), F32)
    init = tuple((v, small, unranked if ranked else small) for v, ranked in zip(scores, by_key))
    return [(vals, aux) for _, vals, aux in lax.fori_loop(0, PEER_TOPK, body, init)]


def _route_head_exact(s1, s2, pos, neg):
    lanes = s1.shape[1]
    slot = lax.broadcasted_iota(jnp.int32, (PEER_TOPK, lanes), 0)
    kio = lax.broadcasted_iota(jnp.int32, (PEER_N_KEYS, lanes), 0).astype(F32)
    (v1, idx1), (v2, rank2) = _extract_sorted([s1, s2], [False, True])
    blocks = [v1[0:1] + v2[0:8], v1[0:1] + v2[8:16]]
    blocks += [v1[a:a + 1] + v2[0:8] for a in range(1, 8)]
    blocks += [v1[8:16] + v2[0:1]]
    cand = jnp.concatenate(blocks, axis=0) + neg

    def pick(kk, carry):
        cand, chosen, ts = carry
        m = jnp.max(cand, axis=0, keepdims=True)
        first = jnp.min(jnp.where(cand == m, pos, 1e9), axis=0, keepdims=True)
        hit = pos == first
        return (jnp.where(hit, -jnp.inf, cand), jnp.where(hit, 1.0, chosen), jnp.where(slot == kk, m, ts))

    _, chosen, ts = lax.fori_loop(0, PEER_TOPK, pick,
                                  (cand, jnp.zeros_like(cand), jnp.zeros((PEER_TOPK, lanes), F32)))
    z = jnp.sum(jnp.exp(ts - ts[0:1]), axis=0, keepdims=True)
    counts = [jnp.sum(chosen[0:16], axis=0, keepdims=True)]
    counts += [jnp.sum(chosen[8 * a + 8:8 * a + 16], axis=0, keepdims=True) for a in range(1, 8)]
    counts += [chosen[72 + a:73 + a] for a in range(8)]
    c1 = jnp.zeros((PEER_N_KEYS, lanes), F32)
    for a in range(PEER_TOPK):
        c1 = jnp.where(kio == idx1[a:a + 1], counts[a], c1)
    return c1, jnp.exp(s1 - v1[0:1]) / z, rank2, jnp.exp(s2 - v2[0:1])


def _route_kernel(h_ref, wq_ref, keys_ref, pos_ref, neg_ref,
                  c1_ref, e1_ref, r2_ref, e2_ref, qt_ref, sc_ref):
    half = PEER_QUERY_DIM // 2
    qt_ref[...] = _dot_nt(wq_ref[...], h_ref[...]).astype(BF16)

    def store(h, maps):
        c1, e1, r2, e2 = maps
        c1_ref[h] = c1
        e1_ref[h] = e1
        r2_ref[h] = r2.astype(BF16)
        e2_ref[h] = e2.astype(BF16)

    def head_body(h, _):
        r0 = pl.multiple_of(h * PEER_QUERY_DIM, PEER_QUERY_DIM)
        sc_ref[0] = _dot(keys_ref[2 * h], qt_ref[pl.ds(r0, half), :])
        sc_ref[1] = _dot(keys_ref[2 * h + 1], qt_ref[pl.ds(r0 + half, half), :])
        maps, tied = _route_head_fast(sc_ref[0], sc_ref[1])
        any_tied = jnp.max(jnp.where(tied, 1.0, 0.0)) > 0.0

        @pl.when(any_tied)
        def _():
            store(h, _route_head_exact(sc_ref[0], sc_ref[1], pos_ref[...], neg_ref[...]))

        @pl.when(jnp.logical_not(any_tied))
        def _():
            store(h, maps)

        return 0

    lax.fori_loop(0, PEER_HEADS, head_body, 0)


def _peer_route(h2, wq_t, keys, tt=ROUTE_TILE):
    T, D = h2.shape
    pos, neg = _cand_tables(tt)
    stat_spec = pl.BlockSpec((PEER_HEADS, PEER_N_KEYS, tt), lambda i: (0, 0, i))
    stat = lambda dt: jax.ShapeDtypeStruct((PEER_HEADS, PEER_N_KEYS, T), dt)
    return pl.pallas_call(
        _route_kernel,
        grid=(T // tt,),
        in_specs=[pl.BlockSpec((tt, D), lambda i: (i, 0)),
                  pl.BlockSpec(wq_t.shape, lambda i: (0, 0)),
                  pl.BlockSpec(keys.shape, lambda i: (0, 0, 0)),
                  pl.BlockSpec((_CAND_ROWS, tt), lambda i: (0, 0)),
                  pl.BlockSpec((_CAND_ROWS, tt), lambda i: (0, 0))],
        out_specs=[stat_spec] * 4,
        out_shape=[stat(F32), stat(F32), stat(BF16), stat(BF16)],
        scratch_shapes=[pltpu.VMEM((PEER_HEADS * PEER_QUERY_DIM, tt), BF16),
                        pltpu.VMEM((2, PEER_N_KEYS, tt), F32)],
        compiler_params=_cparams(("parallel",)),
        name="peer_route",
    )(h2, wq_t, keys, pos, neg)


_KEY_GROUP = 16
_UNITS = 4
_DRAIN_PIECES = 2


def _build_gated(a_ref, p_ref, c1_ref, e1_ref, r2_ref, e2_ref, key0, g0, ng, lt):
    rep = PEER_N_KEYS // BF16_ROWS

    def rows16(row):
        blk = jnp.broadcast_to(row, (BF16_ROWS, LANES)).astype(BF16)
        return jnp.concatenate([blk] * rep, axis=0)

    ls = slice(lt * LANES, (lt + 1) * LANES)
    w = [jnp.zeros((PEER_N_KEYS, LANES), BF16) for _ in range(ng)]
    for h in range(PEER_HEADS):
        c1 = c1_ref[h, pl.ds(key0, _KEY_GROUP), ls]
        e1 = e1_ref[h, pl.ds(key0, _KEY_GROUP), ls]
        r2 = r2_ref[h, :, ls]
        e2 = e2_ref[h, :, ls]
        for g in range(ng):
            thr = rows16(c1[g0 + g:g0 + g + 1])
            gate = rows16(e1[g0 + g:g0 + g + 1])
            w[g] = w[g] + jnp.where(r2 < thr, e2, jnp.zeros_like(e2)) * gate
    for g in range(g0, g0 + ng):
        rs = slice(g * PEER_N_KEYS, (g + 1) * PEER_N_KEYS)
        a = a_ref[rs, ls]
        gelu = 0.5 * a * (1.0 + lax.erf(a * (2.0 ** -0.5)))
        p_ref[rs, ls] = gelu.astype(BF16) * w[g - g0]


def _experts_kernel(h_ref, dn_ref, upt_ref, c1_ref, e1_ref, r2_ref, e2_ref, x_ref, *rest, emit_norm):
    if emit_norm:
        g_ref, o_ref, hn_ref, a_ref, p_ref, acc_ref = rest
    else:
        o_ref, a_ref, p_ref, acc_ref = rest
    te, tt = a_ref.shape
    d_model = acc_ref.shape[0]
    e = pl.program_id(1)
    n_tiles = pl.num_programs(1) - 1
    cur = e % 2
    ng = _KEY_GROUP // _UNITS
    n_lane = tt // LANES
    key0 = pl.multiple_of(jnp.minimum(e, n_tiles - 1) * _KEY_GROUP, _KEY_GROUP)

    def front_mm(u):
        rows = te // _UNITS
        rs = slice(u * rows, (u + 1) * rows)
        a_ref[rs, :] = _dot_nt(dn_ref[rs, :], h_ref[...])

    def back_mm(r):
        rows = d_model // _DRAIN_PIECES
        rs = slice(r * rows, (r + 1) * rows)
        acc_ref[rs, :] += _dot(upt_ref[rs, :], p_ref[1 - cur])

    units_per_drain = _UNITS // _DRAIN_PIECES

    def run(front, back):
        if front:
            front_mm(0)
        for u in range(_UNITS):
            for lt in range(n_lane):
                if front:
                    _build_gated(a_ref, p_ref.at[cur], c1_ref, e1_ref, r2_ref, e2_ref, key0, u * ng, ng, lt)
                if front and lt == 0 and u + 1 < _UNITS:
                    front_mm(u + 1)
                if back and lt == n_lane // 2 and (u + 1) % units_per_drain == 0:
                    back_mm(u // units_per_drain)

    @pl.when(e == 0)
    def _():
        acc_ref[...] = jnp.zeros_like(acc_ref)
        run(True, False)

    @pl.when((e > 0) & (e < n_tiles))
    def _():
        run(True, True)

    @pl.when(e == n_tiles)
    def _():
        run(False, True)
        y = x_ref[...] + acc_ref[...].T
        o_ref[...] = y
        if emit_norm:
            hn_ref[...] = _rms_normed(y, g_ref[...])


def _peer_experts(h2, down, up_t, stats, x, next_gain, tt=EXPERT_TOKEN_TILE):
    T, D = h2.shape
    E = down.shape[0]
    te = _KEY_GROUP * PEER_N_KEYS
    n_tiles = E // te
    emit_norm = next_gain is not None
    stat_spec = pl.BlockSpec((PEER_HEADS, PEER_N_KEYS, tt), lambda i, e: (0, 0, i))
    row_spec = pl.BlockSpec((tt, D), lambda i, e: (i, 0))
    in_specs = [row_spec,
                pl.BlockSpec((te, D), lambda i, e: (jnp.minimum(e, n_tiles - 1), 0)),
                pl.BlockSpec((None, D, te), lambda i, e: (jnp.maximum(e - 1, 0), 0, 0)),
                stat_spec, stat_spec, stat_spec, stat_spec,
                row_spec]
    args = [h2, down, up_t, *stats, x]
    out_specs = [row_spec]
    out_shape = [jax.ShapeDtypeStruct((T, D), F32)]
    if emit_norm:
        in_specs.append(pl.BlockSpec((1, D), lambda i, e: (0, 0)))
        args.append(next_gain.reshape(1, D).astype(F32))
        out_specs.append(row_spec)
        out_shape.append(jax.ShapeDtypeStruct((T, D), BF16))
    stat_bytes = 2 * PEER_HEADS * PEER_N_KEYS * tt * (4 + 2)
    block_bytes = (tt * D * 2 + 2 * te * D * 2 + stat_bytes + tt * D * 4
                   + tt * D * 4 + (tt * D * 2 + D * 4 if emit_norm else 0))
    scratch_bytes = te * tt * 4 + 2 * te * tt * 2 + D * tt * 4
    res = pl.pallas_call(
        functools.partial(_experts_kernel, emit_norm=emit_norm),
        grid=(T // tt, n_tiles + 1),
        in_specs=in_specs, out_specs=out_specs, out_shape=out_shape,
        scratch_shapes=[pltpu.VMEM((te, tt), F32),
                        pltpu.VMEM((2, te, tt), BF16),
                        pltpu.VMEM((D, tt), F32)],
        compiler_params=_cparams(("parallel", "arbitrary"), 2 * block_bytes + scratch_bytes + VMEM_SPILL_MARGIN),
        name="peer_experts",
    )(*args)
    return (res[0], res[1]) if emit_norm else (res[0], None)


def _peer_layer(x, h2, next_gain, w_query, sub_keys, down, up):
    keys = sub_keys.reshape(PEER_HEADS * 2, PEER_N_KEYS, PEER_QUERY_DIM // 2).astype(BF16)
    stats = _peer_route(h2, w_query.T.astype(BF16), keys)
    te = _KEY_GROUP * PEER_N_KEYS
    up_t = up.reshape(up.shape[0] // te, te, up.shape[1]).transpose(0, 2, 1).astype(BF16)
    return _peer_experts(h2, down.astype(BF16), up_t, stats, x, next_gain)


def _tile_heads(g, n):
    return jnp.tile(g.astype(F32), n)


def _even_mixer(x, h, ffn_gain, w_in, f_bias, qn_a, kn_a, qn_b, kn_b, w_out, tabs, *, batch, seq):
    w = w_in.astype(BF16)
    o_qa, o_ka, o_va, o_qb, o_kb, o_vb, o_gb, o_fb = (
        0, A_W, 2 * A_W, 3 * A_W, 3 * A_W + B_W, 3 * A_W + 2 * B_W, 3 * A_W + 3 * B_W, 3 * A_W + 4 * B_W)
    gain_a = jnp.concatenate([_tile_heads(qn_a, MOBA_HEADS), _tile_heads(kn_a, MOBA_HEADS)])
    qk_a, km = _proj(h, w[:, o_qa:o_va], seq=seq, tn=A_W, gain=gain_a, rope_tabs=tabs, kmean=True)
    nb = seq // MOBA_BLOCK
    kmean = km.reshape(batch, nb, 2 * A_W)[:, :, A_W:]
    kmean = jnp.pad(kmean, ((0, 0), (0, LANES - nb), (0, 0))).astype(BF16)
    gain_b = jnp.concatenate([_tile_heads(qn_b, FOX_HEADS), _tile_heads(kn_b, FOX_HEADS)])
    qk_b = _proj(h, w[:, o_qb:o_vb], seq=seq, tn=B_W, gain=gain_b)
    w_plain = jnp.concatenate([w[:, o_va:o_qb], w[:, o_vb:o_fb]], axis=1)
    vvg = _proj(h, w_plain, seq=seq, tn=A_W)
    va, vb, gb = vvg[:, :A_W], vvg[:, A_W:A_W + B_W], vvg[:, A_W + B_W:]
    frow = _fox_gates(h, w[:, o_fb:].T, f_bias, batch=batch, seq=seq)
    oa = _moba(qk_a[:, :A_W], qk_a[:, A_W:], va, kmean, batch=batch, seq=seq)
    logit_bound = 1.01 * ATTN_SCALE * HEAD_DIM * jnp.max(jnp.abs(qn_b)) * jnp.max(jnp.abs(kn_b)) + 0.01
    ob = _fox(qk_b[:, :B_W], qk_b[:, B_W:], vb, gb, frow, logit_bound, batch=batch, seq=seq)
    return _outproj([oa, ob], w_out.astype(BF16), x, ffn_gain)


def _odd_mixer(x, h, ffn_gain, w_in, qn, kn, sinks, w_out, tabs, *, batch, seq):
    w = w_in.astype(BF16)
    qw = SWA_Q_HEADS * HEAD_DIM
    kw = SWA_KV_HEADS * HEAD_DIM
    q = _proj(h, w[:, :qw], seq=seq, tn=512, gain=_tile_heads(qn, SWA_Q_HEADS), rope_tabs=tabs)
    k = _proj(h, w[:, qw:qw + kw], seq=seq, tn=kw, gain=_tile_heads(kn, SWA_KV_HEADS), rope_tabs=tabs)
    v = _proj(h, w[:, qw + kw:], seq=seq, tn=kw)
    o = _swa(q, k, v, sinks, batch=batch, seq=seq)
    return _outproj([o], w_out.astype(BF16), x, ffn_gain)


def kernel(x, attn_norm, ffn_norm, ev_w_in, ev_forget_bias, ev_q_norm_a, ev_k_norm_a, ev_q_norm_b,
           ev_k_norm_b, ev_w_out, od_w_in, od_q_norm, od_k_norm, od_sinks, od_w_out,
           peer_w_query, peer_sub_keys, peer_down, peer_up):
    batch, seq, d_model = x.shape
    depth = attn_norm.shape[0]
    tabs = _rope_tables(seq)
    xt = x.reshape(batch * seq, d_model)
    h = _rmsnorm(xt, attn_norm[0])
    for l in range(depth):
        i = l // 2
        if l % 2 == 0:
            xt, h2 = _even_mixer(xt, h, ffn_norm[l], ev_w_in[i], ev_forget_bias[i], ev_q_norm_a[i],
                                 ev_k_norm_a[i], ev_q_norm_b[i], ev_k_norm_b[i], ev_w_out[i], tabs,
                                 batch=batch, seq=seq)
        else:
            xt, h2 = _odd_mixer(xt, h, ffn_norm[l], od_w_in[i], od_q_norm[i], od_k_norm[i], od_sinks[i],
                                od_w_out[i], tabs, batch=batch, seq=seq)
        next_gain = attn_norm[l + 1] if l + 1 < depth else None
        xt, h = _peer_layer(xt, h2, next_gain, peer_w_query[l], peer_sub_keys[l], peer_down[l], peer_up[l])
    return xt.reshape(batch, seq, d_model)
```

```python
import functools

import numpy as np
import jax
import jax.numpy as jnp
from jax import lax
from jax.experimental import pallas as pl
from jax.experimental.pallas import tpu as pltpu

F32 = jnp.float32
BF16 = jnp.bfloat16

HEAD_DIM = 64
ROT_DIM = HEAD_DIM // 4
ROPE_THETA = 500000.0
ATTN_SCALE = HEAD_DIM ** -0.5
EPS = 1e-6
NEG_INF = -1e30

MOBA_HEADS = 8
FOX_HEADS = 8
MOBA_BLOCK = 256
MOBA_TOPK = 3
A_W = MOBA_HEADS * HEAD_DIM
B_W = FOX_HEADS * HEAD_DIM

SWA_Q_HEADS = 16
SWA_KV_HEADS = 2
SWA_WINDOW = 128

PEER_HEADS = 8
PEER_N_KEYS = 128
PEER_TOPK = 16
PEER_QUERY_DIM = 128

LANES = 128
BF16_ROWS = 16
PAIR_W = 2 * HEAD_DIM

ROW_TILE = 512
FLASH_TILE = 2 * MOBA_BLOCK
ROUTE_TILE = 4 * LANES
EXPERT_TOKEN_TILE = 512
VMEM_LIMIT = 48 * 1024 * 1024
VMEM_SPILL_MARGIN = 4 * 1024 * 1024


def _cparams(sem, vmem_limit=VMEM_LIMIT):
    return pltpu.CompilerParams(dimension_semantics=sem, vmem_limit_bytes=vmem_limit)


def _dot_nt(a, b):
    return lax.dot_general(a, b, (((1,), (1,)), ((), ())), preferred_element_type=F32)


def _dot(a, b):
    return jnp.dot(a, b, preferred_element_type=F32)


def _split2(x):
    h1 = x.astype(BF16)
    return h1, (x - h1.astype(F32)).astype(BF16)


def _split3(x):
    h1 = x.astype(BF16)
    r1 = x - h1.astype(F32)
    h2 = r1.astype(BF16)
    h3 = (r1 - h2.astype(F32)).astype(BF16)
    return h1, h2, h3


def _rmsnorm_kernel(x_ref, g_ref, o_ref):
    x = x_ref[...]
    ms = jnp.mean(x * x, axis=-1, keepdims=True)
    o_ref[...] = (x * lax.rsqrt(ms + EPS) * g_ref[...]).astype(o_ref.dtype)


def _rmsnorm(x, gain, tm=ROW_TILE):
    T, D = x.shape
    return pl.pallas_call(
        _rmsnorm_kernel,
        grid=(T // tm,),
        in_specs=[pl.BlockSpec((tm, D), lambda i: (i, 0)),
                  pl.BlockSpec((1, D), lambda i: (0, 0))],
        out_specs=pl.BlockSpec((tm, D), lambda i: (i, 0)),
        out_shape=jax.ShapeDtypeStruct((T, D), BF16),
        compiler_params=_cparams(("parallel",)),
        name="rmsnorm",
    )(x, gain.reshape(1, D))


def _proj_kernel(*refs, norm, rope, kmean, tn):
    it = iter(refs)
    h_ref, w_ref = next(it), next(it)
    gain_ref = next(it) if norm else None
    bd_ref = next(it) if norm else None
    if rope:
        c_ref, sa_ref, sb_ref = next(it), next(it), next(it)
    o_ref = next(it)
    km_ref = next(it) if kmean else None

    y = _dot(h_ref[...], w_ref[...])
    if norm:
        y2 = y * y
        bd = bd_ref[...]
        cols = []
        tm = y.shape[0]
        for c in range(tn // LANES):
            h1, h2 = _split2(y2[:, c * LANES:(c + 1) * LANES])
            t = _dot(jnp.concatenate([h1, h2], axis=0), bd)
            cols.append(t[:tm] + t[tm:])
        ms = cols[0] if len(cols) == 1 else jnp.concatenate(cols, axis=1)
        y = y * lax.rsqrt(ms + EPS) * gain_ref[...]
    if rope:
        rep = tn // LANES
        tile = (lambda t: t) if rep == 1 else (lambda t: jnp.concatenate([t] * rep, axis=1))
        y = (y * tile(c_ref[...])
             + pltpu.roll(y, tn - ROT_DIM // 2, 1) * tile(sa_ref[...])
             + pltpu.roll(y, ROT_DIM // 2, 1) * tile(sb_ref[...]))
    o_ref[...] = y.astype(o_ref.dtype)
    if kmean:
        for r in range(km_ref.shape[0]):
            km_ref[r] = jnp.mean(y[r * MOBA_BLOCK:(r + 1) * MOBA_BLOCK], axis=0, keepdims=True)


def _proj(h, w, *, seq, tn, tm=ROW_TILE, gain=None, rope_tabs=None, kmean=False):
    T, D = h.shape
    N = w.shape[1]
    norm = gain is not None
    rope = rope_tabs is not None
    nseq = seq // tm
    in_specs = [pl.BlockSpec((tm, D), lambda i, j: (i, 0)),
                pl.BlockSpec((D, tn), lambda i, j: (0, j))]
    args = [h, w]
    if norm:
        bd = np.kron(np.eye(LANES // HEAD_DIM), np.ones((HEAD_DIM, HEAD_DIM))) / HEAD_DIM
        in_specs += [pl.BlockSpec((1, tn), lambda i, j: (0, j)),
                     pl.BlockSpec((LANES, LANES), lambda i, j: (0, 0))]
        args += [gain.reshape(1, N).astype(F32), jnp.asarray(bd, BF16)]
    if rope:
        in_specs += [pl.BlockSpec((tm, LANES), lambda i, j: (i % nseq, 0))] * 3
        args += list(rope_tabs)
    out_specs = [pl.BlockSpec((tm, tn), lambda i, j: (i, j))]
    out_shape = [jax.ShapeDtypeStruct((T, N), BF16)]
    if kmean:
        out_specs.append(pl.BlockSpec((tm // MOBA_BLOCK, 1, tn), lambda i, j: (i, 0, j)))
        out_shape.append(jax.ShapeDtypeStruct((T // MOBA_BLOCK, 1, N), F32))
    res = pl.pallas_call(
        functools.partial(_proj_kernel, norm=norm, rope=rope, kmean=kmean, tn=tn),
        grid=(T // tm, N // tn),
        in_specs=in_specs, out_specs=out_specs, out_shape=out_shape,
        compiler_params=_cparams(("parallel", "parallel")),
        name="proj",
    )(*args)
    return res if kmean else res[0]


def _rope_tables(seq):
    half = ROT_DIM // 2
    inv_freq = jnp.power(ROPE_THETA, -jnp.arange(0, ROT_DIM, 2, dtype=F32) / ROT_DIM)
    ang = jnp.arange(seq, dtype=F32)[:, None] * inv_freq[None, :]
    cos, sin = jnp.cos(ang), jnp.sin(ang)
    one = jnp.ones((seq, HEAD_DIM - ROT_DIM), F32)
    zero = jnp.zeros((seq, HEAD_DIM - ROT_DIM), F32)
    z8 = jnp.zeros((seq, half), F32)
    c = jnp.concatenate([cos, cos, one], axis=1)
    sa = jnp.concatenate([-sin, z8, zero], axis=1)
    sb = jnp.concatenate([z8, sin, zero], axis=1)
    rep = LANES // HEAD_DIM
    return tuple(jnp.concatenate([t] * rep, axis=1) for t in (c, sa, sb))


def _gates_kernel(h_ref, wf_ref, b_ref, tri_ref, o_ref, carry_ref):
    @pl.when(pl.program_id(1) == 0)
    def _():
        carry_ref[...] = jnp.zeros_like(carry_ref)

    z = _dot_nt(wf_ref[...], h_ref[...]) + b_ref[...][:, :1]
    lf = jnp.minimum(z, 0.0) - jnp.log1p(jnp.exp(-jnp.abs(z)))
    tri = tri_ref[...]
    h1, h2, h3 = _split3(lf)
    cs = _dot(h1, tri) + _dot(h2, tri) + _dot(h3, tri) + carry_ref[...][:, :1]
    o_ref[0] = cs
    carry_ref[...] = jnp.broadcast_to(cs[:, -1:], carry_ref.shape)


def _fox_gates(h, wf_t, bias, *, batch, seq, tm=ROW_TILE):
    T, D = h.shape
    nh = wf_t.shape[0]
    nseq = seq // tm
    tri = jnp.asarray(np.triu(np.ones((tm, tm))), BF16)
    return pl.pallas_call(
        _gates_kernel,
        grid=(batch, nseq),
        in_specs=[pl.BlockSpec((tm, D), lambda b, s: (b * nseq + s, 0)),
                  pl.BlockSpec((nh, D), lambda b, s: (0, 0)),
                  pl.BlockSpec((nh, LANES), lambda b, s: (0, 0)),
                  pl.BlockSpec((tm, tm), lambda b, s: (0, 0))],
        out_specs=pl.BlockSpec((1, nh, tm), lambda b, s: (b, 0, s)),
        out_shape=jax.ShapeDtypeStruct((batch, nh, seq), F32),
        scratch_shapes=[pltpu.VMEM((nh, LANES), F32)],
        compiler_params=_cparams(("parallel", "arbitrary")),
        name="fox_gates",
    )(h, wf_t, jnp.broadcast_to(bias.astype(F32)[:, None], (nh, LANES)), tri)


def _lane_tile(x, width):
    rep = width // LANES
    return x if rep == 1 else jnp.concatenate([x] * rep, axis=1)


def _flash_init(m_ref, acc_ref):
    m_ref[...] = jnp.full(m_ref.shape, NEG_INF, F32)
    acc_ref[...] = jnp.zeros(acc_ref.shape, F32)


def _head_values(v):
    lane = lax.broadcasted_iota(jnp.int32, v.shape, 1)
    return [jnp.where((lane >= hh * HEAD_DIM) & (lane < (hh + 1) * HEAD_DIM), v, jnp.ones_like(v))
            for hh in range(2)]


def _flash_update(slot, s, v, m_ref, acc_ref):
    tk = s.shape[1]
    m_prev = m_ref[slot]
    m_new = jnp.maximum(m_prev, jnp.max(s, axis=1, keepdims=True))
    alpha = jnp.exp(m_prev - m_new)
    p = jnp.exp(s - _lane_tile(m_new, tk))
    acc_ref[slot] = alpha * acc_ref[slot] + _dot(p.astype(BF16), v)
    m_ref[slot] = m_new


def _flash_finish(lane, acc_ref):
    outs = []
    for hh in range(2):
        acc = acc_ref[hh]
        den = (1 - hh) * HEAD_DIM
        outs.append(acc / acc[:, den:den + 1])
    return jnp.where(lane < HEAD_DIM, outs[0], outs[1])


def _flash_tiles(qi, tk, logits, values, m_ref, acc_ref):
    def absorb(s, off):
        vh = values(off)
        for hh in range(2):
            _flash_update(hh, s[hh], vh[hh], m_ref, acc_ref)

    def pair(off_a, off_b, b_diagonal):
        sa = logits(off_a, False)
        sb = logits(off_b, b_diagonal)
        absorb(sa, off_a)
        absorb(sb, off_b)

    def body(jj, carry):
        off = pl.multiple_of(2 * jj * tk, 2 * tk)
        pair(off, off + tk, False)
        return carry

    lax.fori_loop(0, qi // 2, body, 0)
    diag = pl.multiple_of(qi * tk, tk)

    @pl.when(qi % 2 == 1)
    def _():
        pair(diag - tk, diag, True)

    @pl.when(qi % 2 == 0)
    def _():
        absorb(logits(diag, True), diag)


def _head_queries(q, lane):
    qs = q * ATTN_SCALE
    return [jnp.where((lane >= hh * HEAD_DIM) & (lane < (hh + 1) * HEAD_DIM), qs, jnp.zeros_like(qs))
            for hh in range(2)]


def _moba_kernel(q_ref, k_ref, v_ref, km_ref, o_ref, m_ref, acc_ref):
    tq = q_ref.shape[1]
    tk = tq
    qi = pl.program_id(2)
    lane = lax.broadcasted_iota(jnp.int32, (tq, LANES), 1)
    lane_f = lane.astype(F32)
    rowv = lax.broadcasted_iota(jnp.int32, (tq, LANES), 0)
    row_blk = 2 * qi + (rowv >= MOBA_BLOCK).astype(jnp.int32)
    row = lax.broadcasted_iota(jnp.int32, (tq, tk), 0)
    col = lax.broadcasted_iota(jnp.int32, (tq, tk), 1)
    qh = _head_queries(q_ref[0], lane)
    _flash_init(m_ref, acc_ref)

    sels = []
    for hh in range(2):
        gate = _dot_nt(qh[hh], km_ref[0])
        gate = jnp.where(lane < row_blk, gate, -jnp.inf)
        sel = jnp.zeros((tq, LANES), F32)
        for _ in range(MOBA_TOPK):
            m = jnp.max(gate, axis=1, keepdims=True)
            idx = jnp.min(jnp.where(gate == m, lane_f, float(LANES)), axis=1, keepdims=True)
            hit = lane_f == idx
            sel = jnp.where(hit & (m > -jnp.inf), 1.0, sel)
            gate = jnp.where(hit, -jnp.inf, gate)
        sels.append(sel)

    def chosen(sel, blk):
        return jnp.max(jnp.where(lane == blk, sel, 0.0), axis=1, keepdims=True) > 0.0

    def logits(off, diagonal):
        kj = k_ref[0, pl.ds(off, tk), :]
        out = []
        for hh in range(2):
            s = _dot_nt(qh[hh], kj)
            if diagonal:
                visible = (col >= MOBA_BLOCK) | (row < MOBA_BLOCK) | chosen(sels[hh], 2 * qi)
                s = jnp.where((col <= row) & visible, s, NEG_INF)
            else:
                blk = 2 * (off // tk)
                s = jnp.concatenate(
                    [jnp.where(chosen(sels[hh], blk), s[:, :MOBA_BLOCK], NEG_INF),
                     jnp.where(chosen(sels[hh], blk + 1), s[:, MOBA_BLOCK:], NEG_INF)], axis=1)
            out.append(s)
        return out

    def values(off):
        return _head_values(v_ref[0, pl.ds(off, tk), :])

    _flash_tiles(qi, tk, logits, values, m_ref, acc_ref)
    o_ref[0] = _flash_finish(lane, acc_ref).astype(o_ref.dtype)


def _flash_scratch(tq):
    return [pltpu.VMEM((2, tq, LANES), F32)] * 2


def _moba(q, k, v, kmean, *, batch, seq):
    W = q.shape[1]
    tq = FLASH_TILE
    q3, k3, v3 = (t.reshape(batch, seq, W) for t in (q, k, v))
    out = pl.pallas_call(
        _moba_kernel,
        grid=(batch, W // PAIR_W, seq // tq),
        in_specs=[pl.BlockSpec((1, tq, PAIR_W), lambda b, p, i: (b, i, p)),
                  pl.BlockSpec((1, seq, PAIR_W), lambda b, p, i: (b, 0, p)),
                  pl.BlockSpec((1, seq, PAIR_W), lambda b, p, i: (b, 0, p)),
                  pl.BlockSpec((1, LANES, PAIR_W), lambda b, p, i: (b, 0, p))],
        out_specs=pl.BlockSpec((1, tq, PAIR_W), lambda b, p, i: (b, i, p)),
        out_shape=jax.ShapeDtypeStruct((batch, seq, W), BF16),
        scratch_shapes=_flash_scratch(tq),
        compiler_params=_cparams(("parallel", "parallel", "parallel")),
        name="moba",
    )(q3, k3, v3, kmean)
    return out.reshape(batch * seq, W)


_EXP_UNDERFLOW = 112.0


def _fox_kernel(q_ref, k_ref, v_ref, g_ref, frow_ref, far_ref, o_ref, m_ref, acc_ref):
    tq = q_ref.shape[1]
    tk = tq
    pr = pl.program_id(1)
    qi = pl.program_id(2)
    lane = lax.broadcasted_iota(jnp.int32, (tq, LANES), 1)
    row = lax.broadcasted_iota(jnp.int32, (tq, tk), 0)
    col = lax.broadcasted_iota(jnp.int32, (tq, tk), 1)
    qh = _head_queries(q_ref[0], lane)
    _flash_init(m_ref, acc_ref)

    def key_gates(off, width):
        f_all = frow_ref[0, :, pl.ds(off, width)]
        sub = lax.broadcasted_iota(jnp.int32, f_all.shape, 0)
        return [jnp.sum(jnp.where(sub == 2 * pr + hh, f_all, 0.0), axis=0, keepdims=True) for hh in range(2)]

    f_ref = [f[:, :1] for f in key_gates(pl.multiple_of(qi * tq, tq), LANES)]

    def logits(off, diagonal):
        kj = k_ref[0, pl.ds(off, tk), :]
        fk = key_gates(off, tk)
        out = []
        for hh in range(2):
            s = _dot_nt(qh[hh], kj) - (fk[hh] - f_ref[hh])
            out.append(jnp.where(col <= row, s, NEG_INF) if diagonal else s)
        return out

    def absorb(s, off):
        vh = _head_values(v_ref[0, pl.ds(off, tk), :])
        for hh in range(2):
            _flash_update(hh, s[hh], vh[hh], m_ref, acc_ref)

    def vanishes(j):
        tail = frow_ref[0, :, pl.ds(pl.multiple_of((j + 1) * tk - LANES, LANES), LANES)]
        sub = lax.broadcasted_iota(jnp.int32, tail.shape, 0)
        last = lax.broadcasted_iota(jnp.int32, tail.shape, 1) == LANES - 1
        gap = tail - jnp.where(sub == 2 * pr, f_ref[0], f_ref[1]) - far_ref[...][:, :1]
        mine = last & ((sub == 2 * pr) | (sub == 2 * pr + 1))
        return jnp.min(jnp.where(mine, gap, jnp.inf)) >= 0.0

    absorb(logits(pl.multiple_of(qi * tk, tk), True), pl.multiple_of(qi * tk, tk))

    def more(j):
        return jnp.logical_and(j >= 1, jnp.logical_not(vanishes(jnp.maximum(j, 0))))

    def pair(j):
        off_a = pl.multiple_of(j * tk, tk)
        off_b = pl.multiple_of((j - 1) * tk, tk)
        sa = logits(off_a, False)
        sb = logits(off_b, False)
        absorb(sa, off_a)
        absorb(sb, off_b)
        return j - 2

    j = lax.while_loop(more, pair, qi - 1)

    @pl.when(jnp.logical_and(j == 0, jnp.logical_not(vanishes(0))))
    def _():
        absorb(logits(0, False), 0)

    o = _flash_finish(lane, acc_ref)
    o_ref[0] = (o * jax.nn.sigmoid(g_ref[0].astype(F32))).astype(o_ref.dtype)


def _fox(q, k, v, g, frow, logit_bound, *, batch, seq, tq=FLASH_TILE):
    far = jnp.full((1, LANES), 2.0 * logit_bound + _EXP_UNDERFLOW, F32)
    W = q.shape[1]
    nh = frow.shape[1]
    q3, k3, v3, g3 = (t.reshape(batch, seq, W) for t in (q, k, v, g))
    out = pl.pallas_call(
        _fox_kernel,
        grid=(batch, W // PAIR_W, seq // tq),
        in_specs=[pl.BlockSpec((1, tq, PAIR_W), lambda b, p, i: (b, i, p)),
                  pl.BlockSpec((1, seq, PAIR_W), lambda b, p, i: (b, 0, p)),
                  pl.BlockSpec((1, seq, PAIR_W), lambda b, p, i: (b, 0, p)),
                  pl.BlockSpec((1, tq, PAIR_W), lambda b, p, i: (b, i, p)),
                  pl.BlockSpec((1, nh, seq), lambda b, p, i: (b, 0, 0)),
                  pl.BlockSpec((1, LANES), lambda b, p, i: (0, 0))],
        out_specs=pl.BlockSpec((1, tq, PAIR_W), lambda b, p, i: (b, i, p)),
        out_shape=jax.ShapeDtypeStruct((batch, seq, W), BF16),
        scratch_shapes=_flash_scratch(tq),
        compiler_params=_cparams(("parallel", "parallel", "parallel")),
        name="fox",
    )(q3, k3, v3, g3, frow, far)
    return out.reshape(batch * seq, W)


def _swa_kernel(q_ref, k_ref, v_ref, sink_ref, bias_ref, o_ref):
    tq = q_ref.shape[1]
    qi = pl.program_id(1)
    group = SWA_Q_HEADS // SWA_KV_HEADS
    tk = tq + SWA_WINDOW
    lane = lax.broadcasted_iota(jnp.int32, (tq, LANES), 1)
    kstart = pl.multiple_of(jnp.maximum(qi * tq - SWA_WINDOW, 0), SWA_WINDOW)
    k = k_ref[0, pl.ds(kstart, tk), :]
    vh = _head_values(v_ref[0, pl.ds(kstart, tk), :])
    swap = lambda t: pltpu.roll(t.astype(F32), HEAD_DIM, 1).astype(BF16)
    k_by_half = [k, swap(k)]
    bias = bias_ref[jnp.minimum(qi, 1)]
    bias = jnp.concatenate([bias] * (group // 2), axis=0)
    sink_tab = sink_ref[...]

    chains = []
    for c in range(SWA_KV_HEADS):
        v_by_half = [vh[c], swap(vh[c])]
        for swapped in range(2):
            hh = c if not swapped else 1 - c
            heads = [h for h in range(c * group, (c + 1) * group) if h % 2 == hh]
            pieces, sinks = [], []
            for head in heads:
                blk = q_ref[0, :, (head // 2) * PAIR_W:(head // 2 + 1) * PAIR_W] * ATTN_SCALE
                pieces.append(jnp.where((lane >= hh * HEAD_DIM) & (lane < (hh + 1) * HEAD_DIM),
                                        blk, jnp.zeros_like(blk)))
                sinks.append(jnp.broadcast_to(sink_tab[head:head + 1, :], (tq, LANES)))
            s = _dot_nt(jnp.concatenate(pieces, axis=0), k_by_half[swapped]) + bias
            chains.append((heads, s, jnp.concatenate(sinks, axis=0), v_by_half[swapped]))

    outs = [None] * SWA_Q_HEADS
    for heads, s, sink, v in chains:
        m = jnp.maximum(sink, jnp.max(s, axis=1, keepdims=True))
        p = jnp.exp(s - _lane_tile(m, tk))
        acc = _dot(p.astype(BF16), v)
        den = pltpu.roll(acc, HEAD_DIM, 1) + jnp.exp(sink - m)
        o = acc / den
        for idx, head in enumerate(heads):
            outs[head] = o[idx * tq:(idx + 1) * tq]
    for pp in range(SWA_Q_HEADS // 2):
        o_ref[0, :, pp * PAIR_W:(pp + 1) * PAIR_W] = jnp.where(
            lane < HEAD_DIM, outs[2 * pp], outs[2 * pp + 1]).astype(o_ref.dtype)


def _swa_bias(tq):
    r = np.arange(tq)[:, None]
    c = np.arange(tq + SWA_WINDOW)[None, :]
    tabs = []
    for key_offset in (0, SWA_WINDOW):
        dist = r + key_offset - c
        tabs.append(np.where((dist >= 0) & (dist < SWA_WINDOW), 0.0, NEG_INF))
    return jnp.asarray(np.stack(tabs), F32)


def _swa(q, k, v, sinks, *, batch, seq, tq=SWA_WINDOW):
    W = q.shape[1]
    tk = tq + SWA_WINDOW
    q3 = q.reshape(batch, seq, W)
    k3, v3 = (t.reshape(batch, seq, PAIR_W) for t in (k, v))
    sink_tab = jnp.broadcast_to(sinks.astype(F32)[:, None], (SWA_Q_HEADS, LANES))
    out = pl.pallas_call(
        _swa_kernel,
        grid=(batch, seq // tq),
        in_specs=[pl.BlockSpec((1, tq, W), lambda b, i: (b, i, 0)),
                  pl.BlockSpec((1, seq, PAIR_W), lambda b, i: (b, 0, 0)),
                  pl.BlockSpec((1, seq, PAIR_W), lambda b, i: (b, 0, 0)),
                  pl.BlockSpec((SWA_Q_HEADS, LANES), lambda b, i: (0, 0)),
                  pl.BlockSpec((2, tq, tk), lambda b, i: (0, 0, 0))],
        out_specs=pl.BlockSpec((1, tq, W), lambda b, i: (b, i, 0)),
        out_shape=jax.ShapeDtypeStruct((batch, seq, W), BF16),
        compiler_params=_cparams(("parallel", "parallel")),
        name="swa",
    )(q3, k3, v3, sink_tab, _swa_bias(tq))
    return out.reshape(batch * seq, W)


def _rms_normed(x, gain):
    ms = jnp.mean(x * x, axis=-1, keepdims=True)
    return (x * lax.rsqrt(ms + EPS) * gain).astype(BF16)


def _outproj_kernel(*refs, n_parts):
    parts = refs[:n_parts]
    w_ref, x_ref, g_ref, o_ref, h_ref = refs[n_parts:]
    y = x_ref[...]
    off = 0
    for p_ref in parts:
        kw = p_ref.shape[1]
        y = y + _dot(p_ref[...], w_ref[off:off + kw, :])
        off += kw
    o_ref[...] = y
    h_ref[...] = _rms_normed(y, g_ref[...])


def _outproj(parts, w, x, next_gain, tm=ROW_TILE):
    T, D = x.shape
    in_specs = [pl.BlockSpec((tm, p.shape[1]), lambda i: (i, 0)) for p in parts]
    in_specs += [pl.BlockSpec(w.shape, lambda i: (0, 0)),
                 pl.BlockSpec((tm, D), lambda i: (i, 0)),
                 pl.BlockSpec((1, D), lambda i: (0, 0))]
    row_spec = pl.BlockSpec((tm, D), lambda i: (i, 0))
    return pl.pallas_call(
        functools.partial(_outproj_kernel, n_parts=len(parts)),
        grid=(T // tm,),
        in_specs=in_specs,
        out_specs=[row_spec, row_spec],
        out_shape=[jax.ShapeDtypeStruct((T, D), F32), jax.ShapeDtypeStruct((T, D), BF16)],
        compiler_params=_cparams(("parallel",)),
        name="outproj",
    )(*parts, w, x, next_gain.reshape(1, D).astype(F32))


_CAND_ROWS = 80


def _cand_tables(lanes):
    pos = np.zeros((_CAND_ROWS,), np.float32)
    neg = np.zeros((_CAND_ROWS,), np.float32)
    r = 0
    for a, nb in ((0, 16), (1, 8), (2, 8), (3, 8), (4, 8), (5, 8), (6, 8), (7, 8)):
        for b in range(nb):
            pos[r] = a * PEER_TOPK + b
            neg[r] = 0.0 if (a + 1) * (b + 1) <= PEER_TOPK else -np.inf
            r += 1
    for a in range(8, 16):
        pos[r] = a * PEER_TOPK
        r += 1
    assert r == _CAND_ROWS
    tab = lambda t: jnp.asarray(np.broadcast_to(t[:, None], (_CAND_ROWS, lanes)).copy())
    return tab(pos), tab(neg)


def _batcher_pairs(n):
    pairs, p = [], 1
    while p < n:
        k = p
        while k >= 1:
            for j in range(k % p, n - k, 2 * k):
                for i in range(min(k, n - j - k)):
                    if (i + j) // (2 * p) == (i + j + k) // (2 * p):
                        pairs.append((i + j, i + j + k))
            k //= 2
        p *= 2
    return pairs


_SORT16 = _batcher_pairs(PEER_TOPK)
_SUBLANES = 8
_N_CAND_PIECES = _CAND_ROWS // _SUBLANES
_SORT10 = [(i, j) for i, j in _SORT16 if j < _N_CAND_PIECES]


def _compare_exchange(items, i, j):
    items[i], items[j] = jnp.maximum(items[i], items[j]), jnp.minimum(items[i], items[j])


def _top_sorted(pieces, pairs):
    items = list(pieces)
    for i, j in pairs:
        _compare_exchange(items, i, j)
    n = PEER_TOPK
    items += [jnp.full(items[0].shape, -jnp.inf, F32)] * (n - len(items))
    for shift in (4, 2, 1):
        items = [jnp.maximum(items[i], pltpu.roll(items[n - 1 - i], shift, 0)) for i in range(n)]
        d = n // 2
        while d >= 1:
            for i in range(n):
                if i & d == 0:
                    _compare_exchange(items, i, i + d)
            d //= 2
    return items


def _pieces(x):
    return [x[_SUBLANES * g:_SUBLANES * (g + 1)] for g in range(x.shape[0] // _SUBLANES)]


def _sublane_total(x):
    for shift in (4, 2, 1):
        x = x + pltpu.roll(x, shift, 0)
    return x


def _count_ge(pieces, thr):
    total = jnp.zeros(thr.shape, F32)
    for p in pieces:
        total = total + jnp.where(p >= thr, 1.0, 0.0)
    return _sublane_total(total)


def _route_head_fast(s1, s2):
    p1, p2 = _pieces(s1), _pieces(s2)
    v1 = _top_sorted(p1, _SORT16)
    v2 = _top_sorted(p2, _SORT16)
    sub = lax.broadcasted_iota(jnp.int32, v1[0].shape, 0)

    def spread(vals):
        out = vals[0]
        for r in range(1, _SUBLANES):
            out = jnp.where(sub == r, vals[r], out)
        return out

    v2_lo, v2_hi, v1_hi = spread(v2[:8]), spread(v2[8:]), spread(v1[8:])
    cands = [v1[0] + v2_lo, v1[0] + v2_hi, v1[1] + v2_lo]
    for a in range(2, 8):
        cands.append(jnp.where(sub < PEER_TOPK // (a + 1), v1[a] + v2_lo, -jnp.inf))
    cands.append(v1_hi + v2[0])
    ts = _top_sorted(cands, _SORT10)
    tau = ts[PEER_TOPK - 1]
    z = jnp.exp(ts[0] - ts[0])
    for kk in range(1, PEER_TOPK):
        z = z + jnp.exp(ts[kk] - ts[0])

    tied = (_count_ge(p1, v1[-1]) != float(PEER_TOPK)) | (_count_ge(p2, v2[-1]) != float(PEER_TOPK))
    tied = tied | (_count_ge(cands, tau) != float(PEER_TOPK))
    for b in range(PEER_TOPK - 1):
        tied = tied | (v1[b] == v1[b + 1]) | (v2[b] == v2[b + 1])

    cnt = []
    for a in range(PEER_TOPK):
        c = jnp.zeros(tau.shape, F32)
        for b in range(PEER_TOPK // (a + 1)):
            c = c + jnp.where(v1[a] + v2[b] >= tau, 1.0, 0.0)
        cnt.append(c)
    c1, r2 = [], []
    for x in p1:
        c = jnp.zeros(x.shape, F32)
        for a in range(PEER_TOPK):
            c = jnp.where(x == v1[a], cnt[a], c)
        c1.append(c)
    for x in p2:
        r = jnp.zeros(x.shape, F32)
        for b in range(PEER_TOPK):
            r = r + jnp.where(v2[b] > x, 1.0, 0.0)
        r2.append(r)
    inv_z = 1.0 / z
    e1 = [jnp.exp(x - v1[0]) * inv_z for x in p1]
    e2 = [jnp.exp(x - v2[0]) for x in p2]
    cat = lambda ps: jnp.concatenate(ps, axis=0)
    return (cat(c1), cat(e1), cat(r2), cat(e2)), tied


def _extract_sorted(scores, by_key):
    nk, lanes = scores[0].shape
    kio = lax.broadcasted_iota(jnp.int32, (nk, lanes), 0).astype(F32)
    slot = lax.broadcasted_iota(jnp.int32, (PEER_TOPK, lanes), 0)

    def body(a, carry):
        here = slot == a
        out = []
        for (v, vals, aux), ranked in zip(carry, by_key):
            m = jnp.max(v, axis=0, keepdims=True)
            idx = jnp.min(jnp.where(v == m, kio, float(nk)), axis=0, keepdims=True)
            hit = kio == idx
            aux = jnp.where(hit, jnp.asarray(a, F32), aux) if ranked else jnp.where(here, idx, aux)
            out.append((jnp.where(hit, -jnp.inf, v), jnp.where(here, m, vals), aux))
        return tuple(out)

    small = jnp.zeros((PEER_TOPK, lanes), F32)
    unranked = jnp.full((nk, lanes), float(PEER_TOPK), F32)
    init = tuple((v, small, unranked if ranked else small) for v, ranked in zip(scores, by_key))
    return [(vals, aux) for _, vals, aux in lax.fori_loop(0, PEER_TOPK, body, init)]


def _route_head_exact(s1, s2, pos, neg):
    lanes = s1.shape[1]
    slot = lax.broadcasted_iota(jnp.int32, (PEER_TOPK, lanes), 0)
    kio = lax.broadcasted_iota(jnp.int32, (PEER_N_KEYS, lanes), 0).astype(F32)
    (v1, idx1), (v2, rank2) = _extract_sorted([s1, s2], [False, True])
    blocks = [v1[0:1] + v2[0:8], v1[0:1] + v2[8:16]]
    blocks += [v1[a:a + 1] + v2[0:8] for a in range(1, 8)]
    blocks += [v1[8:16] + v2[0:1]]
    cand = jnp.concatenate(blocks, axis=0) + neg

    def pick(kk, carry):
        cand, chosen, ts = carry
        m = jnp.max(cand, axis=0, keepdims=True)
        first = jnp.min(jnp.where(cand == m, pos, 1e9), axis=0, keepdims=True)
        hit = pos == first
        return (jnp.where(hit, -jnp.inf, cand), jnp.where(hit, 1.0, chosen), jnp.where(slot == kk, m, ts))

    _, chosen, ts = lax.fori_loop(0, PEER_TOPK, pick,
                                  (cand, jnp.zeros_like(cand), jnp.zeros((PEER_TOPK, lanes), F32)))
    z = jnp.sum(jnp.exp(ts - ts[0:1]), axis=0, keepdims=True)
    counts = [jnp.sum(chosen[0:16], axis=0, keepdims=True)]
    counts += [jnp.sum(chosen[8 * a + 8:8 * a + 16], axis=0, keepdims=True) for a in range(1, 8)]
    counts += [chosen[72 + a:73 + a] for a in range(8)]
    c1 = jnp.zeros((PEER_N_KEYS, lanes), F32)
    for a in range(PEER_TOPK):
        c1 = jnp.where(kio == idx1[a:a + 1], counts[a], c1)
    return c1, jnp.exp(s1 - v1[0:1]) / z, rank2, jnp.exp(s2 - v2[0:1])


def _route_kernel(h_ref, wq_ref, keys_ref, pos_ref, neg_ref,
                  c1_ref, e1_ref, r2_ref, e2_ref, qt_ref, sc_ref):
    half = PEER_QUERY_DIM // 2
    qt_ref[...] = _dot_nt(wq_ref[...], h_ref[...]).astype(BF16)

    def store(h, maps):
        c1, e1, r2, e2 = maps
        c1_ref[h] = c1
        e1_ref[h] = e1
        r2_ref[h] = r2.astype(BF16)
        e2_ref[h] = e2.astype(BF16)

    def head_body(h, _):
        r0 = pl.multiple_of(h * PEER_QUERY_DIM, PEER_QUERY_DIM)
        sc_ref[0] = _dot(keys_ref[2 * h], qt_ref[pl.ds(r0, half), :])
        sc_ref[1] = _dot(keys_ref[2 * h + 1], qt_ref[pl.ds(r0 + half, half), :])
        maps, tied = _route_head_fast(sc_ref[0], sc_ref[1])
        any_tied = jnp.max(jnp.where(tied, 1.0, 0.0)) > 0.0

        @pl.when(any_tied)
        def _():
            store(h, _route_head_exact(sc_ref[0], sc_ref[1], pos_ref[...], neg_ref[...]))

        @pl.when(jnp.logical_not(any_tied))
        def _():
            store(h, maps)

        return 0

    lax.fori_loop(0, PEER_HEADS, head_body, 0)


def _peer_route(h2, wq_t, keys, tt=ROUTE_TILE):
    T, D = h2.shape
    pos, neg = _cand_tables(tt)
    stat_spec = pl.BlockSpec((PEER_HEADS, PEER_N_KEYS, tt), lambda i: (0, 0, i))
    stat = lambda dt: jax.ShapeDtypeStruct((PEER_HEADS, PEER_N_KEYS, T), dt)
    return pl.pallas_call(
        _route_kernel,
        grid=(T // tt,),
        in_specs=[pl.BlockSpec((tt, D), lambda i: (i, 0)),
                  pl.BlockSpec(wq_t.shape, lambda i: (0, 0)),
                  pl.BlockSpec(keys.shape, lambda i: (0, 0, 0)),
                  pl.BlockSpec((_CAND_ROWS, tt), lambda i: (0, 0)),
                  pl.BlockSpec((_CAND_ROWS, tt), lambda i: (0, 0))],
        out_specs=[stat_spec] * 4,
        out_shape=[stat(F32), stat(F32), stat(BF16), stat(BF16)],
        scratch_shapes=[pltpu.VMEM((PEER_HEADS * PEER_QUERY_DIM, tt), BF16),
                        pltpu.VMEM((2, PEER_N_KEYS, tt), F32)],
        compiler_params=_cparams(("parallel",)),
        name="peer_route",
    )(h2, wq_t, keys, pos, neg)


_KEY_GROUP = 16
_UNITS = 4
_DRAIN_PIECES = 2


def _build_gated(a_ref, p_ref, c1_ref, e1_ref, r2_ref, e2_ref, key0, g0, ng, lt):
    rep = PEER_N_KEYS // BF16_ROWS

    def rows16(row):
        blk = jnp.broadcast_to(row, (BF16_ROWS, LANES)).astype(BF16)
        return jnp.concatenate([blk] * rep, axis=0)

    ls = slice(lt * LANES, (lt + 1) * LANES)
    w = [jnp.zeros((PEER_N_KEYS, LANES), BF16) for _ in range(ng)]
    for h in range(PEER_HEADS):
        c1 = c1_ref[h, pl.ds(key0, _KEY_GROUP), ls]
        e1 = e1_ref[h, pl.ds(key0, _KEY_GROUP), ls]
        r2 = r2_ref[h, :, ls]
        e2 = e2_ref[h, :, ls]
        for g in range(ng):
            thr = rows16(c1[g0 + g:g0 + g + 1])
            gate = rows16(e1[g0 + g:g0 + g + 1])
            w[g] = w[g] + jnp.where(r2 < thr, e2, jnp.zeros_like(e2)) * gate
    for g in range(g0, g0 + ng):
        rs = slice(g * PEER_N_KEYS, (g + 1) * PEER_N_KEYS)
        a = a_ref[rs, ls]
        gelu = 0.5 * a * (1.0 + lax.erf(a * (2.0 ** -0.5)))
        p_ref[rs, ls] = gelu.astype(BF16) * w[g - g0]


def _experts_kernel(h_ref, dn_ref, upt_ref, c1_ref, e1_ref, r2_ref, e2_ref, x_ref, *rest, emit_norm):
    if emit_norm:
        g_ref, o_ref, hn_ref, a_ref, p_ref, acc_ref = rest
    else:
        o_ref, a_ref, p_ref, acc_ref = rest
    te, tt = a_ref.shape
    d_model = acc_ref.shape[0]
    e = pl.program_id(1)
    n_tiles = pl.num_programs(1) - 1
    cur = e % 2
    ng = _KEY_GROUP // _UNITS
    n_lane = tt // LANES
    key0 = pl.multiple_of(jnp.minimum(e, n_tiles - 1) * _KEY_GROUP, _KEY_GROUP)

    def front_mm(u):
        rows = te // _UNITS
        rs = slice(u * rows, (u + 1) * rows)
        a_ref[rs, :] = _dot_nt(dn_ref[rs, :], h_ref[...])

    def back_mm(r):
        rows = d_model // _DRAIN_PIECES
        rs = slice(r * rows, (r + 1) * rows)
        acc_ref[rs, :] += _dot(upt_ref[rs, :], p_ref[1 - cur])

    units_per_drain = _UNITS // _DRAIN_PIECES

    def run(front, back):
        if front:
            front_mm(0)
        for u in range(_UNITS):
            for lt in range(n_lane):
                if front:
                    _build_gated(a_ref, p_ref.at[cur], c1_ref, e1_ref, r2_ref, e2_ref, key0, u * ng, ng, lt)
                if front and lt == 0 and u + 1 < _UNITS:
                    front_mm(u + 1)
                if back and lt == n_lane // 2 and (u + 1) % units_per_drain == 0:
                    back_mm(u // units_per_drain)

    @pl.when(e == 0)
    def _():
        acc_ref[...] = jnp.zeros_like(acc_ref)
        run(True, False)

    @pl.when((e > 0) & (e < n_tiles))
    def _():
        run(True, True)

    @pl.when(e == n_tiles)
    def _():
        run(False, True)
        y = x_ref[...] + acc_ref[...].T
        o_ref[...] = y
        if emit_norm:
            hn_ref[...] = _rms_normed(y, g_ref[...])


def _peer_experts(h2, down, up_t, stats, x, next_gain, tt=EXPERT_TOKEN_TILE):
    T, D = h2.shape
    E = down.shape[0]
    te = _KEY_GROUP * PEER_N_KEYS
    n_tiles = E // te
    emit_norm = next_gain is not None
    stat_spec = pl.BlockSpec((PEER_HEADS, PEER_N_KEYS, tt), lambda i, e: (0, 0, i))
    row_spec = pl.BlockSpec((tt, D), lambda i, e: (i, 0))
    in_specs = [row_spec,
                pl.BlockSpec((te, D), lambda i, e: (jnp.minimum(e, n_tiles - 1), 0)),
                pl.BlockSpec((None, D, te), lambda i, e: (jnp.maximum(e - 1, 0), 0, 0)),
                stat_spec, stat_spec, stat_spec, stat_spec,
                row_spec]
    args = [h2, down, up_t, *stats, x]
    out_specs = [row_spec]
    out_shape = [jax.ShapeDtypeStruct((T, D), F32)]
    if emit_norm:
        in_specs.append(pl.BlockSpec((1, D), lambda i, e: (0, 0)))
        args.append(next_gain.reshape(1, D).astype(F32))
        out_specs.append(row_spec)
        out_shape.append(jax.ShapeDtypeStruct((T, D), BF16))
    stat_bytes = 2 * PEER_HEADS * PEER_N_KEYS * tt * (4 + 2)
    block_bytes = (tt * D * 2 + 2 * te * D * 2 + stat_bytes + tt * D * 4
                   + tt * D * 4 + (tt * D * 2 + D * 4 if emit_norm else 0))
    scratch_bytes = te * tt * 4 + 2 * te * tt * 2 + D * tt * 4
    res = pl.pallas_call(
        functools.partial(_experts_kernel, emit_norm=emit_norm),
        grid=(T // tt, n_tiles + 1),
        in_specs=in_specs, out_specs=out_specs, out_shape=out_shape,
        scratch_shapes=[pltpu.VMEM((te, tt), F32),
                        pltpu.VMEM((2, te, tt), BF16),
                        pltpu.VMEM((D, tt), F32)],
        compiler_params=_cparams(("parallel", "arbitrary"), 2 * block_bytes + scratch_bytes + VMEM_SPILL_MARGIN),
        name="peer_experts",
    )(*args)
    return (res[0], res[1]) if emit_norm else (res[0], None)


def _tables_kernel(dn_ref, up_ref, dn_o, upt_o):
    dn_o[...] = dn_ref[...].astype(BF16)
    upt_o[...] = up_ref[...].T.astype(BF16)


def _expert_tables(down, up, rows=1024):
    E, D = down.shape
    te = _KEY_GROUP * PEER_N_KEYS
    per_tile = te // rows
    row_spec = pl.BlockSpec((rows, D), lambda i: (i, 0))
    return pl.pallas_call(
        _tables_kernel,
        grid=(E // rows,),
        in_specs=[row_spec, row_spec],
        out_specs=[row_spec, pl.BlockSpec((None, D, rows), lambda i: (i // per_tile, 0, i % per_tile))],
        out_shape=[jax.ShapeDtypeStruct((E, D), BF16), jax.ShapeDtypeStruct((E // te, D, te), BF16)],
        compiler_params=_cparams(("parallel",)),
        name="expert_tables",
    )(down, up)


def _peer_layer(x, h2, next_gain, w_query, sub_keys, down, up):
    keys = sub_keys.reshape(PEER_HEADS * 2, PEER_N_KEYS, PEER_QUERY_DIM // 2).astype(BF16)
    stats = _peer_route(h2, w_query.T.astype(BF16), keys)
    down_bf, up_t = _expert_tables(down, up)
    return _peer_experts(h2, down_bf, up_t, stats, x, next_gain)


def _tile_heads(g, n):
    return jnp.tile(g.astype(F32), n)


def _even_mixer(x, h, ffn_gain, w_in, f_bias, qn_a, kn_a, qn_b, kn_b, w_out, tabs, *, batch, seq):
    w = w_in.astype(BF16)
    o_qa, o_ka, o_va, o_qb, o_kb, o_vb, o_gb, o_fb = (
        0, A_W, 2 * A_W, 3 * A_W, 3 * A_W + B_W, 3 * A_W + 2 * B_W, 3 * A_W + 3 * B_W, 3 * A_W + 4 * B_W)
    gain_a = jnp.concatenate([_tile_heads(qn_a, MOBA_HEADS), _tile_heads(kn_a, MOBA_HEADS)])
    qk_a, km = _proj(h, w[:, o_qa:o_va], seq=seq, tn=A_W, gain=gain_a, rope_tabs=tabs, kmean=True)
    nb = seq // MOBA_BLOCK
    kmean = km.reshape(batch, nb, 2 * A_W)[:, :, A_W:]
    kmean = jnp.pad(kmean, ((0, 0), (0, LANES - nb), (0, 0))).astype(BF16)
    gain_b = jnp.concatenate([_tile_heads(qn_b, FOX_HEADS), _tile_heads(kn_b, FOX_HEADS)])
    qk_b = _proj(h, w[:, o_qb:o_vb], seq=seq, tn=B_W, gain=gain_b)
    w_plain = jnp.concatenate([w[:, o_va:o_qb], w[:, o_vb:o_fb]], axis=1)
    vvg = _proj(h, w_plain, seq=seq, tn=A_W)
    va, vb, gb = vvg[:, :A_W], vvg[:, A_W:A_W + B_W], vvg[:, A_W + B_W:]
    frow = _fox_gates(h, w[:, o_fb:].T, f_bias, batch=batch, seq=seq)
    oa = _moba(qk_a[:, :A_W], qk_a[:, A_W:], va, kmean, batch=batch, seq=seq)
    logit_bound = 1.01 * ATTN_SCALE * HEAD_DIM * jnp.max(jnp.abs(qn_b)) * jnp.max(jnp.abs(kn_b)) + 0.01
    ob = _fox(qk_b[:, :B_W], qk_b[:, B_W:], vb, gb, frow, logit_bound, batch=batch, seq=seq)
    return _outproj([oa, ob], w_out.astype(BF16), x, ffn_gain)


def _odd_mixer(x, h, ffn_gain, w_in, qn, kn, sinks, w_out, tabs, *, batch, seq):
    w = w_in.astype(BF16)
    qw = SWA_Q_HEADS * HEAD_DIM
    kw = SWA_KV_HEADS * HEAD_DIM
    q = _proj(h, w[:, :qw], seq=seq, tn=512, gain=_tile_heads(qn, SWA_Q_HEADS), rope_tabs=tabs)
    k = _proj(h, w[:, qw:qw + kw], seq=seq, tn=kw, gain=_tile_heads(kn, SWA_KV_HEADS), rope_tabs=tabs)
    v = _proj(h, w[:, qw + kw:], seq=seq, tn=kw)
    o = _swa(q, k, v, sinks, batch=batch, seq=seq)
    return _outproj([o], w_out.astype(BF16), x, ffn_gain)


def kernel(x, attn_norm, ffn_norm, ev_w_in, ev_forget_bias, ev_q_norm_a, ev_k_norm_a, ev_q_norm_b,
           ev_k_norm_b, ev_w_out, od_w_in, od_q_norm, od_k_norm, od_sinks, od_w_out,
           peer_w_query, peer_sub_keys, peer_down, peer_up):
    batch, seq, d_model = x.shape
    depth = attn_norm.shape[0]
    tabs = _rope_tables(seq)
    xt = x.reshape(batch * seq, d_model)
    h = _rmsnorm(xt, attn_norm[0])
    for l in range(depth):
        i = l // 2
        if l % 2 == 0:
            xt, h2 = _even_mixer(xt, h, ffn_norm[l], ev_w_in[i], ev_forget_bias[i], ev_q_norm_a[i],
                                 ev_k_norm_a[i], ev_q_norm_b[i], ev_k_norm_b[i], ev_w_out[i], tabs,
                                 batch=batch, seq=seq)
        else:
            xt, h2 = _odd_mixer(xt, h, ffn_norm[l], od_w_in[i], od_q_norm[i], od_k_norm[i], od_sinks[i],
                                od_w_out[i], tabs, batch=batch, seq=seq)
        next_gain = attn_norm[l + 1] if l + 1 < depth else None
        xt, h = _peer_layer(xt, h2, next_gain, peer_w_query[l], peer_sub_keys[l], peer_down[l], peer_up[l])
    return xt.reshape(batch, seq, d_model)
```

```python
import functools

import numpy as np
import jax
import jax.numpy as jnp
from jax import lax
from jax.experimental import pallas as pl
from jax.experimental.pallas import tpu as pltpu

F32 = jnp.float32
BF16 = jnp.bfloat16

HEAD_DIM = 64
ROT_DIM = HEAD_DIM // 4
ROPE_THETA = 500000.0
ATTN_SCALE = HEAD_DIM ** -0.5
EPS = 1e-6
NEG_INF = -1e30

MOBA_HEADS = 8
FOX_HEADS = 8
MOBA_BLOCK = 256
MOBA_TOPK = 3
A_W = MOBA_HEADS * HEAD_DIM
B_W = FOX_HEADS * HEAD_DIM

SWA_Q_HEADS = 16
SWA_KV_HEADS = 2
SWA_WINDOW = 128

PEER_HEADS = 8
PEER_N_KEYS = 128
PEER_TOPK = 16
PEER_QUERY_DIM = 128

LANES = 128
BF16_ROWS = 16
PAIR_W = 2 * HEAD_DIM

ROW_TILE = 512
FLASH_TILE = 2 * MOBA_BLOCK
ROUTE_TILE = 4 * LANES
EXPERT_TOKEN_TILE = 512
VMEM_LIMIT = 48 * 1024 * 1024
VMEM_SPILL_MARGIN = 4 * 1024 * 1024


def _cparams(sem, vmem_limit=VMEM_LIMIT):
    return pltpu.CompilerParams(dimension_semantics=sem, vmem_limit_bytes=vmem_limit)


def _dot_nt(a, b):
    return lax.dot_general(a, b, (((1,), (1,)), ((), ())), preferred_element_type=F32)


def _dot(a, b):
    return jnp.dot(a, b, preferred_element_type=F32)


def _split2(x):
    h1 = x.astype(BF16)
    return h1, (x - h1.astype(F32)).astype(BF16)


def _split3(x):
    h1 = x.astype(BF16)
    r1 = x - h1.astype(F32)
    h2 = r1.astype(BF16)
    h3 = (r1 - h2.astype(F32)).astype(BF16)
    return h1, h2, h3


def _rmsnorm_kernel(x_ref, g_ref, o_ref):
    x = x_ref[...]
    ms = jnp.mean(x * x, axis=-1, keepdims=True)
    o_ref[...] = (x * lax.rsqrt(ms + EPS) * g_ref[...]).astype(o_ref.dtype)


def _rmsnorm(x, gain, tm=ROW_TILE):
    T, D = x.shape
    return pl.pallas_call(
        _rmsnorm_kernel,
        grid=(T // tm,),
        in_specs=[pl.BlockSpec((tm, D), lambda i: (i, 0)),
                  pl.BlockSpec((1, D), lambda i: (0, 0))],
        out_specs=pl.BlockSpec((tm, D), lambda i: (i, 0)),
        out_shape=jax.ShapeDtypeStruct((T, D), BF16),
        compiler_params=_cparams(("parallel",)),
        name="rmsnorm",
    )(x, gain.reshape(1, D))


def _proj_kernel(*refs, norm, rope, kmean, tn):
    it = iter(refs)
    h_ref, w_ref = next(it), next(it)
    gain_ref = next(it) if norm else None
    bd_ref = next(it) if norm else None
    if rope:
        c_ref, sa_ref, sb_ref = next(it), next(it), next(it)
    o_ref = next(it)
    km_ref = next(it) if kmean else None

    y = _dot(h_ref[...], w_ref[...])
    if norm:
        y2 = y * y
        bd = bd_ref[...]
        cols = []
        tm = y.shape[0]
        for c in range(tn // LANES):
            h1, h2 = _split2(y2[:, c * LANES:(c + 1) * LANES])
            t = _dot(jnp.concatenate([h1, h2], axis=0), bd)
            cols.append(t[:tm] + t[tm:])
        ms = cols[0] if len(cols) == 1 else jnp.concatenate(cols, axis=1)
        y = y * lax.rsqrt(ms + EPS) * gain_ref[...]
    if rope:
        rep = tn // LANES
        tile = (lambda t: t) if rep == 1 else (lambda t: jnp.concatenate([t] * rep, axis=1))
        y = (y * tile(c_ref[...])
             + pltpu.roll(y, tn - ROT_DIM // 2, 1) * tile(sa_ref[...])
             + pltpu.roll(y, ROT_DIM // 2, 1) * tile(sb_ref[...]))
    o_ref[...] = y.astype(o_ref.dtype)
    if kmean:
        for r in range(km_ref.shape[0]):
            km_ref[r] = jnp.mean(y[r * MOBA_BLOCK:(r + 1) * MOBA_BLOCK], axis=0, keepdims=True)


def _proj(h, w, *, seq, tn, tm=ROW_TILE, gain=None, rope_tabs=None, kmean=False):
    T, D = h.shape
    N = w.shape[1]
    norm = gain is not None
    rope = rope_tabs is not None
    nseq = seq // tm
    in_specs = [pl.BlockSpec((tm, D), lambda i, j: (i, 0)),
                pl.BlockSpec((D, tn), lambda i, j: (0, j))]
    args = [h, w]
    if norm:
        bd = np.kron(np.eye(LANES // HEAD_DIM), np.ones((HEAD_DIM, HEAD_DIM))) / HEAD_DIM
        in_specs += [pl.BlockSpec((1, tn), lambda i, j: (0, j)),
                     pl.BlockSpec((LANES, LANES), lambda i, j: (0, 0))]
        args += [gain.reshape(1, N).astype(F32), jnp.asarray(bd, BF16)]
    if rope:
        in_specs += [pl.BlockSpec((tm, LANES), lambda i, j: (i % nseq, 0))] * 3
        args += list(rope_tabs)
    out_specs = [pl.BlockSpec((tm, tn), lambda i, j: (i, j))]
    out_shape = [jax.ShapeDtypeStruct((T, N), BF16)]
    if kmean:
        out_specs.append(pl.BlockSpec((tm // MOBA_BLOCK, 1, tn), lambda i, j: (i, 0, j)))
        out_shape.append(jax.ShapeDtypeStruct((T // MOBA_BLOCK, 1, N), F32))
    res = pl.pallas_call(
        functools.partial(_proj_kernel, norm=norm, rope=rope, kmean=kmean, tn=tn),
        grid=(T // tm, N // tn),
        in_specs=in_specs, out_specs=out_specs, out_shape=out_shape,
        compiler_params=_cparams(("parallel", "parallel")),
        name="proj",
    )(*args)
    return res if kmean else res[0]


def _rope_tables(seq):
    half = ROT_DIM // 2
    inv_freq = jnp.power(ROPE_THETA, -jnp.arange(0, ROT_DIM, 2, dtype=F32) / ROT_DIM)
    ang = jnp.arange(seq, dtype=F32)[:, None] * inv_freq[None, :]
    cos, sin = jnp.cos(ang), jnp.sin(ang)
    one = jnp.ones((seq, HEAD_DIM - ROT_DIM), F32)
    zero = jnp.zeros((seq, HEAD_DIM - ROT_DIM), F32)
    z8 = jnp.zeros((seq, half), F32)
    c = jnp.concatenate([cos, cos, one], axis=1)
    sa = jnp.concatenate([-sin, z8, zero], axis=1)
    sb = jnp.concatenate([z8, sin, zero], axis=1)
    rep = LANES // HEAD_DIM
    return tuple(jnp.concatenate([t] * rep, axis=1) for t in (c, sa, sb))


def _gates_kernel(h_ref, wf_ref, b_ref, tri_ref, o_ref, carry_ref):
    @pl.when(pl.program_id(1) == 0)
    def _():
        carry_ref[...] = jnp.zeros_like(carry_ref)

    z = _dot_nt(wf_ref[...], h_ref[...]) + b_ref[...][:, :1]
    lf = jnp.minimum(z, 0.0) - jnp.log1p(jnp.exp(-jnp.abs(z)))
    tri = tri_ref[...]
    h1, h2, h3 = _split3(lf)
    cs = _dot(h1, tri) + _dot(h2, tri) + _dot(h3, tri) + carry_ref[...][:, :1]
    o_ref[0] = cs
    carry_ref[...] = jnp.broadcast_to(cs[:, -1:], carry_ref.shape)


def _fox_gates(h, wf_t, bias, *, batch, seq, tm=ROW_TILE):
    T, D = h.shape
    nh = wf_t.shape[0]
    nseq = seq // tm
    tri = jnp.asarray(np.triu(np.ones((tm, tm))), BF16)
    return pl.pallas_call(
        _gates_kernel,
        grid=(batch, nseq),
        in_specs=[pl.BlockSpec((tm, D), lambda b, s: (b * nseq + s, 0)),
                  pl.BlockSpec((nh, D), lambda b, s: (0, 0)),
                  pl.BlockSpec((nh, LANES), lambda b, s: (0, 0)),
                  pl.BlockSpec((tm, tm), lambda b, s: (0, 0))],
        out_specs=pl.BlockSpec((1, nh, tm), lambda b, s: (b, 0, s)),
        out_shape=jax.ShapeDtypeStruct((batch, nh, seq), F32),
        scratch_shapes=[pltpu.VMEM((nh, LANES), F32)],
        compiler_params=_cparams(("parallel", "arbitrary")),
        name="fox_gates",
    )(h, wf_t, jnp.broadcast_to(bias.astype(F32)[:, None], (nh, LANES)), tri)


def _lane_tile(x, width):
    rep = width // LANES
    return x if rep == 1 else jnp.concatenate([x] * rep, axis=1)


def _flash_init(m_ref, acc_ref):
    m_ref[...] = jnp.full(m_ref.shape, NEG_INF, F32)
    acc_ref[...] = jnp.zeros(acc_ref.shape, F32)


def _head_values(v):
    lane = lax.broadcasted_iota(jnp.int32, v.shape, 1)
    return [jnp.where((lane >= hh * HEAD_DIM) & (lane < (hh + 1) * HEAD_DIM), v, jnp.ones_like(v))
            for hh in range(2)]


def _flash_update(slot, s, v, m_ref, acc_ref):
    tk = s.shape[1]
    m_prev = m_ref[slot]
    m_new = jnp.maximum(m_prev, jnp.max(s, axis=1, keepdims=True))
    alpha = jnp.exp(m_prev - m_new)
    p = jnp.exp(s - _lane_tile(m_new, tk))
    acc_ref[slot] = alpha * acc_ref[slot] + _dot(p.astype(BF16), v)
    m_ref[slot] = m_new


def _flash_finish(lane, acc_ref):
    outs = []
    for hh in range(2):
        acc = acc_ref[hh]
        den = (1 - hh) * HEAD_DIM
        outs.append(acc / acc[:, den:den + 1])
    return jnp.where(lane < HEAD_DIM, outs[0], outs[1])


def _flash_tiles(qi, tk, logits, values, m_ref, acc_ref):
    def absorb(s, off):
        vh = values(off)
        for hh in range(2):
            _flash_update(hh, s[hh], vh[hh], m_ref, acc_ref)

    def pair(off_a, off_b, b_diagonal):
        sa = logits(off_a, False)
        sb = logits(off_b, b_diagonal)
        absorb(sa, off_a)
        absorb(sb, off_b)

    def body(jj, carry):
        off = pl.multiple_of(2 * jj * tk, 2 * tk)
        pair(off, off + tk, False)
        return carry

    lax.fori_loop(0, qi // 2, body, 0)
    diag = pl.multiple_of(qi * tk, tk)

    @pl.when(qi % 2 == 1)
    def _():
        pair(diag - tk, diag, True)

    @pl.when(qi % 2 == 0)
    def _():
        absorb(logits(diag, True), diag)


def _head_queries(q, lane):
    qs = q * ATTN_SCALE
    return [jnp.where((lane >= hh * HEAD_DIM) & (lane < (hh + 1) * HEAD_DIM), qs, jnp.zeros_like(qs))
            for hh in range(2)]


def _moba_kernel(q_ref, k_ref, v_ref, km_ref, o_ref, m_ref, acc_ref):
    tq = q_ref.shape[1]
    tk = tq
    qi = pl.program_id(2)
    lane = lax.broadcasted_iota(jnp.int32, (tq, LANES), 1)
    lane_f = lane.astype(F32)
    rowv = lax.broadcasted_iota(jnp.int32, (tq, LANES), 0)
    row_blk = 2 * qi + (rowv >= MOBA_BLOCK).astype(jnp.int32)
    row = lax.broadcasted_iota(jnp.int32, (tq, tk), 0)
    col = lax.broadcasted_iota(jnp.int32, (tq, tk), 1)
    qh = _head_queries(q_ref[0], lane)
    _flash_init(m_ref, acc_ref)

    sels = []
    for hh in range(2):
        gate = _dot_nt(qh[hh], km_ref[0])
        gate = jnp.where(lane < row_blk, gate, -jnp.inf)
        sel = jnp.zeros((tq, LANES), F32)
        for _ in range(MOBA_TOPK):
            m = jnp.max(gate, axis=1, keepdims=True)
            idx = jnp.min(jnp.where(gate == m, lane_f, float(LANES)), axis=1, keepdims=True)
            hit = lane_f == idx
            sel = jnp.where(hit & (m > -jnp.inf), 1.0, sel)
            gate = jnp.where(hit, -jnp.inf, gate)
        sels.append(sel)

    def chosen(sel, blk):
        return jnp.max(jnp.where(lane == blk, sel, 0.0), axis=1, keepdims=True) > 0.0

    def logits(off, diagonal):
        kj = k_ref[0, pl.ds(off, tk), :]
        out = []
        for hh in range(2):
            s = _dot_nt(qh[hh], kj)
            if diagonal:
                visible = (col >= MOBA_BLOCK) | (row < MOBA_BLOCK) | chosen(sels[hh], 2 * qi)
                s = jnp.where((col <= row) & visible, s, NEG_INF)
            else:
                blk = 2 * (off // tk)
                s = jnp.concatenate(
                    [jnp.where(chosen(sels[hh], blk), s[:, :MOBA_BLOCK], NEG_INF),
                     jnp.where(chosen(sels[hh], blk + 1), s[:, MOBA_BLOCK:], NEG_INF)], axis=1)
            out.append(s)
        return out

    def values(off):
        return _head_values(v_ref[0, pl.ds(off, tk), :])

    _flash_tiles(qi, tk, logits, values, m_ref, acc_ref)
    o_ref[0] = _flash_finish(lane, acc_ref).astype(o_ref.dtype)


def _flash_scratch(tq):
    return [pltpu.VMEM((2, tq, LANES), F32)] * 2


def _moba(q, k, v, kmean, *, batch, seq):
    W = q.shape[1]
    tq = FLASH_TILE
    q3, k3, v3 = (t.reshape(batch, seq, W) for t in (q, k, v))
    out = pl.pallas_call(
        _moba_kernel,
        grid=(batch, W // PAIR_W, seq // tq),
        in_specs=[pl.BlockSpec((1, tq, PAIR_W), lambda b, p, i: (b, i, p)),
                  pl.BlockSpec((1, seq, PAIR_W), lambda b, p, i: (b, 0, p)),
                  pl.BlockSpec((1, seq, PAIR_W), lambda b, p, i: (b, 0, p)),
                  pl.BlockSpec((1, LANES, PAIR_W), lambda b, p, i: (b, 0, p))],
        out_specs=pl.BlockSpec((1, tq, PAIR_W), lambda b, p, i: (b, i, p)),
        out_shape=jax.ShapeDtypeStruct((batch, seq, W), BF16),
        scratch_shapes=_flash_scratch(tq),
        compiler_params=_cparams(("parallel", "parallel", "parallel")),
        name="moba",
    )(q3, k3, v3, kmean)
    return out.reshape(batch * seq, W)


_EXP_UNDERFLOW = 112.0


def _fox_kernel(q_ref, k_ref, v_ref, g_ref, frow_ref, far_ref, o_ref, m_ref, acc_ref):
    tq = q_ref.shape[1]
    tk = tq
    pr = pl.program_id(1)
    qi = pl.program_id(2)
    lane = lax.broadcasted_iota(jnp.int32, (tq, LANES), 1)
    row = lax.broadcasted_iota(jnp.int32, (tq, tk), 0)
    col = lax.broadcasted_iota(jnp.int32, (tq, tk), 1)
    qh = _head_queries(q_ref[0], lane)
    _flash_init(m_ref, acc_ref)

    def key_gates(off, width):
        f_all = frow_ref[0, :, pl.ds(off, width)]
        sub = lax.broadcasted_iota(jnp.int32, f_all.shape, 0)
        return [jnp.sum(jnp.where(sub == 2 * pr + hh, f_all, 0.0), axis=0, keepdims=True) for hh in range(2)]

    f_ref = [f[:, :1] for f in key_gates(pl.multiple_of(qi * tq, tq), LANES)]

    def logits(off, diagonal):
        kj = k_ref[0, pl.ds(off, tk), :]
        fk = key_gates(off, tk)
        out = []
        for hh in range(2):
            s = _dot_nt(qh[hh], kj) - (fk[hh] - f_ref[hh])
            out.append(jnp.where(col <= row, s, NEG_INF) if diagonal else s)
        return out

    def absorb(s, off):
        vh = _head_values(v_ref[0, pl.ds(off, tk), :])
        for hh in range(2):
            _flash_update(hh, s[hh], vh[hh], m_ref, acc_ref)

    def vanishes(j):
        tail = frow_ref[0, :, pl.ds(pl.multiple_of((j + 1) * tk - LANES, LANES), LANES)]
        sub = lax.broadcasted_iota(jnp.int32, tail.shape, 0)
        last = lax.broadcasted_iota(jnp.int32, tail.shape, 1) == LANES - 1
        gap = tail - jnp.where(sub == 2 * pr, f_ref[0], f_ref[1]) - far_ref[...][:, :1]
        mine = last & ((sub == 2 * pr) | (sub == 2 * pr + 1))
        return jnp.min(jnp.where(mine, gap, jnp.inf)) >= 0.0

    absorb(logits(pl.multiple_of(qi * tk, tk), True), pl.multiple_of(qi * tk, tk))

    def more(j):
        return jnp.logical_and(j >= 1, jnp.logical_not(vanishes(jnp.maximum(j, 0))))

    def pair(j):
        off_a = pl.multiple_of(j * tk, tk)
        off_b = pl.multiple_of((j - 1) * tk, tk)
        sa = logits(off_a, False)
        sb = logits(off_b, False)
        absorb(sa, off_a)
        absorb(sb, off_b)
        return j - 2

    j = lax.while_loop(more, pair, qi - 1)

    @pl.when(jnp.logical_and(j == 0, jnp.logical_not(vanishes(0))))
    def _():
        absorb(logits(0, False), 0)

    o = _flash_finish(lane, acc_ref)
    o_ref[0] = (o * jax.nn.sigmoid(g_ref[0].astype(F32))).astype(o_ref.dtype)


def _fox(q, k, v, g, frow, logit_bound, *, batch, seq, tq=FLASH_TILE):
    far = jnp.full((1, LANES), 2.0 * logit_bound + _EXP_UNDERFLOW, F32)
    W = q.shape[1]
    nh = frow.shape[1]
    q3, k3, v3, g3 = (t.reshape(batch, seq, W) for t in (q, k, v, g))
    out = pl.pallas_call(
        _fox_kernel,
        grid=(batch, W // PAIR_W, seq // tq),
        in_specs=[pl.BlockSpec((1, tq, PAIR_W), lambda b, p, i: (b, i, p)),
                  pl.BlockSpec((1, seq, PAIR_W), lambda b, p, i: (b, 0, p)),
                  pl.BlockSpec((1, seq, PAIR_W), lambda b, p, i: (b, 0, p)),
                  pl.BlockSpec((1, tq, PAIR_W), lambda b, p, i: (b, i, p)),
                  pl.BlockSpec((1, nh, seq), lambda b, p, i: (b, 0, 0)),
                  pl.BlockSpec((1, LANES), lambda b, p, i: (0, 0))],
        out_specs=pl.BlockSpec((1, tq, PAIR_W), lambda b, p, i: (b, i, p)),
        out_shape=jax.ShapeDtypeStruct((batch, seq, W), BF16),
        scratch_shapes=_flash_scratch(tq),
        compiler_params=_cparams(("parallel", "parallel", "parallel")),
        name="fox",
    )(q3, k3, v3, g3, frow, far)
    return out.reshape(batch * seq, W)


def _swa_kernel(q_ref, k_ref, v_ref, sink_ref, bias_ref, o_ref):
    tq = q_ref.shape[1]
    qi = pl.program_id(1)
    group = SWA_Q_HEADS // SWA_KV_HEADS
    tk = tq + SWA_WINDOW
    lane = lax.broadcasted_iota(jnp.int32, (tq, LANES), 1)
    kstart = pl.multiple_of(jnp.maximum(qi * tq - SWA_WINDOW, 0), SWA_WINDOW)
    k = k_ref[0, pl.ds(kstart, tk), :]
    vh = _head_values(v_ref[0, pl.ds(kstart, tk), :])
    swap = lambda t: pltpu.roll(t.astype(F32), HEAD_DIM, 1).astype(BF16)
    k_by_half = [k, swap(k)]
    bias = bias_ref[jnp.minimum(qi, 1)]
    bias = jnp.concatenate([bias] * (group // 2), axis=0)
    sink_tab = sink_ref[...]

    chains = []
    for c in range(SWA_KV_HEADS):
        v_by_half = [vh[c], swap(vh[c])]
        for swapped in range(2):
            hh = c if not swapped else 1 - c
            heads = [h for h in range(c * group, (c + 1) * group) if h % 2 == hh]
            pieces, sinks = [], []
            for head in heads:
                blk = q_ref[0, :, (head // 2) * PAIR_W:(head // 2 + 1) * PAIR_W] * ATTN_SCALE
                pieces.append(jnp.where((lane >= hh * HEAD_DIM) & (lane < (hh + 1) * HEAD_DIM),
                                        blk, jnp.zeros_like(blk)))
                sinks.append(jnp.broadcast_to(sink_tab[head:head + 1, :], (tq, LANES)))
            s = _dot_nt(jnp.concatenate(pieces, axis=0), k_by_half[swapped]) + bias
            chains.append((heads, s, jnp.concatenate(sinks, axis=0), v_by_half[swapped]))

    outs = [None] * SWA_Q_HEADS
    for heads, s, sink, v in chains:
        m = jnp.maximum(sink, jnp.max(s, axis=1, keepdims=True))
        p = jnp.exp(s - _lane_tile(m, tk))
        acc = _dot(p.astype(BF16), v)
        den = pltpu.roll(acc, HEAD_DIM, 1) + jnp.exp(sink - m)
        o = acc / den
        for idx, head in enumerate(heads):
            outs[head] = o[idx * tq:(idx + 1) * tq]
    for pp in range(SWA_Q_HEADS // 2):
        o_ref[0, :, pp * PAIR_W:(pp + 1) * PAIR_W] = jnp.where(
            lane < HEAD_DIM, outs[2 * pp], outs[2 * pp + 1]).astype(o_ref.dtype)


def _swa_bias(tq):
    r = np.arange(tq)[:, None]
    c = np.arange(tq + SWA_WINDOW)[None, :]
    tabs = []
    for key_offset in (0, SWA_WINDOW):
        dist = r + key_offset - c
        tabs.append(np.where((dist >= 0) & (dist < SWA_WINDOW), 0.0, NEG_INF))
    return jnp.asarray(np.stack(tabs), F32)


def _swa(q, k, v, sinks, *, batch, seq, tq=SWA_WINDOW):
    W = q.shape[1]
    tk = tq + SWA_WINDOW
    q3 = q.reshape(batch, seq, W)
    k3, v3 = (t.reshape(batch, seq, PAIR_W) for t in (k, v))
    sink_tab = jnp.broadcast_to(sinks.astype(F32)[:, None], (SWA_Q_HEADS, LANES))
    out = pl.pallas_call(
        _swa_kernel,
        grid=(batch, seq // tq),
        in_specs=[pl.BlockSpec((1, tq, W), lambda b, i: (b, i, 0)),
                  pl.BlockSpec((1, seq, PAIR_W), lambda b, i: (b, 0, 0)),
                  pl.BlockSpec((1, seq, PAIR_W), lambda b, i: (b, 0, 0)),
                  pl.BlockSpec((SWA_Q_HEADS, LANES), lambda b, i: (0, 0)),
                  pl.BlockSpec((2, tq, tk), lambda b, i: (0, 0, 0))],
        out_specs=pl.BlockSpec((1, tq, W), lambda b, i: (b, i, 0)),
        out_shape=jax.ShapeDtypeStruct((batch, seq, W), BF16),
        compiler_params=_cparams(("parallel", "parallel")),
        name="swa",
    )(q3, k3, v3, sink_tab, _swa_bias(tq))
    return out.reshape(batch * seq, W)


def _rms_normed(x, gain):
    ms = jnp.mean(x * x, axis=-1, keepdims=True)
    return (x * lax.rsqrt(ms + EPS) * gain).astype(BF16)


def _outproj_kernel(*refs, n_parts):
    parts = refs[:n_parts]
    w_ref, x_ref, g_ref, o_ref, h_ref = refs[n_parts:]
    y = x_ref[...]
    off = 0
    for p_ref in parts:
        kw = p_ref.shape[1]
        y = y + _dot(p_ref[...], w_ref[off:off + kw, :])
        off += kw
    o_ref[...] = y
    h_ref[...] = _rms_normed(y, g_ref[...])


def _outproj(parts, w, x, next_gain, tm=ROW_TILE):
    T, D = x.shape
    in_specs = [pl.BlockSpec((tm, p.shape[1]), lambda i: (i, 0)) for p in parts]
    in_specs += [pl.BlockSpec(w.shape, lambda i: (0, 0)),
                 pl.BlockSpec((tm, D), lambda i: (i, 0)),
                 pl.BlockSpec((1, D), lambda i: (0, 0))]
    row_spec = pl.BlockSpec((tm, D), lambda i: (i, 0))
    return pl.pallas_call(
        functools.partial(_outproj_kernel, n_parts=len(parts)),
        grid=(T // tm,),
        in_specs=in_specs,
        out_specs=[row_spec, row_spec],
        out_shape=[jax.ShapeDtypeStruct((T, D), F32), jax.ShapeDtypeStruct((T, D), BF16)],
        compiler_params=_cparams(("parallel",)),
        name="outproj",
    )(*parts, w, x, next_gain.reshape(1, D).astype(F32))


_CAND_ROWS = 80


def _cand_tables(lanes):
    pos = np.zeros((_CAND_ROWS,), np.float32)
    neg = np.zeros((_CAND_ROWS,), np.float32)
    r = 0
    for a, nb in ((0, 16), (1, 8), (2, 8), (3, 8), (4, 8), (5, 8), (6, 8), (7, 8)):
        for b in range(nb):
            pos[r] = a * PEER_TOPK + b
            neg[r] = 0.0 if (a + 1) * (b + 1) <= PEER_TOPK else -np.inf
            r += 1
    for a in range(8, 16):
        pos[r] = a * PEER_TOPK
        r += 1
    assert r == _CAND_ROWS
    tab = lambda t: jnp.asarray(np.broadcast_to(t[:, None], (_CAND_ROWS, lanes)).copy())
    return tab(pos), tab(neg)


def _batcher_pairs(n):
    pairs, p = [], 1
    while p < n:
        k = p
        while k >= 1:
            for j in range(k % p, n - k, 2 * k):
                for i in range(min(k, n - j - k)):
                    if (i + j) // (2 * p) == (i + j + k) // (2 * p):
                        pairs.append((i + j, i + j + k))
            k //= 2
        p *= 2
    return pairs


_SORT16 = _batcher_pairs(PEER_TOPK)
_SUBLANES = 8
_N_CAND_PIECES = _CAND_ROWS // _SUBLANES
_SORT10 = [(i, j) for i, j in _SORT16 if j < _N_CAND_PIECES]


def _compare_exchange(items, i, j):
    items[i], items[j] = jnp.maximum(items[i], items[j]), jnp.minimum(items[i], items[j])


def _top_sorted(pieces, pairs):
    items = list(pieces)
    for i, j in pairs:
        _compare_exchange(items, i, j)
    n = PEER_TOPK
    items += [jnp.full(items[0].shape, -jnp.inf, F32)] * (n - len(items))
    for shift in (4, 2, 1):
        items = [jnp.maximum(items[i], pltpu.roll(items[n - 1 - i], shift, 0)) for i in range(n)]
        d = n // 2
        while d >= 1:
            for i in range(n):
                if i & d == 0:
                    _compare_exchange(items, i, i + d)
            d //= 2
    return items


def _pieces(x):
    return [x[_SUBLANES * g:_SUBLANES * (g + 1)] for g in range(x.shape[0] // _SUBLANES)]


def _sublane_total(x):
    for shift in (4, 2, 1):
        x = x + pltpu.roll(x, shift, 0)
    return x


def _count_ge(pieces, thr):
    total = jnp.zeros(thr.shape, F32)
    for p in pieces:
        total = total + jnp.where(p >= thr, 1.0, 0.0)
    return _sublane_total(total)


def _route_head_fast(s1, s2):
    p1, p2 = _pieces(s1), _pieces(s2)
    v1 = _top_sorted(p1, _SORT16)
    v2 = _top_sorted(p2, _SORT16)
    sub = lax.broadcasted_iota(jnp.int32, v1[0].shape, 0)

    def spread(vals):
        out = vals[0]
        for r in range(1, _SUBLANES):
            out = jnp.where(sub == r, vals[r], out)
        return out

    v2_lo, v2_hi, v1_hi = spread(v2[:8]), spread(v2[8:]), spread(v1[8:])
    cands = [v1[0] + v2_lo, v1[0] + v2_hi, v1[1] + v2_lo]
    for a in range(2, 8):
        cands.append(jnp.where(sub < PEER_TOPK // (a + 1), v1[a] + v2_lo, -jnp.inf))
    cands.append(v1_hi + v2[0])
    ts = _top_sorted(cands, _SORT10)
    tau = ts[PEER_TOPK - 1]
    z = jnp.exp(ts[0] - ts[0])
    for kk in range(1, PEER_TOPK):
        z = z + jnp.exp(ts[kk] - ts[0])

    tied = (_count_ge(p1, v1[-1]) != float(PEER_TOPK)) | (_count_ge(p2, v2[-1]) != float(PEER_TOPK))
    tied = tied | (_count_ge(cands, tau) != float(PEER_TOPK))
    for b in range(PEER_TOPK - 1):
        tied = tied | (v1[b] == v1[b + 1]) | (v2[b] == v2[b + 1])

    cnt = []
    for a in range(PEER_TOPK):
        c = jnp.zeros(tau.shape, F32)
        for b in range(PEER_TOPK // (a + 1)):
            c = c + jnp.where(v1[a] + v2[b] >= tau, 1.0, 0.0)
        cnt.append(c)
    c1, r2 = [], []
    for x in p1:
        c = jnp.zeros(x.shape, F32)
        for a in range(PEER_TOPK):
            c = jnp.where(x == v1[a], cnt[a], c)
        c1.append(c)
    for x in p2:
        r = jnp.zeros(x.shape, F32)
        for b in range(PEER_TOPK):
            r = r + jnp.where(v2[b] > x, 1.0, 0.0)
        r2.append(r)
    inv_z = 1.0 / z
    e1 = [jnp.exp(x - v1[0]) * inv_z for x in p1]
    e2 = [jnp.exp(x - v2[0]) for x in p2]
    cat = lambda ps: jnp.concatenate(ps, axis=0)
    return (cat(c1), cat(e1), cat(r2), cat(e2)), tied


def _extract_sorted(scores, by_key):
    nk, lanes = scores[0].shape
    kio = lax.broadcasted_iota(jnp.int32, (nk, lanes), 0).astype(F32)
    slot = lax.broadcasted_iota(jnp.int32, (PEER_TOPK, lanes), 0)

    def body(a, carry):
        here = slot == a
        out = []
        for (v, vals, aux), ranked in zip(carry, by_key):
            m = jnp.max(v, axis=0, keepdims=True)
            idx = jnp.min(jnp.where(v == m, kio, float(nk)), axis=0, keepdims=True)
            hit = kio == idx
            aux = jnp.where(hit, jnp.asarray(a, F32), aux) if ranked else jnp.where(here, idx, aux)
            out.append((jnp.where(hit, -jnp.inf, v), jnp.where(here, m, vals), aux))
        return tuple(out)

    small = jnp.zeros((PEER_TOPK, lanes), F32)
    unranked = jnp.full((nk, lanes), float(PEER_TOPK), F32)
    init = tuple((v, small, unranked if ranked else small) for v, ranked in zip(scores, by_key))
    return [(vals, aux) for _, vals, aux in lax.fori_loop(0, PEER_TOPK, body, init)]


def _route_head_exact(s1, s2, pos, neg):
    lanes = s1.shape[1]
    slot = lax.broadcasted_iota(jnp.int32, (PEER_TOPK, lanes), 0)
    kio = lax.broadcasted_iota(jnp.int32, (PEER_N_KEYS, lanes), 0).astype(F32)
    (v1, idx1), (v2, rank2) = _extract_sorted([s1, s2], [False, True])
    blocks = [v1[0:1] + v2[0:8], v1[0:1] + v2[8:16]]
    blocks += [v1[a:a + 1] + v2[0:8] for a in range(1, 8)]
    blocks += [v1[8:16] + v2[0:1]]
    cand = jnp.concatenate(blocks, axis=0) + neg

    def pick(kk, carry):
        cand, chosen, ts = carry
        m = jnp.max(cand, axis=0, keepdims=True)
        first = jnp.min(jnp.where(cand == m, pos, 1e9), axis=0, keepdims=True)
        hit = pos == first
        return (jnp.where(hit, -jnp.inf, cand), jnp.where(hit, 1.0, chosen), jnp.where(slot == kk, m, ts))

    _, chosen, ts = lax.fori_loop(0, PEER_TOPK, pick,
                                  (cand, jnp.zeros_like(cand), jnp.zeros((PEER_TOPK, lanes), F32)))
    z = jnp.sum(jnp.exp(ts - ts[0:1]), axis=0, keepdims=True)
    counts = [jnp.sum(chosen[0:16], axis=0, keepdims=True)]
    counts += [jnp.sum(chosen[8 * a + 8:8 * a + 16], axis=0, keepdims=True) for a in range(1, 8)]
    counts += [chosen[72 + a:73 + a] for a in range(8)]
    c1 = jnp.zeros((PEER_N_KEYS, lanes), F32)
    for a in range(PEER_TOPK):
        c1 = jnp.where(kio == idx1[a:a + 1], counts[a], c1)
    return c1, jnp.exp(s1 - v1[0:1]) / z, rank2, jnp.exp(s2 - v2[0:1])


def _route_kernel(h_ref, wq_ref, keys_ref, pos_ref, neg_ref,
                  c1_ref, e1_ref, r2_ref, e2_ref, qt_ref, sc_ref):
    half = PEER_QUERY_DIM // 2
    qt_ref[...] = _dot_nt(wq_ref[...], h_ref[...]).astype(BF16)

    def store(h, maps):
        c1, e1, r2, e2 = maps
        c1_ref[h] = c1
        e1_ref[h] = e1
        r2_ref[h] = r2.astype(BF16)
        e2_ref[h] = e2.astype(BF16)

    def head_body(h, _):
        r0 = pl.multiple_of(h * PEER_QUERY_DIM, PEER_QUERY_DIM)
        sc_ref[0] = _dot(keys_ref[2 * h], qt_ref[pl.ds(r0, half), :])
        sc_ref[1] = _dot(keys_ref[2 * h + 1], qt_ref[pl.ds(r0 + half, half), :])
        maps, tied = _route_head_fast(sc_ref[0], sc_ref[1])
        any_tied = jnp.max(jnp.where(tied, 1.0, 0.0)) > 0.0

        @pl.when(any_tied)
        def _():
            store(h, _route_head_exact(sc_ref[0], sc_ref[1], pos_ref[...], neg_ref[...]))

        @pl.when(jnp.logical_not(any_tied))
        def _():
            store(h, maps)

        return 0

    lax.fori_loop(0, PEER_HEADS, head_body, 0)


def _peer_route(h2, wq_t, keys, tt=ROUTE_TILE):
    T, D = h2.shape
    pos, neg = _cand_tables(tt)
    stat_spec = pl.BlockSpec((PEER_HEADS, PEER_N_KEYS, tt), lambda i: (0, 0, i))
    stat = lambda dt: jax.ShapeDtypeStruct((PEER_HEADS, PEER_N_KEYS, T), dt)
    return pl.pallas_call(
        _route_kernel,
        grid=(T // tt,),
        in_specs=[pl.BlockSpec((tt, D), lambda i: (i, 0)),
                  pl.BlockSpec(wq_t.shape, lambda i: (0, 0)),
                  pl.BlockSpec(keys.shape, lambda i: (0, 0, 0)),
                  pl.BlockSpec((_CAND_ROWS, tt), lambda i: (0, 0)),
                  pl.BlockSpec((_CAND_ROWS, tt), lambda i: (0, 0))],
        out_specs=[stat_spec] * 4,
        out_shape=[stat(F32), stat(F32), stat(BF16), stat(BF16)],
        scratch_shapes=[pltpu.VMEM((PEER_HEADS * PEER_QUERY_DIM, tt), BF16),
                        pltpu.VMEM((2, PEER_N_KEYS, tt), F32)],
        compiler_params=_cparams(("parallel",)),
        name="peer_route",
    )(h2, wq_t, keys, pos, neg)


_KEY_GROUP = 16
_UNITS = 4
_DRAIN_PIECES = 2


def _build_gated(a_ref, p_ref, c1_ref, e1_ref, r2_ref, e2_ref, key0, g0, ng, lt):
    rep = PEER_N_KEYS // BF16_ROWS

    def rows16(row):
        blk = jnp.broadcast_to(row, (BF16_ROWS, LANES)).astype(BF16)
        return jnp.concatenate([blk] * rep, axis=0)

    ls = slice(lt * LANES, (lt + 1) * LANES)
    w = [jnp.zeros((PEER_N_KEYS, LANES), BF16) for _ in range(ng)]
    for h in range(PEER_HEADS):
        c1 = c1_ref[h, pl.ds(key0, _KEY_GROUP), ls]
        e1 = e1_ref[h, pl.ds(key0, _KEY_GROUP), ls]
        r2 = r2_ref[h, :, ls]
        e2 = e2_ref[h, :, ls]
        for g in range(ng):
            thr = rows16(c1[g0 + g:g0 + g + 1])
            gate = rows16(e1[g0 + g:g0 + g + 1])
            w[g] = w[g] + jnp.where(r2 < thr, e2, jnp.zeros_like(e2)) * gate
    for g in range(g0, g0 + ng):
        rs = slice(g * PEER_N_KEYS, (g + 1) * PEER_N_KEYS)
        a = a_ref[rs, ls]
        gelu = 0.5 * a * (1.0 + lax.erf(a * (2.0 ** -0.5)))
        p_ref[rs, ls] = gelu.astype(BF16) * w[g - g0]


def _experts_kernel(h_ref, dn_ref, upt_ref, c1_ref, e1_ref, r2_ref, e2_ref, x_ref, *rest, emit_norm):
    if emit_norm:
        g_ref, o_ref, hn_ref, a_ref, p_ref, acc_ref = rest
    else:
        o_ref, a_ref, p_ref, acc_ref = rest
    te, tt = a_ref.shape
    d_model = acc_ref.shape[0]
    e = pl.program_id(1)
    n_tiles = pl.num_programs(1) - 1
    cur = e % 2
    ng = _KEY_GROUP // _UNITS
    n_lane = tt // LANES
    key0 = pl.multiple_of(jnp.minimum(e, n_tiles - 1) * _KEY_GROUP, _KEY_GROUP)

    def front_mm(u):
        rows = te // _UNITS
        rs = slice(u * rows, (u + 1) * rows)
        a_ref[rs, :] = _dot_nt(dn_ref[rs, :], h_ref[...])

    def back_mm(r):
        rows = d_model // _DRAIN_PIECES
        rs = slice(r * rows, (r + 1) * rows)
        acc_ref[rs, :] += _dot(upt_ref[rs, :], p_ref[1 - cur])

    units_per_drain = _UNITS // _DRAIN_PIECES

    def run(front, back):
        if front:
            front_mm(0)
        for u in range(_UNITS):
            for lt in range(n_lane):
                if front:
                    _build_gated(a_ref, p_ref.at[cur], c1_ref, e1_ref, r2_ref, e2_ref, key0, u * ng, ng, lt)
                if front and lt == 0 and u + 1 < _UNITS:
                    front_mm(u + 1)
                if back and lt == n_lane // 2 and (u + 1) % units_per_drain == 0:
                    back_mm(u // units_per_drain)

    @pl.when(e == 0)
    def _():
        acc_ref[...] = jnp.zeros_like(acc_ref)
        run(True, False)

    @pl.when((e > 0) & (e < n_tiles))
    def _():
        run(True, True)

    @pl.when(e == n_tiles)
    def _():
        run(False, True)
        y = x_ref[...] + acc_ref[...].T
        o_ref[...] = y
        if emit_norm:
            hn_ref[...] = _rms_normed(y, g_ref[...])


def _peer_experts(h2, down, up_t, stats, x, next_gain, tt=EXPERT_TOKEN_TILE):
    T, D = h2.shape
    E = down.shape[0]
    te = _KEY_GROUP * PEER_N_KEYS
    n_tiles = E // te
    emit_norm = next_gain is not None
    stat_spec = pl.BlockSpec((PEER_HEADS, PEER_N_KEYS, tt), lambda i, e: (0, 0, i))
    row_spec = pl.BlockSpec((tt, D), lambda i, e: (i, 0))
    in_specs = [row_spec,
                pl.BlockSpec((te, D), lambda i, e: (jnp.minimum(e, n_tiles - 1), 0)),
                pl.BlockSpec((None, D, te), lambda i, e: (jnp.maximum(e - 1, 0), 0, 0)),
                stat_spec, stat_spec, stat_spec, stat_spec,
                row_spec]
    args = [h2, down, up_t, *stats, x]
    out_specs = [row_spec]
    out_shape = [jax.ShapeDtypeStruct((T, D), F32)]
    if emit_norm:
        in_specs.append(pl.BlockSpec((1, D), lambda i, e: (0, 0)))
        args.append(next_gain.reshape(1, D).astype(F32))
        out_specs.append(row_spec)
        out_shape.append(jax.ShapeDtypeStruct((T, D), BF16))
    stat_bytes = 2 * PEER_HEADS * PEER_N_KEYS * tt * (4 + 2)
    block_bytes = (tt * D * 2 + 2 * te * D * 2 + stat_bytes + tt * D * 4
                   + tt * D * 4 + (tt * D * 2 + D * 4 if emit_norm else 0))
    scratch_bytes = te * tt * 4 + 2 * te * tt * 2 + D * tt * 4
    res = pl.pallas_call(
        functools.partial(_experts_kernel, emit_norm=emit_norm),
        grid=(T // tt, n_tiles + 1),
        in_specs=in_specs, out_specs=out_specs, out_shape=out_shape,
        scratch_shapes=[pltpu.VMEM((te, tt), F32),
                        pltpu.VMEM((2, te, tt), BF16),
                        pltpu.VMEM((D, tt), F32)],
        compiler_params=_cparams(("parallel", "arbitrary"), 2 * block_bytes + scratch_bytes + VMEM_SPILL_MARGIN),
        name="peer_experts",
    )(*args)
    return (res[0], res[1]) if emit_norm else (res[0], None)


def _tables_kernel(dn_ref, up_ref, dn_o, upt_o):
    dn_o[...] = dn_ref[...].astype(BF16)
    upt_o[...] = up_ref[...].T.astype(BF16)


def _expert_tables(down, up, layer, rows=1024):
    _, E, D = down.shape
    te = _KEY_GROUP * PEER_N_KEYS
    per_tile = te // rows
    row_spec = pl.BlockSpec((rows, D), lambda i: (i, 0))
    layer_spec = pl.BlockSpec((None, rows, D), lambda i: (layer, i, 0))
    return pl.pallas_call(
        _tables_kernel,
        grid=(E // rows,),
        in_specs=[layer_spec, layer_spec],
        out_specs=[row_spec, pl.BlockSpec((None, D, rows), lambda i: (i // per_tile, 0, i % per_tile))],
        out_shape=[jax.ShapeDtypeStruct((E, D), BF16), jax.ShapeDtypeStruct((E // te, D, te), BF16)],
        compiler_params=_cparams(("parallel",)),
        name="expert_tables",
    )(down, up)


def _peer_layer(x, h2, next_gain, w_query, sub_keys, down, up, layer):
    keys = sub_keys.reshape(PEER_HEADS * 2, PEER_N_KEYS, PEER_QUERY_DIM // 2).astype(BF16)
    stats = _peer_route(h2, w_query.T.astype(BF16), keys)
    down_bf, up_t = _expert_tables(down, up, layer)
    return _peer_experts(h2, down_bf, up_t, stats, x, next_gain)


def _tile_heads(g, n):
    return jnp.tile(g.astype(F32), n)


def _even_mixer(x, h, ffn_gain, w_in, f_bias, qn_a, kn_a, qn_b, kn_b, w_out, tabs, *, batch, seq):
    w = w_in.astype(BF16)
    o_qa, o_ka, o_va, o_qb, o_kb, o_vb, o_gb, o_fb = (
        0, A_W, 2 * A_W, 3 * A_W, 3 * A_W + B_W, 3 * A_W + 2 * B_W, 3 * A_W + 3 * B_W, 3 * A_W + 4 * B_W)
    gain_a = jnp.concatenate([_tile_heads(qn_a, MOBA_HEADS), _tile_heads(kn_a, MOBA_HEADS)])
    qk_a, km = _proj(h, w[:, o_qa:o_va], seq=seq, tn=A_W, gain=gain_a, rope_tabs=tabs, kmean=True)
    nb = seq // MOBA_BLOCK
    kmean = km.reshape(batch, nb, 2 * A_W)[:, :, A_W:]
    kmean = jnp.pad(kmean, ((0, 0), (0, LANES - nb), (0, 0))).astype(BF16)
    gain_b = jnp.concatenate([_tile_heads(qn_b, FOX_HEADS), _tile_heads(kn_b, FOX_HEADS)])
    qk_b = _proj(h, w[:, o_qb:o_vb], seq=seq, tn=B_W, gain=gain_b)
    w_plain = jnp.concatenate([w[:, o_va:o_qb], w[:, o_vb:o_fb]], axis=1)
    vvg = _proj(h, w_plain, seq=seq, tn=A_W)
    va, vb, gb = vvg[:, :A_W], vvg[:, A_W:A_W + B_W], vvg[:, A_W + B_W:]
    frow = _fox_gates(h, w[:, o_fb:].T, f_bias, batch=batch, seq=seq)
    oa = _moba(qk_a[:, :A_W], qk_a[:, A_W:], va, kmean, batch=batch, seq=seq)
    logit_bound = 1.01 * ATTN_SCALE * HEAD_DIM * jnp.max(jnp.abs(qn_b)) * jnp.max(jnp.abs(kn_b)) + 0.01
    ob = _fox(qk_b[:, :B_W], qk_b[:, B_W:], vb, gb, frow, logit_bound, batch=batch, seq=seq)
    return _outproj([oa, ob], w_out.astype(BF16), x, ffn_gain)


def _odd_mixer(x, h, ffn_gain, w_in, qn, kn, sinks, w_out, tabs, *, batch, seq):
    w = w_in.astype(BF16)
    qw = SWA_Q_HEADS * HEAD_DIM
    kw = SWA_KV_HEADS * HEAD_DIM
    q = _proj(h, w[:, :qw], seq=seq, tn=512, gain=_tile_heads(qn, SWA_Q_HEADS), rope_tabs=tabs)
    k = _proj(h, w[:, qw:qw + kw], seq=seq, tn=kw, gain=_tile_heads(kn, SWA_KV_HEADS), rope_tabs=tabs)
    v = _proj(h, w[:, qw + kw:], seq=seq, tn=kw)
    o = _swa(q, k, v, sinks, batch=batch, seq=seq)
    return _outproj([o], w_out.astype(BF16), x, ffn_gain)


def kernel(x, attn_norm, ffn_norm, ev_w_in, ev_forget_bias, ev_q_norm_a, ev_k_norm_a, ev_q_norm_b,
           ev_k_norm_b, ev_w_out, od_w_in, od_q_norm, od_k_norm, od_sinks, od_w_out,
           peer_w_query, peer_sub_keys, peer_down, peer_up):
    batch, seq, d_model = x.shape
    depth = attn_norm.shape[0]
    tabs = _rope_tables(seq)
    xt = x.reshape(batch * seq, d_model)
    h = _rmsnorm(xt, attn_norm[0])
    for l in range(depth):
        i = l // 2
        if l % 2 == 0:
            xt, h2 = _even_mixer(xt, h, ffn_norm[l], ev_w_in[i], ev_forget_bias[i], ev_q_norm_a[i],
                                 ev_k_norm_a[i], ev_q_norm_b[i], ev_k_norm_b[i], ev_w_out[i], tabs,
                                 batch=batch, seq=seq)
        else:
            xt, h2 = _odd_mixer(xt, h, ffn_norm[l], od_w_in[i], od_q_norm[i], od_k_norm[i], od_sinks[i],
                                od_w_out[i], tabs, batch=batch, seq=seq)
        next_gain = attn_norm[l + 1] if l + 1 < depth else None
        xt, h = _peer_layer(xt, h2, next_gain, peer_w_query[l], peer_sub_keys[l], peer_down, peer_up, l)
    return xt.reshape(batch, seq, d_model)
```
